```python
import jax, jax.numpy as jnp
from jax import lax
import numpy as np

D_MODEL = 2048
BATCH = 8
SEQ = 4096
DEPTH = 1

N_META = 16
D_MIX = D_MODEL
D_SC = D_MIX // 2
D_CF = D_MIX - D_SC
SC_WIDTH = 3
CF_WIDTH = 31
D_IN = 3 * D_SC + 2 * D_CF
D_FF = 5632
FFN_RES_SCALE = 0.5
EPS = 1e-6

kernel_name = "hymba_parallel_shortconv_conformer_macaron"


def rmsnorm(x, g):
    xf = x.astype(jnp.float32)
    y = xf * lax.rsqrt(jnp.mean(xf * xf, axis=-1, keepdims=True) + EPS)
    return (y * g.astype(jnp.float32)).astype(x.dtype)


def layernorm(x, g, b):
    xf = x.astype(jnp.float32)
    mu = jnp.mean(xf, axis=-1, keepdims=True)
    var = jnp.mean(jnp.square(xf - mu), axis=-1, keepdims=True)
    y = (xf - mu) * lax.rsqrt(var + EPS)
    return (y * g.astype(jnp.float32) + b.astype(jnp.float32)).astype(x.dtype)


def causal_dwconv(x, w):
    k, c = w.shape
    return lax.conv_general_dilated(
        x, w.astype(x.dtype)[:, None, :], window_strides=(1,), padding=[(k - 1, 0)],
        dimension_numbers=("NWC", "WIO", "NWC"), feature_group_count=c)


def swiglu(h, w_gate, w_up, w_down):
    return (jax.nn.silu(h @ w_gate) * (h @ w_up)) @ w_down


def _fwd_setup_inputs(seed: int = 0) -> dict:
    key = jax.random.key(seed)
    ks = jax.random.split(key, 24)
    f32 = jnp.float32
    nrm = lambda k, shape, scale: (jax.random.normal(k, shape, f32) * scale).astype(f32)
    gain = lambda k, shape: 1.0 + 0.05 * jax.random.normal(k, shape, f32)
    L = DEPTH
    return {
        "x": jax.random.normal(ks[0], (BATCH, SEQ, D_MODEL), f32),
        "meta_tokens": nrm(ks[1], (N_META, D_MODEL), 1.0),
        "ffn1_norm": gain(ks[2], (L, D_MODEL)),
        "ffn1_w_gate": nrm(ks[3], (L, D_MODEL, D_FF), D_MODEL ** -0.5),
        "ffn1_w_up": nrm(ks[4], (L, D_MODEL, D_FF), D_MODEL ** -0.5),
        "ffn1_w_down": nrm(ks[5], (L, D_FF, D_MODEL), D_FF ** -0.5),
        "mix_norm": gain(ks[6], (L, D_MODEL)),
        "w_in": nrm(ks[7], (L, D_MODEL, D_IN), D_MODEL ** -0.5),
        "b_in": nrm(ks[8], (L, D_IN), 0.02),
        "conv_sc_w": nrm(ks[9], (L, SC_WIDTH, D_SC), SC_WIDTH ** -0.5),
        "conv_cf_w": nrm(ks[10], (L, CF_WIDTH, D_CF), CF_WIDTH ** -0.5),
        "conv_cf_b": nrm(ks[11], (L, D_CF), 0.02),
        "ln_cf_g": gain(ks[12], (L, D_CF)),
        "ln_cf_b": nrm(ks[13], (L, D_CF), 0.02),
        "w_out": nrm(ks[14], (L, D_MIX, D_MODEL), D_MIX ** -0.5),
        "ffn2_norm": gain(ks[15], (L, D_MODEL)),
        "ffn2_w_gate": nrm(ks[16], (L, D_MODEL, D_FF), D_MODEL ** -0.5),
        "ffn2_w_up": nrm(ks[17], (L, D_MODEL, D_FF), D_MODEL ** -0.5),
        "ffn2_w_down": nrm(ks[18], (L, D_FF, D_MODEL), D_FF ** -0.5),
        "final_norm": gain(ks[19], (D_MODEL,)),
    }


def token_mixing(h, w_in, b_in, conv_sc_w, conv_cf_w, conv_cf_b, ln_cf_g, ln_cf_b, w_out):
    u = h @ w_in + b_in.astype(h.dtype)
    b_sc = u[..., :D_SC]
    c_sc = u[..., D_SC:2 * D_SC]
    v_sc = u[..., 2 * D_SC:3 * D_SC]
    a_cf = u[..., 3 * D_SC:3 * D_SC + D_CF]
    g_cf = u[..., 3 * D_SC + D_CF:]
    y_sc = b_sc * causal_dwconv(c_sc * v_sc, conv_sc_w)
    z = a_cf * jax.nn.sigmoid(g_cf)
    z = causal_dwconv(z, conv_cf_w) + conv_cf_b.astype(z.dtype)
    y_cf = jax.nn.silu(layernorm(z, ln_cf_g, ln_cf_b))
    return jnp.concatenate([y_sc, y_cf], axis=-1) @ w_out


def _fwd_reference(x, meta_tokens, ffn1_norm, ffn1_w_gate, ffn1_w_up, ffn1_w_down, mix_norm, w_in, b_in,
              conv_sc_w, conv_cf_w, conv_cf_b, ln_cf_g, ln_cf_b, w_out,
              ffn2_norm, ffn2_w_gate, ffn2_w_up, ffn2_w_down, final_norm):
    bsz = x.shape[0]
    meta = jnp.broadcast_to(meta_tokens.astype(x.dtype)[None], (bsz, N_META, x.shape[-1]))
    hs = jnp.concatenate([meta, x], axis=1)
    for l in range(DEPTH):
        hs = hs + FFN_RES_SCALE * swiglu(rmsnorm(hs, ffn1_norm[l]), ffn1_w_gate[l], ffn1_w_up[l], ffn1_w_down[l])
        hs = hs + token_mixing(rmsnorm(hs, mix_norm[l]), w_in[l], b_in[l], conv_sc_w[l], conv_cf_w[l],
                               conv_cf_b[l], ln_cf_g[l], ln_cf_b[l], w_out[l])
        hs = hs + FFN_RES_SCALE * swiglu(rmsnorm(hs, ffn2_norm[l]), ffn2_w_gate[l], ffn2_w_up[l], ffn2_w_down[l])
    out = rmsnorm(hs, final_norm)
    return out[:, N_META:]


import jax as _jax
import jax.numpy as _jnp

TWIN_FORMAT = 'train_step'
FWD_PARAMS = ['x', 'meta_tokens', 'ffn1_norm', 'ffn1_w_gate', 'ffn1_w_up', 'ffn1_w_down', 'mix_norm', 'w_in', 'b_in', 'conv_sc_w', 'conv_cf_w', 'conv_cf_b', 'ln_cf_g', 'ln_cf_b', 'w_out', 'ffn2_norm', 'ffn2_w_gate', 'ffn2_w_up', 'ffn2_w_down', 'final_norm']
TWIN_WEIGHTS = ['meta_tokens', 'ffn1_norm', 'ffn1_w_gate', 'ffn1_w_up', 'ffn1_w_down', 'mix_norm', 'w_in', 'b_in', 'conv_sc_w', 'conv_cf_w', 'conv_cf_b', 'ln_cf_g', 'ln_cf_b', 'w_out', 'ffn2_norm', 'ffn2_w_gate', 'ffn2_w_up', 'ffn2_w_down', 'final_norm']
TWIN_DIFF_INPUT = 'x'
TWIN_INPUTS = ['x', 'meta_tokens', 'ffn1_norm', 'ffn1_w_gate', 'ffn1_w_up', 'ffn1_w_down', 'mix_norm', 'w_in', 'b_in', 'conv_sc_w', 'conv_cf_w', 'conv_cf_b', 'ln_cf_g', 'ln_cf_b', 'w_out', 'ffn2_norm', 'ffn2_w_gate', 'ffn2_w_up', 'ffn2_w_down', 'final_norm', 'loss_target', 'm_meta_tokens', 'm_ffn1_norm', 'm_ffn1_w_gate', 'm_ffn1_w_up', 'm_ffn1_w_down', 'm_mix_norm', 'm_w_in', 'm_b_in', 'm_conv_sc_w', 'm_conv_cf_w', 'm_conv_cf_b', 'm_ln_cf_g', 'm_ln_cf_b', 'm_w_out', 'm_ffn2_norm', 'm_ffn2_w_gate', 'm_ffn2_w_up', 'm_ffn2_w_down', 'm_final_norm', 'v_meta_tokens', 'v_ffn1_norm', 'v_ffn1_w_gate', 'v_ffn1_w_up', 'v_ffn1_w_down', 'v_mix_norm', 'v_w_in', 'v_b_in', 'v_conv_sc_w', 'v_conv_cf_w', 'v_conv_cf_b', 'v_ln_cf_g', 'v_ln_cf_b', 'v_w_out', 'v_ffn2_norm', 'v_ffn2_w_gate', 'v_ffn2_w_up', 'v_ffn2_w_down', 'v_final_norm']
TWIN_OUTPUTS = ['loss', 'grad_x', 'grad_meta_tokens', 'grad_ffn1_norm', 'grad_ffn1_w_gate', 'grad_ffn1_w_up', 'grad_ffn1_w_down', 'grad_mix_norm', 'grad_w_in', 'grad_b_in', 'grad_conv_sc_w', 'grad_conv_cf_w', 'grad_conv_cf_b', 'grad_ln_cf_g', 'grad_ln_cf_b', 'grad_w_out', 'grad_ffn2_norm', 'grad_ffn2_w_gate', 'grad_ffn2_w_up', 'grad_ffn2_w_down', 'grad_final_norm', 'delta_meta_tokens', 'delta_ffn1_norm', 'delta_ffn1_w_gate', 'delta_ffn1_w_up', 'delta_ffn1_w_down', 'delta_mix_norm', 'delta_w_in', 'delta_b_in', 'delta_conv_sc_w', 'delta_conv_cf_w', 'delta_conv_cf_b', 'delta_ln_cf_g', 'delta_ln_cf_b', 'delta_w_out', 'delta_ffn2_norm', 'delta_ffn2_w_gate', 'delta_ffn2_w_up', 'delta_ffn2_w_down', 'delta_final_norm', 'new_m_meta_tokens', 'new_m_ffn1_norm', 'new_m_ffn1_w_gate', 'new_m_ffn1_w_up', 'new_m_ffn1_w_down', 'new_m_mix_norm', 'new_m_w_in', 'new_m_b_in', 'new_m_conv_sc_w', 'new_m_conv_cf_w', 'new_m_conv_cf_b', 'new_m_ln_cf_g', 'new_m_ln_cf_b', 'new_m_w_out', 'new_m_ffn2_norm', 'new_m_ffn2_w_gate', 'new_m_ffn2_w_up', 'new_m_ffn2_w_down', 'new_m_final_norm', 'new_v_meta_tokens', 'new_v_ffn1_norm', 'new_v_ffn1_w_gate', 'new_v_ffn1_w_up', 'new_v_ffn1_w_down', 'new_v_mix_norm', 'new_v_w_in', 'new_v_b_in', 'new_v_conv_sc_w', 'new_v_conv_cf_w', 'new_v_conv_cf_b', 'new_v_ln_cf_g', 'new_v_ln_cf_b', 'new_v_w_out', 'new_v_ffn2_norm', 'new_v_ffn2_w_gate', 'new_v_ffn2_w_up', 'new_v_ffn2_w_down', 'new_v_final_norm']
TWIN_LEAF_KINDS = {'loss': 'loss', 'grad_x': 'grad_x', 'grad_meta_tokens': 'grad_w', 'grad_ffn1_norm': 'grad_w', 'grad_ffn1_w_gate': 'grad_w', 'grad_ffn1_w_up': 'grad_w', 'grad_ffn1_w_down': 'grad_w', 'grad_mix_norm': 'grad_w', 'grad_w_in': 'grad_w', 'grad_b_in': 'grad_w', 'grad_conv_sc_w': 'grad_w', 'grad_conv_cf_w': 'grad_w', 'grad_conv_cf_b': 'grad_w', 'grad_ln_cf_g': 'grad_w', 'grad_ln_cf_b': 'grad_w', 'grad_w_out': 'grad_w', 'grad_ffn2_norm': 'grad_w', 'grad_ffn2_w_gate': 'grad_w', 'grad_ffn2_w_up': 'grad_w', 'grad_ffn2_w_down': 'grad_w', 'grad_final_norm': 'grad_w', 'delta_meta_tokens': 'delta_w', 'delta_ffn1_norm': 'delta_w', 'delta_ffn1_w_gate': 'delta_w', 'delta_ffn1_w_up': 'delta_w', 'delta_ffn1_w_down': 'delta_w', 'delta_mix_norm': 'delta_w', 'delta_w_in': 'delta_w', 'delta_b_in': 'delta_w', 'delta_conv_sc_w': 'delta_w', 'delta_conv_cf_w': 'delta_w', 'delta_conv_cf_b': 'delta_w', 'delta_ln_cf_g': 'delta_w', 'delta_ln_cf_b': 'delta_w', 'delta_w_out': 'delta_w', 'delta_ffn2_norm': 'delta_w', 'delta_ffn2_w_gate': 'delta_w', 'delta_ffn2_w_up': 'delta_w', 'delta_ffn2_w_down': 'delta_w', 'delta_final_norm': 'delta_w', 'new_m_meta_tokens': 'new_m', 'new_m_ffn1_norm': 'new_m', 'new_m_ffn1_w_gate': 'new_m', 'new_m_ffn1_w_up': 'new_m', 'new_m_ffn1_w_down': 'new_m', 'new_m_mix_norm': 'new_m', 'new_m_w_in': 'new_m', 'new_m_b_in': 'new_m', 'new_m_conv_sc_w': 'new_m', 'new_m_conv_cf_w': 'new_m', 'new_m_conv_cf_b': 'new_m', 'new_m_ln_cf_g': 'new_m', 'new_m_ln_cf_b': 'new_m', 'new_m_w_out': 'new_m', 'new_m_ffn2_norm': 'new_m', 'new_m_ffn2_w_gate': 'new_m', 'new_m_ffn2_w_up': 'new_m', 'new_m_ffn2_w_down': 'new_m', 'new_m_final_norm': 'new_m', 'new_v_meta_tokens': 'new_v', 'new_v_ffn1_norm': 'new_v', 'new_v_ffn1_w_gate': 'new_v', 'new_v_ffn1_w_up': 'new_v', 'new_v_ffn1_w_down': 'new_v', 'new_v_mix_norm': 'new_v', 'new_v_w_in': 'new_v', 'new_v_b_in': 'new_v', 'new_v_conv_sc_w': 'new_v', 'new_v_conv_cf_w': 'new_v', 'new_v_conv_cf_b': 'new_v', 'new_v_ln_cf_g': 'new_v', 'new_v_ln_cf_b': 'new_v', 'new_v_w_out': 'new_v', 'new_v_ffn2_norm': 'new_v', 'new_v_ffn2_w_gate': 'new_v', 'new_v_ffn2_w_up': 'new_v', 'new_v_ffn2_w_down': 'new_v', 'new_v_final_norm': 'new_v'}


def _forward(args):
    return _fwd_reference(*[args[k] for k in FWD_PARAMS])


def _output_shape():
    def fwd():
        inp = _fwd_setup_inputs(0)
        return _fwd_reference(*[inp[k] for k in FWD_PARAMS])
    out = _jax.eval_shape(fwd)
    return out.shape, out.dtype

N_MICROBATCH = 1
ADAM_LR = 0.001
ADAM_B1 = 0.9
ADAM_B2 = 0.999
ADAM_EPS = 1e-08
ADAM_WD = 0.01
ADAM_STEP = 10
PER_EXAMPLE_BATCH_AXIS = {'x': 0, 'loss_target': 0}
SHARED_INPUTS = []
_WEIGHT_DTYPES = {'meta_tokens': _jnp.float32, 'ffn1_norm': _jnp.float32, 'ffn1_w_gate': _jnp.float32, 'ffn1_w_up': _jnp.float32, 'ffn1_w_down': _jnp.float32, 'mix_norm': _jnp.float32, 'w_in': _jnp.float32, 'b_in': _jnp.float32, 'conv_sc_w': _jnp.float32, 'conv_cf_w': _jnp.float32, 'conv_cf_b': _jnp.float32, 'ln_cf_g': _jnp.float32, 'ln_cf_b': _jnp.float32, 'w_out': _jnp.float32, 'ffn2_norm': _jnp.float32, 'ffn2_w_gate': _jnp.float32, 'ffn2_w_up': _jnp.float32, 'ffn2_w_down': _jnp.float32, 'final_norm': _jnp.float32}
MOMENT_SCALE = {'meta_tokens': 2.246361e-03, 'ffn1_norm': 4.693509e-02, 'ffn1_w_gate': 2.062026e-02, 'ffn1_w_up': 1.996931e-02, 'ffn1_w_down': 3.316565e-02, 'mix_norm': 9.373761e-02, 'w_in': 5.731907e-02, 'b_in': 6.335200e-02, 'conv_sc_w': 7.109116e-02, 'conv_cf_w': 4.350433e-02, 'conv_cf_b': 1.141295e-01, 'ln_cf_g': 6.024905e-02, 'ln_cf_b': 6.183300e-02, 'w_out': 5.945832e-02, 'ffn2_norm': 2.827305e-02, 'ffn2_w_gate': 1.212946e-02, 'ffn2_w_up': 1.179429e-02, 'ffn2_w_down': 1.958489e-02, 'final_norm': 1.602481e+01}


def _to_microbatches(a, axis):
    t = _jnp.moveaxis(a, axis, 0)
    t = t.reshape((N_MICROBATCH, t.shape[0] // N_MICROBATCH) + t.shape[1:])
    return _jnp.moveaxis(t, 1, axis + 1)


def setup_inputs(seed: int = 0) -> dict:
    inp = _fwd_setup_inputs(seed)
    key = _jax.random.fold_in(_jax.random.key(seed), 7919)
    shape, _ = _output_shape()
    out = dict(inp)
    out["loss_target"] = _jax.random.normal(_jax.random.fold_in(key, 0), shape, _jnp.float32)
    for i, name in enumerate(TWIN_WEIGHTS):
        w = inp[name].astype(_jnp.float32)
        if MOMENT_SCALE is None:
            s = _jnp.sqrt(_jnp.mean(_jnp.square(w)) + 1e-30)
        else:
            s = MOMENT_SCALE[name]
        km, kv = _jax.random.split(_jax.random.fold_in(key, i + 1))
        out[name] = w
        out["m_" + name] = s * _jax.random.normal(km, w.shape, _jnp.float32)
        out["v_" + name] = (s * s) * _jax.random.uniform(kv, w.shape, _jnp.float32, 0.5, 1.5)
    if N_MICROBATCH > 1:
        for name, axis in PER_EXAMPLE_BATCH_AXIS.items():
            out[name] = _to_microbatches(out[name], axis)
    return {'x': out['x'], 'meta_tokens': out['meta_tokens'], 'ffn1_norm': out['ffn1_norm'], 'ffn1_w_gate': out['ffn1_w_gate'], 'ffn1_w_up': out['ffn1_w_up'], 'ffn1_w_down': out['ffn1_w_down'], 'mix_norm': out['mix_norm'], 'w_in': out['w_in'], 'b_in': out['b_in'], 'conv_sc_w': out['conv_sc_w'], 'conv_cf_w': out['conv_cf_w'], 'conv_cf_b': out['conv_cf_b'], 'ln_cf_g': out['ln_cf_g'], 'ln_cf_b': out['ln_cf_b'], 'w_out': out['w_out'], 'ffn2_norm': out['ffn2_norm'], 'ffn2_w_gate': out['ffn2_w_gate'], 'ffn2_w_up': out['ffn2_w_up'], 'ffn2_w_down': out['ffn2_w_down'], 'final_norm': out['final_norm'], 'loss_target': out['loss_target'], 'm_meta_tokens': out['m_meta_tokens'], 'm_ffn1_norm': out['m_ffn1_norm'], 'm_ffn1_w_gate': out['m_ffn1_w_gate'], 'm_ffn1_w_up': out['m_ffn1_w_up'], 'm_ffn1_w_down': out['m_ffn1_w_down'], 'm_mix_norm': out['m_mix_norm'], 'm_w_in': out['m_w_in'], 'm_b_in': out['m_b_in'], 'm_conv_sc_w': out['m_conv_sc_w'], 'm_conv_cf_w': out['m_conv_cf_w'], 'm_conv_cf_b': out['m_conv_cf_b'], 'm_ln_cf_g': out['m_ln_cf_g'], 'm_ln_cf_b': out['m_ln_cf_b'], 'm_w_out': out['m_w_out'], 'm_ffn2_norm': out['m_ffn2_norm'], 'm_ffn2_w_gate': out['m_ffn2_w_gate'], 'm_ffn2_w_up': out['m_ffn2_w_up'], 'm_ffn2_w_down': out['m_ffn2_w_down'], 'm_final_norm': out['m_final_norm'], 'v_meta_tokens': out['v_meta_tokens'], 'v_ffn1_norm': out['v_ffn1_norm'], 'v_ffn1_w_gate': out['v_ffn1_w_gate'], 'v_ffn1_w_up': out['v_ffn1_w_up'], 'v_ffn1_w_down': out['v_ffn1_w_down'], 'v_mix_norm': out['v_mix_norm'], 'v_w_in': out['v_w_in'], 'v_b_in': out['v_b_in'], 'v_conv_sc_w': out['v_conv_sc_w'], 'v_conv_cf_w': out['v_conv_cf_w'], 'v_conv_cf_b': out['v_conv_cf_b'], 'v_ln_cf_g': out['v_ln_cf_g'], 'v_ln_cf_b': out['v_ln_cf_b'], 'v_w_out': out['v_w_out'], 'v_ffn2_norm': out['v_ffn2_norm'], 'v_ffn2_w_gate': out['v_ffn2_w_gate'], 'v_ffn2_w_up': out['v_ffn2_w_up'], 'v_ffn2_w_down': out['v_ffn2_w_down'], 'v_final_norm': out['v_final_norm']}


def _loss(weights, diff, rest, loss_target):
    with _jax.named_scope("forward"):
        args = {**rest, TWIN_DIFF_INPUT: diff, **{k: w.astype(_WEIGHT_DTYPES[k]) for k, w in weights.items()}}
        y = _forward(args)
    with _jax.named_scope("loss_head"):
        err = _jnp.square(y.astype(_jnp.float32) - loss_target)
        return 0.5 * _jnp.sum(_jnp.mean(err, axis=-1)) if err.ndim else 0.5 * err


def _adamw(w, g, m, v):
    m = ADAM_B1 * m + (1.0 - ADAM_B1) * g
    v = ADAM_B2 * v + (1.0 - ADAM_B2) * _jnp.square(g)
    m_hat = m / (1.0 - ADAM_B1 ** ADAM_STEP)
    v_hat = v / (1.0 - ADAM_B2 ** ADAM_STEP)
    delta = -ADAM_LR * (m_hat / (_jnp.sqrt(v_hat) + ADAM_EPS) + ADAM_WD * w)
    return delta, m, v


def reference(x, meta_tokens, ffn1_norm, ffn1_w_gate, ffn1_w_up, ffn1_w_down, mix_norm, w_in, b_in, conv_sc_w, conv_cf_w, conv_cf_b, ln_cf_g, ln_cf_b, w_out, ffn2_norm, ffn2_w_gate, ffn2_w_up, ffn2_w_down, final_norm, loss_target, m_meta_tokens, m_ffn1_norm, m_ffn1_w_gate, m_ffn1_w_up, m_ffn1_w_down, m_mix_norm, m_w_in, m_b_in, m_conv_sc_w, m_conv_cf_w, m_conv_cf_b, m_ln_cf_g, m_ln_cf_b, m_w_out, m_ffn2_norm, m_ffn2_w_gate, m_ffn2_w_up, m_ffn2_w_down, m_final_norm, v_meta_tokens, v_ffn1_norm, v_ffn1_w_gate, v_ffn1_w_up, v_ffn1_w_down, v_mix_norm, v_w_in, v_b_in, v_conv_sc_w, v_conv_cf_w, v_conv_cf_b, v_ln_cf_g, v_ln_cf_b, v_w_out, v_ffn2_norm, v_ffn2_w_gate, v_ffn2_w_up, v_ffn2_w_down, v_final_norm):
    given = dict(x=x, meta_tokens=meta_tokens, ffn1_norm=ffn1_norm, ffn1_w_gate=ffn1_w_gate, ffn1_w_up=ffn1_w_up, ffn1_w_down=ffn1_w_down, mix_norm=mix_norm, w_in=w_in, b_in=b_in, conv_sc_w=conv_sc_w, conv_cf_w=conv_cf_w, conv_cf_b=conv_cf_b, ln_cf_g=ln_cf_g, ln_cf_b=ln_cf_b, w_out=w_out, ffn2_norm=ffn2_norm, ffn2_w_gate=ffn2_w_gate, ffn2_w_up=ffn2_w_up, ffn2_w_down=ffn2_w_down, final_norm=final_norm, loss_target=loss_target, m_meta_tokens=m_meta_tokens, m_ffn1_norm=m_ffn1_norm, m_ffn1_w_gate=m_ffn1_w_gate, m_ffn1_w_up=m_ffn1_w_up, m_ffn1_w_down=m_ffn1_w_down, m_mix_norm=m_mix_norm, m_w_in=m_w_in, m_b_in=m_b_in, m_conv_sc_w=m_conv_sc_w, m_conv_cf_w=m_conv_cf_w, m_conv_cf_b=m_conv_cf_b, m_ln_cf_g=m_ln_cf_g, m_ln_cf_b=m_ln_cf_b, m_w_out=m_w_out, m_ffn2_norm=m_ffn2_norm, m_ffn2_w_gate=m_ffn2_w_gate, m_ffn2_w_up=m_ffn2_w_up, m_ffn2_w_down=m_ffn2_w_down, m_final_norm=m_final_norm, v_meta_tokens=v_meta_tokens, v_ffn1_norm=v_ffn1_norm, v_ffn1_w_gate=v_ffn1_w_gate, v_ffn1_w_up=v_ffn1_w_up, v_ffn1_w_down=v_ffn1_w_down, v_mix_norm=v_mix_norm, v_w_in=v_w_in, v_b_in=v_b_in, v_conv_sc_w=v_conv_sc_w, v_conv_cf_w=v_conv_cf_w, v_conv_cf_b=v_conv_cf_b, v_ln_cf_g=v_ln_cf_g, v_ln_cf_b=v_ln_cf_b, v_w_out=v_w_out, v_ffn2_norm=v_ffn2_norm, v_ffn2_w_gate=v_ffn2_w_gate, v_ffn2_w_up=v_ffn2_w_up, v_ffn2_w_down=v_ffn2_w_down, v_final_norm=v_final_norm)
    weights = {n: given[n] for n in TWIN_WEIGHTS}
    shared = {n: given[n] for n in SHARED_INPUTS}
    per_example = {n: given[n] for n in ['x']}
    grad_fn = _jax.value_and_grad(_loss, argnums=(0, 1))

    def one_microbatch(ex, loss_target):
        ex = dict(ex)
        diff = ex.pop(TWIN_DIFF_INPUT)
        return grad_fn(weights, diff, {**shared, **ex}, loss_target)

    if N_MICROBATCH == 1:
        loss, (grad_w, grad_x) = one_microbatch(per_example, given["loss_target"])
    else:
        def body(carry, xs):
            loss_sum, grad_sum = carry
            l_k, (gw_k, gx_k) = one_microbatch(xs[0], xs[1])
            with _jax.named_scope("update"):
                return (loss_sum + l_k, _jax.tree.map(_jnp.add, grad_sum, gw_k)), gx_k

        init = (_jnp.zeros((), _jnp.float32), _jax.tree.map(_jnp.zeros_like, weights))
        (loss, grad_w), grad_x = _jax.lax.scan(body, init, (per_example, given["loss_target"]))
    with _jax.named_scope("update"):
        delta_w, new_m, new_v = {}, {}, {}
        for n in TWIN_WEIGHTS:
            delta_w[n], new_m[n], new_v[n] = _adamw(weights[n], grad_w[n], given["m_" + n], given["v_" + n])
    return (loss, grad_x, *[grad_w[n] for n in TWIN_WEIGHTS], *[delta_w[n] for n in TWIN_WEIGHTS],
            *[new_m[n] for n in TWIN_WEIGHTS], *[new_v[n] for n in TWIN_WEIGHTS])
```

```python
import functools

import jax
import jax.numpy as jnp
from jax import lax
from jax.experimental import pallas as pl
from jax.experimental.pallas import tpu as pltpu

F32 = jnp.float32
BF16 = jnp.bfloat16
MESH = pl.DeviceIdType.MESH

N_META = 16
TT = 128
PAD = TT - N_META
HALO = 32
EPS = 1e-6
FFN_RES_SCALE = 0.5
N_CHIPS = 4
N_DEV = 8

ADAM_LR = 0.001
ADAM_B1 = 0.9
ADAM_B2 = 0.999
ADAM_EPS = 1e-08
ADAM_WD = 0.01
ADAM_STEP = 10

V7X_VMEM_BYTES = 64 * 2 ** 20
NT_DIMS = (((1,), (1,)), ((), ()))
TN_DIMS = (((0,), (0,)), ((), ()))


def _params(semantics, block_bytes):
    limit = min(2 * block_bytes + 16 * 2 ** 20, V7X_VMEM_BYTES - 6 * 2 ** 20)
    return pltpu.CompilerParams(dimension_semantics=semantics, vmem_limit_bytes=int(limit))


def _nbytes(shape, dtype):
    n = 1
    for d in shape:
        if d is not None:
            n *= d
    return n * jnp.dtype(dtype).itemsize


def _row_tile(rows, target, mult=8):
    best = None
    for t in range(mult, min(rows, target) + 1, mult):
        if rows % t == 0:
            best = t
    assert best is not None, (rows, target, mult)
    return best


def _sigmoid(v):
    return jax.nn.sigmoid(v)


def _dsilu(v, s):
    return s * (1.0 + v * (1.0 - s))


def _embed_rms(x2, meta, gain):
    S, D = x2.shape
    T = S + TT

    def body(x_ref, meta_ref, g_ref, hs_ref, n_ref):
        i = pl.program_id(0)

        @pl.when(i == 0)
        def _():
            hs_ref[...] = jnp.zeros_like(hs_ref)
            hs_ref[PAD:, :] = meta_ref[...]

        @pl.when(i > 0)
        def _():
            hs_ref[...] = x_ref[...]

        h = hs_ref[...]
        r = lax.rsqrt(jnp.mean(h * h, axis=-1, keepdims=True) + EPS)
        n_ref[...] = ((h * r) * g_ref[...]).astype(BF16)

    blk = _nbytes((TT, D), F32) * 2 + _nbytes((TT, D), BF16)
    return pl.pallas_call(
        body, name="embed_rms", grid=(T // TT,),
        in_specs=[pl.BlockSpec((TT, D), lambda i: (jnp.maximum(i - 1, 0), 0)),
                  pl.BlockSpec((N_META, D), lambda i: (0, 0)),
                  pl.BlockSpec((1, D), lambda i: (0, 0))],
        out_specs=[pl.BlockSpec((TT, D), lambda i: (i, 0)), pl.BlockSpec((TT, D), lambda i: (i, 0))],
        out_shape=[jax.ShapeDtypeStruct((T, D), F32), jax.ShapeDtypeStruct((T, D), BF16)],
        compiler_params=_params(("parallel",), blk),
    )(x2, meta, gain)


def _rms(hs, gain, name):
    T, D = hs.shape
    te = _row_tile(T, 384)

    def body(h_ref, g_ref, n_ref):
        h = h_ref[...]
        r = lax.rsqrt(jnp.mean(h * h, axis=-1, keepdims=True) + EPS)
        n_ref[...] = ((h * r) * g_ref[...]).astype(BF16)

    blk = _nbytes((te, D), F32) + _nbytes((te, D), BF16)
    return pl.pallas_call(
        body, name=name, grid=(T // te,),
        in_specs=[pl.BlockSpec((te, D), lambda i: (i, 0)), pl.BlockSpec((1, D), lambda i: (0, 0))],
        out_specs=pl.BlockSpec((te, D), lambda i: (i, 0)),
        out_shape=jax.ShapeDtypeStruct((T, D), BF16),
        compiler_params=_params(("parallel",), blk),
    )(hs, gain)


def _rms_bwd_math(dn, h, g):
    r = lax.rsqrt(jnp.mean(h * h, axis=-1, keepdims=True) + EPS)
    xh = h * r
    dgain = jnp.sum(dn * xh, axis=0, keepdims=True)
    dxh = dn * g
    dh = r * (dxh - xh * jnp.mean(dxh * xh, axis=-1, keepdims=True))
    return dh, dgain


def _rms_bwd(dn, hs, gain, dres, scale, name):
    T, D = hs.shape
    te = _row_tile(T, 384)

    def body(dn_ref, h_ref, g_ref, dres_ref, dhs_ref, dhb_ref, dg_ref):
        dh, dgain = _rms_bwd_math(dn_ref[...], h_ref[...], g_ref[...])
        d = dres_ref[...] + dh
        dhs_ref[...] = d
        dhb_ref[...] = (scale * d).astype(BF16)

        @pl.when(pl.program_id(0) == 0)
        def _():
            dg_ref[...] = jnp.zeros_like(dg_ref)

        dg_ref[...] += dgain

    blk = _nbytes((te, D), F32) * 4 + _nbytes((te, D), BF16)
    row = lambda i: (i, 0)
    return pl.pallas_call(
        body, name=name, grid=(T // te,),
        in_specs=[pl.BlockSpec((te, D), row), pl.BlockSpec((te, D), row), pl.BlockSpec((1, D), lambda i: (0, 0)),
                  pl.BlockSpec((te, D), row)],
        out_specs=[pl.BlockSpec((te, D), row), pl.BlockSpec((te, D), row), pl.BlockSpec((1, D), lambda i: (0, 0))],
        out_shape=[jax.ShapeDtypeStruct((T, D), F32), jax.ShapeDtypeStruct((T, D), BF16),
                   jax.ShapeDtypeStruct((1, D), F32)],
        compiler_params=_params(("arbitrary",), blk),
    )(dn, hs, gain, dres)


def _rms_bwd_first(dn, hs, gain, dres):
    T, D = hs.shape
    S = T - TT

    def body(dn_ref, h_ref, g_ref, dres_ref, gx_ref, gm_ref, dg_ref):
        i = pl.program_id(0)
        dh, dgain = _rms_bwd_math(dn_ref[...], h_ref[...], g_ref[...])
        d = dres_ref[...] + dh

        @pl.when(i == 0)
        def _():
            dg_ref[...] = jnp.zeros_like(dg_ref)
            gm_ref[...] = d[PAD:, :]

        @pl.when(i > 0)
        def _():
            gx_ref[...] = d

        dg_ref[...] += dgain

    blk = _nbytes((TT, D), F32) * 4
    row = lambda i: (i, 0)
    return pl.pallas_call(
        body, name="rms_bwd_ffn1", grid=(T // TT,),
        in_specs=[pl.BlockSpec((TT, D), row), pl.BlockSpec((TT, D), row), pl.BlockSpec((1, D), lambda i: (0, 0)),
                  pl.BlockSpec((TT, D), row)],
        out_specs=[pl.BlockSpec((TT, D), lambda i: (jnp.maximum(i - 1, 0), 0)),
                   pl.BlockSpec((N_META, D), lambda i: (0, 0)), pl.BlockSpec((1, D), lambda i: (0, 0))],
        out_shape=[jax.ShapeDtypeStruct((S, D), F32), jax.ShapeDtypeStruct((N_META, D), F32),
                   jax.ShapeDtypeStruct((1, D), F32)],
        compiler_params=_params(("arbitrary",), blk),
    )(dn, hs, gain, dres)


def _final_loss(hs, gain, tgt):
    T, D = hs.shape

    def body(h_ref, g_ref, t_ref, dhs_ref, dhb_ref, loss_ref, dg_ref):
        i = pl.program_id(0)
        h = h_ref[...]
        g = g_ref[...]
        r = lax.rsqrt(jnp.mean(h * h, axis=-1, keepdims=True) + EPS)
        xh = h * r
        e = jnp.where(i > 0, xh * g - t_ref[...], 0.0)
        tile_loss = jnp.sum(jnp.sum(e * e, axis=1, keepdims=True), axis=0, keepdims=True) * (0.5 / D)
        dout = e * (1.0 / D)
        dgain = jnp.sum(dout * xh, axis=0, keepdims=True)
        dxh = dout * g
        d = r * (dxh - xh * jnp.mean(dxh * xh, axis=-1, keepdims=True))
        dhs_ref[...] = d
        dhb_ref[...] = (FFN_RES_SCALE * d).astype(BF16)

        @pl.when(i == 0)
        def _():
            loss_ref[...] = jnp.zeros_like(loss_ref)
            dg_ref[...] = jnp.zeros_like(dg_ref)

        loss_ref[...] += jnp.broadcast_to(tile_loss, loss_ref.shape)
        dg_ref[...] += dgain

    blk = _nbytes((TT, D), F32) * 3 + _nbytes((TT, D), BF16)
    row = lambda i: (i, 0)
    return pl.pallas_call(
        body, name="final_loss", grid=(T // TT,),
        in_specs=[pl.BlockSpec((TT, D), row), pl.BlockSpec((1, D), lambda i: (0, 0)),
                  pl.BlockSpec((TT, D), lambda i: (jnp.maximum(i - 1, 0), 0))],
        out_specs=[pl.BlockSpec((TT, D), row), pl.BlockSpec((TT, D), row),
                   pl.BlockSpec((1, 128), lambda i: (0, 0)), pl.BlockSpec((1, D), lambda i: (0, 0))],
        out_shape=[jax.ShapeDtypeStruct((T, D), F32), jax.ShapeDtypeStruct((T, D), BF16),
                   jax.ShapeDtypeStruct((1, 128), F32), jax.ShapeDtypeStruct((1, D), F32)],
        compiler_params=_params(("arbitrary",), blk),
    )(hs, gain, tgt)


def _tm(T):
    return _row_tile(T, 384, 128)


def _ffn_up(n, wg, wu, name):
    T, D = n.shape
    Fs = wg.shape[2]
    tm = _tm(T)

    def body(n_ref, wg_ref, wu_ref, g_ref, u_ref, a_ref):
        nn = n_ref[...]
        g = jnp.dot(nn, wg_ref[...], preferred_element_type=F32)
        u = jnp.dot(nn, wu_ref[...], preferred_element_type=F32)
        g_ref[...] = g.astype(BF16)
        u_ref[...] = u.astype(BF16)
        a_ref[...] = (jax.nn.silu(g) * u).astype(BF16)

    blk = _nbytes((tm, D), BF16) + 2 * _nbytes((D, Fs), BF16) + 3 * _nbytes((tm, Fs), BF16) + 2 * _nbytes((tm, Fs), F32)
    out = pl.BlockSpec((tm, Fs), lambda j, i: (i, j))
    shp = jax.ShapeDtypeStruct((T, N_CHIPS * Fs), BF16)
    return pl.pallas_call(
        body, name=name, grid=(N_CHIPS, T // tm),
        in_specs=[pl.BlockSpec((tm, D), lambda j, i: (i, 0)),
                  pl.BlockSpec((None, D, Fs), lambda j, i: (j, 0, 0)),
                  pl.BlockSpec((None, D, Fs), lambda j, i: (j, 0, 0))],
        out_specs=[out, out, out], out_shape=[shp, shp, shp],
        compiler_params=_params(("parallel", "parallel"), blk),
    )(n, wg, wu)


def _ffn_down(a, wd, hs, name):
    T, F = a.shape
    D = wd.shape[1]
    tm = _tm(T)
    tn = D // 2

    def body(a_ref, w_ref, h_ref, o_ref):
        o_ref[...] = h_ref[...] + FFN_RES_SCALE * jnp.dot(a_ref[...], w_ref[...], preferred_element_type=F32)

    blk = _nbytes((tm, F), BF16) + _nbytes((F, tn), BF16) + 3 * _nbytes((tm, tn), F32)
    return pl.pallas_call(
        body, name=name, grid=(D // tn, T // tm),
        in_specs=[pl.BlockSpec((tm, F), lambda n, i: (i, 0)), pl.BlockSpec((F, tn), lambda n, i: (0, n)),
                  pl.BlockSpec((tm, tn), lambda n, i: (i, n))],
        out_specs=pl.BlockSpec((tm, tn), lambda n, i: (i, n)),
        out_shape=jax.ShapeDtypeStruct((T, D), F32),
        compiler_params=_params(("parallel", "parallel"), blk),
    )(a, wd, hs)


def _mix_in(n, w, b):
    T, D = n.shape
    Ns = w.shape[2]
    tm = _tm(T)

    def body(n_ref, w_ref, b_ref, u_ref):
        u_ref[...] = jnp.dot(n_ref[...], w_ref[...], preferred_element_type=F32) + b_ref[...]

    blk = _nbytes((tm, D), BF16) + _nbytes((D, Ns), BF16) + 2 * _nbytes((tm, Ns), F32)
    return pl.pallas_call(
        body, name="mix_in", grid=(N_CHIPS, T // tm),
        in_specs=[pl.BlockSpec((tm, D), lambda j, i: (i, 0)), pl.BlockSpec((None, D, Ns), lambda j, i: (j, 0, 0)),
                  pl.BlockSpec((1, Ns), lambda j, i: (0, j))],
        out_specs=pl.BlockSpec((tm, Ns), lambda j, i: (i, j)),
        out_shape=jax.ShapeDtypeStruct((T, N_CHIPS * Ns), F32),
        compiler_params=_params(("parallel", "parallel"), blk),
    )(n, w, b)


def _mix_out(y, w, hs):
    T, D = y.shape
    tm = _tm(T)

    def body(y_ref, w_ref, h_ref, o_ref):
        o_ref[...] = h_ref[...] + jnp.dot(y_ref[...], w_ref[...], preferred_element_type=F32)

    blk = _nbytes((tm, D), BF16) + _nbytes((D, D), BF16) + 3 * _nbytes((tm, D), F32)
    return pl.pallas_call(
        body, name="mix_out", grid=(T // tm,),
        in_specs=[pl.BlockSpec((tm, D), lambda i: (i, 0)), pl.BlockSpec((D, D), lambda i: (0, 0)),
                  pl.BlockSpec((tm, D), lambda i: (i, 0))],
        out_specs=pl.BlockSpec((tm, D), lambda i: (i, 0)),
        out_shape=jax.ShapeDtypeStruct((T, D), F32),
        compiler_params=_params(("parallel",), blk),
    )(y, w, hs)


def _ffn_bwd_act(dfb, wd, g, u, name):
    T, D = dfb.shape
    Fs = wd.shape[1]
    tm = _tm(T)

    def body(d_ref, w_ref, g_ref, u_ref, dg_ref, du_ref):
        da = lax.dot_general(d_ref[...], w_ref[...], NT_DIMS, preferred_element_type=F32)
        gv = g_ref[...].astype(F32)
        uv = u_ref[...].astype(F32)
        s = _sigmoid(gv)
        du_ref[...] = (da * (gv * s)).astype(BF16)
        dg_ref[...] = (da * uv * _dsilu(gv, s)).astype(BF16)

    blk = _nbytes((tm, D), BF16) + _nbytes((Fs, D), BF16) + 4 * _nbytes((tm, Fs), BF16) + 3 * _nbytes((tm, Fs), F32)
    io = pl.BlockSpec((tm, Fs), lambda j, i: (i, j))
    shp = jax.ShapeDtypeStruct((T, N_CHIPS * Fs), BF16)
    return pl.pallas_call(
        body, name=name, grid=(N_CHIPS, T // tm),
        in_specs=[pl.BlockSpec((tm, D), lambda j, i: (i, 0)), pl.BlockSpec((None, Fs, D), lambda j, i: (j, 0, 0)), io, io],
        out_specs=[io, io], out_shape=[shp, shp],
        compiler_params=_params(("parallel", "parallel"), blk),
    )(dfb, wd, g, u)


def _nt_panel(lhs_list, w_list, name):
    T = lhs_list[0].shape[0]
    nsh, Dout, Ks = w_list[0].shape
    npair = len(lhs_list)
    tm = _tm(T)
    tn = Dout // 4 if npair * nsh * Ks > 4096 else Dout // 2

    def body(*refs):
        l_refs, w_refs, o_ref = refs[:npair], refs[npair:2 * npair], refs[2 * npair]
        acc = None
        for p in range(npair):
            for j in range(nsh):
                part = lax.dot_general(l_refs[p][:, j * Ks:(j + 1) * Ks], w_refs[p][j], NT_DIMS,
                                       preferred_element_type=F32)
                acc = part if acc is None else acc + part
        o_ref[...] = acc

    blk = npair * (_nbytes((tm, nsh * Ks), BF16) + _nbytes((nsh, tn, Ks), BF16)) + 3 * _nbytes((tm, tn), F32)
    return pl.pallas_call(
        body, name=name, grid=(Dout // tn, T // tm),
        in_specs=[pl.BlockSpec((tm, nsh * Ks), lambda n, i: (i, 0))] * npair
                 + [pl.BlockSpec((nsh, tn, Ks), lambda n, i: (0, n, 0))] * npair,
        out_specs=pl.BlockSpec((tm, tn), lambda n, i: (i, n)),
        out_shape=jax.ShapeDtypeStruct((T, Dout), F32),
        compiler_params=_params(("parallel", "parallel"), blk),
    )(*lhs_list, *w_list)


def _tn_call(name, grid, lhs, lhs_spec, rhs_list, rhs_specs, out_shapes, out_specs, blk):
    nr = len(rhs_list)

    def body(*refs):
        l_ref, r_refs, o_refs = refs[0], refs[1:1 + nr], refs[1 + nr:]
        k = pl.program_id(len(grid) - 1)
        lv = l_ref[...]
        for q in range(nr):
            part = lax.dot_general(lv, r_refs[q][...], TN_DIMS, preferred_element_type=F32)
            part = part.reshape(o_refs[q].shape)

            @pl.when(k == 0)
            def _(o=o_refs[q], part=part):
                o[...] = part

            @pl.when(k > 0)
            def _(o=o_refs[q], part=part):
                o[...] += part

    return pl.pallas_call(
        body, name=name, grid=grid, in_specs=[lhs_spec] + rhs_specs, out_specs=out_specs, out_shape=out_shapes,
        compiler_params=_params(("parallel",) * (len(grid) - 1) + ("arbitrary",), blk),
    )(lhs, *rhs_list)


def _tk(T):
    return _row_tile(T, 1408, 128)


def _wgrad_cols(n, rhs_list, name):
    T, D = n.shape
    Ns = rhs_list[0].shape[1] // N_CHIPS
    tk = _tk(T)
    nr = len(rhs_list)
    blk = _nbytes((tk, D // 2), BF16) + nr * (_nbytes((tk, Ns), BF16) + 2 * _nbytes((D // 2, Ns), F32))
    return _tn_call(
        name, (N_CHIPS, 2, T // tk), n, pl.BlockSpec((tk, D // 2), lambda j, m, k: (k, m)),
        rhs_list, [pl.BlockSpec((tk, Ns), lambda j, m, k: (k, j))] * nr,
        [jax.ShapeDtypeStruct((N_CHIPS, 2, D // 2, Ns), F32)] * nr,
        [pl.BlockSpec((None, None, D // 2, Ns), lambda j, m, k: (j, m, 0, 0))] * nr, blk)


def _wgrad_down(a, dfb, name):
    T, F = a.shape
    D = dfb.shape[1]
    Fs = F // N_CHIPS
    tk = _tk(T)
    tn = D // 2
    blk = _nbytes((tk, Fs), BF16) + _nbytes((tk, tn), BF16) + 2 * _nbytes((Fs, tn), F32)
    return _tn_call(
        name, (N_CHIPS, D // tn, T // tk), a, pl.BlockSpec((tk, Fs), lambda j, n, k: (k, j)),
        [dfb], [pl.BlockSpec((tk, tn), lambda j, n, k: (k, n))],
        [jax.ShapeDtypeStruct((N_CHIPS, 2, Fs // 2, D), F32)],
        [pl.BlockSpec((None, 2, Fs // 2, tn), lambda j, n, k: (j, 0, 0, n))], blk)[0]


def _wgrad_out(y, dmb):
    T, D = y.shape
    tk = _tk(T)
    tn = D // 2
    rows = D // (2 * N_CHIPS)
    blk = _nbytes((tk, D // 2), BF16) + _nbytes((tk, tn), BF16) + 2 * _nbytes((D // 2, tn), F32)
    return _tn_call(
        "wgrad_w_out", (2, D // tn, T // tk), y, pl.BlockSpec((tk, D // 2), lambda m, n, k: (k, m)),
        [dmb], [pl.BlockSpec((tk, tn), lambda m, n, k: (k, n))],
        [jax.ShapeDtypeStruct((N_CHIPS, 2, rows, D), F32)],
        [pl.BlockSpec((2, 2, rows, tn), lambda m, n, k: (m, 0, 0, n))], blk)[0]


def _row_masks(i, last):
    rows = i * TT + lax.broadcasted_iota(jnp.int32, (TT, 1), 0)
    prows = i * TT - HALO + lax.broadcasted_iota(jnp.int32, (HALO, 1), 0)
    return rows >= PAD, (prows >= PAD) & (i > 0), i < last


def _conv_inputs(u, up, mask_c, mask_p, zbuf, pbuf, C1):
    b, c, v, a, g = (u[:, k * C1:(k + 1) * C1] for k in range(5))
    cp, vp, ap, gp = (up[:, k * C1:(k + 1) * C1] for k in range(1, 5))
    sg = _sigmoid(g)
    pbuf[0:HALO, :] = jnp.where(mask_p, cp * vp, 0.0)
    pbuf[HALO:, :] = jnp.where(mask_c, c * v, 0.0)
    zbuf[0:HALO, :] = jnp.where(mask_p, ap * _sigmoid(gp), 0.0)
    zbuf[HALO:, :] = jnp.where(mask_c, a * sg, 0.0)
    return b, c, v, a, sg


def _causal_conv(w_ref, buf):
    K = w_ref.shape[0]
    acc = None
    for k in range(K):
        lo = HALO - (K - 1) + k
        term = w_ref[k:k + 1, :] * buf[lo:lo + TT, :]
        acc = term if acc is None else acc + term
    return acc


def _anticausal_conv(w_ref, buf):
    K = w_ref.shape[0]
    acc = None
    for k in range(K):
        lo = K - 1 - k
        term = w_ref[k:k + 1, :] * buf[lo:lo + TT, :]
        acc = term if acc is None else acc + term
    return acc


def _conv_weight_sums(dw_ref, dy, buf):
    K = dw_ref.shape[0]
    for k in range(K):
        lo = HALO - (K - 1) + k
        dw_ref[k:k + 1, :] += jnp.sum(dy * buf[lo:lo + TT, :], axis=0, keepdims=True)


def _layernorm_stats(z1):
    mu = jnp.mean(z1, axis=-1, keepdims=True)
    zc = z1 - mu
    rs = lax.rsqrt(jnp.mean(zc * zc, axis=-1, keepdims=True) + EPS)
    return zc * rs, rs


def _mixer_specs(T, DIN, C1, ksc, kcf):
    cur = pl.BlockSpec((TT, DIN), lambda i: (i, 0))
    prev = pl.BlockSpec((HALO, DIN), lambda i: (jnp.maximum(i * (TT // HALO) - 1, 0), 0))
    full = lambda r: pl.BlockSpec((r, C1), lambda i: (0, 0))
    return cur, prev, [full(ksc), full(kcf), full(1), full(1), full(1)]


def _mix_conv_fwd(u, wsc, wcf, bcf, lg, lb):
    T, DIN = u.shape
    C1 = DIN // 5
    last = T // TT - 1

    def body(u_ref, up_ref, wsc_ref, wcf_ref, bcf_ref, lg_ref, lb_ref, y_ref, zbuf, pbuf):
        i = pl.program_id(0)
        mask_c, mask_p, _ = _row_masks(i, last)
        b, _, _, _, _ = _conv_inputs(u_ref[...], up_ref[...], mask_c, mask_p, zbuf, pbuf, C1)
        cs = _causal_conv(wsc_ref, pbuf)
        z1 = _causal_conv(wcf_ref, zbuf) + bcf_ref[...]
        zh, _ = _layernorm_stats(z1)
        ln = zh * lg_ref[...] + lb_ref[...]
        y_ref[:, 0:C1] = jnp.where(mask_c, b * cs, 0.0).astype(BF16)
        y_ref[:, C1:] = jnp.where(mask_c, jax.nn.silu(ln), 0.0).astype(BF16)

    cur, prev, small = _mixer_specs(T, DIN, C1, wsc.shape[0], wcf.shape[0])
    blk = _nbytes((TT + HALO, DIN), F32) + _nbytes((TT, 2 * C1), BF16) + 12 * _nbytes((TT + HALO, C1), F32)
    return pl.pallas_call(
        body, name="mix_conv_fwd", grid=(T // TT,),
        in_specs=[cur, prev] + small,
        out_specs=pl.BlockSpec((TT, 2 * C1), lambda i: (i, 0)),
        out_shape=jax.ShapeDtypeStruct((T, 2 * C1), BF16),
        scratch_shapes=[pltpu.VMEM((TT + HALO, C1), F32), pltpu.VMEM((TT + HALO, C1), F32)],
        compiler_params=_params(("arbitrary",), blk),
    )(u, u, wsc, wcf, bcf, lg, lb)


def _mix_conv_bwd1(u, dy, wsc, wcf, bcf, lg, lb):
    T, DIN = u.shape
    C1 = DIN // 5
    last = T // TT - 1

    def body(u_ref, up_ref, dy_ref, wsc_ref, wcf_ref, bcf_ref, lg_ref, lb_ref,
             dz1_ref, dcs_ref, db_ref, dlg_ref, dlb_ref, dbcf_ref, zbuf, pbuf):
        i = pl.program_id(0)
        mask_c, mask_p, _ = _row_masks(i, last)
        b, _, _, _, _ = _conv_inputs(u_ref[...], up_ref[...], mask_c, mask_p, zbuf, pbuf, C1)
        cs = _causal_conv(wsc_ref, pbuf)
        z1 = _causal_conv(wcf_ref, zbuf) + bcf_ref[...]
        zh, rs = _layernorm_stats(z1)
        ln = zh * lg_ref[...] + lb_ref[...]
        dy = dy_ref[...]
        dysc = jnp.where(mask_c, dy[:, 0:C1], 0.0)
        dycf = jnp.where(mask_c, dy[:, C1:], 0.0)
        db_ref[...] = (dysc * cs).astype(BF16)
        dcs_ref[...] = dysc * b
        dl = dycf * _dsilu(ln, _sigmoid(ln))
        dzh = dl * lg_ref[...]
        dz1 = rs * (dzh - jnp.mean(dzh, axis=-1, keepdims=True) - zh * jnp.mean(dzh * zh, axis=-1, keepdims=True))
        dz1_ref[...] = dz1

        @pl.when(i == 0)
        def _():
            dlg_ref[...] = jnp.zeros_like(dlg_ref)
            dlb_ref[...] = jnp.zeros_like(dlb_ref)
            dbcf_ref[...] = jnp.zeros_like(dbcf_ref)

        dlg_ref[...] += jnp.sum(dl * zh, axis=0, keepdims=True)
        dlb_ref[...] += jnp.sum(dl, axis=0, keepdims=True)
        dbcf_ref[...] += jnp.sum(dz1, axis=0, keepdims=True)

    cur, prev, small = _mixer_specs(T, DIN, C1, wsc.shape[0], wcf.shape[0])
    tile = lambda: pl.BlockSpec((TT, C1), lambda i: (i, 0))
    vec = lambda: pl.BlockSpec((1, C1), lambda i: (0, 0))
    blk = _nbytes((TT + HALO, DIN), F32) + 4 * _nbytes((TT, C1), F32) + 16 * _nbytes((TT + HALO, C1), F32)
    return pl.pallas_call(
        body, name="mix_conv_bwd1", grid=(T // TT,),
        in_specs=[cur, prev, pl.BlockSpec((TT, 2 * C1), lambda i: (i, 0))] + small,
        out_specs=[tile(), tile(), tile(), vec(), vec(), vec()],
        out_shape=[jax.ShapeDtypeStruct((T, C1), F32), jax.ShapeDtypeStruct((T, C1), F32),
                   jax.ShapeDtypeStruct((T, C1), BF16)] + [jax.ShapeDtypeStruct((1, C1), F32)] * 3,
        scratch_shapes=[pltpu.VMEM((TT + HALO, C1), F32), pltpu.VMEM((TT + HALO, C1), F32)],
        compiler_params=_params(("arbitrary",), blk),
    )(u, u, dy, wsc, wcf, bcf, lg, lb)


def _mix_conv_bwd2(u, dz1, dcs, db, wsc, wcf):
    T, DIN = u.shape
    C1 = DIN // 5
    last = T // TT - 1
    ksc, kcf = wsc.shape[0], wcf.shape[0]

    def body(u_ref, up_ref, dz_ref, dzn_ref, dc_ref, dcn_ref, db_ref, wsc_ref, wcf_ref,
             du_ref, dbin_ref, dwsc_ref, dwcf_ref, zbuf, pbuf, dzbuf, dcbuf):
        i = pl.program_id(0)
        mask_c, mask_p, has_next = _row_masks(i, last)
        _, c, v, a, sg = _conv_inputs(u_ref[...], up_ref[...], mask_c, mask_p, zbuf, pbuf, C1)
        dz1 = dz_ref[...]
        dcs = dc_ref[...]
        dzbuf[0:TT, :] = dz1
        dzbuf[TT:, :] = jnp.where(has_next, dzn_ref[...], 0.0)
        dcbuf[0:TT, :] = dcs
        dcbuf[TT:, :] = jnp.where(has_next, dcn_ref[...], 0.0)

        @pl.when(i == 0)
        def _():
            dbin_ref[...] = jnp.zeros_like(dbin_ref)
            dwsc_ref[...] = jnp.zeros_like(dwsc_ref)
            dwcf_ref[...] = jnp.zeros_like(dwcf_ref)

        _conv_weight_sums(dwcf_ref, dz1, zbuf)
        _conv_weight_sums(dwsc_ref, dcs, pbuf)
        dz0 = jnp.where(mask_c, _anticausal_conv(wcf_ref, dzbuf), 0.0)
        dp = jnp.where(mask_c, _anticausal_conv(wsc_ref, dcbuf), 0.0)
        parts = (db_ref[...].astype(F32), dp * v, dp * c, dz0 * sg, dz0 * a * sg * (1.0 - sg))
        for k, part in enumerate(parts):
            du_ref[:, k * C1:(k + 1) * C1] = part.astype(BF16)
            dbin_ref[:, k * C1:(k + 1) * C1] += jnp.sum(part, axis=0, keepdims=True)

    cur, prev, small = _mixer_specs(T, DIN, C1, ksc, kcf)
    tile = lambda: pl.BlockSpec((TT, C1), lambda i: (i, 0))
    nxt = lambda: pl.BlockSpec((HALO, C1), lambda i: (jnp.minimum((i + 1) * (TT // HALO), T // HALO - 1), 0))
    blk = (_nbytes((TT + HALO, DIN), F32) + _nbytes((TT, DIN), BF16) + 5 * _nbytes((TT, C1), F32)
           + 16 * _nbytes((TT + HALO, C1), F32))
    buf = lambda: pltpu.VMEM((TT + HALO, C1), F32)
    return pl.pallas_call(
        body, name="mix_conv_bwd2", grid=(T // TT,),
        in_specs=[cur, prev, tile(), nxt(), tile(), nxt(), tile(), small[0], small[1]],
        out_specs=[pl.BlockSpec((TT, DIN), lambda i: (i, 0)), pl.BlockSpec((1, DIN), lambda i: (0, 0)),
                   pl.BlockSpec((ksc, C1), lambda i: (0, 0)), pl.BlockSpec((kcf, C1), lambda i: (0, 0))],
        out_shape=[jax.ShapeDtypeStruct((T, DIN), BF16), jax.ShapeDtypeStruct((1, DIN), F32),
                   jax.ShapeDtypeStruct((ksc, C1), F32), jax.ShapeDtypeStruct((kcf, C1), F32)],
        scratch_shapes=[buf(), buf(), buf(), buf()],
        compiler_params=_params(("arbitrary",), blk),
    )(u, u, dz1, dz1, dcs, dcs, db, wsc, wcf)


def _place():
    x, y, c = lax.axis_index("x"), lax.axis_index("y"), lax.axis_index("c")
    chips = [(1 - x, y), (x, 1 - y), (1 - x, 1 - y)]
    return x, y, c, chips


ANY = pl.BlockSpec(memory_space=pl.ANY)


def _gather_weights(ws):
    nw = len(ws)

    def body(*refs):
        w_refs, o_refs = refs[:nw], refs[nw:2 * nw]
        send, recv, lsem = refs[2 * nw:]
        x, y, c, chips = _place()
        s = 2 * x + y
        sib = (x, y, 1 - c)

        def remote(w, k, src, dst, to):
            return pltpu.make_async_remote_copy(src_ref=src, dst_ref=dst, send_sem=send.at[6 * w + k],
                                                recv_sem=recv.at[6 * w + k], device_id=to, device_id_type=MESH)

        local = [pltpu.make_async_copy(w_refs[w].at[c], o_refs[w].at[s, c], lsem.at[w]) for w in range(nw)]
        for cp in local:
            cp.start()
        local2 = [pltpu.make_async_copy(w_refs[w].at[1 - c], o_refs[w].at[s, 1 - c], lsem.at[nw + w]) for w in range(nw)]
        for cp in local2:
            cp.start()
        sends = []
        for w in range(nw):
            for r, (tx, ty) in enumerate(chips):
                cp = remote(w, r, w_refs[w].at[c], o_refs[w].at[s, c], (tx, ty, c))
                cp.start()
                sends.append(cp)
        for w in range(nw):
            for r, (tx, ty) in enumerate(chips):
                sr = 2 * tx + ty
                remote(w, r, w_refs[w].at[c], o_refs[w].at[sr, c], (tx, ty, c)).wait_recv()
                cp = remote(w, 3 + r, o_refs[w].at[sr, c], o_refs[w].at[sr, c], sib)
                cp.start()
                sends.append(cp)
        for w in range(nw):
            for r, (tx, ty) in enumerate(chips):
                sr = 2 * tx + ty
                remote(w, 3 + r, o_refs[w].at[sr, 1 - c], o_refs[w].at[sr, 1 - c], sib).wait_recv()
        for cp in sends:
            cp.wait_send()
        for cp in local + local2:
            cp.wait()

    return pl.pallas_call(
        body, name="gather_weights", in_specs=[ANY] * nw, out_specs=[ANY] * nw,
        out_shape=[jax.ShapeDtypeStruct((N_CHIPS,) + w.shape, w.dtype) for w in ws],
        scratch_shapes=[pltpu.SemaphoreType.DMA((6 * nw,)), pltpu.SemaphoreType.DMA((6 * nw,)),
                        pltpu.SemaphoreType.DMA((2 * nw,))],
    )(*ws)


def _pair_exchange(gs):
    nw = len(gs)

    def body(*refs):
        g_refs, o_refs = refs[:nw], refs[nw:2 * nw]
        send, recv = refs[2 * nw:]
        x, y, c, _ = _place()
        sib = (x, y, 1 - c)
        copies = []
        for w in range(nw):
            for j in range(N_CHIPS):
                cp = pltpu.make_async_remote_copy(
                    src_ref=g_refs[w].at[j, 1 - c], dst_ref=o_refs[w].at[j], send_sem=send.at[N_CHIPS * w + j],
                    recv_sem=recv.at[N_CHIPS * w + j], device_id=sib, device_id_type=MESH)
                cp.start()
                copies.append(cp)
        for cp in copies:
            cp.wait()

    return pl.pallas_call(
        body, name="pair_exchange", in_specs=[ANY] * nw, out_specs=[ANY] * nw,
        out_shape=[jax.ShapeDtypeStruct((N_CHIPS,) + g.shape[2:], F32) for g in gs],
        scratch_shapes=[pltpu.SemaphoreType.DMA((N_CHIPS * nw,)), pltpu.SemaphoreType.DMA((N_CHIPS * nw,))],
    )(*gs)


def _chip_exchange(qs):
    nw = len(qs)

    def body(*refs):
        q_refs, o_refs = refs[:nw], refs[nw:2 * nw]
        send, recv = refs[2 * nw:]
        x, y, c, chips = _place()
        copies = []
        for w in range(nw):
            for r, (tx, ty) in enumerate(chips):
                cp = pltpu.make_async_remote_copy(
                    src_ref=q_refs[w].at[2 * tx + ty], dst_ref=o_refs[w].at[r], send_sem=send.at[3 * w + r],
                    recv_sem=recv.at[3 * w + r], device_id=(tx, ty, c), device_id_type=MESH)
                cp.start()
                copies.append(cp)
        for cp in copies:
            cp.wait()

    return pl.pallas_call(
        body, name="chip_exchange", in_specs=[ANY] * nw, out_specs=[ANY] * nw,
        out_shape=[jax.ShapeDtypeStruct((3,) + q.shape[1:], q.dtype) for q in qs],
        scratch_shapes=[pltpu.SemaphoreType.DMA((3 * nw,)), pltpu.SemaphoreType.DMA((3 * nw,))],
    )(*qs)


def _half_exchange(hs):
    nw = len(hs)

    def body(*refs):
        h_refs, o_refs = refs[:nw], refs[nw:2 * nw]
        send, recv, lsem = refs[2 * nw:]
        x, y, c, _ = _place()
        sib = (x, y, 1 - c)
        copies = []
        for w in range(nw):
            loc = pltpu.make_async_copy(h_refs[w], o_refs[w].at[c], lsem.at[w])
            loc.start()
            cp = pltpu.make_async_remote_copy(src_ref=h_refs[w], dst_ref=o_refs[w].at[c], send_sem=send.at[w],
                                              recv_sem=recv.at[w], device_id=sib, device_id_type=MESH)
            cp.start()
            copies.append((loc, cp))
        for w, (loc, cp) in enumerate(copies):
            loc.wait()
            cp.wait_send()
            pltpu.make_async_remote_copy(src_ref=h_refs[w], dst_ref=o_refs[w].at[1 - c], send_sem=send.at[w],
                                         recv_sem=recv.at[w], device_id=sib, device_id_type=MESH).wait_recv()

    return pl.pallas_call(
        body, name="half_exchange", in_specs=[ANY] * nw, out_specs=[ANY] * nw,
        out_shape=[jax.ShapeDtypeStruct((2,) + h.shape, F32) for h in hs],
        scratch_shapes=[pltpu.SemaphoreType.DMA((nw,)), pltpu.SemaphoreType.DMA((nw,)),
                        pltpu.SemaphoreType.DMA((nw,))],
    )(*hs)


def _share_small(v, reduce, name):
    R, C = v.shape

    def body(v_ref, o_ref, *scratch):
        if reduce:
            all_ref, send, recv, lsem = scratch
        else:
            all_ref = o_ref
            send, recv, lsem = scratch
        x, y, c, _ = _place()
        me = 4 * x + 2 * y + c
        loc = pltpu.make_async_copy(v_ref, all_ref.at[me], lsem)
        loc.start()
        copies = []
        for k in range(1, N_DEV):
            kx, ky, kc = (k >> 2) & 1, (k >> 1) & 1, k & 1
            peer = (x ^ kx, y ^ ky, c ^ kc)
            cp = pltpu.make_async_remote_copy(src_ref=v_ref, dst_ref=all_ref.at[me], send_sem=send.at[k - 1],
                                              recv_sem=recv.at[k - 1], device_id=peer, device_id_type=MESH)
            cp.start()
            copies.append(cp)
        for k in range(1, N_DEV):
            kx, ky, kc = (k >> 2) & 1, (k >> 1) & 1, k & 1
            src = 4 * (x ^ kx) + 2 * (y ^ ky) + (c ^ kc)
            pltpu.make_async_remote_copy(src_ref=v_ref, dst_ref=all_ref.at[src], send_sem=send.at[k - 1],
                                         recv_sem=recv.at[k - 1], device_id=(x, y, c), device_id_type=MESH).wait_recv()
        for cp in copies:
            cp.wait_send()
        loc.wait()
        if reduce:
            total = all_ref[0]
            for d in range(1, N_DEV):
                total = total + all_ref[d]
            o_ref[...] = total

    vm = pl.BlockSpec(memory_space=pltpu.VMEM)
    sems = [pltpu.SemaphoreType.DMA((N_DEV - 1,)), pltpu.SemaphoreType.DMA((N_DEV - 1,)), pltpu.SemaphoreType.DMA]
    if reduce:
        out_shape = jax.ShapeDtypeStruct((R, C), F32)
        scratch = [pltpu.VMEM((N_DEV, R, C), F32)] + sems
    else:
        out_shape = jax.ShapeDtypeStruct((N_DEV, R, C), F32)
        scratch = sems
    return pl.pallas_call(
        body, name=name, in_specs=[vm], out_specs=vm, out_shape=out_shape, scratch_shapes=scratch,
        compiler_params=pltpu.CompilerParams(vmem_limit_bytes=int(min(4 * N_DEV * R * C * 4 + 2 ** 24, 2 ** 25 + 2 ** 24))),
    )(v)


def _pair_sum(place, g, rb, name):
    _, _, Rh, C = g.shape
    tr = _row_tile(Rh, 256, 16)

    def body(place_ref, g_ref, r_ref, q_ref):
        q_ref[...] = (g_ref[...] + r_ref[...]).astype(BF16)

    blk = 2 * _nbytes((tr, C), F32) + _nbytes((tr, C), BF16)
    return pl.pallas_call(
        body, name=name,
        grid_spec=pltpu.PrefetchScalarGridSpec(
            num_scalar_prefetch=1, grid=(N_CHIPS, Rh // tr),
            in_specs=[pl.BlockSpec((None, None, tr, C), lambda j, i, p: (j, p[1], i, 0)),
                      pl.BlockSpec((None, tr, C), lambda j, i, p: (j, i, 0))],
            out_specs=pl.BlockSpec((None, tr, C), lambda j, i, p: (j, i, 0))),
        out_shape=jax.ShapeDtypeStruct((N_CHIPS, Rh, C), BF16),
        compiler_params=_params(("parallel", "parallel"), blk),
    )(place, g, rb)


def _chip_sum(place, g, rb, rc, name):
    _, _, Rh, C = g.shape
    tr = _row_tile(Rh, 256, 16)

    def body(place_ref, g_ref, r_ref, rc_ref, o_ref):
        total = g_ref[...] + r_ref[...]
        for r in range(3):
            total = total + rc_ref[r].astype(F32)
        o_ref[...] = total

    blk = 3 * _nbytes((tr, C), F32) + 3 * _nbytes((tr, C), BF16)
    return pl.pallas_call(
        body, name=name,
        grid_spec=pltpu.PrefetchScalarGridSpec(
            num_scalar_prefetch=1, grid=(Rh // tr,),
            in_specs=[pl.BlockSpec((None, None, tr, C), lambda i, p: (p[0], p[1], i, 0)),
                      pl.BlockSpec((None, tr, C), lambda i, p: (p[0], i, 0)),
                      pl.BlockSpec((3, tr, C), lambda i, p: (0, i, 0))],
            out_specs=pl.BlockSpec((tr, C), lambda i, p: (i, 0))),
        out_shape=jax.ShapeDtypeStruct((Rh, C), F32),
        compiler_params=_params(("parallel",), blk),
    )(place, g, rb, rc)


def _adamw_math(w, g, m, v):
    m = ADAM_B1 * m + (1.0 - ADAM_B1) * g
    v = ADAM_B2 * v + (1.0 - ADAM_B2) * jnp.square(g)
    m_hat = m / (1.0 - ADAM_B1 ** ADAM_STEP)
    v_hat = v / (1.0 - ADAM_B2 ** ADAM_STEP)
    delta = -ADAM_LR * (m_hat / (jnp.sqrt(v_hat) + ADAM_EPS) + ADAM_WD * w)
    return delta, m, v


def _adamw(w, g, m, v, name):
    R, C = w.shape
    tr = _row_tile(R, 256)

    def body(w_ref, g_ref, m_ref, v_ref, d_ref, nm_ref, nv_ref):
        d, nm, nv = _adamw_math(w_ref[...], g_ref[...], m_ref[...], v_ref[...])
        d_ref[...] = d
        nm_ref[...] = nm
        nv_ref[...] = nv

    spec = pl.BlockSpec((tr, C), lambda i: (i, 0))
    shp = jax.ShapeDtypeStruct((R, C), F32)
    return pl.pallas_call(
        body, name=name, grid=(R // tr,), in_specs=[spec] * 4, out_specs=[spec] * 3, out_shape=[shp] * 3,
        compiler_params=_params(("parallel",), 7 * _nbytes((tr, C), F32)),
    )(w, g, m, v)


def _adamw_small(ws, gs, ms, vs):
    n = len(ws)

    def body(*refs):
        for k in range(n):
            w_ref, g_ref, m_ref, v_ref = (refs[q * n + k] for q in range(4))
            d, nm, nv = _adamw_math(w_ref[...], g_ref[...], m_ref[...], v_ref[...])
            refs[4 * n + k][...] = d
            refs[5 * n + k][...] = nm
            refs[6 * n + k][...] = nv

    vm = pl.BlockSpec(memory_space=pltpu.VMEM)
    shapes = [jax.ShapeDtypeStruct(w.shape, F32) for w in ws]
    outs = pl.pallas_call(
        body, name="adamw_small", in_specs=[vm] * (4 * n), out_specs=[vm] * (3 * n), out_shape=shapes * 3,
    )(*ws, *gs, *ms, *vs)
    return outs[:n], outs[n:2 * n], outs[2 * n:]


def _pad_rows(a, rows):
    return jnp.pad(a, ((0, rows - a.shape[0]), (0, 0)))


def kernel(x, meta_tokens, ffn1_norm, ffn1_w_gate, ffn1_w_up, ffn1_w_down, mix_norm, w_in, b_in, conv_sc_w, conv_cf_w, conv_cf_b, ln_cf_g, ln_cf_b, w_out, ffn2_norm, ffn2_w_gate, ffn2_w_up, ffn2_w_down, final_norm, loss_target, m_meta_tokens, m_ffn1_norm, m_ffn1_w_gate, m_ffn1_w_up, m_ffn1_w_down, m_mix_norm, m_w_in, m_b_in, m_conv_sc_w, m_conv_cf_w, m_conv_cf_b, m_ln_cf_g, m_ln_cf_b, m_w_out, m_ffn2_norm, m_ffn2_w_gate, m_ffn2_w_up, m_ffn2_w_down, m_final_norm, v_meta_tokens, v_ffn1_norm, v_ffn1_w_gate, v_ffn1_w_up, v_ffn1_w_down, v_mix_norm, v_w_in, v_b_in, v_conv_sc_w, v_conv_cf_w, v_conv_cf_b, v_ln_cf_g, v_ln_cf_b, v_w_out, v_ffn2_norm, v_ffn2_w_gate, v_ffn2_w_up, v_ffn2_w_down, v_final_norm):
    xi, yi, ci = lax.axis_index("x"), lax.axis_index("y"), lax.axis_index("c")
    chip = 2 * xi + yi
    place = jnp.stack([chip, ci]).astype(jnp.int32)

    x2 = x[0]
    tgt = loss_target[0]
    S, D = x2.shape
    C1 = D // 2
    cs = conv_sc_w.shape[2]
    ksc, kcf = conv_sc_w.shape[1], conv_cf_w.shape[1]
    ms = meta_tokens.shape[1]

    rows_small = N_META + 8 + 32
    assert ksc <= 8 and kcf <= 32 and cs <= ms
    pack = jnp.concatenate([
        meta_tokens,
        jnp.pad(conv_sc_w[0], ((0, 8 - ksc), (0, ms - cs))),
        jnp.pad(conv_cf_w[0], ((0, 32 - kcf), (0, ms - cs)))], axis=0)
    everyone = _share_small(pack, False, "share_params")[0::2]
    meta_full = jnp.transpose(everyone[:, :N_META, :], (1, 0, 2)).reshape(N_META, D)
    wsc_full = jnp.transpose(everyone[:, N_META:N_META + ksc, :cs], (1, 0, 2)).reshape(ksc, C1)
    wcf_full = jnp.transpose(everyone[:, N_META + 8:N_META + 8 + kcf, :cs], (1, 0, 2)).reshape(kcf, C1)

    big = [ffn1_w_gate[0], ffn1_w_up[0], ffn1_w_down[0], w_in[0], w_out[0], ffn2_w_gate[0], ffn2_w_up[0], ffn2_w_down[0]]
    halves = [w.astype(BF16).reshape(2, w.shape[0] // 2, w.shape[1]) for w in big]
    gathered = _gather_weights(halves)
    full = [g.reshape(N_CHIPS, 2 * g.shape[2], g.shape[3]) for g in gathered]
    wg1, wu1, wd1, win, wout, wg2, wu2, wd2 = full
    F = N_CHIPS * wd1.shape[1]

    hs0, n1 = _embed_rms(x2, meta_full, ffn1_norm)
    g1, u1, a1 = _ffn_up(n1, wg1, wu1, "ffn1_up")
    hs1 = _ffn_down(a1, wd1.reshape(F, D), hs0, "ffn1_down")
    n2 = _rms(hs1, mix_norm, "rms_mix")
    u = _mix_in(n2, win, b_in)
    y = _mix_conv_fwd(u, wsc_full, wcf_full, conv_cf_b, ln_cf_g, ln_cf_b)
    hs2 = _mix_out(y, wout.reshape(D, D), hs1)
    n3 = _rms(hs2, ffn2_norm, "rms_ffn2")
    g2, u2, a2 = _ffn_up(n3, wg2, wu2, "ffn2_up")
    hs3 = _ffn_down(a2, wd2.reshape(F, D), hs2, "ffn2_down")

    dhs3, df2, loss_row, d_final = _final_loss(hs3, final_norm.reshape(1, D), tgt)

    dg2, du2 = _ffn_bwd_act(df2, wd2, g2, u2, "ffn2_bwd_act")
    gw_d2 = _wgrad_down(a2, df2, "wgrad_ffn2_down")
    gw_g2 = _wgrad_cols(n3, [dg2], "wgrad_ffn2_gate")[0]
    gw_u2 = _wgrad_cols(n3, [du2], "wgrad_ffn2_up")[0]
    dn3 = _nt_panel([dg2, du2], [wg2, wu2], "ffn2_bwd_in")
    dhs2, dm, d_ffn2 = _rms_bwd(dn3, hs2, ffn2_norm, dhs3, 1.0, "rms_bwd_ffn2")

    dy = _nt_panel([dm], [wout.reshape(1, D, D)], "mix_bwd_out")
    gw_out = _wgrad_out(y, dm)
    dz1, dcs, db, d_lg, d_lb, d_bcf = _mix_conv_bwd1(u, dy, wsc_full, wcf_full, conv_cf_b, ln_cf_g, ln_cf_b)
    du, d_bin, d_wsc, d_wcf = _mix_conv_bwd2(u, dz1, dcs, db, wsc_full, wcf_full)
    gw_in = _wgrad_cols(n2, [du], "wgrad_w_in")[0]
    dn2 = _nt_panel([du], [win], "mix_bwd_in")
    dhs1, df1, d_mix = _rms_bwd(dn2, hs1, mix_norm, dhs2, FFN_RES_SCALE, "rms_bwd_mix")

    dg1, du1 = _ffn_bwd_act(df1, wd1, g1, u1, "ffn1_bwd_act")
    gw_d1 = _wgrad_down(a1, df1, "wgrad_ffn1_down")
    gw_g1 = _wgrad_cols(n1, [dg1], "wgrad_ffn1_gate")[0]
    gw_u1 = _wgrad_cols(n1, [du1], "wgrad_ffn1_up")[0]
    dn1 = _nt_panel([dg1, du1], [wg1, wu1], "ffn1_bwd_in")
    grad_x, d_meta, d_ffn1 = _rms_bwd_first(dn1, hs0, ffn1_norm, dhs1)

    names = ["ffn1_w_gate", "ffn1_w_up", "ffn1_w_down", "w_in", "w_out", "ffn2_w_gate", "ffn2_w_up", "ffn2_w_down"]
    grads = [gw_g1, gw_u1, gw_d1, gw_in, gw_out, gw_g2, gw_u2, gw_d2]
    from_sibling = _pair_exchange(grads)
    pair_sums = [_pair_sum(place, g, rb, "pair_sum_" + nm) for g, rb, nm in zip(grads, from_sibling, names)]
    from_chips = _chip_exchange(pair_sums)
    mine = [_chip_sum(place, g, rb, rc, "chip_sum_" + nm)
            for g, rb, rc, nm in zip(grads, from_sibling, from_chips, names)]
    whole = _half_exchange(mine)
    big_m = [m_ffn1_w_gate, m_ffn1_w_up, m_ffn1_w_down, m_w_in, m_w_out, m_ffn2_w_gate, m_ffn2_w_up, m_ffn2_w_down]
    big_v = [v_ffn1_w_gate, v_ffn1_w_up, v_ffn1_w_down, v_w_in, v_w_out, v_ffn2_w_gate, v_ffn2_w_up, v_ffn2_w_down]
    big_out = {}
    for nm, w, g, m, v in zip(names, big, whole, big_m, big_v):
        g2d = g.reshape(w.shape)
        d, nm_, nv_ = _adamw(w, g2d, m[0], v[0], "adamw_" + nm)
        big_out[nm] = tuple(t[None] for t in (g2d, d, nm_, nv_))

    W = C1
    rows = lambda a: a.reshape(-1, W)
    parts = [rows(d_ffn1), rows(d_mix), rows(d_ffn2), rows(d_final), rows(d_bin), d_bcf, d_lg, d_lb,
             d_wsc, d_wcf, rows(d_meta), jnp.broadcast_to(loss_row[:, :1], (1, W))]
    sizes = [p.shape[0] for p in parts]
    total_rows = sum(sizes)
    packed = _pad_rows(jnp.concatenate(parts, axis=0), -(-total_rows // 8) * 8)
    summed = _share_small(packed, True, "sum_small")
    offs = [0]
    for n in sizes:
        offs.append(offs[-1] + n)
    piece = lambda k: summed[offs[k]:offs[k + 1]]
    loss = piece(11)[0, 0]
    g_ffn1, g_mix, g_ffn2 = (piece(k).reshape(1, D) for k in range(3))
    g_final = piece(3).reshape(1, D)
    g_bin = piece(4).reshape(1, -1)
    g_bcf, g_lg, g_lb = piece(5), piece(6), piece(7)
    g_wsc = lax.dynamic_slice_in_dim(piece(8), chip * cs, cs, axis=1)
    g_wcf = lax.dynamic_slice_in_dim(piece(9), chip * cs, cs, axis=1)
    g_meta = lax.dynamic_slice_in_dim(piece(10).reshape(N_META, D), chip * ms, ms, axis=1)

    small_names = ["meta_tokens", "ffn1_norm", "mix_norm", "b_in", "conv_sc_w", "conv_cf_w", "conv_cf_b", "ln_cf_g",
                   "ln_cf_b", "ffn2_norm", "final_norm"]
    small_w = [meta_tokens, ffn1_norm, mix_norm, b_in, conv_sc_w[0], conv_cf_w[0], conv_cf_b, ln_cf_g, ln_cf_b,
               ffn2_norm, final_norm.reshape(1, D)]
    small_g = [g_meta, g_ffn1, g_mix, g_bin, g_wsc, g_wcf, g_bcf, g_lg, g_lb, g_ffn2, g_final]
    small_m = [m_meta_tokens, m_ffn1_norm, m_mix_norm, m_b_in, m_conv_sc_w[0], m_conv_cf_w[0], m_conv_cf_b, m_ln_cf_g,
               m_ln_cf_b, m_ffn2_norm, m_final_norm.reshape(1, D)]
    small_v = [v_meta_tokens, v_ffn1_norm, v_mix_norm, v_b_in, v_conv_sc_w[0], v_conv_cf_w[0], v_conv_cf_b, v_ln_cf_g,
               v_ln_cf_b, v_ffn2_norm, v_final_norm.reshape(1, D)]
    s_d, s_m, s_v = _adamw_small(small_w, small_g, small_m, small_v)
    shapes = {"conv_sc_w": conv_sc_w.shape, "conv_cf_w": conv_cf_w.shape, "final_norm": final_norm.shape}
    small_out = {}
    for nm, g, d, m, v in zip(small_names, small_g, s_d, s_m, s_v):
        shp = shapes.get(nm, g.shape)
        small_out[nm] = tuple(t.reshape(shp) for t in (g, d, m, v))

    order = ["meta_tokens", "ffn1_norm", "ffn1_w_gate", "ffn1_w_up", "ffn1_w_down", "mix_norm", "w_in", "b_in",
             "conv_sc_w", "conv_cf_w", "conv_cf_b", "ln_cf_g", "ln_cf_b", "w_out", "ffn2_norm", "ffn2_w_gate",
             "ffn2_w_up", "ffn2_w_down", "final_norm"]
    res = {**big_out, **small_out}
    outs = [loss, grad_x[None]]
    for q in range(4):
        outs.extend(res[nm][q] for nm in order)
    return tuple(outs)
```

```python
import functools

import jax
import jax.numpy as jnp
from jax import lax
from jax.experimental import pallas as pl
from jax.experimental.pallas import tpu as pltpu

F32 = jnp.float32
BF16 = jnp.bfloat16
MESH = pl.DeviceIdType.MESH

N_META = 16
TT = 128
PAD = TT - N_META
HALO = 32
EPS = 1e-6
FFN_RES_SCALE = 0.5
N_CHIPS = 4
N_DEV = 8

ADAM_LR = 0.001
ADAM_B1 = 0.9
ADAM_B2 = 0.999
ADAM_EPS = 1e-08
ADAM_WD = 0.01
ADAM_STEP = 10

V7X_VMEM_BYTES = 64 * 2 ** 20
NT_DIMS = (((1,), (1,)), ((), ()))
TN_DIMS = (((0,), (0,)), ((), ()))


def _params(semantics, block_bytes):
    limit = min(2 * block_bytes + 16 * 2 ** 20, V7X_VMEM_BYTES - 6 * 2 ** 20)
    return pltpu.CompilerParams(dimension_semantics=semantics, vmem_limit_bytes=int(limit))


def _nbytes(shape, dtype):
    n = 1
    for d in shape:
        if d is not None:
            n *= d
    return n * jnp.dtype(dtype).itemsize


def _row_tile(rows, target, mult=8):
    best = None
    for t in range(mult, min(rows, target) + 1, mult):
        if rows % t == 0:
            best = t
    assert best is not None, (rows, target, mult)
    return best


def _sigmoid(v):
    return jax.nn.sigmoid(v)


def _dsilu(v, s):
    return s * (1.0 + v * (1.0 - s))


def _embed_rms(x2, meta, gain):
    S, D = x2.shape
    T = S + TT

    def body(x_ref, meta_ref, g_ref, hs_ref, n_ref):
        i = pl.program_id(0)

        @pl.when(i == 0)
        def _():
            hs_ref[...] = jnp.zeros_like(hs_ref)
            hs_ref[PAD:, :] = meta_ref[...]

        @pl.when(i > 0)
        def _():
            hs_ref[...] = x_ref[...]

        h = hs_ref[...]
        r = lax.rsqrt(jnp.mean(h * h, axis=-1, keepdims=True) + EPS)
        n_ref[...] = ((h * r) * g_ref[...]).astype(BF16)

    blk = _nbytes((TT, D), F32) * 2 + _nbytes((TT, D), BF16)
    return pl.pallas_call(
        body, name="embed_rms", grid=(T // TT,),
        in_specs=[pl.BlockSpec((TT, D), lambda i: (jnp.maximum(i - 1, 0), 0)),
                  pl.BlockSpec((N_META, D), lambda i: (0, 0)),
                  pl.BlockSpec((1, D), lambda i: (0, 0))],
        out_specs=[pl.BlockSpec((TT, D), lambda i: (i, 0)), pl.BlockSpec((TT, D), lambda i: (i, 0))],
        out_shape=[jax.ShapeDtypeStruct((T, D), F32), jax.ShapeDtypeStruct((T, D), BF16)],
        compiler_params=_params(("parallel",), blk),
    )(x2, meta, gain)


def _rms(hs, gain, name):
    T, D = hs.shape
    te = _row_tile(T, 384)

    def body(h_ref, g_ref, n_ref):
        h = h_ref[...]
        r = lax.rsqrt(jnp.mean(h * h, axis=-1, keepdims=True) + EPS)
        n_ref[...] = ((h * r) * g_ref[...]).astype(BF16)

    blk = _nbytes((te, D), F32) + _nbytes((te, D), BF16)
    return pl.pallas_call(
        body, name=name, grid=(T // te,),
        in_specs=[pl.BlockSpec((te, D), lambda i: (i, 0)), pl.BlockSpec((1, D), lambda i: (0, 0))],
        out_specs=pl.BlockSpec((te, D), lambda i: (i, 0)),
        out_shape=jax.ShapeDtypeStruct((T, D), BF16),
        compiler_params=_params(("parallel",), blk),
    )(hs, gain)


def _rms_bwd_math(dn, h, g):
    r = lax.rsqrt(jnp.mean(h * h, axis=-1, keepdims=True) + EPS)
    xh = h * r
    dgain = jnp.sum(dn * xh, axis=0, keepdims=True)
    dxh = dn * g
    dh = r * (dxh - xh * jnp.mean(dxh * xh, axis=-1, keepdims=True))
    return dh, dgain


def _rms_bwd(dn, hs, gain, dres, scale, name):
    T, D = hs.shape
    te = _row_tile(T, 384)

    def body(dn_ref, h_ref, g_ref, dres_ref, dhs_ref, dhb_ref, dg_ref):
        dh, dgain = _rms_bwd_math(dn_ref[...], h_ref[...], g_ref[...])
        d = dres_ref[...] + dh
        dhs_ref[...] = d
        dhb_ref[...] = (scale * d).astype(BF16)

        @pl.when(pl.program_id(0) == 0)
        def _():
            dg_ref[...] = jnp.zeros_like(dg_ref)

        dg_ref[...] += dgain

    blk = _nbytes((te, D), F32) * 4 + _nbytes((te, D), BF16)
    row = lambda i: (i, 0)
    return pl.pallas_call(
        body, name=name, grid=(T // te,),
        in_specs=[pl.BlockSpec((te, D), row), pl.BlockSpec((te, D), row), pl.BlockSpec((1, D), lambda i: (0, 0)),
                  pl.BlockSpec((te, D), row)],
        out_specs=[pl.BlockSpec((te, D), row), pl.BlockSpec((te, D), row), pl.BlockSpec((1, D), lambda i: (0, 0))],
        out_shape=[jax.ShapeDtypeStruct((T, D), F32), jax.ShapeDtypeStruct((T, D), BF16),
                   jax.ShapeDtypeStruct((1, D), F32)],
        compiler_params=_params(("arbitrary",), blk),
    )(dn, hs, gain, dres)


def _rms_bwd_first(dn, hs, gain, dres):
    T, D = hs.shape
    S = T - TT

    def body(dn_ref, h_ref, g_ref, dres_ref, gx_ref, gm_ref, dg_ref):
        i = pl.program_id(0)
        dh, dgain = _rms_bwd_math(dn_ref[...], h_ref[...], g_ref[...])
        d = dres_ref[...] + dh

        @pl.when(i == 0)
        def _():
            dg_ref[...] = jnp.zeros_like(dg_ref)
            gm_ref[...] = d[PAD:, :]

        @pl.when(i > 0)
        def _():
            gx_ref[...] = d

        dg_ref[...] += dgain

    blk = _nbytes((TT, D), F32) * 4
    row = lambda i: (i, 0)
    return pl.pallas_call(
        body, name="rms_bwd_ffn1", grid=(T // TT,),
        in_specs=[pl.BlockSpec((TT, D), row), pl.BlockSpec((TT, D), row), pl.BlockSpec((1, D), lambda i: (0, 0)),
                  pl.BlockSpec((TT, D), row)],
        out_specs=[pl.BlockSpec((TT, D), lambda i: (jnp.maximum(i - 1, 0), 0)),
                   pl.BlockSpec((N_META, D), lambda i: (0, 0)), pl.BlockSpec((1, D), lambda i: (0, 0))],
        out_shape=[jax.ShapeDtypeStruct((S, D), F32), jax.ShapeDtypeStruct((N_META, D), F32),
                   jax.ShapeDtypeStruct((1, D), F32)],
        compiler_params=_params(("arbitrary",), blk),
    )(dn, hs, gain, dres)


def _final_loss(hs, gain, tgt):
    T, D = hs.shape

    def body(h_ref, g_ref, t_ref, dhs_ref, dhb_ref, loss_ref, dg_ref):
        i = pl.program_id(0)
        h = h_ref[...]
        g = g_ref[...]
        r = lax.rsqrt(jnp.mean(h * h, axis=-1, keepdims=True) + EPS)
        xh = h * r
        e = jnp.where(i > 0, xh * g - t_ref[...], 0.0)
        tile_loss = jnp.sum(jnp.sum(e * e, axis=1, keepdims=True), axis=0, keepdims=True) * (0.5 / D)
        dout = e * (1.0 / D)
        dgain = jnp.sum(dout * xh, axis=0, keepdims=True)
        dxh = dout * g
        d = r * (dxh - xh * jnp.mean(dxh * xh, axis=-1, keepdims=True))
        dhs_ref[...] = d
        dhb_ref[...] = (FFN_RES_SCALE * d).astype(BF16)

        @pl.when(i == 0)
        def _():
            loss_ref[...] = jnp.zeros_like(loss_ref)
            dg_ref[...] = jnp.zeros_like(dg_ref)

        loss_ref[...] += jnp.broadcast_to(tile_loss, loss_ref.shape)
        dg_ref[...] += dgain

    blk = _nbytes((TT, D), F32) * 3 + _nbytes((TT, D), BF16)
    row = lambda i: (i, 0)
    return pl.pallas_call(
        body, name="final_loss", grid=(T // TT,),
        in_specs=[pl.BlockSpec((TT, D), row), pl.BlockSpec((1, D), lambda i: (0, 0)),
                  pl.BlockSpec((TT, D), lambda i: (jnp.maximum(i - 1, 0), 0))],
        out_specs=[pl.BlockSpec((TT, D), row), pl.BlockSpec((TT, D), row),
                   pl.BlockSpec((1, 128), lambda i: (0, 0)), pl.BlockSpec((1, D), lambda i: (0, 0))],
        out_shape=[jax.ShapeDtypeStruct((T, D), F32), jax.ShapeDtypeStruct((T, D), BF16),
                   jax.ShapeDtypeStruct((1, 128), F32), jax.ShapeDtypeStruct((1, D), F32)],
        compiler_params=_params(("arbitrary",), blk),
    )(hs, gain, tgt)


def _tm(T):
    return _row_tile(T, 384, 128)


def _ffn_up(n, wg, wu, name):
    T, D = n.shape
    Fs = wg.shape[2]
    tm = _tm(T)

    def body(n_ref, wg_ref, wu_ref, g_ref, u_ref, a_ref):
        nn = n_ref[...]
        g = jnp.dot(nn, wg_ref[...], preferred_element_type=F32)
        u = jnp.dot(nn, wu_ref[...], preferred_element_type=F32)
        g_ref[...] = g.astype(BF16)
        u_ref[...] = u.astype(BF16)
        a_ref[...] = (jax.nn.silu(g) * u).astype(BF16)

    blk = _nbytes((tm, D), BF16) + 2 * _nbytes((D, Fs), BF16) + 3 * _nbytes((tm, Fs), BF16) + 2 * _nbytes((tm, Fs), F32)
    out = pl.BlockSpec((tm, Fs), lambda j, i: (i, j))
    shp = jax.ShapeDtypeStruct((T, N_CHIPS * Fs), BF16)
    return pl.pallas_call(
        body, name=name, grid=(N_CHIPS, T // tm),
        in_specs=[pl.BlockSpec((tm, D), lambda j, i: (i, 0)),
                  pl.BlockSpec((None, D, Fs), lambda j, i: (j, 0, 0)),
                  pl.BlockSpec((None, D, Fs), lambda j, i: (j, 0, 0))],
        out_specs=[out, out, out], out_shape=[shp, shp, shp],
        compiler_params=_params(("parallel", "parallel"), blk),
    )(n, wg, wu)


def _ffn_down(a, wd, hs, name):
    T, F = a.shape
    D = wd.shape[1]
    tm = _tm(T)
    tn = D // 2

    def body(a_ref, w_ref, h_ref, o_ref):
        o_ref[...] = h_ref[...] + FFN_RES_SCALE * jnp.dot(a_ref[...], w_ref[...], preferred_element_type=F32)

    blk = _nbytes((tm, F), BF16) + _nbytes((F, tn), BF16) + 3 * _nbytes((tm, tn), F32)
    return pl.pallas_call(
        body, name=name, grid=(D // tn, T // tm),
        in_specs=[pl.BlockSpec((tm, F), lambda n, i: (i, 0)), pl.BlockSpec((F, tn), lambda n, i: (0, n)),
                  pl.BlockSpec((tm, tn), lambda n, i: (i, n))],
        out_specs=pl.BlockSpec((tm, tn), lambda n, i: (i, n)),
        out_shape=jax.ShapeDtypeStruct((T, D), F32),
        compiler_params=_params(("parallel", "parallel"), blk),
    )(a, wd, hs)


def _mix_in(n, w, b):
    T, D = n.shape
    Ns = w.shape[2]
    tm = _tm(T)

    def body(n_ref, w_ref, b_ref, u_ref):
        u_ref[...] = jnp.dot(n_ref[...], w_ref[...], preferred_element_type=F32) + b_ref[...]

    blk = _nbytes((tm, D), BF16) + _nbytes((D, Ns), BF16) + 2 * _nbytes((tm, Ns), F32)
    return pl.pallas_call(
        body, name="mix_in", grid=(N_CHIPS, T // tm),
        in_specs=[pl.BlockSpec((tm, D), lambda j, i: (i, 0)), pl.BlockSpec((None, D, Ns), lambda j, i: (j, 0, 0)),
                  pl.BlockSpec((1, Ns), lambda j, i: (0, j))],
        out_specs=pl.BlockSpec((tm, Ns), lambda j, i: (i, j)),
        out_shape=jax.ShapeDtypeStruct((T, N_CHIPS * Ns), F32),
        compiler_params=_params(("parallel", "parallel"), blk),
    )(n, w, b)


def _mix_out(y, w, hs):
    T, D = y.shape
    tm = _tm(T)

    def body(y_ref, w_ref, h_ref, o_ref):
        o_ref[...] = h_ref[...] + jnp.dot(y_ref[...], w_ref[...], preferred_element_type=F32)

    blk = _nbytes((tm, D), BF16) + _nbytes((D, D), BF16) + 3 * _nbytes((tm, D), F32)
    return pl.pallas_call(
        body, name="mix_out", grid=(T // tm,),
        in_specs=[pl.BlockSpec((tm, D), lambda i: (i, 0)), pl.BlockSpec((D, D), lambda i: (0, 0)),
                  pl.BlockSpec((tm, D), lambda i: (i, 0))],
        out_specs=pl.BlockSpec((tm, D), lambda i: (i, 0)),
        out_shape=jax.ShapeDtypeStruct((T, D), F32),
        compiler_params=_params(("parallel",), blk),
    )(y, w, hs)


def _ffn_bwd_act(dfb, wd, g, u, name):
    T, D = dfb.shape
    Fs = wd.shape[1]
    tm = _tm(T)

    def body(d_ref, w_ref, g_ref, u_ref, dg_ref, du_ref):
        da = lax.dot_general(d_ref[...], w_ref[...], NT_DIMS, preferred_element_type=F32)
        gv = g_ref[...].astype(F32)
        uv = u_ref[...].astype(F32)
        s = _sigmoid(gv)
        du_ref[...] = (da * (gv * s)).astype(BF16)
        dg_ref[...] = (da * uv * _dsilu(gv, s)).astype(BF16)

    blk = _nbytes((tm, D), BF16) + _nbytes((Fs, D), BF16) + 4 * _nbytes((tm, Fs), BF16) + 3 * _nbytes((tm, Fs), F32)
    io = pl.BlockSpec((tm, Fs), lambda j, i: (i, j))
    shp = jax.ShapeDtypeStruct((T, N_CHIPS * Fs), BF16)
    return pl.pallas_call(
        body, name=name, grid=(N_CHIPS, T // tm),
        in_specs=[pl.BlockSpec((tm, D), lambda j, i: (i, 0)), pl.BlockSpec((None, Fs, D), lambda j, i: (j, 0, 0)), io, io],
        out_specs=[io, io], out_shape=[shp, shp],
        compiler_params=_params(("parallel", "parallel"), blk),
    )(dfb, wd, g, u)


def _nt_panel(lhs_list, w_list, name):
    T = lhs_list[0].shape[0]
    nsh, Dout, Ks = w_list[0].shape
    npair = len(lhs_list)
    tm = _tm(T)
    tn = Dout // 4 if npair * nsh * Ks > 4096 else Dout // 2

    def body(*refs):
        l_refs, w_refs, o_ref = refs[:npair], refs[npair:2 * npair], refs[2 * npair]
        acc = None
        for p in range(npair):
            for j in range(nsh):
                part = lax.dot_general(l_refs[p][:, j * Ks:(j + 1) * Ks], w_refs[p][j], NT_DIMS,
                                       preferred_element_type=F32)
                acc = part if acc is None else acc + part
        o_ref[...] = acc

    blk = npair * (_nbytes((tm, nsh * Ks), BF16) + _nbytes((nsh, tn, Ks), BF16)) + 3 * _nbytes((tm, tn), F32)
    return pl.pallas_call(
        body, name=name, grid=(Dout // tn, T // tm),
        in_specs=[pl.BlockSpec((tm, nsh * Ks), lambda n, i: (i, 0))] * npair
                 + [pl.BlockSpec((nsh, tn, Ks), lambda n, i: (0, n, 0))] * npair,
        out_specs=pl.BlockSpec((tm, tn), lambda n, i: (i, n)),
        out_shape=jax.ShapeDtypeStruct((T, Dout), F32),
        compiler_params=_params(("parallel", "parallel"), blk),
    )(*lhs_list, *w_list)


def _tn_call(name, grid, lhs, lhs_spec, rhs_list, rhs_specs, out_shapes, out_specs, blk):
    nr = len(rhs_list)

    def body(*refs):
        l_ref, r_refs, o_refs = refs[0], refs[1:1 + nr], refs[1 + nr:]
        k = pl.program_id(len(grid) - 1)
        lv = l_ref[...]
        for q in range(nr):
            part = lax.dot_general(lv, r_refs[q][...], TN_DIMS, preferred_element_type=F32)
            part = part.reshape(o_refs[q].shape)

            @pl.when(k == 0)
            def _(o=o_refs[q], part=part):
                o[...] = part

            @pl.when(k > 0)
            def _(o=o_refs[q], part=part):
                o[...] += part

    return pl.pallas_call(
        body, name=name, grid=grid, in_specs=[lhs_spec] + rhs_specs, out_specs=out_specs, out_shape=out_shapes,
        compiler_params=_params(("parallel",) * (len(grid) - 1) + ("arbitrary",), blk),
    )(lhs, *rhs_list)


def _tk(T):
    return _row_tile(T, 1408, 128)


def _wgrad_cols(n, rhs_list, name):
    T, D = n.shape
    Ns = rhs_list[0].shape[1] // N_CHIPS
    tk = _tk(T)
    nr = len(rhs_list)
    blk = _nbytes((tk, D // 2), BF16) + nr * (_nbytes((tk, Ns), BF16) + 2 * _nbytes((D // 2, Ns), F32))
    return _tn_call(
        name, (N_CHIPS, 2, T // tk), n, pl.BlockSpec((tk, D // 2), lambda j, m, k: (k, m)),
        rhs_list, [pl.BlockSpec((tk, Ns), lambda j, m, k: (k, j))] * nr,
        [jax.ShapeDtypeStruct((N_CHIPS, 2, D // 2, Ns), F32)] * nr,
        [pl.BlockSpec((None, None, D // 2, Ns), lambda j, m, k: (j, m, 0, 0))] * nr, blk)


def _wgrad_down(a, dfb, name):
    T, F = a.shape
    D = dfb.shape[1]
    Fs = F // N_CHIPS
    tk = _tk(T)
    tn = D // 2
    blk = _nbytes((tk, Fs), BF16) + _nbytes((tk, tn), BF16) + 2 * _nbytes((Fs, tn), F32)
    return _tn_call(
        name, (N_CHIPS, D // tn, T // tk), a, pl.BlockSpec((tk, Fs), lambda j, n, k: (k, j)),
        [dfb], [pl.BlockSpec((tk, tn), lambda j, n, k: (k, n))],
        [jax.ShapeDtypeStruct((N_CHIPS, 2, Fs // 2, D), F32)],
        [pl.BlockSpec((None, 2, Fs // 2, tn), lambda j, n, k: (j, 0, 0, n))], blk)[0]


def _wgrad_out(y, dmb):
    T, D = y.shape
    tk = _tk(T)
    tn = D // 2
    rows = D // (2 * N_CHIPS)
    blk = _nbytes((tk, D // 2), BF16) + _nbytes((tk, tn), BF16) + 2 * _nbytes((D // 2, tn), F32)
    return _tn_call(
        "wgrad_w_out", (2, D // tn, T // tk), y, pl.BlockSpec((tk, D // 2), lambda m, n, k: (k, m)),
        [dmb], [pl.BlockSpec((tk, tn), lambda m, n, k: (k, n))],
        [jax.ShapeDtypeStruct((N_CHIPS, 2, rows, D), F32)],
        [pl.BlockSpec((2, 2, rows, tn), lambda m, n, k: (m, 0, 0, n))], blk)[0]


def _row_masks(i, last):
    rows = i * TT + lax.broadcasted_iota(jnp.int32, (TT, 1), 0)
    prows = i * TT - HALO + lax.broadcasted_iota(jnp.int32, (HALO, 1), 0)
    return rows >= PAD, (prows >= PAD) & (i > 0), i < last


def _conv_inputs(u, up, mask_c, mask_p, zbuf, pbuf, C1):
    b, c, v, a, g = (u[:, k * C1:(k + 1) * C1] for k in range(5))
    cp, vp, ap, gp = (up[:, k * C1:(k + 1) * C1] for k in range(1, 5))
    sg = _sigmoid(g)
    pbuf[0:HALO, :] = jnp.where(mask_p, cp * vp, 0.0)
    pbuf[HALO:, :] = jnp.where(mask_c, c * v, 0.0)
    zbuf[0:HALO, :] = jnp.where(mask_p, ap * _sigmoid(gp), 0.0)
    zbuf[HALO:, :] = jnp.where(mask_c, a * sg, 0.0)
    return b, c, v, a, sg


def _causal_conv(w_ref, buf):
    K = w_ref.shape[0]
    acc = None
    for k in range(K):
        lo = HALO - (K - 1) + k
        term = w_ref[k:k + 1, :] * buf[lo:lo + TT, :]
        acc = term if acc is None else acc + term
    return acc


def _anticausal_conv(w_ref, buf):
    K = w_ref.shape[0]
    acc = None
    for k in range(K):
        lo = K - 1 - k
        term = w_ref[k:k + 1, :] * buf[lo:lo + TT, :]
        acc = term if acc is None else acc + term
    return acc


def _conv_weight_sums(dw_ref, dy, buf):
    K = dw_ref.shape[0]
    for k in range(K):
        lo = HALO - (K - 1) + k
        dw_ref[k:k + 1, :] += jnp.sum(dy * buf[lo:lo + TT, :], axis=0, keepdims=True)


def _layernorm_stats(z1):
    mu = jnp.mean(z1, axis=-1, keepdims=True)
    zc = z1 - mu
    rs = lax.rsqrt(jnp.mean(zc * zc, axis=-1, keepdims=True) + EPS)
    return zc * rs, rs


def _mixer_specs(T, DIN, C1, ksc, kcf):
    cur = pl.BlockSpec((TT, DIN), lambda i: (i, 0))
    prev = pl.BlockSpec((HALO, DIN), lambda i: (jnp.maximum(i * (TT // HALO) - 1, 0), 0))
    full = lambda r: pl.BlockSpec((r, C1), lambda i: (0, 0))
    return cur, prev, [full(ksc), full(kcf), full(1), full(1), full(1)]


def _mix_conv_fwd(u, wsc, wcf, bcf, lg, lb):
    T, DIN = u.shape
    C1 = DIN // 5
    last = T // TT - 1

    def body(u_ref, up_ref, wsc_ref, wcf_ref, bcf_ref, lg_ref, lb_ref, y_ref, zbuf, pbuf):
        i = pl.program_id(0)
        mask_c, mask_p, _ = _row_masks(i, last)
        b, _, _, _, _ = _conv_inputs(u_ref[...], up_ref[...], mask_c, mask_p, zbuf, pbuf, C1)
        cs = _causal_conv(wsc_ref, pbuf)
        z1 = _causal_conv(wcf_ref, zbuf) + bcf_ref[...]
        zh, _ = _layernorm_stats(z1)
        ln = zh * lg_ref[...] + lb_ref[...]
        y_ref[:, 0:C1] = jnp.where(mask_c, b * cs, 0.0).astype(BF16)
        y_ref[:, C1:] = jnp.where(mask_c, jax.nn.silu(ln), 0.0).astype(BF16)

    cur, prev, small = _mixer_specs(T, DIN, C1, wsc.shape[0], wcf.shape[0])
    blk = _nbytes((TT + HALO, DIN), F32) + _nbytes((TT, 2 * C1), BF16) + 12 * _nbytes((TT + HALO, C1), F32)
    return pl.pallas_call(
        body, name="mix_conv_fwd", grid=(T // TT,),
        in_specs=[cur, prev] + small,
        out_specs=pl.BlockSpec((TT, 2 * C1), lambda i: (i, 0)),
        out_shape=jax.ShapeDtypeStruct((T, 2 * C1), BF16),
        scratch_shapes=[pltpu.VMEM((TT + HALO, C1), F32), pltpu.VMEM((TT + HALO, C1), F32)],
        compiler_params=_params(("arbitrary",), blk),
    )(u, u, wsc, wcf, bcf, lg, lb)


def _mix_conv_bwd1(u, dy, wsc, wcf, bcf, lg, lb):
    T, DIN = u.shape
    C1 = DIN // 5
    last = T // TT - 1

    def body(u_ref, up_ref, dy_ref, wsc_ref, wcf_ref, bcf_ref, lg_ref, lb_ref,
             dz1_ref, dcs_ref, db_ref, dlg_ref, dlb_ref, dbcf_ref, zbuf, pbuf):
        i = pl.program_id(0)
        mask_c, mask_p, _ = _row_masks(i, last)
        b, _, _, _, _ = _conv_inputs(u_ref[...], up_ref[...], mask_c, mask_p, zbuf, pbuf, C1)
        cs = _causal_conv(wsc_ref, pbuf)
        z1 = _causal_conv(wcf_ref, zbuf) + bcf_ref[...]
        zh, rs = _layernorm_stats(z1)
        ln = zh * lg_ref[...] + lb_ref[...]
        dy = dy_ref[...]
        dysc = jnp.where(mask_c, dy[:, 0:C1], 0.0)
        dycf = jnp.where(mask_c, dy[:, C1:], 0.0)
        db_ref[...] = (dysc * cs).astype(BF16)
        dcs_ref[...] = dysc * b
        dl = dycf * _dsilu(ln, _sigmoid(ln))
        dzh = dl * lg_ref[...]
        dz1 = rs * (dzh - jnp.mean(dzh, axis=-1, keepdims=True) - zh * jnp.mean(dzh * zh, axis=-1, keepdims=True))
        dz1_ref[...] = dz1

        @pl.when(i == 0)
        def _():
            dlg_ref[...] = jnp.zeros_like(dlg_ref)
            dlb_ref[...] = jnp.zeros_like(dlb_ref)
            dbcf_ref[...] = jnp.zeros_like(dbcf_ref)

        dlg_ref[...] += jnp.sum(dl * zh, axis=0, keepdims=True)
        dlb_ref[...] += jnp.sum(dl, axis=0, keepdims=True)
        dbcf_ref[...] += jnp.sum(dz1, axis=0, keepdims=True)

    cur, prev, small = _mixer_specs(T, DIN, C1, wsc.shape[0], wcf.shape[0])
    tile = lambda: pl.BlockSpec((TT, C1), lambda i: (i, 0))
    vec = lambda: pl.BlockSpec((1, C1), lambda i: (0, 0))
    blk = _nbytes((TT + HALO, DIN), F32) + 4 * _nbytes((TT, C1), F32) + 16 * _nbytes((TT + HALO, C1), F32)
    return pl.pallas_call(
        body, name="mix_conv_bwd1", grid=(T // TT,),
        in_specs=[cur, prev, pl.BlockSpec((TT, 2 * C1), lambda i: (i, 0))] + small,
        out_specs=[tile(), tile(), tile(), vec(), vec(), vec()],
        out_shape=[jax.ShapeDtypeStruct((T, C1), F32), jax.ShapeDtypeStruct((T, C1), F32),
                   jax.ShapeDtypeStruct((T, C1), BF16)] + [jax.ShapeDtypeStruct((1, C1), F32)] * 3,
        scratch_shapes=[pltpu.VMEM((TT + HALO, C1), F32), pltpu.VMEM((TT + HALO, C1), F32)],
        compiler_params=_params(("arbitrary",), blk),
    )(u, u, dy, wsc, wcf, bcf, lg, lb)


def _mix_conv_bwd2(u, dz1, dcs, db, wsc, wcf):
    T, DIN = u.shape
    C1 = DIN // 5
    last = T // TT - 1
    ksc, kcf = wsc.shape[0], wcf.shape[0]

    def body(u_ref, up_ref, dz_ref, dzn_ref, dc_ref, dcn_ref, db_ref, wsc_ref, wcf_ref,
             du_ref, dbin_ref, dwsc_ref, dwcf_ref, zbuf, pbuf, dzbuf, dcbuf):
        i = pl.program_id(0)
        mask_c, mask_p, has_next = _row_masks(i, last)
        _, c, v, a, sg = _conv_inputs(u_ref[...], up_ref[...], mask_c, mask_p, zbuf, pbuf, C1)
        dz1 = dz_ref[...]
        dcs = dc_ref[...]
        dzbuf[0:TT, :] = dz1
        dzbuf[TT:, :] = jnp.where(has_next, dzn_ref[...], 0.0)
        dcbuf[0:TT, :] = dcs
        dcbuf[TT:, :] = jnp.where(has_next, dcn_ref[...], 0.0)

        @pl.when(i == 0)
        def _():
            dbin_ref[...] = jnp.zeros_like(dbin_ref)
            dwsc_ref[...] = jnp.zeros_like(dwsc_ref)
            dwcf_ref[...] = jnp.zeros_like(dwcf_ref)

        _conv_weight_sums(dwcf_ref, dz1, zbuf)
        _conv_weight_sums(dwsc_ref, dcs, pbuf)
        dz0 = jnp.where(mask_c, _anticausal_conv(wcf_ref, dzbuf), 0.0)
        dp = jnp.where(mask_c, _anticausal_conv(wsc_ref, dcbuf), 0.0)
        parts = (db_ref[...].astype(F32), dp * v, dp * c, dz0 * sg, dz0 * a * sg * (1.0 - sg))
        for k, part in enumerate(parts):
            du_ref[:, k * C1:(k + 1) * C1] = part.astype(BF16)
            dbin_ref[:, k * C1:(k + 1) * C1] += jnp.sum(part, axis=0, keepdims=True)

    cur, prev, small = _mixer_specs(T, DIN, C1, ksc, kcf)
    tile = lambda: pl.BlockSpec((TT, C1), lambda i: (i, 0))
    nxt = lambda: pl.BlockSpec((HALO, C1), lambda i: (jnp.minimum((i + 1) * (TT // HALO), T // HALO - 1), 0))
    blk = (_nbytes((TT + HALO, DIN), F32) + _nbytes((TT, DIN), BF16) + 5 * _nbytes((TT, C1), F32)
           + 16 * _nbytes((TT + HALO, C1), F32))
    buf = lambda: pltpu.VMEM((TT + HALO, C1), F32)
    return pl.pallas_call(
        body, name="mix_conv_bwd2", grid=(T // TT,),
        in_specs=[cur, prev, tile(), nxt(), tile(), nxt(), tile(), small[0], small[1]],
        out_specs=[pl.BlockSpec((TT, DIN), lambda i: (i, 0)), pl.BlockSpec((1, DIN), lambda i: (0, 0)),
                   pl.BlockSpec((ksc, C1), lambda i: (0, 0)), pl.BlockSpec((kcf, C1), lambda i: (0, 0))],
        out_shape=[jax.ShapeDtypeStruct((T, DIN), BF16), jax.ShapeDtypeStruct((1, DIN), F32),
                   jax.ShapeDtypeStruct((ksc, C1), F32), jax.ShapeDtypeStruct((kcf, C1), F32)],
        scratch_shapes=[buf(), buf(), buf(), buf()],
        compiler_params=_params(("arbitrary",), blk),
    )(u, u, dz1, dz1, dcs, dcs, db, wsc, wcf)


def _place():
    x, y, c = lax.axis_index("x"), lax.axis_index("y"), lax.axis_index("c")
    chips = [(1 - x, y), (x, 1 - y), (1 - x, 1 - y)]
    return x, y, c, chips


ANY = pl.BlockSpec(memory_space=pl.ANY)


def _cast_own_block(place, w, name):
    R, C = w.shape
    tr = _row_tile(R // 2, 256, 16)
    nblk = R // 2 // tr

    def body(place_ref, w_ref, o_ref):
        o_ref[...] = w_ref[...].astype(BF16)

    return pl.pallas_call(
        body, name=name,
        grid_spec=pltpu.PrefetchScalarGridSpec(
            num_scalar_prefetch=1, grid=(2, nblk),
            in_specs=[pl.BlockSpec((tr, C), lambda h, i, p: (h * nblk + i, 0))],
            out_specs=pl.BlockSpec((None, None, tr, C), lambda h, i, p: (p[0], h, i, 0))),
        out_shape=jax.ShapeDtypeStruct((N_CHIPS, 2, R // 2, C), BF16),
        compiler_params=_params(("parallel", "parallel"), _nbytes((tr, C), F32) + _nbytes((tr, C), BF16)),
    )(place, w)


def _gather_weights(bufs):
    nw = len(bufs)

    def body(*refs):
        o_refs = refs[nw:2 * nw]
        send, recv = refs[2 * nw:]
        x, y, c, chips = _place()
        s = 2 * x + y
        sib = (x, y, 1 - c)

        def remote(w, k, blk, half, to):
            ref = o_refs[w].at[blk, half]
            return pltpu.make_async_remote_copy(src_ref=ref, dst_ref=ref, send_sem=send.at[6 * w + k],
                                                recv_sem=recv.at[6 * w + k], device_id=to, device_id_type=MESH)

        sends = []
        for w in range(nw):
            for r, (tx, ty) in enumerate(chips):
                cp = remote(w, r, s, c, (tx, ty, c))
                cp.start()
                sends.append(cp)
        for w in range(nw):
            for r, (tx, ty) in enumerate(chips):
                sr = 2 * tx + ty
                remote(w, r, sr, c, (tx, ty, c)).wait_recv()
                cp = remote(w, 3 + r, sr, c, sib)
                cp.start()
                sends.append(cp)
        for w in range(nw):
            for r, (tx, ty) in enumerate(chips):
                remote(w, 3 + r, 2 * tx + ty, 1 - c, sib).wait_recv()
        for cp in sends:
            cp.wait_send()

    return pl.pallas_call(
        body, name="gather_weights", in_specs=[ANY] * nw, out_specs=[ANY] * nw,
        out_shape=[jax.ShapeDtypeStruct(b.shape, b.dtype) for b in bufs],
        input_output_aliases={w: w for w in range(nw)},
        scratch_shapes=[pltpu.SemaphoreType.DMA((6 * nw,)), pltpu.SemaphoreType.DMA((6 * nw,))],
    )(*bufs)


def _pair_exchange(gs):
    nw = len(gs)

    def body(*refs):
        g_refs, o_refs = refs[:nw], refs[nw:2 * nw]
        send, recv = refs[2 * nw:]
        x, y, c, _ = _place()
        sib = (x, y, 1 - c)
        copies = []
        for w in range(nw):
            for j in range(N_CHIPS):
                cp = pltpu.make_async_remote_copy(
                    src_ref=g_refs[w].at[j, 1 - c], dst_ref=o_refs[w].at[j], send_sem=send.at[N_CHIPS * w + j],
                    recv_sem=recv.at[N_CHIPS * w + j], device_id=sib, device_id_type=MESH)
                cp.start()
                copies.append(cp)
        for cp in copies:
            cp.wait()

    return pl.pallas_call(
        body, name="pair_exchange", in_specs=[ANY] * nw, out_specs=[ANY] * nw,
        out_shape=[jax.ShapeDtypeStruct((N_CHIPS,) + g.shape[2:], F32) for g in gs],
        scratch_shapes=[pltpu.SemaphoreType.DMA((N_CHIPS * nw,)), pltpu.SemaphoreType.DMA((N_CHIPS * nw,))],
    )(*gs)


def _chip_exchange(qs):
    nw = len(qs)

    def body(*refs):
        q_refs, o_refs = refs[:nw], refs[nw:2 * nw]
        send, recv = refs[2 * nw:]
        x, y, c, chips = _place()
        copies = []
        for w in range(nw):
            for r, (tx, ty) in enumerate(chips):
                cp = pltpu.make_async_remote_copy(
                    src_ref=q_refs[w].at[2 * tx + ty], dst_ref=o_refs[w].at[r], send_sem=send.at[3 * w + r],
                    recv_sem=recv.at[3 * w + r], device_id=(tx, ty, c), device_id_type=MESH)
                cp.start()
                copies.append(cp)
        for cp in copies:
            cp.wait()

    return pl.pallas_call(
        body, name="chip_exchange", in_specs=[ANY] * nw, out_specs=[ANY] * nw,
        out_shape=[jax.ShapeDtypeStruct((3,) + q.shape[1:], q.dtype) for q in qs],
        scratch_shapes=[pltpu.SemaphoreType.DMA((3 * nw,)), pltpu.SemaphoreType.DMA((3 * nw,))],
    )(*qs)


def _half_exchange(hs):
    nw = len(hs)

    def body(*refs):
        o_refs = refs[nw:2 * nw]
        send, recv = refs[2 * nw:]
        x, y, c, _ = _place()
        sib = (x, y, 1 - c)
        copies = []
        for w in range(nw):
            cp = pltpu.make_async_remote_copy(src_ref=o_refs[w].at[c], dst_ref=o_refs[w].at[c], send_sem=send.at[w],
                                              recv_sem=recv.at[w], device_id=sib, device_id_type=MESH)
            cp.start()
            copies.append(cp)
        for w, cp in enumerate(copies):
            cp.wait_send()
            pltpu.make_async_remote_copy(src_ref=o_refs[w].at[c], dst_ref=o_refs[w].at[1 - c], send_sem=send.at[w],
                                         recv_sem=recv.at[w], device_id=sib, device_id_type=MESH).wait_recv()

    return pl.pallas_call(
        body, name="half_exchange", in_specs=[ANY] * nw, out_specs=[ANY] * nw,
        out_shape=[jax.ShapeDtypeStruct(h.shape, F32) for h in hs],
        input_output_aliases={w: w for w in range(nw)},
        scratch_shapes=[pltpu.SemaphoreType.DMA((nw,)), pltpu.SemaphoreType.DMA((nw,))],
    )(*hs)


def _share_small(v, reduce, name):
    R, C = v.shape

    def body(v_ref, o_ref, *scratch):
        if reduce:
            all_ref, send, recv, lsem = scratch
        else:
            all_ref = o_ref
            send, recv, lsem = scratch
        x, y, c, _ = _place()
        me = 4 * x + 2 * y + c
        loc = pltpu.make_async_copy(v_ref, all_ref.at[me], lsem)
        loc.start()
        copies = []
        for k in range(1, N_DEV):
            kx, ky, kc = (k >> 2) & 1, (k >> 1) & 1, k & 1
            peer = (x ^ kx, y ^ ky, c ^ kc)
            cp = pltpu.make_async_remote_copy(src_ref=v_ref, dst_ref=all_ref.at[me], send_sem=send.at[k - 1],
                                              recv_sem=recv.at[k - 1], device_id=peer, device_id_type=MESH)
            cp.start()
            copies.append(cp)
        for k in range(1, N_DEV):
            kx, ky, kc = (k >> 2) & 1, (k >> 1) & 1, k & 1
            src = 4 * (x ^ kx) + 2 * (y ^ ky) + (c ^ kc)
            pltpu.make_async_remote_copy(src_ref=v_ref, dst_ref=all_ref.at[src], send_sem=send.at[k - 1],
                                         recv_sem=recv.at[k - 1], device_id=(x, y, c), device_id_type=MESH).wait_recv()
        for cp in copies:
            cp.wait_send()
        loc.wait()
        if reduce:
            total = all_ref[0]
            for d in range(1, N_DEV):
                total = total + all_ref[d]
            o_ref[...] = total

    vm = pl.BlockSpec(memory_space=pltpu.VMEM)
    sems = [pltpu.SemaphoreType.DMA((N_DEV - 1,)), pltpu.SemaphoreType.DMA((N_DEV - 1,)), pltpu.SemaphoreType.DMA]
    if reduce:
        out_shape = jax.ShapeDtypeStruct((R, C), F32)
        scratch = [pltpu.VMEM((N_DEV, R, C), F32)] + sems
    else:
        out_shape = jax.ShapeDtypeStruct((N_DEV, R, C), F32)
        scratch = sems
    return pl.pallas_call(
        body, name=name, in_specs=[vm], out_specs=vm, out_shape=out_shape, scratch_shapes=scratch,
        compiler_params=pltpu.CompilerParams(vmem_limit_bytes=int(min(4 * N_DEV * R * C * 4 + 2 ** 24, 2 ** 25 + 2 ** 24))),
    )(v)


def _pair_sum(place, g, rb, name):
    _, _, Rh, C = g.shape
    tr = _row_tile(Rh, 256, 16)

    def body(place_ref, g_ref, r_ref, q_ref):
        q_ref[...] = (g_ref[...] + r_ref[...]).astype(BF16)

    blk = 2 * _nbytes((tr, C), F32) + _nbytes((tr, C), BF16)
    return pl.pallas_call(
        body, name=name,
        grid_spec=pltpu.PrefetchScalarGridSpec(
            num_scalar_prefetch=1, grid=(N_CHIPS, Rh // tr),
            in_specs=[pl.BlockSpec((None, None, tr, C), lambda j, i, p: (j, p[1], i, 0)),
                      pl.BlockSpec((None, tr, C), lambda j, i, p: (j, i, 0))],
            out_specs=pl.BlockSpec((None, tr, C), lambda j, i, p: (j, i, 0))),
        out_shape=jax.ShapeDtypeStruct((N_CHIPS, Rh, C), BF16),
        compiler_params=_params(("parallel", "parallel"), blk),
    )(place, g, rb)


def _chip_sum(place, g, rb, rc, name):
    _, _, Rh, C = g.shape
    tr = _row_tile(Rh, 256, 16)

    def body(place_ref, g_ref, r_ref, rc_ref, o_ref):
        total = g_ref[...] + r_ref[...]
        for r in range(3):
            total = total + rc_ref[r].astype(F32)
        o_ref[...] = total

    blk = 3 * _nbytes((tr, C), F32) + 3 * _nbytes((tr, C), BF16)
    return pl.pallas_call(
        body, name=name,
        grid_spec=pltpu.PrefetchScalarGridSpec(
            num_scalar_prefetch=1, grid=(Rh // tr,),
            in_specs=[pl.BlockSpec((None, None, tr, C), lambda i, p: (p[0], p[1], i, 0)),
                      pl.BlockSpec((None, tr, C), lambda i, p: (p[0], i, 0)),
                      pl.BlockSpec((3, tr, C), lambda i, p: (0, i, 0))],
            out_specs=pl.BlockSpec((None, tr, C), lambda i, p: (p[1], i, 0))),
        out_shape=jax.ShapeDtypeStruct((2, Rh, C), F32),
        compiler_params=_params(("parallel",), blk),
    )(place, g, rb, rc)


def _adamw_math(w, g, m, v):
    m = ADAM_B1 * m + (1.0 - ADAM_B1) * g
    v = ADAM_B2 * v + (1.0 - ADAM_B2) * jnp.square(g)
    m_hat = m / (1.0 - ADAM_B1 ** ADAM_STEP)
    v_hat = v / (1.0 - ADAM_B2 ** ADAM_STEP)
    delta = -ADAM_LR * (m_hat / (jnp.sqrt(v_hat) + ADAM_EPS) + ADAM_WD * w)
    return delta, m, v


def _adamw(w, g, m, v, name):
    R, C = w.shape
    tr = _row_tile(R, 256)

    def body(w_ref, g_ref, m_ref, v_ref, d_ref, nm_ref, nv_ref):
        d, nm, nv = _adamw_math(w_ref[...], g_ref[...], m_ref[...], v_ref[...])
        d_ref[...] = d
        nm_ref[...] = nm
        nv_ref[...] = nv

    spec = pl.BlockSpec((tr, C), lambda i: (i, 0))
    shp = jax.ShapeDtypeStruct((R, C), F32)
    return pl.pallas_call(
        body, name=name, grid=(R // tr,), in_specs=[spec] * 4, out_specs=[spec] * 3, out_shape=[shp] * 3,
        compiler_params=_params(("parallel",), 7 * _nbytes((tr, C), F32)),
    )(w, g, m, v)


def _adamw_small(ws, gs, ms, vs):
    n = len(ws)

    def body(*refs):
        for k in range(n):
            w_ref, g_ref, m_ref, v_ref = (refs[q * n + k] for q in range(4))
            d, nm, nv = _adamw_math(w_ref[...], g_ref[...], m_ref[...], v_ref[...])
            refs[4 * n + k][...] = d
            refs[5 * n + k][...] = nm
            refs[6 * n + k][...] = nv

    vm = pl.BlockSpec(memory_space=pltpu.VMEM)
    shapes = [jax.ShapeDtypeStruct(w.shape, F32) for w in ws]
    outs = pl.pallas_call(
        body, name="adamw_small", in_specs=[vm] * (4 * n), out_specs=[vm] * (3 * n), out_shape=shapes * 3,
    )(*ws, *gs, *ms, *vs)
    return outs[:n], outs[n:2 * n], outs[2 * n:]


def _pad_rows(a, rows):
    return jnp.pad(a, ((0, rows - a.shape[0]), (0, 0)))


def kernel(x, meta_tokens, ffn1_norm, ffn1_w_gate, ffn1_w_up, ffn1_w_down, mix_norm, w_in, b_in, conv_sc_w, conv_cf_w, conv_cf_b, ln_cf_g, ln_cf_b, w_out, ffn2_norm, ffn2_w_gate, ffn2_w_up, ffn2_w_down, final_norm, loss_target, m_meta_tokens, m_ffn1_norm, m_ffn1_w_gate, m_ffn1_w_up, m_ffn1_w_down, m_mix_norm, m_w_in, m_b_in, m_conv_sc_w, m_conv_cf_w, m_conv_cf_b, m_ln_cf_g, m_ln_cf_b, m_w_out, m_ffn2_norm, m_ffn2_w_gate, m_ffn2_w_up, m_ffn2_w_down, m_final_norm, v_meta_tokens, v_ffn1_norm, v_ffn1_w_gate, v_ffn1_w_up, v_ffn1_w_down, v_mix_norm, v_w_in, v_b_in, v_conv_sc_w, v_conv_cf_w, v_conv_cf_b, v_ln_cf_g, v_ln_cf_b, v_w_out, v_ffn2_norm, v_ffn2_w_gate, v_ffn2_w_up, v_ffn2_w_down, v_final_norm):
    xi, yi, ci = lax.axis_index("x"), lax.axis_index("y"), lax.axis_index("c")
    chip = 2 * xi + yi
    place = jnp.stack([chip, ci]).astype(jnp.int32)

    x2 = x[0]
    tgt = loss_target[0]
    S, D = x2.shape
    C1 = D // 2
    cs = conv_sc_w.shape[2]
    ksc, kcf = conv_sc_w.shape[1], conv_cf_w.shape[1]
    ms = meta_tokens.shape[1]

    rows_small = N_META + 8 + 32
    assert ksc <= 8 and kcf <= 32 and cs <= ms
    pack = jnp.concatenate([
        meta_tokens,
        jnp.pad(conv_sc_w[0], ((0, 8 - ksc), (0, ms - cs))),
        jnp.pad(conv_cf_w[0], ((0, 32 - kcf), (0, ms - cs)))], axis=0)
    everyone = _share_small(pack, False, "share_params")[0::2]
    meta_full = jnp.transpose(everyone[:, :N_META, :], (1, 0, 2)).reshape(N_META, D)
    wsc_full = jnp.transpose(everyone[:, N_META:N_META + ksc, :cs], (1, 0, 2)).reshape(ksc, C1)
    wcf_full = jnp.transpose(everyone[:, N_META + 8:N_META + 8 + kcf, :cs], (1, 0, 2)).reshape(kcf, C1)

    big = [ffn1_w_gate[0], ffn1_w_up[0], ffn1_w_down[0], w_in[0], w_out[0], ffn2_w_gate[0], ffn2_w_up[0], ffn2_w_down[0]]
    names = ["ffn1_w_gate", "ffn1_w_up", "ffn1_w_down", "w_in", "w_out", "ffn2_w_gate", "ffn2_w_up", "ffn2_w_down"]
    gathered = _gather_weights([_cast_own_block(place, w, "cast_" + nm) for w, nm in zip(big, names)])
    full = [g.reshape(N_CHIPS, 2 * g.shape[2], g.shape[3]) for g in gathered]
    wg1, wu1, wd1, win, wout, wg2, wu2, wd2 = full
    F = N_CHIPS * wd1.shape[1]

    hs0, n1 = _embed_rms(x2, meta_full, ffn1_norm)
    g1, u1, a1 = _ffn_up(n1, wg1, wu1, "ffn1_up")
    hs1 = _ffn_down(a1, wd1.reshape(F, D), hs0, "ffn1_down")
    n2 = _rms(hs1, mix_norm, "rms_mix")
    u = _mix_in(n2, win, b_in)
    y = _mix_conv_fwd(u, wsc_full, wcf_full, conv_cf_b, ln_cf_g, ln_cf_b)
    hs2 = _mix_out(y, wout.reshape(D, D), hs1)
    n3 = _rms(hs2, ffn2_norm, "rms_ffn2")
    g2, u2, a2 = _ffn_up(n3, wg2, wu2, "ffn2_up")
    hs3 = _ffn_down(a2, wd2.reshape(F, D), hs2, "ffn2_down")

    dhs3, df2, loss_row, d_final = _final_loss(hs3, final_norm.reshape(1, D), tgt)

    dg2, du2 = _ffn_bwd_act(df2, wd2, g2, u2, "ffn2_bwd_act")
    gw_d2 = _wgrad_down(a2, df2, "wgrad_ffn2_down")
    gw_g2 = _wgrad_cols(n3, [dg2], "wgrad_ffn2_gate")[0]
    gw_u2 = _wgrad_cols(n3, [du2], "wgrad_ffn2_up")[0]
    dn3 = _nt_panel([dg2, du2], [wg2, wu2], "ffn2_bwd_in")
    dhs2, dm, d_ffn2 = _rms_bwd(dn3, hs2, ffn2_norm, dhs3, 1.0, "rms_bwd_ffn2")

    dy = _nt_panel([dm], [wout.reshape(1, D, D)], "mix_bwd_out")
    gw_out = _wgrad_out(y, dm)
    dz1, dcs, db, d_lg, d_lb, d_bcf = _mix_conv_bwd1(u, dy, wsc_full, wcf_full, conv_cf_b, ln_cf_g, ln_cf_b)
    du, d_bin, d_wsc, d_wcf = _mix_conv_bwd2(u, dz1, dcs, db, wsc_full, wcf_full)
    gw_in = _wgrad_cols(n2, [du], "wgrad_w_in")[0]
    dn2 = _nt_panel([du], [win], "mix_bwd_in")
    dhs1, df1, d_mix = _rms_bwd(dn2, hs1, mix_norm, dhs2, FFN_RES_SCALE, "rms_bwd_mix")

    dg1, du1 = _ffn_bwd_act(df1, wd1, g1, u1, "ffn1_bwd_act")
    gw_d1 = _wgrad_down(a1, df1, "wgrad_ffn1_down")
    gw_g1 = _wgrad_cols(n1, [dg1], "wgrad_ffn1_gate")[0]
    gw_u1 = _wgrad_cols(n1, [du1], "wgrad_ffn1_up")[0]
    dn1 = _nt_panel([dg1, du1], [wg1, wu1], "ffn1_bwd_in")
    grad_x, d_meta, d_ffn1 = _rms_bwd_first(dn1, hs0, ffn1_norm, dhs1)

    grads = [gw_g1, gw_u1, gw_d1, gw_in, gw_out, gw_g2, gw_u2, gw_d2]
    from_sibling = _pair_exchange(grads)
    pair_sums = [_pair_sum(place, g, rb, "pair_sum_" + nm) for g, rb, nm in zip(grads, from_sibling, names)]
    from_chips = _chip_exchange(pair_sums)
    mine = [_chip_sum(place, g, rb, rc, "chip_sum_" + nm)
            for g, rb, rc, nm in zip(grads, from_sibling, from_chips, names)]
    whole = _half_exchange(mine)
    big_m = [m_ffn1_w_gate, m_ffn1_w_up, m_ffn1_w_down, m_w_in, m_w_out, m_ffn2_w_gate, m_ffn2_w_up, m_ffn2_w_down]
    big_v = [v_ffn1_w_gate, v_ffn1_w_up, v_ffn1_w_down, v_w_in, v_w_out, v_ffn2_w_gate, v_ffn2_w_up, v_ffn2_w_down]
    big_out = {}
    for nm, w, g, m, v in zip(names, big, whole, big_m, big_v):
        g2d = g.reshape(w.shape)
        d, nm_, nv_ = _adamw(w, g2d, m[0], v[0], "adamw_" + nm)
        big_out[nm] = tuple(t[None] for t in (g2d, d, nm_, nv_))

    W = C1
    rows = lambda a: a.reshape(-1, W)
    parts = [rows(d_ffn1), rows(d_mix), rows(d_ffn2), rows(d_final), rows(d_bin), d_bcf, d_lg, d_lb,
             d_wsc, d_wcf, rows(d_meta), jnp.broadcast_to(loss_row[:, :1], (1, W))]
    sizes = [p.shape[0] for p in parts]
    total_rows = sum(sizes)
    packed = _pad_rows(jnp.concatenate(parts, axis=0), -(-total_rows // 8) * 8)
    summed = _share_small(packed, True, "sum_small")
    offs = [0]
    for n in sizes:
        offs.append(offs[-1] + n)
    piece = lambda k: summed[offs[k]:offs[k + 1]]
    loss = piece(11)[0, 0]
    g_ffn1, g_mix, g_ffn2 = (piece(k).reshape(1, D) for k in range(3))
    g_final = piece(3).reshape(1, D)
    g_bin = piece(4).reshape(1, -1)
    g_bcf, g_lg, g_lb = piece(5), piece(6), piece(7)
    g_wsc = lax.dynamic_slice_in_dim(piece(8), chip * cs, cs, axis=1)
    g_wcf = lax.dynamic_slice_in_dim(piece(9), chip * cs, cs, axis=1)
    g_meta = lax.dynamic_slice_in_dim(piece(10).reshape(N_META, D), chip * ms, ms, axis=1)

    small_names = ["meta_tokens", "ffn1_norm", "mix_norm", "b_in", "conv_sc_w", "conv_cf_w", "conv_cf_b", "ln_cf_g",
                   "ln_cf_b", "ffn2_norm", "final_norm"]
    small_w = [meta_tokens, ffn1_norm, mix_norm, b_in, conv_sc_w[0], conv_cf_w[0], conv_cf_b, ln_cf_g, ln_cf_b,
               ffn2_norm, final_norm.reshape(1, D)]
    small_g = [g_meta, g_ffn1, g_mix, g_bin, g_wsc, g_wcf, g_bcf, g_lg, g_lb, g_ffn2, g_final]
    small_m = [m_meta_tokens, m_ffn1_norm, m_mix_norm, m_b_in, m_conv_sc_w[0], m_conv_cf_w[0], m_conv_cf_b, m_ln_cf_g,
               m_ln_cf_b, m_ffn2_norm, m_final_norm.reshape(1, D)]
    small_v = [v_meta_tokens, v_ffn1_norm, v_mix_norm, v_b_in, v_conv_sc_w[0], v_conv_cf_w[0], v_conv_cf_b, v_ln_cf_g,
               v_ln_cf_b, v_ffn2_norm, v_final_norm.reshape(1, D)]
    s_d, s_m, s_v = _adamw_small(small_w, small_g, small_m, small_v)
    shapes = {"conv_sc_w": conv_sc_w.shape, "conv_cf_w": conv_cf_w.shape, "final_norm": final_norm.shape}
    small_out = {}
    for nm, g, d, m, v in zip(small_names, small_g, s_d, s_m, s_v):
        shp = shapes.get(nm, g.shape)
        small_out[nm] = tuple(t.reshape(shp) for t in (g, d, m, v))

    order = ["meta_tokens", "ffn1_norm", "ffn1_w_gate", "ffn1_w_up", "ffn1_w_down", "mix_norm", "w_in", "b_in",
             "conv_sc_w", "conv_cf_w", "conv_cf_b", "ln_cf_g", "ln_cf_b", "w_out", "ffn2_norm", "ffn2_w_gate",
             "ffn2_w_up", "ffn2_w_down", "final_norm"]
    res = {**big_out, **small_out}
    outs = [loss, grad_x[None]]
    for q in range(4):
        outs.extend(res[nm][q] for nm in order)
    return tuple(outs)
```

```python
import functools

import jax
import jax.numpy as jnp
from jax import lax
from jax.experimental import pallas as pl
from jax.experimental.pallas import tpu as pltpu

F32 = jnp.float32
BF16 = jnp.bfloat16
MESH = pl.DeviceIdType.MESH

N_META = 16
TT = 128
PAD = TT - N_META
HALO = 32
EPS = 1e-6
FFN_RES_SCALE = 0.5
N_CHIPS = 4
N_DEV = 8

ADAM_LR = 0.001
ADAM_B1 = 0.9
ADAM_B2 = 0.999
ADAM_EPS = 1e-08
ADAM_WD = 0.01
ADAM_STEP = 10

V7X_VMEM_BYTES = 64 * 2 ** 20
NT_DIMS = (((1,), (1,)), ((), ()))
TN_DIMS = (((0,), (0,)), ((), ()))


def _params(semantics, block_bytes):
    limit = min(2 * block_bytes + 16 * 2 ** 20, V7X_VMEM_BYTES - 6 * 2 ** 20)
    return pltpu.CompilerParams(dimension_semantics=semantics, vmem_limit_bytes=int(limit))


def _nbytes(shape, dtype):
    n = 1
    for d in shape:
        if d is not None:
            n *= d
    return n * jnp.dtype(dtype).itemsize


def _row_tile(rows, target, mult=8):
    best = None
    for t in range(mult, min(rows, target) + 1, mult):
        if rows % t == 0:
            best = t
    assert best is not None, (rows, target, mult)
    return best


def _sigmoid(v):
    return jax.nn.sigmoid(v)


def _dsilu(v, s):
    return s * (1.0 + v * (1.0 - s))


def _embed_rms(x2, meta, gain):
    S, D = x2.shape
    T = S + TT

    def body(x_ref, meta_ref, g_ref, hs_ref, n_ref):
        i = pl.program_id(0)

        @pl.when(i == 0)
        def _():
            hs_ref[...] = jnp.zeros_like(hs_ref)
            hs_ref[PAD:, :] = meta_ref[...]

        @pl.when(i > 0)
        def _():
            hs_ref[...] = x_ref[...]

        h = hs_ref[...]
        r = lax.rsqrt(jnp.mean(h * h, axis=-1, keepdims=True) + EPS)
        n_ref[...] = ((h * r) * g_ref[...]).astype(BF16)

    blk = _nbytes((TT, D), F32) * 2 + _nbytes((TT, D), BF16)
    return pl.pallas_call(
        body, name="embed_rms", grid=(T // TT,),
        in_specs=[pl.BlockSpec((TT, D), lambda i: (jnp.maximum(i - 1, 0), 0)),
                  pl.BlockSpec((N_META, D), lambda i: (0, 0)),
                  pl.BlockSpec((1, D), lambda i: (0, 0))],
        out_specs=[pl.BlockSpec((TT, D), lambda i: (i, 0)), pl.BlockSpec((TT, D), lambda i: (i, 0))],
        out_shape=[jax.ShapeDtypeStruct((T, D), F32), jax.ShapeDtypeStruct((T, D), BF16)],
        compiler_params=_params(("parallel",), blk),
    )(x2, meta, gain)


def _rms(hs, gain, name):
    T, D = hs.shape
    te = _row_tile(T, 384)

    def body(h_ref, g_ref, n_ref):
        h = h_ref[...]
        r = lax.rsqrt(jnp.mean(h * h, axis=-1, keepdims=True) + EPS)
        n_ref[...] = ((h * r) * g_ref[...]).astype(BF16)

    blk = _nbytes((te, D), F32) + _nbytes((te, D), BF16)
    return pl.pallas_call(
        body, name=name, grid=(T // te,),
        in_specs=[pl.BlockSpec((te, D), lambda i: (i, 0)), pl.BlockSpec((1, D), lambda i: (0, 0))],
        out_specs=pl.BlockSpec((te, D), lambda i: (i, 0)),
        out_shape=jax.ShapeDtypeStruct((T, D), BF16),
        compiler_params=_params(("parallel",), blk),
    )(hs, gain)


def _rms_bwd_math(dn, h, g):
    r = lax.rsqrt(jnp.mean(h * h, axis=-1, keepdims=True) + EPS)
    xh = h * r
    dgain = jnp.sum(dn * xh, axis=0, keepdims=True)
    dxh = dn * g
    dh = r * (dxh - xh * jnp.mean(dxh * xh, axis=-1, keepdims=True))
    return dh, dgain


def _rms_bwd(dn, hs, gain, dres, scale, name):
    T, D = hs.shape
    te = _row_tile(T, 384)

    def body(dn_ref, h_ref, g_ref, dres_ref, dhs_ref, dhb_ref, dg_ref):
        dh, dgain = _rms_bwd_math(dn_ref[...], h_ref[...], g_ref[...])
        d = dres_ref[...] + dh
        dhs_ref[...] = d
        dhb_ref[...] = (scale * d).astype(BF16)

        @pl.when(pl.program_id(0) == 0)
        def _():
            dg_ref[...] = jnp.zeros_like(dg_ref)

        dg_ref[...] += dgain

    blk = _nbytes((te, D), F32) * 4 + _nbytes((te, D), BF16)
    row = lambda i: (i, 0)
    return pl.pallas_call(
        body, name=name, grid=(T // te,),
        in_specs=[pl.BlockSpec((te, D), row), pl.BlockSpec((te, D), row), pl.BlockSpec((1, D), lambda i: (0, 0)),
                  pl.BlockSpec((te, D), row)],
        out_specs=[pl.BlockSpec((te, D), row), pl.BlockSpec((te, D), row), pl.BlockSpec((1, D), lambda i: (0, 0))],
        out_shape=[jax.ShapeDtypeStruct((T, D), F32), jax.ShapeDtypeStruct((T, D), BF16),
                   jax.ShapeDtypeStruct((1, D), F32)],
        compiler_params=_params(("arbitrary",), blk),
    )(dn, hs, gain, dres)


def _rms_bwd_first(dn, hs, gain, dres):
    T, D = hs.shape
    S = T - TT

    def body(dn_ref, h_ref, g_ref, dres_ref, gx_ref, gm_ref, dg_ref):
        i = pl.program_id(0)
        dh, dgain = _rms_bwd_math(dn_ref[...], h_ref[...], g_ref[...])
        d = dres_ref[...] + dh

        @pl.when(i == 0)
        def _():
            dg_ref[...] = jnp.zeros_like(dg_ref)
            gm_ref[...] = d[PAD:, :]

        @pl.when(i > 0)
        def _():
            gx_ref[...] = d

        dg_ref[...] += dgain

    blk = _nbytes((TT, D), F32) * 4
    row = lambda i: (i, 0)
    return pl.pallas_call(
        body, name="rms_bwd_ffn1", grid=(T // TT,),
        in_specs=[pl.BlockSpec((TT, D), row), pl.BlockSpec((TT, D), row), pl.BlockSpec((1, D), lambda i: (0, 0)),
                  pl.BlockSpec((TT, D), row)],
        out_specs=[pl.BlockSpec((TT, D), lambda i: (jnp.maximum(i - 1, 0), 0)),
                   pl.BlockSpec((N_META, D), lambda i: (0, 0)), pl.BlockSpec((1, D), lambda i: (0, 0))],
        out_shape=[jax.ShapeDtypeStruct((S, D), F32), jax.ShapeDtypeStruct((N_META, D), F32),
                   jax.ShapeDtypeStruct((1, D), F32)],
        compiler_params=_params(("arbitrary",), blk),
    )(dn, hs, gain, dres)


def _final_loss(hs, gain, tgt):
    T, D = hs.shape

    def body(h_ref, g_ref, t_ref, dhs_ref, dhb_ref, loss_ref, dg_ref):
        i = pl.program_id(0)
        h = h_ref[...]
        g = g_ref[...]
        r = lax.rsqrt(jnp.mean(h * h, axis=-1, keepdims=True) + EPS)
        xh = h * r
        e = jnp.where(i > 0, xh * g - t_ref[...], 0.0)
        tile_loss = jnp.sum(jnp.sum(e * e, axis=1, keepdims=True), axis=0, keepdims=True) * (0.5 / D)
        dout = e * (1.0 / D)
        dgain = jnp.sum(dout * xh, axis=0, keepdims=True)
        dxh = dout * g
        d = r * (dxh - xh * jnp.mean(dxh * xh, axis=-1, keepdims=True))
        dhs_ref[...] = d
        dhb_ref[...] = (FFN_RES_SCALE * d).astype(BF16)

        @pl.when(i == 0)
        def _():
            loss_ref[...] = jnp.zeros_like(loss_ref)
            dg_ref[...] = jnp.zeros_like(dg_ref)

        loss_ref[...] += jnp.broadcast_to(tile_loss, loss_ref.shape)
        dg_ref[...] += dgain

    blk = _nbytes((TT, D), F32) * 3 + _nbytes((TT, D), BF16)
    row = lambda i: (i, 0)
    return pl.pallas_call(
        body, name="final_loss", grid=(T // TT,),
        in_specs=[pl.BlockSpec((TT, D), row), pl.BlockSpec((1, D), lambda i: (0, 0)),
                  pl.BlockSpec((TT, D), lambda i: (jnp.maximum(i - 1, 0), 0))],
        out_specs=[pl.BlockSpec((TT, D), row), pl.BlockSpec((TT, D), row),
                   pl.BlockSpec((1, 128), lambda i: (0, 0)), pl.BlockSpec((1, D), lambda i: (0, 0))],
        out_shape=[jax.ShapeDtypeStruct((T, D), F32), jax.ShapeDtypeStruct((T, D), BF16),
                   jax.ShapeDtypeStruct((1, 128), F32), jax.ShapeDtypeStruct((1, D), F32)],
        compiler_params=_params(("arbitrary",), blk),
    )(hs, gain, tgt)


def _tm(T):
    return _row_tile(T, 384, 128)


TOKEN = pl.BlockSpec((8, 128), lambda *_: (0, 0))


def _ffn_up(n, wg, wu, after, name):
    T, D = n.shape
    Fs = wg.shape[2]
    tm = _tm(T)

    def body(n_ref, wg_ref, wu_ref, after_ref, g_ref, u_ref, a_ref):
        nn = n_ref[...]
        g = jnp.dot(nn, wg_ref[...], preferred_element_type=F32)
        u = jnp.dot(nn, wu_ref[...], preferred_element_type=F32)
        g_ref[...] = g.astype(BF16)
        u_ref[...] = u.astype(BF16)
        a_ref[...] = (jax.nn.silu(g) * u).astype(BF16)

    blk = _nbytes((tm, D), BF16) + 2 * _nbytes((D, Fs), BF16) + 3 * _nbytes((tm, Fs), BF16) + 2 * _nbytes((tm, Fs), F32)
    out = pl.BlockSpec((tm, Fs), lambda j, i: (i, j))
    shp = jax.ShapeDtypeStruct((T, N_CHIPS * Fs), BF16)
    return pl.pallas_call(
        body, name=name, grid=(N_CHIPS, T // tm),
        in_specs=[pl.BlockSpec((tm, D), lambda j, i: (i, 0)),
                  pl.BlockSpec((None, D, Fs), lambda j, i: (j, 0, 0)),
                  pl.BlockSpec((None, D, Fs), lambda j, i: (j, 0, 0)), TOKEN],
        out_specs=[out, out, out], out_shape=[shp, shp, shp],
        compiler_params=_params(("parallel", "parallel"), blk),
    )(n, wg, wu, after)


def _ffn_down(a, wd, hs, name):
    T, F = a.shape
    D = wd.shape[1]
    tm = _tm(T)
    tn = D // 2

    def body(a_ref, w_ref, h_ref, o_ref):
        o_ref[...] = h_ref[...] + FFN_RES_SCALE * jnp.dot(a_ref[...], w_ref[...], preferred_element_type=F32)

    blk = _nbytes((tm, F), BF16) + _nbytes((F, tn), BF16) + 3 * _nbytes((tm, tn), F32)
    return pl.pallas_call(
        body, name=name, grid=(D // tn, T // tm),
        in_specs=[pl.BlockSpec((tm, F), lambda n, i: (i, 0)), pl.BlockSpec((F, tn), lambda n, i: (0, n)),
                  pl.BlockSpec((tm, tn), lambda n, i: (i, n))],
        out_specs=pl.BlockSpec((tm, tn), lambda n, i: (i, n)),
        out_shape=jax.ShapeDtypeStruct((T, D), F32),
        compiler_params=_params(("parallel", "parallel"), blk),
    )(a, wd, hs)


def _mix_in(n, w, b):
    T, D = n.shape
    Ns = w.shape[2]
    tm = _tm(T)

    def body(n_ref, w_ref, b_ref, u_ref):
        u_ref[...] = jnp.dot(n_ref[...], w_ref[...], preferred_element_type=F32) + b_ref[...]

    blk = _nbytes((tm, D), BF16) + _nbytes((D, Ns), BF16) + 2 * _nbytes((tm, Ns), F32)
    return pl.pallas_call(
        body, name="mix_in", grid=(N_CHIPS, T // tm),
        in_specs=[pl.BlockSpec((tm, D), lambda j, i: (i, 0)), pl.BlockSpec((None, D, Ns), lambda j, i: (j, 0, 0)),
                  pl.BlockSpec((1, Ns), lambda j, i: (0, j))],
        out_specs=pl.BlockSpec((tm, Ns), lambda j, i: (i, j)),
        out_shape=jax.ShapeDtypeStruct((T, N_CHIPS * Ns), F32),
        compiler_params=_params(("parallel", "parallel"), blk),
    )(n, w, b)


def _mix_out(y, w, hs):
    T, D = y.shape
    tm = _tm(T)

    def body(y_ref, w_ref, h_ref, o_ref):
        o_ref[...] = h_ref[...] + jnp.dot(y_ref[...], w_ref[...], preferred_element_type=F32)

    blk = _nbytes((tm, D), BF16) + _nbytes((D, D), BF16) + 3 * _nbytes((tm, D), F32)
    return pl.pallas_call(
        body, name="mix_out", grid=(T // tm,),
        in_specs=[pl.BlockSpec((tm, D), lambda i: (i, 0)), pl.BlockSpec((D, D), lambda i: (0, 0)),
                  pl.BlockSpec((tm, D), lambda i: (i, 0))],
        out_specs=pl.BlockSpec((tm, D), lambda i: (i, 0)),
        out_shape=jax.ShapeDtypeStruct((T, D), F32),
        compiler_params=_params(("parallel",), blk),
    )(y, w, hs)


def _ffn_bwd_act(dfb, wd, g, u, name):
    T, D = dfb.shape
    Fs = wd.shape[1]
    tm = _tm(T)

    def body(d_ref, w_ref, g_ref, u_ref, dg_ref, du_ref):
        da = lax.dot_general(d_ref[...], w_ref[...], NT_DIMS, preferred_element_type=F32)
        gv = g_ref[...].astype(F32)
        uv = u_ref[...].astype(F32)
        s = _sigmoid(gv)
        du_ref[...] = (da * (gv * s)).astype(BF16)
        dg_ref[...] = (da * uv * _dsilu(gv, s)).astype(BF16)

    blk = _nbytes((tm, D), BF16) + _nbytes((Fs, D), BF16) + 4 * _nbytes((tm, Fs), BF16) + 3 * _nbytes((tm, Fs), F32)
    io = pl.BlockSpec((tm, Fs), lambda j, i: (i, j))
    shp = jax.ShapeDtypeStruct((T, N_CHIPS * Fs), BF16)
    return pl.pallas_call(
        body, name=name, grid=(N_CHIPS, T // tm),
        in_specs=[pl.BlockSpec((tm, D), lambda j, i: (i, 0)), pl.BlockSpec((None, Fs, D), lambda j, i: (j, 0, 0)), io, io],
        out_specs=[io, io], out_shape=[shp, shp],
        compiler_params=_params(("parallel", "parallel"), blk),
    )(dfb, wd, g, u)


def _nt_panel(lhs_list, w_list, after, name):
    T = lhs_list[0].shape[0]
    nsh, Dout, Ks = w_list[0].shape
    npair = len(lhs_list)
    tm = _tm(T)
    tn = Dout // 4 if npair * nsh * Ks > 4096 else Dout // 2

    def body(*refs):
        l_refs, w_refs, o_ref = refs[:npair], refs[npair:2 * npair], refs[2 * npair + 1]
        acc = None
        for p in range(npair):
            for j in range(nsh):
                part = lax.dot_general(l_refs[p][:, j * Ks:(j + 1) * Ks], w_refs[p][j], NT_DIMS,
                                       preferred_element_type=F32)
                acc = part if acc is None else acc + part
        o_ref[...] = acc

    blk = npair * (_nbytes((tm, nsh * Ks), BF16) + _nbytes((nsh, tn, Ks), BF16)) + 3 * _nbytes((tm, tn), F32)
    return pl.pallas_call(
        body, name=name, grid=(Dout // tn, T // tm),
        in_specs=[pl.BlockSpec((tm, nsh * Ks), lambda n, i: (i, 0))] * npair
                 + [pl.BlockSpec((nsh, tn, Ks), lambda n, i: (0, n, 0))] * npair + [TOKEN],
        out_specs=pl.BlockSpec((tm, tn), lambda n, i: (i, n)),
        out_shape=jax.ShapeDtypeStruct((T, Dout), F32),
        compiler_params=_params(("parallel", "parallel"), blk),
    )(*lhs_list, *w_list, after)


def _tn_call(name, grid, lhs, lhs_spec, rhs_list, rhs_specs, out_shapes, out_specs, blk):
    nr = len(rhs_list)

    def body(*refs):
        l_ref, r_refs, o_refs = refs[0], refs[1:1 + nr], refs[1 + nr:]
        k = pl.program_id(len(grid) - 1)
        lv = l_ref[...]
        for q in range(nr):
            part = lax.dot_general(lv, r_refs[q][...], TN_DIMS, preferred_element_type=F32)
            part = part.reshape(o_refs[q].shape)

            @pl.when(k == 0)
            def _(o=o_refs[q], part=part):
                o[...] = part

            @pl.when(k > 0)
            def _(o=o_refs[q], part=part):
                o[...] += part

    return pl.pallas_call(
        body, name=name, grid=grid, in_specs=[lhs_spec] + rhs_specs, out_specs=out_specs, out_shape=out_shapes,
        compiler_params=_params(("parallel",) * (len(grid) - 1) + ("arbitrary",), blk),
    )(lhs, *rhs_list)


def _tk(T):
    return _row_tile(T, 1408, 128)


def _wgrad_cols(n, rhs_list, name):
    T, D = n.shape
    Ns = rhs_list[0].shape[1] // N_CHIPS
    tk = _tk(T)
    nr = len(rhs_list)
    blk = _nbytes((tk, D // 2), BF16) + nr * (_nbytes((tk, Ns), BF16) + 2 * _nbytes((D // 2, Ns), F32))
    return _tn_call(
        name, (N_CHIPS, 2, T // tk), n, pl.BlockSpec((tk, D // 2), lambda j, m, k: (k, m)),
        rhs_list, [pl.BlockSpec((tk, Ns), lambda j, m, k: (k, j))] * nr,
        [jax.ShapeDtypeStruct((N_CHIPS, 2, D // 2, Ns), F32)] * nr,
        [pl.BlockSpec((None, None, D // 2, Ns), lambda j, m, k: (j, m, 0, 0))] * nr, blk)


def _wgrad_down(a, dfb, name):
    T, F = a.shape
    D = dfb.shape[1]
    Fs = F // N_CHIPS
    tk = _tk(T)
    tn = D // 2
    blk = _nbytes((tk, Fs), BF16) + _nbytes((tk, tn), BF16) + 2 * _nbytes((Fs, tn), F32)
    return _tn_call(
        name, (N_CHIPS, D // tn, T // tk), a, pl.BlockSpec((tk, Fs), lambda j, n, k: (k, j)),
        [dfb], [pl.BlockSpec((tk, tn), lambda j, n, k: (k, n))],
        [jax.ShapeDtypeStruct((N_CHIPS, 2, Fs // 2, D), F32)],
        [pl.BlockSpec((None, 2, Fs // 2, tn), lambda j, n, k: (j, 0, 0, n))], blk)[0]


def _wgrad_out(y, dmb):
    T, D = y.shape
    tk = _tk(T)
    tn = D // 2
    rows = D // (2 * N_CHIPS)
    blk = _nbytes((tk, D // 2), BF16) + _nbytes((tk, tn), BF16) + 2 * _nbytes((D // 2, tn), F32)
    return _tn_call(
        "wgrad_w_out", (2, D // tn, T // tk), y, pl.BlockSpec((tk, D // 2), lambda m, n, k: (k, m)),
        [dmb], [pl.BlockSpec((tk, tn), lambda m, n, k: (k, n))],
        [jax.ShapeDtypeStruct((N_CHIPS, 2, rows, D), F32)],
        [pl.BlockSpec((2, 2, rows, tn), lambda m, n, k: (m, 0, 0, n))], blk)[0]


def _row_masks(i, last):
    rows = i * TT + lax.broadcasted_iota(jnp.int32, (TT, 1), 0)
    prows = i * TT - HALO + lax.broadcasted_iota(jnp.int32, (HALO, 1), 0)
    return rows >= PAD, (prows >= PAD) & (i > 0), i < last


def _conv_inputs(u, up, mask_c, mask_p, zbuf, pbuf, C1):
    b, c, v, a, g = (u[:, k * C1:(k + 1) * C1] for k in range(5))
    cp, vp, ap, gp = (up[:, k * C1:(k + 1) * C1] for k in range(1, 5))
    sg = _sigmoid(g)
    pbuf[0:HALO, :] = jnp.where(mask_p, cp * vp, 0.0)
    pbuf[HALO:, :] = jnp.where(mask_c, c * v, 0.0)
    zbuf[0:HALO, :] = jnp.where(mask_p, ap * _sigmoid(gp), 0.0)
    zbuf[HALO:, :] = jnp.where(mask_c, a * sg, 0.0)
    return b, c, v, a, sg


def _causal_conv(w_ref, buf):
    K = w_ref.shape[0]
    acc = None
    for k in range(K):
        lo = HALO - (K - 1) + k
        term = w_ref[k:k + 1, :] * buf[lo:lo + TT, :]
        acc = term if acc is None else acc + term
    return acc


def _anticausal_conv(w_ref, buf):
    K = w_ref.shape[0]
    acc = None
    for k in range(K):
        lo = K - 1 - k
        term = w_ref[k:k + 1, :] * buf[lo:lo + TT, :]
        acc = term if acc is None else acc + term
    return acc


def _conv_weight_sums(dw_ref, dy, buf):
    K = dw_ref.shape[0]
    for k in range(K):
        lo = HALO - (K - 1) + k
        dw_ref[k:k + 1, :] += jnp.sum(dy * buf[lo:lo + TT, :], axis=0, keepdims=True)


def _layernorm_stats(z1):
    mu = jnp.mean(z1, axis=-1, keepdims=True)
    zc = z1 - mu
    rs = lax.rsqrt(jnp.mean(zc * zc, axis=-1, keepdims=True) + EPS)
    return zc * rs, rs


def _mixer_specs(T, DIN, C1, ksc, kcf):
    cur = pl.BlockSpec((TT, DIN), lambda i: (i, 0))
    prev = pl.BlockSpec((HALO, DIN), lambda i: (jnp.maximum(i * (TT // HALO) - 1, 0), 0))
    full = lambda r: pl.BlockSpec((r, C1), lambda i: (0, 0))
    return cur, prev, [full(ksc), full(kcf), full(1), full(1), full(1)]


def _mix_conv_fwd(u, wsc, wcf, bcf, lg, lb):
    T, DIN = u.shape
    C1 = DIN // 5
    last = T // TT - 1

    def body(u_ref, up_ref, wsc_ref, wcf_ref, bcf_ref, lg_ref, lb_ref, y_ref, zbuf, pbuf):
        i = pl.program_id(0)
        mask_c, mask_p, _ = _row_masks(i, last)
        b, _, _, _, _ = _conv_inputs(u_ref[...], up_ref[...], mask_c, mask_p, zbuf, pbuf, C1)
        cs = _causal_conv(wsc_ref, pbuf)
        z1 = _causal_conv(wcf_ref, zbuf) + bcf_ref[...]
        zh, _ = _layernorm_stats(z1)
        ln = zh * lg_ref[...] + lb_ref[...]
        y_ref[:, 0:C1] = jnp.where(mask_c, b * cs, 0.0).astype(BF16)
        y_ref[:, C1:] = jnp.where(mask_c, jax.nn.silu(ln), 0.0).astype(BF16)

    cur, prev, small = _mixer_specs(T, DIN, C1, wsc.shape[0], wcf.shape[0])
    blk = _nbytes((TT + HALO, DIN), F32) + _nbytes((TT, 2 * C1), BF16) + 12 * _nbytes((TT + HALO, C1), F32)
    return pl.pallas_call(
        body, name="mix_conv_fwd", grid=(T // TT,),
        in_specs=[cur, prev] + small,
        out_specs=pl.BlockSpec((TT, 2 * C1), lambda i: (i, 0)),
        out_shape=jax.ShapeDtypeStruct((T, 2 * C1), BF16),
        scratch_shapes=[pltpu.VMEM((TT + HALO, C1), F32), pltpu.VMEM((TT + HALO, C1), F32)],
        compiler_params=_params(("arbitrary",), blk),
    )(u, u, wsc, wcf, bcf, lg, lb)


def _mix_conv_bwd1(u, dy, wsc, wcf, bcf, lg, lb):
    T, DIN = u.shape
    C1 = DIN // 5
    last = T // TT - 1

    def body(u_ref, up_ref, dy_ref, wsc_ref, wcf_ref, bcf_ref, lg_ref, lb_ref,
             dz1_ref, dcs_ref, db_ref, dlg_ref, dlb_ref, dbcf_ref, zbuf, pbuf):
        i = pl.program_id(0)
        mask_c, mask_p, _ = _row_masks(i, last)
        b, _, _, _, _ = _conv_inputs(u_ref[...], up_ref[...], mask_c, mask_p, zbuf, pbuf, C1)
        cs = _causal_conv(wsc_ref, pbuf)
        z1 = _causal_conv(wcf_ref, zbuf) + bcf_ref[...]
        zh, rs = _layernorm_stats(z1)
        ln = zh * lg_ref[...] + lb_ref[...]
        dy = dy_ref[...]
        dysc = jnp.where(mask_c, dy[:, 0:C1], 0.0)
        dycf = jnp.where(mask_c, dy[:, C1:], 0.0)
        db_ref[...] = (dysc * cs).astype(BF16)
        dcs_ref[...] = dysc * b
        dl = dycf * _dsilu(ln, _sigmoid(ln))
        dzh = dl * lg_ref[...]
        dz1 = rs * (dzh - jnp.mean(dzh, axis=-1, keepdims=True) - zh * jnp.mean(dzh * zh, axis=-1, keepdims=True))
        dz1_ref[...] = dz1

        @pl.when(i == 0)
        def _():
            dlg_ref[...] = jnp.zeros_like(dlg_ref)
            dlb_ref[...] = jnp.zeros_like(dlb_ref)
            dbcf_ref[...] = jnp.zeros_like(dbcf_ref)

        dlg_ref[...] += jnp.sum(dl * zh, axis=0, keepdims=True)
        dlb_ref[...] += jnp.sum(dl, axis=0, keepdims=True)
        dbcf_ref[...] += jnp.sum(dz1, axis=0, keepdims=True)

    cur, prev, small = _mixer_specs(T, DIN, C1, wsc.shape[0], wcf.shape[0])
    tile = lambda: pl.BlockSpec((TT, C1), lambda i: (i, 0))
    vec = lambda: pl.BlockSpec((1, C1), lambda i: (0, 0))
    blk = _nbytes((TT + HALO, DIN), F32) + 4 * _nbytes((TT, C1), F32) + 16 * _nbytes((TT + HALO, C1), F32)
    return pl.pallas_call(
        body, name="mix_conv_bwd1", grid=(T // TT,),
        in_specs=[cur, prev, pl.BlockSpec((TT, 2 * C1), lambda i: (i, 0))] + small,
        out_specs=[tile(), tile(), tile(), vec(), vec(), vec()],
        out_shape=[jax.ShapeDtypeStruct((T, C1), F32), jax.ShapeDtypeStruct((T, C1), F32),
                   jax.ShapeDtypeStruct((T, C1), BF16)] + [jax.ShapeDtypeStruct((1, C1), F32)] * 3,
        scratch_shapes=[pltpu.VMEM((TT + HALO, C1), F32), pltpu.VMEM((TT + HALO, C1), F32)],
        compiler_params=_params(("arbitrary",), blk),
    )(u, u, dy, wsc, wcf, bcf, lg, lb)


def _mix_conv_bwd2(u, dz1, dcs, db, wsc, wcf):
    T, DIN = u.shape
    C1 = DIN // 5
    last = T // TT - 1
    ksc, kcf = wsc.shape[0], wcf.shape[0]

    def body(u_ref, up_ref, dz_ref, dzn_ref, dc_ref, dcn_ref, db_ref, wsc_ref, wcf_ref,
             du_ref, dbin_ref, dwsc_ref, dwcf_ref, zbuf, pbuf, dzbuf, dcbuf):
        i = pl.program_id(0)
        mask_c, mask_p, has_next = _row_masks(i, last)
        _, c, v, a, sg = _conv_inputs(u_ref[...], up_ref[...], mask_c, mask_p, zbuf, pbuf, C1)
        dz1 = dz_ref[...]
        dcs = dc_ref[...]
        dzbuf[0:TT, :] = dz1
        dzbuf[TT:, :] = jnp.where(has_next, dzn_ref[...], 0.0)
        dcbuf[0:TT, :] = dcs
        dcbuf[TT:, :] = jnp.where(has_next, dcn_ref[...], 0.0)

        @pl.when(i == 0)
        def _():
            dbin_ref[...] = jnp.zeros_like(dbin_ref)
            dwsc_ref[...] = jnp.zeros_like(dwsc_ref)
            dwcf_ref[...] = jnp.zeros_like(dwcf_ref)

        _conv_weight_sums(dwcf_ref, dz1, zbuf)
        _conv_weight_sums(dwsc_ref, dcs, pbuf)
        dz0 = jnp.where(mask_c, _anticausal_conv(wcf_ref, dzbuf), 0.0)
        dp = jnp.where(mask_c, _anticausal_conv(wsc_ref, dcbuf), 0.0)
        parts = (db_ref[...].astype(F32), dp * v, dp * c, dz0 * sg, dz0 * a * sg * (1.0 - sg))
        for k, part in enumerate(parts):
            du_ref[:, k * C1:(k + 1) * C1] = part.astype(BF16)
            dbin_ref[:, k * C1:(k + 1) * C1] += jnp.sum(part, axis=0, keepdims=True)

    cur, prev, small = _mixer_specs(T, DIN, C1, ksc, kcf)
    tile = lambda: pl.BlockSpec((TT, C1), lambda i: (i, 0))
    nxt = lambda: pl.BlockSpec((HALO, C1), lambda i: (jnp.minimum((i + 1) * (TT // HALO), T // HALO - 1), 0))
    blk = (_nbytes((TT + HALO, DIN), F32) + _nbytes((TT, DIN), BF16) + 5 * _nbytes((TT, C1), F32)
           + 16 * _nbytes((TT + HALO, C1), F32))
    buf = lambda: pltpu.VMEM((TT + HALO, C1), F32)
    return pl.pallas_call(
        body, name="mix_conv_bwd2", grid=(T // TT,),
        in_specs=[cur, prev, tile(), nxt(), tile(), nxt(), tile(), small[0], small[1]],
        out_specs=[pl.BlockSpec((TT, DIN), lambda i: (i, 0)), pl.BlockSpec((1, DIN), lambda i: (0, 0)),
                   pl.BlockSpec((ksc, C1), lambda i: (0, 0)), pl.BlockSpec((kcf, C1), lambda i: (0, 0))],
        out_shape=[jax.ShapeDtypeStruct((T, DIN), BF16), jax.ShapeDtypeStruct((1, DIN), F32),
                   jax.ShapeDtypeStruct((ksc, C1), F32), jax.ShapeDtypeStruct((kcf, C1), F32)],
        scratch_shapes=[buf(), buf(), buf(), buf()],
        compiler_params=_params(("arbitrary",), blk),
    )(u, u, dz1, dz1, dcs, dcs, db, wsc, wcf)


def _place():
    x, y, c = lax.axis_index("x"), lax.axis_index("y"), lax.axis_index("c")
    chips = [(1 - x, y), (x, 1 - y), (1 - x, 1 - y)]
    return x, y, c, chips


ANY = pl.BlockSpec(memory_space=pl.ANY)


def _cast_own_block(place, w, name):
    R, C = w.shape
    tr = _row_tile(R // 2, 256, 16)
    nblk = R // 2 // tr

    def body(place_ref, w_ref, o_ref):
        o_ref[...] = w_ref[...].astype(BF16)

    return pl.pallas_call(
        body, name=name,
        grid_spec=pltpu.PrefetchScalarGridSpec(
            num_scalar_prefetch=1, grid=(2, nblk),
            in_specs=[pl.BlockSpec((tr, C), lambda h, i, p: (h * nblk + i, 0))],
            out_specs=pl.BlockSpec((None, None, tr, C), lambda h, i, p: (p[0], h, i, 0))),
        out_shape=jax.ShapeDtypeStruct((N_CHIPS, 2, R // 2, C), BF16),
        compiler_params=_params(("parallel", "parallel"), _nbytes((tr, C), F32) + _nbytes((tr, C), BF16)),
    )(place, w)


def _gather_weights(bufs):
    nw = len(bufs)

    def body(*refs):
        o_refs = refs[nw:2 * nw]
        send, recv = refs[2 * nw:]
        x, y, c, chips = _place()
        s = 2 * x + y
        sib = (x, y, 1 - c)

        def remote(w, k, blk, half, to):
            ref = o_refs[w].at[blk, half]
            return pltpu.make_async_remote_copy(src_ref=ref, dst_ref=ref, send_sem=send.at[6 * w + k],
                                                recv_sem=recv.at[6 * w + k], device_id=to, device_id_type=MESH)

        sends = []
        for w in range(nw):
            for r, (tx, ty) in enumerate(chips):
                cp = remote(w, r, s, c, (tx, ty, c))
                cp.start()
                sends.append(cp)
        for w in range(nw):
            for r, (tx, ty) in enumerate(chips):
                sr = 2 * tx + ty
                remote(w, r, sr, c, (tx, ty, c)).wait_recv()
                cp = remote(w, 3 + r, sr, c, sib)
                cp.start()
                sends.append(cp)
        for w in range(nw):
            for r, (tx, ty) in enumerate(chips):
                remote(w, 3 + r, 2 * tx + ty, 1 - c, sib).wait_recv()
        for cp in sends:
            cp.wait_send()

    return pl.pallas_call(
        body, name="gather_weights", in_specs=[ANY] * nw, out_specs=[ANY] * nw,
        out_shape=[jax.ShapeDtypeStruct(b.shape, b.dtype) for b in bufs],
        input_output_aliases={w: w for w in range(nw)},
        scratch_shapes=[pltpu.SemaphoreType.DMA((6 * nw,)), pltpu.SemaphoreType.DMA((6 * nw,))],
    )(*bufs)


HBM = pl.BlockSpec(memory_space=pltpu.HBM)
SEM = pl.BlockSpec(memory_space=pltpu.SEMAPHORE)
EFFECT = pltpu.SideEffectType.DATAFLOW_SIDE_EFFECTING


def _gather_copies(refs, send, recv):
    x, y, c, chips = _place()
    s = 2 * x + y
    return [pltpu.make_async_remote_copy(src_ref=ref.at[s, c], dst_ref=ref.at[s, c], send_sem=send.at[3 * w + r],
                                         recv_sem=recv.at[3 * w + r], device_id=(tx, ty, c), device_id_type=MESH)
            for w, ref in enumerate(refs) for r, (tx, ty) in enumerate(chips)]


def _scatter_copies(refs, send, recv):
    x, y, c, chips = _place()
    nw = len(refs) // 2
    return [pltpu.make_async_remote_copy(src_ref=refs[w].at[2 * tx + ty], dst_ref=refs[nw + w].at[r],
                                         send_sem=send.at[3 * w + r], recv_sem=recv.at[3 * w + r],
                                         device_id=(tx, ty, c), device_id_type=MESH)
            for w in range(nw) for r, (tx, ty) in enumerate(chips)]


def _start_copies(bufs, ncopies, make_copies, name):
    n = len(bufs)

    def body(*refs):
        in_refs, send, recv, token = refs[:n], refs[n], refs[n + 1], refs[2 * n + 2]
        for cp in make_copies(in_refs, send, recv):
            cp.start()
        token[...] = jnp.zeros_like(token)

    outs = pl.pallas_call(
        body, name=name, in_specs=[HBM] * n,
        out_specs=[SEM, SEM] + [HBM] * n + [pl.BlockSpec(memory_space=pltpu.VMEM)],
        out_shape=[pltpu.SemaphoreType.DMA((ncopies,)), pltpu.SemaphoreType.DMA((ncopies,))]
                  + [pltpu.HBM(b.shape, b.dtype) for b in bufs] + [jax.ShapeDtypeStruct((8, 128), F32)],
        input_output_aliases={k: 2 + k for k in range(n)},
        compiler_params=pltpu.CompilerParams(has_side_effects=EFFECT),
    )(*[pltpu.with_memory_space_constraint(b, pltpu.HBM) for b in bufs])
    return outs[0], outs[1], list(outs[2:2 + n]), outs[2 + n]


def _wait_copies(send, recv, bufs, after, make_copies, name):
    n = len(bufs)

    def body(*refs):
        in_refs, send_ref, recv_ref = refs[:n], refs[n], refs[n + 1]
        for cp in make_copies(in_refs, send_ref, recv_ref):
            cp.wait_send()
            cp.wait_recv()

    outs = pl.pallas_call(
        body, name=name, in_specs=[HBM] * n + [SEM, SEM, ANY], out_specs=[HBM] * n,
        out_shape=[pltpu.HBM(b.shape, b.dtype) for b in bufs],
        input_output_aliases={k: k for k in range(n)},
        compiler_params=pltpu.CompilerParams(has_side_effects=EFFECT),
    )(*bufs, send, recv, after)
    return list(outs)


def _forward_halves(bufs, name):
    nw = len(bufs)

    def body(*refs):
        o_refs = refs[nw:2 * nw]
        send, recv = refs[2 * nw:]
        x, y, c, chips = _place()
        sib = (x, y, 1 - c)
        copies = []
        for w in range(nw):
            for r, (tx, ty) in enumerate(chips):
                ref = o_refs[w].at[2 * tx + ty, c]
                cp = pltpu.make_async_remote_copy(src_ref=ref, dst_ref=ref, send_sem=send.at[3 * w + r],
                                                  recv_sem=recv.at[3 * w + r], device_id=sib, device_id_type=MESH)
                cp.start()
                copies.append(cp)
        for w in range(nw):
            for r, (tx, ty) in enumerate(chips):
                ref = o_refs[w].at[2 * tx + ty, 1 - c]
                pltpu.make_async_remote_copy(src_ref=ref, dst_ref=ref, send_sem=send.at[3 * w + r],
                                             recv_sem=recv.at[3 * w + r], device_id=sib, device_id_type=MESH).wait_recv()
        for cp in copies:
            cp.wait_send()

    return pl.pallas_call(
        body, name=name, in_specs=[ANY] * nw, out_specs=[ANY] * nw,
        out_shape=[jax.ShapeDtypeStruct(b.shape, b.dtype) for b in bufs],
        input_output_aliases={w: w for w in range(nw)},
        scratch_shapes=[pltpu.SemaphoreType.DMA((3 * nw,)), pltpu.SemaphoreType.DMA((3 * nw,))],
    )(*bufs)


def _pair_exchange(gs, name):
    nw = len(gs)

    def body(*refs):
        g_refs, o_refs = refs[:nw], refs[nw:2 * nw]
        send, recv = refs[2 * nw:]
        x, y, c, _ = _place()
        sib = (x, y, 1 - c)
        copies = []
        for w in range(nw):
            for j in range(N_CHIPS):
                cp = pltpu.make_async_remote_copy(
                    src_ref=g_refs[w].at[j, 1 - c], dst_ref=o_refs[w].at[j], send_sem=send.at[N_CHIPS * w + j],
                    recv_sem=recv.at[N_CHIPS * w + j], device_id=sib, device_id_type=MESH)
                cp.start()
                copies.append(cp)
        for cp in copies:
            cp.wait()

    return pl.pallas_call(
        body, name=name, in_specs=[ANY] * nw, out_specs=[ANY] * nw,
        out_shape=[jax.ShapeDtypeStruct((N_CHIPS,) + g.shape[2:], F32) for g in gs],
        scratch_shapes=[pltpu.SemaphoreType.DMA((N_CHIPS * nw,)), pltpu.SemaphoreType.DMA((N_CHIPS * nw,))],
    )(*gs)


def _half_exchange(hs, name):
    nw = len(hs)

    def body(*refs):
        o_refs = refs[nw:2 * nw]
        send, recv = refs[2 * nw:]
        x, y, c, _ = _place()
        sib = (x, y, 1 - c)
        copies = []
        for w in range(nw):
            cp = pltpu.make_async_remote_copy(src_ref=o_refs[w].at[c], dst_ref=o_refs[w].at[c], send_sem=send.at[w],
                                              recv_sem=recv.at[w], device_id=sib, device_id_type=MESH)
            cp.start()
            copies.append(cp)
        for w, cp in enumerate(copies):
            cp.wait_send()
            pltpu.make_async_remote_copy(src_ref=o_refs[w].at[c], dst_ref=o_refs[w].at[1 - c], send_sem=send.at[w],
                                         recv_sem=recv.at[w], device_id=sib, device_id_type=MESH).wait_recv()

    return pl.pallas_call(
        body, name=name, in_specs=[ANY] * nw, out_specs=[ANY] * nw,
        out_shape=[jax.ShapeDtypeStruct(h.shape, F32) for h in hs],
        input_output_aliases={w: w for w in range(nw)},
        scratch_shapes=[pltpu.SemaphoreType.DMA((nw,)), pltpu.SemaphoreType.DMA((nw,))],
    )(*hs)


def _share_small(v, reduce, name):
    R, C = v.shape

    def body(v_ref, o_ref, *scratch):
        if reduce:
            all_ref, send, recv, lsem = scratch
        else:
            all_ref = o_ref
            send, recv, lsem = scratch
        x, y, c, _ = _place()
        me = 4 * x + 2 * y + c
        loc = pltpu.make_async_copy(v_ref, all_ref.at[me], lsem)
        loc.start()
        copies = []
        for k in range(1, N_DEV):
            kx, ky, kc = (k >> 2) & 1, (k >> 1) & 1, k & 1
            peer = (x ^ kx, y ^ ky, c ^ kc)
            cp = pltpu.make_async_remote_copy(src_ref=v_ref, dst_ref=all_ref.at[me], send_sem=send.at[k - 1],
                                              recv_sem=recv.at[k - 1], device_id=peer, device_id_type=MESH)
            cp.start()
            copies.append(cp)
        for k in range(1, N_DEV):
            kx, ky, kc = (k >> 2) & 1, (k >> 1) & 1, k & 1
            src = 4 * (x ^ kx) + 2 * (y ^ ky) + (c ^ kc)
            pltpu.make_async_remote_copy(src_ref=v_ref, dst_ref=all_ref.at[src], send_sem=send.at[k - 1],
                                         recv_sem=recv.at[k - 1], device_id=(x, y, c), device_id_type=MESH).wait_recv()
        for cp in copies:
            cp.wait_send()
        loc.wait()
        if reduce:
            total = all_ref[0]
            for d in range(1, N_DEV):
                total = total + all_ref[d]
            o_ref[...] = total

    vm = pl.BlockSpec(memory_space=pltpu.VMEM)
    sems = [pltpu.SemaphoreType.DMA((N_DEV - 1,)), pltpu.SemaphoreType.DMA((N_DEV - 1,)), pltpu.SemaphoreType.DMA]
    if reduce:
        out_shape = jax.ShapeDtypeStruct((R, C), F32)
        scratch = [pltpu.VMEM((N_DEV, R, C), F32)] + sems
    else:
        out_shape = jax.ShapeDtypeStruct((N_DEV, R, C), F32)
        scratch = sems
    return pl.pallas_call(
        body, name=name, in_specs=[vm], out_specs=vm, out_shape=out_shape, scratch_shapes=scratch,
        compiler_params=pltpu.CompilerParams(vmem_limit_bytes=int(min(4 * N_DEV * R * C * 4 + 2 ** 24, 2 ** 25 + 2 ** 24))),
    )(v)


def _pair_sum(place, g, rb, name):
    _, _, Rh, C = g.shape
    tr = _row_tile(Rh, 256, 16)

    def body(place_ref, g_ref, r_ref, q_ref):
        q_ref[...] = (g_ref[...] + r_ref[...]).astype(BF16)

    blk = 2 * _nbytes((tr, C), F32) + _nbytes((tr, C), BF16)
    return pl.pallas_call(
        body, name=name,
        grid_spec=pltpu.PrefetchScalarGridSpec(
            num_scalar_prefetch=1, grid=(N_CHIPS, Rh // tr),
            in_specs=[pl.BlockSpec((None, None, tr, C), lambda j, i, p: (j, p[1], i, 0)),
                      pl.BlockSpec((None, tr, C), lambda j, i, p: (j, i, 0))],
            out_specs=pl.BlockSpec((None, tr, C), lambda j, i, p: (j, i, 0))),
        out_shape=jax.ShapeDtypeStruct((N_CHIPS, Rh, C), BF16),
        compiler_params=_params(("parallel", "parallel"), blk),
    )(place, g, rb)


def _chip_sum(place, g, rb, rc, name):
    _, _, Rh, C = g.shape
    tr = _row_tile(Rh, 256, 16)

    def body(place_ref, g_ref, r_ref, rc_ref, o_ref):
        total = g_ref[...] + r_ref[...]
        for r in range(3):
            total = total + rc_ref[r].astype(F32)
        o_ref[...] = total

    blk = 3 * _nbytes((tr, C), F32) + 3 * _nbytes((tr, C), BF16)
    return pl.pallas_call(
        body, name=name,
        grid_spec=pltpu.PrefetchScalarGridSpec(
            num_scalar_prefetch=1, grid=(Rh // tr,),
            in_specs=[pl.BlockSpec((None, None, tr, C), lambda i, p: (p[0], p[1], i, 0)),
                      pl.BlockSpec((None, tr, C), lambda i, p: (p[0], i, 0)),
                      pl.BlockSpec((3, tr, C), lambda i, p: (0, i, 0))],
            out_specs=pl.BlockSpec((None, tr, C), lambda i, p: (p[1], i, 0))),
        out_shape=jax.ShapeDtypeStruct((2, Rh, C), F32),
        compiler_params=_params(("parallel",), blk),
    )(place, g, rb, rc)


def _adamw_math(w, g, m, v):
    m = ADAM_B1 * m + (1.0 - ADAM_B1) * g
    v = ADAM_B2 * v + (1.0 - ADAM_B2) * jnp.square(g)
    m_hat = m / (1.0 - ADAM_B1 ** ADAM_STEP)
    v_hat = v / (1.0 - ADAM_B2 ** ADAM_STEP)
    delta = -ADAM_LR * (m_hat / (jnp.sqrt(v_hat) + ADAM_EPS) + ADAM_WD * w)
    return delta, m, v


def _adamw(w, g, m, v, name):
    R, C = w.shape
    tr = _row_tile(R, 256)

    def body(w_ref, g_ref, m_ref, v_ref, d_ref, nm_ref, nv_ref):
        d, nm, nv = _adamw_math(w_ref[...], g_ref[...], m_ref[...], v_ref[...])
        d_ref[...] = d
        nm_ref[...] = nm
        nv_ref[...] = nv

    spec = pl.BlockSpec((tr, C), lambda i: (i, 0))
    shp = jax.ShapeDtypeStruct((R, C), F32)
    return pl.pallas_call(
        body, name=name, grid=(R // tr,), in_specs=[spec] * 4, out_specs=[spec] * 3, out_shape=[shp] * 3,
        compiler_params=_params(("parallel",), 7 * _nbytes((tr, C), F32)),
    )(w, g, m, v)


def _adamw_small(ws, gs, ms, vs):
    n = len(ws)

    def body(*refs):
        for k in range(n):
            w_ref, g_ref, m_ref, v_ref = (refs[q * n + k] for q in range(4))
            d, nm, nv = _adamw_math(w_ref[...], g_ref[...], m_ref[...], v_ref[...])
            refs[4 * n + k][...] = d
            refs[5 * n + k][...] = nm
            refs[6 * n + k][...] = nv

    vm = pl.BlockSpec(memory_space=pltpu.VMEM)
    shapes = [jax.ShapeDtypeStruct(w.shape, F32) for w in ws]
    outs = pl.pallas_call(
        body, name="adamw_small", in_specs=[vm] * (4 * n), out_specs=[vm] * (3 * n), out_shape=shapes * 3,
    )(*ws, *gs, *ms, *vs)
    return outs[:n], outs[n:2 * n], outs[2 * n:]


def _pad_rows(a, rows):
    return jnp.pad(a, ((0, rows - a.shape[0]), (0, 0)))


def kernel(x, meta_tokens, ffn1_norm, ffn1_w_gate, ffn1_w_up, ffn1_w_down, mix_norm, w_in, b_in, conv_sc_w, conv_cf_w, conv_cf_b, ln_cf_g, ln_cf_b, w_out, ffn2_norm, ffn2_w_gate, ffn2_w_up, ffn2_w_down, final_norm, loss_target, m_meta_tokens, m_ffn1_norm, m_ffn1_w_gate, m_ffn1_w_up, m_ffn1_w_down, m_mix_norm, m_w_in, m_b_in, m_conv_sc_w, m_conv_cf_w, m_conv_cf_b, m_ln_cf_g, m_ln_cf_b, m_w_out, m_ffn2_norm, m_ffn2_w_gate, m_ffn2_w_up, m_ffn2_w_down, m_final_norm, v_meta_tokens, v_ffn1_norm, v_ffn1_w_gate, v_ffn1_w_up, v_ffn1_w_down, v_mix_norm, v_w_in, v_b_in, v_conv_sc_w, v_conv_cf_w, v_conv_cf_b, v_ln_cf_g, v_ln_cf_b, v_w_out, v_ffn2_norm, v_ffn2_w_gate, v_ffn2_w_up, v_ffn2_w_down, v_final_norm):
    xi, yi, ci = lax.axis_index("x"), lax.axis_index("y"), lax.axis_index("c")
    chip = 2 * xi + yi
    place = jnp.stack([chip, ci]).astype(jnp.int32)

    x2 = x[0]
    tgt = loss_target[0]
    S, D = x2.shape
    C1 = D // 2
    cs = conv_sc_w.shape[2]
    ksc, kcf = conv_sc_w.shape[1], conv_cf_w.shape[1]
    ms = meta_tokens.shape[1]

    rows_small = N_META + 8 + 32
    assert ksc <= 8 and kcf <= 32 and cs <= ms
    pack = jnp.concatenate([
        meta_tokens,
        jnp.pad(conv_sc_w[0], ((0, 8 - ksc), (0, ms - cs))),
        jnp.pad(conv_cf_w[0], ((0, 32 - kcf), (0, ms - cs)))], axis=0)
    everyone = _share_small(pack, False, "share_params")[0::2]
    meta_full = jnp.transpose(everyone[:, :N_META, :], (1, 0, 2)).reshape(N_META, D)
    wsc_full = jnp.transpose(everyone[:, N_META:N_META + ksc, :cs], (1, 0, 2)).reshape(ksc, C1)
    wcf_full = jnp.transpose(everyone[:, N_META + 8:N_META + 8 + kcf, :cs], (1, 0, 2)).reshape(kcf, C1)

    big = {"ffn1_w_gate": ffn1_w_gate, "ffn1_w_up": ffn1_w_up, "ffn1_w_down": ffn1_w_down, "w_in": w_in, "w_out": w_out,
           "ffn2_w_gate": ffn2_w_gate, "ffn2_w_up": ffn2_w_up, "ffn2_w_down": ffn2_w_down}
    big_m = {"ffn1_w_gate": m_ffn1_w_gate, "ffn1_w_up": m_ffn1_w_up, "ffn1_w_down": m_ffn1_w_down, "w_in": m_w_in,
             "w_out": m_w_out, "ffn2_w_gate": m_ffn2_w_gate, "ffn2_w_up": m_ffn2_w_up, "ffn2_w_down": m_ffn2_w_down}
    big_v = {"ffn1_w_gate": v_ffn1_w_gate, "ffn1_w_up": v_ffn1_w_up, "ffn1_w_down": v_ffn1_w_down, "w_in": v_w_in,
             "w_out": v_w_out, "ffn2_w_gate": v_ffn2_w_gate, "ffn2_w_up": v_ffn2_w_up, "ffn2_w_down": v_ffn2_w_down}
    buf = {nm: _cast_own_block(place, w[0], "cast_" + nm) for nm, w in big.items()}
    whole_weight = lambda g: g.reshape(N_CHIPS, 2 * g.shape[2], g.shape[3])
    group_mix, group_ffn2 = ["w_in", "w_out"], ["ffn2_w_gate", "ffn2_w_up", "ffn2_w_down"]

    wg1, wu1, wd1 = (whole_weight(g) for g in _gather_weights([buf[nm] for nm in ["ffn1_w_gate", "ffn1_w_up", "ffn1_w_down"]]))
    send_mix, recv_mix, thru_mix, token_mix = _start_copies(
        [buf[nm] for nm in group_mix], 3 * len(group_mix), _gather_copies, "gather_start_mix")
    send_ffn2, recv_ffn2, thru_ffn2, token_ffn2 = _start_copies(
        [buf[nm] for nm in group_ffn2], 3 * len(group_ffn2), _gather_copies, "gather_start_ffn2")
    F = N_CHIPS * wd1.shape[1]

    hs0, n1 = _embed_rms(x2, meta_full, ffn1_norm)
    g1, u1, a1 = _ffn_up(n1, wg1, wu1, token_mix + token_ffn2, "ffn1_up")
    hs1 = _ffn_down(a1, wd1.reshape(F, D), hs0, "ffn1_down")
    arrived = _wait_copies(send_mix, recv_mix, thru_mix, hs1, _gather_copies, "gather_wait_mix")
    win, wout = (whole_weight(g) for g in _forward_halves(arrived, "gather_forward_mix"))
    n2 = _rms(hs1, mix_norm, "rms_mix")
    u = _mix_in(n2, win, b_in)
    y = _mix_conv_fwd(u, wsc_full, wcf_full, conv_cf_b, ln_cf_g, ln_cf_b)
    hs2 = _mix_out(y, wout.reshape(D, D), hs1)
    arrived = _wait_copies(send_ffn2, recv_ffn2, thru_ffn2, hs2, _gather_copies, "gather_wait_ffn2")
    wg2, wu2, wd2 = (whole_weight(g) for g in _forward_halves(arrived, "gather_forward_ffn2"))
    n3 = _rms(hs2, ffn2_norm, "rms_ffn2")
    g2, u2, a2 = _ffn_up(n3, wg2, wu2, token_ffn2, "ffn2_up")
    hs3 = _ffn_down(a2, wd2.reshape(F, D), hs2, "ffn2_down")

    def reduce_start(group, tag):
        gs = [g for _, g in group]
        sib = _pair_exchange(gs, "pair_exchange_" + tag)
        sums = [_pair_sum(place, g, rb, "pair_sum_" + nm) for (nm, g), rb in zip(group, sib)]
        lands = [lax.empty((3,) + q.shape[1:], BF16) for q in sums]
        send, recv, thru, token = _start_copies(sums + lands, 3 * len(gs), _scatter_copies, "scatter_start_" + tag)
        return (group, sib, send, recv, thru, tag), token

    def reduce_finish(state, after):
        group, sib, send, recv, thru, tag = state
        lands = _wait_copies(send, recv, thru, after, _scatter_copies, "scatter_wait_" + tag)[len(group):]
        mine = [_chip_sum(place, g, rb, rc, "chip_sum_" + nm) for (nm, g), rb, rc in zip(group, sib, lands)]
        whole = _half_exchange(mine, "half_exchange_" + tag)
        out = {}
        for (nm, _), g in zip(group, whole):
            w = big[nm]
            g3d = g.reshape(w.shape)
            d, new_m, new_v = _adamw(w[0], g3d[0], big_m[nm][0], big_v[nm][0], "adamw_" + nm)
            out[nm] = (g3d, d[None], new_m[None], new_v[None])
        return out

    dhs3, df2, loss_row, d_final = _final_loss(hs3, final_norm.reshape(1, D), tgt)

    dg2, du2 = _ffn_bwd_act(df2, wd2, g2, u2, "ffn2_bwd_act")
    gw_d2 = _wgrad_down(a2, df2, "wgrad_ffn2_down")
    gw_g2 = _wgrad_cols(n3, [dg2], "wgrad_ffn2_gate")[0]
    gw_u2 = _wgrad_cols(n3, [du2], "wgrad_ffn2_up")[0]
    red_ffn2, token = reduce_start([("ffn2_w_gate", gw_g2), ("ffn2_w_up", gw_u2), ("ffn2_w_down", gw_d2)], "ffn2")
    dn3 = _nt_panel([dg2, du2], [wg2, wu2], token, "ffn2_bwd_in")
    dhs2, dm, d_ffn2 = _rms_bwd(dn3, hs2, ffn2_norm, dhs3, 1.0, "rms_bwd_ffn2")

    dy = _nt_panel([dm], [wout.reshape(1, D, D)], token, "mix_bwd_out")
    gw_out = _wgrad_out(y, dm)
    dz1, dcs, db, d_lg, d_lb, d_bcf = _mix_conv_bwd1(u, dy, wsc_full, wcf_full, conv_cf_b, ln_cf_g, ln_cf_b)
    du, d_bin, d_wsc, d_wcf = _mix_conv_bwd2(u, dz1, dcs, db, wsc_full, wcf_full)
    gw_in = _wgrad_cols(n2, [du], "wgrad_w_in")[0]
    red_mix, token = reduce_start([("w_in", gw_in), ("w_out", gw_out)], "mix")
    dn2 = _nt_panel([du], [win], token, "mix_bwd_in")
    dhs1, df1, d_mix = _rms_bwd(dn2, hs1, mix_norm, dhs2, FFN_RES_SCALE, "rms_bwd_mix")

    dg1, du1 = _ffn_bwd_act(df1, wd1, g1, u1, "ffn1_bwd_act")
    gw_d1 = _wgrad_down(a1, df1, "wgrad_ffn1_down")
    gw_g1 = _wgrad_cols(n1, [dg1], "wgrad_ffn1_gate")[0]
    gw_u1 = _wgrad_cols(n1, [du1], "wgrad_ffn1_up")[0]
    red_ffn1, token = reduce_start([("ffn1_w_gate", gw_g1), ("ffn1_w_up", gw_u1), ("ffn1_w_down", gw_d1)], "ffn1")
    dn1 = _nt_panel([dg1, du1], [wg1, wu1], token, "ffn1_bwd_in")
    grad_x, d_meta, d_ffn1 = _rms_bwd_first(dn1, hs0, ffn1_norm, dhs1)

    big_out = reduce_finish(red_ffn2, grad_x)
    big_out.update(reduce_finish(red_mix, big_out["ffn2_w_down"][1]))
    big_out.update(reduce_finish(red_ffn1, big_out["w_out"][1]))

    W = C1
    rows = lambda a: a.reshape(-1, W)
    parts = [rows(d_ffn1), rows(d_mix), rows(d_ffn2), rows(d_final), rows(d_bin), d_bcf, d_lg, d_lb,
             d_wsc, d_wcf, rows(d_meta), jnp.broadcast_to(loss_row[:, :1], (1, W))]
    sizes = [p.shape[0] for p in parts]
    total_rows = sum(sizes)
    packed = _pad_rows(jnp.concatenate(parts, axis=0), -(-total_rows // 8) * 8)
    summed = _share_small(packed, True, "sum_small")
    offs = [0]
    for n in sizes:
        offs.append(offs[-1] + n)
    piece = lambda k: summed[offs[k]:offs[k + 1]]
    loss = piece(11)[0, 0]
    g_ffn1, g_mix, g_ffn2 = (piece(k).reshape(1, D) for k in range(3))
    g_final = piece(3).reshape(1, D)
    g_bin = piece(4).reshape(1, -1)
    g_bcf, g_lg, g_lb = piece(5), piece(6), piece(7)
    g_wsc = lax.dynamic_slice_in_dim(piece(8), chip * cs, cs, axis=1)
    g_wcf = lax.dynamic_slice_in_dim(piece(9), chip * cs, cs, axis=1)
    g_meta = lax.dynamic_slice_in_dim(piece(10).reshape(N_META, D), chip * ms, ms, axis=1)

    small_names = ["meta_tokens", "ffn1_norm", "mix_norm", "b_in", "conv_sc_w", "conv_cf_w", "conv_cf_b", "ln_cf_g",
                   "ln_cf_b", "ffn2_norm", "final_norm"]
    small_w = [meta_tokens, ffn1_norm, mix_norm, b_in, conv_sc_w[0], conv_cf_w[0], conv_cf_b, ln_cf_g, ln_cf_b,
               ffn2_norm, final_norm.reshape(1, D)]
    small_g = [g_meta, g_ffn1, g_mix, g_bin, g_wsc, g_wcf, g_bcf, g_lg, g_lb, g_ffn2, g_final]
    small_m = [m_meta_tokens, m_ffn1_norm, m_mix_norm, m_b_in, m_conv_sc_w[0], m_conv_cf_w[0], m_conv_cf_b, m_ln_cf_g,
               m_ln_cf_b, m_ffn2_norm, m_final_norm.reshape(1, D)]
    small_v = [v_meta_tokens, v_ffn1_norm, v_mix_norm, v_b_in, v_conv_sc_w[0], v_conv_cf_w[0], v_conv_cf_b, v_ln_cf_g,
               v_ln_cf_b, v_ffn2_norm, v_final_norm.reshape(1, D)]
    s_d, s_m, s_v = _adamw_small(small_w, small_g, small_m, small_v)
    shapes = {"conv_sc_w": conv_sc_w.shape, "conv_cf_w": conv_cf_w.shape, "final_norm": final_norm.shape}
    small_out = {}
    for nm, g, d, m, v in zip(small_names, small_g, s_d, s_m, s_v):
        shp = shapes.get(nm, g.shape)
        small_out[nm] = tuple(t.reshape(shp) for t in (g, d, m, v))

    order = ["meta_tokens", "ffn1_norm", "ffn1_w_gate", "ffn1_w_up", "ffn1_w_down", "mix_norm", "w_in", "b_in",
             "conv_sc_w", "conv_cf_w", "conv_cf_b", "ln_cf_g", "ln_cf_b", "w_out", "ffn2_norm", "ffn2_w_gate",
             "ffn2_w_up", "ffn2_w_down", "final_norm"]
    res = {**big_out, **small_out}
    outs = [loss, grad_x[None]]
    for q in range(4):
        outs.extend(res[nm][q] for nm in order)
    return tuple(outs)
```

```python
import functools

import jax
import jax.numpy as jnp
from jax import lax
from jax.experimental import pallas as pl
from jax.experimental.pallas import tpu as pltpu

F32 = jnp.float32
BF16 = jnp.bfloat16
MESH = pl.DeviceIdType.MESH

N_META = 16
TT = 128
PAD = TT - N_META
HALO = 32
EPS = 1e-6
FFN_RES_SCALE = 0.5
N_CHIPS = 4
N_DEV = 8

ADAM_LR = 0.001
ADAM_B1 = 0.9
ADAM_B2 = 0.999
ADAM_EPS = 1e-08
ADAM_WD = 0.01
ADAM_STEP = 10

V7X_VMEM_BYTES = 64 * 2 ** 20
NT_DIMS = (((1,), (1,)), ((), ()))
TN_DIMS = (((0,), (0,)), ((), ()))


def _params(semantics, block_bytes):
    limit = min(2 * block_bytes + 16 * 2 ** 20, V7X_VMEM_BYTES - 6 * 2 ** 20)
    return pltpu.CompilerParams(dimension_semantics=semantics, vmem_limit_bytes=int(limit))


def _nbytes(shape, dtype):
    n = 1
    for d in shape:
        if d is not None:
            n *= d
    return n * jnp.dtype(dtype).itemsize


def _row_tile(rows, target, mult=8):
    best = None
    for t in range(mult, min(rows, target) + 1, mult):
        if rows % t == 0:
            best = t
    assert best is not None, (rows, target, mult)
    return best


def _sigmoid(v):
    return jax.nn.sigmoid(v)


def _dsilu(v, s):
    return s * (1.0 + v * (1.0 - s))


def _embed_rms(x2, meta, gain):
    S, D = x2.shape
    T = S + TT

    def body(x_ref, meta_ref, g_ref, hs_ref, n_ref):
        i = pl.program_id(0)

        @pl.when(i == 0)
        def _():
            hs_ref[...] = jnp.zeros_like(hs_ref)
            hs_ref[PAD:, :] = meta_ref[...]

        @pl.when(i > 0)
        def _():
            hs_ref[...] = x_ref[...]

        h = hs_ref[...]
        r = lax.rsqrt(jnp.mean(h * h, axis=-1, keepdims=True) + EPS)
        n_ref[...] = ((h * r) * g_ref[...]).astype(BF16)

    blk = _nbytes((TT, D), F32) * 2 + _nbytes((TT, D), BF16)
    return pl.pallas_call(
        body, name="embed_rms", grid=(T // TT,),
        in_specs=[pl.BlockSpec((TT, D), lambda i: (jnp.maximum(i - 1, 0), 0)),
                  pl.BlockSpec((N_META, D), lambda i: (0, 0)),
                  pl.BlockSpec((1, D), lambda i: (0, 0))],
        out_specs=[pl.BlockSpec((TT, D), lambda i: (i, 0)), pl.BlockSpec((TT, D), lambda i: (i, 0))],
        out_shape=[jax.ShapeDtypeStruct((T, D), F32), jax.ShapeDtypeStruct((T, D), BF16)],
        compiler_params=_params(("parallel",), blk),
    )(x2, meta, gain)


def _rms(hs, gain, name):
    T, D = hs.shape
    te = _row_tile(T, 384)

    def body(h_ref, g_ref, n_ref):
        h = h_ref[...]
        r = lax.rsqrt(jnp.mean(h * h, axis=-1, keepdims=True) + EPS)
        n_ref[...] = ((h * r) * g_ref[...]).astype(BF16)

    blk = _nbytes((te, D), F32) + _nbytes((te, D), BF16)
    return pl.pallas_call(
        body, name=name, grid=(T // te,),
        in_specs=[pl.BlockSpec((te, D), lambda i: (i, 0)), pl.BlockSpec((1, D), lambda i: (0, 0))],
        out_specs=pl.BlockSpec((te, D), lambda i: (i, 0)),
        out_shape=jax.ShapeDtypeStruct((T, D), BF16),
        compiler_params=_params(("parallel",), blk),
    )(hs, gain)


def _rms_bwd_math(dn, h, g):
    r = lax.rsqrt(jnp.mean(h * h, axis=-1, keepdims=True) + EPS)
    xh = h * r
    dgain = jnp.sum(dn * xh, axis=0, keepdims=True)
    dxh = dn * g
    dh = r * (dxh - xh * jnp.mean(dxh * xh, axis=-1, keepdims=True))
    return dh, dgain


def _rms_bwd(dn, hs, gain, dres, scale, name):
    T, D = hs.shape
    te = _row_tile(T, 384)

    def body(dn_ref, h_ref, g_ref, dres_ref, dhs_ref, dhb_ref, dg_ref):
        dh, dgain = _rms_bwd_math(dn_ref[...], h_ref[...], g_ref[...])
        d = dres_ref[...] + dh
        dhs_ref[...] = d
        dhb_ref[...] = (scale * d).astype(BF16)

        @pl.when(pl.program_id(0) == 0)
        def _():
            dg_ref[...] = jnp.zeros_like(dg_ref)

        dg_ref[...] += dgain

    blk = _nbytes((te, D), F32) * 4 + _nbytes((te, D), BF16)
    row = lambda i: (i, 0)
    return pl.pallas_call(
        body, name=name, grid=(T // te,),
        in_specs=[pl.BlockSpec((te, D), row), pl.BlockSpec((te, D), row), pl.BlockSpec((1, D), lambda i: (0, 0)),
                  pl.BlockSpec((te, D), row)],
        out_specs=[pl.BlockSpec((te, D), row), pl.BlockSpec((te, D), row), pl.BlockSpec((1, D), lambda i: (0, 0))],
        out_shape=[jax.ShapeDtypeStruct((T, D), F32), jax.ShapeDtypeStruct((T, D), BF16),
                   jax.ShapeDtypeStruct((1, D), F32)],
        compiler_params=_params(("arbitrary",), blk),
    )(dn, hs, gain, dres)


def _rms_bwd_first(dn, hs, gain, dres, after):
    T, D = hs.shape
    S = T - TT

    def body(dn_ref, h_ref, g_ref, dres_ref, after_ref, gx_ref, gm_ref, dg_ref):
        i = pl.program_id(0)
        dh, dgain = _rms_bwd_math(dn_ref[...], h_ref[...], g_ref[...])
        d = dres_ref[...] + dh

        @pl.when(i == 0)
        def _():
            dg_ref[...] = jnp.zeros_like(dg_ref)
            gm_ref[...] = d[PAD:, :]

        @pl.when(i > 0)
        def _():
            gx_ref[...] = d

        dg_ref[...] += dgain

    blk = _nbytes((TT, D), F32) * 4
    row = lambda i: (i, 0)
    return pl.pallas_call(
        body, name="rms_bwd_ffn1", grid=(T // TT,),
        in_specs=[pl.BlockSpec((TT, D), row), pl.BlockSpec((TT, D), row), pl.BlockSpec((1, D), lambda i: (0, 0)),
                  pl.BlockSpec((TT, D), row), TOKEN],
        out_specs=[pl.BlockSpec((TT, D), lambda i: (jnp.maximum(i - 1, 0), 0)),
                   pl.BlockSpec((N_META, D), lambda i: (0, 0)), pl.BlockSpec((1, D), lambda i: (0, 0))],
        out_shape=[jax.ShapeDtypeStruct((S, D), F32), jax.ShapeDtypeStruct((N_META, D), F32),
                   jax.ShapeDtypeStruct((1, D), F32)],
        compiler_params=_params(("arbitrary",), blk),
    )(dn, hs, gain, dres, after)


def _final_loss(hs, gain, tgt):
    T, D = hs.shape

    def body(h_ref, g_ref, t_ref, dhs_ref, dhb_ref, loss_ref, dg_ref):
        i = pl.program_id(0)
        h = h_ref[...]
        g = g_ref[...]
        r = lax.rsqrt(jnp.mean(h * h, axis=-1, keepdims=True) + EPS)
        xh = h * r
        e = jnp.where(i > 0, xh * g - t_ref[...], 0.0)
        tile_loss = jnp.sum(jnp.sum(e * e, axis=1, keepdims=True), axis=0, keepdims=True) * (0.5 / D)
        dout = e * (1.0 / D)
        dgain = jnp.sum(dout * xh, axis=0, keepdims=True)
        dxh = dout * g
        d = r * (dxh - xh * jnp.mean(dxh * xh, axis=-1, keepdims=True))
        dhs_ref[...] = d
        dhb_ref[...] = (FFN_RES_SCALE * d).astype(BF16)

        @pl.when(i == 0)
        def _():
            loss_ref[...] = jnp.zeros_like(loss_ref)
            dg_ref[...] = jnp.zeros_like(dg_ref)

        loss_ref[...] += jnp.broadcast_to(tile_loss, loss_ref.shape)
        dg_ref[...] += dgain

    blk = _nbytes((TT, D), F32) * 3 + _nbytes((TT, D), BF16)
    row = lambda i: (i, 0)
    return pl.pallas_call(
        body, name="final_loss", grid=(T // TT,),
        in_specs=[pl.BlockSpec((TT, D), row), pl.BlockSpec((1, D), lambda i: (0, 0)),
                  pl.BlockSpec((TT, D), lambda i: (jnp.maximum(i - 1, 0), 0))],
        out_specs=[pl.BlockSpec((TT, D), row), pl.BlockSpec((TT, D), row),
                   pl.BlockSpec((1, 128), lambda i: (0, 0)), pl.BlockSpec((1, D), lambda i: (0, 0))],
        out_shape=[jax.ShapeDtypeStruct((T, D), F32), jax.ShapeDtypeStruct((T, D), BF16),
                   jax.ShapeDtypeStruct((1, 128), F32), jax.ShapeDtypeStruct((1, D), F32)],
        compiler_params=_params(("arbitrary",), blk),
    )(hs, gain, tgt)


def _tm(T):
    return _row_tile(T, 384, 128)


TOKEN = pl.BlockSpec((8, 128), lambda *_: (0, 0))


def _ffn_up(n, wg, wu, after, name):
    T, D = n.shape
    Fs = wg.shape[2]
    tm = _tm(T)

    def body(n_ref, wg_ref, wu_ref, after_ref, g_ref, u_ref, a_ref):
        nn = n_ref[...]
        g = jnp.dot(nn, wg_ref[...], preferred_element_type=F32)
        u = jnp.dot(nn, wu_ref[...], preferred_element_type=F32)
        g_ref[...] = g.astype(BF16)
        u_ref[...] = u.astype(BF16)
        a_ref[...] = (jax.nn.silu(g) * u).astype(BF16)

    blk = _nbytes((tm, D), BF16) + 2 * _nbytes((D, Fs), BF16) + 3 * _nbytes((tm, Fs), BF16) + 2 * _nbytes((tm, Fs), F32)
    out = pl.BlockSpec((tm, Fs), lambda j, i: (i, j))
    shp = jax.ShapeDtypeStruct((T, N_CHIPS * Fs), BF16)
    return pl.pallas_call(
        body, name=name, grid=(N_CHIPS, T // tm),
        in_specs=[pl.BlockSpec((tm, D), lambda j, i: (i, 0)),
                  pl.BlockSpec((None, D, Fs), lambda j, i: (j, 0, 0)),
                  pl.BlockSpec((None, D, Fs), lambda j, i: (j, 0, 0)), TOKEN],
        out_specs=[out, out, out], out_shape=[shp, shp, shp],
        compiler_params=_params(("parallel", "parallel"), blk),
    )(n, wg, wu, after)


def _ffn_down(a, wd, hs, name):
    T, F = a.shape
    D = wd.shape[1]
    tm = _tm(T)
    tn = D // 2

    def body(a_ref, w_ref, h_ref, o_ref):
        o_ref[...] = h_ref[...] + FFN_RES_SCALE * jnp.dot(a_ref[...], w_ref[...], preferred_element_type=F32)

    blk = _nbytes((tm, F), BF16) + _nbytes((F, tn), BF16) + 3 * _nbytes((tm, tn), F32)
    return pl.pallas_call(
        body, name=name, grid=(D // tn, T // tm),
        in_specs=[pl.BlockSpec((tm, F), lambda n, i: (i, 0)), pl.BlockSpec((F, tn), lambda n, i: (0, n)),
                  pl.BlockSpec((tm, tn), lambda n, i: (i, n))],
        out_specs=pl.BlockSpec((tm, tn), lambda n, i: (i, n)),
        out_shape=jax.ShapeDtypeStruct((T, D), F32),
        compiler_params=_params(("parallel", "parallel"), blk),
    )(a, wd, hs)


def _mix_in(n, w, b):
    T, D = n.shape
    Ns = w.shape[2]
    tm = _tm(T)

    def body(n_ref, w_ref, b_ref, u_ref):
        u_ref[...] = jnp.dot(n_ref[...], w_ref[...], preferred_element_type=F32) + b_ref[...]

    blk = _nbytes((tm, D), BF16) + _nbytes((D, Ns), BF16) + 2 * _nbytes((tm, Ns), F32)
    return pl.pallas_call(
        body, name="mix_in", grid=(N_CHIPS, T // tm),
        in_specs=[pl.BlockSpec((tm, D), lambda j, i: (i, 0)), pl.BlockSpec((None, D, Ns), lambda j, i: (j, 0, 0)),
                  pl.BlockSpec((1, Ns), lambda j, i: (0, j))],
        out_specs=pl.BlockSpec((tm, Ns), lambda j, i: (i, j)),
        out_shape=jax.ShapeDtypeStruct((T, N_CHIPS * Ns), F32),
        compiler_params=_params(("parallel", "parallel"), blk),
    )(n, w, b)


def _mix_out(y, w, hs):
    T, D = y.shape
    tm = _tm(T)

    def body(y_ref, w_ref, h_ref, o_ref):
        o_ref[...] = h_ref[...] + jnp.dot(y_ref[...], w_ref[...], preferred_element_type=F32)

    blk = _nbytes((tm, D), BF16) + _nbytes((D, D), BF16) + 3 * _nbytes((tm, D), F32)
    return pl.pallas_call(
        body, name="mix_out", grid=(T // tm,),
        in_specs=[pl.BlockSpec((tm, D), lambda i: (i, 0)), pl.BlockSpec((D, D), lambda i: (0, 0)),
                  pl.BlockSpec((tm, D), lambda i: (i, 0))],
        out_specs=pl.BlockSpec((tm, D), lambda i: (i, 0)),
        out_shape=jax.ShapeDtypeStruct((T, D), F32),
        compiler_params=_params(("parallel",), blk),
    )(y, w, hs)


def _ffn_bwd_act(dfb, wd, g, u, after, name):
    T, D = dfb.shape
    Fs = wd.shape[1]
    tm = _tm(T)

    def body(d_ref, w_ref, g_ref, u_ref, after_ref, dg_ref, du_ref):
        da = lax.dot_general(d_ref[...], w_ref[...], NT_DIMS, preferred_element_type=F32)
        gv = g_ref[...].astype(F32)
        uv = u_ref[...].astype(F32)
        s = _sigmoid(gv)
        du_ref[...] = (da * (gv * s)).astype(BF16)
        dg_ref[...] = (da * uv * _dsilu(gv, s)).astype(BF16)

    blk = _nbytes((tm, D), BF16) + _nbytes((Fs, D), BF16) + 4 * _nbytes((tm, Fs), BF16) + 3 * _nbytes((tm, Fs), F32)
    io = pl.BlockSpec((tm, Fs), lambda j, i: (i, j))
    shp = jax.ShapeDtypeStruct((T, N_CHIPS * Fs), BF16)
    return pl.pallas_call(
        body, name=name, grid=(N_CHIPS, T // tm),
        in_specs=[pl.BlockSpec((tm, D), lambda j, i: (i, 0)), pl.BlockSpec((None, Fs, D), lambda j, i: (j, 0, 0)), io, io,
                  TOKEN],
        out_specs=[io, io], out_shape=[shp, shp],
        compiler_params=_params(("parallel", "parallel"), blk),
    )(dfb, wd, g, u, after)


def _nt_panel(lhs_list, w_list, after, name):
    T = lhs_list[0].shape[0]
    nsh, Dout, Ks = w_list[0].shape
    npair = len(lhs_list)
    tm = _tm(T)
    tn = Dout // 4 if npair * nsh * Ks > 4096 else Dout // 2

    def body(*refs):
        l_refs, w_refs, o_ref = refs[:npair], refs[npair:2 * npair], refs[2 * npair + 1]
        acc = None
        for p in range(npair):
            for j in range(nsh):
                part = lax.dot_general(l_refs[p][:, j * Ks:(j + 1) * Ks], w_refs[p][j], NT_DIMS,
                                       preferred_element_type=F32)
                acc = part if acc is None else acc + part
        o_ref[...] = acc

    blk = npair * (_nbytes((tm, nsh * Ks), BF16) + _nbytes((nsh, tn, Ks), BF16)) + 3 * _nbytes((tm, tn), F32)
    return pl.pallas_call(
        body, name=name, grid=(Dout // tn, T // tm),
        in_specs=[pl.BlockSpec((tm, nsh * Ks), lambda n, i: (i, 0))] * npair
                 + [pl.BlockSpec((nsh, tn, Ks), lambda n, i: (0, n, 0))] * npair + [TOKEN],
        out_specs=pl.BlockSpec((tm, tn), lambda n, i: (i, n)),
        out_shape=jax.ShapeDtypeStruct((T, Dout), F32),
        compiler_params=_params(("parallel", "parallel"), blk),
    )(*lhs_list, *w_list, after)


def _tn_call(name, grid, lhs, lhs_spec, rhs_list, rhs_specs, out_shapes, out_specs, blk):
    nr = len(rhs_list)

    def body(*refs):
        l_ref, r_refs, o_refs = refs[0], refs[1:1 + nr], refs[1 + nr:]
        k = pl.program_id(len(grid) - 1)
        lv = l_ref[...]
        for q in range(nr):
            part = lax.dot_general(lv, r_refs[q][...], TN_DIMS, preferred_element_type=F32)
            part = part.reshape(o_refs[q].shape)

            @pl.when(k == 0)
            def _(o=o_refs[q], part=part):
                o[...] = part

            @pl.when(k > 0)
            def _(o=o_refs[q], part=part):
                o[...] += part

    return pl.pallas_call(
        body, name=name, grid=grid, in_specs=[lhs_spec] + rhs_specs, out_specs=out_specs, out_shape=out_shapes,
        compiler_params=_params(("parallel",) * (len(grid) - 1) + ("arbitrary",), blk),
    )(lhs, *rhs_list)


def _tk(T):
    return _row_tile(T, 1408, 128)


def _wgrad_cols(n, rhs_list, name):
    T, D = n.shape
    Ns = rhs_list[0].shape[1] // N_CHIPS
    tk = _tk(T)
    nr = len(rhs_list)
    blk = _nbytes((tk, D // 2), BF16) + nr * (_nbytes((tk, Ns), BF16) + 2 * _nbytes((D // 2, Ns), F32))
    return _tn_call(
        name, (N_CHIPS, 2, T // tk), n, pl.BlockSpec((tk, D // 2), lambda j, m, k: (k, m)),
        rhs_list, [pl.BlockSpec((tk, Ns), lambda j, m, k: (k, j))] * nr,
        [jax.ShapeDtypeStruct((N_CHIPS, 2, D // 2, Ns), F32)] * nr,
        [pl.BlockSpec((None, None, D // 2, Ns), lambda j, m, k: (j, m, 0, 0))] * nr, blk)


def _wgrad_down(a, dfb, name):
    T, F = a.shape
    D = dfb.shape[1]
    Fs = F // N_CHIPS
    tk = _tk(T)
    tn = D // 2
    blk = _nbytes((tk, Fs), BF16) + _nbytes((tk, tn), BF16) + 2 * _nbytes((Fs, tn), F32)
    return _tn_call(
        name, (N_CHIPS, D // tn, T // tk), a, pl.BlockSpec((tk, Fs), lambda j, n, k: (k, j)),
        [dfb], [pl.BlockSpec((tk, tn), lambda j, n, k: (k, n))],
        [jax.ShapeDtypeStruct((N_CHIPS, 2, Fs // 2, D), F32)],
        [pl.BlockSpec((None, 2, Fs // 2, tn), lambda j, n, k: (j, 0, 0, n))], blk)[0]


def _wgrad_out(y, dmb):
    T, D = y.shape
    tk = _tk(T)
    tn = D // 2
    rows = D // (2 * N_CHIPS)
    blk = _nbytes((tk, D // 2), BF16) + _nbytes((tk, tn), BF16) + 2 * _nbytes((D // 2, tn), F32)
    return _tn_call(
        "wgrad_w_out", (2, D // tn, T // tk), y, pl.BlockSpec((tk, D // 2), lambda m, n, k: (k, m)),
        [dmb], [pl.BlockSpec((tk, tn), lambda m, n, k: (k, n))],
        [jax.ShapeDtypeStruct((N_CHIPS, 2, rows, D), F32)],
        [pl.BlockSpec((2, 2, rows, tn), lambda m, n, k: (m, 0, 0, n))], blk)[0]


def _row_masks(i, last):
    rows = i * TT + lax.broadcasted_iota(jnp.int32, (TT, 1), 0)
    prows = i * TT - HALO + lax.broadcasted_iota(jnp.int32, (HALO, 1), 0)
    return rows >= PAD, (prows >= PAD) & (i > 0), i < last


def _conv_inputs(u, up, mask_c, mask_p, zbuf, pbuf, C1):
    b, c, v, a, g = (u[:, k * C1:(k + 1) * C1] for k in range(5))
    cp, vp, ap, gp = (up[:, k * C1:(k + 1) * C1] for k in range(1, 5))
    sg = _sigmoid(g)
    pbuf[0:HALO, :] = jnp.where(mask_p, cp * vp, 0.0)
    pbuf[HALO:, :] = jnp.where(mask_c, c * v, 0.0)
    zbuf[0:HALO, :] = jnp.where(mask_p, ap * _sigmoid(gp), 0.0)
    zbuf[HALO:, :] = jnp.where(mask_c, a * sg, 0.0)
    return b, c, v, a, sg


def _causal_conv(w_ref, buf):
    K = w_ref.shape[0]
    acc = None
    for k in range(K):
        lo = HALO - (K - 1) + k
        term = w_ref[k:k + 1, :] * buf[lo:lo + TT, :]
        acc = term if acc is None else acc + term
    return acc


def _anticausal_conv(w_ref, buf):
    K = w_ref.shape[0]
    acc = None
    for k in range(K):
        lo = K - 1 - k
        term = w_ref[k:k + 1, :] * buf[lo:lo + TT, :]
        acc = term if acc is None else acc + term
    return acc


def _conv_weight_sums(dw_ref, dy, buf):
    K = dw_ref.shape[0]
    for k in range(K):
        lo = HALO - (K - 1) + k
        dw_ref[k:k + 1, :] += jnp.sum(dy * buf[lo:lo + TT, :], axis=0, keepdims=True)


def _layernorm_stats(z1):
    mu = jnp.mean(z1, axis=-1, keepdims=True)
    zc = z1 - mu
    rs = lax.rsqrt(jnp.mean(zc * zc, axis=-1, keepdims=True) + EPS)
    return zc * rs, rs


def _mixer_specs(T, DIN, C1, ksc, kcf):
    cur = pl.BlockSpec((TT, DIN), lambda i: (i, 0))
    prev = pl.BlockSpec((HALO, DIN), lambda i: (jnp.maximum(i * (TT // HALO) - 1, 0), 0))
    full = lambda r: pl.BlockSpec((r, C1), lambda i: (0, 0))
    return cur, prev, [full(ksc), full(kcf), full(1), full(1), full(1)]


def _mix_conv_fwd(u, wsc, wcf, bcf, lg, lb):
    T, DIN = u.shape
    C1 = DIN // 5
    last = T // TT - 1

    def body(u_ref, up_ref, wsc_ref, wcf_ref, bcf_ref, lg_ref, lb_ref, y_ref, zbuf, pbuf):
        i = pl.program_id(0)
        mask_c, mask_p, _ = _row_masks(i, last)
        b, _, _, _, _ = _conv_inputs(u_ref[...], up_ref[...], mask_c, mask_p, zbuf, pbuf, C1)
        cs = _causal_conv(wsc_ref, pbuf)
        z1 = _causal_conv(wcf_ref, zbuf) + bcf_ref[...]
        zh, _ = _layernorm_stats(z1)
        ln = zh * lg_ref[...] + lb_ref[...]
        y_ref[:, 0:C1] = jnp.where(mask_c, b * cs, 0.0).astype(BF16)
        y_ref[:, C1:] = jnp.where(mask_c, jax.nn.silu(ln), 0.0).astype(BF16)

    cur, prev, small = _mixer_specs(T, DIN, C1, wsc.shape[0], wcf.shape[0])
    blk = _nbytes((TT + HALO, DIN), F32) + _nbytes((TT, 2 * C1), BF16) + 12 * _nbytes((TT + HALO, C1), F32)
    return pl.pallas_call(
        body, name="mix_conv_fwd", grid=(T // TT,),
        in_specs=[cur, prev] + small,
        out_specs=pl.BlockSpec((TT, 2 * C1), lambda i: (i, 0)),
        out_shape=jax.ShapeDtypeStruct((T, 2 * C1), BF16),
        scratch_shapes=[pltpu.VMEM((TT + HALO, C1), F32), pltpu.VMEM((TT + HALO, C1), F32)],
        compiler_params=_params(("arbitrary",), blk),
    )(u, u, wsc, wcf, bcf, lg, lb)


def _mix_conv_bwd1(u, dy, wsc, wcf, bcf, lg, lb):
    T, DIN = u.shape
    C1 = DIN // 5
    last = T // TT - 1

    def body(u_ref, up_ref, dy_ref, wsc_ref, wcf_ref, bcf_ref, lg_ref, lb_ref,
             dz1_ref, dcs_ref, db_ref, dlg_ref, dlb_ref, dbcf_ref, zbuf, pbuf):
        i = pl.program_id(0)
        mask_c, mask_p, _ = _row_masks(i, last)
        b, _, _, _, _ = _conv_inputs(u_ref[...], up_ref[...], mask_c, mask_p, zbuf, pbuf, C1)
        cs = _causal_conv(wsc_ref, pbuf)
        z1 = _causal_conv(wcf_ref, zbuf) + bcf_ref[...]
        zh, rs = _layernorm_stats(z1)
        ln = zh * lg_ref[...] + lb_ref[...]
        dy = dy_ref[...]
        dysc = jnp.where(mask_c, dy[:, 0:C1], 0.0)
        dycf = jnp.where(mask_c, dy[:, C1:], 0.0)
        db_ref[...] = (dysc * cs).astype(BF16)
        dcs_ref[...] = dysc * b
        dl = dycf * _dsilu(ln, _sigmoid(ln))
        dzh = dl * lg_ref[...]
        dz1 = rs * (dzh - jnp.mean(dzh, axis=-1, keepdims=True) - zh * jnp.mean(dzh * zh, axis=-1, keepdims=True))
        dz1_ref[...] = dz1

        @pl.when(i == 0)
        def _():
            dlg_ref[...] = jnp.zeros_like(dlg_ref)
            dlb_ref[...] = jnp.zeros_like(dlb_ref)
            dbcf_ref[...] = jnp.zeros_like(dbcf_ref)

        dlg_ref[...] += jnp.sum(dl * zh, axis=0, keepdims=True)
        dlb_ref[...] += jnp.sum(dl, axis=0, keepdims=True)
        dbcf_ref[...] += jnp.sum(dz1, axis=0, keepdims=True)

    cur, prev, small = _mixer_specs(T, DIN, C1, wsc.shape[0], wcf.shape[0])
    tile = lambda: pl.BlockSpec((TT, C1), lambda i: (i, 0))
    vec = lambda: pl.BlockSpec((1, C1), lambda i: (0, 0))
    blk = _nbytes((TT + HALO, DIN), F32) + 4 * _nbytes((TT, C1), F32) + 16 * _nbytes((TT + HALO, C1), F32)
    return pl.pallas_call(
        body, name="mix_conv_bwd1", grid=(T // TT,),
        in_specs=[cur, prev, pl.BlockSpec((TT, 2 * C1), lambda i: (i, 0))] + small,
        out_specs=[tile(), tile(), tile(), vec(), vec(), vec()],
        out_shape=[jax.ShapeDtypeStruct((T, C1), F32), jax.ShapeDtypeStruct((T, C1), F32),
                   jax.ShapeDtypeStruct((T, C1), BF16)] + [jax.ShapeDtypeStruct((1, C1), F32)] * 3,
        scratch_shapes=[pltpu.VMEM((TT + HALO, C1), F32), pltpu.VMEM((TT + HALO, C1), F32)],
        compiler_params=_params(("arbitrary",), blk),
    )(u, u, dy, wsc, wcf, bcf, lg, lb)


def _mix_conv_bwd2(u, dz1, dcs, db, wsc, wcf):
    T, DIN = u.shape
    C1 = DIN // 5
    last = T // TT - 1
    ksc, kcf = wsc.shape[0], wcf.shape[0]

    def body(u_ref, up_ref, dz_ref, dzn_ref, dc_ref, dcn_ref, db_ref, wsc_ref, wcf_ref,
             du_ref, dbin_ref, dwsc_ref, dwcf_ref, zbuf, pbuf, dzbuf, dcbuf):
        i = pl.program_id(0)
        mask_c, mask_p, has_next = _row_masks(i, last)
        _, c, v, a, sg = _conv_inputs(u_ref[...], up_ref[...], mask_c, mask_p, zbuf, pbuf, C1)
        dz1 = dz_ref[...]
        dcs = dc_ref[...]
        dzbuf[0:TT, :] = dz1
        dzbuf[TT:, :] = jnp.where(has_next, dzn_ref[...], 0.0)
        dcbuf[0:TT, :] = dcs
        dcbuf[TT:, :] = jnp.where(has_next, dcn_ref[...], 0.0)

        @pl.when(i == 0)
        def _():
            dbin_ref[...] = jnp.zeros_like(dbin_ref)
            dwsc_ref[...] = jnp.zeros_like(dwsc_ref)
            dwcf_ref[...] = jnp.zeros_like(dwcf_ref)

        _conv_weight_sums(dwcf_ref, dz1, zbuf)
        _conv_weight_sums(dwsc_ref, dcs, pbuf)
        dz0 = jnp.where(mask_c, _anticausal_conv(wcf_ref, dzbuf), 0.0)
        dp = jnp.where(mask_c, _anticausal_conv(wsc_ref, dcbuf), 0.0)
        parts = (db_ref[...].astype(F32), dp * v, dp * c, dz0 * sg, dz0 * a * sg * (1.0 - sg))
        for k, part in enumerate(parts):
            du_ref[:, k * C1:(k + 1) * C1] = part.astype(BF16)
            dbin_ref[:, k * C1:(k + 1) * C1] += jnp.sum(part, axis=0, keepdims=True)

    cur, prev, small = _mixer_specs(T, DIN, C1, ksc, kcf)
    tile = lambda: pl.BlockSpec((TT, C1), lambda i: (i, 0))
    nxt = lambda: pl.BlockSpec((HALO, C1), lambda i: (jnp.minimum((i + 1) * (TT // HALO), T // HALO - 1), 0))
    blk = (_nbytes((TT + HALO, DIN), F32) + _nbytes((TT, DIN), BF16) + 5 * _nbytes((TT, C1), F32)
           + 16 * _nbytes((TT + HALO, C1), F32))
    buf = lambda: pltpu.VMEM((TT + HALO, C1), F32)
    return pl.pallas_call(
        body, name="mix_conv_bwd2", grid=(T // TT,),
        in_specs=[cur, prev, tile(), nxt(), tile(), nxt(), tile(), small[0], small[1]],
        out_specs=[pl.BlockSpec((TT, DIN), lambda i: (i, 0)), pl.BlockSpec((1, DIN), lambda i: (0, 0)),
                   pl.BlockSpec((ksc, C1), lambda i: (0, 0)), pl.BlockSpec((kcf, C1), lambda i: (0, 0))],
        out_shape=[jax.ShapeDtypeStruct((T, DIN), BF16), jax.ShapeDtypeStruct((1, DIN), F32),
                   jax.ShapeDtypeStruct((ksc, C1), F32), jax.ShapeDtypeStruct((kcf, C1), F32)],
        scratch_shapes=[buf(), buf(), buf(), buf()],
        compiler_params=_params(("arbitrary",), blk),
    )(u, u, dz1, dz1, dcs, dcs, db, wsc, wcf)


def _place():
    x, y, c = lax.axis_index("x"), lax.axis_index("y"), lax.axis_index("c")
    chips = [(1 - x, y), (x, 1 - y), (1 - x, 1 - y)]
    return x, y, c, chips


ANY = pl.BlockSpec(memory_space=pl.ANY)


def _cast_own_block(place, w, name):
    R, C = w.shape
    tr = _row_tile(R // 2, 256, 16)
    nblk = R // 2 // tr

    def body(place_ref, w_ref, o_ref):
        o_ref[...] = w_ref[...].astype(BF16)

    return pl.pallas_call(
        body, name=name,
        grid_spec=pltpu.PrefetchScalarGridSpec(
            num_scalar_prefetch=1, grid=(2, nblk),
            in_specs=[pl.BlockSpec((tr, C), lambda h, i, p: (h * nblk + i, 0))],
            out_specs=pl.BlockSpec((None, None, tr, C), lambda h, i, p: (p[0], h, i, 0))),
        out_shape=jax.ShapeDtypeStruct((N_CHIPS, 2, R // 2, C), BF16),
        compiler_params=_params(("parallel", "parallel"), _nbytes((tr, C), F32) + _nbytes((tr, C), BF16)),
    )(place, w)


def _gather_weights(bufs):
    nw = len(bufs)

    def body(*refs):
        o_refs = refs[nw:2 * nw]
        send, recv = refs[2 * nw:]
        x, y, c, chips = _place()
        s = 2 * x + y
        sib = (x, y, 1 - c)

        def remote(w, k, blk, half, to):
            ref = o_refs[w].at[blk, half]
            return pltpu.make_async_remote_copy(src_ref=ref, dst_ref=ref, send_sem=send.at[6 * w + k],
                                                recv_sem=recv.at[6 * w + k], device_id=to, device_id_type=MESH)

        sends = []
        for w in range(nw):
            for r, (tx, ty) in enumerate(chips):
                cp = remote(w, r, s, c, (tx, ty, c))
                cp.start()
                sends.append(cp)
        for w in range(nw):
            for r, (tx, ty) in enumerate(chips):
                sr = 2 * tx + ty
                remote(w, r, sr, c, (tx, ty, c)).wait_recv()
                cp = remote(w, 3 + r, sr, c, sib)
                cp.start()
                sends.append(cp)
        for w in range(nw):
            for r, (tx, ty) in enumerate(chips):
                remote(w, 3 + r, 2 * tx + ty, 1 - c, sib).wait_recv()
        for cp in sends:
            cp.wait_send()

    return pl.pallas_call(
        body, name="gather_weights", in_specs=[ANY] * nw, out_specs=[ANY] * nw,
        out_shape=[jax.ShapeDtypeStruct(b.shape, b.dtype) for b in bufs],
        input_output_aliases={w: w for w in range(nw)},
        scratch_shapes=[pltpu.SemaphoreType.DMA((6 * nw,)), pltpu.SemaphoreType.DMA((6 * nw,))],
    )(*bufs)


HBM = pl.BlockSpec(memory_space=pltpu.HBM)
SEM = pl.BlockSpec(memory_space=pltpu.SEMAPHORE)
EFFECT = pltpu.SideEffectType.DATAFLOW_SIDE_EFFECTING


def _gather_copies(refs, send, recv):
    x, y, c, chips = _place()
    s = 2 * x + y
    return [pltpu.make_async_remote_copy(src_ref=ref.at[s, c], dst_ref=ref.at[s, c], send_sem=send.at[3 * w + r],
                                         recv_sem=recv.at[3 * w + r], device_id=(tx, ty, c), device_id_type=MESH)
            for w, ref in enumerate(refs) for r, (tx, ty) in enumerate(chips)]


def _scatter_copies(refs, send, recv):
    x, y, c, chips = _place()
    nw = len(refs) // 2
    return [pltpu.make_async_remote_copy(src_ref=refs[w].at[2 * tx + ty], dst_ref=refs[nw + w].at[r],
                                         send_sem=send.at[3 * w + r], recv_sem=recv.at[3 * w + r],
                                         device_id=(tx, ty, c), device_id_type=MESH)
            for w in range(nw) for r, (tx, ty) in enumerate(chips)]


def _pair_copies(refs, send, recv):
    x, y, c, _ = _place()
    nw = len(refs) // 2
    return [pltpu.make_async_remote_copy(src_ref=refs[w].at[j, 1 - c], dst_ref=refs[nw + w].at[j],
                                         send_sem=send.at[N_CHIPS * w + j], recv_sem=recv.at[N_CHIPS * w + j],
                                         device_id=(x, y, 1 - c), device_id_type=MESH)
            for w in range(nw) for j in range(N_CHIPS)]


def _start_copies(bufs, after, ncopies, make_copies, name):
    n = len(bufs)

    def body(*refs):
        in_refs, send, recv, token = refs[:n], refs[n + 1], refs[n + 2], refs[2 * n + 3]
        for cp in make_copies(in_refs, send, recv):
            cp.start()
        token[...] = jnp.zeros_like(token)

    outs = pl.pallas_call(
        body, name=name, in_specs=[HBM] * n + [ANY],
        out_specs=[SEM, SEM] + [HBM] * n + [pl.BlockSpec(memory_space=pltpu.VMEM)],
        out_shape=[pltpu.SemaphoreType.DMA((ncopies,)), pltpu.SemaphoreType.DMA((ncopies,))]
                  + [pltpu.HBM(b.shape, b.dtype) for b in bufs] + [jax.ShapeDtypeStruct((8, 128), F32)],
        input_output_aliases={k: 2 + k for k in range(n)},
        compiler_params=pltpu.CompilerParams(has_side_effects=EFFECT),
    )(*[pltpu.with_memory_space_constraint(b, pltpu.HBM) for b in bufs], after)
    return outs[0], outs[1], list(outs[2:2 + n]), outs[2 + n]


def _wait_copies(send, recv, bufs, after, make_copies, name):
    n = len(bufs)

    def body(*refs):
        in_refs, send_ref, recv_ref = refs[:n], refs[n], refs[n + 1]
        for cp in make_copies(in_refs, send_ref, recv_ref):
            cp.wait_send()
            cp.wait_recv()

    outs = pl.pallas_call(
        body, name=name, in_specs=[HBM] * n + [SEM, SEM, ANY], out_specs=[HBM] * n,
        out_shape=[pltpu.HBM(b.shape, b.dtype) for b in bufs],
        input_output_aliases={k: k for k in range(n)},
        compiler_params=pltpu.CompilerParams(has_side_effects=EFFECT),
    )(*bufs, send, recv, after)
    return list(outs)


def _forward_halves(bufs, name):
    nw = len(bufs)

    def body(*refs):
        o_refs = refs[nw:2 * nw]
        send, recv = refs[2 * nw:]
        x, y, c, chips = _place()
        sib = (x, y, 1 - c)
        copies = []
        for w in range(nw):
            for r, (tx, ty) in enumerate(chips):
                ref = o_refs[w].at[2 * tx + ty, c]
                cp = pltpu.make_async_remote_copy(src_ref=ref, dst_ref=ref, send_sem=send.at[3 * w + r],
                                                  recv_sem=recv.at[3 * w + r], device_id=sib, device_id_type=MESH)
                cp.start()
                copies.append(cp)
        for w in range(nw):
            for r, (tx, ty) in enumerate(chips):
                ref = o_refs[w].at[2 * tx + ty, 1 - c]
                pltpu.make_async_remote_copy(src_ref=ref, dst_ref=ref, send_sem=send.at[3 * w + r],
                                             recv_sem=recv.at[3 * w + r], device_id=sib, device_id_type=MESH).wait_recv()
        for cp in copies:
            cp.wait_send()

    return pl.pallas_call(
        body, name=name, in_specs=[ANY] * nw, out_specs=[ANY] * nw,
        out_shape=[jax.ShapeDtypeStruct(b.shape, b.dtype) for b in bufs],
        input_output_aliases={w: w for w in range(nw)},
        scratch_shapes=[pltpu.SemaphoreType.DMA((3 * nw,)), pltpu.SemaphoreType.DMA((3 * nw,))],
    )(*bufs)


def _half_exchange(hs, name):
    nw = len(hs)

    def body(*refs):
        o_refs = refs[nw:2 * nw]
        send, recv = refs[2 * nw:]
        x, y, c, _ = _place()
        sib = (x, y, 1 - c)
        copies = []
        for w in range(nw):
            cp = pltpu.make_async_remote_copy(src_ref=o_refs[w].at[c], dst_ref=o_refs[w].at[c], send_sem=send.at[w],
                                              recv_sem=recv.at[w], device_id=sib, device_id_type=MESH)
            cp.start()
            copies.append(cp)
        for w, cp in enumerate(copies):
            cp.wait_send()
            pltpu.make_async_remote_copy(src_ref=o_refs[w].at[c], dst_ref=o_refs[w].at[1 - c], send_sem=send.at[w],
                                         recv_sem=recv.at[w], device_id=sib, device_id_type=MESH).wait_recv()

    return pl.pallas_call(
        body, name=name, in_specs=[ANY] * nw, out_specs=[ANY] * nw,
        out_shape=[jax.ShapeDtypeStruct(h.shape, F32) for h in hs],
        input_output_aliases={w: w for w in range(nw)},
        scratch_shapes=[pltpu.SemaphoreType.DMA((nw,)), pltpu.SemaphoreType.DMA((nw,))],
    )(*hs)


def _share_small(v, reduce, name):
    R, C = v.shape

    def body(v_ref, o_ref, *scratch):
        if reduce:
            all_ref, send, recv, lsem = scratch
        else:
            all_ref = o_ref
            send, recv, lsem = scratch
        x, y, c, _ = _place()
        me = 4 * x + 2 * y + c
        loc = pltpu.make_async_copy(v_ref, all_ref.at[me], lsem)
        loc.start()
        copies = []
        for k in range(1, N_DEV):
            kx, ky, kc = (k >> 2) & 1, (k >> 1) & 1, k & 1
            peer = (x ^ kx, y ^ ky, c ^ kc)
            cp = pltpu.make_async_remote_copy(src_ref=v_ref, dst_ref=all_ref.at[me], send_sem=send.at[k - 1],
                                              recv_sem=recv.at[k - 1], device_id=peer, device_id_type=MESH)
            cp.start()
            copies.append(cp)
        for k in range(1, N_DEV):
            kx, ky, kc = (k >> 2) & 1, (k >> 1) & 1, k & 1
            src = 4 * (x ^ kx) + 2 * (y ^ ky) + (c ^ kc)
            pltpu.make_async_remote_copy(src_ref=v_ref, dst_ref=all_ref.at[src], send_sem=send.at[k - 1],
                                         recv_sem=recv.at[k - 1], device_id=(x, y, c), device_id_type=MESH).wait_recv()
        for cp in copies:
            cp.wait_send()
        loc.wait()
        if reduce:
            total = all_ref[0]
            for d in range(1, N_DEV):
                total = total + all_ref[d]
            o_ref[...] = total

    vm = pl.BlockSpec(memory_space=pltpu.VMEM)
    sems = [pltpu.SemaphoreType.DMA((N_DEV - 1,)), pltpu.SemaphoreType.DMA((N_DEV - 1,)), pltpu.SemaphoreType.DMA]
    if reduce:
        out_shape = jax.ShapeDtypeStruct((R, C), F32)
        scratch = [pltpu.VMEM((N_DEV, R, C), F32)] + sems
    else:
        out_shape = jax.ShapeDtypeStruct((N_DEV, R, C), F32)
        scratch = sems
    return pl.pallas_call(
        body, name=name, in_specs=[vm], out_specs=vm, out_shape=out_shape, scratch_shapes=scratch,
        compiler_params=pltpu.CompilerParams(vmem_limit_bytes=int(min(4 * N_DEV * R * C * 4 + 2 ** 24, 2 ** 25 + 2 ** 24))),
    )(v)


def _pair_sum(place, g, rb, name):
    _, _, Rh, C = g.shape
    tr = _row_tile(Rh, 256, 16)

    def body(place_ref, g_ref, r_ref, q_ref):
        q_ref[...] = (g_ref[...] + r_ref[...]).astype(BF16)

    blk = 2 * _nbytes((tr, C), F32) + _nbytes((tr, C), BF16)
    return pl.pallas_call(
        body, name=name,
        grid_spec=pltpu.PrefetchScalarGridSpec(
            num_scalar_prefetch=1, grid=(N_CHIPS, Rh // tr),
            in_specs=[pl.BlockSpec((None, None, tr, C), lambda j, i, p: (j, p[1], i, 0)),
                      pl.BlockSpec((None, tr, C), lambda j, i, p: (j, i, 0))],
            out_specs=pl.BlockSpec((None, tr, C), lambda j, i, p: (j, i, 0))),
        out_shape=jax.ShapeDtypeStruct((N_CHIPS, Rh, C), BF16),
        compiler_params=_params(("parallel", "parallel"), blk),
    )(place, g, rb)


def _chip_sum(place, g, rb, rc, name):
    _, _, Rh, C = g.shape
    tr = _row_tile(Rh, 256, 16)

    def body(place_ref, g_ref, r_ref, rc_ref, o_ref):
        total = g_ref[...] + r_ref[...]
        for r in range(3):
            total = total + rc_ref[r].astype(F32)
        o_ref[...] = total

    blk = 3 * _nbytes((tr, C), F32) + 3 * _nbytes((tr, C), BF16)
    return pl.pallas_call(
        body, name=name,
        grid_spec=pltpu.PrefetchScalarGridSpec(
            num_scalar_prefetch=1, grid=(Rh // tr,),
            in_specs=[pl.BlockSpec((None, None, tr, C), lambda i, p: (p[0], p[1], i, 0)),
                      pl.BlockSpec((None, tr, C), lambda i, p: (p[0], i, 0)),
                      pl.BlockSpec((3, tr, C), lambda i, p: (0, i, 0))],
            out_specs=pl.BlockSpec((None, tr, C), lambda i, p: (p[1], i, 0))),
        out_shape=jax.ShapeDtypeStruct((2, Rh, C), F32),
        compiler_params=_params(("parallel",), blk),
    )(place, g, rb, rc)


def _adamw_math(w, g, m, v):
    m = ADAM_B1 * m + (1.0 - ADAM_B1) * g
    v = ADAM_B2 * v + (1.0 - ADAM_B2) * jnp.square(g)
    m_hat = m / (1.0 - ADAM_B1 ** ADAM_STEP)
    v_hat = v / (1.0 - ADAM_B2 ** ADAM_STEP)
    delta = -ADAM_LR * (m_hat / (jnp.sqrt(v_hat) + ADAM_EPS) + ADAM_WD * w)
    return delta, m, v


def _adamw(w, g, m, v, name):
    R, C = w.shape
    tr = _row_tile(R, 256)

    def body(w_ref, g_ref, m_ref, v_ref, d_ref, nm_ref, nv_ref):
        d, nm, nv = _adamw_math(w_ref[...], g_ref[...], m_ref[...], v_ref[...])
        d_ref[...] = d
        nm_ref[...] = nm
        nv_ref[...] = nv

    spec = pl.BlockSpec((tr, C), lambda i: (i, 0))
    shp = jax.ShapeDtypeStruct((R, C), F32)
    return pl.pallas_call(
        body, name=name, grid=(R // tr,), in_specs=[spec] * 4, out_specs=[spec] * 3, out_shape=[shp] * 3,
        compiler_params=_params(("parallel",), 7 * _nbytes((tr, C), F32)),
    )(w, g, m, v)


def _adamw_small(ws, gs, ms, vs):
    n = len(ws)

    def body(*refs):
        for k in range(n):
            w_ref, g_ref, m_ref, v_ref = (refs[q * n + k] for q in range(4))
            d, nm, nv = _adamw_math(w_ref[...], g_ref[...], m_ref[...], v_ref[...])
            refs[4 * n + k][...] = d
            refs[5 * n + k][...] = nm
            refs[6 * n + k][...] = nv

    vm = pl.BlockSpec(memory_space=pltpu.VMEM)
    shapes = [jax.ShapeDtypeStruct(w.shape, F32) for w in ws]
    outs = pl.pallas_call(
        body, name="adamw_small", in_specs=[vm] * (4 * n), out_specs=[vm] * (3 * n), out_shape=shapes * 3,
    )(*ws, *gs, *ms, *vs)
    return outs[:n], outs[n:2 * n], outs[2 * n:]


def _pad_rows(a, rows):
    return jnp.pad(a, ((0, rows - a.shape[0]), (0, 0)))


def kernel(x, meta_tokens, ffn1_norm, ffn1_w_gate, ffn1_w_up, ffn1_w_down, mix_norm, w_in, b_in, conv_sc_w, conv_cf_w, conv_cf_b, ln_cf_g, ln_cf_b, w_out, ffn2_norm, ffn2_w_gate, ffn2_w_up, ffn2_w_down, final_norm, loss_target, m_meta_tokens, m_ffn1_norm, m_ffn1_w_gate, m_ffn1_w_up, m_ffn1_w_down, m_mix_norm, m_w_in, m_b_in, m_conv_sc_w, m_conv_cf_w, m_conv_cf_b, m_ln_cf_g, m_ln_cf_b, m_w_out, m_ffn2_norm, m_ffn2_w_gate, m_ffn2_w_up, m_ffn2_w_down, m_final_norm, v_meta_tokens, v_ffn1_norm, v_ffn1_w_gate, v_ffn1_w_up, v_ffn1_w_down, v_mix_norm, v_w_in, v_b_in, v_conv_sc_w, v_conv_cf_w, v_conv_cf_b, v_ln_cf_g, v_ln_cf_b, v_w_out, v_ffn2_norm, v_ffn2_w_gate, v_ffn2_w_up, v_ffn2_w_down, v_final_norm):
    xi, yi, ci = lax.axis_index("x"), lax.axis_index("y"), lax.axis_index("c")
    chip = 2 * xi + yi
    place = jnp.stack([chip, ci]).astype(jnp.int32)

    x2 = x[0]
    tgt = loss_target[0]
    S, D = x2.shape
    C1 = D // 2
    cs = conv_sc_w.shape[2]
    ksc, kcf = conv_sc_w.shape[1], conv_cf_w.shape[1]
    ms = meta_tokens.shape[1]

    rows_small = N_META + 8 + 32
    assert ksc <= 8 and kcf <= 32 and cs <= ms
    pack = jnp.concatenate([
        meta_tokens,
        jnp.pad(conv_sc_w[0], ((0, 8 - ksc), (0, ms - cs))),
        jnp.pad(conv_cf_w[0], ((0, 32 - kcf), (0, ms - cs)))], axis=0)
    everyone = _share_small(pack, False, "share_params")[0::2]
    meta_full = jnp.transpose(everyone[:, :N_META, :], (1, 0, 2)).reshape(N_META, D)
    wsc_full = jnp.transpose(everyone[:, N_META:N_META + ksc, :cs], (1, 0, 2)).reshape(ksc, C1)
    wcf_full = jnp.transpose(everyone[:, N_META + 8:N_META + 8 + kcf, :cs], (1, 0, 2)).reshape(kcf, C1)

    big = {"ffn1_w_gate": ffn1_w_gate, "ffn1_w_up": ffn1_w_up, "ffn1_w_down": ffn1_w_down, "w_in": w_in, "w_out": w_out,
           "ffn2_w_gate": ffn2_w_gate, "ffn2_w_up": ffn2_w_up, "ffn2_w_down": ffn2_w_down}
    big_m = {"ffn1_w_gate": m_ffn1_w_gate, "ffn1_w_up": m_ffn1_w_up, "ffn1_w_down": m_ffn1_w_down, "w_in": m_w_in,
             "w_out": m_w_out, "ffn2_w_gate": m_ffn2_w_gate, "ffn2_w_up": m_ffn2_w_up, "ffn2_w_down": m_ffn2_w_down}
    big_v = {"ffn1_w_gate": v_ffn1_w_gate, "ffn1_w_up": v_ffn1_w_up, "ffn1_w_down": v_ffn1_w_down, "w_in": v_w_in,
             "w_out": v_w_out, "ffn2_w_gate": v_ffn2_w_gate, "ffn2_w_up": v_ffn2_w_up, "ffn2_w_down": v_ffn2_w_down}
    buf = {nm: _cast_own_block(place, w[0], "cast_" + nm) for nm, w in big.items()}
    whole_weight = lambda g: g.reshape(N_CHIPS, 2 * g.shape[2], g.shape[3])
    group_mix, group_ffn2 = ["w_in", "w_out"], ["ffn2_w_gate", "ffn2_w_up", "ffn2_w_down"]

    wg1, wu1, wd1 = (whole_weight(g) for g in _gather_weights([buf[nm] for nm in ["ffn1_w_gate", "ffn1_w_up", "ffn1_w_down"]]))
    corner = lambda a: a.reshape(-1, a.shape[-1])[:8, :128]
    send_mix, recv_mix, thru_mix, token_mix = _start_copies(
        [buf[nm] for nm in group_mix], corner(wd1), 3 * len(group_mix), _gather_copies, "gather_start_mix")
    send_ffn2, recv_ffn2, thru_ffn2, token_ffn2 = _start_copies(
        [buf[nm] for nm in group_ffn2], token_mix, 3 * len(group_ffn2), _gather_copies, "gather_start_ffn2")
    F = N_CHIPS * wd1.shape[1]

    hs0, n1 = _embed_rms(x2, meta_full, ffn1_norm)
    g1, u1, a1 = _ffn_up(n1, wg1, wu1, token_ffn2, "ffn1_up")
    hs1 = _ffn_down(a1, wd1.reshape(F, D), hs0, "ffn1_down")
    arrived = _wait_copies(send_mix, recv_mix, thru_mix, corner(hs1), _gather_copies, "gather_wait_mix")
    win, wout = (whole_weight(g) for g in _forward_halves(arrived, "gather_forward_mix"))
    n2 = _rms(hs1, mix_norm, "rms_mix")
    u = _mix_in(n2, win, b_in)
    y = _mix_conv_fwd(u, wsc_full, wcf_full, conv_cf_b, ln_cf_g, ln_cf_b)
    hs2 = _mix_out(y, wout.reshape(D, D), hs1)
    arrived = _wait_copies(send_ffn2, recv_ffn2, thru_ffn2, corner(hs2), _gather_copies, "gather_wait_ffn2")
    wg2, wu2, wd2 = (whole_weight(g) for g in _forward_halves(arrived, "gather_forward_ffn2"))
    n3 = _rms(hs2, ffn2_norm, "rms_ffn2")
    g2, u2, a2 = _ffn_up(n3, wg2, wu2, token_ffn2, "ffn2_up")
    hs3 = _ffn_down(a2, wd2.reshape(F, D), hs2, "ffn2_down")

    def pair_start(group, tag):
        gs = [g for _, g in group]
        lands = [lax.empty((N_CHIPS,) + g.shape[2:], F32) for g in gs]
        send, recv, thru, token = _start_copies(gs + lands, corner(gs[-1]), N_CHIPS * len(gs), _pair_copies,
                                                "pair_start_" + tag)
        return (group, send, recv, thru, tag), token

    def scatter_start(state, after):
        group, send, recv, thru, tag = state
        thru = _wait_copies(send, recv, thru, corner(after), _pair_copies, "pair_wait_" + tag)
        gs, sib = thru[:len(group)], thru[len(group):]
        sums = [_pair_sum(place, g, rb, "pair_sum_" + nm) for (nm, _), g, rb in zip(group, gs, sib)]
        lands = [lax.empty((3,) + q.shape[1:], BF16) for q in sums]
        send, recv, thru, token = _start_copies(sums + lands, corner(sums[-1]), 3 * len(gs), _scatter_copies,
                                                "scatter_start_" + tag)
        return ([(nm, g) for (nm, _), g in zip(group, gs)], sib, send, recv, thru, tag), token

    def reduce_finish(state, after):
        group, sib, send, recv, thru, tag = state
        lands = _wait_copies(send, recv, thru, corner(after), _scatter_copies, "scatter_wait_" + tag)[len(group):]
        mine = [_chip_sum(place, g, rb, rc, "chip_sum_" + nm) for (nm, g), rb, rc in zip(group, sib, lands)]
        whole = _half_exchange(mine, "half_exchange_" + tag)
        out = {}
        for (nm, _), g in zip(group, whole):
            w = big[nm]
            g3d = g.reshape(w.shape)
            d, new_m, new_v = _adamw(w[0], g3d[0], big_m[nm][0], big_v[nm][0], "adamw_" + nm)
            out[nm] = (g3d, d[None], new_m[None], new_v[None])
        return out

    dhs3, df2, loss_row, d_final = _final_loss(hs3, final_norm.reshape(1, D), tgt)

    dg2, du2 = _ffn_bwd_act(df2, wd2, g2, u2, token_ffn2, "ffn2_bwd_act")
    gw_d2 = _wgrad_down(a2, df2, "wgrad_ffn2_down")
    gw_g2 = _wgrad_cols(n3, [dg2], "wgrad_ffn2_gate")[0]
    gw_u2 = _wgrad_cols(n3, [du2], "wgrad_ffn2_up")[0]
    pair_ffn2, token = pair_start([("ffn2_w_gate", gw_g2), ("ffn2_w_up", gw_u2), ("ffn2_w_down", gw_d2)], "ffn2")
    dn3 = _nt_panel([dg2, du2], [wg2, wu2], token, "ffn2_bwd_in")
    red_ffn2, token = scatter_start(pair_ffn2, dn3)
    dhs2, dm, d_ffn2 = _rms_bwd(dn3, hs2, ffn2_norm, dhs3, 1.0, "rms_bwd_ffn2")

    dy = _nt_panel([dm], [wout.reshape(1, D, D)], token, "mix_bwd_out")
    gw_out = _wgrad_out(y, dm)
    dz1, dcs, db, d_lg, d_lb, d_bcf = _mix_conv_bwd1(u, dy, wsc_full, wcf_full, conv_cf_b, ln_cf_g, ln_cf_b)
    du, d_bin, d_wsc, d_wcf = _mix_conv_bwd2(u, dz1, dcs, db, wsc_full, wcf_full)
    gw_in = _wgrad_cols(n2, [du], "wgrad_w_in")[0]
    pair_mix, token = pair_start([("w_in", gw_in), ("w_out", gw_out)], "mix")
    dn2 = _nt_panel([du], [win], token, "mix_bwd_in")
    red_mix, token = scatter_start(pair_mix, dn2)
    dhs1, df1, d_mix = _rms_bwd(dn2, hs1, mix_norm, dhs2, FFN_RES_SCALE, "rms_bwd_mix")

    dg1, du1 = _ffn_bwd_act(df1, wd1, g1, u1, token, "ffn1_bwd_act")
    gw_d1 = _wgrad_down(a1, df1, "wgrad_ffn1_down")
    gw_g1 = _wgrad_cols(n1, [dg1], "wgrad_ffn1_gate")[0]
    gw_u1 = _wgrad_cols(n1, [du1], "wgrad_ffn1_up")[0]
    pair_ffn1, token = pair_start([("ffn1_w_gate", gw_g1), ("ffn1_w_up", gw_u1), ("ffn1_w_down", gw_d1)], "ffn1")
    dn1 = _nt_panel([dg1, du1], [wg1, wu1], token, "ffn1_bwd_in")
    red_ffn1, token = scatter_start(pair_ffn1, dn1)
    grad_x, d_meta, d_ffn1 = _rms_bwd_first(dn1, hs0, ffn1_norm, dhs1, token)

    big_out = reduce_finish(red_ffn2, grad_x)
    big_out.update(reduce_finish(red_mix, big_out["ffn2_w_down"][1]))
    big_out.update(reduce_finish(red_ffn1, big_out["w_out"][1]))

    W = C1
    rows = lambda a: a.reshape(-1, W)
    parts = [rows(d_ffn1), rows(d_mix), rows(d_ffn2), rows(d_final), rows(d_bin), d_bcf, d_lg, d_lb,
             d_wsc, d_wcf, rows(d_meta), jnp.broadcast_to(loss_row[:, :1], (1, W))]
    sizes = [p.shape[0] for p in parts]
    total_rows = sum(sizes)
    packed = _pad_rows(jnp.concatenate(parts, axis=0), -(-total_rows // 8) * 8)
    summed = _share_small(packed, True, "sum_small")
    offs = [0]
    for n in sizes:
        offs.append(offs[-1] + n)
    piece = lambda k: summed[offs[k]:offs[k + 1]]
    loss = piece(11)[0, 0]
    g_ffn1, g_mix, g_ffn2 = (piece(k).reshape(1, D) for k in range(3))
    g_final = piece(3).reshape(1, D)
    g_bin = piece(4).reshape(1, -1)
    g_bcf, g_lg, g_lb = piece(5), piece(6), piece(7)
    g_wsc = lax.dynamic_slice_in_dim(piece(8), chip * cs, cs, axis=1)
    g_wcf = lax.dynamic_slice_in_dim(piece(9), chip * cs, cs, axis=1)
    g_meta = lax.dynamic_slice_in_dim(piece(10).reshape(N_META, D), chip * ms, ms, axis=1)

    small_names = ["meta_tokens", "ffn1_norm", "mix_norm", "b_in", "conv_sc_w", "conv_cf_w", "conv_cf_b", "ln_cf_g",
                   "ln_cf_b", "ffn2_norm", "final_norm"]
    small_w = [meta_tokens, ffn1_norm, mix_norm, b_in, conv_sc_w[0], conv_cf_w[0], conv_cf_b, ln_cf_g, ln_cf_b,
               ffn2_norm, final_norm.reshape(1, D)]
    small_g = [g_meta, g_ffn1, g_mix, g_bin, g_wsc, g_wcf, g_bcf, g_lg, g_lb, g_ffn2, g_final]
    small_m = [m_meta_tokens, m_ffn1_norm, m_mix_norm, m_b_in, m_conv_sc_w[0], m_conv_cf_w[0], m_conv_cf_b, m_ln_cf_g,
               m_ln_cf_b, m_ffn2_norm, m_final_norm.reshape(1, D)]
    small_v = [v_meta_tokens, v_ffn1_norm, v_mix_norm, v_b_in, v_conv_sc_w[0], v_conv_cf_w[0], v_conv_cf_b, v_ln_cf_g,
               v_ln_cf_b, v_ffn2_norm, v_final_norm.reshape(1, D)]
    s_d, s_m, s_v = _adamw_small(small_w, small_g, small_m, small_v)
    shapes = {"conv_sc_w": conv_sc_w.shape, "conv_cf_w": conv_cf_w.shape, "final_norm": final_norm.shape}
    small_out = {}
    for nm, g, d, m, v in zip(small_names, small_g, s_d, s_m, s_v):
        shp = shapes.get(nm, g.shape)
        small_out[nm] = tuple(t.reshape(shp) for t in (g, d, m, v))

    order = ["meta_tokens", "ffn1_norm", "ffn1_w_gate", "ffn1_w_up", "ffn1_w_down", "mix_norm", "w_in", "b_in",
             "conv_sc_w", "conv_cf_w", "conv_cf_b", "ln_cf_g", "ln_cf_b", "w_out", "ffn2_norm", "ffn2_w_gate",
             "ffn2_w_up", "ffn2_w_down", "final_norm"]
    res = {**big_out, **small_out}
    outs = [loss, grad_x[None]]
    for q in range(4):
        outs.extend(res[nm][q] for nm in order)
    return tuple(outs)
```

```python
import functools

import jax
import jax.numpy as jnp
from jax import lax
from jax.experimental import pallas as pl
from jax.experimental.pallas import tpu as pltpu

F32 = jnp.float32
BF16 = jnp.bfloat16
MESH = pl.DeviceIdType.MESH

N_META = 16
TT = 128
PAD = TT - N_META
HALO = 32
EPS = 1e-6
FFN_RES_SCALE = 0.5
N_CHIPS = 4
N_DEV = 8

ADAM_LR = 0.001
ADAM_B1 = 0.9
ADAM_B2 = 0.999
ADAM_EPS = 1e-08
ADAM_WD = 0.01
ADAM_STEP = 10

V7X_VMEM_BYTES = 64 * 2 ** 20
NT_DIMS = (((1,), (1,)), ((), ()))
TN_DIMS = (((0,), (0,)), ((), ()))


def _params(semantics, block_bytes):
    limit = min(2 * block_bytes + 16 * 2 ** 20, V7X_VMEM_BYTES - 6 * 2 ** 20)
    return pltpu.CompilerParams(dimension_semantics=semantics, vmem_limit_bytes=int(limit))


def _nbytes(shape, dtype):
    n = 1
    for d in shape:
        if d is not None:
            n *= d
    return n * jnp.dtype(dtype).itemsize


def _row_tile(rows, target, mult=8):
    best = None
    for t in range(mult, min(rows, target) + 1, mult):
        if rows % t == 0:
            best = t
    assert best is not None, (rows, target, mult)
    return best


def _sigmoid(v):
    return jax.nn.sigmoid(v)


def _dsilu(v, s):
    return s * (1.0 + v * (1.0 - s))


def _embed_rms(x2, meta, gain):
    S, D = x2.shape
    T = S + TT

    def body(x_ref, meta_ref, g_ref, hs_ref, n_ref):
        i = pl.program_id(0)

        @pl.when(i == 0)
        def _():
            hs_ref[...] = jnp.zeros_like(hs_ref)
            hs_ref[PAD:, :] = meta_ref[...]

        @pl.when(i > 0)
        def _():
            hs_ref[...] = x_ref[...]

        h = hs_ref[...]
        r = lax.rsqrt(jnp.mean(h * h, axis=-1, keepdims=True) + EPS)
        n_ref[...] = ((h * r) * g_ref[...]).astype(BF16)

    blk = _nbytes((TT, D), F32) * 2 + _nbytes((TT, D), BF16)
    return pl.pallas_call(
        body, name="embed_rms", grid=(T // TT,),
        in_specs=[pl.BlockSpec((TT, D), lambda i: (jnp.maximum(i - 1, 0), 0)),
                  pl.BlockSpec((N_META, D), lambda i: (0, 0)),
                  pl.BlockSpec((1, D), lambda i: (0, 0))],
        out_specs=[pl.BlockSpec((TT, D), lambda i: (i, 0)), pl.BlockSpec((TT, D), lambda i: (i, 0))],
        out_shape=[jax.ShapeDtypeStruct((T, D), F32), jax.ShapeDtypeStruct((T, D), BF16)],
        compiler_params=_params(("parallel",), blk),
    )(x2, meta, gain)


def _rms(hs, gain, name):
    T, D = hs.shape
    te = _row_tile(T, 384)

    def body(h_ref, g_ref, n_ref):
        h = h_ref[...]
        r = lax.rsqrt(jnp.mean(h * h, axis=-1, keepdims=True) + EPS)
        n_ref[...] = ((h * r) * g_ref[...]).astype(BF16)

    blk = _nbytes((te, D), F32) + _nbytes((te, D), BF16)
    return pl.pallas_call(
        body, name=name, grid=(T // te,),
        in_specs=[pl.BlockSpec((te, D), lambda i: (i, 0)), pl.BlockSpec((1, D), lambda i: (0, 0))],
        out_specs=pl.BlockSpec((te, D), lambda i: (i, 0)),
        out_shape=jax.ShapeDtypeStruct((T, D), BF16),
        compiler_params=_params(("parallel",), blk),
    )(hs, gain)


def _rms_bwd_math(dn, h, g):
    r = lax.rsqrt(jnp.mean(h * h, axis=-1, keepdims=True) + EPS)
    xh = h * r
    dgain = jnp.sum(dn * xh, axis=0, keepdims=True)
    dxh = dn * g
    dh = r * (dxh - xh * jnp.mean(dxh * xh, axis=-1, keepdims=True))
    return dh, dgain


def _rms_bwd(dn, hs, gain, dres, scale, name):
    T, D = hs.shape
    te = _row_tile(T, 384)

    def body(dn_ref, h_ref, g_ref, dres_ref, dhs_ref, dhb_ref, dg_ref):
        dh, dgain = _rms_bwd_math(dn_ref[...], h_ref[...], g_ref[...])
        d = dres_ref[...] + dh
        dhs_ref[...] = d
        dhb_ref[...] = (scale * d).astype(BF16)

        @pl.when(pl.program_id(0) == 0)
        def _():
            dg_ref[...] = jnp.zeros_like(dg_ref)

        dg_ref[...] += dgain

    blk = _nbytes((te, D), F32) * 4 + _nbytes((te, D), BF16)
    row = lambda i: (i, 0)
    return pl.pallas_call(
        body, name=name, grid=(T // te,),
        in_specs=[pl.BlockSpec((te, D), row), pl.BlockSpec((te, D), row), pl.BlockSpec((1, D), lambda i: (0, 0)),
                  pl.BlockSpec((te, D), row)],
        out_specs=[pl.BlockSpec((te, D), row), pl.BlockSpec((te, D), row), pl.BlockSpec((1, D), lambda i: (0, 0))],
        out_shape=[jax.ShapeDtypeStruct((T, D), F32), jax.ShapeDtypeStruct((T, D), BF16),
                   jax.ShapeDtypeStruct((1, D), F32)],
        compiler_params=_params(("arbitrary",), blk),
    )(dn, hs, gain, dres)


def _rms_bwd_first(dn, hs, gain, dres, after):
    T, D = hs.shape
    S = T - TT

    def body(dn_ref, h_ref, g_ref, dres_ref, after_ref, gx_ref, gm_ref, dg_ref):
        i = pl.program_id(0)
        dh, dgain = _rms_bwd_math(dn_ref[...], h_ref[...], g_ref[...])
        d = dres_ref[...] + dh

        @pl.when(i == 0)
        def _():
            dg_ref[...] = jnp.zeros_like(dg_ref)
            gm_ref[...] = d[PAD:, :]

        @pl.when(i > 0)
        def _():
            gx_ref[...] = d

        dg_ref[...] += dgain

    blk = _nbytes((TT, D), F32) * 4
    row = lambda i: (i, 0)
    return pl.pallas_call(
        body, name="rms_bwd_ffn1", grid=(T // TT,),
        in_specs=[pl.BlockSpec((TT, D), row), pl.BlockSpec((TT, D), row), pl.BlockSpec((1, D), lambda i: (0, 0)),
                  pl.BlockSpec((TT, D), row), TOKEN],
        out_specs=[pl.BlockSpec((TT, D), lambda i: (jnp.maximum(i - 1, 0), 0)),
                   pl.BlockSpec((N_META, D), lambda i: (0, 0)), pl.BlockSpec((1, D), lambda i: (0, 0))],
        out_shape=[jax.ShapeDtypeStruct((S, D), F32), jax.ShapeDtypeStruct((N_META, D), F32),
                   jax.ShapeDtypeStruct((1, D), F32)],
        compiler_params=_params(("arbitrary",), blk),
    )(dn, hs, gain, dres, after)


def _final_loss(hs, gain, tgt):
    T, D = hs.shape

    def body(h_ref, g_ref, t_ref, dhs_ref, dhb_ref, loss_ref, dg_ref):
        i = pl.program_id(0)
        h = h_ref[...]
        g = g_ref[...]
        r = lax.rsqrt(jnp.mean(h * h, axis=-1, keepdims=True) + EPS)
        xh = h * r
        e = jnp.where(i > 0, xh * g - t_ref[...], 0.0)
        tile_loss = jnp.sum(jnp.sum(e * e, axis=1, keepdims=True), axis=0, keepdims=True) * (0.5 / D)
        dout = e * (1.0 / D)
        dgain = jnp.sum(dout * xh, axis=0, keepdims=True)
        dxh = dout * g
        d = r * (dxh - xh * jnp.mean(dxh * xh, axis=-1, keepdims=True))
        dhs_ref[...] = d
        dhb_ref[...] = (FFN_RES_SCALE * d).astype(BF16)

        @pl.when(i == 0)
        def _():
            loss_ref[...] = jnp.zeros_like(loss_ref)
            dg_ref[...] = jnp.zeros_like(dg_ref)

        loss_ref[...] += jnp.broadcast_to(tile_loss, loss_ref.shape)
        dg_ref[...] += dgain

    blk = _nbytes((TT, D), F32) * 3 + _nbytes((TT, D), BF16)
    row = lambda i: (i, 0)
    return pl.pallas_call(
        body, name="final_loss", grid=(T // TT,),
        in_specs=[pl.BlockSpec((TT, D), row), pl.BlockSpec((1, D), lambda i: (0, 0)),
                  pl.BlockSpec((TT, D), lambda i: (jnp.maximum(i - 1, 0), 0))],
        out_specs=[pl.BlockSpec((TT, D), row), pl.BlockSpec((TT, D), row),
                   pl.BlockSpec((1, 128), lambda i: (0, 0)), pl.BlockSpec((1, D), lambda i: (0, 0))],
        out_shape=[jax.ShapeDtypeStruct((T, D), F32), jax.ShapeDtypeStruct((T, D), BF16),
                   jax.ShapeDtypeStruct((1, 128), F32), jax.ShapeDtypeStruct((1, D), F32)],
        compiler_params=_params(("arbitrary",), blk),
    )(hs, gain, tgt)


def _tm(T):
    return _row_tile(T, 384, 128)


TOKEN = pl.BlockSpec((8, 128), lambda *_: (0, 0))


def _ffn_up(n, wg, wu, after, name):
    T, D = n.shape
    Fs = wg.shape[2]
    tm = _tm(T)

    def body(n_ref, wg_ref, wu_ref, after_ref, g_ref, u_ref, a_ref):
        nn = n_ref[...]
        g = jnp.dot(nn, wg_ref[...], preferred_element_type=F32)
        u = jnp.dot(nn, wu_ref[...], preferred_element_type=F32)
        g_ref[...] = g.astype(BF16)
        u_ref[...] = u.astype(BF16)
        a_ref[...] = (jax.nn.silu(g) * u).astype(BF16)

    blk = _nbytes((tm, D), BF16) + 2 * _nbytes((D, Fs), BF16) + 3 * _nbytes((tm, Fs), BF16) + 2 * _nbytes((tm, Fs), F32)
    out = pl.BlockSpec((tm, Fs), lambda j, i: (i, j))
    shp = jax.ShapeDtypeStruct((T, N_CHIPS * Fs), BF16)
    return pl.pallas_call(
        body, name=name, grid=(N_CHIPS, T // tm),
        in_specs=[pl.BlockSpec((tm, D), lambda j, i: (i, 0)),
                  pl.BlockSpec((None, D, Fs), lambda j, i: (j, 0, 0)),
                  pl.BlockSpec((None, D, Fs), lambda j, i: (j, 0, 0)), TOKEN],
        out_specs=[out, out, out], out_shape=[shp, shp, shp],
        compiler_params=_params(("parallel", "parallel"), blk),
    )(n, wg, wu, after)


def _ffn_down(a, wd, hs, name):
    T, F = a.shape
    D = wd.shape[1]
    tm = _tm(T)
    tn = D // 2

    def body(a_ref, w_ref, h_ref, o_ref):
        o_ref[...] = h_ref[...] + FFN_RES_SCALE * jnp.dot(a_ref[...], w_ref[...], preferred_element_type=F32)

    blk = _nbytes((tm, F), BF16) + _nbytes((F, tn), BF16) + 3 * _nbytes((tm, tn), F32)
    return pl.pallas_call(
        body, name=name, grid=(D // tn, T // tm),
        in_specs=[pl.BlockSpec((tm, F), lambda n, i: (i, 0)), pl.BlockSpec((F, tn), lambda n, i: (0, n)),
                  pl.BlockSpec((tm, tn), lambda n, i: (i, n))],
        out_specs=pl.BlockSpec((tm, tn), lambda n, i: (i, n)),
        out_shape=jax.ShapeDtypeStruct((T, D), F32),
        compiler_params=_params(("parallel", "parallel"), blk),
    )(a, wd, hs)


def _mix_in(n, w, b):
    T, D = n.shape
    Ns = w.shape[2]
    tm = _tm(T)

    def body(n_ref, w_ref, b_ref, u_ref):
        u_ref[...] = jnp.dot(n_ref[...], w_ref[...], preferred_element_type=F32) + b_ref[...]

    blk = _nbytes((tm, D), BF16) + _nbytes((D, Ns), BF16) + 2 * _nbytes((tm, Ns), F32)
    return pl.pallas_call(
        body, name="mix_in", grid=(N_CHIPS, T // tm),
        in_specs=[pl.BlockSpec((tm, D), lambda j, i: (i, 0)), pl.BlockSpec((None, D, Ns), lambda j, i: (j, 0, 0)),
                  pl.BlockSpec((1, Ns), lambda j, i: (0, j))],
        out_specs=pl.BlockSpec((tm, Ns), lambda j, i: (i, j)),
        out_shape=jax.ShapeDtypeStruct((T, N_CHIPS * Ns), F32),
        compiler_params=_params(("parallel", "parallel"), blk),
    )(n, w, b)


def _mix_out(y, w, hs):
    T, D = y.shape
    tm = _tm(T)

    def body(y_ref, w_ref, h_ref, o_ref):
        o_ref[...] = h_ref[...] + jnp.dot(y_ref[...], w_ref[...], preferred_element_type=F32)

    blk = _nbytes((tm, D), BF16) + _nbytes((D, D), BF16) + 3 * _nbytes((tm, D), F32)
    return pl.pallas_call(
        body, name="mix_out", grid=(T // tm,),
        in_specs=[pl.BlockSpec((tm, D), lambda i: (i, 0)), pl.BlockSpec((D, D), lambda i: (0, 0)),
                  pl.BlockSpec((tm, D), lambda i: (i, 0))],
        out_specs=pl.BlockSpec((tm, D), lambda i: (i, 0)),
        out_shape=jax.ShapeDtypeStruct((T, D), F32),
        compiler_params=_params(("parallel",), blk),
    )(y, w, hs)


def _ffn_bwd_act(dfb, wd, g, u, after, name):
    T, D = dfb.shape
    Fs = wd.shape[1]
    tm = _tm(T)

    def body(d_ref, w_ref, g_ref, u_ref, after_ref, dg_ref, du_ref):
        da = lax.dot_general(d_ref[...], w_ref[...], NT_DIMS, preferred_element_type=F32)
        gv = g_ref[...].astype(F32)
        uv = u_ref[...].astype(F32)
        s = _sigmoid(gv)
        du_ref[...] = (da * (gv * s)).astype(BF16)
        dg_ref[...] = (da * uv * _dsilu(gv, s)).astype(BF16)

    blk = _nbytes((tm, D), BF16) + _nbytes((Fs, D), BF16) + 4 * _nbytes((tm, Fs), BF16) + 3 * _nbytes((tm, Fs), F32)
    io = pl.BlockSpec((tm, Fs), lambda j, i: (i, j))
    shp = jax.ShapeDtypeStruct((T, N_CHIPS * Fs), BF16)
    return pl.pallas_call(
        body, name=name, grid=(N_CHIPS, T // tm),
        in_specs=[pl.BlockSpec((tm, D), lambda j, i: (i, 0)), pl.BlockSpec((None, Fs, D), lambda j, i: (j, 0, 0)), io, io,
                  TOKEN],
        out_specs=[io, io], out_shape=[shp, shp],
        compiler_params=_params(("parallel", "parallel"), blk),
    )(dfb, wd, g, u, after)


def _nt_panel(lhs_list, w_list, after, name):
    T = lhs_list[0].shape[0]
    nsh, Dout, Ks = w_list[0].shape
    npair = len(lhs_list)
    tm = _tm(T)
    tn = Dout // 4 if npair * nsh * Ks > 4096 else Dout // 2

    def body(*refs):
        l_refs, w_refs, o_ref = refs[:npair], refs[npair:2 * npair], refs[2 * npair + 1]
        acc = None
        for p in range(npair):
            for j in range(nsh):
                part = lax.dot_general(l_refs[p][:, j * Ks:(j + 1) * Ks], w_refs[p][j], NT_DIMS,
                                       preferred_element_type=F32)
                acc = part if acc is None else acc + part
        o_ref[...] = acc

    blk = npair * (_nbytes((tm, nsh * Ks), BF16) + _nbytes((nsh, tn, Ks), BF16)) + 3 * _nbytes((tm, tn), F32)
    return pl.pallas_call(
        body, name=name, grid=(Dout // tn, T // tm),
        in_specs=[pl.BlockSpec((tm, nsh * Ks), lambda n, i: (i, 0))] * npair
                 + [pl.BlockSpec((nsh, tn, Ks), lambda n, i: (0, n, 0))] * npair + [TOKEN],
        out_specs=pl.BlockSpec((tm, tn), lambda n, i: (i, n)),
        out_shape=jax.ShapeDtypeStruct((T, Dout), F32),
        compiler_params=_params(("parallel", "parallel"), blk),
    )(*lhs_list, *w_list, after)


def _tn_call(name, grid, lhs, lhs_spec, rhs_list, rhs_specs, out_shapes, out_specs, blk, after=None):
    nr = len(rhs_list)
    extra = [] if after is None else [after]

    def body(*refs):
        l_ref, r_refs, o_refs = refs[0], refs[1:1 + nr], refs[len(refs) - nr:]
        k = pl.program_id(len(grid) - 1)
        lv = l_ref[...]
        for q in range(nr):
            part = lax.dot_general(lv, r_refs[q][...], TN_DIMS, preferred_element_type=F32)
            part = part.reshape(o_refs[q].shape)

            @pl.when(k == 0)
            def _(o=o_refs[q], part=part):
                o[...] = part

            @pl.when(k > 0)
            def _(o=o_refs[q], part=part):
                o[...] += part

    return pl.pallas_call(
        body, name=name, grid=grid, in_specs=[lhs_spec] + rhs_specs + [TOKEN] * len(extra), out_specs=out_specs,
        out_shape=out_shapes, compiler_params=_params(("parallel",) * (len(grid) - 1) + ("arbitrary",), blk),
    )(lhs, *rhs_list, *extra)


def _tk(T):
    return _row_tile(T, 1408, 128)


def _wgrad_cols(n, rhs_list, name, after=None):
    T, D = n.shape
    Ns = rhs_list[0].shape[1] // N_CHIPS
    tk = _tk(T)
    nr = len(rhs_list)
    blk = _nbytes((tk, D // 2), BF16) + nr * (_nbytes((tk, Ns), BF16) + 2 * _nbytes((D // 2, Ns), F32))
    return _tn_call(
        name, (N_CHIPS, 2, T // tk), n, pl.BlockSpec((tk, D // 2), lambda j, m, k: (k, m)),
        rhs_list, [pl.BlockSpec((tk, Ns), lambda j, m, k: (k, j))] * nr,
        [jax.ShapeDtypeStruct((N_CHIPS, 2, D // 2, Ns), F32)] * nr,
        [pl.BlockSpec((None, None, D // 2, Ns), lambda j, m, k: (j, m, 0, 0))] * nr, blk, after)


def _wgrad_down(a, dfb, name):
    T, F = a.shape
    D = dfb.shape[1]
    Fs = F // N_CHIPS
    tk = _tk(T)
    tn = D // 2
    blk = _nbytes((tk, Fs), BF16) + _nbytes((tk, tn), BF16) + 2 * _nbytes((Fs, tn), F32)
    return _tn_call(
        name, (N_CHIPS, D // tn, T // tk), a, pl.BlockSpec((tk, Fs), lambda j, n, k: (k, j)),
        [dfb], [pl.BlockSpec((tk, tn), lambda j, n, k: (k, n))],
        [jax.ShapeDtypeStruct((N_CHIPS, 2, Fs // 2, D), F32)],
        [pl.BlockSpec((None, 2, Fs // 2, tn), lambda j, n, k: (j, 0, 0, n))], blk)[0]


def _wgrad_out(y, dmb):
    T, D = y.shape
    tk = _tk(T)
    tn = D // 2
    rows = D // (2 * N_CHIPS)
    blk = _nbytes((tk, D // 2), BF16) + _nbytes((tk, tn), BF16) + 2 * _nbytes((D // 2, tn), F32)
    return _tn_call(
        "wgrad_w_out", (2, D // tn, T // tk), y, pl.BlockSpec((tk, D // 2), lambda m, n, k: (k, m)),
        [dmb], [pl.BlockSpec((tk, tn), lambda m, n, k: (k, n))],
        [jax.ShapeDtypeStruct((N_CHIPS, 2, rows, D), F32)],
        [pl.BlockSpec((2, 2, rows, tn), lambda m, n, k: (m, 0, 0, n))], blk)[0]


def _row_masks(i, last):
    rows = i * TT + lax.broadcasted_iota(jnp.int32, (TT, 1), 0)
    prows = i * TT - HALO + lax.broadcasted_iota(jnp.int32, (HALO, 1), 0)
    return rows >= PAD, (prows >= PAD) & (i > 0), i < last


def _conv_inputs(u, up, mask_c, mask_p, zbuf, pbuf, C1):
    b, c, v, a, g = (u[:, k * C1:(k + 1) * C1] for k in range(5))
    cp, vp, ap, gp = (up[:, k * C1:(k + 1) * C1] for k in range(1, 5))
    sg = _sigmoid(g)
    pbuf[0:HALO, :] = jnp.where(mask_p, cp * vp, 0.0)
    pbuf[HALO:, :] = jnp.where(mask_c, c * v, 0.0)
    zbuf[0:HALO, :] = jnp.where(mask_p, ap * _sigmoid(gp), 0.0)
    zbuf[HALO:, :] = jnp.where(mask_c, a * sg, 0.0)
    return b, c, v, a, sg


def _causal_conv(w_ref, buf):
    K = w_ref.shape[0]
    acc = None
    for k in range(K):
        lo = HALO - (K - 1) + k
        term = w_ref[k:k + 1, :] * buf[lo:lo + TT, :]
        acc = term if acc is None else acc + term
    return acc


def _anticausal_conv(w_ref, buf):
    K = w_ref.shape[0]
    acc = None
    for k in range(K):
        lo = K - 1 - k
        term = w_ref[k:k + 1, :] * buf[lo:lo + TT, :]
        acc = term if acc is None else acc + term
    return acc


def _conv_weight_sums(dw_ref, dy, buf):
    K = dw_ref.shape[0]
    for k in range(K):
        lo = HALO - (K - 1) + k
        dw_ref[k:k + 1, :] += jnp.sum(dy * buf[lo:lo + TT, :], axis=0, keepdims=True)


def _layernorm_stats(z1):
    mu = jnp.mean(z1, axis=-1, keepdims=True)
    zc = z1 - mu
    rs = lax.rsqrt(jnp.mean(zc * zc, axis=-1, keepdims=True) + EPS)
    return zc * rs, rs


def _mixer_specs(T, DIN, C1, ksc, kcf):
    cur = pl.BlockSpec((TT, DIN), lambda i: (i, 0))
    prev = pl.BlockSpec((HALO, DIN), lambda i: (jnp.maximum(i * (TT // HALO) - 1, 0), 0))
    full = lambda r: pl.BlockSpec((r, C1), lambda i: (0, 0))
    return cur, prev, [full(ksc), full(kcf), full(1), full(1), full(1)]


def _mix_conv_fwd(u, wsc, wcf, bcf, lg, lb):
    T, DIN = u.shape
    C1 = DIN // 5
    last = T // TT - 1

    def body(u_ref, up_ref, wsc_ref, wcf_ref, bcf_ref, lg_ref, lb_ref, y_ref, zbuf, pbuf):
        i = pl.program_id(0)
        mask_c, mask_p, _ = _row_masks(i, last)
        b, _, _, _, _ = _conv_inputs(u_ref[...], up_ref[...], mask_c, mask_p, zbuf, pbuf, C1)
        cs = _causal_conv(wsc_ref, pbuf)
        z1 = _causal_conv(wcf_ref, zbuf) + bcf_ref[...]
        zh, _ = _layernorm_stats(z1)
        ln = zh * lg_ref[...] + lb_ref[...]
        y_ref[:, 0:C1] = jnp.where(mask_c, b * cs, 0.0).astype(BF16)
        y_ref[:, C1:] = jnp.where(mask_c, jax.nn.silu(ln), 0.0).astype(BF16)

    cur, prev, small = _mixer_specs(T, DIN, C1, wsc.shape[0], wcf.shape[0])
    blk = _nbytes((TT + HALO, DIN), F32) + _nbytes((TT, 2 * C1), BF16) + 12 * _nbytes((TT + HALO, C1), F32)
    return pl.pallas_call(
        body, name="mix_conv_fwd", grid=(T // TT,),
        in_specs=[cur, prev] + small,
        out_specs=pl.BlockSpec((TT, 2 * C1), lambda i: (i, 0)),
        out_shape=jax.ShapeDtypeStruct((T, 2 * C1), BF16),
        scratch_shapes=[pltpu.VMEM((TT + HALO, C1), F32), pltpu.VMEM((TT + HALO, C1), F32)],
        compiler_params=_params(("arbitrary",), blk),
    )(u, u, wsc, wcf, bcf, lg, lb)


def _mix_conv_bwd1(u, dy, wsc, wcf, bcf, lg, lb):
    T, DIN = u.shape
    C1 = DIN // 5
    last = T // TT - 1

    def body(u_ref, up_ref, dy_ref, wsc_ref, wcf_ref, bcf_ref, lg_ref, lb_ref,
             dz1_ref, dcs_ref, db_ref, dlg_ref, dlb_ref, dbcf_ref, zbuf, pbuf):
        i = pl.program_id(0)
        mask_c, mask_p, _ = _row_masks(i, last)
        b, _, _, _, _ = _conv_inputs(u_ref[...], up_ref[...], mask_c, mask_p, zbuf, pbuf, C1)
        cs = _causal_conv(wsc_ref, pbuf)
        z1 = _causal_conv(wcf_ref, zbuf) + bcf_ref[...]
        zh, rs = _layernorm_stats(z1)
        ln = zh * lg_ref[...] + lb_ref[...]
        dy = dy_ref[...]
        dysc = jnp.where(mask_c, dy[:, 0:C1], 0.0)
        dycf = jnp.where(mask_c, dy[:, C1:], 0.0)
        db_ref[...] = (dysc * cs).astype(BF16)
        dcs_ref[...] = dysc * b
        dl = dycf * _dsilu(ln, _sigmoid(ln))
        dzh = dl * lg_ref[...]
        dz1 = rs * (dzh - jnp.mean(dzh, axis=-1, keepdims=True) - zh * jnp.mean(dzh * zh, axis=-1, keepdims=True))
        dz1_ref[...] = dz1

        @pl.when(i == 0)
        def _():
            dlg_ref[...] = jnp.zeros_like(dlg_ref)
            dlb_ref[...] = jnp.zeros_like(dlb_ref)
            dbcf_ref[...] = jnp.zeros_like(dbcf_ref)

        dlg_ref[...] += jnp.sum(dl * zh, axis=0, keepdims=True)
        dlb_ref[...] += jnp.sum(dl, axis=0, keepdims=True)
        dbcf_ref[...] += jnp.sum(dz1, axis=0, keepdims=True)

    cur, prev, small = _mixer_specs(T, DIN, C1, wsc.shape[0], wcf.shape[0])
    tile = lambda: pl.BlockSpec((TT, C1), lambda i: (i, 0))
    vec = lambda: pl.BlockSpec((1, C1), lambda i: (0, 0))
    blk = _nbytes((TT + HALO, DIN), F32) + 4 * _nbytes((TT, C1), F32) + 16 * _nbytes((TT + HALO, C1), F32)
    return pl.pallas_call(
        body, name="mix_conv_bwd1", grid=(T // TT,),
        in_specs=[cur, prev, pl.BlockSpec((TT, 2 * C1), lambda i: (i, 0))] + small,
        out_specs=[tile(), tile(), tile(), vec(), vec(), vec()],
        out_shape=[jax.ShapeDtypeStruct((T, C1), F32), jax.ShapeDtypeStruct((T, C1), F32),
                   jax.ShapeDtypeStruct((T, C1), BF16)] + [jax.ShapeDtypeStruct((1, C1), F32)] * 3,
        scratch_shapes=[pltpu.VMEM((TT + HALO, C1), F32), pltpu.VMEM((TT + HALO, C1), F32)],
        compiler_params=_params(("arbitrary",), blk),
    )(u, u, dy, wsc, wcf, bcf, lg, lb)


def _mix_conv_bwd2(u, dz1, dcs, db, wsc, wcf):
    T, DIN = u.shape
    C1 = DIN // 5
    last = T // TT - 1
    ksc, kcf = wsc.shape[0], wcf.shape[0]

    def body(u_ref, up_ref, dz_ref, dzn_ref, dc_ref, dcn_ref, db_ref, wsc_ref, wcf_ref,
             du_ref, dbin_ref, dwsc_ref, dwcf_ref, zbuf, pbuf, dzbuf, dcbuf):
        i = pl.program_id(0)
        mask_c, mask_p, has_next = _row_masks(i, last)
        _, c, v, a, sg = _conv_inputs(u_ref[...], up_ref[...], mask_c, mask_p, zbuf, pbuf, C1)
        dz1 = dz_ref[...]
        dcs = dc_ref[...]
        dzbuf[0:TT, :] = dz1
        dzbuf[TT:, :] = jnp.where(has_next, dzn_ref[...], 0.0)
        dcbuf[0:TT, :] = dcs
        dcbuf[TT:, :] = jnp.where(has_next, dcn_ref[...], 0.0)

        @pl.when(i == 0)
        def _():
            dbin_ref[...] = jnp.zeros_like(dbin_ref)
            dwsc_ref[...] = jnp.zeros_like(dwsc_ref)
            dwcf_ref[...] = jnp.zeros_like(dwcf_ref)

        _conv_weight_sums(dwcf_ref, dz1, zbuf)
        _conv_weight_sums(dwsc_ref, dcs, pbuf)
        dz0 = jnp.where(mask_c, _anticausal_conv(wcf_ref, dzbuf), 0.0)
        dp = jnp.where(mask_c, _anticausal_conv(wsc_ref, dcbuf), 0.0)
        parts = (db_ref[...].astype(F32), dp * v, dp * c, dz0 * sg, dz0 * a * sg * (1.0 - sg))
        for k, part in enumerate(parts):
            du_ref[:, k * C1:(k + 1) * C1] = part.astype(BF16)
            dbin_ref[:, k * C1:(k + 1) * C1] += jnp.sum(part, axis=0, keepdims=True)

    cur, prev, small = _mixer_specs(T, DIN, C1, ksc, kcf)
    tile = lambda: pl.BlockSpec((TT, C1), lambda i: (i, 0))
    nxt = lambda: pl.BlockSpec((HALO, C1), lambda i: (jnp.minimum((i + 1) * (TT // HALO), T // HALO - 1), 0))
    blk = (_nbytes((TT + HALO, DIN), F32) + _nbytes((TT, DIN), BF16) + 5 * _nbytes((TT, C1), F32)
           + 16 * _nbytes((TT + HALO, C1), F32))
    buf = lambda: pltpu.VMEM((TT + HALO, C1), F32)
    return pl.pallas_call(
        body, name="mix_conv_bwd2", grid=(T // TT,),
        in_specs=[cur, prev, tile(), nxt(), tile(), nxt(), tile(), small[0], small[1]],
        out_specs=[pl.BlockSpec((TT, DIN), lambda i: (i, 0)), pl.BlockSpec((1, DIN), lambda i: (0, 0)),
                   pl.BlockSpec((ksc, C1), lambda i: (0, 0)), pl.BlockSpec((kcf, C1), lambda i: (0, 0))],
        out_shape=[jax.ShapeDtypeStruct((T, DIN), BF16), jax.ShapeDtypeStruct((1, DIN), F32),
                   jax.ShapeDtypeStruct((ksc, C1), F32), jax.ShapeDtypeStruct((kcf, C1), F32)],
        scratch_shapes=[buf(), buf(), buf(), buf()],
        compiler_params=_params(("arbitrary",), blk),
    )(u, u, dz1, dz1, dcs, dcs, db, wsc, wcf)


def _place():
    x, y, c = lax.axis_index("x"), lax.axis_index("y"), lax.axis_index("c")
    chips = [(1 - x, y), (x, 1 - y), (1 - x, 1 - y)]
    return x, y, c, chips


ANY = pl.BlockSpec(memory_space=pl.ANY)


def _cast_own_block(place, w, name):
    R, C = w.shape
    tr = _row_tile(R // 2, 256, 16)
    nblk = R // 2 // tr

    def body(place_ref, w_ref, o_ref):
        o_ref[...] = w_ref[...].astype(BF16)

    return pl.pallas_call(
        body, name=name,
        grid_spec=pltpu.PrefetchScalarGridSpec(
            num_scalar_prefetch=1, grid=(2, nblk),
            in_specs=[pl.BlockSpec((tr, C), lambda h, i, p: (h * nblk + i, 0))],
            out_specs=pl.BlockSpec((None, None, tr, C), lambda h, i, p: (p[0], h, i, 0))),
        out_shape=jax.ShapeDtypeStruct((N_CHIPS, 2, R // 2, C), BF16),
        compiler_params=_params(("parallel", "parallel"), _nbytes((tr, C), F32) + _nbytes((tr, C), BF16)),
    )(place, w)


def _gather_weights(bufs, after):
    nw = len(bufs)

    def body(*refs):
        o_refs = refs[nw + 1:2 * nw + 1]
        send, recv = refs[2 * nw + 1:]
        x, y, c, chips = _place()
        s = 2 * x + y
        sib = (x, y, 1 - c)

        def remote(w, k, blk, half, to):
            ref = o_refs[w].at[blk, half]
            return pltpu.make_async_remote_copy(src_ref=ref, dst_ref=ref, send_sem=send.at[6 * w + k],
                                                recv_sem=recv.at[6 * w + k], device_id=to, device_id_type=MESH)

        sends = []
        for w in range(nw):
            for r, (tx, ty) in enumerate(chips):
                cp = remote(w, r, s, c, (tx, ty, c))
                cp.start()
                sends.append(cp)
        for w in range(nw):
            for r, (tx, ty) in enumerate(chips):
                sr = 2 * tx + ty
                remote(w, r, sr, c, (tx, ty, c)).wait_recv()
                cp = remote(w, 3 + r, sr, c, sib)
                cp.start()
                sends.append(cp)
        for w in range(nw):
            for r, (tx, ty) in enumerate(chips):
                remote(w, 3 + r, 2 * tx + ty, 1 - c, sib).wait_recv()
        for cp in sends:
            cp.wait_send()

    return pl.pallas_call(
        body, name="gather_weights", in_specs=[ANY] * (nw + 1), out_specs=[ANY] * nw,
        out_shape=[jax.ShapeDtypeStruct(b.shape, b.dtype) for b in bufs],
        input_output_aliases={w: w for w in range(nw)},
        scratch_shapes=[pltpu.SemaphoreType.DMA((6 * nw,)), pltpu.SemaphoreType.DMA((6 * nw,))],
    )(*bufs, after)


HBM = pl.BlockSpec(memory_space=pltpu.HBM)
SEM = pl.BlockSpec(memory_space=pltpu.SEMAPHORE)
EFFECT = pltpu.SideEffectType.DATAFLOW_SIDE_EFFECTING


def _gather_copies(refs, send, recv):
    x, y, c, chips = _place()
    s = 2 * x + y
    return [pltpu.make_async_remote_copy(src_ref=ref.at[s, c], dst_ref=ref.at[s, c], send_sem=send.at[3 * w + r],
                                         recv_sem=recv.at[3 * w + r], device_id=(tx, ty, c), device_id_type=MESH)
            for w, ref in enumerate(refs) for r, (tx, ty) in enumerate(chips)]


def _scatter_copies(refs, send, recv):
    x, y, c, chips = _place()
    nw = len(refs) // 2
    return [pltpu.make_async_remote_copy(src_ref=refs[w].at[2 * tx + ty], dst_ref=refs[nw + w].at[r],
                                         send_sem=send.at[3 * w + r], recv_sem=recv.at[3 * w + r],
                                         device_id=(tx, ty, c), device_id_type=MESH)
            for w in range(nw) for r, (tx, ty) in enumerate(chips)]


def _pair_copies(refs, send, recv):
    x, y, c, _ = _place()
    nw = len(refs) // 2
    return [pltpu.make_async_remote_copy(src_ref=refs[w].at[j, 1 - c], dst_ref=refs[nw + w].at[j],
                                         send_sem=send.at[N_CHIPS * w + j], recv_sem=recv.at[N_CHIPS * w + j],
                                         device_id=(x, y, 1 - c), device_id_type=MESH)
            for w in range(nw) for j in range(N_CHIPS)]


def _start_copies(bufs, after, ncopies, make_copies, name):
    n = len(bufs)

    def body(*refs):
        in_refs, send, recv, token = refs[:n], refs[n + 1], refs[n + 2], refs[2 * n + 3]
        for cp in make_copies(in_refs, send, recv):
            cp.start()
        token[...] = jnp.zeros_like(token)

    outs = pl.pallas_call(
        body, name=name, in_specs=[HBM] * n + [ANY],
        out_specs=[SEM, SEM] + [HBM] * n + [pl.BlockSpec(memory_space=pltpu.VMEM)],
        out_shape=[pltpu.SemaphoreType.DMA((ncopies,)), pltpu.SemaphoreType.DMA((ncopies,))]
                  + [pltpu.HBM(b.shape, b.dtype) for b in bufs] + [jax.ShapeDtypeStruct((8, 128), F32)],
        input_output_aliases={k: 2 + k for k in range(n)},
        compiler_params=pltpu.CompilerParams(has_side_effects=EFFECT),
    )(*[pltpu.with_memory_space_constraint(b, pltpu.HBM) for b in bufs], after)
    return outs[0], outs[1], list(outs[2:2 + n]), outs[2 + n]


def _wait_copies(send, recv, bufs, after, make_copies, name):
    n = len(bufs)

    def body(*refs):
        in_refs, send_ref, recv_ref = refs[:n], refs[n], refs[n + 1]
        for cp in make_copies(in_refs, send_ref, recv_ref):
            cp.wait_send()
            cp.wait_recv()

    outs = pl.pallas_call(
        body, name=name, in_specs=[HBM] * n + [SEM, SEM, ANY], out_specs=[HBM] * n,
        out_shape=[pltpu.HBM(b.shape, b.dtype) for b in bufs],
        input_output_aliases={k: k for k in range(n)},
        compiler_params=pltpu.CompilerParams(has_side_effects=EFFECT),
    )(*bufs, send, recv, after)
    return list(outs)


def _forward_halves(bufs, name):
    nw = len(bufs)

    def body(*refs):
        o_refs = refs[nw:2 * nw]
        send, recv = refs[2 * nw:]
        x, y, c, chips = _place()
        sib = (x, y, 1 - c)
        copies = []
        for w in range(nw):
            for r, (tx, ty) in enumerate(chips):
                ref = o_refs[w].at[2 * tx + ty, c]
                cp = pltpu.make_async_remote_copy(src_ref=ref, dst_ref=ref, send_sem=send.at[3 * w + r],
                                                  recv_sem=recv.at[3 * w + r], device_id=sib, device_id_type=MESH)
                cp.start()
                copies.append(cp)
        for w in range(nw):
            for r, (tx, ty) in enumerate(chips):
                ref = o_refs[w].at[2 * tx + ty, 1 - c]
                pltpu.make_async_remote_copy(src_ref=ref, dst_ref=ref, send_sem=send.at[3 * w + r],
                                             recv_sem=recv.at[3 * w + r], device_id=sib, device_id_type=MESH).wait_recv()
        for cp in copies:
            cp.wait_send()

    return pl.pallas_call(
        body, name=name, in_specs=[ANY] * nw, out_specs=[ANY] * nw,
        out_shape=[jax.ShapeDtypeStruct(b.shape, b.dtype) for b in bufs],
        input_output_aliases={w: w for w in range(nw)},
        scratch_shapes=[pltpu.SemaphoreType.DMA((3 * nw,)), pltpu.SemaphoreType.DMA((3 * nw,))],
    )(*bufs)


def _half_exchange(hs, name):
    nw = len(hs)

    def body(*refs):
        o_refs = refs[nw:2 * nw]
        send, recv = refs[2 * nw:]
        x, y, c, _ = _place()
        sib = (x, y, 1 - c)
        copies = []
        for w in range(nw):
            cp = pltpu.make_async_remote_copy(src_ref=o_refs[w].at[c], dst_ref=o_refs[w].at[c], send_sem=send.at[w],
                                              recv_sem=recv.at[w], device_id=sib, device_id_type=MESH)
            cp.start()
            copies.append(cp)
        for w, cp in enumerate(copies):
            cp.wait_send()
            pltpu.make_async_remote_copy(src_ref=o_refs[w].at[c], dst_ref=o_refs[w].at[1 - c], send_sem=send.at[w],
                                         recv_sem=recv.at[w], device_id=sib, device_id_type=MESH).wait_recv()

    return pl.pallas_call(
        body, name=name, in_specs=[ANY] * nw, out_specs=[ANY] * nw,
        out_shape=[jax.ShapeDtypeStruct(h.shape, F32) for h in hs],
        input_output_aliases={w: w for w in range(nw)},
        scratch_shapes=[pltpu.SemaphoreType.DMA((nw,)), pltpu.SemaphoreType.DMA((nw,))],
    )(*hs)


def _share_small(v, reduce, name):
    R, C = v.shape

    def body(v_ref, o_ref, *scratch):
        if reduce:
            all_ref, send, recv, lsem = scratch
        else:
            all_ref = o_ref
            send, recv, lsem = scratch
        x, y, c, _ = _place()
        me = 4 * x + 2 * y + c
        loc = pltpu.make_async_copy(v_ref, all_ref.at[me], lsem)
        loc.start()
        copies = []
        for k in range(1, N_DEV):
            kx, ky, kc = (k >> 2) & 1, (k >> 1) & 1, k & 1
            peer = (x ^ kx, y ^ ky, c ^ kc)
            cp = pltpu.make_async_remote_copy(src_ref=v_ref, dst_ref=all_ref.at[me], send_sem=send.at[k - 1],
                                              recv_sem=recv.at[k - 1], device_id=peer, device_id_type=MESH)
            cp.start()
            copies.append(cp)
        for k in range(1, N_DEV):
            kx, ky, kc = (k >> 2) & 1, (k >> 1) & 1, k & 1
            src = 4 * (x ^ kx) + 2 * (y ^ ky) + (c ^ kc)
            pltpu.make_async_remote_copy(src_ref=v_ref, dst_ref=all_ref.at[src], send_sem=send.at[k - 1],
                                         recv_sem=recv.at[k - 1], device_id=(x, y, c), device_id_type=MESH).wait_recv()
        for cp in copies:
            cp.wait_send()
        loc.wait()
        if reduce:
            total = all_ref[0]
            for d in range(1, N_DEV):
                total = total + all_ref[d]
            o_ref[...] = total

    vm = pl.BlockSpec(memory_space=pltpu.VMEM)
    sems = [pltpu.SemaphoreType.DMA((N_DEV - 1,)), pltpu.SemaphoreType.DMA((N_DEV - 1,)), pltpu.SemaphoreType.DMA]
    if reduce:
        out_shape = jax.ShapeDtypeStruct((R, C), F32)
        scratch = [pltpu.VMEM((N_DEV, R, C), F32)] + sems
    else:
        out_shape = jax.ShapeDtypeStruct((N_DEV, R, C), F32)
        scratch = sems
    return pl.pallas_call(
        body, name=name, in_specs=[vm], out_specs=vm, out_shape=out_shape, scratch_shapes=scratch,
        compiler_params=pltpu.CompilerParams(vmem_limit_bytes=int(min(4 * N_DEV * R * C * 4 + 2 ** 24, 2 ** 25 + 2 ** 24))),
    )(v)


def _pair_sum(place, g, rb, name):
    _, _, Rh, C = g.shape
    tr = _row_tile(Rh, 256, 16)

    def body(place_ref, g_ref, r_ref, q_ref):
        q_ref[...] = (g_ref[...] + r_ref[...]).astype(BF16)

    blk = 2 * _nbytes((tr, C), F32) + _nbytes((tr, C), BF16)
    return pl.pallas_call(
        body, name=name,
        grid_spec=pltpu.PrefetchScalarGridSpec(
            num_scalar_prefetch=1, grid=(N_CHIPS, Rh // tr),
            in_specs=[pl.BlockSpec((None, None, tr, C), lambda j, i, p: (j, p[1], i, 0)),
                      pl.BlockSpec((None, tr, C), lambda j, i, p: (j, i, 0))],
            out_specs=pl.BlockSpec((None, tr, C), lambda j, i, p: (j, i, 0))),
        out_shape=jax.ShapeDtypeStruct((N_CHIPS, Rh, C), BF16),
        compiler_params=_params(("parallel", "parallel"), blk),
    )(place, g, rb)


def _chip_sum(place, g, rb, rc, name):
    _, _, Rh, C = g.shape
    tr = _row_tile(Rh, 256, 16)

    def body(place_ref, g_ref, r_ref, rc_ref, o_ref):
        total = g_ref[...] + r_ref[...]
        for r in range(3):
            total = total + rc_ref[r].astype(F32)
        o_ref[...] = total

    blk = 3 * _nbytes((tr, C), F32) + 3 * _nbytes((tr, C), BF16)
    return pl.pallas_call(
        body, name=name,
        grid_spec=pltpu.PrefetchScalarGridSpec(
            num_scalar_prefetch=1, grid=(Rh // tr,),
            in_specs=[pl.BlockSpec((None, None, tr, C), lambda i, p: (p[0], p[1], i, 0)),
                      pl.BlockSpec((None, tr, C), lambda i, p: (p[0], i, 0)),
                      pl.BlockSpec((3, tr, C), lambda i, p: (0, i, 0))],
            out_specs=pl.BlockSpec((None, tr, C), lambda i, p: (p[1], i, 0))),
        out_shape=jax.ShapeDtypeStruct((2, Rh, C), F32),
        compiler_params=_params(("parallel",), blk),
    )(place, g, rb, rc)


def _adamw_math(w, g, m, v):
    m = ADAM_B1 * m + (1.0 - ADAM_B1) * g
    v = ADAM_B2 * v + (1.0 - ADAM_B2) * jnp.square(g)
    m_hat = m / (1.0 - ADAM_B1 ** ADAM_STEP)
    v_hat = v / (1.0 - ADAM_B2 ** ADAM_STEP)
    delta = -ADAM_LR * (m_hat / (jnp.sqrt(v_hat) + ADAM_EPS) + ADAM_WD * w)
    return delta, m, v


def _adamw(w, g, m, v, name):
    R, C = w.shape
    tr = _row_tile(R, 256)

    def body(w_ref, g_ref, m_ref, v_ref, d_ref, nm_ref, nv_ref):
        d, nm, nv = _adamw_math(w_ref[...], g_ref[...], m_ref[...], v_ref[...])
        d_ref[...] = d
        nm_ref[...] = nm
        nv_ref[...] = nv

    spec = pl.BlockSpec((tr, C), lambda i: (i, 0))
    shp = jax.ShapeDtypeStruct((R, C), F32)
    return pl.pallas_call(
        body, name=name, grid=(R // tr,), in_specs=[spec] * 4, out_specs=[spec] * 3, out_shape=[shp] * 3,
        compiler_params=_params(("parallel",), 7 * _nbytes((tr, C), F32)),
    )(w, g, m, v)


def _adamw_small(ws, gs, ms, vs):
    n = len(ws)

    def body(*refs):
        for k in range(n):
            w_ref, g_ref, m_ref, v_ref = (refs[q * n + k] for q in range(4))
            d, nm, nv = _adamw_math(w_ref[...], g_ref[...], m_ref[...], v_ref[...])
            refs[4 * n + k][...] = d
            refs[5 * n + k][...] = nm
            refs[6 * n + k][...] = nv

    vm = pl.BlockSpec(memory_space=pltpu.VMEM)
    shapes = [jax.ShapeDtypeStruct(w.shape, F32) for w in ws]
    outs = pl.pallas_call(
        body, name="adamw_small", in_specs=[vm] * (4 * n), out_specs=[vm] * (3 * n), out_shape=shapes * 3,
    )(*ws, *gs, *ms, *vs)
    return outs[:n], outs[n:2 * n], outs[2 * n:]


def _pad_rows(a, rows):
    return jnp.pad(a, ((0, rows - a.shape[0]), (0, 0)))


def kernel(x, meta_tokens, ffn1_norm, ffn1_w_gate, ffn1_w_up, ffn1_w_down, mix_norm, w_in, b_in, conv_sc_w, conv_cf_w, conv_cf_b, ln_cf_g, ln_cf_b, w_out, ffn2_norm, ffn2_w_gate, ffn2_w_up, ffn2_w_down, final_norm, loss_target, m_meta_tokens, m_ffn1_norm, m_ffn1_w_gate, m_ffn1_w_up, m_ffn1_w_down, m_mix_norm, m_w_in, m_b_in, m_conv_sc_w, m_conv_cf_w, m_conv_cf_b, m_ln_cf_g, m_ln_cf_b, m_w_out, m_ffn2_norm, m_ffn2_w_gate, m_ffn2_w_up, m_ffn2_w_down, m_final_norm, v_meta_tokens, v_ffn1_norm, v_ffn1_w_gate, v_ffn1_w_up, v_ffn1_w_down, v_mix_norm, v_w_in, v_b_in, v_conv_sc_w, v_conv_cf_w, v_conv_cf_b, v_ln_cf_g, v_ln_cf_b, v_w_out, v_ffn2_norm, v_ffn2_w_gate, v_ffn2_w_up, v_ffn2_w_down, v_final_norm):
    xi, yi, ci = lax.axis_index("x"), lax.axis_index("y"), lax.axis_index("c")
    chip = 2 * xi + yi
    place = jnp.stack([chip, ci]).astype(jnp.int32)

    x2 = x[0]
    tgt = loss_target[0]
    S, D = x2.shape
    C1 = D // 2
    cs = conv_sc_w.shape[2]
    ksc, kcf = conv_sc_w.shape[1], conv_cf_w.shape[1]
    ms = meta_tokens.shape[1]

    rows_small = N_META + 8 + 32
    assert ksc <= 8 and kcf <= 32 and cs <= ms
    pack = jnp.concatenate([
        meta_tokens,
        jnp.pad(conv_sc_w[0], ((0, 8 - ksc), (0, ms - cs))),
        jnp.pad(conv_cf_w[0], ((0, 32 - kcf), (0, ms - cs)))], axis=0)
    everyone = _share_small(pack, False, "share_params")[0::2]
    meta_full = jnp.transpose(everyone[:, :N_META, :], (1, 0, 2)).reshape(N_META, D)
    wsc_full = jnp.transpose(everyone[:, N_META:N_META + ksc, :cs], (1, 0, 2)).reshape(ksc, C1)
    wcf_full = jnp.transpose(everyone[:, N_META + 8:N_META + 8 + kcf, :cs], (1, 0, 2)).reshape(kcf, C1)

    big = {"ffn1_w_gate": ffn1_w_gate, "ffn1_w_up": ffn1_w_up, "ffn1_w_down": ffn1_w_down, "w_in": w_in, "w_out": w_out,
           "ffn2_w_gate": ffn2_w_gate, "ffn2_w_up": ffn2_w_up, "ffn2_w_down": ffn2_w_down}
    big_m = {"ffn1_w_gate": m_ffn1_w_gate, "ffn1_w_up": m_ffn1_w_up, "ffn1_w_down": m_ffn1_w_down, "w_in": m_w_in,
             "w_out": m_w_out, "ffn2_w_gate": m_ffn2_w_gate, "ffn2_w_up": m_ffn2_w_up, "ffn2_w_down": m_ffn2_w_down}
    big_v = {"ffn1_w_gate": v_ffn1_w_gate, "ffn1_w_up": v_ffn1_w_up, "ffn1_w_down": v_ffn1_w_down, "w_in": v_w_in,
             "w_out": v_w_out, "ffn2_w_gate": v_ffn2_w_gate, "ffn2_w_up": v_ffn2_w_up, "ffn2_w_down": v_ffn2_w_down}
    buf = {nm: _cast_own_block(place, w[0], "cast_" + nm) for nm, w in big.items()}
    whole_weight = lambda g: g.reshape(N_CHIPS, 2 * g.shape[2], g.shape[3])
    group_mix, group_ffn2 = ["w_in", "w_out"], ["ffn2_w_gate", "ffn2_w_up", "ffn2_w_down"]

    corner = lambda a: a.reshape(-1, a.shape[-1])[:8, :128]
    wg1, wu1, wd1 = (whole_weight(g) for g in _gather_weights(
        [buf[nm] for nm in ["ffn1_w_gate", "ffn1_w_up", "ffn1_w_down"]], corner(everyone)))
    send_mix, recv_mix, thru_mix, token_mix = _start_copies(
        [buf[nm] for nm in group_mix], corner(wd1), 3 * len(group_mix), _gather_copies, "gather_start_mix")
    send_ffn2, recv_ffn2, thru_ffn2, token_ffn2 = _start_copies(
        [buf[nm] for nm in group_ffn2], token_mix, 3 * len(group_ffn2), _gather_copies, "gather_start_ffn2")
    F = N_CHIPS * wd1.shape[1]

    hs0, n1 = _embed_rms(x2, meta_full, ffn1_norm)
    g1, u1, a1 = _ffn_up(n1, wg1, wu1, token_ffn2, "ffn1_up")
    hs1 = _ffn_down(a1, wd1.reshape(F, D), hs0, "ffn1_down")
    arrived = _wait_copies(send_mix, recv_mix, thru_mix, corner(hs1), _gather_copies, "gather_wait_mix")
    win, wout = (whole_weight(g) for g in _forward_halves(arrived, "gather_forward_mix"))
    n2 = _rms(hs1, mix_norm, "rms_mix")
    u = _mix_in(n2, win, b_in)
    y = _mix_conv_fwd(u, wsc_full, wcf_full, conv_cf_b, ln_cf_g, ln_cf_b)
    hs2 = _mix_out(y, wout.reshape(D, D), hs1)
    arrived = _wait_copies(send_ffn2, recv_ffn2, thru_ffn2, corner(hs2), _gather_copies, "gather_wait_ffn2")
    wg2, wu2, wd2 = (whole_weight(g) for g in _forward_halves(arrived, "gather_forward_ffn2"))
    n3 = _rms(hs2, ffn2_norm, "rms_ffn2")
    g2, u2, a2 = _ffn_up(n3, wg2, wu2, token_ffn2, "ffn2_up")
    hs3 = _ffn_down(a2, wd2.reshape(F, D), hs2, "ffn2_down")

    def pair_start(group, after, tag):
        gs = [g for _, g in group]
        lands = [lax.empty((N_CHIPS,) + g.shape[2:], F32) for g in gs]
        send, recv, thru, token = _start_copies(gs + lands, after, N_CHIPS * len(gs), _pair_copies,
                                                "pair_start_" + tag)
        return (group, send, recv, thru, tag), token

    def scatter_start(state, after):
        group, send, recv, thru, tag = state
        thru = _wait_copies(send, recv, thru, corner(after), _pair_copies, "pair_wait_" + tag)
        gs, sib = thru[:len(group)], thru[len(group):]
        sums = [_pair_sum(place, g, rb, "pair_sum_" + nm) for (nm, _), g, rb in zip(group, gs, sib)]
        lands = [lax.empty((3,) + q.shape[1:], BF16) for q in sums]
        send, recv, thru, token = _start_copies(sums + lands, corner(sums[-1]), 3 * len(gs), _scatter_copies,
                                                "scatter_start_" + tag)
        return ([(nm, g) for (nm, _), g in zip(group, gs)], sib, send, recv, thru, tag), token

    def reduce_finish(state, after):
        group, sib, send, recv, thru, tag = state
        lands = _wait_copies(send, recv, thru, corner(after), _scatter_copies, "scatter_wait_" + tag)[len(group):]
        mine = [_chip_sum(place, g, rb, rc, "chip_sum_" + nm) for (nm, g), rb, rc in zip(group, sib, lands)]
        whole = _half_exchange(mine, "half_exchange_" + tag)
        out = {}
        for (nm, _), g in zip(group, whole):
            w = big[nm]
            g3d = g.reshape(w.shape)
            d, new_m, new_v = _adamw(w[0], g3d[0], big_m[nm][0], big_v[nm][0], "adamw_" + nm)
            out[nm] = (g3d, d[None], new_m[None], new_v[None])
        return out

    dhs3, df2, loss_row, d_final = _final_loss(hs3, final_norm.reshape(1, D), tgt)

    dg2, du2 = _ffn_bwd_act(df2, wd2, g2, u2, token_ffn2, "ffn2_bwd_act")
    gw_d2 = _wgrad_down(a2, df2, "wgrad_ffn2_down")
    gw_g2 = _wgrad_cols(n3, [dg2], "wgrad_ffn2_gate")[0]
    gw_u2 = _wgrad_cols(n3, [du2], "wgrad_ffn2_up")[0]
    pair_ffn2, token = pair_start([("ffn2_w_gate", gw_g2), ("ffn2_w_up", gw_u2), ("ffn2_w_down", gw_d2)],
                                  corner(gw_u2), "ffn2")
    dn3 = _nt_panel([dg2, du2], [wg2, wu2], token, "ffn2_bwd_in")
    red_ffn2, token = scatter_start(pair_ffn2, dn3)
    dhs2, dm, d_ffn2 = _rms_bwd(dn3, hs2, ffn2_norm, dhs3, 1.0, "rms_bwd_ffn2")

    dy = _nt_panel([dm], [wout.reshape(1, D, D)], token, "mix_bwd_out")
    gw_out = _wgrad_out(y, dm)
    dz1, dcs, db, d_lg, d_lb, d_bcf = _mix_conv_bwd1(u, dy, wsc_full, wcf_full, conv_cf_b, ln_cf_g, ln_cf_b)
    du, d_bin, d_wsc, d_wcf = _mix_conv_bwd2(u, dz1, dcs, db, wsc_full, wcf_full)
    gw_in = _wgrad_cols(n2, [du], "wgrad_w_in")[0]
    pair_mix, token = pair_start([("w_in", gw_in), ("w_out", gw_out)], corner(gw_in), "mix")
    dn2 = _nt_panel([du], [win], token, "mix_bwd_in")
    red_mix, token = scatter_start(pair_mix, dn2)
    dhs1, df1, d_mix = _rms_bwd(dn2, hs1, mix_norm, dhs2, FFN_RES_SCALE, "rms_bwd_mix")

    dg1, du1 = _ffn_bwd_act(df1, wd1, g1, u1, token, "ffn1_bwd_act")
    gw_d1 = _wgrad_down(a1, df1, "wgrad_ffn1_down")
    gw_g1 = _wgrad_cols(n1, [dg1], "wgrad_ffn1_gate")[0]
    pair_ffn1a, token = pair_start([("ffn1_w_down", gw_d1), ("ffn1_w_gate", gw_g1)], corner(gw_g1), "ffn1a")
    gw_u1 = _wgrad_cols(n1, [du1], "wgrad_ffn1_up", token)[0]
    red_ffn1a, token = scatter_start(pair_ffn1a, gw_u1)
    pair_ffn1b, token = pair_start([("ffn1_w_up", gw_u1)], token, "ffn1b")
    dn1 = _nt_panel([dg1, du1], [wg1, wu1], token, "ffn1_bwd_in")
    red_ffn1b, token = scatter_start(pair_ffn1b, dn1)
    grad_x, d_meta, d_ffn1 = _rms_bwd_first(dn1, hs0, ffn1_norm, dhs1, token)

    big_out = reduce_finish(red_ffn2, grad_x)
    big_out.update(reduce_finish(red_mix, big_out["ffn2_w_down"][1]))
    big_out.update(reduce_finish(red_ffn1a, big_out["w_out"][1]))
    big_out.update(reduce_finish(red_ffn1b, big_out["ffn1_w_gate"][1]))

    W = C1
    rows = lambda a: a.reshape(-1, W)
    parts = [rows(d_ffn1), rows(d_mix), rows(d_ffn2), rows(d_final), rows(d_bin), d_bcf, d_lg, d_lb,
             d_wsc, d_wcf, rows(d_meta), jnp.broadcast_to(loss_row[:, :1], (1, W))]
    sizes = [p.shape[0] for p in parts]
    total_rows = sum(sizes)
    packed = _pad_rows(jnp.concatenate(parts, axis=0), -(-total_rows // 8) * 8)
    summed = _share_small(packed, True, "sum_small")
    offs = [0]
    for n in sizes:
        offs.append(offs[-1] + n)
    piece = lambda k: summed[offs[k]:offs[k + 1]]
    loss = piece(11)[0, 0]
    g_ffn1, g_mix, g_ffn2 = (piece(k).reshape(1, D) for k in range(3))
    g_final = piece(3).reshape(1, D)
    g_bin = piece(4).reshape(1, -1)
    g_bcf, g_lg, g_lb = piece(5), piece(6), piece(7)
    g_wsc = lax.dynamic_slice_in_dim(piece(8), chip * cs, cs, axis=1)
    g_wcf = lax.dynamic_slice_in_dim(piece(9), chip * cs, cs, axis=1)
    g_meta = lax.dynamic_slice_in_dim(piece(10).reshape(N_META, D), chip * ms, ms, axis=1)

    small_names = ["meta_tokens", "ffn1_norm", "mix_norm", "b_in", "conv_sc_w", "conv_cf_w", "conv_cf_b", "ln_cf_g",
                   "ln_cf_b", "ffn2_norm", "final_norm"]
    small_w = [meta_tokens, ffn1_norm, mix_norm, b_in, conv_sc_w[0], conv_cf_w[0], conv_cf_b, ln_cf_g, ln_cf_b,
               ffn2_norm, final_norm.reshape(1, D)]
    small_g = [g_meta, g_ffn1, g_mix, g_bin, g_wsc, g_wcf, g_bcf, g_lg, g_lb, g_ffn2, g_final]
    small_m = [m_meta_tokens, m_ffn1_norm, m_mix_norm, m_b_in, m_conv_sc_w[0], m_conv_cf_w[0], m_conv_cf_b, m_ln_cf_g,
               m_ln_cf_b, m_ffn2_norm, m_final_norm.reshape(1, D)]
    small_v = [v_meta_tokens, v_ffn1_norm, v_mix_norm, v_b_in, v_conv_sc_w[0], v_conv_cf_w[0], v_conv_cf_b, v_ln_cf_g,
               v_ln_cf_b, v_ffn2_norm, v_final_norm.reshape(1, D)]
    s_d, s_m, s_v = _adamw_small(small_w, small_g, small_m, small_v)
    shapes = {"conv_sc_w": conv_sc_w.shape, "conv_cf_w": conv_cf_w.shape, "final_norm": final_norm.shape}
    small_out = {}
    for nm, g, d, m, v in zip(small_names, small_g, s_d, s_m, s_v):
        shp = shapes.get(nm, g.shape)
        small_out[nm] = tuple(t.reshape(shp) for t in (g, d, m, v))

    order = ["meta_tokens", "ffn1_norm", "ffn1_w_gate", "ffn1_w_up", "ffn1_w_down", "mix_norm", "w_in", "b_in",
             "conv_sc_w", "conv_cf_w", "conv_cf_b", "ln_cf_g", "ln_cf_b", "w_out", "ffn2_norm", "ffn2_w_gate",
             "ffn2_w_up", "ffn2_w_down", "final_norm"]
    res = {**big_out, **small_out}
    outs = [loss, grad_x[None]]
    for q in range(4):
        outs.extend(res[nm][q] for nm in order)
    return tuple(outs)
```

```python
import functools

import jax
import jax.numpy as jnp
from jax import lax
from jax.experimental import pallas as pl
from jax.experimental.pallas import tpu as pltpu

F32 = jnp.float32
BF16 = jnp.bfloat16
MESH = pl.DeviceIdType.MESH

N_META = 16
TT = 128
PAD = TT - N_META
HALO = 32
EPS = 1e-6
FFN_RES_SCALE = 0.5
N_CHIPS = 4
N_DEV = 8

ADAM_LR = 0.001
ADAM_B1 = 0.9
ADAM_B2 = 0.999
ADAM_EPS = 1e-08
ADAM_WD = 0.01
ADAM_STEP = 10

V7X_VMEM_BYTES = 64 * 2 ** 20
NT_DIMS = (((1,), (1,)), ((), ()))
TN_DIMS = (((0,), (0,)), ((), ()))


def _params(semantics, block_bytes):
    limit = min(2 * block_bytes + 16 * 2 ** 20, V7X_VMEM_BYTES - 6 * 2 ** 20)
    return pltpu.CompilerParams(dimension_semantics=semantics, vmem_limit_bytes=int(limit))


def _pallas(body, out_shape, **kw):
    if "grid" not in kw and "grid_spec" not in kw:
        return pl.pallas_call(body, out_shape=out_shape, **kw)
    big = lambda shape, dtype: jnp.issubdtype(dtype, jnp.floating) and len(shape) >= 2
    pin_out = lambda s: pltpu.HBM(s.shape, s.dtype) if big(s.shape, s.dtype) else s
    single = not isinstance(out_shape, (list, tuple))
    shapes = pin_out(out_shape) if single else [pin_out(s) for s in out_shape]
    call = pl.pallas_call(body, out_shape=shapes, **kw)
    pin = lambda a: pltpu.with_memory_space_constraint(a, pltpu.HBM) if big(a.shape, a.dtype) else a
    return lambda *operands: call(*[pin(a) for a in operands])


def _nbytes(shape, dtype):
    n = 1
    for d in shape:
        if d is not None:
            n *= d
    return n * jnp.dtype(dtype).itemsize


def _row_tile(rows, target, mult=8):
    best = None
    for t in range(mult, min(rows, target) + 1, mult):
        if rows % t == 0:
            best = t
    assert best is not None, (rows, target, mult)
    return best


def _sigmoid(v):
    return jax.nn.sigmoid(v)


def _dsilu(v, s):
    return s * (1.0 + v * (1.0 - s))


def _embed_rms(x2, meta, gain):
    S, D = x2.shape
    T = S + TT

    def body(x_ref, meta_ref, g_ref, hs_ref, n_ref):
        i = pl.program_id(0)

        @pl.when(i == 0)
        def _():
            hs_ref[...] = jnp.zeros_like(hs_ref)
            hs_ref[PAD:, :] = meta_ref[...]

        @pl.when(i > 0)
        def _():
            hs_ref[...] = x_ref[...]

        h = hs_ref[...]
        r = lax.rsqrt(jnp.mean(h * h, axis=-1, keepdims=True) + EPS)
        n_ref[...] = ((h * r) * g_ref[...]).astype(BF16)

    blk = _nbytes((TT, D), F32) * 2 + _nbytes((TT, D), BF16)
    return _pallas(
        body, name="embed_rms", grid=(T // TT,),
        in_specs=[pl.BlockSpec((TT, D), lambda i: (jnp.maximum(i - 1, 0), 0)),
                  pl.BlockSpec((N_META, D), lambda i: (0, 0)),
                  pl.BlockSpec((1, D), lambda i: (0, 0))],
        out_specs=[pl.BlockSpec((TT, D), lambda i: (i, 0)), pl.BlockSpec((TT, D), lambda i: (i, 0))],
        out_shape=[jax.ShapeDtypeStruct((T, D), F32), jax.ShapeDtypeStruct((T, D), BF16)],
        compiler_params=_params(("parallel",), blk),
    )(x2, meta, gain)


def _rms(hs, gain, name):
    T, D = hs.shape
    te = _row_tile(T, 384)

    def body(h_ref, g_ref, n_ref):
        h = h_ref[...]
        r = lax.rsqrt(jnp.mean(h * h, axis=-1, keepdims=True) + EPS)
        n_ref[...] = ((h * r) * g_ref[...]).astype(BF16)

    blk = _nbytes((te, D), F32) + _nbytes((te, D), BF16)
    return _pallas(
        body, name=name, grid=(T // te,),
        in_specs=[pl.BlockSpec((te, D), lambda i: (i, 0)), pl.BlockSpec((1, D), lambda i: (0, 0))],
        out_specs=pl.BlockSpec((te, D), lambda i: (i, 0)),
        out_shape=jax.ShapeDtypeStruct((T, D), BF16),
        compiler_params=_params(("parallel",), blk),
    )(hs, gain)


def _rms_bwd_math(dn, h, g):
    r = lax.rsqrt(jnp.mean(h * h, axis=-1, keepdims=True) + EPS)
    xh = h * r
    dgain = jnp.sum(dn * xh, axis=0, keepdims=True)
    dxh = dn * g
    dh = r * (dxh - xh * jnp.mean(dxh * xh, axis=-1, keepdims=True))
    return dh, dgain


def _rms_bwd(dn, hs, gain, dres, scale, name):
    T, D = hs.shape
    te = _row_tile(T, 384)

    def body(dn_ref, h_ref, g_ref, dres_ref, dhs_ref, dhb_ref, dg_ref):
        dh, dgain = _rms_bwd_math(dn_ref[...], h_ref[...], g_ref[...])
        d = dres_ref[...] + dh
        dhs_ref[...] = d
        dhb_ref[...] = (scale * d).astype(BF16)

        @pl.when(pl.program_id(0) == 0)
        def _():
            dg_ref[...] = jnp.zeros_like(dg_ref)

        dg_ref[...] += dgain

    blk = _nbytes((te, D), F32) * 4 + _nbytes((te, D), BF16)
    row = lambda i: (i, 0)
    return _pallas(
        body, name=name, grid=(T // te,),
        in_specs=[pl.BlockSpec((te, D), row), pl.BlockSpec((te, D), row), pl.BlockSpec((1, D), lambda i: (0, 0)),
                  pl.BlockSpec((te, D), row)],
        out_specs=[pl.BlockSpec((te, D), row), pl.BlockSpec((te, D), row), pl.BlockSpec((1, D), lambda i: (0, 0))],
        out_shape=[jax.ShapeDtypeStruct((T, D), F32), jax.ShapeDtypeStruct((T, D), BF16),
                   jax.ShapeDtypeStruct((1, D), F32)],
        compiler_params=_params(("arbitrary",), blk),
    )(dn, hs, gain, dres)


def _rms_bwd_first(dn, hs, gain, dres, after):
    T, D = hs.shape
    S = T - TT

    def body(dn_ref, h_ref, g_ref, dres_ref, after_ref, gx_ref, gm_ref, dg_ref):
        i = pl.program_id(0)
        dh, dgain = _rms_bwd_math(dn_ref[...], h_ref[...], g_ref[...])
        d = dres_ref[...] + dh

        @pl.when(i == 0)
        def _():
            dg_ref[...] = jnp.zeros_like(dg_ref)
            gm_ref[...] = d[PAD:, :]

        @pl.when(i > 0)
        def _():
            gx_ref[...] = d

        dg_ref[...] += dgain

    blk = _nbytes((TT, D), F32) * 4
    row = lambda i: (i, 0)
    return _pallas(
        body, name="rms_bwd_ffn1", grid=(T // TT,),
        in_specs=[pl.BlockSpec((TT, D), row), pl.BlockSpec((TT, D), row), pl.BlockSpec((1, D), lambda i: (0, 0)),
                  pl.BlockSpec((TT, D), row), TOKEN],
        out_specs=[pl.BlockSpec((TT, D), lambda i: (jnp.maximum(i - 1, 0), 0)),
                   pl.BlockSpec((N_META, D), lambda i: (0, 0)), pl.BlockSpec((1, D), lambda i: (0, 0))],
        out_shape=[jax.ShapeDtypeStruct((S, D), F32), jax.ShapeDtypeStruct((N_META, D), F32),
                   jax.ShapeDtypeStruct((1, D), F32)],
        compiler_params=_params(("arbitrary",), blk),
    )(dn, hs, gain, dres, after)


def _final_loss(hs, gain, tgt):
    T, D = hs.shape

    def body(h_ref, g_ref, t_ref, dhs_ref, dhb_ref, loss_ref, dg_ref):
        i = pl.program_id(0)
        h = h_ref[...]
        g = g_ref[...]
        r = lax.rsqrt(jnp.mean(h * h, axis=-1, keepdims=True) + EPS)
        xh = h * r
        e = jnp.where(i > 0, xh * g - t_ref[...], 0.0)
        tile_loss = jnp.sum(jnp.sum(e * e, axis=1, keepdims=True), axis=0, keepdims=True) * (0.5 / D)
        dout = e * (1.0 / D)
        dgain = jnp.sum(dout * xh, axis=0, keepdims=True)
        dxh = dout * g
        d = r * (dxh - xh * jnp.mean(dxh * xh, axis=-1, keepdims=True))
        dhs_ref[...] = d
        dhb_ref[...] = (FFN_RES_SCALE * d).astype(BF16)

        @pl.when(i == 0)
        def _():
            loss_ref[...] = jnp.zeros_like(loss_ref)
            dg_ref[...] = jnp.zeros_like(dg_ref)

        loss_ref[...] += jnp.broadcast_to(tile_loss, loss_ref.shape)
        dg_ref[...] += dgain

    blk = _nbytes((TT, D), F32) * 3 + _nbytes((TT, D), BF16)
    row = lambda i: (i, 0)
    return _pallas(
        body, name="final_loss", grid=(T // TT,),
        in_specs=[pl.BlockSpec((TT, D), row), pl.BlockSpec((1, D), lambda i: (0, 0)),
                  pl.BlockSpec((TT, D), lambda i: (jnp.maximum(i - 1, 0), 0))],
        out_specs=[pl.BlockSpec((TT, D), row), pl.BlockSpec((TT, D), row),
                   pl.BlockSpec((1, 128), lambda i: (0, 0)), pl.BlockSpec((1, D), lambda i: (0, 0))],
        out_shape=[jax.ShapeDtypeStruct((T, D), F32), jax.ShapeDtypeStruct((T, D), BF16),
                   jax.ShapeDtypeStruct((1, 128), F32), jax.ShapeDtypeStruct((1, D), F32)],
        compiler_params=_params(("arbitrary",), blk),
    )(hs, gain, tgt)


MXU_COLS = 256


def _tm(T):
    return _row_tile(T, 704, 16)


def _col_chunks(n):
    return [(c, min(MXU_COLS, n - c)) for c in range(0, n, MXU_COLS)]


TOKEN = pl.BlockSpec((8, 128), lambda *_: (0, 0))


def _ffn_up(n, wg, wu, after, name):
    T, D = n.shape
    Fs = wg.shape[2]
    tm = _tm(T)

    def body(n_ref, wg_ref, wu_ref, after_ref, g_ref, u_ref, a_ref):
        nn = n_ref[...]
        for c0, cw in _col_chunks(Fs):
            g = jnp.dot(nn, wg_ref[:, c0:c0 + cw], preferred_element_type=F32)
            u = jnp.dot(nn, wu_ref[:, c0:c0 + cw], preferred_element_type=F32)
            g_ref[:, c0:c0 + cw] = g.astype(BF16)
            u_ref[:, c0:c0 + cw] = u.astype(BF16)
            a_ref[:, c0:c0 + cw] = (jax.nn.silu(g) * u).astype(BF16)

    blk = _nbytes((tm, D), BF16) + 2 * _nbytes((D, Fs), BF16) + 3 * _nbytes((tm, Fs), BF16)
    out = pl.BlockSpec((tm, Fs), lambda j, i: (i, j))
    shp = jax.ShapeDtypeStruct((T, N_CHIPS * Fs), BF16)
    return _pallas(
        body, name=name, grid=(N_CHIPS, T // tm),
        in_specs=[pl.BlockSpec((tm, D), lambda j, i: (i, 0)),
                  pl.BlockSpec((None, D, Fs), lambda j, i: (j, 0, 0)),
                  pl.BlockSpec((None, D, Fs), lambda j, i: (j, 0, 0)), TOKEN],
        out_specs=[out, out, out], out_shape=[shp, shp, shp],
        compiler_params=_params(("parallel", "parallel"), blk),
    )(n, wg, wu, after)


def _ffn_down(a, wd, hs, name):
    T, F = a.shape
    D = wd.shape[1]
    tm = _tm(T)
    tn = D // 4

    def body(a_ref, w_ref, h_ref, o_ref):
        o_ref[...] = h_ref[...] + FFN_RES_SCALE * jnp.dot(a_ref[...], w_ref[...], preferred_element_type=F32)

    blk = _nbytes((tm, F), BF16) + _nbytes((F, tn), BF16) + 3 * _nbytes((tm, tn), F32)
    return _pallas(
        body, name=name, grid=(D // tn, T // tm),
        in_specs=[pl.BlockSpec((tm, F), lambda n, i: (i, 0)), pl.BlockSpec((F, tn), lambda n, i: (0, n)),
                  pl.BlockSpec((tm, tn), lambda n, i: (i, n))],
        out_specs=pl.BlockSpec((tm, tn), lambda n, i: (i, n)),
        out_shape=jax.ShapeDtypeStruct((T, D), F32),
        compiler_params=_params(("parallel", "parallel"), blk),
    )(a, wd, hs)


def _mix_in(n, w, b):
    T, D = n.shape
    Ns = w.shape[2]
    tm = _tm(T)

    def body(n_ref, w_ref, b_ref, u_ref):
        u_ref[...] = jnp.dot(n_ref[...], w_ref[...], preferred_element_type=F32) + b_ref[...]

    blk = _nbytes((tm, D), BF16) + _nbytes((D, Ns), BF16) + 2 * _nbytes((tm, Ns), F32)
    return _pallas(
        body, name="mix_in", grid=(N_CHIPS, T // tm),
        in_specs=[pl.BlockSpec((tm, D), lambda j, i: (i, 0)), pl.BlockSpec((None, D, Ns), lambda j, i: (j, 0, 0)),
                  pl.BlockSpec((1, Ns), lambda j, i: (0, j))],
        out_specs=pl.BlockSpec((tm, Ns), lambda j, i: (i, j)),
        out_shape=jax.ShapeDtypeStruct((T, N_CHIPS * Ns), F32),
        compiler_params=_params(("parallel", "parallel"), blk),
    )(n, w, b)


def _mix_out(y, w, hs):
    T, D = y.shape
    tm = _tm(T)

    def body(y_ref, w_ref, h_ref, o_ref):
        o_ref[...] = h_ref[...] + jnp.dot(y_ref[...], w_ref[...], preferred_element_type=F32)

    blk = _nbytes((tm, D), BF16) + _nbytes((D, D), BF16) + 3 * _nbytes((tm, D), F32)
    return _pallas(
        body, name="mix_out", grid=(T // tm,),
        in_specs=[pl.BlockSpec((tm, D), lambda i: (i, 0)), pl.BlockSpec((D, D), lambda i: (0, 0)),
                  pl.BlockSpec((tm, D), lambda i: (i, 0))],
        out_specs=pl.BlockSpec((tm, D), lambda i: (i, 0)),
        out_shape=jax.ShapeDtypeStruct((T, D), F32),
        compiler_params=_params(("parallel",), blk),
    )(y, w, hs)


def _ffn_bwd_act(dfb, wd, g, u, after, name):
    T, D = dfb.shape
    Fs = wd.shape[1]
    tm = _tm(T)

    def body(d_ref, w_ref, g_ref, u_ref, after_ref, dg_ref, du_ref):
        dv = d_ref[...]
        for c0, cw in _col_chunks(Fs):
            da = lax.dot_general(dv, w_ref[c0:c0 + cw, :], NT_DIMS, preferred_element_type=F32)
            gv = g_ref[:, c0:c0 + cw].astype(F32)
            uv = u_ref[:, c0:c0 + cw].astype(F32)
            s = _sigmoid(gv)
            du_ref[:, c0:c0 + cw] = (da * (gv * s)).astype(BF16)
            dg_ref[:, c0:c0 + cw] = (da * uv * _dsilu(gv, s)).astype(BF16)

    blk = _nbytes((tm, D), BF16) + _nbytes((Fs, D), BF16) + 4 * _nbytes((tm, Fs), BF16)
    io = pl.BlockSpec((tm, Fs), lambda j, i: (i, j))
    shp = jax.ShapeDtypeStruct((T, N_CHIPS * Fs), BF16)
    return _pallas(
        body, name=name, grid=(N_CHIPS, T // tm),
        in_specs=[pl.BlockSpec((tm, D), lambda j, i: (i, 0)), pl.BlockSpec((None, Fs, D), lambda j, i: (j, 0, 0)), io, io,
                  TOKEN],
        out_specs=[io, io], out_shape=[shp, shp],
        compiler_params=_params(("parallel", "parallel"), blk),
    )(dfb, wd, g, u, after)


def _nt_panel(lhs_list, w_list, after, name):
    T = lhs_list[0].shape[0]
    nsh, Dout, Ks = w_list[0].shape
    npair = len(lhs_list)
    tm = _row_tile(T, 1408, 16)
    tn = Dout // 2

    def body(*refs):
        l_refs, w_refs, o_ref = refs[:npair], refs[npair:2 * npair], refs[2 * npair + 1]
        j = pl.program_id(2)
        acc = None
        for p in range(npair):
            part = lax.dot_general(l_refs[p][...], w_refs[p][...], NT_DIMS, preferred_element_type=F32)
            acc = part if acc is None else acc + part

        @pl.when(j == 0)
        def _():
            o_ref[...] = acc

        @pl.when(j > 0)
        def _():
            o_ref[...] += acc

    blk = npair * (_nbytes((tm, Ks), BF16) + _nbytes((tn, Ks), BF16)) + 2 * _nbytes((tm, tn), F32)
    return _pallas(
        body, name=name, grid=(Dout // tn, T // tm, nsh),
        in_specs=[pl.BlockSpec((tm, Ks), lambda n, i, j: (i, j))] * npair
                 + [pl.BlockSpec((None, tn, Ks), lambda n, i, j: (j, n, 0))] * npair + [TOKEN],
        out_specs=pl.BlockSpec((tm, tn), lambda n, i, j: (i, n)),
        out_shape=jax.ShapeDtypeStruct((T, Dout), F32),
        compiler_params=_params(("parallel", "parallel", "arbitrary"), blk),
    )(*lhs_list, *w_list, after)


def _tn_call(name, grid, lhs, lhs_spec, rhs_list, rhs_specs, out_shapes, out_specs, blk, after=None):
    nr = len(rhs_list)
    extra = [] if after is None else [after]

    def body(*refs):
        l_ref, r_refs, o_refs = refs[0], refs[1:1 + nr], refs[len(refs) - nr:]
        k = pl.program_id(len(grid) - 1)
        lv = l_ref[...]
        for q in range(nr):
            part = lax.dot_general(lv, r_refs[q][...], TN_DIMS, preferred_element_type=F32)
            part = part.reshape(o_refs[q].shape)

            @pl.when(k == 0)
            def _(o=o_refs[q], part=part):
                o[...] = part

            @pl.when(k > 0)
            def _(o=o_refs[q], part=part):
                o[...] += part

    return _pallas(
        body, name=name, grid=grid, in_specs=[lhs_spec] + rhs_specs + [TOKEN] * len(extra), out_specs=out_specs,
        out_shape=out_shapes, compiler_params=_params(("parallel",) * (len(grid) - 1) + ("arbitrary",), blk),
    )(lhs, *rhs_list, *extra)


def _tk(T):
    return _row_tile(T, 1408, 128)


def _wgrad_cols(n, rhs_list, name, after=None):
    T, D = n.shape
    Ns = rhs_list[0].shape[1] // N_CHIPS
    tk = _tk(T)
    nr = len(rhs_list)
    blk = _nbytes((tk, D // 2), BF16) + nr * (_nbytes((tk, Ns), BF16) + 2 * _nbytes((D // 2, Ns), F32))
    return _tn_call(
        name, (N_CHIPS, 2, T // tk), n, pl.BlockSpec((tk, D // 2), lambda j, m, k: (k, m)),
        rhs_list, [pl.BlockSpec((tk, Ns), lambda j, m, k: (k, j))] * nr,
        [jax.ShapeDtypeStruct((N_CHIPS, 2, D // 2, Ns), F32)] * nr,
        [pl.BlockSpec((None, None, D // 2, Ns), lambda j, m, k: (j, m, 0, 0))] * nr, blk, after)


def _wgrad_down(a, dfb, name):
    T, F = a.shape
    D = dfb.shape[1]
    Fs = F // N_CHIPS
    tk = _tk(T)
    tn = D // 2
    blk = _nbytes((tk, Fs), BF16) + _nbytes((tk, tn), BF16) + 2 * _nbytes((Fs, tn), F32)
    return _tn_call(
        name, (N_CHIPS, D // tn, T // tk), a, pl.BlockSpec((tk, Fs), lambda j, n, k: (k, j)),
        [dfb], [pl.BlockSpec((tk, tn), lambda j, n, k: (k, n))],
        [jax.ShapeDtypeStruct((N_CHIPS, 2, Fs // 2, D), F32)],
        [pl.BlockSpec((None, 2, Fs // 2, tn), lambda j, n, k: (j, 0, 0, n))], blk)[0]


def _wgrad_out(y, dmb):
    T, D = y.shape
    tk = _tk(T)
    tn = D // 2
    rows = D // (2 * N_CHIPS)
    blk = _nbytes((tk, D // 2), BF16) + _nbytes((tk, tn), BF16) + 2 * _nbytes((D // 2, tn), F32)
    return _tn_call(
        "wgrad_w_out", (2, D // tn, T // tk), y, pl.BlockSpec((tk, D // 2), lambda m, n, k: (k, m)),
        [dmb], [pl.BlockSpec((tk, tn), lambda m, n, k: (k, n))],
        [jax.ShapeDtypeStruct((N_CHIPS, 2, rows, D), F32)],
        [pl.BlockSpec((2, 2, rows, tn), lambda m, n, k: (m, 0, 0, n))], blk)[0]


def _row_masks(i, last):
    rows = i * TT + lax.broadcasted_iota(jnp.int32, (TT, 1), 0)
    prows = i * TT - HALO + lax.broadcasted_iota(jnp.int32, (HALO, 1), 0)
    return rows >= PAD, (prows >= PAD) & (i > 0), i < last


def _conv_inputs(u, up, mask_c, mask_p, zbuf, pbuf, C1):
    b, c, v, a, g = (u[:, k * C1:(k + 1) * C1] for k in range(5))
    cp, vp, ap, gp = (up[:, k * C1:(k + 1) * C1] for k in range(1, 5))
    sg = _sigmoid(g)
    pbuf[0:HALO, :] = jnp.where(mask_p, cp * vp, 0.0)
    pbuf[HALO:, :] = jnp.where(mask_c, c * v, 0.0)
    zbuf[0:HALO, :] = jnp.where(mask_p, ap * _sigmoid(gp), 0.0)
    zbuf[HALO:, :] = jnp.where(mask_c, a * sg, 0.0)
    return b, c, v, a, sg


def _causal_conv(w_ref, buf):
    K = w_ref.shape[0]
    acc = None
    for k in range(K):
        lo = HALO - (K - 1) + k
        term = w_ref[k:k + 1, :] * buf[lo:lo + TT, :]
        acc = term if acc is None else acc + term
    return acc


def _anticausal_conv(w_ref, buf):
    K = w_ref.shape[0]
    acc = None
    for k in range(K):
        lo = K - 1 - k
        term = w_ref[k:k + 1, :] * buf[lo:lo + TT, :]
        acc = term if acc is None else acc + term
    return acc


def _conv_weight_sums(dw_ref, dy, buf):
    K = dw_ref.shape[0]
    for k in range(K):
        lo = HALO - (K - 1) + k
        dw_ref[k:k + 1, :] += jnp.sum(dy * buf[lo:lo + TT, :], axis=0, keepdims=True)


def _layernorm_stats(z1):
    mu = jnp.mean(z1, axis=-1, keepdims=True)
    zc = z1 - mu
    rs = lax.rsqrt(jnp.mean(zc * zc, axis=-1, keepdims=True) + EPS)
    return zc * rs, rs


def _mixer_specs(T, DIN, C1, ksc, kcf):
    cur = pl.BlockSpec((TT, DIN), lambda i: (i, 0))
    prev = pl.BlockSpec((HALO, DIN), lambda i: (jnp.maximum(i * (TT // HALO) - 1, 0), 0))
    full = lambda r: pl.BlockSpec((r, C1), lambda i: (0, 0))
    return cur, prev, [full(ksc), full(kcf), full(1), full(1), full(1)]


def _mix_conv_fwd(u, wsc, wcf, bcf, lg, lb):
    T, DIN = u.shape
    C1 = DIN // 5
    last = T // TT - 1

    def body(u_ref, up_ref, wsc_ref, wcf_ref, bcf_ref, lg_ref, lb_ref, y_ref, zbuf, pbuf):
        i = pl.program_id(0)
        mask_c, mask_p, _ = _row_masks(i, last)
        b, _, _, _, _ = _conv_inputs(u_ref[...], up_ref[...], mask_c, mask_p, zbuf, pbuf, C1)
        cs = _causal_conv(wsc_ref, pbuf)
        z1 = _causal_conv(wcf_ref, zbuf) + bcf_ref[...]
        zh, _ = _layernorm_stats(z1)
        ln = zh * lg_ref[...] + lb_ref[...]
        y_ref[:, 0:C1] = jnp.where(mask_c, b * cs, 0.0).astype(BF16)
        y_ref[:, C1:] = jnp.where(mask_c, jax.nn.silu(ln), 0.0).astype(BF16)

    cur, prev, small = _mixer_specs(T, DIN, C1, wsc.shape[0], wcf.shape[0])
    blk = _nbytes((TT + HALO, DIN), F32) + _nbytes((TT, 2 * C1), BF16) + 12 * _nbytes((TT + HALO, C1), F32)
    return _pallas(
        body, name="mix_conv_fwd", grid=(T // TT,),
        in_specs=[cur, prev] + small,
        out_specs=pl.BlockSpec((TT, 2 * C1), lambda i: (i, 0)),
        out_shape=jax.ShapeDtypeStruct((T, 2 * C1), BF16),
        scratch_shapes=[pltpu.VMEM((TT + HALO, C1), F32), pltpu.VMEM((TT + HALO, C1), F32)],
        compiler_params=_params(("arbitrary",), blk),
    )(u, u, wsc, wcf, bcf, lg, lb)


def _mix_conv_bwd1(u, dy, wsc, wcf, bcf, lg, lb):
    T, DIN = u.shape
    C1 = DIN // 5
    last = T // TT - 1

    def body(u_ref, up_ref, dy_ref, wsc_ref, wcf_ref, bcf_ref, lg_ref, lb_ref,
             dz1_ref, dcs_ref, db_ref, dlg_ref, dlb_ref, dbcf_ref, zbuf, pbuf):
        i = pl.program_id(0)
        mask_c, mask_p, _ = _row_masks(i, last)
        b, _, _, _, _ = _conv_inputs(u_ref[...], up_ref[...], mask_c, mask_p, zbuf, pbuf, C1)
        cs = _causal_conv(wsc_ref, pbuf)
        z1 = _causal_conv(wcf_ref, zbuf) + bcf_ref[...]
        zh, rs = _layernorm_stats(z1)
        ln = zh * lg_ref[...] + lb_ref[...]
        dy = dy_ref[...]
        dysc = jnp.where(mask_c, dy[:, 0:C1], 0.0)
        dycf = jnp.where(mask_c, dy[:, C1:], 0.0)
        db_ref[...] = (dysc * cs).astype(BF16)
        dcs_ref[...] = dysc * b
        dl = dycf * _dsilu(ln, _sigmoid(ln))
        dzh = dl * lg_ref[...]
        dz1 = rs * (dzh - jnp.mean(dzh, axis=-1, keepdims=True) - zh * jnp.mean(dzh * zh, axis=-1, keepdims=True))
        dz1_ref[...] = dz1

        @pl.when(i == 0)
        def _():
            dlg_ref[...] = jnp.zeros_like(dlg_ref)
            dlb_ref[...] = jnp.zeros_like(dlb_ref)
            dbcf_ref[...] = jnp.zeros_like(dbcf_ref)

        dlg_ref[...] += jnp.sum(dl * zh, axis=0, keepdims=True)
        dlb_ref[...] += jnp.sum(dl, axis=0, keepdims=True)
        dbcf_ref[...] += jnp.sum(dz1, axis=0, keepdims=True)

    cur, prev, small = _mixer_specs(T, DIN, C1, wsc.shape[0], wcf.shape[0])
    tile = lambda: pl.BlockSpec((TT, C1), lambda i: (i, 0))
    vec = lambda: pl.BlockSpec((1, C1), lambda i: (0, 0))
    blk = _nbytes((TT + HALO, DIN), F32) + 4 * _nbytes((TT, C1), F32) + 16 * _nbytes((TT + HALO, C1), F32)
    return _pallas(
        body, name="mix_conv_bwd1", grid=(T // TT,),
        in_specs=[cur, prev, pl.BlockSpec((TT, 2 * C1), lambda i: (i, 0))] + small,
        out_specs=[tile(), tile(), tile(), vec(), vec(), vec()],
        out_shape=[jax.ShapeDtypeStruct((T, C1), F32), jax.ShapeDtypeStruct((T, C1), F32),
                   jax.ShapeDtypeStruct((T, C1), BF16)] + [jax.ShapeDtypeStruct((1, C1), F32)] * 3,
        scratch_shapes=[pltpu.VMEM((TT + HALO, C1), F32), pltpu.VMEM((TT + HALO, C1), F32)],
        compiler_params=_params(("arbitrary",), blk),
    )(u, u, dy, wsc, wcf, bcf, lg, lb)


def _mix_conv_bwd2(u, dz1, dcs, db, wsc, wcf):
    T, DIN = u.shape
    C1 = DIN // 5
    last = T // TT - 1
    ksc, kcf = wsc.shape[0], wcf.shape[0]

    def body(u_ref, up_ref, dz_ref, dzn_ref, dc_ref, dcn_ref, db_ref, wsc_ref, wcf_ref,
             du_ref, dbin_ref, dwsc_ref, dwcf_ref, zbuf, pbuf, dzbuf, dcbuf):
        i = pl.program_id(0)
        mask_c, mask_p, has_next = _row_masks(i, last)
        _, c, v, a, sg = _conv_inputs(u_ref[...], up_ref[...], mask_c, mask_p, zbuf, pbuf, C1)
        dz1 = dz_ref[...]
        dcs = dc_ref[...]
        dzbuf[0:TT, :] = dz1
        dzbuf[TT:, :] = jnp.where(has_next, dzn_ref[...], 0.0)
        dcbuf[0:TT, :] = dcs
        dcbuf[TT:, :] = jnp.where(has_next, dcn_ref[...], 0.0)

        @pl.when(i == 0)
        def _():
            dbin_ref[...] = jnp.zeros_like(dbin_ref)
            dwsc_ref[...] = jnp.zeros_like(dwsc_ref)
            dwcf_ref[...] = jnp.zeros_like(dwcf_ref)

        _conv_weight_sums(dwcf_ref, dz1, zbuf)
        _conv_weight_sums(dwsc_ref, dcs, pbuf)
        dz0 = jnp.where(mask_c, _anticausal_conv(wcf_ref, dzbuf), 0.0)
        dp = jnp.where(mask_c, _anticausal_conv(wsc_ref, dcbuf), 0.0)
        parts = (db_ref[...].astype(F32), dp * v, dp * c, dz0 * sg, dz0 * a * sg * (1.0 - sg))
        for k, part in enumerate(parts):
            du_ref[:, k * C1:(k + 1) * C1] = part.astype(BF16)
            dbin_ref[:, k * C1:(k + 1) * C1] += jnp.sum(part, axis=0, keepdims=True)

    cur, prev, small = _mixer_specs(T, DIN, C1, ksc, kcf)
    tile = lambda: pl.BlockSpec((TT, C1), lambda i: (i, 0))
    nxt = lambda: pl.BlockSpec((HALO, C1), lambda i: (jnp.minimum((i + 1) * (TT // HALO), T // HALO - 1), 0))
    blk = (_nbytes((TT + HALO, DIN), F32) + _nbytes((TT, DIN), BF16) + 5 * _nbytes((TT, C1), F32)
           + 16 * _nbytes((TT + HALO, C1), F32))
    buf = lambda: pltpu.VMEM((TT + HALO, C1), F32)
    return _pallas(
        body, name="mix_conv_bwd2", grid=(T // TT,),
        in_specs=[cur, prev, tile(), nxt(), tile(), nxt(), tile(), small[0], small[1]],
        out_specs=[pl.BlockSpec((TT, DIN), lambda i: (i, 0)), pl.BlockSpec((1, DIN), lambda i: (0, 0)),
                   pl.BlockSpec((ksc, C1), lambda i: (0, 0)), pl.BlockSpec((kcf, C1), lambda i: (0, 0))],
        out_shape=[jax.ShapeDtypeStruct((T, DIN), BF16), jax.ShapeDtypeStruct((1, DIN), F32),
                   jax.ShapeDtypeStruct((ksc, C1), F32), jax.ShapeDtypeStruct((kcf, C1), F32)],
        scratch_shapes=[buf(), buf(), buf(), buf()],
        compiler_params=_params(("arbitrary",), blk),
    )(u, u, dz1, dz1, dcs, dcs, db, wsc, wcf)


def _place():
    x, y, c = lax.axis_index("x"), lax.axis_index("y"), lax.axis_index("c")
    chips = [(1 - x, y), (x, 1 - y), (1 - x, 1 - y)]
    return x, y, c, chips


ANY = pl.BlockSpec(memory_space=pl.ANY)


def _cast_own_block(place, w, name):
    R, C = w.shape
    tr = _row_tile(R // 2, 256, 16)
    nblk = R // 2 // tr

    def body(place_ref, w_ref, o_ref):
        o_ref[...] = w_ref[...].astype(BF16)

    return _pallas(
        body, name=name,
        grid_spec=pltpu.PrefetchScalarGridSpec(
            num_scalar_prefetch=1, grid=(2, nblk),
            in_specs=[pl.BlockSpec((tr, C), lambda h, i, p: (h * nblk + i, 0))],
            out_specs=pl.BlockSpec((None, None, tr, C), lambda h, i, p: (p[0], h, i, 0))),
        out_shape=jax.ShapeDtypeStruct((N_CHIPS, 2, R // 2, C), BF16),
        compiler_params=_params(("parallel", "parallel"), _nbytes((tr, C), F32) + _nbytes((tr, C), BF16)),
    )(place, w)


def _gather_weights(bufs, after):
    nw = len(bufs)

    def body(*refs):
        o_refs = refs[nw + 1:2 * nw + 1]
        send, recv = refs[2 * nw + 1:]
        x, y, c, chips = _place()
        s = 2 * x + y
        sib = (x, y, 1 - c)

        def remote(w, k, blk, half, to):
            ref = o_refs[w].at[blk, half]
            return pltpu.make_async_remote_copy(src_ref=ref, dst_ref=ref, send_sem=send.at[6 * w + k],
                                                recv_sem=recv.at[6 * w + k], device_id=to, device_id_type=MESH)

        sends = []
        for w in range(nw):
            for r, (tx, ty) in enumerate(chips):
                cp = remote(w, r, s, c, (tx, ty, c))
                cp.start()
                sends.append(cp)
        for w in range(nw):
            for r, (tx, ty) in enumerate(chips):
                sr = 2 * tx + ty
                remote(w, r, sr, c, (tx, ty, c)).wait_recv()
                cp = remote(w, 3 + r, sr, c, sib)
                cp.start()
                sends.append(cp)
        for w in range(nw):
            for r, (tx, ty) in enumerate(chips):
                remote(w, 3 + r, 2 * tx + ty, 1 - c, sib).wait_recv()
        for cp in sends:
            cp.wait_send()

    return _pallas(
        body, name="gather_weights", in_specs=[ANY] * (nw + 1), out_specs=[ANY] * nw,
        out_shape=[jax.ShapeDtypeStruct(b.shape, b.dtype) for b in bufs],
        input_output_aliases={w: w for w in range(nw)},
        scratch_shapes=[pltpu.SemaphoreType.DMA((6 * nw,)), pltpu.SemaphoreType.DMA((6 * nw,))],
    )(*bufs, after)


HBM = pl.BlockSpec(memory_space=pltpu.HBM)
SEM = pl.BlockSpec(memory_space=pltpu.SEMAPHORE)
EFFECT = pltpu.SideEffectType.DATAFLOW_SIDE_EFFECTING


def _gather_copies(refs, send, recv):
    x, y, c, chips = _place()
    s = 2 * x + y
    return [pltpu.make_async_remote_copy(src_ref=ref.at[s, c], dst_ref=ref.at[s, c], send_sem=send.at[3 * w + r],
                                         recv_sem=recv.at[3 * w + r], device_id=(tx, ty, c), device_id_type=MESH)
            for w, ref in enumerate(refs) for r, (tx, ty) in enumerate(chips)]


def _scatter_copies(refs, send, recv):
    x, y, c, chips = _place()
    nw = len(refs) // 2
    return [pltpu.make_async_remote_copy(src_ref=refs[w].at[2 * tx + ty], dst_ref=refs[nw + w].at[r],
                                         send_sem=send.at[3 * w + r], recv_sem=recv.at[3 * w + r],
                                         device_id=(tx, ty, c), device_id_type=MESH)
            for w in range(nw) for r, (tx, ty) in enumerate(chips)]


def _pair_copies(refs, send, recv):
    x, y, c, _ = _place()
    nw = len(refs) // 2
    return [pltpu.make_async_remote_copy(src_ref=refs[w].at[j, 1 - c], dst_ref=refs[nw + w].at[j],
                                         send_sem=send.at[N_CHIPS * w + j], recv_sem=recv.at[N_CHIPS * w + j],
                                         device_id=(x, y, 1 - c), device_id_type=MESH)
            for w in range(nw) for j in range(N_CHIPS)]


def _start_copies(bufs, after, ncopies, make_copies, name):
    n = len(bufs)

    def body(*refs):
        in_refs, send, recv, token = refs[:n], refs[n + 1], refs[n + 2], refs[2 * n + 3]
        for cp in make_copies(in_refs, send, recv):
            cp.start()
        token[...] = jnp.zeros_like(token)

    outs = _pallas(
        body, name=name, in_specs=[HBM] * n + [ANY],
        out_specs=[SEM, SEM] + [HBM] * n + [pl.BlockSpec(memory_space=pltpu.VMEM)],
        out_shape=[pltpu.SemaphoreType.DMA((ncopies,)), pltpu.SemaphoreType.DMA((ncopies,))]
                  + [pltpu.HBM(b.shape, b.dtype) for b in bufs] + [jax.ShapeDtypeStruct((8, 128), F32)],
        input_output_aliases={k: 2 + k for k in range(n)},
        compiler_params=pltpu.CompilerParams(has_side_effects=EFFECT),
    )(*[pltpu.with_memory_space_constraint(b, pltpu.HBM) for b in bufs], after)
    return outs[0], outs[1], list(outs[2:2 + n]), outs[2 + n]


def _wait_copies(send, recv, bufs, after, make_copies, name):
    n = len(bufs)

    def body(*refs):
        in_refs, send_ref, recv_ref = refs[:n], refs[n], refs[n + 1]
        for cp in make_copies(in_refs, send_ref, recv_ref):
            cp.wait_send()
            cp.wait_recv()

    outs = _pallas(
        body, name=name, in_specs=[HBM] * n + [SEM, SEM, ANY], out_specs=[HBM] * n,
        out_shape=[pltpu.HBM(b.shape, b.dtype) for b in bufs],
        input_output_aliases={k: k for k in range(n)},
        compiler_params=pltpu.CompilerParams(has_side_effects=EFFECT),
    )(*bufs, send, recv, after)
    return list(outs)


def _forward_halves(bufs, name):
    nw = len(bufs)

    def body(*refs):
        o_refs = refs[nw:2 * nw]
        send, recv = refs[2 * nw:]
        x, y, c, chips = _place()
        sib = (x, y, 1 - c)
        copies = []
        for w in range(nw):
            for r, (tx, ty) in enumerate(chips):
                ref = o_refs[w].at[2 * tx + ty, c]
                cp = pltpu.make_async_remote_copy(src_ref=ref, dst_ref=ref, send_sem=send.at[3 * w + r],
                                                  recv_sem=recv.at[3 * w + r], device_id=sib, device_id_type=MESH)
                cp.start()
                copies.append(cp)
        for w in range(nw):
            for r, (tx, ty) in enumerate(chips):
                ref = o_refs[w].at[2 * tx + ty, 1 - c]
                pltpu.make_async_remote_copy(src_ref=ref, dst_ref=ref, send_sem=send.at[3 * w + r],
                                             recv_sem=recv.at[3 * w + r], device_id=sib, device_id_type=MESH).wait_recv()
        for cp in copies:
            cp.wait_send()

    return _pallas(
        body, name=name, in_specs=[ANY] * nw, out_specs=[ANY] * nw,
        out_shape=[jax.ShapeDtypeStruct(b.shape, b.dtype) for b in bufs],
        input_output_aliases={w: w for w in range(nw)},
        scratch_shapes=[pltpu.SemaphoreType.DMA((3 * nw,)), pltpu.SemaphoreType.DMA((3 * nw,))],
    )(*bufs)


def _half_exchange(hs, name):
    nw = len(hs)

    def body(*refs):
        o_refs = refs[nw:2 * nw]
        send, recv = refs[2 * nw:]
        x, y, c, _ = _place()
        sib = (x, y, 1 - c)
        copies = []
        for w in range(nw):
            cp = pltpu.make_async_remote_copy(src_ref=o_refs[w].at[c], dst_ref=o_refs[w].at[c], send_sem=send.at[w],
                                              recv_sem=recv.at[w], device_id=sib, device_id_type=MESH)
            cp.start()
            copies.append(cp)
        for w, cp in enumerate(copies):
            cp.wait_send()
            pltpu.make_async_remote_copy(src_ref=o_refs[w].at[c], dst_ref=o_refs[w].at[1 - c], send_sem=send.at[w],
                                         recv_sem=recv.at[w], device_id=sib, device_id_type=MESH).wait_recv()

    return _pallas(
        body, name=name, in_specs=[ANY] * nw, out_specs=[ANY] * nw,
        out_shape=[jax.ShapeDtypeStruct(h.shape, F32) for h in hs],
        input_output_aliases={w: w for w in range(nw)},
        scratch_shapes=[pltpu.SemaphoreType.DMA((nw,)), pltpu.SemaphoreType.DMA((nw,))],
    )(*hs)


def _share_small(v, reduce, name):
    R, C = v.shape

    def body(v_ref, o_ref, *scratch):
        if reduce:
            all_ref, send, recv, lsem = scratch
        else:
            all_ref = o_ref
            send, recv, lsem = scratch
        x, y, c, _ = _place()
        me = 4 * x + 2 * y + c
        loc = pltpu.make_async_copy(v_ref, all_ref.at[me], lsem)
        loc.start()
        copies = []
        for k in range(1, N_DEV):
            kx, ky, kc = (k >> 2) & 1, (k >> 1) & 1, k & 1
            peer = (x ^ kx, y ^ ky, c ^ kc)
            cp = pltpu.make_async_remote_copy(src_ref=v_ref, dst_ref=all_ref.at[me], send_sem=send.at[k - 1],
                                              recv_sem=recv.at[k - 1], device_id=peer, device_id_type=MESH)
            cp.start()
            copies.append(cp)
        for k in range(1, N_DEV):
            kx, ky, kc = (k >> 2) & 1, (k >> 1) & 1, k & 1
            src = 4 * (x ^ kx) + 2 * (y ^ ky) + (c ^ kc)
            pltpu.make_async_remote_copy(src_ref=v_ref, dst_ref=all_ref.at[src], send_sem=send.at[k - 1],
                                         recv_sem=recv.at[k - 1], device_id=(x, y, c), device_id_type=MESH).wait_recv()
        for cp in copies:
            cp.wait_send()
        loc.wait()
        if reduce:
            total = all_ref[0]
            for d in range(1, N_DEV):
                total = total + all_ref[d]
            o_ref[...] = total

    vm = pl.BlockSpec(memory_space=pltpu.VMEM)
    sems = [pltpu.SemaphoreType.DMA((N_DEV - 1,)), pltpu.SemaphoreType.DMA((N_DEV - 1,)), pltpu.SemaphoreType.DMA]
    if reduce:
        out_shape = jax.ShapeDtypeStruct((R, C), F32)
        scratch = [pltpu.VMEM((N_DEV, R, C), F32)] + sems
    else:
        out_shape = jax.ShapeDtypeStruct((N_DEV, R, C), F32)
        scratch = sems
    return _pallas(
        body, name=name, in_specs=[vm], out_specs=vm, out_shape=out_shape, scratch_shapes=scratch,
        compiler_params=pltpu.CompilerParams(vmem_limit_bytes=int(min(4 * N_DEV * R * C * 4 + 2 ** 24, 2 ** 25 + 2 ** 24))),
    )(v)


def _pair_sum(place, g, rb, name):
    _, _, Rh, C = g.shape
    tr = _row_tile(Rh, 256, 16)

    def body(place_ref, g_ref, r_ref, q_ref):
        q_ref[...] = (g_ref[...] + r_ref[...]).astype(BF16)

    blk = 2 * _nbytes((tr, C), F32) + _nbytes((tr, C), BF16)
    return _pallas(
        body, name=name,
        grid_spec=pltpu.PrefetchScalarGridSpec(
            num_scalar_prefetch=1, grid=(N_CHIPS, Rh // tr),
            in_specs=[pl.BlockSpec((None, None, tr, C), lambda j, i, p: (j, p[1], i, 0)),
                      pl.BlockSpec((None, tr, C), lambda j, i, p: (j, i, 0))],
            out_specs=pl.BlockSpec((None, tr, C), lambda j, i, p: (j, i, 0))),
        out_shape=jax.ShapeDtypeStruct((N_CHIPS, Rh, C), BF16),
        compiler_params=_params(("parallel", "parallel"), blk),
    )(place, g, rb)


def _chip_sum(place, g, rb, rc, name):
    _, _, Rh, C = g.shape
    tr = _row_tile(Rh, 256, 16)

    def body(place_ref, g_ref, r_ref, rc_ref, o_ref):
        total = g_ref[...] + r_ref[...]
        for r in range(3):
            total = total + rc_ref[r].astype(F32)
        o_ref[...] = total

    blk = 3 * _nbytes((tr, C), F32) + 3 * _nbytes((tr, C), BF16)
    return _pallas(
        body, name=name,
        grid_spec=pltpu.PrefetchScalarGridSpec(
            num_scalar_prefetch=1, grid=(Rh // tr,),
            in_specs=[pl.BlockSpec((None, None, tr, C), lambda i, p: (p[0], p[1], i, 0)),
                      pl.BlockSpec((None, tr, C), lambda i, p: (p[0], i, 0)),
                      pl.BlockSpec((3, tr, C), lambda i, p: (0, i, 0))],
            out_specs=pl.BlockSpec((None, tr, C), lambda i, p: (p[1], i, 0))),
        out_shape=jax.ShapeDtypeStruct((2, Rh, C), F32),
        compiler_params=_params(("parallel",), blk),
    )(place, g, rb, rc)


def _adamw_math(w, g, m, v):
    m = ADAM_B1 * m + (1.0 - ADAM_B1) * g
    v = ADAM_B2 * v + (1.0 - ADAM_B2) * jnp.square(g)
    m_hat = m / (1.0 - ADAM_B1 ** ADAM_STEP)
    v_hat = v / (1.0 - ADAM_B2 ** ADAM_STEP)
    delta = -ADAM_LR * (m_hat / (jnp.sqrt(v_hat) + ADAM_EPS) + ADAM_WD * w)
    return delta, m, v


def _adamw(w, g, m, v, name):
    R, C = w.shape
    tr = _row_tile(R, 256)

    def body(w_ref, g_ref, m_ref, v_ref, d_ref, nm_ref, nv_ref):
        d, nm, nv = _adamw_math(w_ref[...], g_ref[...], m_ref[...], v_ref[...])
        d_ref[...] = d
        nm_ref[...] = nm
        nv_ref[...] = nv

    spec = pl.BlockSpec((tr, C), lambda i: (i, 0))
    shp = jax.ShapeDtypeStruct((R, C), F32)
    return _pallas(
        body, name=name, grid=(R // tr,), in_specs=[spec] * 4, out_specs=[spec] * 3, out_shape=[shp] * 3,
        compiler_params=_params(("parallel",), 7 * _nbytes((tr, C), F32)),
    )(w, g, m, v)


def _adamw_small(ws, gs, ms, vs):
    n = len(ws)

    def body(*refs):
        for k in range(n):
            w_ref, g_ref, m_ref, v_ref = (refs[q * n + k] for q in range(4))
            d, nm, nv = _adamw_math(w_ref[...], g_ref[...], m_ref[...], v_ref[...])
            refs[4 * n + k][...] = d
            refs[5 * n + k][...] = nm
            refs[6 * n + k][...] = nv

    vm = pl.BlockSpec(memory_space=pltpu.VMEM)
    shapes = [jax.ShapeDtypeStruct(w.shape, F32) for w in ws]
    outs = _pallas(
        body, name="adamw_small", in_specs=[vm] * (4 * n), out_specs=[vm] * (3 * n), out_shape=shapes * 3,
    )(*ws, *gs, *ms, *vs)
    return outs[:n], outs[n:2 * n], outs[2 * n:]


def _pad_rows(a, rows):
    return jnp.pad(a, ((0, rows - a.shape[0]), (0, 0)))


def kernel(x, meta_tokens, ffn1_norm, ffn1_w_gate, ffn1_w_up, ffn1_w_down, mix_norm, w_in, b_in, conv_sc_w, conv_cf_w, conv_cf_b, ln_cf_g, ln_cf_b, w_out, ffn2_norm, ffn2_w_gate, ffn2_w_up, ffn2_w_down, final_norm, loss_target, m_meta_tokens, m_ffn1_norm, m_ffn1_w_gate, m_ffn1_w_up, m_ffn1_w_down, m_mix_norm, m_w_in, m_b_in, m_conv_sc_w, m_conv_cf_w, m_conv_cf_b, m_ln_cf_g, m_ln_cf_b, m_w_out, m_ffn2_norm, m_ffn2_w_gate, m_ffn2_w_up, m_ffn2_w_down, m_final_norm, v_meta_tokens, v_ffn1_norm, v_ffn1_w_gate, v_ffn1_w_up, v_ffn1_w_down, v_mix_norm, v_w_in, v_b_in, v_conv_sc_w, v_conv_cf_w, v_conv_cf_b, v_ln_cf_g, v_ln_cf_b, v_w_out, v_ffn2_norm, v_ffn2_w_gate, v_ffn2_w_up, v_ffn2_w_down, v_final_norm):
    xi, yi, ci = lax.axis_index("x"), lax.axis_index("y"), lax.axis_index("c")
    chip = 2 * xi + yi
    place = jnp.stack([chip, ci]).astype(jnp.int32)

    x2 = x[0]
    tgt = loss_target[0]
    S, D = x2.shape
    C1 = D // 2
    cs = conv_sc_w.shape[2]
    ksc, kcf = conv_sc_w.shape[1], conv_cf_w.shape[1]
    ms = meta_tokens.shape[1]

    rows_small = N_META + 8 + 32
    assert ksc <= 8 and kcf <= 32 and cs <= ms
    pack = jnp.concatenate([
        meta_tokens,
        jnp.pad(conv_sc_w[0], ((0, 8 - ksc), (0, ms - cs))),
        jnp.pad(conv_cf_w[0], ((0, 32 - kcf), (0, ms - cs)))], axis=0)
    everyone = _share_small(pack, False, "share_params")[0::2]
    meta_full = jnp.transpose(everyone[:, :N_META, :], (1, 0, 2)).reshape(N_META, D)
    wsc_full = jnp.transpose(everyone[:, N_META:N_META + ksc, :cs], (1, 0, 2)).reshape(ksc, C1)
    wcf_full = jnp.transpose(everyone[:, N_META + 8:N_META + 8 + kcf, :cs], (1, 0, 2)).reshape(kcf, C1)

    big = {"ffn1_w_gate": ffn1_w_gate, "ffn1_w_up": ffn1_w_up, "ffn1_w_down": ffn1_w_down, "w_in": w_in, "w_out": w_out,
           "ffn2_w_gate": ffn2_w_gate, "ffn2_w_up": ffn2_w_up, "ffn2_w_down": ffn2_w_down}
    big_m = {"ffn1_w_gate": m_ffn1_w_gate, "ffn1_w_up": m_ffn1_w_up, "ffn1_w_down": m_ffn1_w_down, "w_in": m_w_in,
             "w_out": m_w_out, "ffn2_w_gate": m_ffn2_w_gate, "ffn2_w_up": m_ffn2_w_up, "ffn2_w_down": m_ffn2_w_down}
    big_v = {"ffn1_w_gate": v_ffn1_w_gate, "ffn1_w_up": v_ffn1_w_up, "ffn1_w_down": v_ffn1_w_down, "w_in": v_w_in,
             "w_out": v_w_out, "ffn2_w_gate": v_ffn2_w_gate, "ffn2_w_up": v_ffn2_w_up, "ffn2_w_down": v_ffn2_w_down}
    buf = {nm: _cast_own_block(place, w[0], "cast_" + nm) for nm, w in big.items()}
    whole_weight = lambda g: g.reshape(N_CHIPS, 2 * g.shape[2], g.shape[3])
    group_mix, group_ffn2 = ["w_in", "w_out"], ["ffn2_w_gate", "ffn2_w_up", "ffn2_w_down"]

    corner = lambda a: a.reshape(-1, a.shape[-1])[:8, :128]
    wg1, wu1, wd1 = (whole_weight(g) for g in _gather_weights(
        [buf[nm] for nm in ["ffn1_w_gate", "ffn1_w_up", "ffn1_w_down"]], corner(everyone)))
    send_mix, recv_mix, thru_mix, token_mix = _start_copies(
        [buf[nm] for nm in group_mix], corner(wd1), 3 * len(group_mix), _gather_copies, "gather_start_mix")
    send_ffn2, recv_ffn2, thru_ffn2, token_ffn2 = _start_copies(
        [buf[nm] for nm in group_ffn2], token_mix, 3 * len(group_ffn2), _gather_copies, "gather_start_ffn2")
    F = N_CHIPS * wd1.shape[1]

    hs0, n1 = _embed_rms(x2, meta_full, ffn1_norm)
    g1, u1, a1 = _ffn_up(n1, wg1, wu1, token_ffn2, "ffn1_up")
    hs1 = _ffn_down(a1, wd1.reshape(F, D), hs0, "ffn1_down")
    arrived = _wait_copies(send_mix, recv_mix, thru_mix, corner(hs1), _gather_copies, "gather_wait_mix")
    win, wout = (whole_weight(g) for g in _forward_halves(arrived, "gather_forward_mix"))
    n2 = _rms(hs1, mix_norm, "rms_mix")
    u = _mix_in(n2, win, b_in)
    y = _mix_conv_fwd(u, wsc_full, wcf_full, conv_cf_b, ln_cf_g, ln_cf_b)
    hs2 = _mix_out(y, wout.reshape(D, D), hs1)
    arrived = _wait_copies(send_ffn2, recv_ffn2, thru_ffn2, corner(hs2), _gather_copies, "gather_wait_ffn2")
    wg2, wu2, wd2 = (whole_weight(g) for g in _forward_halves(arrived, "gather_forward_ffn2"))
    n3 = _rms(hs2, ffn2_norm, "rms_ffn2")
    g2, u2, a2 = _ffn_up(n3, wg2, wu2, token_ffn2, "ffn2_up")
    hs3 = _ffn_down(a2, wd2.reshape(F, D), hs2, "ffn2_down")

    def pair_start(group, after, tag):
        gs = [g for _, g in group]
        lands = [lax.empty((N_CHIPS,) + g.shape[2:], F32) for g in gs]
        send, recv, thru, token = _start_copies(gs + lands, after, N_CHIPS * len(gs), _pair_copies,
                                                "pair_start_" + tag)
        return (group, send, recv, thru, tag), token

    def scatter_start(state, after):
        group, send, recv, thru, tag = state
        thru = _wait_copies(send, recv, thru, corner(after), _pair_copies, "pair_wait_" + tag)
        gs, sib = thru[:len(group)], thru[len(group):]
        sums = [_pair_sum(place, g, rb, "pair_sum_" + nm) for (nm, _), g, rb in zip(group, gs, sib)]
        lands = [lax.empty((3,) + q.shape[1:], BF16) for q in sums]
        send, recv, thru, token = _start_copies(sums + lands, corner(sums[-1]), 3 * len(gs), _scatter_copies,
                                                "scatter_start_" + tag)
        return ([(nm, g) for (nm, _), g in zip(group, gs)], sib, send, recv, thru, tag), token

    def reduce_finish(state, after):
        group, sib, send, recv, thru, tag = state
        lands = _wait_copies(send, recv, thru, corner(after), _scatter_copies, "scatter_wait_" + tag)[len(group):]
        mine = [_chip_sum(place, g, rb, rc, "chip_sum_" + nm) for (nm, g), rb, rc in zip(group, sib, lands)]
        whole = _half_exchange(mine, "half_exchange_" + tag)
        out = {}
        for (nm, _), g in zip(group, whole):
            w = big[nm]
            g3d = g.reshape(w.shape)
            d, new_m, new_v = _adamw(w[0], g3d[0], big_m[nm][0], big_v[nm][0], "adamw_" + nm)
            out[nm] = (g3d, d[None], new_m[None], new_v[None])
        return out

    dhs3, df2, loss_row, d_final = _final_loss(hs3, final_norm.reshape(1, D), tgt)

    dg2, du2 = _ffn_bwd_act(df2, wd2, g2, u2, token_ffn2, "ffn2_bwd_act")
    gw_d2 = _wgrad_down(a2, df2, "wgrad_ffn2_down")
    gw_g2 = _wgrad_cols(n3, [dg2], "wgrad_ffn2_gate")[0]
    gw_u2 = _wgrad_cols(n3, [du2], "wgrad_ffn2_up")[0]
    pair_ffn2, token = pair_start([("ffn2_w_gate", gw_g2), ("ffn2_w_up", gw_u2), ("ffn2_w_down", gw_d2)],
                                  corner(gw_u2), "ffn2")
    dn3 = _nt_panel([dg2, du2], [wg2, wu2], token, "ffn2_bwd_in")
    red_ffn2, token = scatter_start(pair_ffn2, dn3)
    dhs2, dm, d_ffn2 = _rms_bwd(dn3, hs2, ffn2_norm, dhs3, 1.0, "rms_bwd_ffn2")

    dy = _nt_panel([dm], [wout.reshape(1, D, D)], token, "mix_bwd_out")
    gw_out = _wgrad_out(y, dm)
    dz1, dcs, db, d_lg, d_lb, d_bcf = _mix_conv_bwd1(u, dy, wsc_full, wcf_full, conv_cf_b, ln_cf_g, ln_cf_b)
    du, d_bin, d_wsc, d_wcf = _mix_conv_bwd2(u, dz1, dcs, db, wsc_full, wcf_full)
    gw_in = _wgrad_cols(n2, [du], "wgrad_w_in")[0]
    pair_mix, token = pair_start([("w_in", gw_in), ("w_out", gw_out)], corner(gw_in), "mix")
    dn2 = _nt_panel([du], [win], token, "mix_bwd_in")
    red_mix, token = scatter_start(pair_mix, dn2)
    dhs1, df1, d_mix = _rms_bwd(dn2, hs1, mix_norm, dhs2, FFN_RES_SCALE, "rms_bwd_mix")

    dg1, du1 = _ffn_bwd_act(df1, wd1, g1, u1, token, "ffn1_bwd_act")
    gw_d1 = _wgrad_down(a1, df1, "wgrad_ffn1_down")
    gw_g1 = _wgrad_cols(n1, [dg1], "wgrad_ffn1_gate")[0]
    pair_ffn1a, token = pair_start([("ffn1_w_down", gw_d1), ("ffn1_w_gate", gw_g1)], corner(gw_g1), "ffn1a")
    gw_u1 = _wgrad_cols(n1, [du1], "wgrad_ffn1_up", token)[0]
    red_ffn1a, token = scatter_start(pair_ffn1a, gw_u1)
    pair_ffn1b, token = pair_start([("ffn1_w_up", gw_u1)], token, "ffn1b")
    dn1 = _nt_panel([dg1, du1], [wg1, wu1], token, "ffn1_bwd_in")
    red_ffn1b, token = scatter_start(pair_ffn1b, dn1)
    grad_x, d_meta, d_ffn1 = _rms_bwd_first(dn1, hs0, ffn1_norm, dhs1, token)

    big_out = reduce_finish(red_ffn2, grad_x)
    big_out.update(reduce_finish(red_mix, big_out["ffn2_w_down"][1]))
    big_out.update(reduce_finish(red_ffn1a, big_out["w_out"][1]))
    big_out.update(reduce_finish(red_ffn1b, big_out["ffn1_w_gate"][1]))

    W = C1
    rows = lambda a: a.reshape(-1, W)
    parts = [rows(d_ffn1), rows(d_mix), rows(d_ffn2), rows(d_final), rows(d_bin), d_bcf, d_lg, d_lb,
             d_wsc, d_wcf, rows(d_meta), jnp.broadcast_to(loss_row[:, :1], (1, W))]
    sizes = [p.shape[0] for p in parts]
    total_rows = sum(sizes)
    packed = _pad_rows(jnp.concatenate(parts, axis=0), -(-total_rows // 8) * 8)
    summed = _share_small(packed, True, "sum_small")
    offs = [0]
    for n in sizes:
        offs.append(offs[-1] + n)
    piece = lambda k: summed[offs[k]:offs[k + 1]]
    loss = piece(11)[0, 0]
    g_ffn1, g_mix, g_ffn2 = (piece(k).reshape(1, D) for k in range(3))
    g_final = piece(3).reshape(1, D)
    g_bin = piece(4).reshape(1, -1)
    g_bcf, g_lg, g_lb = piece(5), piece(6), piece(7)
    g_wsc = lax.dynamic_slice_in_dim(piece(8), chip * cs, cs, axis=1)
    g_wcf = lax.dynamic_slice_in_dim(piece(9), chip * cs, cs, axis=1)
    g_meta = lax.dynamic_slice_in_dim(piece(10).reshape(N_META, D), chip * ms, ms, axis=1)

    small_names = ["meta_tokens", "ffn1_norm", "mix_norm", "b_in", "conv_sc_w", "conv_cf_w", "conv_cf_b", "ln_cf_g",
                   "ln_cf_b", "ffn2_norm", "final_norm"]
    small_w = [meta_tokens, ffn1_norm, mix_norm, b_in, conv_sc_w[0], conv_cf_w[0], conv_cf_b, ln_cf_g, ln_cf_b,
               ffn2_norm, final_norm.reshape(1, D)]
    small_g = [g_meta, g_ffn1, g_mix, g_bin, g_wsc, g_wcf, g_bcf, g_lg, g_lb, g_ffn2, g_final]
    small_m = [m_meta_tokens, m_ffn1_norm, m_mix_norm, m_b_in, m_conv_sc_w[0], m_conv_cf_w[0], m_conv_cf_b, m_ln_cf_g,
               m_ln_cf_b, m_ffn2_norm, m_final_norm.reshape(1, D)]
    small_v = [v_meta_tokens, v_ffn1_norm, v_mix_norm, v_b_in, v_conv_sc_w[0], v_conv_cf_w[0], v_conv_cf_b, v_ln_cf_g,
               v_ln_cf_b, v_ffn2_norm, v_final_norm.reshape(1, D)]
    s_d, s_m, s_v = _adamw_small(small_w, small_g, small_m, small_v)
    shapes = {"conv_sc_w": conv_sc_w.shape, "conv_cf_w": conv_cf_w.shape, "final_norm": final_norm.shape}
    small_out = {}
    for nm, g, d, m, v in zip(small_names, small_g, s_d, s_m, s_v):
        shp = shapes.get(nm, g.shape)
        small_out[nm] = tuple(t.reshape(shp) for t in (g, d, m, v))

    order = ["meta_tokens", "ffn1_norm", "ffn1_w_gate", "ffn1_w_up", "ffn1_w_down", "mix_norm", "w_in", "b_in",
             "conv_sc_w", "conv_cf_w", "conv_cf_b", "ln_cf_g", "ln_cf_b", "w_out", "ffn2_norm", "ffn2_w_gate",
             "ffn2_w_up", "ffn2_w_down", "final_norm"]
    res = {**big_out, **small_out}
    outs = [loss, grad_x[None]]
    for q in range(4):
        outs.extend(res[nm][q] for nm in order)
    return tuple(outs)
```

```python
import functools

import jax
import jax.numpy as jnp
from jax import lax
from jax.experimental import pallas as pl
from jax.experimental.pallas import tpu as pltpu

F32 = jnp.float32
BF16 = jnp.bfloat16
MESH = pl.DeviceIdType.MESH

N_META = 16
TT = 128
PAD = TT - N_META
HALO = 32
EPS = 1e-6
FFN_RES_SCALE = 0.5
N_CHIPS = 4
N_DEV = 8

ADAM_LR = 0.001
ADAM_B1 = 0.9
ADAM_B2 = 0.999
ADAM_EPS = 1e-08
ADAM_WD = 0.01
ADAM_STEP = 10

V7X_VMEM_BYTES = 64 * 2 ** 20
NT_DIMS = (((1,), (1,)), ((), ()))
TN_DIMS = (((0,), (0,)), ((), ()))


def _params(semantics, block_bytes):
    limit = min(2 * block_bytes + 16 * 2 ** 20, V7X_VMEM_BYTES - 6 * 2 ** 20)
    return pltpu.CompilerParams(dimension_semantics=semantics, vmem_limit_bytes=int(limit))


def _pallas(body, out_shape, **kw):
    if "grid" not in kw and "grid_spec" not in kw:
        return pl.pallas_call(body, out_shape=out_shape, **kw)
    big = lambda shape, dtype: jnp.issubdtype(dtype, jnp.floating) and len(shape) >= 2
    pin_out = lambda s: pltpu.HBM(s.shape, s.dtype) if big(s.shape, s.dtype) else s
    single = not isinstance(out_shape, (list, tuple))
    shapes = pin_out(out_shape) if single else [pin_out(s) for s in out_shape]
    call = pl.pallas_call(body, out_shape=shapes, **kw)
    pin = lambda a: pltpu.with_memory_space_constraint(a, pltpu.HBM) if big(a.shape, a.dtype) else a
    return lambda *operands: call(*[pin(a) for a in operands])


def _nbytes(shape, dtype):
    n = 1
    for d in shape:
        if d is not None:
            n *= d
    return n * jnp.dtype(dtype).itemsize


def _row_tile(rows, target, mult=8):
    best = None
    for t in range(mult, min(rows, target) + 1, mult):
        if rows % t == 0:
            best = t
    assert best is not None, (rows, target, mult)
    return best


def _sigmoid(v):
    return jax.nn.sigmoid(v)


def _dsilu(v, s):
    return s * (1.0 + v * (1.0 - s))


def _embed_rms(x2, meta, gain):
    S, D = x2.shape
    T = S + TT

    def body(x_ref, meta_ref, g_ref, hs_ref, n_ref):
        i = pl.program_id(0)

        @pl.when(i == 0)
        def _():
            hs_ref[...] = jnp.zeros_like(hs_ref)
            hs_ref[PAD:, :] = meta_ref[...]

        @pl.when(i > 0)
        def _():
            hs_ref[...] = x_ref[...]

        h = hs_ref[...]
        r = lax.rsqrt(jnp.mean(h * h, axis=-1, keepdims=True) + EPS)
        n_ref[...] = ((h * r) * g_ref[...]).astype(BF16)

    blk = _nbytes((TT, D), F32) * 2 + _nbytes((TT, D), BF16)
    return _pallas(
        body, name="embed_rms", grid=(T // TT,),
        in_specs=[pl.BlockSpec((TT, D), lambda i: (jnp.maximum(i - 1, 0), 0)),
                  pl.BlockSpec((N_META, D), lambda i: (0, 0)),
                  pl.BlockSpec((1, D), lambda i: (0, 0))],
        out_specs=[pl.BlockSpec((TT, D), lambda i: (i, 0)), pl.BlockSpec((TT, D), lambda i: (i, 0))],
        out_shape=[jax.ShapeDtypeStruct((T, D), F32), jax.ShapeDtypeStruct((T, D), BF16)],
        compiler_params=_params(("parallel",), blk),
    )(x2, meta, gain)


def _rms(hs, gain, name):
    T, D = hs.shape
    te = _row_tile(T, 384)

    def body(h_ref, g_ref, n_ref):
        h = h_ref[...]
        r = lax.rsqrt(jnp.mean(h * h, axis=-1, keepdims=True) + EPS)
        n_ref[...] = ((h * r) * g_ref[...]).astype(BF16)

    blk = _nbytes((te, D), F32) + _nbytes((te, D), BF16)
    return _pallas(
        body, name=name, grid=(T // te,),
        in_specs=[pl.BlockSpec((te, D), lambda i: (i, 0)), pl.BlockSpec((1, D), lambda i: (0, 0))],
        out_specs=pl.BlockSpec((te, D), lambda i: (i, 0)),
        out_shape=jax.ShapeDtypeStruct((T, D), BF16),
        compiler_params=_params(("parallel",), blk),
    )(hs, gain)


def _rms_bwd_math(dn, h, g):
    r = lax.rsqrt(jnp.mean(h * h, axis=-1, keepdims=True) + EPS)
    xh = h * r
    dgain = jnp.sum(dn * xh, axis=0, keepdims=True)
    dxh = dn * g
    dh = r * (dxh - xh * jnp.mean(dxh * xh, axis=-1, keepdims=True))
    return dh, dgain


def _rms_bwd(dn, hs, gain, dres, scale, name):
    T, D = hs.shape
    te = _row_tile(T, 384)

    def body(dn_ref, h_ref, g_ref, dres_ref, dhs_ref, dhb_ref, dg_ref):
        dh, dgain = _rms_bwd_math(dn_ref[...], h_ref[...], g_ref[...])
        d = dres_ref[...] + dh
        dhs_ref[...] = d
        dhb_ref[...] = (scale * d).astype(BF16)

        @pl.when(pl.program_id(0) == 0)
        def _():
            dg_ref[...] = jnp.zeros_like(dg_ref)

        dg_ref[...] += dgain

    blk = _nbytes((te, D), F32) * 4 + _nbytes((te, D), BF16)
    row = lambda i: (i, 0)
    return _pallas(
        body, name=name, grid=(T // te,),
        in_specs=[pl.BlockSpec((te, D), row), pl.BlockSpec((te, D), row), pl.BlockSpec((1, D), lambda i: (0, 0)),
                  pl.BlockSpec((te, D), row)],
        out_specs=[pl.BlockSpec((te, D), row), pl.BlockSpec((te, D), row), pl.BlockSpec((1, D), lambda i: (0, 0))],
        out_shape=[jax.ShapeDtypeStruct((T, D), F32), jax.ShapeDtypeStruct((T, D), BF16),
                   jax.ShapeDtypeStruct((1, D), F32)],
        compiler_params=_params(("arbitrary",), blk),
    )(dn, hs, gain, dres)


def _rms_bwd_first(dn, hs, gain, dres, after):
    T, D = hs.shape
    S = T - TT

    def body(dn_ref, h_ref, g_ref, dres_ref, after_ref, gx_ref, gm_ref, dg_ref):
        i = pl.program_id(0)
        dh, dgain = _rms_bwd_math(dn_ref[...], h_ref[...], g_ref[...])
        d = dres_ref[...] + dh

        @pl.when(i == 0)
        def _():
            dg_ref[...] = jnp.zeros_like(dg_ref)
            gm_ref[...] = d[PAD:, :]

        @pl.when(i > 0)
        def _():
            gx_ref[...] = d

        dg_ref[...] += dgain

    blk = _nbytes((TT, D), F32) * 4
    row = lambda i: (i, 0)
    return _pallas(
        body, name="rms_bwd_ffn1", grid=(T // TT,),
        in_specs=[pl.BlockSpec((TT, D), row), pl.BlockSpec((TT, D), row), pl.BlockSpec((1, D), lambda i: (0, 0)),
                  pl.BlockSpec((TT, D), row), TOKEN],
        out_specs=[pl.BlockSpec((TT, D), lambda i: (jnp.maximum(i - 1, 0), 0)),
                   pl.BlockSpec((N_META, D), lambda i: (0, 0)), pl.BlockSpec((1, D), lambda i: (0, 0))],
        out_shape=[jax.ShapeDtypeStruct((S, D), F32), jax.ShapeDtypeStruct((N_META, D), F32),
                   jax.ShapeDtypeStruct((1, D), F32)],
        compiler_params=_params(("arbitrary",), blk),
    )(dn, hs, gain, dres, after)


def _final_loss(hs, gain, tgt):
    T, D = hs.shape

    def body(h_ref, g_ref, t_ref, dhs_ref, dhb_ref, loss_ref, dg_ref):
        i = pl.program_id(0)
        h = h_ref[...]
        g = g_ref[...]
        r = lax.rsqrt(jnp.mean(h * h, axis=-1, keepdims=True) + EPS)
        xh = h * r
        e = jnp.where(i > 0, xh * g - t_ref[...], 0.0)
        tile_loss = jnp.sum(jnp.sum(e * e, axis=1, keepdims=True), axis=0, keepdims=True) * (0.5 / D)
        dout = e * (1.0 / D)
        dgain = jnp.sum(dout * xh, axis=0, keepdims=True)
        dxh = dout * g
        d = r * (dxh - xh * jnp.mean(dxh * xh, axis=-1, keepdims=True))
        dhs_ref[...] = d
        dhb_ref[...] = (FFN_RES_SCALE * d).astype(BF16)

        @pl.when(i == 0)
        def _():
            loss_ref[...] = jnp.zeros_like(loss_ref)
            dg_ref[...] = jnp.zeros_like(dg_ref)

        loss_ref[...] += jnp.broadcast_to(tile_loss, loss_ref.shape)
        dg_ref[...] += dgain

    blk = _nbytes((TT, D), F32) * 3 + _nbytes((TT, D), BF16)
    row = lambda i: (i, 0)
    return _pallas(
        body, name="final_loss", grid=(T // TT,),
        in_specs=[pl.BlockSpec((TT, D), row), pl.BlockSpec((1, D), lambda i: (0, 0)),
                  pl.BlockSpec((TT, D), lambda i: (jnp.maximum(i - 1, 0), 0))],
        out_specs=[pl.BlockSpec((TT, D), row), pl.BlockSpec((TT, D), row),
                   pl.BlockSpec((1, 128), lambda i: (0, 0)), pl.BlockSpec((1, D), lambda i: (0, 0))],
        out_shape=[jax.ShapeDtypeStruct((T, D), F32), jax.ShapeDtypeStruct((T, D), BF16),
                   jax.ShapeDtypeStruct((1, 128), F32), jax.ShapeDtypeStruct((1, D), F32)],
        compiler_params=_params(("arbitrary",), blk),
    )(hs, gain, tgt)


MXU_COLS = 256


def _tm(T):
    return _row_tile(T, 704, 16)


def _col_chunks(n):
    return [(c, min(MXU_COLS, n - c)) for c in range(0, n, MXU_COLS)]


TOKEN = pl.BlockSpec((8, 128), lambda *_: (0, 0))


def _ffn_up(n, wg, wu, after, name):
    T, D = n.shape
    Fs = wg.shape[2]
    tm = _tm(T)

    def body(n_ref, wg_ref, wu_ref, after_ref, g_ref, u_ref, a_ref):
        nn = n_ref[...]
        for c0, cw in _col_chunks(Fs):
            g = jnp.dot(nn, wg_ref[:, c0:c0 + cw], preferred_element_type=F32)
            u = jnp.dot(nn, wu_ref[:, c0:c0 + cw], preferred_element_type=F32)
            g_ref[:, c0:c0 + cw] = g.astype(BF16)
            u_ref[:, c0:c0 + cw] = u.astype(BF16)
            a_ref[:, c0:c0 + cw] = (jax.nn.silu(g) * u).astype(BF16)

    blk = _nbytes((tm, D), BF16) + 2 * _nbytes((D, Fs), BF16) + 3 * _nbytes((tm, Fs), BF16)
    out = pl.BlockSpec((tm, Fs), lambda j, i: (i, j))
    shp = jax.ShapeDtypeStruct((T, N_CHIPS * Fs), BF16)
    return _pallas(
        body, name=name, grid=(N_CHIPS, T // tm),
        in_specs=[pl.BlockSpec((tm, D), lambda j, i: (i, 0)),
                  pl.BlockSpec((None, D, Fs), lambda j, i: (j, 0, 0)),
                  pl.BlockSpec((None, D, Fs), lambda j, i: (j, 0, 0)), TOKEN],
        out_specs=[out, out, out], out_shape=[shp, shp, shp],
        compiler_params=_params(("parallel", "parallel"), blk),
    )(n, wg, wu, after)


def _ffn_down(a, wd, hs, name):
    T, F = a.shape
    D = wd.shape[1]
    tm = _tm(T)
    tn = D // 4

    def body(a_ref, w_ref, h_ref, o_ref):
        o_ref[...] = h_ref[...] + FFN_RES_SCALE * jnp.dot(a_ref[...], w_ref[...], preferred_element_type=F32)

    blk = _nbytes((tm, F), BF16) + _nbytes((F, tn), BF16) + 3 * _nbytes((tm, tn), F32)
    return _pallas(
        body, name=name, grid=(D // tn, T // tm),
        in_specs=[pl.BlockSpec((tm, F), lambda n, i: (i, 0)), pl.BlockSpec((F, tn), lambda n, i: (0, n)),
                  pl.BlockSpec((tm, tn), lambda n, i: (i, n))],
        out_specs=pl.BlockSpec((tm, tn), lambda n, i: (i, n)),
        out_shape=jax.ShapeDtypeStruct((T, D), F32),
        compiler_params=_params(("parallel", "parallel"), blk),
    )(a, wd, hs)


def _mix_in(n, w, b):
    T, D = n.shape
    Ns = w.shape[2]
    tm = _tm(T)

    def body(n_ref, w_ref, b_ref, u_ref):
        u_ref[...] = jnp.dot(n_ref[...], w_ref[...], preferred_element_type=F32) + b_ref[...]

    blk = _nbytes((tm, D), BF16) + _nbytes((D, Ns), BF16) + 2 * _nbytes((tm, Ns), F32)
    return _pallas(
        body, name="mix_in", grid=(N_CHIPS, T // tm),
        in_specs=[pl.BlockSpec((tm, D), lambda j, i: (i, 0)), pl.BlockSpec((None, D, Ns), lambda j, i: (j, 0, 0)),
                  pl.BlockSpec((1, Ns), lambda j, i: (0, j))],
        out_specs=pl.BlockSpec((tm, Ns), lambda j, i: (i, j)),
        out_shape=jax.ShapeDtypeStruct((T, N_CHIPS * Ns), F32),
        compiler_params=_params(("parallel", "parallel"), blk),
    )(n, w, b)


def _mix_out(y, w, hs):
    T, D = y.shape
    tm = _tm(T)

    def body(y_ref, w_ref, h_ref, o_ref):
        o_ref[...] = h_ref[...] + jnp.dot(y_ref[...], w_ref[...], preferred_element_type=F32)

    blk = _nbytes((tm, D), BF16) + _nbytes((D, D), BF16) + 3 * _nbytes((tm, D), F32)
    return _pallas(
        body, name="mix_out", grid=(T // tm,),
        in_specs=[pl.BlockSpec((tm, D), lambda i: (i, 0)), pl.BlockSpec((D, D), lambda i: (0, 0)),
                  pl.BlockSpec((tm, D), lambda i: (i, 0))],
        out_specs=pl.BlockSpec((tm, D), lambda i: (i, 0)),
        out_shape=jax.ShapeDtypeStruct((T, D), F32),
        compiler_params=_params(("parallel",), blk),
    )(y, w, hs)


def _ffn_bwd_act(dfb, wd, g, u, after, name):
    T, D = dfb.shape
    Fs = wd.shape[1]
    tm = _tm(T)

    def body(d_ref, w_ref, g_ref, u_ref, after_ref, dg_ref, du_ref):
        dv = d_ref[...]
        for c0, cw in _col_chunks(Fs):
            da = lax.dot_general(dv, w_ref[c0:c0 + cw, :], NT_DIMS, preferred_element_type=F32)
            gv = g_ref[:, c0:c0 + cw].astype(F32)
            uv = u_ref[:, c0:c0 + cw].astype(F32)
            s = _sigmoid(gv)
            du_ref[:, c0:c0 + cw] = (da * (gv * s)).astype(BF16)
            dg_ref[:, c0:c0 + cw] = (da * uv * _dsilu(gv, s)).astype(BF16)

    blk = _nbytes((tm, D), BF16) + _nbytes((Fs, D), BF16) + 4 * _nbytes((tm, Fs), BF16)
    io = pl.BlockSpec((tm, Fs), lambda j, i: (i, j))
    shp = jax.ShapeDtypeStruct((T, N_CHIPS * Fs), BF16)
    return _pallas(
        body, name=name, grid=(N_CHIPS, T // tm),
        in_specs=[pl.BlockSpec((tm, D), lambda j, i: (i, 0)), pl.BlockSpec((None, Fs, D), lambda j, i: (j, 0, 0)), io, io,
                  TOKEN],
        out_specs=[io, io], out_shape=[shp, shp],
        compiler_params=_params(("parallel", "parallel"), blk),
    )(dfb, wd, g, u, after)


def _nt_panel(lhs_list, w_list, after, name):
    T = lhs_list[0].shape[0]
    nsh, Dout, Ks = w_list[0].shape
    npair = len(lhs_list)
    tm = _row_tile(T, 1408, 16)
    tn = Dout // 2

    def body(*refs):
        l_refs, w_refs, o_ref = refs[:npair], refs[npair:2 * npair], refs[2 * npair + 1]
        j = pl.program_id(2)
        acc = None
        for p in range(npair):
            part = lax.dot_general(l_refs[p][...], w_refs[p][...], NT_DIMS, preferred_element_type=F32)
            acc = part if acc is None else acc + part

        @pl.when(j == 0)
        def _():
            o_ref[...] = acc

        @pl.when(j > 0)
        def _():
            o_ref[...] += acc

    blk = npair * (_nbytes((tm, Ks), BF16) + _nbytes((tn, Ks), BF16)) + 2 * _nbytes((tm, tn), F32)
    return _pallas(
        body, name=name, grid=(Dout // tn, T // tm, nsh),
        in_specs=[pl.BlockSpec((tm, Ks), lambda n, i, j: (i, j))] * npair
                 + [pl.BlockSpec((None, tn, Ks), lambda n, i, j: (j, n, 0))] * npair + [TOKEN],
        out_specs=pl.BlockSpec((tm, tn), lambda n, i, j: (i, n)),
        out_shape=jax.ShapeDtypeStruct((T, Dout), F32),
        compiler_params=_params(("parallel", "parallel", "arbitrary"), blk),
    )(*lhs_list, *w_list, after)


def _tn_call(name, grid, lhs, lhs_spec, rhs_list, rhs_specs, out_shapes, out_specs, blk, after=None):
    nr = len(rhs_list)
    extra = [] if after is None else [after]

    def body(*refs):
        l_ref, r_refs, o_refs = refs[0], refs[1:1 + nr], refs[len(refs) - nr:]
        k = pl.program_id(len(grid) - 1)
        lv = l_ref[...]
        for q in range(nr):
            part = lax.dot_general(lv, r_refs[q][...], TN_DIMS, preferred_element_type=F32)
            part = part.reshape(o_refs[q].shape)

            @pl.when(k == 0)
            def _(o=o_refs[q], part=part):
                o[...] = part

            @pl.when(k > 0)
            def _(o=o_refs[q], part=part):
                o[...] += part

    return _pallas(
        body, name=name, grid=grid, in_specs=[lhs_spec] + rhs_specs + [TOKEN] * len(extra), out_specs=out_specs,
        out_shape=out_shapes, compiler_params=_params(("parallel",) * (len(grid) - 1) + ("arbitrary",), blk),
    )(lhs, *rhs_list, *extra)


def _tk(T):
    return _row_tile(T, 1408, 128)


def _wgrad_cols(n, rhs_list, name, after=None):
    T, D = n.shape
    Ns = rhs_list[0].shape[1] // N_CHIPS
    tk = _tk(T)
    nr = len(rhs_list)
    blk = _nbytes((tk, D // 2), BF16) + nr * (_nbytes((tk, Ns), BF16) + 2 * _nbytes((D // 2, Ns), F32))
    return _tn_call(
        name, (N_CHIPS, 2, T // tk), n, pl.BlockSpec((tk, D // 2), lambda j, m, k: (k, m)),
        rhs_list, [pl.BlockSpec((tk, Ns), lambda j, m, k: (k, j))] * nr,
        [jax.ShapeDtypeStruct((N_CHIPS, 2, D // 2, Ns), F32)] * nr,
        [pl.BlockSpec((None, None, D // 2, Ns), lambda j, m, k: (j, m, 0, 0))] * nr, blk, after)


def _wgrad_down(a, dfb, name):
    T, F = a.shape
    D = dfb.shape[1]
    Fs = F // N_CHIPS
    tk = _tk(T)
    tn = D // 2
    blk = _nbytes((tk, Fs), BF16) + _nbytes((tk, tn), BF16) + 2 * _nbytes((Fs, tn), F32)
    return _tn_call(
        name, (N_CHIPS, D // tn, T // tk), a, pl.BlockSpec((tk, Fs), lambda j, n, k: (k, j)),
        [dfb], [pl.BlockSpec((tk, tn), lambda j, n, k: (k, n))],
        [jax.ShapeDtypeStruct((N_CHIPS, 2, Fs // 2, D), F32)],
        [pl.BlockSpec((None, 2, Fs // 2, tn), lambda j, n, k: (j, 0, 0, n))], blk)[0]


def _wgrad_out(y, dmb):
    T, D = y.shape
    tk = _tk(T)
    tn = D // 2
    rows = D // (2 * N_CHIPS)
    blk = _nbytes((tk, D // 2), BF16) + _nbytes((tk, tn), BF16) + 2 * _nbytes((D // 2, tn), F32)
    return _tn_call(
        "wgrad_w_out", (2, D // tn, T // tk), y, pl.BlockSpec((tk, D // 2), lambda m, n, k: (k, m)),
        [dmb], [pl.BlockSpec((tk, tn), lambda m, n, k: (k, n))],
        [jax.ShapeDtypeStruct((N_CHIPS, 2, rows, D), F32)],
        [pl.BlockSpec((2, 2, rows, tn), lambda m, n, k: (m, 0, 0, n))], blk)[0]


def _row_masks(i, last):
    rows = i * TT + lax.broadcasted_iota(jnp.int32, (TT, 1), 0)
    prows = i * TT - HALO + lax.broadcasted_iota(jnp.int32, (HALO, 1), 0)
    return rows >= PAD, (prows >= PAD) & (i > 0), i < last


def _conv_inputs(u, up, mask_c, mask_p, zbuf, pbuf, C1):
    b, c, v, a, g = (u[:, k * C1:(k + 1) * C1] for k in range(5))
    cp, vp, ap, gp = (up[:, k * C1:(k + 1) * C1] for k in range(1, 5))
    sg = _sigmoid(g)
    pbuf[0:HALO, :] = jnp.where(mask_p, cp * vp, 0.0)
    pbuf[HALO:, :] = jnp.where(mask_c, c * v, 0.0)
    zbuf[0:HALO, :] = jnp.where(mask_p, ap * _sigmoid(gp), 0.0)
    zbuf[HALO:, :] = jnp.where(mask_c, a * sg, 0.0)
    return b, c, v, a, sg


SUBLANES = 8
SHIFT_ROWS = TT + HALO - SUBLANES


def _shifted_scratch(C1):
    return pltpu.VMEM((SUBLANES - 1, SHIFT_ROWS, C1), F32)


def _fill_shifted(buf, sh):
    for r in range(1, SUBLANES):
        sh[r - 1] = buf[r:r + SHIFT_ROWS, :]


LANES = 128


def _window(buf, sh, lo, c0):
    if sh is None or lo % SUBLANES == 0:
        return buf[lo:lo + TT, c0:c0 + LANES]
    q, r = divmod(lo, SUBLANES)
    return sh[r - 1, q * SUBLANES:q * SUBLANES + TT, c0:c0 + LANES]


def _tap_sum(w_ref, buf, sh, starts):
    chunks = []
    for c0 in range(0, buf.shape[1], LANES):
        acc = None
        for k, lo in enumerate(starts):
            term = w_ref[k:k + 1, c0:c0 + LANES] * _window(buf, sh, lo, c0)
            acc = term if acc is None else acc + term
        chunks.append(acc)
    return jnp.concatenate(chunks, axis=1)


def _causal_conv(w_ref, buf, sh=None):
    K = w_ref.shape[0]
    return _tap_sum(w_ref, buf, sh, [HALO - (K - 1) + k for k in range(K)])


def _anticausal_conv(w_ref, buf, sh=None):
    K = w_ref.shape[0]
    return _tap_sum(w_ref, buf, sh, [K - 1 - k for k in range(K)])


def _conv_weight_sums(dw_ref, dy, buf, sh=None):
    K = dw_ref.shape[0]
    for c0 in range(0, buf.shape[1], LANES):
        dyc = dy[:, c0:c0 + LANES]
        for k in range(K):
            prod = dyc * _window(buf, sh, HALO - (K - 1) + k, c0)
            dw_ref[k:k + 1, c0:c0 + LANES] += jnp.sum(prod, axis=0, keepdims=True)


def _layernorm_stats(z1):
    mu = jnp.mean(z1, axis=-1, keepdims=True)
    zc = z1 - mu
    rs = lax.rsqrt(jnp.mean(zc * zc, axis=-1, keepdims=True) + EPS)
    return zc * rs, rs


def _mixer_specs(T, DIN, C1, ksc, kcf):
    cur = pl.BlockSpec((TT, DIN), lambda i: (i, 0))
    prev = pl.BlockSpec((HALO, DIN), lambda i: (jnp.maximum(i * (TT // HALO) - 1, 0), 0))
    full = lambda r: pl.BlockSpec((r, C1), lambda i: (0, 0))
    return cur, prev, [full(ksc), full(kcf), full(1), full(1), full(1)]


def _mix_conv_fwd(u, wsc, wcf, bcf, lg, lb):
    T, DIN = u.shape
    C1 = DIN // 5
    last = T // TT - 1

    def body(u_ref, up_ref, wsc_ref, wcf_ref, bcf_ref, lg_ref, lb_ref, y_ref, zbuf, pbuf, zsh):
        i = pl.program_id(0)
        mask_c, mask_p, _ = _row_masks(i, last)
        b, _, _, _, _ = _conv_inputs(u_ref[...], up_ref[...], mask_c, mask_p, zbuf, pbuf, C1)
        _fill_shifted(zbuf, zsh)
        cs = _causal_conv(wsc_ref, pbuf)
        z1 = _causal_conv(wcf_ref, zbuf, zsh) + bcf_ref[...]
        zh, _ = _layernorm_stats(z1)
        ln = zh * lg_ref[...] + lb_ref[...]
        y_ref[:, 0:C1] = jnp.where(mask_c, b * cs, 0.0).astype(BF16)
        y_ref[:, C1:] = jnp.where(mask_c, jax.nn.silu(ln), 0.0).astype(BF16)

    cur, prev, small = _mixer_specs(T, DIN, C1, wsc.shape[0], wcf.shape[0])
    blk = _nbytes((TT + HALO, DIN), F32) + _nbytes((TT, 2 * C1), BF16) + 12 * _nbytes((TT + HALO, C1), F32)
    return _pallas(
        body, name="mix_conv_fwd", grid=(T // TT,),
        in_specs=[cur, prev] + small,
        out_specs=pl.BlockSpec((TT, 2 * C1), lambda i: (i, 0)),
        out_shape=jax.ShapeDtypeStruct((T, 2 * C1), BF16),
        scratch_shapes=[pltpu.VMEM((TT + HALO, C1), F32), pltpu.VMEM((TT + HALO, C1), F32), _shifted_scratch(C1)],
        compiler_params=_params(("arbitrary",), blk),
    )(u, u, wsc, wcf, bcf, lg, lb)


def _mix_conv_bwd1(u, dy, wsc, wcf, bcf, lg, lb):
    T, DIN = u.shape
    C1 = DIN // 5
    last = T // TT - 1

    def body(u_ref, up_ref, dy_ref, wsc_ref, wcf_ref, bcf_ref, lg_ref, lb_ref,
             dz1_ref, dcs_ref, db_ref, dlg_ref, dlb_ref, dbcf_ref, zbuf, pbuf, zsh):
        i = pl.program_id(0)
        mask_c, mask_p, _ = _row_masks(i, last)
        b, _, _, _, _ = _conv_inputs(u_ref[...], up_ref[...], mask_c, mask_p, zbuf, pbuf, C1)
        _fill_shifted(zbuf, zsh)
        cs = _causal_conv(wsc_ref, pbuf)
        z1 = _causal_conv(wcf_ref, zbuf, zsh) + bcf_ref[...]
        zh, rs = _layernorm_stats(z1)
        ln = zh * lg_ref[...] + lb_ref[...]
        dy = dy_ref[...]
        dysc = jnp.where(mask_c, dy[:, 0:C1], 0.0)
        dycf = jnp.where(mask_c, dy[:, C1:], 0.0)
        db_ref[...] = (dysc * cs).astype(BF16)
        dcs_ref[...] = dysc * b
        dl = dycf * _dsilu(ln, _sigmoid(ln))
        dzh = dl * lg_ref[...]
        dz1 = rs * (dzh - jnp.mean(dzh, axis=-1, keepdims=True) - zh * jnp.mean(dzh * zh, axis=-1, keepdims=True))
        dz1_ref[...] = dz1

        @pl.when(i == 0)
        def _():
            dlg_ref[...] = jnp.zeros_like(dlg_ref)
            dlb_ref[...] = jnp.zeros_like(dlb_ref)
            dbcf_ref[...] = jnp.zeros_like(dbcf_ref)

        dlg_ref[...] += jnp.sum(dl * zh, axis=0, keepdims=True)
        dlb_ref[...] += jnp.sum(dl, axis=0, keepdims=True)
        dbcf_ref[...] += jnp.sum(dz1, axis=0, keepdims=True)

    cur, prev, small = _mixer_specs(T, DIN, C1, wsc.shape[0], wcf.shape[0])
    tile = lambda: pl.BlockSpec((TT, C1), lambda i: (i, 0))
    vec = lambda: pl.BlockSpec((1, C1), lambda i: (0, 0))
    blk = _nbytes((TT + HALO, DIN), F32) + 4 * _nbytes((TT, C1), F32) + 16 * _nbytes((TT + HALO, C1), F32)
    return _pallas(
        body, name="mix_conv_bwd1", grid=(T // TT,),
        in_specs=[cur, prev, pl.BlockSpec((TT, 2 * C1), lambda i: (i, 0))] + small,
        out_specs=[tile(), tile(), tile(), vec(), vec(), vec()],
        out_shape=[jax.ShapeDtypeStruct((T, C1), F32), jax.ShapeDtypeStruct((T, C1), F32),
                   jax.ShapeDtypeStruct((T, C1), BF16)] + [jax.ShapeDtypeStruct((1, C1), F32)] * 3,
        scratch_shapes=[pltpu.VMEM((TT + HALO, C1), F32), pltpu.VMEM((TT + HALO, C1), F32), _shifted_scratch(C1)],
        compiler_params=_params(("arbitrary",), blk),
    )(u, u, dy, wsc, wcf, bcf, lg, lb)


def _mix_conv_bwd2(u, dz1, dcs, db, wsc, wcf):
    T, DIN = u.shape
    C1 = DIN // 5
    last = T // TT - 1
    ksc, kcf = wsc.shape[0], wcf.shape[0]

    def body(u_ref, up_ref, dz_ref, dzn_ref, dc_ref, dcn_ref, db_ref, wsc_ref, wcf_ref,
             du_ref, dbin_ref, dwsc_ref, dwcf_ref, zbuf, pbuf, dzbuf, dcbuf, zsh, dzsh):
        i = pl.program_id(0)
        mask_c, mask_p, has_next = _row_masks(i, last)
        _, c, v, a, sg = _conv_inputs(u_ref[...], up_ref[...], mask_c, mask_p, zbuf, pbuf, C1)
        dz1 = dz_ref[...]
        dcs = dc_ref[...]
        dzbuf[0:TT, :] = dz1
        dzbuf[TT:, :] = jnp.where(has_next, dzn_ref[...], 0.0)
        dcbuf[0:TT, :] = dcs
        dcbuf[TT:, :] = jnp.where(has_next, dcn_ref[...], 0.0)

        @pl.when(i == 0)
        def _():
            dbin_ref[...] = jnp.zeros_like(dbin_ref)
            dwsc_ref[...] = jnp.zeros_like(dwsc_ref)
            dwcf_ref[...] = jnp.zeros_like(dwcf_ref)

        _fill_shifted(zbuf, zsh)
        _fill_shifted(dzbuf, dzsh)
        _conv_weight_sums(dwcf_ref, dz1, zbuf, zsh)
        _conv_weight_sums(dwsc_ref, dcs, pbuf)
        dz0 = jnp.where(mask_c, _anticausal_conv(wcf_ref, dzbuf, dzsh), 0.0)
        dp = jnp.where(mask_c, _anticausal_conv(wsc_ref, dcbuf), 0.0)
        parts = (db_ref[...].astype(F32), dp * v, dp * c, dz0 * sg, dz0 * a * sg * (1.0 - sg))
        for k, part in enumerate(parts):
            du_ref[:, k * C1:(k + 1) * C1] = part.astype(BF16)
            dbin_ref[:, k * C1:(k + 1) * C1] += jnp.sum(part, axis=0, keepdims=True)

    cur, prev, small = _mixer_specs(T, DIN, C1, ksc, kcf)
    tile = lambda: pl.BlockSpec((TT, C1), lambda i: (i, 0))
    nxt = lambda: pl.BlockSpec((HALO, C1), lambda i: (jnp.minimum((i + 1) * (TT // HALO), T // HALO - 1), 0))
    blk = (_nbytes((TT + HALO, DIN), F32) + _nbytes((TT, DIN), BF16) + 5 * _nbytes((TT, C1), F32)
           + 16 * _nbytes((TT + HALO, C1), F32))
    buf = lambda: pltpu.VMEM((TT + HALO, C1), F32)
    return _pallas(
        body, name="mix_conv_bwd2", grid=(T // TT,),
        in_specs=[cur, prev, tile(), nxt(), tile(), nxt(), tile(), small[0], small[1]],
        out_specs=[pl.BlockSpec((TT, DIN), lambda i: (i, 0)), pl.BlockSpec((1, DIN), lambda i: (0, 0)),
                   pl.BlockSpec((ksc, C1), lambda i: (0, 0)), pl.BlockSpec((kcf, C1), lambda i: (0, 0))],
        out_shape=[jax.ShapeDtypeStruct((T, DIN), BF16), jax.ShapeDtypeStruct((1, DIN), F32),
                   jax.ShapeDtypeStruct((ksc, C1), F32), jax.ShapeDtypeStruct((kcf, C1), F32)],
        scratch_shapes=[buf(), buf(), buf(), buf(), _shifted_scratch(C1), _shifted_scratch(C1)],
        compiler_params=_params(("arbitrary",), blk),
    )(u, u, dz1, dz1, dcs, dcs, db, wsc, wcf)


def _place():
    x, y, c = lax.axis_index("x"), lax.axis_index("y"), lax.axis_index("c")
    chips = [(1 - x, y), (x, 1 - y), (1 - x, 1 - y)]
    return x, y, c, chips


ANY = pl.BlockSpec(memory_space=pl.ANY)


def _cast_own_block(place, w, name):
    R, C = w.shape
    tr = _row_tile(R // 2, 256, 16)
    nblk = R // 2 // tr

    def body(place_ref, w_ref, o_ref):
        o_ref[...] = w_ref[...].astype(BF16)

    return _pallas(
        body, name=name,
        grid_spec=pltpu.PrefetchScalarGridSpec(
            num_scalar_prefetch=1, grid=(2, nblk),
            in_specs=[pl.BlockSpec((tr, C), lambda h, i, p: (h * nblk + i, 0))],
            out_specs=pl.BlockSpec((None, None, tr, C), lambda h, i, p: (p[0], h, i, 0))),
        out_shape=jax.ShapeDtypeStruct((N_CHIPS, 2, R // 2, C), BF16),
        compiler_params=_params(("parallel", "parallel"), _nbytes((tr, C), F32) + _nbytes((tr, C), BF16)),
    )(place, w)


def _gather_weights(bufs, after):
    nw = len(bufs)

    def body(*refs):
        o_refs = refs[nw + 1:2 * nw + 1]
        send, recv = refs[2 * nw + 1:]
        x, y, c, chips = _place()
        s = 2 * x + y
        sib = (x, y, 1 - c)

        def remote(w, k, blk, half, to):
            ref = o_refs[w].at[blk, half]
            return pltpu.make_async_remote_copy(src_ref=ref, dst_ref=ref, send_sem=send.at[6 * w + k],
                                                recv_sem=recv.at[6 * w + k], device_id=to, device_id_type=MESH)

        sends = []
        for w in range(nw):
            for r, (tx, ty) in enumerate(chips):
                cp = remote(w, r, s, c, (tx, ty, c))
                cp.start()
                sends.append(cp)
        for w in range(nw):
            for r, (tx, ty) in enumerate(chips):
                sr = 2 * tx + ty
                remote(w, r, sr, c, (tx, ty, c)).wait_recv()
                cp = remote(w, 3 + r, sr, c, sib)
                cp.start()
                sends.append(cp)
        for w in range(nw):
            for r, (tx, ty) in enumerate(chips):
                remote(w, 3 + r, 2 * tx + ty, 1 - c, sib).wait_recv()
        for cp in sends:
            cp.wait_send()

    return _pallas(
        body, name="gather_weights", in_specs=[ANY] * (nw + 1), out_specs=[ANY] * nw,
        out_shape=[jax.ShapeDtypeStruct(b.shape, b.dtype) for b in bufs],
        input_output_aliases={w: w for w in range(nw)},
        scratch_shapes=[pltpu.SemaphoreType.DMA((6 * nw,)), pltpu.SemaphoreType.DMA((6 * nw,))],
    )(*bufs, after)


HBM = pl.BlockSpec(memory_space=pltpu.HBM)
SEM = pl.BlockSpec(memory_space=pltpu.SEMAPHORE)
EFFECT = pltpu.SideEffectType.DATAFLOW_SIDE_EFFECTING


def _gather_copies(refs, send, recv):
    x, y, c, chips = _place()
    s = 2 * x + y
    return [pltpu.make_async_remote_copy(src_ref=ref.at[s, c], dst_ref=ref.at[s, c], send_sem=send.at[3 * w + r],
                                         recv_sem=recv.at[3 * w + r], device_id=(tx, ty, c), device_id_type=MESH)
            for w, ref in enumerate(refs) for r, (tx, ty) in enumerate(chips)]


def _scatter_copies(refs, send, recv):
    x, y, c, chips = _place()
    nw = len(refs) // 2
    return [pltpu.make_async_remote_copy(src_ref=refs[w].at[2 * tx + ty], dst_ref=refs[nw + w].at[r],
                                         send_sem=send.at[3 * w + r], recv_sem=recv.at[3 * w + r],
                                         device_id=(tx, ty, c), device_id_type=MESH)
            for w in range(nw) for r, (tx, ty) in enumerate(chips)]


def _pair_copies(refs, send, recv):
    x, y, c, _ = _place()
    nw = len(refs) // 2
    return [pltpu.make_async_remote_copy(src_ref=refs[w].at[j, 1 - c], dst_ref=refs[nw + w].at[j],
                                         send_sem=send.at[N_CHIPS * w + j], recv_sem=recv.at[N_CHIPS * w + j],
                                         device_id=(x, y, 1 - c), device_id_type=MESH)
            for w in range(nw) for j in range(N_CHIPS)]


def _start_copies(bufs, after, ncopies, make_copies, name):
    n = len(bufs)

    def body(*refs):
        in_refs, send, recv, token = refs[:n], refs[n + 1], refs[n + 2], refs[2 * n + 3]
        for cp in make_copies(in_refs, send, recv):
            cp.start()
        token[...] = jnp.zeros_like(token)

    outs = _pallas(
        body, name=name, in_specs=[HBM] * n + [ANY],
        out_specs=[SEM, SEM] + [HBM] * n + [pl.BlockSpec(memory_space=pltpu.VMEM)],
        out_shape=[pltpu.SemaphoreType.DMA((ncopies,)), pltpu.SemaphoreType.DMA((ncopies,))]
                  + [pltpu.HBM(b.shape, b.dtype) for b in bufs] + [jax.ShapeDtypeStruct((8, 128), F32)],
        input_output_aliases={k: 2 + k for k in range(n)},
        compiler_params=pltpu.CompilerParams(has_side_effects=EFFECT),
    )(*[pltpu.with_memory_space_constraint(b, pltpu.HBM) for b in bufs], after)
    return outs[0], outs[1], list(outs[2:2 + n]), outs[2 + n]


def _wait_copies(send, recv, bufs, after, make_copies, name):
    n = len(bufs)

    def body(*refs):
        in_refs, send_ref, recv_ref = refs[:n], refs[n], refs[n + 1]
        for cp in make_copies(in_refs, send_ref, recv_ref):
            cp.wait_send()
            cp.wait_recv()

    outs = _pallas(
        body, name=name, in_specs=[HBM] * n + [SEM, SEM, ANY], out_specs=[HBM] * n,
        out_shape=[pltpu.HBM(b.shape, b.dtype) for b in bufs],
        input_output_aliases={k: k for k in range(n)},
        compiler_params=pltpu.CompilerParams(has_side_effects=EFFECT),
    )(*bufs, send, recv, after)
    return list(outs)


def _forward_halves(bufs, name):
    nw = len(bufs)

    def body(*refs):
        o_refs = refs[nw:2 * nw]
        send, recv = refs[2 * nw:]
        x, y, c, chips = _place()
        sib = (x, y, 1 - c)
        copies = []
        for w in range(nw):
            for r, (tx, ty) in enumerate(chips):
                ref = o_refs[w].at[2 * tx + ty, c]
                cp = pltpu.make_async_remote_copy(src_ref=ref, dst_ref=ref, send_sem=send.at[3 * w + r],
                                                  recv_sem=recv.at[3 * w + r], device_id=sib, device_id_type=MESH)
                cp.start()
                copies.append(cp)
        for w in range(nw):
            for r, (tx, ty) in enumerate(chips):
                ref = o_refs[w].at[2 * tx + ty, 1 - c]
                pltpu.make_async_remote_copy(src_ref=ref, dst_ref=ref, send_sem=send.at[3 * w + r],
                                             recv_sem=recv.at[3 * w + r], device_id=sib, device_id_type=MESH).wait_recv()
        for cp in copies:
            cp.wait_send()

    return _pallas(
        body, name=name, in_specs=[ANY] * nw, out_specs=[ANY] * nw,
        out_shape=[jax.ShapeDtypeStruct(b.shape, b.dtype) for b in bufs],
        input_output_aliases={w: w for w in range(nw)},
        scratch_shapes=[pltpu.SemaphoreType.DMA((3 * nw,)), pltpu.SemaphoreType.DMA((3 * nw,))],
    )(*bufs)


def _half_exchange(hs, name):
    nw = len(hs)

    def body(*refs):
        o_refs = refs[nw:2 * nw]
        send, recv = refs[2 * nw:]
        x, y, c, _ = _place()
        sib = (x, y, 1 - c)
        copies = []
        for w in range(nw):
            cp = pltpu.make_async_remote_copy(src_ref=o_refs[w].at[c], dst_ref=o_refs[w].at[c], send_sem=send.at[w],
                                              recv_sem=recv.at[w], device_id=sib, device_id_type=MESH)
            cp.start()
            copies.append(cp)
        for w, cp in enumerate(copies):
            cp.wait_send()
            pltpu.make_async_remote_copy(src_ref=o_refs[w].at[c], dst_ref=o_refs[w].at[1 - c], send_sem=send.at[w],
                                         recv_sem=recv.at[w], device_id=sib, device_id_type=MESH).wait_recv()

    return _pallas(
        body, name=name, in_specs=[ANY] * nw, out_specs=[ANY] * nw,
        out_shape=[jax.ShapeDtypeStruct(h.shape, F32) for h in hs],
        input_output_aliases={w: w for w in range(nw)},
        scratch_shapes=[pltpu.SemaphoreType.DMA((nw,)), pltpu.SemaphoreType.DMA((nw,))],
    )(*hs)


def _share_small(v, reduce, name):
    R, C = v.shape

    def body(v_ref, o_ref, *scratch):
        if reduce:
            all_ref, send, recv, lsem = scratch
        else:
            all_ref = o_ref
            send, recv, lsem = scratch
        x, y, c, _ = _place()
        me = 4 * x + 2 * y + c
        loc = pltpu.make_async_copy(v_ref, all_ref.at[me], lsem)
        loc.start()
        copies = []
        for k in range(1, N_DEV):
            kx, ky, kc = (k >> 2) & 1, (k >> 1) & 1, k & 1
            peer = (x ^ kx, y ^ ky, c ^ kc)
            cp = pltpu.make_async_remote_copy(src_ref=v_ref, dst_ref=all_ref.at[me], send_sem=send.at[k - 1],
                                              recv_sem=recv.at[k - 1], device_id=peer, device_id_type=MESH)
            cp.start()
            copies.append(cp)
        for k in range(1, N_DEV):
            kx, ky, kc = (k >> 2) & 1, (k >> 1) & 1, k & 1
            src = 4 * (x ^ kx) + 2 * (y ^ ky) + (c ^ kc)
            pltpu.make_async_remote_copy(src_ref=v_ref, dst_ref=all_ref.at[src], send_sem=send.at[k - 1],
                                         recv_sem=recv.at[k - 1], device_id=(x, y, c), device_id_type=MESH).wait_recv()
        for cp in copies:
            cp.wait_send()
        loc.wait()
        if reduce:
            total = all_ref[0]
            for d in range(1, N_DEV):
                total = total + all_ref[d]
            o_ref[...] = total

    vm = pl.BlockSpec(memory_space=pltpu.VMEM)
    sems = [pltpu.SemaphoreType.DMA((N_DEV - 1,)), pltpu.SemaphoreType.DMA((N_DEV - 1,)), pltpu.SemaphoreType.DMA]
    if reduce:
        out_shape = jax.ShapeDtypeStruct((R, C), F32)
        scratch = [pltpu.VMEM((N_DEV, R, C), F32)] + sems
    else:
        out_shape = jax.ShapeDtypeStruct((N_DEV, R, C), F32)
        scratch = sems
    return _pallas(
        body, name=name, in_specs=[vm], out_specs=vm, out_shape=out_shape, scratch_shapes=scratch,
        compiler_params=pltpu.CompilerParams(vmem_limit_bytes=int(min(4 * N_DEV * R * C * 4 + 2 ** 24, 2 ** 25 + 2 ** 24))),
    )(v)


def _pair_sum(place, g, rb, name):
    _, _, Rh, C = g.shape
    tr = _row_tile(Rh, 256, 16)

    def body(place_ref, g_ref, r_ref, q_ref):
        q_ref[...] = (g_ref[...] + r_ref[...]).astype(BF16)

    blk = 2 * _nbytes((tr, C), F32) + _nbytes((tr, C), BF16)
    return _pallas(
        body, name=name,
        grid_spec=pltpu.PrefetchScalarGridSpec(
            num_scalar_prefetch=1, grid=(N_CHIPS, Rh // tr),
            in_specs=[pl.BlockSpec((None, None, tr, C), lambda j, i, p: (j, p[1], i, 0)),
                      pl.BlockSpec((None, tr, C), lambda j, i, p: (j, i, 0))],
            out_specs=pl.BlockSpec((None, tr, C), lambda j, i, p: (j, i, 0))),
        out_shape=jax.ShapeDtypeStruct((N_CHIPS, Rh, C), BF16),
        compiler_params=_params(("parallel", "parallel"), blk),
    )(place, g, rb)


def _chip_sum(place, g, rb, rc, name):
    _, _, Rh, C = g.shape
    tr = _row_tile(Rh, 256, 16)

    def body(place_ref, g_ref, r_ref, rc_ref, o_ref):
        total = g_ref[...] + r_ref[...]
        for r in range(3):
            total = total + rc_ref[r].astype(F32)
        o_ref[...] = total

    blk = 3 * _nbytes((tr, C), F32) + 3 * _nbytes((tr, C), BF16)
    return _pallas(
        body, name=name,
        grid_spec=pltpu.PrefetchScalarGridSpec(
            num_scalar_prefetch=1, grid=(Rh // tr,),
            in_specs=[pl.BlockSpec((None, None, tr, C), lambda i, p: (p[0], p[1], i, 0)),
                      pl.BlockSpec((None, tr, C), lambda i, p: (p[0], i, 0)),
                      pl.BlockSpec((3, tr, C), lambda i, p: (0, i, 0))],
            out_specs=pl.BlockSpec((None, tr, C), lambda i, p: (p[1], i, 0))),
        out_shape=jax.ShapeDtypeStruct((2, Rh, C), F32),
        compiler_params=_params(("parallel",), blk),
    )(place, g, rb, rc)


def _adamw_math(w, g, m, v):
    m = ADAM_B1 * m + (1.0 - ADAM_B1) * g
    v = ADAM_B2 * v + (1.0 - ADAM_B2) * jnp.square(g)
    m_hat = m / (1.0 - ADAM_B1 ** ADAM_STEP)
    v_hat = v / (1.0 - ADAM_B2 ** ADAM_STEP)
    delta = -ADAM_LR * (m_hat / (jnp.sqrt(v_hat) + ADAM_EPS) + ADAM_WD * w)
    return delta, m, v


def _adamw(w, g, m, v, name):
    R, C = w.shape
    tr = _row_tile(R, 256)

    def body(w_ref, g_ref, m_ref, v_ref, d_ref, nm_ref, nv_ref):
        d, nm, nv = _adamw_math(w_ref[...], g_ref[...], m_ref[...], v_ref[...])
        d_ref[...] = d
        nm_ref[...] = nm
        nv_ref[...] = nv

    spec = pl.BlockSpec((tr, C), lambda i: (i, 0))
    shp = jax.ShapeDtypeStruct((R, C), F32)
    return _pallas(
        body, name=name, grid=(R // tr,), in_specs=[spec] * 4, out_specs=[spec] * 3, out_shape=[shp] * 3,
        compiler_params=_params(("parallel",), 7 * _nbytes((tr, C), F32)),
    )(w, g, m, v)


def _adamw_small(ws, gs, ms, vs):
    n = len(ws)

    def body(*refs):
        for k in range(n):
            w_ref, g_ref, m_ref, v_ref = (refs[q * n + k] for q in range(4))
            d, nm, nv = _adamw_math(w_ref[...], g_ref[...], m_ref[...], v_ref[...])
            refs[4 * n + k][...] = d
            refs[5 * n + k][...] = nm
            refs[6 * n + k][...] = nv

    vm = pl.BlockSpec(memory_space=pltpu.VMEM)
    shapes = [jax.ShapeDtypeStruct(w.shape, F32) for w in ws]
    outs = _pallas(
        body, name="adamw_small", in_specs=[vm] * (4 * n), out_specs=[vm] * (3 * n), out_shape=shapes * 3,
    )(*ws, *gs, *ms, *vs)
    return outs[:n], outs[n:2 * n], outs[2 * n:]


def _pad_rows(a, rows):
    return jnp.pad(a, ((0, rows - a.shape[0]), (0, 0)))


def kernel(x, meta_tokens, ffn1_norm, ffn1_w_gate, ffn1_w_up, ffn1_w_down, mix_norm, w_in, b_in, conv_sc_w, conv_cf_w, conv_cf_b, ln_cf_g, ln_cf_b, w_out, ffn2_norm, ffn2_w_gate, ffn2_w_up, ffn2_w_down, final_norm, loss_target, m_meta_tokens, m_ffn1_norm, m_ffn1_w_gate, m_ffn1_w_up, m_ffn1_w_down, m_mix_norm, m_w_in, m_b_in, m_conv_sc_w, m_conv_cf_w, m_conv_cf_b, m_ln_cf_g, m_ln_cf_b, m_w_out, m_ffn2_norm, m_ffn2_w_gate, m_ffn2_w_up, m_ffn2_w_down, m_final_norm, v_meta_tokens, v_ffn1_norm, v_ffn1_w_gate, v_ffn1_w_up, v_ffn1_w_down, v_mix_norm, v_w_in, v_b_in, v_conv_sc_w, v_conv_cf_w, v_conv_cf_b, v_ln_cf_g, v_ln_cf_b, v_w_out, v_ffn2_norm, v_ffn2_w_gate, v_ffn2_w_up, v_ffn2_w_down, v_final_norm):
    xi, yi, ci = lax.axis_index("x"), lax.axis_index("y"), lax.axis_index("c")
    chip = 2 * xi + yi
    place = jnp.stack([chip, ci]).astype(jnp.int32)

    x2 = x[0]
    tgt = loss_target[0]
    S, D = x2.shape
    C1 = D // 2
    cs = conv_sc_w.shape[2]
    ksc, kcf = conv_sc_w.shape[1], conv_cf_w.shape[1]
    ms = meta_tokens.shape[1]

    rows_small = N_META + 8 + 32
    assert ksc <= 8 and kcf <= 32 and cs <= ms
    pack = jnp.concatenate([
        meta_tokens,
        jnp.pad(conv_sc_w[0], ((0, 8 - ksc), (0, ms - cs))),
        jnp.pad(conv_cf_w[0], ((0, 32 - kcf), (0, ms - cs)))], axis=0)
    everyone = _share_small(pack, False, "share_params")[0::2]
    meta_full = jnp.transpose(everyone[:, :N_META, :], (1, 0, 2)).reshape(N_META, D)
    wsc_full = jnp.transpose(everyone[:, N_META:N_META + ksc, :cs], (1, 0, 2)).reshape(ksc, C1)
    wcf_full = jnp.transpose(everyone[:, N_META + 8:N_META + 8 + kcf, :cs], (1, 0, 2)).reshape(kcf, C1)

    big = {"ffn1_w_gate": ffn1_w_gate, "ffn1_w_up": ffn1_w_up, "ffn1_w_down": ffn1_w_down, "w_in": w_in, "w_out": w_out,
           "ffn2_w_gate": ffn2_w_gate, "ffn2_w_up": ffn2_w_up, "ffn2_w_down": ffn2_w_down}
    big_m = {"ffn1_w_gate": m_ffn1_w_gate, "ffn1_w_up": m_ffn1_w_up, "ffn1_w_down": m_ffn1_w_down, "w_in": m_w_in,
             "w_out": m_w_out, "ffn2_w_gate": m_ffn2_w_gate, "ffn2_w_up": m_ffn2_w_up, "ffn2_w_down": m_ffn2_w_down}
    big_v = {"ffn1_w_gate": v_ffn1_w_gate, "ffn1_w_up": v_ffn1_w_up, "ffn1_w_down": v_ffn1_w_down, "w_in": v_w_in,
             "w_out": v_w_out, "ffn2_w_gate": v_ffn2_w_gate, "ffn2_w_up": v_ffn2_w_up, "ffn2_w_down": v_ffn2_w_down}
    buf = {nm: _cast_own_block(place, w[0], "cast_" + nm) for nm, w in big.items()}
    whole_weight = lambda g: g.reshape(N_CHIPS, 2 * g.shape[2], g.shape[3])
    group_mix, group_ffn2 = ["w_in", "w_out"], ["ffn2_w_gate", "ffn2_w_up", "ffn2_w_down"]

    corner = lambda a: a.reshape(-1, a.shape[-1])[:8, :128]
    wg1, wu1, wd1 = (whole_weight(g) for g in _gather_weights(
        [buf[nm] for nm in ["ffn1_w_gate", "ffn1_w_up", "ffn1_w_down"]], corner(everyone)))
    send_mix, recv_mix, thru_mix, token_mix = _start_copies(
        [buf[nm] for nm in group_mix], corner(wd1), 3 * len(group_mix), _gather_copies, "gather_start_mix")
    send_ffn2, recv_ffn2, thru_ffn2, token_ffn2 = _start_copies(
        [buf[nm] for nm in group_ffn2], token_mix, 3 * len(group_ffn2), _gather_copies, "gather_start_ffn2")
    F = N_CHIPS * wd1.shape[1]

    hs0, n1 = _embed_rms(x2, meta_full, ffn1_norm)
    g1, u1, a1 = _ffn_up(n1, wg1, wu1, token_ffn2, "ffn1_up")
    hs1 = _ffn_down(a1, wd1.reshape(F, D), hs0, "ffn1_down")
    arrived = _wait_copies(send_mix, recv_mix, thru_mix, corner(hs1), _gather_copies, "gather_wait_mix")
    win, wout = (whole_weight(g) for g in _forward_halves(arrived, "gather_forward_mix"))
    n2 = _rms(hs1, mix_norm, "rms_mix")
    u = _mix_in(n2, win, b_in)
    y = _mix_conv_fwd(u, wsc_full, wcf_full, conv_cf_b, ln_cf_g, ln_cf_b)
    hs2 = _mix_out(y, wout.reshape(D, D), hs1)
    arrived = _wait_copies(send_ffn2, recv_ffn2, thru_ffn2, corner(hs2), _gather_copies, "gather_wait_ffn2")
    wg2, wu2, wd2 = (whole_weight(g) for g in _forward_halves(arrived, "gather_forward_ffn2"))
    n3 = _rms(hs2, ffn2_norm, "rms_ffn2")
    g2, u2, a2 = _ffn_up(n3, wg2, wu2, token_ffn2, "ffn2_up")
    hs3 = _ffn_down(a2, wd2.reshape(F, D), hs2, "ffn2_down")

    def pair_start(group, after, tag):
        gs = [g for _, g in group]
        lands = [lax.empty((N_CHIPS,) + g.shape[2:], F32) for g in gs]
        send, recv, thru, token = _start_copies(gs + lands, after, N_CHIPS * len(gs), _pair_copies,
                                                "pair_start_" + tag)
        return (group, send, recv, thru, tag), token

    def scatter_start(state, after):
        group, send, recv, thru, tag = state
        thru = _wait_copies(send, recv, thru, corner(after), _pair_copies, "pair_wait_" + tag)
        gs, sib = thru[:len(group)], thru[len(group):]
        sums = [_pair_sum(place, g, rb, "pair_sum_" + nm) for (nm, _), g, rb in zip(group, gs, sib)]
        lands = [lax.empty((3,) + q.shape[1:], BF16) for q in sums]
        send, recv, thru, token = _start_copies(sums + lands, corner(sums[-1]), 3 * len(gs), _scatter_copies,
                                                "scatter_start_" + tag)
        return ([(nm, g) for (nm, _), g in zip(group, gs)], sib, send, recv, thru, tag), token

    def reduce_finish(state, after):
        group, sib, send, recv, thru, tag = state
        lands = _wait_copies(send, recv, thru, corner(after), _scatter_copies, "scatter_wait_" + tag)[len(group):]
        mine = [_chip_sum(place, g, rb, rc, "chip_sum_" + nm) for (nm, g), rb, rc in zip(group, sib, lands)]
        whole = _half_exchange(mine, "half_exchange_" + tag)
        out = {}
        for (nm, _), g in zip(group, whole):
            w = big[nm]
            g3d = g.reshape(w.shape)
            d, new_m, new_v = _adamw(w[0], g3d[0], big_m[nm][0], big_v[nm][0], "adamw_" + nm)
            out[nm] = (g3d, d[None], new_m[None], new_v[None])
        return out

    dhs3, df2, loss_row, d_final = _final_loss(hs3, final_norm.reshape(1, D), tgt)

    dg2, du2 = _ffn_bwd_act(df2, wd2, g2, u2, token_ffn2, "ffn2_bwd_act")
    gw_d2 = _wgrad_down(a2, df2, "wgrad_ffn2_down")
    gw_g2 = _wgrad_cols(n3, [dg2], "wgrad_ffn2_gate")[0]
    gw_u2 = _wgrad_cols(n3, [du2], "wgrad_ffn2_up")[0]
    pair_ffn2, token = pair_start([("ffn2_w_gate", gw_g2), ("ffn2_w_up", gw_u2), ("ffn2_w_down", gw_d2)],
                                  corner(gw_u2), "ffn2")
    dn3 = _nt_panel([dg2, du2], [wg2, wu2], token, "ffn2_bwd_in")
    red_ffn2, token = scatter_start(pair_ffn2, dn3)
    dhs2, dm, d_ffn2 = _rms_bwd(dn3, hs2, ffn2_norm, dhs3, 1.0, "rms_bwd_ffn2")

    dy = _nt_panel([dm], [wout.reshape(1, D, D)], token, "mix_bwd_out")
    gw_out = _wgrad_out(y, dm)
    dz1, dcs, db, d_lg, d_lb, d_bcf = _mix_conv_bwd1(u, dy, wsc_full, wcf_full, conv_cf_b, ln_cf_g, ln_cf_b)
    du, d_bin, d_wsc, d_wcf = _mix_conv_bwd2(u, dz1, dcs, db, wsc_full, wcf_full)
    gw_in = _wgrad_cols(n2, [du], "wgrad_w_in")[0]
    pair_mix, token = pair_start([("w_in", gw_in), ("w_out", gw_out)], corner(gw_in), "mix")
    dn2 = _nt_panel([du], [win], token, "mix_bwd_in")
    red_mix, token = scatter_start(pair_mix, dn2)
    dhs1, df1, d_mix = _rms_bwd(dn2, hs1, mix_norm, dhs2, FFN_RES_SCALE, "rms_bwd_mix")

    dg1, du1 = _ffn_bwd_act(df1, wd1, g1, u1, token, "ffn1_bwd_act")
    gw_d1 = _wgrad_down(a1, df1, "wgrad_ffn1_down")
    gw_g1 = _wgrad_cols(n1, [dg1], "wgrad_ffn1_gate")[0]
    pair_ffn1a, token = pair_start([("ffn1_w_down", gw_d1), ("ffn1_w_gate", gw_g1)], corner(gw_g1), "ffn1a")
    gw_u1 = _wgrad_cols(n1, [du1], "wgrad_ffn1_up", token)[0]
    red_ffn1a, token = scatter_start(pair_ffn1a, gw_u1)
    pair_ffn1b, token = pair_start([("ffn1_w_up", gw_u1)], token, "ffn1b")
    dn1 = _nt_panel([dg1, du1], [wg1, wu1], token, "ffn1_bwd_in")
    red_ffn1b, token = scatter_start(pair_ffn1b, dn1)
    grad_x, d_meta, d_ffn1 = _rms_bwd_first(dn1, hs0, ffn1_norm, dhs1, token)

    big_out = reduce_finish(red_ffn2, grad_x)
    big_out.update(reduce_finish(red_mix, big_out["ffn2_w_down"][1]))
    big_out.update(reduce_finish(red_ffn1a, big_out["w_out"][1]))
    big_out.update(reduce_finish(red_ffn1b, big_out["ffn1_w_gate"][1]))

    W = C1
    rows = lambda a: a.reshape(-1, W)
    parts = [rows(d_ffn1), rows(d_mix), rows(d_ffn2), rows(d_final), rows(d_bin), d_bcf, d_lg, d_lb,
             d_wsc, d_wcf, rows(d_meta), jnp.broadcast_to(loss_row[:, :1], (1, W))]
    sizes = [p.shape[0] for p in parts]
    total_rows = sum(sizes)
    packed = _pad_rows(jnp.concatenate(parts, axis=0), -(-total_rows // 8) * 8)
    summed = _share_small(packed, True, "sum_small")
    offs = [0]
    for n in sizes:
        offs.append(offs[-1] + n)
    piece = lambda k: summed[offs[k]:offs[k + 1]]
    loss = piece(11)[0, 0]
    g_ffn1, g_mix, g_ffn2 = (piece(k).reshape(1, D) for k in range(3))
    g_final = piece(3).reshape(1, D)
    g_bin = piece(4).reshape(1, -1)
    g_bcf, g_lg, g_lb = piece(5), piece(6), piece(7)
    g_wsc = lax.dynamic_slice_in_dim(piece(8), chip * cs, cs, axis=1)
    g_wcf = lax.dynamic_slice_in_dim(piece(9), chip * cs, cs, axis=1)
    g_meta = lax.dynamic_slice_in_dim(piece(10).reshape(N_META, D), chip * ms, ms, axis=1)

    small_names = ["meta_tokens", "ffn1_norm", "mix_norm", "b_in", "conv_sc_w", "conv_cf_w", "conv_cf_b", "ln_cf_g",
                   "ln_cf_b", "ffn2_norm", "final_norm"]
    small_w = [meta_tokens, ffn1_norm, mix_norm, b_in, conv_sc_w[0], conv_cf_w[0], conv_cf_b, ln_cf_g, ln_cf_b,
               ffn2_norm, final_norm.reshape(1, D)]
    small_g = [g_meta, g_ffn1, g_mix, g_bin, g_wsc, g_wcf, g_bcf, g_lg, g_lb, g_ffn2, g_final]
    small_m = [m_meta_tokens, m_ffn1_norm, m_mix_norm, m_b_in, m_conv_sc_w[0], m_conv_cf_w[0], m_conv_cf_b, m_ln_cf_g,
               m_ln_cf_b, m_ffn2_norm, m_final_norm.reshape(1, D)]
    small_v = [v_meta_tokens, v_ffn1_norm, v_mix_norm, v_b_in, v_conv_sc_w[0], v_conv_cf_w[0], v_conv_cf_b, v_ln_cf_g,
               v_ln_cf_b, v_ffn2_norm, v_final_norm.reshape(1, D)]
    s_d, s_m, s_v = _adamw_small(small_w, small_g, small_m, small_v)
    shapes = {"conv_sc_w": conv_sc_w.shape, "conv_cf_w": conv_cf_w.shape, "final_norm": final_norm.shape}
    small_out = {}
    for nm, g, d, m, v in zip(small_names, small_g, s_d, s_m, s_v):
        shp = shapes.get(nm, g.shape)
        small_out[nm] = tuple(t.reshape(shp) for t in (g, d, m, v))

    order = ["meta_tokens", "ffn1_norm", "ffn1_w_gate", "ffn1_w_up", "ffn1_w_down", "mix_norm", "w_in", "b_in",
             "conv_sc_w", "conv_cf_w", "conv_cf_b", "ln_cf_g", "ln_cf_b", "w_out", "ffn2_norm", "ffn2_w_gate",
             "ffn2_w_up", "ffn2_w_down", "final_norm"]
    res = {**big_out, **small_out}
    outs = [loss, grad_x[None]]
    for q in range(4):
        outs.extend(res[nm][q] for nm in order)
    return tuple(outs)
```

```python
import functools

import jax
import jax.numpy as jnp
from jax import lax
from jax.experimental import pallas as pl
from jax.experimental.pallas import tpu as pltpu

F32 = jnp.float32
BF16 = jnp.bfloat16
MESH = pl.DeviceIdType.MESH

N_META = 16
TT = 128
PAD = TT - N_META
HALO = 32
EPS = 1e-6
FFN_RES_SCALE = 0.5
N_CHIPS = 4
N_DEV = 8

ADAM_LR = 0.001
ADAM_B1 = 0.9
ADAM_B2 = 0.999
ADAM_EPS = 1e-08
ADAM_WD = 0.01
ADAM_STEP = 10

V7X_VMEM_BYTES = 64 * 2 ** 20
NT_DIMS = (((1,), (1,)), ((), ()))
TN_DIMS = (((0,), (0,)), ((), ()))


def _params(semantics, block_bytes):
    limit = min(2 * block_bytes + 16 * 2 ** 20, V7X_VMEM_BYTES - 6 * 2 ** 20)
    return pltpu.CompilerParams(dimension_semantics=semantics, vmem_limit_bytes=int(limit))


def _pallas(body, out_shape, **kw):
    if "grid" not in kw and "grid_spec" not in kw:
        return pl.pallas_call(body, out_shape=out_shape, **kw)
    big = lambda shape, dtype: jnp.issubdtype(dtype, jnp.floating) and len(shape) >= 2
    pin_out = lambda s: pltpu.HBM(s.shape, s.dtype) if big(s.shape, s.dtype) else s
    single = not isinstance(out_shape, (list, tuple))
    shapes = pin_out(out_shape) if single else [pin_out(s) for s in out_shape]
    call = pl.pallas_call(body, out_shape=shapes, **kw)
    pin = lambda a: pltpu.with_memory_space_constraint(a, pltpu.HBM) if big(a.shape, a.dtype) else a
    return lambda *operands: call(*[pin(a) for a in operands])


def _nbytes(shape, dtype):
    n = 1
    for d in shape:
        if d is not None:
            n *= d
    return n * jnp.dtype(dtype).itemsize


def _row_tile(rows, target, mult=8):
    best = None
    for t in range(mult, min(rows, target) + 1, mult):
        if rows % t == 0:
            best = t
    assert best is not None, (rows, target, mult)
    return best


def _sigmoid(v):
    return jax.nn.sigmoid(v)


def _dsilu(v, s):
    return s * (1.0 + v * (1.0 - s))


def _embed_rms(x2, meta, gain):
    S, D = x2.shape
    T = S + TT

    def body(x_ref, meta_ref, g_ref, hs_ref, n_ref):
        i = pl.program_id(0)

        @pl.when(i == 0)
        def _():
            hs_ref[...] = jnp.zeros_like(hs_ref)
            hs_ref[PAD:, :] = meta_ref[...]

        @pl.when(i > 0)
        def _():
            hs_ref[...] = x_ref[...]

        h = hs_ref[...]
        r = lax.rsqrt(jnp.mean(h * h, axis=-1, keepdims=True) + EPS)
        n_ref[...] = ((h * r) * g_ref[...]).astype(BF16)

    blk = _nbytes((TT, D), F32) * 2 + _nbytes((TT, D), BF16)
    return _pallas(
        body, name="embed_rms", grid=(T // TT,),
        in_specs=[pl.BlockSpec((TT, D), lambda i: (jnp.maximum(i - 1, 0), 0)),
                  pl.BlockSpec((N_META, D), lambda i: (0, 0)),
                  pl.BlockSpec((1, D), lambda i: (0, 0))],
        out_specs=[pl.BlockSpec((TT, D), lambda i: (i, 0)), pl.BlockSpec((TT, D), lambda i: (i, 0))],
        out_shape=[jax.ShapeDtypeStruct((T, D), F32), jax.ShapeDtypeStruct((T, D), BF16)],
        compiler_params=_params(("parallel",), blk),
    )(x2, meta, gain)


def _rms(hs, gain, name):
    T, D = hs.shape
    te = _row_tile(T, 384)

    def body(h_ref, g_ref, n_ref):
        h = h_ref[...]
        r = lax.rsqrt(jnp.mean(h * h, axis=-1, keepdims=True) + EPS)
        n_ref[...] = ((h * r) * g_ref[...]).astype(BF16)

    blk = _nbytes((te, D), F32) + _nbytes((te, D), BF16)
    return _pallas(
        body, name=name, grid=(T // te,),
        in_specs=[pl.BlockSpec((te, D), lambda i: (i, 0)), pl.BlockSpec((1, D), lambda i: (0, 0))],
        out_specs=pl.BlockSpec((te, D), lambda i: (i, 0)),
        out_shape=jax.ShapeDtypeStruct((T, D), BF16),
        compiler_params=_params(("parallel",), blk),
    )(hs, gain)


def _rms_bwd_math(dn, h, g):
    r = lax.rsqrt(jnp.mean(h * h, axis=-1, keepdims=True) + EPS)
    xh = h * r
    dgain = jnp.sum(dn * xh, axis=0, keepdims=True)
    dxh = dn * g
    dh = r * (dxh - xh * jnp.mean(dxh * xh, axis=-1, keepdims=True))
    return dh, dgain


def _rms_bwd(dn, hs, gain, dres, scale, name):
    T, D = hs.shape
    te = _row_tile(T, 384)

    def body(dn_ref, h_ref, g_ref, dres_ref, dhs_ref, dhb_ref, dg_ref):
        dh, dgain = _rms_bwd_math(dn_ref[...], h_ref[...], g_ref[...])
        d = dres_ref[...] + dh
        dhs_ref[...] = d
        dhb_ref[...] = (scale * d).astype(BF16)

        @pl.when(pl.program_id(0) == 0)
        def _():
            dg_ref[...] = jnp.zeros_like(dg_ref)

        dg_ref[...] += dgain

    blk = _nbytes((te, D), F32) * 4 + _nbytes((te, D), BF16)
    row = lambda i: (i, 0)
    return _pallas(
        body, name=name, grid=(T // te,),
        in_specs=[pl.BlockSpec((te, D), row), pl.BlockSpec((te, D), row), pl.BlockSpec((1, D), lambda i: (0, 0)),
                  pl.BlockSpec((te, D), row)],
        out_specs=[pl.BlockSpec((te, D), row), pl.BlockSpec((te, D), row), pl.BlockSpec((1, D), lambda i: (0, 0))],
        out_shape=[jax.ShapeDtypeStruct((T, D), F32), jax.ShapeDtypeStruct((T, D), BF16),
                   jax.ShapeDtypeStruct((1, D), F32)],
        compiler_params=_params(("arbitrary",), blk),
    )(dn, hs, gain, dres)


def _rms_bwd_first(dn, hs, gain, dres, after):
    T, D = hs.shape
    S = T - TT

    def body(dn_ref, h_ref, g_ref, dres_ref, after_ref, gx_ref, gm_ref, dg_ref):
        i = pl.program_id(0)
        dh, dgain = _rms_bwd_math(dn_ref[...], h_ref[...], g_ref[...])
        d = dres_ref[...] + dh

        @pl.when(i == 0)
        def _():
            dg_ref[...] = jnp.zeros_like(dg_ref)
            gm_ref[...] = d[PAD:, :]

        @pl.when(i > 0)
        def _():
            gx_ref[...] = d

        dg_ref[...] += dgain

    blk = _nbytes((TT, D), F32) * 4
    row = lambda i: (i, 0)
    return _pallas(
        body, name="rms_bwd_ffn1", grid=(T // TT,),
        in_specs=[pl.BlockSpec((TT, D), row), pl.BlockSpec((TT, D), row), pl.BlockSpec((1, D), lambda i: (0, 0)),
                  pl.BlockSpec((TT, D), row), TOKEN],
        out_specs=[pl.BlockSpec((TT, D), lambda i: (jnp.maximum(i - 1, 0), 0)),
                   pl.BlockSpec((N_META, D), lambda i: (0, 0)), pl.BlockSpec((1, D), lambda i: (0, 0))],
        out_shape=[jax.ShapeDtypeStruct((S, D), F32), jax.ShapeDtypeStruct((N_META, D), F32),
                   jax.ShapeDtypeStruct((1, D), F32)],
        compiler_params=_params(("arbitrary",), blk),
    )(dn, hs, gain, dres, after)


def _final_loss(hs, gain, tgt):
    T, D = hs.shape

    def body(h_ref, g_ref, t_ref, dhs_ref, dhb_ref, loss_ref, dg_ref):
        i = pl.program_id(0)
        h = h_ref[...]
        g = g_ref[...]
        r = lax.rsqrt(jnp.mean(h * h, axis=-1, keepdims=True) + EPS)
        xh = h * r
        e = jnp.where(i > 0, xh * g - t_ref[...], 0.0)
        tile_loss = jnp.sum(jnp.sum(e * e, axis=1, keepdims=True), axis=0, keepdims=True) * (0.5 / D)
        dout = e * (1.0 / D)
        dgain = jnp.sum(dout * xh, axis=0, keepdims=True)
        dxh = dout * g
        d = r * (dxh - xh * jnp.mean(dxh * xh, axis=-1, keepdims=True))
        dhs_ref[...] = d
        dhb_ref[...] = (FFN_RES_SCALE * d).astype(BF16)

        @pl.when(i == 0)
        def _():
            loss_ref[...] = jnp.zeros_like(loss_ref)
            dg_ref[...] = jnp.zeros_like(dg_ref)

        loss_ref[...] += jnp.broadcast_to(tile_loss, loss_ref.shape)
        dg_ref[...] += dgain

    blk = _nbytes((TT, D), F32) * 3 + _nbytes((TT, D), BF16)
    row = lambda i: (i, 0)
    return _pallas(
        body, name="final_loss", grid=(T // TT,),
        in_specs=[pl.BlockSpec((TT, D), row), pl.BlockSpec((1, D), lambda i: (0, 0)),
                  pl.BlockSpec((TT, D), lambda i: (jnp.maximum(i - 1, 0), 0))],
        out_specs=[pl.BlockSpec((TT, D), row), pl.BlockSpec((TT, D), row),
                   pl.BlockSpec((1, 128), lambda i: (0, 0)), pl.BlockSpec((1, D), lambda i: (0, 0))],
        out_shape=[jax.ShapeDtypeStruct((T, D), F32), jax.ShapeDtypeStruct((T, D), BF16),
                   jax.ShapeDtypeStruct((1, 128), F32), jax.ShapeDtypeStruct((1, D), F32)],
        compiler_params=_params(("arbitrary",), blk),
    )(hs, gain, tgt)


MXU_COLS = 256


def _tm(T):
    return _row_tile(T, 704, 16)


def _col_chunks(n):
    return [(c, min(MXU_COLS, n - c)) for c in range(0, n, MXU_COLS)]


TOKEN = pl.BlockSpec((8, 128), lambda *_: (0, 0))


def _ffn_up(n, wg, wu, after, name):
    T, D = n.shape
    Fs = wg.shape[2]
    tm = _tm(T)

    def body(n_ref, wg_ref, wu_ref, after_ref, g_ref, u_ref, a_ref):
        nn = n_ref[...]
        for c0, cw in _col_chunks(Fs):
            if 2 * cw == MXU_COLS:
                both = jnp.concatenate([wg_ref[:, c0:c0 + cw], wu_ref[:, c0:c0 + cw]], axis=1)
                gu = jnp.dot(nn, both, preferred_element_type=F32)
                g, u = gu[:, :cw], gu[:, cw:]
            else:
                g = jnp.dot(nn, wg_ref[:, c0:c0 + cw], preferred_element_type=F32)
                u = jnp.dot(nn, wu_ref[:, c0:c0 + cw], preferred_element_type=F32)
            g_ref[:, c0:c0 + cw] = g.astype(BF16)
            u_ref[:, c0:c0 + cw] = u.astype(BF16)
            a_ref[:, c0:c0 + cw] = (jax.nn.silu(g) * u).astype(BF16)

    blk = _nbytes((tm, D), BF16) + 2 * _nbytes((D, Fs), BF16) + 3 * _nbytes((tm, Fs), BF16)
    out = pl.BlockSpec((tm, Fs), lambda j, i: (i, j))
    shp = jax.ShapeDtypeStruct((T, N_CHIPS * Fs), BF16)
    return _pallas(
        body, name=name, grid=(N_CHIPS, T // tm),
        in_specs=[pl.BlockSpec((tm, D), lambda j, i: (i, 0)),
                  pl.BlockSpec((None, D, Fs), lambda j, i: (j, 0, 0)),
                  pl.BlockSpec((None, D, Fs), lambda j, i: (j, 0, 0)), TOKEN],
        out_specs=[out, out, out], out_shape=[shp, shp, shp],
        compiler_params=_params(("parallel", "parallel"), blk),
    )(n, wg, wu, after)


def _ffn_down(a, wd, hs, name):
    T, F = a.shape
    D = wd.shape[1]
    tm = _tm(T)
    tn = D // 4

    def body(a_ref, w_ref, h_ref, o_ref):
        o_ref[...] = h_ref[...] + FFN_RES_SCALE * jnp.dot(a_ref[...], w_ref[...], preferred_element_type=F32)

    blk = _nbytes((tm, F), BF16) + _nbytes((F, tn), BF16) + 3 * _nbytes((tm, tn), F32)
    return _pallas(
        body, name=name, grid=(D // tn, T // tm),
        in_specs=[pl.BlockSpec((tm, F), lambda n, i: (i, 0)), pl.BlockSpec((F, tn), lambda n, i: (0, n)),
                  pl.BlockSpec((tm, tn), lambda n, i: (i, n))],
        out_specs=pl.BlockSpec((tm, tn), lambda n, i: (i, n)),
        out_shape=jax.ShapeDtypeStruct((T, D), F32),
        compiler_params=_params(("parallel", "parallel"), blk),
    )(a, wd, hs)


def _mix_in(n, w, b):
    T, D = n.shape
    Ns = w.shape[2]
    tm = _tm(T)

    def body(n_ref, w_ref, b_ref, u_ref):
        u_ref[...] = jnp.dot(n_ref[...], w_ref[...], preferred_element_type=F32) + b_ref[...]

    blk = _nbytes((tm, D), BF16) + _nbytes((D, Ns), BF16) + 2 * _nbytes((tm, Ns), F32)
    return _pallas(
        body, name="mix_in", grid=(N_CHIPS, T // tm),
        in_specs=[pl.BlockSpec((tm, D), lambda j, i: (i, 0)), pl.BlockSpec((None, D, Ns), lambda j, i: (j, 0, 0)),
                  pl.BlockSpec((1, Ns), lambda j, i: (0, j))],
        out_specs=pl.BlockSpec((tm, Ns), lambda j, i: (i, j)),
        out_shape=jax.ShapeDtypeStruct((T, N_CHIPS * Ns), F32),
        compiler_params=_params(("parallel", "parallel"), blk),
    )(n, w, b)


def _mix_out(y, w, hs):
    T, D = y.shape
    tm = _tm(T)

    def body(y_ref, w_ref, h_ref, o_ref):
        o_ref[...] = h_ref[...] + jnp.dot(y_ref[...], w_ref[...], preferred_element_type=F32)

    blk = _nbytes((tm, D), BF16) + _nbytes((D, D), BF16) + 3 * _nbytes((tm, D), F32)
    return _pallas(
        body, name="mix_out", grid=(T // tm,),
        in_specs=[pl.BlockSpec((tm, D), lambda i: (i, 0)), pl.BlockSpec((D, D), lambda i: (0, 0)),
                  pl.BlockSpec((tm, D), lambda i: (i, 0))],
        out_specs=pl.BlockSpec((tm, D), lambda i: (i, 0)),
        out_shape=jax.ShapeDtypeStruct((T, D), F32),
        compiler_params=_params(("parallel",), blk),
    )(y, w, hs)


def _ffn_bwd_act(dfb, wd, g, u, after, name):
    T, D = dfb.shape
    F = wd.shape[0]
    tm = _row_tile(T, 1408, 16)
    tn = 2 * MXU_COLS

    tr = _tm(tm)

    def body(d_ref, w_ref, g_ref, u_ref, after_ref, dg_ref, du_ref):
        for r0 in range(0, tm, tr):
            dv = d_ref[r0:r0 + tr, :]
            for c0, cw in _col_chunks(tn):
                da = lax.dot_general(dv, w_ref[c0:c0 + cw, :], NT_DIMS, preferred_element_type=F32)
                gv = g_ref[r0:r0 + tr, c0:c0 + cw].astype(F32)
                uv = u_ref[r0:r0 + tr, c0:c0 + cw].astype(F32)
                s = _sigmoid(gv)
                du_ref[r0:r0 + tr, c0:c0 + cw] = (da * (gv * s)).astype(BF16)
                dg_ref[r0:r0 + tr, c0:c0 + cw] = (da * uv * _dsilu(gv, s)).astype(BF16)

    blk = _nbytes((tm, D), BF16) + _nbytes((tn, D), BF16) + 4 * _nbytes((tm, tn), BF16)
    io = pl.BlockSpec((tm, tn), lambda n, i: (i, n))
    shp = jax.ShapeDtypeStruct((T, F), BF16)
    return _pallas(
        body, name=name, grid=(F // tn, T // tm),
        in_specs=[pl.BlockSpec((tm, D), lambda n, i: (i, 0)), pl.BlockSpec((tn, D), lambda n, i: (n, 0)), io, io, TOKEN],
        out_specs=[io, io], out_shape=[shp, shp],
        compiler_params=_params(("parallel", "parallel"), blk),
    )(dfb, wd, g, u, after)


def _nt_panel(lhs_list, w_list, after, name):
    T = lhs_list[0].shape[0]
    nsh, Dout, Ks = w_list[0].shape
    npair = len(lhs_list)
    tm = _row_tile(T, 1408, 16)
    tn = Dout // 2

    def body(*refs):
        l_refs, w_refs, o_ref = refs[:npair], refs[npair:2 * npair], refs[2 * npair + 1]
        j = pl.program_id(2)
        k0 = Ks - Ks % MXU_COLS if npair == 2 and 2 * (Ks % MXU_COLS) == MXU_COLS else Ks
        acc = None
        for p in range(npair):
            part = lax.dot_general(l_refs[p][:, :k0], w_refs[p][:, :k0], NT_DIMS, preferred_element_type=F32)
            acc = part if acc is None else acc + part
        if k0 < Ks:
            lhs = jnp.concatenate([l_refs[p][:, k0:] for p in range(npair)], axis=1)
            rhs = jnp.concatenate([w_refs[p][:, k0:] for p in range(npair)], axis=1)
            acc = acc + lax.dot_general(lhs, rhs, NT_DIMS, preferred_element_type=F32)

        @pl.when(j == 0)
        def _():
            o_ref[...] = acc

        @pl.when(j > 0)
        def _():
            o_ref[...] += acc

    blk = npair * (_nbytes((tm, Ks), BF16) + _nbytes((tn, Ks), BF16)) + 2 * _nbytes((tm, tn), F32)
    return _pallas(
        body, name=name, grid=(Dout // tn, T // tm, nsh),
        in_specs=[pl.BlockSpec((tm, Ks), lambda n, i, j: (i, j))] * npair
                 + [pl.BlockSpec((None, tn, Ks), lambda n, i, j: (j, n, 0))] * npair + [TOKEN],
        out_specs=pl.BlockSpec((tm, tn), lambda n, i, j: (i, n)),
        out_shape=jax.ShapeDtypeStruct((T, Dout), F32),
        compiler_params=_params(("parallel", "parallel", "arbitrary"), blk),
    )(*lhs_list, *w_list, after)


def _tn_call(name, grid, lhs, lhs_spec, rhs_list, rhs_specs, out_shapes, out_specs, blk, after=None):
    nr = len(rhs_list)
    extra = [] if after is None else [after]

    def body(*refs):
        l_ref, r_refs, o_refs = refs[0], refs[1:1 + nr], refs[len(refs) - nr:]
        k = pl.program_id(len(grid) - 1)
        lv = l_ref[...]
        for q in range(nr):
            part = lax.dot_general(lv, r_refs[q][...], TN_DIMS, preferred_element_type=F32)
            part = part.reshape(o_refs[q].shape)

            @pl.when(k == 0)
            def _(o=o_refs[q], part=part):
                o[...] = part

            @pl.when(k > 0)
            def _(o=o_refs[q], part=part):
                o[...] += part

    return _pallas(
        body, name=name, grid=grid, in_specs=[lhs_spec] + rhs_specs + [TOKEN] * len(extra), out_specs=out_specs,
        out_shape=out_shapes, compiler_params=_params(("parallel",) * (len(grid) - 1) + ("arbitrary",), blk),
    )(lhs, *rhs_list, *extra)


def _tk(T):
    return _row_tile(T, 1408, 128)


def _wgrad_cols(n, rhs_list, name, after=None):
    T, D = n.shape
    Ns = rhs_list[0].shape[1] // N_CHIPS
    tk = _tk(T)
    nr = len(rhs_list)
    blk = _nbytes((tk, D // 2), BF16) + nr * (_nbytes((tk, Ns), BF16) + 2 * _nbytes((D // 2, Ns), F32))
    return _tn_call(
        name, (N_CHIPS, 2, T // tk), n, pl.BlockSpec((tk, D // 2), lambda j, m, k: (k, m)),
        rhs_list, [pl.BlockSpec((tk, Ns), lambda j, m, k: (k, j))] * nr,
        [jax.ShapeDtypeStruct((N_CHIPS, 2, D // 2, Ns), F32)] * nr,
        [pl.BlockSpec((None, None, D // 2, Ns), lambda j, m, k: (j, m, 0, 0))] * nr, blk, after)


def _wgrad_down(a, dfb, name):
    T, F = a.shape
    D = dfb.shape[1]
    Fs = F // N_CHIPS
    tk = _tk(T)
    tn = D // 2
    blk = _nbytes((tk, Fs), BF16) + _nbytes((tk, tn), BF16) + 2 * _nbytes((Fs, tn), F32)
    return _tn_call(
        name, (N_CHIPS, D // tn, T // tk), a, pl.BlockSpec((tk, Fs), lambda j, n, k: (k, j)),
        [dfb], [pl.BlockSpec((tk, tn), lambda j, n, k: (k, n))],
        [jax.ShapeDtypeStruct((N_CHIPS, 2, Fs // 2, D), F32)],
        [pl.BlockSpec((None, 2, Fs // 2, tn), lambda j, n, k: (j, 0, 0, n))], blk)[0]


def _wgrad_out(y, dmb):
    T, D = y.shape
    tk = _tk(T)
    tn = D // 2
    rows = D // (2 * N_CHIPS)
    blk = _nbytes((tk, D // 2), BF16) + _nbytes((tk, tn), BF16) + 2 * _nbytes((D // 2, tn), F32)
    return _tn_call(
        "wgrad_w_out", (2, D // tn, T // tk), y, pl.BlockSpec((tk, D // 2), lambda m, n, k: (k, m)),
        [dmb], [pl.BlockSpec((tk, tn), lambda m, n, k: (k, n))],
        [jax.ShapeDtypeStruct((N_CHIPS, 2, rows, D), F32)],
        [pl.BlockSpec((2, 2, rows, tn), lambda m, n, k: (m, 0, 0, n))], blk)[0]


def _row_masks(i, last):
    rows = i * TT + lax.broadcasted_iota(jnp.int32, (TT, 1), 0)
    prows = i * TT - HALO + lax.broadcasted_iota(jnp.int32, (HALO, 1), 0)
    return rows >= PAD, (prows >= PAD) & (i > 0), i < last


def _conv_inputs(u, up, mask_c, mask_p, zbuf, pbuf, C1):
    b, c, v, a, g = (u[:, k * C1:(k + 1) * C1] for k in range(5))
    cp, vp, ap, gp = (up[:, k * C1:(k + 1) * C1] for k in range(1, 5))
    sg = _sigmoid(g)
    pbuf[0:HALO, :] = jnp.where(mask_p, cp * vp, 0.0)
    pbuf[HALO:, :] = jnp.where(mask_c, c * v, 0.0)
    zbuf[0:HALO, :] = jnp.where(mask_p, ap * _sigmoid(gp), 0.0)
    zbuf[HALO:, :] = jnp.where(mask_c, a * sg, 0.0)
    return b, c, v, a, sg


SUBLANES = 8
SHIFT_ROWS = TT + HALO - SUBLANES


def _shifted_scratch(C1):
    return pltpu.VMEM((SUBLANES - 1, SHIFT_ROWS, C1), F32)


def _fill_shifted(buf, sh):
    for r in range(1, SUBLANES):
        sh[r - 1] = buf[r:r + SHIFT_ROWS, :]


LANES = 128


def _window(buf, sh, lo, c0):
    if sh is None or lo % SUBLANES == 0:
        return buf[lo:lo + TT, c0:c0 + LANES]
    q, r = divmod(lo, SUBLANES)
    return sh[r - 1, q * SUBLANES:q * SUBLANES + TT, c0:c0 + LANES]


def _tap_sum(w_ref, buf, sh, starts):
    chunks = []
    for c0 in range(0, buf.shape[1], LANES):
        acc = None
        for k, lo in enumerate(starts):
            term = w_ref[k:k + 1, c0:c0 + LANES] * _window(buf, sh, lo, c0)
            acc = term if acc is None else acc + term
        chunks.append(acc)
    return jnp.concatenate(chunks, axis=1)


def _causal_conv(w_ref, buf, sh=None):
    K = w_ref.shape[0]
    return _tap_sum(w_ref, buf, sh, [HALO - (K - 1) + k for k in range(K)])


def _anticausal_conv(w_ref, buf, sh=None):
    K = w_ref.shape[0]
    return _tap_sum(w_ref, buf, sh, [K - 1 - k for k in range(K)])


def _conv_weight_sums(dw_ref, dy, buf, sh=None):
    K = dw_ref.shape[0]
    for c0 in range(0, buf.shape[1], LANES):
        dyc = dy[:, c0:c0 + LANES]
        for k in range(K):
            prod = dyc * _window(buf, sh, HALO - (K - 1) + k, c0)
            dw_ref[k:k + 1, c0:c0 + LANES] += jnp.sum(prod, axis=0, keepdims=True)


def _layernorm_stats(z1):
    mu = jnp.mean(z1, axis=-1, keepdims=True)
    zc = z1 - mu
    rs = lax.rsqrt(jnp.mean(zc * zc, axis=-1, keepdims=True) + EPS)
    return zc * rs, rs


def _mixer_specs(T, DIN, C1, ksc, kcf):
    cur = pl.BlockSpec((TT, DIN), lambda i: (i, 0))
    prev = pl.BlockSpec((HALO, DIN), lambda i: (jnp.maximum(i * (TT // HALO) - 1, 0), 0))
    full = lambda r: pl.BlockSpec((r, C1), lambda i: (0, 0))
    return cur, prev, [full(ksc), full(kcf), full(1), full(1), full(1)]


def _mix_conv_fwd(u, wsc, wcf, bcf, lg, lb):
    T, DIN = u.shape
    C1 = DIN // 5
    last = T // TT - 1

    def body(u_ref, up_ref, wsc_ref, wcf_ref, bcf_ref, lg_ref, lb_ref, y_ref, zbuf, pbuf, zsh):
        i = pl.program_id(0)
        mask_c, mask_p, _ = _row_masks(i, last)
        b, _, _, _, _ = _conv_inputs(u_ref[...], up_ref[...], mask_c, mask_p, zbuf, pbuf, C1)
        _fill_shifted(zbuf, zsh)
        cs = _causal_conv(wsc_ref, pbuf)
        z1 = _causal_conv(wcf_ref, zbuf, zsh) + bcf_ref[...]
        zh, _ = _layernorm_stats(z1)
        ln = zh * lg_ref[...] + lb_ref[...]
        y_ref[:, 0:C1] = jnp.where(mask_c, b * cs, 0.0).astype(BF16)
        y_ref[:, C1:] = jnp.where(mask_c, jax.nn.silu(ln), 0.0).astype(BF16)

    cur, prev, small = _mixer_specs(T, DIN, C1, wsc.shape[0], wcf.shape[0])
    blk = _nbytes((TT + HALO, DIN), F32) + _nbytes((TT, 2 * C1), BF16) + 12 * _nbytes((TT + HALO, C1), F32)
    return _pallas(
        body, name="mix_conv_fwd", grid=(T // TT,),
        in_specs=[cur, prev] + small,
        out_specs=pl.BlockSpec((TT, 2 * C1), lambda i: (i, 0)),
        out_shape=jax.ShapeDtypeStruct((T, 2 * C1), BF16),
        scratch_shapes=[pltpu.VMEM((TT + HALO, C1), F32), pltpu.VMEM((TT + HALO, C1), F32), _shifted_scratch(C1)],
        compiler_params=_params(("arbitrary",), blk),
    )(u, u, wsc, wcf, bcf, lg, lb)


def _mix_conv_bwd1(u, dy, wsc, wcf, bcf, lg, lb):
    T, DIN = u.shape
    C1 = DIN // 5
    last = T // TT - 1

    def body(u_ref, up_ref, dy_ref, wsc_ref, wcf_ref, bcf_ref, lg_ref, lb_ref,
             dz1_ref, dcs_ref, db_ref, dlg_ref, dlb_ref, dbcf_ref, zbuf, pbuf, zsh):
        i = pl.program_id(0)
        mask_c, mask_p, _ = _row_masks(i, last)
        b, _, _, _, _ = _conv_inputs(u_ref[...], up_ref[...], mask_c, mask_p, zbuf, pbuf, C1)
        _fill_shifted(zbuf, zsh)
        cs = _causal_conv(wsc_ref, pbuf)
        z1 = _causal_conv(wcf_ref, zbuf, zsh) + bcf_ref[...]
        zh, rs = _layernorm_stats(z1)
        ln = zh * lg_ref[...] + lb_ref[...]
        dy = dy_ref[...]
        dysc = jnp.where(mask_c, dy[:, 0:C1], 0.0)
        dycf = jnp.where(mask_c, dy[:, C1:], 0.0)
        db_ref[...] = (dysc * cs).astype(BF16)
        dcs_ref[...] = dysc * b
        dl = dycf * _dsilu(ln, _sigmoid(ln))
        dzh = dl * lg_ref[...]
        dz1 = rs * (dzh - jnp.mean(dzh, axis=-1, keepdims=True) - zh * jnp.mean(dzh * zh, axis=-1, keepdims=True))
        dz1_ref[...] = dz1

        @pl.when(i == 0)
        def _():
            dlg_ref[...] = jnp.zeros_like(dlg_ref)
            dlb_ref[...] = jnp.zeros_like(dlb_ref)
            dbcf_ref[...] = jnp.zeros_like(dbcf_ref)

        dlg_ref[...] += jnp.sum(dl * zh, axis=0, keepdims=True)
        dlb_ref[...] += jnp.sum(dl, axis=0, keepdims=True)
        dbcf_ref[...] += jnp.sum(dz1, axis=0, keepdims=True)

    cur, prev, small = _mixer_specs(T, DIN, C1, wsc.shape[0], wcf.shape[0])
    tile = lambda: pl.BlockSpec((TT, C1), lambda i: (i, 0))
    vec = lambda: pl.BlockSpec((1, C1), lambda i: (0, 0))
    blk = _nbytes((TT + HALO, DIN), F32) + 4 * _nbytes((TT, C1), F32) + 16 * _nbytes((TT + HALO, C1), F32)
    return _pallas(
        body, name="mix_conv_bwd1", grid=(T // TT,),
        in_specs=[cur, prev, pl.BlockSpec((TT, 2 * C1), lambda i: (i, 0))] + small,
        out_specs=[tile(), tile(), tile(), vec(), vec(), vec()],
        out_shape=[jax.ShapeDtypeStruct((T, C1), F32), jax.ShapeDtypeStruct((T, C1), F32),
                   jax.ShapeDtypeStruct((T, C1), BF16)] + [jax.ShapeDtypeStruct((1, C1), F32)] * 3,
        scratch_shapes=[pltpu.VMEM((TT + HALO, C1), F32), pltpu.VMEM((TT + HALO, C1), F32), _shifted_scratch(C1)],
        compiler_params=_params(("arbitrary",), blk),
    )(u, u, dy, wsc, wcf, bcf, lg, lb)


def _mix_conv_bwd2(u, dz1, dcs, db, wsc, wcf):
    T, DIN = u.shape
    C1 = DIN // 5
    last = T // TT - 1
    ksc, kcf = wsc.shape[0], wcf.shape[0]

    def body(u_ref, up_ref, dz_ref, dzn_ref, dc_ref, dcn_ref, db_ref, wsc_ref, wcf_ref,
             du_ref, dbin_ref, dwsc_ref, dwcf_ref, zbuf, pbuf, dzbuf, dcbuf, zsh, dzsh):
        i = pl.program_id(0)
        mask_c, mask_p, has_next = _row_masks(i, last)
        _, c, v, a, sg = _conv_inputs(u_ref[...], up_ref[...], mask_c, mask_p, zbuf, pbuf, C1)
        dz1 = dz_ref[...]
        dcs = dc_ref[...]
        dzbuf[0:TT, :] = dz1
        dzbuf[TT:, :] = jnp.where(has_next, dzn_ref[...], 0.0)
        dcbuf[0:TT, :] = dcs
        dcbuf[TT:, :] = jnp.where(has_next, dcn_ref[...], 0.0)

        @pl.when(i == 0)
        def _():
            dbin_ref[...] = jnp.zeros_like(dbin_ref)
            dwsc_ref[...] = jnp.zeros_like(dwsc_ref)
            dwcf_ref[...] = jnp.zeros_like(dwcf_ref)

        _fill_shifted(zbuf, zsh)
        _fill_shifted(dzbuf, dzsh)
        _conv_weight_sums(dwcf_ref, dz1, zbuf, zsh)
        _conv_weight_sums(dwsc_ref, dcs, pbuf)
        dz0 = jnp.where(mask_c, _anticausal_conv(wcf_ref, dzbuf, dzsh), 0.0)
        dp = jnp.where(mask_c, _anticausal_conv(wsc_ref, dcbuf), 0.0)
        parts = (db_ref[...].astype(F32), dp * v, dp * c, dz0 * sg, dz0 * a * sg * (1.0 - sg))
        for k, part in enumerate(parts):
            du_ref[:, k * C1:(k + 1) * C1] = part.astype(BF16)
            dbin_ref[:, k * C1:(k + 1) * C1] += jnp.sum(part, axis=0, keepdims=True)

    cur, prev, small = _mixer_specs(T, DIN, C1, ksc, kcf)
    tile = lambda: pl.BlockSpec((TT, C1), lambda i: (i, 0))
    nxt = lambda: pl.BlockSpec((HALO, C1), lambda i: (jnp.minimum((i + 1) * (TT // HALO), T // HALO - 1), 0))
    blk = (_nbytes((TT + HALO, DIN), F32) + _nbytes((TT, DIN), BF16) + 5 * _nbytes((TT, C1), F32)
           + 16 * _nbytes((TT + HALO, C1), F32))
    buf = lambda: pltpu.VMEM((TT + HALO, C1), F32)
    return _pallas(
        body, name="mix_conv_bwd2", grid=(T // TT,),
        in_specs=[cur, prev, tile(), nxt(), tile(), nxt(), tile(), small[0], small[1]],
        out_specs=[pl.BlockSpec((TT, DIN), lambda i: (i, 0)), pl.BlockSpec((1, DIN), lambda i: (0, 0)),
                   pl.BlockSpec((ksc, C1), lambda i: (0, 0)), pl.BlockSpec((kcf, C1), lambda i: (0, 0))],
        out_shape=[jax.ShapeDtypeStruct((T, DIN), BF16), jax.ShapeDtypeStruct((1, DIN), F32),
                   jax.ShapeDtypeStruct((ksc, C1), F32), jax.ShapeDtypeStruct((kcf, C1), F32)],
        scratch_shapes=[buf(), buf(), buf(), buf(), _shifted_scratch(C1), _shifted_scratch(C1)],
        compiler_params=_params(("arbitrary",), blk),
    )(u, u, dz1, dz1, dcs, dcs, db, wsc, wcf)


def _place():
    x, y, c = lax.axis_index("x"), lax.axis_index("y"), lax.axis_index("c")
    chips = [(1 - x, y), (x, 1 - y), (1 - x, 1 - y)]
    return x, y, c, chips


ANY = pl.BlockSpec(memory_space=pl.ANY)


def _cast_own_block(place, w, name):
    R, C = w.shape
    tr = _row_tile(R // 2, 256, 16)
    nblk = R // 2 // tr

    def body(place_ref, w_ref, o_ref):
        o_ref[...] = w_ref[...].astype(BF16)

    return _pallas(
        body, name=name,
        grid_spec=pltpu.PrefetchScalarGridSpec(
            num_scalar_prefetch=1, grid=(2, nblk),
            in_specs=[pl.BlockSpec((tr, C), lambda h, i, p: (h * nblk + i, 0))],
            out_specs=pl.BlockSpec((None, None, tr, C), lambda h, i, p: (p[0], h, i, 0))),
        out_shape=jax.ShapeDtypeStruct((N_CHIPS, 2, R // 2, C), BF16),
        compiler_params=_params(("parallel", "parallel"), _nbytes((tr, C), F32) + _nbytes((tr, C), BF16)),
    )(place, w)


def _gather_weights(bufs, after):
    nw = len(bufs)

    def body(*refs):
        o_refs = refs[nw + 1:2 * nw + 1]
        send, recv = refs[2 * nw + 1:]
        x, y, c, chips = _place()
        s = 2 * x + y
        sib = (x, y, 1 - c)

        def remote(w, k, blk, half, to):
            ref = o_refs[w].at[blk, half]
            return pltpu.make_async_remote_copy(src_ref=ref, dst_ref=ref, send_sem=send.at[6 * w + k],
                                                recv_sem=recv.at[6 * w + k], device_id=to, device_id_type=MESH)

        sends = []
        for w in range(nw):
            for r, (tx, ty) in enumerate(chips):
                cp = remote(w, r, s, c, (tx, ty, c))
                cp.start()
                sends.append(cp)
        for w in range(nw):
            for r, (tx, ty) in enumerate(chips):
                sr = 2 * tx + ty
                remote(w, r, sr, c, (tx, ty, c)).wait_recv()
                cp = remote(w, 3 + r, sr, c, sib)
                cp.start()
                sends.append(cp)
        for w in range(nw):
            for r, (tx, ty) in enumerate(chips):
                remote(w, 3 + r, 2 * tx + ty, 1 - c, sib).wait_recv()
        for cp in sends:
            cp.wait_send()

    return _pallas(
        body, name="gather_weights", in_specs=[ANY] * (nw + 1), out_specs=[ANY] * nw,
        out_shape=[jax.ShapeDtypeStruct(b.shape, b.dtype) for b in bufs],
        input_output_aliases={w: w for w in range(nw)},
        scratch_shapes=[pltpu.SemaphoreType.DMA((6 * nw,)), pltpu.SemaphoreType.DMA((6 * nw,))],
    )(*bufs, after)


HBM = pl.BlockSpec(memory_space=pltpu.HBM)
SEM = pl.BlockSpec(memory_space=pltpu.SEMAPHORE)
EFFECT = pltpu.SideEffectType.DATAFLOW_SIDE_EFFECTING


def _gather_copies(refs, send, recv):
    x, y, c, chips = _place()
    s = 2 * x + y
    return [pltpu.make_async_remote_copy(src_ref=ref.at[s, c], dst_ref=ref.at[s, c], send_sem=send.at[3 * w + r],
                                         recv_sem=recv.at[3 * w + r], device_id=(tx, ty, c), device_id_type=MESH)
            for w, ref in enumerate(refs) for r, (tx, ty) in enumerate(chips)]


def _scatter_copies(refs, send, recv):
    x, y, c, chips = _place()
    nw = len(refs) // 2
    return [pltpu.make_async_remote_copy(src_ref=refs[w].at[2 * tx + ty], dst_ref=refs[nw + w].at[r],
                                         send_sem=send.at[3 * w + r], recv_sem=recv.at[3 * w + r],
                                         device_id=(tx, ty, c), device_id_type=MESH)
            for w in range(nw) for r, (tx, ty) in enumerate(chips)]


def _pair_copies(refs, send, recv):
    x, y, c, _ = _place()
    nw = len(refs) // 2
    return [pltpu.make_async_remote_copy(src_ref=refs[w].at[j, 1 - c], dst_ref=refs[nw + w].at[j],
                                         send_sem=send.at[N_CHIPS * w + j], recv_sem=recv.at[N_CHIPS * w + j],
                                         device_id=(x, y, 1 - c), device_id_type=MESH)
            for w in range(nw) for j in range(N_CHIPS)]


def _start_copies(bufs, after, ncopies, make_copies, name):
    n = len(bufs)

    def body(*refs):
        in_refs, send, recv, token = refs[:n], refs[n + 1], refs[n + 2], refs[2 * n + 3]
        for cp in make_copies(in_refs, send, recv):
            cp.start()
        token[...] = jnp.zeros_like(token)

    outs = _pallas(
        body, name=name, in_specs=[HBM] * n + [ANY],
        out_specs=[SEM, SEM] + [HBM] * n + [pl.BlockSpec(memory_space=pltpu.VMEM)],
        out_shape=[pltpu.SemaphoreType.DMA((ncopies,)), pltpu.SemaphoreType.DMA((ncopies,))]
                  + [pltpu.HBM(b.shape, b.dtype) for b in bufs] + [jax.ShapeDtypeStruct((8, 128), F32)],
        input_output_aliases={k: 2 + k for k in range(n)},
        compiler_params=pltpu.CompilerParams(has_side_effects=EFFECT),
    )(*[pltpu.with_memory_space_constraint(b, pltpu.HBM) for b in bufs], after)
    return outs[0], outs[1], list(outs[2:2 + n]), outs[2 + n]


def _wait_copies(send, recv, bufs, after, make_copies, name):
    n = len(bufs)

    def body(*refs):
        in_refs, send_ref, recv_ref = refs[:n], refs[n], refs[n + 1]
        for cp in make_copies(in_refs, send_ref, recv_ref):
            cp.wait_send()
            cp.wait_recv()

    outs = _pallas(
        body, name=name, in_specs=[HBM] * n + [SEM, SEM, ANY], out_specs=[HBM] * n,
        out_shape=[pltpu.HBM(b.shape, b.dtype) for b in bufs],
        input_output_aliases={k: k for k in range(n)},
        compiler_params=pltpu.CompilerParams(has_side_effects=EFFECT),
    )(*bufs, send, recv, after)
    return list(outs)


def _forward_halves(bufs, name):
    nw = len(bufs)

    def body(*refs):
        o_refs = refs[nw:2 * nw]
        send, recv = refs[2 * nw:]
        x, y, c, chips = _place()
        sib = (x, y, 1 - c)
        copies = []
        for w in range(nw):
            for r, (tx, ty) in enumerate(chips):
                ref = o_refs[w].at[2 * tx + ty, c]
                cp = pltpu.make_async_remote_copy(src_ref=ref, dst_ref=ref, send_sem=send.at[3 * w + r],
                                                  recv_sem=recv.at[3 * w + r], device_id=sib, device_id_type=MESH)
                cp.start()
                copies.append(cp)
        for w in range(nw):
            for r, (tx, ty) in enumerate(chips):
                ref = o_refs[w].at[2 * tx + ty, 1 - c]
                pltpu.make_async_remote_copy(src_ref=ref, dst_ref=ref, send_sem=send.at[3 * w + r],
                                             recv_sem=recv.at[3 * w + r], device_id=sib, device_id_type=MESH).wait_recv()
        for cp in copies:
            cp.wait_send()

    return _pallas(
        body, name=name, in_specs=[ANY] * nw, out_specs=[ANY] * nw,
        out_shape=[jax.ShapeDtypeStruct(b.shape, b.dtype) for b in bufs],
        input_output_aliases={w: w for w in range(nw)},
        scratch_shapes=[pltpu.SemaphoreType.DMA((3 * nw,)), pltpu.SemaphoreType.DMA((3 * nw,))],
    )(*bufs)


def _half_exchange(hs, name):
    nw = len(hs)

    def body(*refs):
        o_refs = refs[nw:2 * nw]
        send, recv = refs[2 * nw:]
        x, y, c, _ = _place()
        sib = (x, y, 1 - c)
        copies = []
        for w in range(nw):
            cp = pltpu.make_async_remote_copy(src_ref=o_refs[w].at[c], dst_ref=o_refs[w].at[c], send_sem=send.at[w],
                                              recv_sem=recv.at[w], device_id=sib, device_id_type=MESH)
            cp.start()
            copies.append(cp)
        for w, cp in enumerate(copies):
            cp.wait_send()
            pltpu.make_async_remote_copy(src_ref=o_refs[w].at[c], dst_ref=o_refs[w].at[1 - c], send_sem=send.at[w],
                                         recv_sem=recv.at[w], device_id=sib, device_id_type=MESH).wait_recv()

    return _pallas(
        body, name=name, in_specs=[ANY] * nw, out_specs=[ANY] * nw,
        out_shape=[jax.ShapeDtypeStruct(h.shape, F32) for h in hs],
        input_output_aliases={w: w for w in range(nw)},
        scratch_shapes=[pltpu.SemaphoreType.DMA((nw,)), pltpu.SemaphoreType.DMA((nw,))],
    )(*hs)


def _share_small(v, reduce, name):
    R, C = v.shape

    def body(v_ref, o_ref, *scratch):
        if reduce:
            all_ref, send, recv, lsem = scratch
        else:
            all_ref = o_ref
            send, recv, lsem = scratch
        x, y, c, _ = _place()
        me = 4 * x + 2 * y + c
        loc = pltpu.make_async_copy(v_ref, all_ref.at[me], lsem)
        loc.start()
        copies = []
        for k in range(1, N_DEV):
            kx, ky, kc = (k >> 2) & 1, (k >> 1) & 1, k & 1
            peer = (x ^ kx, y ^ ky, c ^ kc)
            cp = pltpu.make_async_remote_copy(src_ref=v_ref, dst_ref=all_ref.at[me], send_sem=send.at[k - 1],
                                              recv_sem=recv.at[k - 1], device_id=peer, device_id_type=MESH)
            cp.start()
            copies.append(cp)
        for k in range(1, N_DEV):
            kx, ky, kc = (k >> 2) & 1, (k >> 1) & 1, k & 1
            src = 4 * (x ^ kx) + 2 * (y ^ ky) + (c ^ kc)
            pltpu.make_async_remote_copy(src_ref=v_ref, dst_ref=all_ref.at[src], send_sem=send.at[k - 1],
                                         recv_sem=recv.at[k - 1], device_id=(x, y, c), device_id_type=MESH).wait_recv()
        for cp in copies:
            cp.wait_send()
        loc.wait()
        if reduce:
            total = all_ref[0]
            for d in range(1, N_DEV):
                total = total + all_ref[d]
            o_ref[...] = total

    vm = pl.BlockSpec(memory_space=pltpu.VMEM)
    sems = [pltpu.SemaphoreType.DMA((N_DEV - 1,)), pltpu.SemaphoreType.DMA((N_DEV - 1,)), pltpu.SemaphoreType.DMA]
    if reduce:
        out_shape = jax.ShapeDtypeStruct((R, C), F32)
        scratch = [pltpu.VMEM((N_DEV, R, C), F32)] + sems
    else:
        out_shape = jax.ShapeDtypeStruct((N_DEV, R, C), F32)
        scratch = sems
    return _pallas(
        body, name=name, in_specs=[vm], out_specs=vm, out_shape=out_shape, scratch_shapes=scratch,
        compiler_params=pltpu.CompilerParams(vmem_limit_bytes=int(min(4 * N_DEV * R * C * 4 + 2 ** 24, 2 ** 25 + 2 ** 24))),
    )(v)


def _pair_sum(place, g, rb, name):
    _, _, Rh, C = g.shape
    tr = _row_tile(Rh, 256, 16)

    def body(place_ref, g_ref, r_ref, q_ref):
        q_ref[...] = (g_ref[...] + r_ref[...]).astype(BF16)

    blk = 2 * _nbytes((tr, C), F32) + _nbytes((tr, C), BF16)
    return _pallas(
        body, name=name,
        grid_spec=pltpu.PrefetchScalarGridSpec(
            num_scalar_prefetch=1, grid=(N_CHIPS - 1, Rh // tr),
            in_specs=[pl.BlockSpec((None, None, tr, C), lambda j, i, p: (p[0] ^ (j + 1), p[1], i, 0)),
                      pl.BlockSpec((None, tr, C), lambda j, i, p: (p[0] ^ (j + 1), i, 0))],
            out_specs=pl.BlockSpec((None, tr, C), lambda j, i, p: (p[0] ^ (j + 1), i, 0))),
        out_shape=jax.ShapeDtypeStruct((N_CHIPS, Rh, C), BF16),
        compiler_params=_params(("parallel", "parallel"), blk),
    )(place, g, rb)


def _chip_sum(place, g, rb, rc, name):
    _, _, Rh, C = g.shape
    tr = _row_tile(Rh, 256, 16)

    def body(place_ref, g_ref, r_ref, rc_ref, o_ref):
        total = g_ref[...] + r_ref[...]
        for r in range(3):
            total = total + rc_ref[r].astype(F32)
        o_ref[...] = total

    blk = 3 * _nbytes((tr, C), F32) + 3 * _nbytes((tr, C), BF16)
    return _pallas(
        body, name=name,
        grid_spec=pltpu.PrefetchScalarGridSpec(
            num_scalar_prefetch=1, grid=(Rh // tr,),
            in_specs=[pl.BlockSpec((None, None, tr, C), lambda i, p: (p[0], p[1], i, 0)),
                      pl.BlockSpec((None, tr, C), lambda i, p: (p[0], i, 0)),
                      pl.BlockSpec((3, tr, C), lambda i, p: (0, i, 0))],
            out_specs=pl.BlockSpec((None, tr, C), lambda i, p: (p[1], i, 0))),
        out_shape=jax.ShapeDtypeStruct((2, Rh, C), F32),
        compiler_params=_params(("parallel",), blk),
    )(place, g, rb, rc)


def _adamw_math(w, g, m, v):
    m = ADAM_B1 * m + (1.0 - ADAM_B1) * g
    v = ADAM_B2 * v + (1.0 - ADAM_B2) * jnp.square(g)
    m_hat = m / (1.0 - ADAM_B1 ** ADAM_STEP)
    v_hat = v / (1.0 - ADAM_B2 ** ADAM_STEP)
    delta = -ADAM_LR * (m_hat / (jnp.sqrt(v_hat) + ADAM_EPS) + ADAM_WD * w)
    return delta, m, v


def _adamw(w, g, m, v, name):
    R, C = w.shape
    tr = _row_tile(R, 256)

    def body(w_ref, g_ref, m_ref, v_ref, go_ref, d_ref, nm_ref, nv_ref):
        gv = g_ref[...]
        d, nm, nv = _adamw_math(w_ref[...], gv, m_ref[...], v_ref[...])
        go_ref[...] = gv
        d_ref[...] = d
        nm_ref[...] = nm
        nv_ref[...] = nv

    spec = pl.BlockSpec((tr, C), lambda i: (i, 0))
    shp = jax.ShapeDtypeStruct((R, C), F32)
    return _pallas(
        body, name=name, grid=(R // tr,), in_specs=[spec] * 4, out_specs=[spec] * 4, out_shape=[shp] * 4,
        compiler_params=_params(("parallel",), 8 * _nbytes((tr, C), F32)),
    )(w, g, m, v)


def _adamw_small(ws, gs, ms, vs):
    n = len(ws)

    def body(*refs):
        for k in range(n):
            w_ref, g_ref, m_ref, v_ref = (refs[q * n + k] for q in range(4))
            d, nm, nv = _adamw_math(w_ref[...], g_ref[...], m_ref[...], v_ref[...])
            refs[4 * n + k][...] = d
            refs[5 * n + k][...] = nm
            refs[6 * n + k][...] = nv

    vm = pl.BlockSpec(memory_space=pltpu.VMEM)
    shapes = [jax.ShapeDtypeStruct(w.shape, F32) for w in ws]
    outs = _pallas(
        body, name="adamw_small", in_specs=[vm] * (4 * n), out_specs=[vm] * (3 * n), out_shape=shapes * 3,
    )(*ws, *gs, *ms, *vs)
    return outs[:n], outs[n:2 * n], outs[2 * n:]


def _pad_rows(a, rows):
    return jnp.pad(a, ((0, rows - a.shape[0]), (0, 0)))


def kernel(x, meta_tokens, ffn1_norm, ffn1_w_gate, ffn1_w_up, ffn1_w_down, mix_norm, w_in, b_in, conv_sc_w, conv_cf_w, conv_cf_b, ln_cf_g, ln_cf_b, w_out, ffn2_norm, ffn2_w_gate, ffn2_w_up, ffn2_w_down, final_norm, loss_target, m_meta_tokens, m_ffn1_norm, m_ffn1_w_gate, m_ffn1_w_up, m_ffn1_w_down, m_mix_norm, m_w_in, m_b_in, m_conv_sc_w, m_conv_cf_w, m_conv_cf_b, m_ln_cf_g, m_ln_cf_b, m_w_out, m_ffn2_norm, m_ffn2_w_gate, m_ffn2_w_up, m_ffn2_w_down, m_final_norm, v_meta_tokens, v_ffn1_norm, v_ffn1_w_gate, v_ffn1_w_up, v_ffn1_w_down, v_mix_norm, v_w_in, v_b_in, v_conv_sc_w, v_conv_cf_w, v_conv_cf_b, v_ln_cf_g, v_ln_cf_b, v_w_out, v_ffn2_norm, v_ffn2_w_gate, v_ffn2_w_up, v_ffn2_w_down, v_final_norm):
    xi, yi, ci = lax.axis_index("x"), lax.axis_index("y"), lax.axis_index("c")
    chip = 2 * xi + yi
    place = jnp.stack([chip, ci]).astype(jnp.int32)

    x2 = x[0]
    tgt = loss_target[0]
    S, D = x2.shape
    C1 = D // 2
    cs = conv_sc_w.shape[2]
    ksc, kcf = conv_sc_w.shape[1], conv_cf_w.shape[1]
    ms = meta_tokens.shape[1]

    rows_small = N_META + 8 + 32
    assert ksc <= 8 and kcf <= 32 and cs <= ms
    pack = jnp.concatenate([
        meta_tokens,
        jnp.pad(conv_sc_w[0], ((0, 8 - ksc), (0, ms - cs))),
        jnp.pad(conv_cf_w[0], ((0, 32 - kcf), (0, ms - cs)))], axis=0)
    everyone = _share_small(pack, False, "share_params")[0::2]
    meta_full = jnp.transpose(everyone[:, :N_META, :], (1, 0, 2)).reshape(N_META, D)
    wsc_full = jnp.transpose(everyone[:, N_META:N_META + ksc, :cs], (1, 0, 2)).reshape(ksc, C1)
    wcf_full = jnp.transpose(everyone[:, N_META + 8:N_META + 8 + kcf, :cs], (1, 0, 2)).reshape(kcf, C1)

    big = {"ffn1_w_gate": ffn1_w_gate, "ffn1_w_up": ffn1_w_up, "ffn1_w_down": ffn1_w_down, "w_in": w_in, "w_out": w_out,
           "ffn2_w_gate": ffn2_w_gate, "ffn2_w_up": ffn2_w_up, "ffn2_w_down": ffn2_w_down}
    big_m = {"ffn1_w_gate": m_ffn1_w_gate, "ffn1_w_up": m_ffn1_w_up, "ffn1_w_down": m_ffn1_w_down, "w_in": m_w_in,
             "w_out": m_w_out, "ffn2_w_gate": m_ffn2_w_gate, "ffn2_w_up": m_ffn2_w_up, "ffn2_w_down": m_ffn2_w_down}
    big_v = {"ffn1_w_gate": v_ffn1_w_gate, "ffn1_w_up": v_ffn1_w_up, "ffn1_w_down": v_ffn1_w_down, "w_in": v_w_in,
             "w_out": v_w_out, "ffn2_w_gate": v_ffn2_w_gate, "ffn2_w_up": v_ffn2_w_up, "ffn2_w_down": v_ffn2_w_down}
    buf = {nm: _cast_own_block(place, w[0], "cast_" + nm) for nm, w in big.items()}
    whole_weight = lambda g: g.reshape(N_CHIPS, 2 * g.shape[2], g.shape[3])
    group_mix, group_ffn2 = ["w_in", "w_out"], ["ffn2_w_gate", "ffn2_w_up", "ffn2_w_down"]

    corner = lambda a: a.reshape(-1, a.shape[-1])[:8, :128]
    wg1, wu1, wd1 = (whole_weight(g) for g in _gather_weights(
        [buf[nm] for nm in ["ffn1_w_gate", "ffn1_w_up", "ffn1_w_down"]], corner(everyone)))
    send_mix, recv_mix, thru_mix, token_mix = _start_copies(
        [buf[nm] for nm in group_mix], corner(wd1), 3 * len(group_mix), _gather_copies, "gather_start_mix")
    send_ffn2, recv_ffn2, thru_ffn2, token_ffn2 = _start_copies(
        [buf[nm] for nm in group_ffn2], token_mix, 3 * len(group_ffn2), _gather_copies, "gather_start_ffn2")
    F = N_CHIPS * wd1.shape[1]

    hs0, n1 = _embed_rms(x2, meta_full, ffn1_norm)
    g1, u1, a1 = _ffn_up(n1, wg1, wu1, token_ffn2, "ffn1_up")
    hs1 = _ffn_down(a1, wd1.reshape(F, D), hs0, "ffn1_down")
    arrived = _wait_copies(send_mix, recv_mix, thru_mix, corner(hs1), _gather_copies, "gather_wait_mix")
    win, wout = (whole_weight(g) for g in _forward_halves(arrived, "gather_forward_mix"))
    n2 = _rms(hs1, mix_norm, "rms_mix")
    u = _mix_in(n2, win, b_in)
    y = _mix_conv_fwd(u, wsc_full, wcf_full, conv_cf_b, ln_cf_g, ln_cf_b)
    hs2 = _mix_out(y, wout.reshape(D, D), hs1)
    arrived = _wait_copies(send_ffn2, recv_ffn2, thru_ffn2, corner(hs2), _gather_copies, "gather_wait_ffn2")
    wg2, wu2, wd2 = (whole_weight(g) for g in _forward_halves(arrived, "gather_forward_ffn2"))
    n3 = _rms(hs2, ffn2_norm, "rms_ffn2")
    g2, u2, a2 = _ffn_up(n3, wg2, wu2, token_ffn2, "ffn2_up")
    hs3 = _ffn_down(a2, wd2.reshape(F, D), hs2, "ffn2_down")

    def pair_start(group, after, tag):
        gs = [g for _, g in group]
        lands = [lax.empty((N_CHIPS,) + g.shape[2:], F32) for g in gs]
        send, recv, thru, token = _start_copies(gs + lands, after, N_CHIPS * len(gs), _pair_copies,
                                                "pair_start_" + tag)
        return (group, send, recv, thru, tag), token

    def scatter_start(state, after):
        group, send, recv, thru, tag = state
        thru = _wait_copies(send, recv, thru, corner(after), _pair_copies, "pair_wait_" + tag)
        gs, sib = thru[:len(group)], thru[len(group):]
        sums = [_pair_sum(place, g, rb, "pair_sum_" + nm) for (nm, _), g, rb in zip(group, gs, sib)]
        lands = [lax.empty((3,) + q.shape[1:], BF16) for q in sums]
        send, recv, thru, token = _start_copies(sums + lands, corner(sums[-1]), 3 * len(gs), _scatter_copies,
                                                "scatter_start_" + tag)
        return ([(nm, g) for (nm, _), g in zip(group, gs)], sib, send, recv, thru, tag), token

    def reduce_finish(state, after):
        group, sib, send, recv, thru, tag = state
        lands = _wait_copies(send, recv, thru, corner(after), _scatter_copies, "scatter_wait_" + tag)[len(group):]
        mine = [_chip_sum(place, g, rb, rc, "chip_sum_" + nm) for (nm, g), rb, rc in zip(group, sib, lands)]
        whole = _half_exchange(mine, "half_exchange_" + tag)
        out = {}
        for (nm, _), g in zip(group, whole):
            w = big[nm]
            g_out, d, new_m, new_v = _adamw(w[0], g.reshape(w.shape[1:]), big_m[nm][0], big_v[nm][0], "adamw_" + nm)
            out[nm] = (g_out[None], d[None], new_m[None], new_v[None])
        return out

    dhs3, df2, loss_row, d_final = _final_loss(hs3, final_norm.reshape(1, D), tgt)

    dg2, du2 = _ffn_bwd_act(df2, wd2.reshape(F, D), g2, u2, token_ffn2, "ffn2_bwd_act")
    gw_d2 = _wgrad_down(a2, df2, "wgrad_ffn2_down")
    gw_g2 = _wgrad_cols(n3, [dg2], "wgrad_ffn2_gate")[0]
    gw_u2 = _wgrad_cols(n3, [du2], "wgrad_ffn2_up")[0]
    pair_ffn2, token = pair_start([("ffn2_w_gate", gw_g2), ("ffn2_w_up", gw_u2), ("ffn2_w_down", gw_d2)],
                                  corner(gw_u2), "ffn2")
    dn3 = _nt_panel([dg2, du2], [wg2, wu2], token, "ffn2_bwd_in")
    red_ffn2, token = scatter_start(pair_ffn2, dn3)
    dhs2, dm, d_ffn2 = _rms_bwd(dn3, hs2, ffn2_norm, dhs3, 1.0, "rms_bwd_ffn2")

    dy = _nt_panel([dm], [wout.reshape(1, D, D)], token, "mix_bwd_out")
    gw_out = _wgrad_out(y, dm)
    dz1, dcs, db, d_lg, d_lb, d_bcf = _mix_conv_bwd1(u, dy, wsc_full, wcf_full, conv_cf_b, ln_cf_g, ln_cf_b)
    du, d_bin, d_wsc, d_wcf = _mix_conv_bwd2(u, dz1, dcs, db, wsc_full, wcf_full)
    gw_in = _wgrad_cols(n2, [du], "wgrad_w_in")[0]
    pair_mix, token = pair_start([("w_in", gw_in), ("w_out", gw_out)], corner(gw_in), "mix")
    dn2 = _nt_panel([du], [win], token, "mix_bwd_in")
    red_mix, token = scatter_start(pair_mix, dn2)
    dhs1, df1, d_mix = _rms_bwd(dn2, hs1, mix_norm, dhs2, FFN_RES_SCALE, "rms_bwd_mix")

    dg1, du1 = _ffn_bwd_act(df1, wd1.reshape(F, D), g1, u1, token, "ffn1_bwd_act")
    gw_d1 = _wgrad_down(a1, df1, "wgrad_ffn1_down")
    gw_g1 = _wgrad_cols(n1, [dg1], "wgrad_ffn1_gate")[0]
    pair_ffn1a, token = pair_start([("ffn1_w_down", gw_d1), ("ffn1_w_gate", gw_g1)], corner(gw_g1), "ffn1a")
    gw_u1 = _wgrad_cols(n1, [du1], "wgrad_ffn1_up", token)[0]
    red_ffn1a, token = scatter_start(pair_ffn1a, gw_u1)
    pair_ffn1b, token = pair_start([("ffn1_w_up", gw_u1)], token, "ffn1b")
    dn1 = _nt_panel([dg1, du1], [wg1, wu1], token, "ffn1_bwd_in")
    red_ffn1b, token = scatter_start(pair_ffn1b, dn1)
    grad_x, d_meta, d_ffn1 = _rms_bwd_first(dn1, hs0, ffn1_norm, dhs1, token)

    big_out = reduce_finish(red_ffn2, grad_x)
    big_out.update(reduce_finish(red_mix, big_out["ffn2_w_down"][1]))
    big_out.update(reduce_finish(red_ffn1a, big_out["w_out"][1]))
    big_out.update(reduce_finish(red_ffn1b, big_out["ffn1_w_gate"][1]))

    W = C1
    rows = lambda a: a.reshape(-1, W)
    parts = [rows(d_ffn1), rows(d_mix), rows(d_ffn2), rows(d_final), rows(d_bin), d_bcf, d_lg, d_lb,
             d_wsc, d_wcf, rows(d_meta), jnp.broadcast_to(loss_row[:, :1], (1, W))]
    sizes = [p.shape[0] for p in parts]
    total_rows = sum(sizes)
    packed = _pad_rows(jnp.concatenate(parts, axis=0), -(-total_rows // 8) * 8)
    summed = _share_small(packed, True, "sum_small")
    offs = [0]
    for n in sizes:
        offs.append(offs[-1] + n)
    piece = lambda k: summed[offs[k]:offs[k + 1]]
    loss = piece(11)[0, 0]
    g_ffn1, g_mix, g_ffn2 = (piece(k).reshape(1, D) for k in range(3))
    g_final = piece(3).reshape(1, D)
    g_bin = piece(4).reshape(1, -1)
    g_bcf, g_lg, g_lb = piece(5), piece(6), piece(7)
    g_wsc = lax.dynamic_slice_in_dim(piece(8), chip * cs, cs, axis=1)
    g_wcf = lax.dynamic_slice_in_dim(piece(9), chip * cs, cs, axis=1)
    g_meta = lax.dynamic_slice_in_dim(piece(10).reshape(N_META, D), chip * ms, ms, axis=1)

    small_names = ["meta_tokens", "ffn1_norm", "mix_norm", "b_in", "conv_sc_w", "conv_cf_w", "conv_cf_b", "ln_cf_g",
                   "ln_cf_b", "ffn2_norm", "final_norm"]
    small_w = [meta_tokens, ffn1_norm, mix_norm, b_in, conv_sc_w[0], conv_cf_w[0], conv_cf_b, ln_cf_g, ln_cf_b,
               ffn2_norm, final_norm.reshape(1, D)]
    small_g = [g_meta, g_ffn1, g_mix, g_bin, g_wsc, g_wcf, g_bcf, g_lg, g_lb, g_ffn2, g_final]
    small_m = [m_meta_tokens, m_ffn1_norm, m_mix_norm, m_b_in, m_conv_sc_w[0], m_conv_cf_w[0], m_conv_cf_b, m_ln_cf_g,
               m_ln_cf_b, m_ffn2_norm, m_final_norm.reshape(1, D)]
    small_v = [v_meta_tokens, v_ffn1_norm, v_mix_norm, v_b_in, v_conv_sc_w[0], v_conv_cf_w[0], v_conv_cf_b, v_ln_cf_g,
               v_ln_cf_b, v_ffn2_norm, v_final_norm.reshape(1, D)]
    s_d, s_m, s_v = _adamw_small(small_w, small_g, small_m, small_v)
    shapes = {"conv_sc_w": conv_sc_w.shape, "conv_cf_w": conv_cf_w.shape, "final_norm": final_norm.shape}
    small_out = {}
    for nm, g, d, m, v in zip(small_names, small_g, s_d, s_m, s_v):
        shp = shapes.get(nm, g.shape)
        small_out[nm] = tuple(t.reshape(shp) for t in (g, d, m, v))

    order = ["meta_tokens", "ffn1_norm", "ffn1_w_gate", "ffn1_w_up", "ffn1_w_down", "mix_norm", "w_in", "b_in",
             "conv_sc_w", "conv_cf_w", "conv_cf_b", "ln_cf_g", "ln_cf_b", "w_out", "ffn2_norm", "ffn2_w_gate",
             "ffn2_w_up", "ffn2_w_down", "final_norm"]
    res = {**big_out, **small_out}
    outs = [loss, grad_x[None]]
    for q in range(4):
        outs.extend(res[nm][q] for nm in order)
    return tuple(outs)
```

```python
import functools

import jax
import jax.numpy as jnp
from jax import lax
from jax.experimental import pallas as pl
from jax.experimental.pallas import tpu as pltpu

F32 = jnp.float32
BF16 = jnp.bfloat16
MESH = pl.DeviceIdType.MESH

N_META = 16
TT = 128
PAD = TT - N_META
HALO = 32
EPS = 1e-6
FFN_RES_SCALE = 0.5
N_CHIPS = 4
N_DEV = 8

ADAM_LR = 0.001
ADAM_B1 = 0.9
ADAM_B2 = 0.999
ADAM_EPS = 1e-08
ADAM_WD = 0.01
ADAM_STEP = 10

V7X_VMEM_BYTES = 64 * 2 ** 20
NT_DIMS = (((1,), (1,)), ((), ()))
TN_DIMS = (((0,), (0,)), ((), ()))


def _params(semantics, block_bytes):
    limit = min(2 * block_bytes + 16 * 2 ** 20, V7X_VMEM_BYTES - 6 * 2 ** 20)
    return pltpu.CompilerParams(dimension_semantics=semantics, vmem_limit_bytes=int(limit))


def _pallas(body, out_shape, **kw):
    if "grid" not in kw and "grid_spec" not in kw:
        return pl.pallas_call(body, out_shape=out_shape, **kw)
    big = lambda shape, dtype: jnp.issubdtype(dtype, jnp.floating) and len(shape) >= 2
    pin_out = lambda s: pltpu.HBM(s.shape, s.dtype) if big(s.shape, s.dtype) else s
    single = not isinstance(out_shape, (list, tuple))
    shapes = pin_out(out_shape) if single else [pin_out(s) for s in out_shape]
    call = pl.pallas_call(body, out_shape=shapes, **kw)
    pin = lambda a: pltpu.with_memory_space_constraint(a, pltpu.HBM) if big(a.shape, a.dtype) else a
    return lambda *operands: call(*[pin(a) for a in operands])


def _nbytes(shape, dtype):
    n = 1
    for d in shape:
        if d is not None:
            n *= d
    return n * jnp.dtype(dtype).itemsize


def _row_tile(rows, target, mult=8):
    best = None
    for t in range(mult, min(rows, target) + 1, mult):
        if rows % t == 0:
            best = t
    assert best is not None, (rows, target, mult)
    return best


def _sigmoid(v):
    return jax.nn.sigmoid(v)


def _dsilu(v, s):
    return s * (1.0 + v * (1.0 - s))


def _embed_rms(x2, meta, gain):
    S, D = x2.shape
    T = S + TT

    def body(x_ref, meta_ref, g_ref, hs_ref, n_ref):
        i = pl.program_id(0)

        @pl.when(i == 0)
        def _():
            hs_ref[...] = jnp.zeros_like(hs_ref)
            hs_ref[PAD:, :] = meta_ref[...]

        @pl.when(i > 0)
        def _():
            hs_ref[...] = x_ref[...]

        h = hs_ref[...]
        r = lax.rsqrt(jnp.mean(h * h, axis=-1, keepdims=True) + EPS)
        n_ref[...] = ((h * r) * g_ref[...]).astype(BF16)

    blk = _nbytes((TT, D), F32) * 2 + _nbytes((TT, D), BF16)
    return _pallas(
        body, name="embed_rms", grid=(T // TT,),
        in_specs=[pl.BlockSpec((TT, D), lambda i: (jnp.maximum(i - 1, 0), 0)),
                  pl.BlockSpec((N_META, D), lambda i: (0, 0)),
                  pl.BlockSpec((1, D), lambda i: (0, 0))],
        out_specs=[pl.BlockSpec((TT, D), lambda i: (i, 0)), pl.BlockSpec((TT, D), lambda i: (i, 0))],
        out_shape=[jax.ShapeDtypeStruct((T, D), F32), jax.ShapeDtypeStruct((T, D), BF16)],
        compiler_params=_params(("parallel",), blk),
    )(x2, meta, gain)


def _rms(hs, gain, name):
    T, D = hs.shape
    te = _row_tile(T, 384)

    def body(h_ref, g_ref, n_ref):
        h = h_ref[...]
        r = lax.rsqrt(jnp.mean(h * h, axis=-1, keepdims=True) + EPS)
        n_ref[...] = ((h * r) * g_ref[...]).astype(BF16)

    blk = _nbytes((te, D), F32) + _nbytes((te, D), BF16)
    return _pallas(
        body, name=name, grid=(T // te,),
        in_specs=[pl.BlockSpec((te, D), lambda i: (i, 0)), pl.BlockSpec((1, D), lambda i: (0, 0))],
        out_specs=pl.BlockSpec((te, D), lambda i: (i, 0)),
        out_shape=jax.ShapeDtypeStruct((T, D), BF16),
        compiler_params=_params(("parallel",), blk),
    )(hs, gain)


def _rms_bwd_math(dn, h, g):
    r = lax.rsqrt(jnp.mean(h * h, axis=-1, keepdims=True) + EPS)
    xh = h * r
    dgain = jnp.sum(dn * xh, axis=0, keepdims=True)
    dxh = dn * g
    dh = r * (dxh - xh * jnp.mean(dxh * xh, axis=-1, keepdims=True))
    return dh, dgain


def _rms_bwd(dn, hs, gain, dres, scale, name):
    T, D = hs.shape
    te = _row_tile(T, 384)

    def body(dn_ref, h_ref, g_ref, dres_ref, dhs_ref, dhb_ref, dg_ref):
        dh, dgain = _rms_bwd_math(dn_ref[...], h_ref[...], g_ref[...])
        d = dres_ref[...] + dh
        dhs_ref[...] = d
        dhb_ref[...] = (scale * d).astype(BF16)

        @pl.when(pl.program_id(0) == 0)
        def _():
            dg_ref[...] = jnp.zeros_like(dg_ref)

        dg_ref[...] += dgain

    blk = _nbytes((te, D), F32) * 4 + _nbytes((te, D), BF16)
    row = lambda i: (i, 0)
    return _pallas(
        body, name=name, grid=(T // te,),
        in_specs=[pl.BlockSpec((te, D), row), pl.BlockSpec((te, D), row), pl.BlockSpec((1, D), lambda i: (0, 0)),
                  pl.BlockSpec((te, D), row)],
        out_specs=[pl.BlockSpec((te, D), row), pl.BlockSpec((te, D), row), pl.BlockSpec((1, D), lambda i: (0, 0))],
        out_shape=[jax.ShapeDtypeStruct((T, D), F32), jax.ShapeDtypeStruct((T, D), BF16),
                   jax.ShapeDtypeStruct((1, D), F32)],
        compiler_params=_params(("arbitrary",), blk),
    )(dn, hs, gain, dres)


def _rms_bwd_first(dn, hs, gain, dres, after):
    T, D = hs.shape
    S = T - TT

    def body(dn_ref, h_ref, g_ref, dres_ref, after_ref, gx_ref, gm_ref, dg_ref):
        i = pl.program_id(0)
        dh, dgain = _rms_bwd_math(dn_ref[...], h_ref[...], g_ref[...])
        d = dres_ref[...] + dh

        @pl.when(i == 0)
        def _():
            dg_ref[...] = jnp.zeros_like(dg_ref)
            gm_ref[...] = d[PAD:, :]

        @pl.when(i > 0)
        def _():
            gx_ref[...] = d

        dg_ref[...] += dgain

    blk = _nbytes((TT, D), F32) * 4
    row = lambda i: (i, 0)
    return _pallas(
        body, name="rms_bwd_ffn1", grid=(T // TT,),
        in_specs=[pl.BlockSpec((TT, D), row), pl.BlockSpec((TT, D), row), pl.BlockSpec((1, D), lambda i: (0, 0)),
                  pl.BlockSpec((TT, D), row), TOKEN],
        out_specs=[pl.BlockSpec((TT, D), lambda i: (jnp.maximum(i - 1, 0), 0)),
                   pl.BlockSpec((N_META, D), lambda i: (0, 0)), pl.BlockSpec((1, D), lambda i: (0, 0))],
        out_shape=[jax.ShapeDtypeStruct((S, D), F32), jax.ShapeDtypeStruct((N_META, D), F32),
                   jax.ShapeDtypeStruct((1, D), F32)],
        compiler_params=_params(("arbitrary",), blk),
    )(dn, hs, gain, dres, after)


def _final_loss(hs, gain, tgt):
    T, D = hs.shape

    def body(h_ref, g_ref, t_ref, dhs_ref, dhb_ref, loss_ref, dg_ref):
        i = pl.program_id(0)
        h = h_ref[...]
        g = g_ref[...]
        r = lax.rsqrt(jnp.mean(h * h, axis=-1, keepdims=True) + EPS)
        xh = h * r
        e = jnp.where(i > 0, xh * g - t_ref[...], 0.0)
        tile_loss = jnp.sum(jnp.sum(e * e, axis=1, keepdims=True), axis=0, keepdims=True) * (0.5 / D)
        dout = e * (1.0 / D)
        dgain = jnp.sum(dout * xh, axis=0, keepdims=True)
        dxh = dout * g
        d = r * (dxh - xh * jnp.mean(dxh * xh, axis=-1, keepdims=True))
        dhs_ref[...] = d
        dhb_ref[...] = (FFN_RES_SCALE * d).astype(BF16)

        @pl.when(i == 0)
        def _():
            loss_ref[...] = jnp.zeros_like(loss_ref)
            dg_ref[...] = jnp.zeros_like(dg_ref)

        loss_ref[...] += jnp.broadcast_to(tile_loss, loss_ref.shape)
        dg_ref[...] += dgain

    blk = _nbytes((TT, D), F32) * 3 + _nbytes((TT, D), BF16)
    row = lambda i: (i, 0)
    return _pallas(
        body, name="final_loss", grid=(T // TT,),
        in_specs=[pl.BlockSpec((TT, D), row), pl.BlockSpec((1, D), lambda i: (0, 0)),
                  pl.BlockSpec((TT, D), lambda i: (jnp.maximum(i - 1, 0), 0))],
        out_specs=[pl.BlockSpec((TT, D), row), pl.BlockSpec((TT, D), row),
                   pl.BlockSpec((1, 128), lambda i: (0, 0)), pl.BlockSpec((1, D), lambda i: (0, 0))],
        out_shape=[jax.ShapeDtypeStruct((T, D), F32), jax.ShapeDtypeStruct((T, D), BF16),
                   jax.ShapeDtypeStruct((1, 128), F32), jax.ShapeDtypeStruct((1, D), F32)],
        compiler_params=_params(("arbitrary",), blk),
    )(hs, gain, tgt)


MXU_COLS = 256


def _tm(T):
    return _row_tile(T, 704, 16)


def _col_chunks(n):
    return [(c, min(MXU_COLS, n - c)) for c in range(0, n, MXU_COLS)]


TOKEN = pl.BlockSpec((8, 128), lambda *_: (0, 0))


def _ffn_up(n, wg, wu, shards, prev, after, name):
    T, D = n.shape
    Fs = wg.shape[2]
    tm = _tm(T)
    nprev = 0 if prev is None else 3

    def body(shards_ref, n_ref, wg_ref, wu_ref, after_ref, *refs):
        g_ref, u_ref, a_ref = refs[nprev:]
        nn = n_ref[...]
        for c0, cw in _col_chunks(Fs):
            if 2 * cw == MXU_COLS:
                both = jnp.concatenate([wg_ref[:, c0:c0 + cw], wu_ref[:, c0:c0 + cw]], axis=1)
                gu = jnp.dot(nn, both, preferred_element_type=F32)
                g, u = gu[:, :cw], gu[:, cw:]
            else:
                g = jnp.dot(nn, wg_ref[:, c0:c0 + cw], preferred_element_type=F32)
                u = jnp.dot(nn, wu_ref[:, c0:c0 + cw], preferred_element_type=F32)
            g_ref[:, c0:c0 + cw] = g.astype(BF16)
            u_ref[:, c0:c0 + cw] = u.astype(BF16)
            a_ref[:, c0:c0 + cw] = (jax.nn.silu(g) * u).astype(BF16)

    blk = _nbytes((tm, D), BF16) + 2 * _nbytes((D, Fs), BF16) + 3 * _nbytes((tm, Fs), BF16)
    out = pl.BlockSpec((tm, Fs), lambda j, i, p: (i, p[j]))
    shp = jax.ShapeDtypeStruct((T, N_CHIPS * Fs), BF16)
    return _pallas(
        body, name=name,
        grid_spec=pltpu.PrefetchScalarGridSpec(
            num_scalar_prefetch=1, grid=(shards.shape[0], T // tm),
            in_specs=[pl.BlockSpec((tm, D), lambda j, i, p: (i, 0)),
                      pl.BlockSpec((None, D, Fs), lambda j, i, p: (p[j], 0, 0)),
                      pl.BlockSpec((None, D, Fs), lambda j, i, p: (p[j], 0, 0)), TOKEN] + [ANY] * nprev,
            out_specs=[out, out, out]),
        out_shape=[shp, shp, shp], input_output_aliases={5 + q: q for q in range(nprev)},
        compiler_params=_params(("arbitrary", "arbitrary"), blk),
    )(shards, n, wg, wu, after, *(prev or ()))


def _ffn_down(a, wd, hs, shards, name):
    T, F = a.shape
    _, Fs, D = wd.shape
    tm = _tm(T)
    tn = D // 2

    def body(shards_ref, a_ref, w_ref, h_ref, o_ref):
        part = FFN_RES_SCALE * jnp.dot(a_ref[...], w_ref[...], preferred_element_type=F32)

        @pl.when(pl.program_id(2) == 0)
        def _():
            o_ref[...] = h_ref[...] + part

        @pl.when(pl.program_id(2) > 0)
        def _():
            o_ref[...] += part

    blk = _nbytes((tm, Fs), BF16) + _nbytes((Fs, tn), BF16) + 3 * _nbytes((tm, tn), F32)
    return _pallas(
        body, name=name,
        grid_spec=pltpu.PrefetchScalarGridSpec(
            num_scalar_prefetch=1, grid=(D // tn, T // tm, shards.shape[0]),
            in_specs=[pl.BlockSpec((tm, Fs), lambda n, i, k, p: (i, p[k])),
                      pl.BlockSpec((None, Fs, tn), lambda n, i, k, p: (p[k], 0, n)),
                      pl.BlockSpec((tm, tn), lambda n, i, k, p: (i, n))],
            out_specs=pl.BlockSpec((tm, tn), lambda n, i, k, p: (i, n))),
        out_shape=jax.ShapeDtypeStruct((T, D), F32),
        compiler_params=_params(("parallel", "parallel", "arbitrary"), blk),
    )(shards, a, wd, hs)


def _mix_in(n, w, b):
    T, D = n.shape
    Ns = w.shape[2]
    tm = _tm(T)

    def body(n_ref, w_ref, b_ref, u_ref):
        u_ref[...] = jnp.dot(n_ref[...], w_ref[...], preferred_element_type=F32) + b_ref[...]

    blk = _nbytes((tm, D), BF16) + _nbytes((D, Ns), BF16) + 2 * _nbytes((tm, Ns), F32)
    return _pallas(
        body, name="mix_in", grid=(N_CHIPS, T // tm),
        in_specs=[pl.BlockSpec((tm, D), lambda j, i: (i, 0)), pl.BlockSpec((None, D, Ns), lambda j, i: (j, 0, 0)),
                  pl.BlockSpec((1, Ns), lambda j, i: (0, j))],
        out_specs=pl.BlockSpec((tm, Ns), lambda j, i: (i, j)),
        out_shape=jax.ShapeDtypeStruct((T, N_CHIPS * Ns), F32),
        compiler_params=_params(("parallel", "parallel"), blk),
    )(n, w, b)


def _mix_out(y, w, hs):
    T, D = y.shape
    tm = _tm(T)

    def body(y_ref, w_ref, h_ref, o_ref):
        o_ref[...] = h_ref[...] + jnp.dot(y_ref[...], w_ref[...], preferred_element_type=F32)

    blk = _nbytes((tm, D), BF16) + _nbytes((D, D), BF16) + 3 * _nbytes((tm, D), F32)
    return _pallas(
        body, name="mix_out", grid=(T // tm,),
        in_specs=[pl.BlockSpec((tm, D), lambda i: (i, 0)), pl.BlockSpec((D, D), lambda i: (0, 0)),
                  pl.BlockSpec((tm, D), lambda i: (i, 0))],
        out_specs=pl.BlockSpec((tm, D), lambda i: (i, 0)),
        out_shape=jax.ShapeDtypeStruct((T, D), F32),
        compiler_params=_params(("parallel",), blk),
    )(y, w, hs)


def _ffn_bwd_act(dfb, wd, g, u, after, name):
    T, D = dfb.shape
    F = wd.shape[0]
    tm = _row_tile(T, 1408, 16)
    tn = 2 * MXU_COLS

    tr = _tm(tm)

    def body(d_ref, w_ref, g_ref, u_ref, after_ref, dg_ref, du_ref):
        for r0 in range(0, tm, tr):
            dv = d_ref[r0:r0 + tr, :]
            for c0, cw in _col_chunks(tn):
                da = lax.dot_general(dv, w_ref[c0:c0 + cw, :], NT_DIMS, preferred_element_type=F32)
                gv = g_ref[r0:r0 + tr, c0:c0 + cw].astype(F32)
                uv = u_ref[r0:r0 + tr, c0:c0 + cw].astype(F32)
                s = _sigmoid(gv)
                du_ref[r0:r0 + tr, c0:c0 + cw] = (da * (gv * s)).astype(BF16)
                dg_ref[r0:r0 + tr, c0:c0 + cw] = (da * uv * _dsilu(gv, s)).astype(BF16)

    blk = _nbytes((tm, D), BF16) + _nbytes((tn, D), BF16) + 4 * _nbytes((tm, tn), BF16)
    io = pl.BlockSpec((tm, tn), lambda n, i: (i, n))
    shp = jax.ShapeDtypeStruct((T, F), BF16)
    return _pallas(
        body, name=name, grid=(F // tn, T // tm),
        in_specs=[pl.BlockSpec((tm, D), lambda n, i: (i, 0)), pl.BlockSpec((tn, D), lambda n, i: (n, 0)), io, io, TOKEN],
        out_specs=[io, io], out_shape=[shp, shp],
        compiler_params=_params(("parallel", "parallel"), blk),
    )(dfb, wd, g, u, after)


def _nt_panel(lhs_list, w_list, after, name):
    T = lhs_list[0].shape[0]
    nsh, Dout, Ks = w_list[0].shape
    npair = len(lhs_list)
    tm = _row_tile(T, 1408, 16)
    tn = Dout // 2

    def body(*refs):
        l_refs, w_refs, o_ref = refs[:npair], refs[npair:2 * npair], refs[2 * npair + 1]
        j = pl.program_id(2)
        k0 = Ks - Ks % MXU_COLS if npair == 2 and 2 * (Ks % MXU_COLS) == MXU_COLS else Ks
        acc = None
        for p in range(npair):
            part = lax.dot_general(l_refs[p][:, :k0], w_refs[p][:, :k0], NT_DIMS, preferred_element_type=F32)
            acc = part if acc is None else acc + part
        if k0 < Ks:
            lhs = jnp.concatenate([l_refs[p][:, k0:] for p in range(npair)], axis=1)
            rhs = jnp.concatenate([w_refs[p][:, k0:] for p in range(npair)], axis=1)
            acc = acc + lax.dot_general(lhs, rhs, NT_DIMS, preferred_element_type=F32)

        @pl.when(j == 0)
        def _():
            o_ref[...] = acc

        @pl.when(j > 0)
        def _():
            o_ref[...] += acc

    blk = npair * (_nbytes((tm, Ks), BF16) + _nbytes((tn, Ks), BF16)) + 2 * _nbytes((tm, tn), F32)
    return _pallas(
        body, name=name, grid=(Dout // tn, T // tm, nsh),
        in_specs=[pl.BlockSpec((tm, Ks), lambda n, i, j: (i, j))] * npair
                 + [pl.BlockSpec((None, tn, Ks), lambda n, i, j: (j, n, 0))] * npair + [TOKEN],
        out_specs=pl.BlockSpec((tm, tn), lambda n, i, j: (i, n)),
        out_shape=jax.ShapeDtypeStruct((T, Dout), F32),
        compiler_params=_params(("parallel", "parallel", "arbitrary"), blk),
    )(*lhs_list, *w_list, after)


def _tn_call(name, grid, lhs, lhs_spec, rhs_list, rhs_specs, out_shapes, out_specs, blk, after=None):
    nr = len(rhs_list)
    extra = [] if after is None else [after]

    def body(*refs):
        l_ref, r_refs, o_refs = refs[0], refs[1:1 + nr], refs[len(refs) - nr:]
        k = pl.program_id(len(grid) - 1)
        lv = l_ref[...]
        for q in range(nr):
            part = lax.dot_general(lv, r_refs[q][...], TN_DIMS, preferred_element_type=F32)
            part = part.reshape(o_refs[q].shape)

            @pl.when(k == 0)
            def _(o=o_refs[q], part=part):
                o[...] = part

            @pl.when(k > 0)
            def _(o=o_refs[q], part=part):
                o[...] += part

    return _pallas(
        body, name=name, grid=grid, in_specs=[lhs_spec] + rhs_specs + [TOKEN] * len(extra), out_specs=out_specs,
        out_shape=out_shapes, compiler_params=_params(("parallel",) * (len(grid) - 1) + ("arbitrary",), blk),
    )(lhs, *rhs_list, *extra)


def _tk(T):
    return _row_tile(T, 1408, 128)


def _wgrad_cols(n, rhs_list, name, after=None):
    T, D = n.shape
    Ns = rhs_list[0].shape[1] // N_CHIPS
    tk = _tk(T)
    nr = len(rhs_list)
    blk = _nbytes((tk, D // 2), BF16) + nr * (_nbytes((tk, Ns), BF16) + 2 * _nbytes((D // 2, Ns), F32))
    return _tn_call(
        name, (N_CHIPS, 2, T // tk), n, pl.BlockSpec((tk, D // 2), lambda j, m, k: (k, m)),
        rhs_list, [pl.BlockSpec((tk, Ns), lambda j, m, k: (k, j))] * nr,
        [jax.ShapeDtypeStruct((N_CHIPS, 2, D // 2, Ns), F32)] * nr,
        [pl.BlockSpec((None, None, D // 2, Ns), lambda j, m, k: (j, m, 0, 0))] * nr, blk, after)


def _wgrad_down(a, dfb, name):
    T, F = a.shape
    D = dfb.shape[1]
    Fs = F // N_CHIPS
    tk = _tk(T)
    tn = D // 2
    blk = _nbytes((tk, Fs), BF16) + _nbytes((tk, tn), BF16) + 2 * _nbytes((Fs, tn), F32)
    return _tn_call(
        name, (N_CHIPS, D // tn, T // tk), a, pl.BlockSpec((tk, Fs), lambda j, n, k: (k, j)),
        [dfb], [pl.BlockSpec((tk, tn), lambda j, n, k: (k, n))],
        [jax.ShapeDtypeStruct((N_CHIPS, 2, Fs // 2, D), F32)],
        [pl.BlockSpec((None, 2, Fs // 2, tn), lambda j, n, k: (j, 0, 0, n))], blk)[0]


def _wgrad_out(y, dmb):
    T, D = y.shape
    tk = _tk(T)
    tn = D // 2
    rows = D // (2 * N_CHIPS)
    blk = _nbytes((tk, D // 2), BF16) + _nbytes((tk, tn), BF16) + 2 * _nbytes((D // 2, tn), F32)
    return _tn_call(
        "wgrad_w_out", (2, D // tn, T // tk), y, pl.BlockSpec((tk, D // 2), lambda m, n, k: (k, m)),
        [dmb], [pl.BlockSpec((tk, tn), lambda m, n, k: (k, n))],
        [jax.ShapeDtypeStruct((N_CHIPS, 2, rows, D), F32)],
        [pl.BlockSpec((2, 2, rows, tn), lambda m, n, k: (m, 0, 0, n))], blk)[0]


def _row_masks(i, last):
    rows = i * TT + lax.broadcasted_iota(jnp.int32, (TT, 1), 0)
    prows = i * TT - HALO + lax.broadcasted_iota(jnp.int32, (HALO, 1), 0)
    return rows >= PAD, (prows >= PAD) & (i > 0), i < last


def _conv_inputs(u, up, mask_c, mask_p, zbuf, pbuf, C1):
    b, c, v, a, g = (u[:, k * C1:(k + 1) * C1] for k in range(5))
    cp, vp, ap, gp = (up[:, k * C1:(k + 1) * C1] for k in range(1, 5))
    sg = _sigmoid(g)
    pbuf[0:HALO, :] = jnp.where(mask_p, cp * vp, 0.0)
    pbuf[HALO:, :] = jnp.where(mask_c, c * v, 0.0)
    zbuf[0:HALO, :] = jnp.where(mask_p, ap * _sigmoid(gp), 0.0)
    zbuf[HALO:, :] = jnp.where(mask_c, a * sg, 0.0)
    return b, c, v, a, sg


SUBLANES = 8
SHIFT_ROWS = TT + HALO - SUBLANES


def _shifted_scratch(C1):
    return pltpu.VMEM((SUBLANES - 1, SHIFT_ROWS, C1), F32)


def _fill_shifted(buf, sh):
    for r in range(1, SUBLANES):
        sh[r - 1] = buf[r:r + SHIFT_ROWS, :]


LANES = 128


def _window(buf, sh, lo, c0):
    if sh is None or lo % SUBLANES == 0:
        return buf[lo:lo + TT, c0:c0 + LANES]
    q, r = divmod(lo, SUBLANES)
    return sh[r - 1, q * SUBLANES:q * SUBLANES + TT, c0:c0 + LANES]


def _tap_sum(w_ref, buf, sh, starts):
    chunks = []
    for c0 in range(0, buf.shape[1], LANES):
        acc = None
        for k, lo in enumerate(starts):
            term = w_ref[k:k + 1, c0:c0 + LANES] * _window(buf, sh, lo, c0)
            acc = term if acc is None else acc + term
        chunks.append(acc)
    return jnp.concatenate(chunks, axis=1)


def _causal_conv(w_ref, buf, sh=None):
    K = w_ref.shape[0]
    return _tap_sum(w_ref, buf, sh, [HALO - (K - 1) + k for k in range(K)])


def _anticausal_conv(w_ref, buf, sh=None):
    K = w_ref.shape[0]
    return _tap_sum(w_ref, buf, sh, [K - 1 - k for k in range(K)])


def _conv_weight_sums(dw_ref, dy, buf, sh=None):
    K = dw_ref.shape[0]
    for c0 in range(0, buf.shape[1], LANES):
        dyc = dy[:, c0:c0 + LANES]
        for k in range(K):
            prod = dyc * _window(buf, sh, HALO - (K - 1) + k, c0)
            dw_ref[k:k + 1, c0:c0 + LANES] += jnp.sum(prod, axis=0, keepdims=True)


def _layernorm_stats(z1):
    mu = jnp.mean(z1, axis=-1, keepdims=True)
    zc = z1 - mu
    rs = lax.rsqrt(jnp.mean(zc * zc, axis=-1, keepdims=True) + EPS)
    return zc * rs, rs


def _mixer_specs(T, DIN, C1, ksc, kcf):
    cur = pl.BlockSpec((TT, DIN), lambda i: (i, 0))
    prev = pl.BlockSpec((HALO, DIN), lambda i: (jnp.maximum(i * (TT // HALO) - 1, 0), 0))
    full = lambda r: pl.BlockSpec((r, C1), lambda i: (0, 0))
    return cur, prev, [full(ksc), full(kcf), full(1), full(1), full(1)]


def _mix_conv_fwd(u, wsc, wcf, bcf, lg, lb):
    T, DIN = u.shape
    C1 = DIN // 5
    last = T // TT - 1

    def body(u_ref, up_ref, wsc_ref, wcf_ref, bcf_ref, lg_ref, lb_ref, y_ref, zbuf, pbuf, zsh):
        i = pl.program_id(0)
        mask_c, mask_p, _ = _row_masks(i, last)
        b, _, _, _, _ = _conv_inputs(u_ref[...], up_ref[...], mask_c, mask_p, zbuf, pbuf, C1)
        _fill_shifted(zbuf, zsh)
        cs = _causal_conv(wsc_ref, pbuf)
        z1 = _causal_conv(wcf_ref, zbuf, zsh) + bcf_ref[...]
        zh, _ = _layernorm_stats(z1)
        ln = zh * lg_ref[...] + lb_ref[...]
        y_ref[:, 0:C1] = jnp.where(mask_c, b * cs, 0.0).astype(BF16)
        y_ref[:, C1:] = jnp.where(mask_c, jax.nn.silu(ln), 0.0).astype(BF16)

    cur, prev, small = _mixer_specs(T, DIN, C1, wsc.shape[0], wcf.shape[0])
    blk = _nbytes((TT + HALO, DIN), F32) + _nbytes((TT, 2 * C1), BF16) + 12 * _nbytes((TT + HALO, C1), F32)
    return _pallas(
        body, name="mix_conv_fwd", grid=(T // TT,),
        in_specs=[cur, prev] + small,
        out_specs=pl.BlockSpec((TT, 2 * C1), lambda i: (i, 0)),
        out_shape=jax.ShapeDtypeStruct((T, 2 * C1), BF16),
        scratch_shapes=[pltpu.VMEM((TT + HALO, C1), F32), pltpu.VMEM((TT + HALO, C1), F32), _shifted_scratch(C1)],
        compiler_params=_params(("arbitrary",), blk),
    )(u, u, wsc, wcf, bcf, lg, lb)


def _mix_conv_bwd1(u, dy, wsc, wcf, bcf, lg, lb):
    T, DIN = u.shape
    C1 = DIN // 5
    last = T // TT - 1

    def body(u_ref, up_ref, dy_ref, wsc_ref, wcf_ref, bcf_ref, lg_ref, lb_ref,
             dz1_ref, dcs_ref, db_ref, dlg_ref, dlb_ref, dbcf_ref, zbuf, pbuf, zsh):
        i = pl.program_id(0)
        mask_c, mask_p, _ = _row_masks(i, last)
        b, _, _, _, _ = _conv_inputs(u_ref[...], up_ref[...], mask_c, mask_p, zbuf, pbuf, C1)
        _fill_shifted(zbuf, zsh)
        cs = _causal_conv(wsc_ref, pbuf)
        z1 = _causal_conv(wcf_ref, zbuf, zsh) + bcf_ref[...]
        zh, rs = _layernorm_stats(z1)
        ln = zh * lg_ref[...] + lb_ref[...]
        dy = dy_ref[...]
        dysc = jnp.where(mask_c, dy[:, 0:C1], 0.0)
        dycf = jnp.where(mask_c, dy[:, C1:], 0.0)
        db_ref[...] = (dysc * cs).astype(BF16)
        dcs_ref[...] = dysc * b
        dl = dycf * _dsilu(ln, _sigmoid(ln))
        dzh = dl * lg_ref[...]
        dz1 = rs * (dzh - jnp.mean(dzh, axis=-1, keepdims=True) - zh * jnp.mean(dzh * zh, axis=-1, keepdims=True))
        dz1_ref[...] = dz1

        @pl.when(i == 0)
        def _():
            dlg_ref[...] = jnp.zeros_like(dlg_ref)
            dlb_ref[...] = jnp.zeros_like(dlb_ref)
            dbcf_ref[...] = jnp.zeros_like(dbcf_ref)

        dlg_ref[...] += jnp.sum(dl * zh, axis=0, keepdims=True)
        dlb_ref[...] += jnp.sum(dl, axis=0, keepdims=True)
        dbcf_ref[...] += jnp.sum(dz1, axis=0, keepdims=True)

    cur, prev, small = _mixer_specs(T, DIN, C1, wsc.shape[0], wcf.shape[0])
    tile = lambda: pl.BlockSpec((TT, C1), lambda i: (i, 0))
    vec = lambda: pl.BlockSpec((1, C1), lambda i: (0, 0))
    blk = _nbytes((TT + HALO, DIN), F32) + 4 * _nbytes((TT, C1), F32) + 16 * _nbytes((TT + HALO, C1), F32)
    return _pallas(
        body, name="mix_conv_bwd1", grid=(T // TT,),
        in_specs=[cur, prev, pl.BlockSpec((TT, 2 * C1), lambda i: (i, 0))] + small,
        out_specs=[tile(), tile(), tile(), vec(), vec(), vec()],
        out_shape=[jax.ShapeDtypeStruct((T, C1), F32), jax.ShapeDtypeStruct((T, C1), F32),
                   jax.ShapeDtypeStruct((T, C1), BF16)] + [jax.ShapeDtypeStruct((1, C1), F32)] * 3,
        scratch_shapes=[pltpu.VMEM((TT + HALO, C1), F32), pltpu.VMEM((TT + HALO, C1), F32), _shifted_scratch(C1)],
        compiler_params=_params(("arbitrary",), blk),
    )(u, u, dy, wsc, wcf, bcf, lg, lb)


def _mix_conv_bwd2(u, dz1, dcs, db, wsc, wcf):
    T, DIN = u.shape
    C1 = DIN // 5
    last = T // TT - 1
    ksc, kcf = wsc.shape[0], wcf.shape[0]

    def body(u_ref, up_ref, dz_ref, dzn_ref, dc_ref, dcn_ref, db_ref, wsc_ref, wcf_ref,
             du_ref, dbin_ref, dwsc_ref, dwcf_ref, zbuf, pbuf, dzbuf, dcbuf, zsh, dzsh):
        i = pl.program_id(0)
        mask_c, mask_p, has_next = _row_masks(i, last)
        _, c, v, a, sg = _conv_inputs(u_ref[...], up_ref[...], mask_c, mask_p, zbuf, pbuf, C1)
        dz1 = dz_ref[...]
        dcs = dc_ref[...]
        dzbuf[0:TT, :] = dz1
        dzbuf[TT:, :] = jnp.where(has_next, dzn_ref[...], 0.0)
        dcbuf[0:TT, :] = dcs
        dcbuf[TT:, :] = jnp.where(has_next, dcn_ref[...], 0.0)

        @pl.when(i == 0)
        def _():
            dbin_ref[...] = jnp.zeros_like(dbin_ref)
            dwsc_ref[...] = jnp.zeros_like(dwsc_ref)
            dwcf_ref[...] = jnp.zeros_like(dwcf_ref)

        _fill_shifted(zbuf, zsh)
        _fill_shifted(dzbuf, dzsh)
        _conv_weight_sums(dwcf_ref, dz1, zbuf, zsh)
        _conv_weight_sums(dwsc_ref, dcs, pbuf)
        dz0 = jnp.where(mask_c, _anticausal_conv(wcf_ref, dzbuf, dzsh), 0.0)
        dp = jnp.where(mask_c, _anticausal_conv(wsc_ref, dcbuf), 0.0)
        parts = (db_ref[...].astype(F32), dp * v, dp * c, dz0 * sg, dz0 * a * sg * (1.0 - sg))
        for k, part in enumerate(parts):
            du_ref[:, k * C1:(k + 1) * C1] = part.astype(BF16)
            dbin_ref[:, k * C1:(k + 1) * C1] += jnp.sum(part, axis=0, keepdims=True)

    cur, prev, small = _mixer_specs(T, DIN, C1, ksc, kcf)
    tile = lambda: pl.BlockSpec((TT, C1), lambda i: (i, 0))
    nxt = lambda: pl.BlockSpec((HALO, C1), lambda i: (jnp.minimum((i + 1) * (TT // HALO), T // HALO - 1), 0))
    blk = (_nbytes((TT + HALO, DIN), F32) + _nbytes((TT, DIN), BF16) + 5 * _nbytes((TT, C1), F32)
           + 16 * _nbytes((TT + HALO, C1), F32))
    buf = lambda: pltpu.VMEM((TT + HALO, C1), F32)
    return _pallas(
        body, name="mix_conv_bwd2", grid=(T // TT,),
        in_specs=[cur, prev, tile(), nxt(), tile(), nxt(), tile(), small[0], small[1]],
        out_specs=[pl.BlockSpec((TT, DIN), lambda i: (i, 0)), pl.BlockSpec((1, DIN), lambda i: (0, 0)),
                   pl.BlockSpec((ksc, C1), lambda i: (0, 0)), pl.BlockSpec((kcf, C1), lambda i: (0, 0))],
        out_shape=[jax.ShapeDtypeStruct((T, DIN), BF16), jax.ShapeDtypeStruct((1, DIN), F32),
                   jax.ShapeDtypeStruct((ksc, C1), F32), jax.ShapeDtypeStruct((kcf, C1), F32)],
        scratch_shapes=[buf(), buf(), buf(), buf(), _shifted_scratch(C1), _shifted_scratch(C1)],
        compiler_params=_params(("arbitrary",), blk),
    )(u, u, dz1, dz1, dcs, dcs, db, wsc, wcf)


def _place():
    x, y, c = lax.axis_index("x"), lax.axis_index("y"), lax.axis_index("c")
    chips = [(1 - x, y), (x, 1 - y), (1 - x, 1 - y)]
    return x, y, c, chips


ANY = pl.BlockSpec(memory_space=pl.ANY)


def _cast_own_block(place, w, name):
    R, C = w.shape
    tr = _row_tile(R // 2, 256, 16)
    nblk = R // 2 // tr

    def body(place_ref, w_ref, o_ref):
        o_ref[...] = w_ref[...].astype(BF16)

    return _pallas(
        body, name=name,
        grid_spec=pltpu.PrefetchScalarGridSpec(
            num_scalar_prefetch=1, grid=(2, nblk),
            in_specs=[pl.BlockSpec((tr, C), lambda h, i, p: (h * nblk + i, 0))],
            out_specs=pl.BlockSpec((None, None, tr, C), lambda h, i, p: (p[0], h, i, 0))),
        out_shape=jax.ShapeDtypeStruct((N_CHIPS, 2, R // 2, C), BF16),
        compiler_params=_params(("parallel", "parallel"), _nbytes((tr, C), F32) + _nbytes((tr, C), BF16)),
    )(place, w)


HBM = pl.BlockSpec(memory_space=pltpu.HBM)
SEM = pl.BlockSpec(memory_space=pltpu.SEMAPHORE)
EFFECT = pltpu.SideEffectType.DATAFLOW_SIDE_EFFECTING


def _gather_copies(refs, send, recv, rels=(0, 1, 2)):
    x, y, c, chips = _place()
    s = 2 * x + y
    n = len(rels)
    return [pltpu.make_async_remote_copy(src_ref=ref.at[s, c], dst_ref=ref.at[s, c], send_sem=send.at[n * w + k],
                                         recv_sem=recv.at[n * w + k], device_id=(*chips[r], c), device_id_type=MESH)
            for w, ref in enumerate(refs) for k, r in enumerate(rels)]


def _scatter_copies(refs, send, recv):
    x, y, c, chips = _place()
    nw = len(refs) // 2
    return [pltpu.make_async_remote_copy(src_ref=refs[w].at[2 * tx + ty], dst_ref=refs[nw + w].at[r],
                                         send_sem=send.at[3 * w + r], recv_sem=recv.at[3 * w + r],
                                         device_id=(tx, ty, c), device_id_type=MESH)
            for w in range(nw) for r, (tx, ty) in enumerate(chips)]


def _pair_copies(refs, send, recv):
    x, y, c, _ = _place()
    nw = len(refs) // 2
    return [pltpu.make_async_remote_copy(src_ref=refs[w].at[j, 1 - c], dst_ref=refs[nw + w].at[j],
                                         send_sem=send.at[N_CHIPS * w + j], recv_sem=recv.at[N_CHIPS * w + j],
                                         device_id=(x, y, 1 - c), device_id_type=MESH)
            for w in range(nw) for j in range(N_CHIPS)]


def _start_copies(bufs, after, ncopies, make_copies, name):
    n = len(bufs)

    def body(*refs):
        in_refs, send, recv, token = refs[:n], refs[n + 1], refs[n + 2], refs[2 * n + 3]
        for cp in make_copies(in_refs, send, recv):
            cp.start()
        token[...] = jnp.zeros_like(token)

    outs = _pallas(
        body, name=name, in_specs=[HBM] * n + [ANY],
        out_specs=[SEM, SEM] + [HBM] * n + [pl.BlockSpec(memory_space=pltpu.VMEM)],
        out_shape=[pltpu.SemaphoreType.DMA((ncopies,)), pltpu.SemaphoreType.DMA((ncopies,))]
                  + [pltpu.HBM(b.shape, b.dtype) for b in bufs] + [jax.ShapeDtypeStruct((8, 128), F32)],
        input_output_aliases={k: 2 + k for k in range(n)},
        compiler_params=pltpu.CompilerParams(has_side_effects=EFFECT),
    )(*[pltpu.with_memory_space_constraint(b, pltpu.HBM) for b in bufs], after)
    return outs[0], outs[1], list(outs[2:2 + n]), outs[2 + n]


def _wait_copies(send, recv, bufs, after, make_copies, name):
    n = len(bufs)

    def body(*refs):
        in_refs, send_ref, recv_ref = refs[:n], refs[n], refs[n + 1]
        for cp in make_copies(in_refs, send_ref, recv_ref):
            cp.wait_send()
            cp.wait_recv()

    outs = _pallas(
        body, name=name, in_specs=[HBM] * n + [SEM, SEM, ANY], out_specs=[HBM] * n,
        out_shape=[pltpu.HBM(b.shape, b.dtype) for b in bufs],
        input_output_aliases={k: k for k in range(n)},
        compiler_params=pltpu.CompilerParams(has_side_effects=EFFECT),
    )(*bufs, send, recv, after)
    return list(outs)


def _forward_halves(bufs, name, rels=(0, 1, 2)):
    nw = len(bufs)
    n = len(rels)

    def body(*refs):
        o_refs = refs[nw:2 * nw]
        send, recv = refs[2 * nw:]
        x, y, c, chips = _place()
        sib = (x, y, 1 - c)
        copies = []
        for w in range(nw):
            for k, r in enumerate(rels):
                tx, ty = chips[r]
                ref = o_refs[w].at[2 * tx + ty, c]
                cp = pltpu.make_async_remote_copy(src_ref=ref, dst_ref=ref, send_sem=send.at[n * w + k],
                                                  recv_sem=recv.at[n * w + k], device_id=sib, device_id_type=MESH)
                cp.start()
                copies.append(cp)
        for w in range(nw):
            for k, r in enumerate(rels):
                tx, ty = chips[r]
                ref = o_refs[w].at[2 * tx + ty, 1 - c]
                pltpu.make_async_remote_copy(src_ref=ref, dst_ref=ref, send_sem=send.at[n * w + k],
                                             recv_sem=recv.at[n * w + k], device_id=sib, device_id_type=MESH).wait_recv()
        for cp in copies:
            cp.wait_send()

    return _pallas(
        body, name=name, in_specs=[ANY] * nw, out_specs=[ANY] * nw,
        out_shape=[jax.ShapeDtypeStruct(b.shape, b.dtype) for b in bufs],
        input_output_aliases={w: w for w in range(nw)},
        scratch_shapes=[pltpu.SemaphoreType.DMA((n * nw,)), pltpu.SemaphoreType.DMA((n * nw,))],
    )(*bufs)


def _half_exchange(hs, name):
    nw = len(hs)

    def body(*refs):
        o_refs = refs[nw:2 * nw]
        send, recv = refs[2 * nw:]
        x, y, c, _ = _place()
        sib = (x, y, 1 - c)
        copies = []
        for w in range(nw):
            cp = pltpu.make_async_remote_copy(src_ref=o_refs[w].at[c], dst_ref=o_refs[w].at[c], send_sem=send.at[w],
                                              recv_sem=recv.at[w], device_id=sib, device_id_type=MESH)
            cp.start()
            copies.append(cp)
        for w, cp in enumerate(copies):
            cp.wait_send()
            pltpu.make_async_remote_copy(src_ref=o_refs[w].at[c], dst_ref=o_refs[w].at[1 - c], send_sem=send.at[w],
                                         recv_sem=recv.at[w], device_id=sib, device_id_type=MESH).wait_recv()

    return _pallas(
        body, name=name, in_specs=[ANY] * nw, out_specs=[ANY] * nw,
        out_shape=[jax.ShapeDtypeStruct(h.shape, F32) for h in hs],
        input_output_aliases={w: w for w in range(nw)},
        scratch_shapes=[pltpu.SemaphoreType.DMA((nw,)), pltpu.SemaphoreType.DMA((nw,))],
    )(*hs)


def _share_small(v, reduce, name):
    R, C = v.shape

    def body(v_ref, o_ref, *scratch):
        if reduce:
            all_ref, send, recv, lsem = scratch
        else:
            all_ref = o_ref
            send, recv, lsem = scratch
        x, y, c, _ = _place()
        me = 4 * x + 2 * y + c
        loc = pltpu.make_async_copy(v_ref, all_ref.at[me], lsem)
        loc.start()
        copies = []
        for k in range(1, N_DEV):
            kx, ky, kc = (k >> 2) & 1, (k >> 1) & 1, k & 1
            peer = (x ^ kx, y ^ ky, c ^ kc)
            cp = pltpu.make_async_remote_copy(src_ref=v_ref, dst_ref=all_ref.at[me], send_sem=send.at[k - 1],
                                              recv_sem=recv.at[k - 1], device_id=peer, device_id_type=MESH)
            cp.start()
            copies.append(cp)
        for k in range(1, N_DEV):
            kx, ky, kc = (k >> 2) & 1, (k >> 1) & 1, k & 1
            src = 4 * (x ^ kx) + 2 * (y ^ ky) + (c ^ kc)
            pltpu.make_async_remote_copy(src_ref=v_ref, dst_ref=all_ref.at[src], send_sem=send.at[k - 1],
                                         recv_sem=recv.at[k - 1], device_id=(x, y, c), device_id_type=MESH).wait_recv()
        for cp in copies:
            cp.wait_send()
        loc.wait()
        if reduce:
            total = all_ref[0]
            for d in range(1, N_DEV):
                total = total + all_ref[d]
            o_ref[...] = total

    vm = pl.BlockSpec(memory_space=pltpu.VMEM)
    sems = [pltpu.SemaphoreType.DMA((N_DEV - 1,)), pltpu.SemaphoreType.DMA((N_DEV - 1,)), pltpu.SemaphoreType.DMA]
    if reduce:
        out_shape = jax.ShapeDtypeStruct((R, C), F32)
        scratch = [pltpu.VMEM((N_DEV, R, C), F32)] + sems
    else:
        out_shape = jax.ShapeDtypeStruct((N_DEV, R, C), F32)
        scratch = sems
    return _pallas(
        body, name=name, in_specs=[vm], out_specs=vm, out_shape=out_shape, scratch_shapes=scratch,
        compiler_params=pltpu.CompilerParams(vmem_limit_bytes=int(min(4 * N_DEV * R * C * 4 + 2 ** 24, 2 ** 25 + 2 ** 24))),
    )(v)


def _pair_sum(place, g, rb, name):
    _, _, Rh, C = g.shape
    tr = _row_tile(Rh, 256, 16)

    def body(place_ref, g_ref, r_ref, q_ref):
        q_ref[...] = (g_ref[...] + r_ref[...]).astype(BF16)

    blk = 2 * _nbytes((tr, C), F32) + _nbytes((tr, C), BF16)
    return _pallas(
        body, name=name,
        grid_spec=pltpu.PrefetchScalarGridSpec(
            num_scalar_prefetch=1, grid=(N_CHIPS - 1, Rh // tr),
            in_specs=[pl.BlockSpec((None, None, tr, C), lambda j, i, p: (p[0] ^ (j + 1), p[1], i, 0)),
                      pl.BlockSpec((None, tr, C), lambda j, i, p: (p[0] ^ (j + 1), i, 0))],
            out_specs=pl.BlockSpec((None, tr, C), lambda j, i, p: (p[0] ^ (j + 1), i, 0))),
        out_shape=jax.ShapeDtypeStruct((N_CHIPS, Rh, C), BF16),
        compiler_params=_params(("parallel", "parallel"), blk),
    )(place, g, rb)


def _chip_sum(place, g, rb, rc, name):
    _, _, Rh, C = g.shape
    tr = _row_tile(Rh, 256, 16)

    def body(place_ref, g_ref, r_ref, rc_ref, o_ref):
        total = g_ref[...] + r_ref[...]
        for r in range(3):
            total = total + rc_ref[r].astype(F32)
        o_ref[...] = total

    blk = 3 * _nbytes((tr, C), F32) + 3 * _nbytes((tr, C), BF16)
    return _pallas(
        body, name=name,
        grid_spec=pltpu.PrefetchScalarGridSpec(
            num_scalar_prefetch=1, grid=(Rh // tr,),
            in_specs=[pl.BlockSpec((None, None, tr, C), lambda i, p: (p[0], p[1], i, 0)),
                      pl.BlockSpec((None, tr, C), lambda i, p: (p[0], i, 0)),
                      pl.BlockSpec((3, tr, C), lambda i, p: (0, i, 0))],
            out_specs=pl.BlockSpec((None, tr, C), lambda i, p: (p[1], i, 0))),
        out_shape=jax.ShapeDtypeStruct((2, Rh, C), F32),
        compiler_params=_params(("parallel",), blk),
    )(place, g, rb, rc)


def _adamw_math(w, g, m, v):
    m = ADAM_B1 * m + (1.0 - ADAM_B1) * g
    v = ADAM_B2 * v + (1.0 - ADAM_B2) * jnp.square(g)
    m_hat = m / (1.0 - ADAM_B1 ** ADAM_STEP)
    v_hat = v / (1.0 - ADAM_B2 ** ADAM_STEP)
    delta = -ADAM_LR * (m_hat / (jnp.sqrt(v_hat) + ADAM_EPS) + ADAM_WD * w)
    return delta, m, v


def _adamw(w, g, m, v, name):
    R, C = w.shape
    tr = _row_tile(R, 256)

    def body(w_ref, g_ref, m_ref, v_ref, go_ref, d_ref, nm_ref, nv_ref):
        gv = g_ref[...]
        d, nm, nv = _adamw_math(w_ref[...], gv, m_ref[...], v_ref[...])
        go_ref[...] = gv
        d_ref[...] = d
        nm_ref[...] = nm
        nv_ref[...] = nv

    spec = pl.BlockSpec((tr, C), lambda i: (i, 0))
    shp = jax.ShapeDtypeStruct((R, C), F32)
    return _pallas(
        body, name=name, grid=(R // tr,), in_specs=[spec] * 4, out_specs=[spec] * 4, out_shape=[shp] * 4,
        compiler_params=_params(("parallel",), 8 * _nbytes((tr, C), F32)),
    )(w, g, m, v)


def _adamw_small(ws, gs, ms, vs):
    n = len(ws)

    def body(*refs):
        for k in range(n):
            w_ref, g_ref, m_ref, v_ref = (refs[q * n + k] for q in range(4))
            d, nm, nv = _adamw_math(w_ref[...], g_ref[...], m_ref[...], v_ref[...])
            refs[4 * n + k][...] = d
            refs[5 * n + k][...] = nm
            refs[6 * n + k][...] = nv

    vm = pl.BlockSpec(memory_space=pltpu.VMEM)
    shapes = [jax.ShapeDtypeStruct(w.shape, F32) for w in ws]
    outs = _pallas(
        body, name="adamw_small", in_specs=[vm] * (4 * n), out_specs=[vm] * (3 * n), out_shape=shapes * 3,
    )(*ws, *gs, *ms, *vs)
    return outs[:n], outs[n:2 * n], outs[2 * n:]


def _pad_rows(a, rows):
    return jnp.pad(a, ((0, rows - a.shape[0]), (0, 0)))


def kernel(x, meta_tokens, ffn1_norm, ffn1_w_gate, ffn1_w_up, ffn1_w_down, mix_norm, w_in, b_in, conv_sc_w, conv_cf_w, conv_cf_b, ln_cf_g, ln_cf_b, w_out, ffn2_norm, ffn2_w_gate, ffn2_w_up, ffn2_w_down, final_norm, loss_target, m_meta_tokens, m_ffn1_norm, m_ffn1_w_gate, m_ffn1_w_up, m_ffn1_w_down, m_mix_norm, m_w_in, m_b_in, m_conv_sc_w, m_conv_cf_w, m_conv_cf_b, m_ln_cf_g, m_ln_cf_b, m_w_out, m_ffn2_norm, m_ffn2_w_gate, m_ffn2_w_up, m_ffn2_w_down, m_final_norm, v_meta_tokens, v_ffn1_norm, v_ffn1_w_gate, v_ffn1_w_up, v_ffn1_w_down, v_mix_norm, v_w_in, v_b_in, v_conv_sc_w, v_conv_cf_w, v_conv_cf_b, v_ln_cf_g, v_ln_cf_b, v_w_out, v_ffn2_norm, v_ffn2_w_gate, v_ffn2_w_up, v_ffn2_w_down, v_final_norm):
    xi, yi, ci = lax.axis_index("x"), lax.axis_index("y"), lax.axis_index("c")
    chip = 2 * xi + yi
    place = jnp.stack([chip, ci]).astype(jnp.int32)

    x2 = x[0]
    tgt = loss_target[0]
    S, D = x2.shape
    C1 = D // 2
    cs = conv_sc_w.shape[2]
    ksc, kcf = conv_sc_w.shape[1], conv_cf_w.shape[1]
    ms = meta_tokens.shape[1]

    rows_small = N_META + 8 + 32
    assert ksc <= 8 and kcf <= 32 and cs <= ms
    pack = jnp.concatenate([
        meta_tokens,
        jnp.pad(conv_sc_w[0], ((0, 8 - ksc), (0, ms - cs))),
        jnp.pad(conv_cf_w[0], ((0, 32 - kcf), (0, ms - cs)))], axis=0)
    everyone = _share_small(pack, False, "share_params")[0::2]
    meta_full = jnp.transpose(everyone[:, :N_META, :], (1, 0, 2)).reshape(N_META, D)
    wsc_full = jnp.transpose(everyone[:, N_META:N_META + ksc, :cs], (1, 0, 2)).reshape(ksc, C1)
    wcf_full = jnp.transpose(everyone[:, N_META + 8:N_META + 8 + kcf, :cs], (1, 0, 2)).reshape(kcf, C1)

    big = {"ffn1_w_gate": ffn1_w_gate, "ffn1_w_up": ffn1_w_up, "ffn1_w_down": ffn1_w_down, "w_in": w_in, "w_out": w_out,
           "ffn2_w_gate": ffn2_w_gate, "ffn2_w_up": ffn2_w_up, "ffn2_w_down": ffn2_w_down}
    big_m = {"ffn1_w_gate": m_ffn1_w_gate, "ffn1_w_up": m_ffn1_w_up, "ffn1_w_down": m_ffn1_w_down, "w_in": m_w_in,
             "w_out": m_w_out, "ffn2_w_gate": m_ffn2_w_gate, "ffn2_w_up": m_ffn2_w_up, "ffn2_w_down": m_ffn2_w_down}
    big_v = {"ffn1_w_gate": v_ffn1_w_gate, "ffn1_w_up": v_ffn1_w_up, "ffn1_w_down": v_ffn1_w_down, "w_in": v_w_in,
             "w_out": v_w_out, "ffn2_w_gate": v_ffn2_w_gate, "ffn2_w_up": v_ffn2_w_up, "ffn2_w_down": v_ffn2_w_down}
    buf = {nm: _cast_own_block(place, w[0], "cast_" + nm) for nm, w in big.items()}
    whole_weight = lambda g: g.reshape(N_CHIPS, 2 * g.shape[2], g.shape[3])
    corner = lambda a: a.reshape(-1, a.shape[-1])[:8, :128]

    NEAR, FAR = (0, 1), (2,)
    groups = [("ffn1_near", ["ffn1_w_gate", "ffn1_w_up", "ffn1_w_down"], NEAR),
              ("ffn1_far", ["ffn1_w_gate", "ffn1_w_up", "ffn1_w_down"], FAR),
              ("mix", ["w_in", "w_out"], NEAR + FAR),
              ("ffn2_up", ["ffn2_w_gate", "ffn2_w_up"], NEAR + FAR),
              ("ffn2_down", ["ffn2_w_down"], NEAR + FAR)]
    started = {}
    token = corner(everyone)
    for tag, nms, rels in groups:
        copies = functools.partial(_gather_copies, rels=rels)
        send, recv, thru, token = _start_copies([buf[nm] for nm in nms], token, len(rels) * len(nms), copies,
                                                "gather_start_" + tag)
        for nm, b in zip(nms, thru):
            buf[nm] = b
        started[tag] = (send, recv, nms, rels, copies)

    def arrive(tag, after):
        send, recv, nms, rels, copies = started[tag]
        got = _wait_copies(send, recv, [buf[nm] for nm in nms], corner(after), copies, "gather_wait_" + tag)
        for nm, b in zip(nms, _forward_halves(got, "gather_forward_" + tag, rels)):
            buf[nm] = b

    ffn1 = lambda: [whole_weight(buf[nm]) for nm in ["ffn1_w_gate", "ffn1_w_up", "ffn1_w_down"]]
    own = chip[None].astype(jnp.int32)
    near = jnp.stack([chip ^ 2, chip ^ 1]).astype(jnp.int32)
    far = (chip ^ 3)[None].astype(jnp.int32)
    all_chips = jnp.arange(N_CHIPS, dtype=jnp.int32)

    hs0, n1 = _embed_rms(x2, meta_full, ffn1_norm)
    wg1, wu1, wd1 = ffn1()
    gua = _ffn_up(n1, wg1, wu1, own, None, token, "ffn1_up_own")
    hs1 = _ffn_down(gua[2], wd1, hs0, own, "ffn1_down_own")
    arrive("ffn1_near", hs1)
    wg1, wu1, wd1 = ffn1()
    gua = _ffn_up(n1, wg1, wu1, near, gua, token, "ffn1_up_near")
    hs1 = _ffn_down(gua[2], wd1, hs1, near, "ffn1_down_near")
    arrive("ffn1_far", hs1)
    wg1, wu1, wd1 = ffn1()
    g1, u1, a1 = _ffn_up(n1, wg1, wu1, far, gua, token, "ffn1_up_far")
    hs1 = _ffn_down(a1, wd1, hs1, far, "ffn1_down_far")
    F = N_CHIPS * wd1.shape[1]
    arrive("mix", hs1)
    win, wout = whole_weight(buf["w_in"]), whole_weight(buf["w_out"])
    n2 = _rms(hs1, mix_norm, "rms_mix")
    u = _mix_in(n2, win, b_in)
    y = _mix_conv_fwd(u, wsc_full, wcf_full, conv_cf_b, ln_cf_g, ln_cf_b)
    hs2 = _mix_out(y, wout.reshape(D, D), hs1)
    arrive("ffn2_up", hs2)
    wg2, wu2 = whole_weight(buf["ffn2_w_gate"]), whole_weight(buf["ffn2_w_up"])
    n3 = _rms(hs2, ffn2_norm, "rms_ffn2")
    g2, u2, a2 = _ffn_up(n3, wg2, wu2, all_chips, None, token, "ffn2_up")
    arrive("ffn2_down", a2)
    wd2 = whole_weight(buf["ffn2_w_down"])
    hs3 = _ffn_down(a2, wd2, hs2, all_chips, "ffn2_down")
    token_ffn2 = token

    def pair_start(group, after, tag):
        gs = [g for _, g in group]
        lands = [lax.empty((N_CHIPS,) + g.shape[2:], F32) for g in gs]
        send, recv, thru, token = _start_copies(gs + lands, after, N_CHIPS * len(gs), _pair_copies,
                                                "pair_start_" + tag)
        return (group, send, recv, thru, tag), token

    def scatter_start(state, after):
        group, send, recv, thru, tag = state
        thru = _wait_copies(send, recv, thru, corner(after), _pair_copies, "pair_wait_" + tag)
        gs, sib = thru[:len(group)], thru[len(group):]
        sums = [_pair_sum(place, g, rb, "pair_sum_" + nm) for (nm, _), g, rb in zip(group, gs, sib)]
        lands = [lax.empty((3,) + q.shape[1:], BF16) for q in sums]
        send, recv, thru, token = _start_copies(sums + lands, corner(sums[-1]), 3 * len(gs), _scatter_copies,
                                                "scatter_start_" + tag)
        return ([(nm, g) for (nm, _), g in zip(group, gs)], sib, send, recv, thru, tag), token

    def reduce_finish(state, after):
        group, sib, send, recv, thru, tag = state
        lands = _wait_copies(send, recv, thru, corner(after), _scatter_copies, "scatter_wait_" + tag)[len(group):]
        mine = [_chip_sum(place, g, rb, rc, "chip_sum_" + nm) for (nm, g), rb, rc in zip(group, sib, lands)]
        whole = _half_exchange(mine, "half_exchange_" + tag)
        out = {}
        for (nm, _), g in zip(group, whole):
            w = big[nm]
            g_out, d, new_m, new_v = _adamw(w[0], g.reshape(w.shape[1:]), big_m[nm][0], big_v[nm][0], "adamw_" + nm)
            out[nm] = (g_out[None], d[None], new_m[None], new_v[None])
        return out

    dhs3, df2, loss_row, d_final = _final_loss(hs3, final_norm.reshape(1, D), tgt)

    dg2, du2 = _ffn_bwd_act(df2, wd2.reshape(F, D), g2, u2, token_ffn2, "ffn2_bwd_act")
    gw_d2 = _wgrad_down(a2, df2, "wgrad_ffn2_down")
    gw_g2 = _wgrad_cols(n3, [dg2], "wgrad_ffn2_gate")[0]
    gw_u2 = _wgrad_cols(n3, [du2], "wgrad_ffn2_up")[0]
    pair_ffn2, token = pair_start([("ffn2_w_gate", gw_g2), ("ffn2_w_up", gw_u2), ("ffn2_w_down", gw_d2)],
                                  corner(gw_u2), "ffn2")
    dn3 = _nt_panel([dg2, du2], [wg2, wu2], token, "ffn2_bwd_in")
    red_ffn2, token = scatter_start(pair_ffn2, dn3)
    dhs2, dm, d_ffn2 = _rms_bwd(dn3, hs2, ffn2_norm, dhs3, 1.0, "rms_bwd_ffn2")

    dy = _nt_panel([dm], [wout.reshape(1, D, D)], token, "mix_bwd_out")
    gw_out = _wgrad_out(y, dm)
    dz1, dcs, db, d_lg, d_lb, d_bcf = _mix_conv_bwd1(u, dy, wsc_full, wcf_full, conv_cf_b, ln_cf_g, ln_cf_b)
    du, d_bin, d_wsc, d_wcf = _mix_conv_bwd2(u, dz1, dcs, db, wsc_full, wcf_full)
    gw_in = _wgrad_cols(n2, [du], "wgrad_w_in")[0]
    pair_mix, token = pair_start([("w_in", gw_in), ("w_out", gw_out)], corner(gw_in), "mix")
    dn2 = _nt_panel([du], [win], token, "mix_bwd_in")
    red_mix, token = scatter_start(pair_mix, dn2)
    dhs1, df1, d_mix = _rms_bwd(dn2, hs1, mix_norm, dhs2, FFN_RES_SCALE, "rms_bwd_mix")

    dg1, du1 = _ffn_bwd_act(df1, wd1.reshape(F, D), g1, u1, token, "ffn1_bwd_act")
    gw_d1 = _wgrad_down(a1, df1, "wgrad_ffn1_down")
    gw_g1 = _wgrad_cols(n1, [dg1], "wgrad_ffn1_gate")[0]
    pair_ffn1a, token = pair_start([("ffn1_w_down", gw_d1), ("ffn1_w_gate", gw_g1)], corner(gw_g1), "ffn1a")
    gw_u1 = _wgrad_cols(n1, [du1], "wgrad_ffn1_up", token)[0]
    red_ffn1a, token = scatter_start(pair_ffn1a, gw_u1)
    pair_ffn1b, token = pair_start([("ffn1_w_up", gw_u1)], token, "ffn1b")
    dn1 = _nt_panel([dg1, du1], [wg1, wu1], token, "ffn1_bwd_in")
    red_ffn1b, token = scatter_start(pair_ffn1b, dn1)
    grad_x, d_meta, d_ffn1 = _rms_bwd_first(dn1, hs0, ffn1_norm, dhs1, token)

    big_out = reduce_finish(red_ffn2, grad_x)
    big_out.update(reduce_finish(red_mix, big_out["ffn2_w_down"][1]))
    big_out.update(reduce_finish(red_ffn1a, big_out["w_out"][1]))
    big_out.update(reduce_finish(red_ffn1b, big_out["ffn1_w_gate"][1]))

    W = C1
    rows = lambda a: a.reshape(-1, W)
    parts = [rows(d_ffn1), rows(d_mix), rows(d_ffn2), rows(d_final), rows(d_bin), d_bcf, d_lg, d_lb,
             d_wsc, d_wcf, rows(d_meta), jnp.broadcast_to(loss_row[:, :1], (1, W))]
    sizes = [p.shape[0] for p in parts]
    total_rows = sum(sizes)
    packed = _pad_rows(jnp.concatenate(parts, axis=0), -(-total_rows // 8) * 8)
    summed = _share_small(packed, True, "sum_small")
    offs = [0]
    for n in sizes:
        offs.append(offs[-1] + n)
    piece = lambda k: summed[offs[k]:offs[k + 1]]
    loss = piece(11)[0, 0]
    g_ffn1, g_mix, g_ffn2 = (piece(k).reshape(1, D) for k in range(3))
    g_final = piece(3).reshape(1, D)
    g_bin = piece(4).reshape(1, -1)
    g_bcf, g_lg, g_lb = piece(5), piece(6), piece(7)
    g_wsc = lax.dynamic_slice_in_dim(piece(8), chip * cs, cs, axis=1)
    g_wcf = lax.dynamic_slice_in_dim(piece(9), chip * cs, cs, axis=1)
    g_meta = lax.dynamic_slice_in_dim(piece(10).reshape(N_META, D), chip * ms, ms, axis=1)

    small_names = ["meta_tokens", "ffn1_norm", "mix_norm", "b_in", "conv_sc_w", "conv_cf_w", "conv_cf_b", "ln_cf_g",
                   "ln_cf_b", "ffn2_norm", "final_norm"]
    small_w = [meta_tokens, ffn1_norm, mix_norm, b_in, conv_sc_w[0], conv_cf_w[0], conv_cf_b, ln_cf_g, ln_cf_b,
               ffn2_norm, final_norm.reshape(1, D)]
    small_g = [g_meta, g_ffn1, g_mix, g_bin, g_wsc, g_wcf, g_bcf, g_lg, g_lb, g_ffn2, g_final]
    small_m = [m_meta_tokens, m_ffn1_norm, m_mix_norm, m_b_in, m_conv_sc_w[0], m_conv_cf_w[0], m_conv_cf_b, m_ln_cf_g,
               m_ln_cf_b, m_ffn2_norm, m_final_norm.reshape(1, D)]
    small_v = [v_meta_tokens, v_ffn1_norm, v_mix_norm, v_b_in, v_conv_sc_w[0], v_conv_cf_w[0], v_conv_cf_b, v_ln_cf_g,
               v_ln_cf_b, v_ffn2_norm, v_final_norm.reshape(1, D)]
    s_d, s_m, s_v = _adamw_small(small_w, small_g, small_m, small_v)
    shapes = {"conv_sc_w": conv_sc_w.shape, "conv_cf_w": conv_cf_w.shape, "final_norm": final_norm.shape}
    small_out = {}
    for nm, g, d, m, v in zip(small_names, small_g, s_d, s_m, s_v):
        shp = shapes.get(nm, g.shape)
        small_out[nm] = tuple(t.reshape(shp) for t in (g, d, m, v))

    order = ["meta_tokens", "ffn1_norm", "ffn1_w_gate", "ffn1_w_up", "ffn1_w_down", "mix_norm", "w_in", "b_in",
             "conv_sc_w", "conv_cf_w", "conv_cf_b", "ln_cf_g", "ln_cf_b", "w_out", "ffn2_norm", "ffn2_w_gate",
             "ffn2_w_up", "ffn2_w_down", "final_norm"]
    res = {**big_out, **small_out}
    outs = [loss, grad_x[None]]
    for q in range(4):
        outs.extend(res[nm][q] for nm in order)
    return tuple(outs)
```

```python
import functools

import jax
import jax.numpy as jnp
from jax import lax
from jax.experimental import pallas as pl
from jax.experimental.pallas import tpu as pltpu

F32 = jnp.float32
BF16 = jnp.bfloat16
MESH = pl.DeviceIdType.MESH

N_META = 16
TT = 128
PAD = TT - N_META
HALO = 32
EPS = 1e-6
FFN_RES_SCALE = 0.5
N_CHIPS = 4
N_DEV = 8

ADAM_LR = 0.001
ADAM_B1 = 0.9
ADAM_B2 = 0.999
ADAM_EPS = 1e-08
ADAM_WD = 0.01
ADAM_STEP = 10

V7X_VMEM_BYTES = 64 * 2 ** 20
NT_DIMS = (((1,), (1,)), ((), ()))
TN_DIMS = (((0,), (0,)), ((), ()))


def _params(semantics, block_bytes):
    limit = min(2 * block_bytes + 16 * 2 ** 20, V7X_VMEM_BYTES - 6 * 2 ** 20)
    return pltpu.CompilerParams(dimension_semantics=semantics, vmem_limit_bytes=int(limit))


def _pallas(body, out_shape, **kw):
    if "grid" not in kw and "grid_spec" not in kw:
        return pl.pallas_call(body, out_shape=out_shape, **kw)
    big = lambda shape, dtype: jnp.issubdtype(dtype, jnp.floating) and len(shape) >= 2
    pin_out = lambda s: pltpu.HBM(s.shape, s.dtype) if big(s.shape, s.dtype) else s
    single = not isinstance(out_shape, (list, tuple))
    shapes = pin_out(out_shape) if single else [pin_out(s) for s in out_shape]
    call = pl.pallas_call(body, out_shape=shapes, **kw)
    pin = lambda a: pltpu.with_memory_space_constraint(a, pltpu.HBM) if big(a.shape, a.dtype) else a
    return lambda *operands: call(*[pin(a) for a in operands])


def _nbytes(shape, dtype):
    n = 1
    for d in shape:
        if d is not None:
            n *= d
    return n * jnp.dtype(dtype).itemsize


def _row_tile(rows, target, mult=8):
    best = None
    for t in range(mult, min(rows, target) + 1, mult):
        if rows % t == 0:
            best = t
    assert best is not None, (rows, target, mult)
    return best


def _sigmoid(v):
    return jax.nn.sigmoid(v)


def _dsilu(v, s):
    return s * (1.0 + v * (1.0 - s))


def _embed_rms(x2, meta, gain):
    S, D = x2.shape
    T = S + TT

    def body(x_ref, meta_ref, g_ref, hs_ref, n_ref):
        i = pl.program_id(0)

        @pl.when(i == 0)
        def _():
            hs_ref[...] = jnp.zeros_like(hs_ref)
            hs_ref[PAD:, :] = meta_ref[...]

        @pl.when(i > 0)
        def _():
            hs_ref[...] = x_ref[...]

        h = hs_ref[...]
        r = lax.rsqrt(jnp.mean(h * h, axis=-1, keepdims=True) + EPS)
        n_ref[...] = ((h * r) * g_ref[...]).astype(BF16)

    blk = _nbytes((TT, D), F32) * 2 + _nbytes((TT, D), BF16)
    return _pallas(
        body, name="embed_rms", grid=(T // TT,),
        in_specs=[pl.BlockSpec((TT, D), lambda i: (jnp.maximum(i - 1, 0), 0)),
                  pl.BlockSpec((N_META, D), lambda i: (0, 0)),
                  pl.BlockSpec((1, D), lambda i: (0, 0))],
        out_specs=[pl.BlockSpec((TT, D), lambda i: (i, 0)), pl.BlockSpec((TT, D), lambda i: (i, 0))],
        out_shape=[jax.ShapeDtypeStruct((T, D), F32), jax.ShapeDtypeStruct((T, D), BF16)],
        compiler_params=_params(("parallel",), blk),
    )(x2, meta, gain)


def _rms(hs, gain, name):
    T, D = hs.shape
    te = _row_tile(T, 384)

    def body(h_ref, g_ref, n_ref):
        h = h_ref[...]
        r = lax.rsqrt(jnp.mean(h * h, axis=-1, keepdims=True) + EPS)
        n_ref[...] = ((h * r) * g_ref[...]).astype(BF16)

    blk = _nbytes((te, D), F32) + _nbytes((te, D), BF16)
    return _pallas(
        body, name=name, grid=(T // te,),
        in_specs=[pl.BlockSpec((te, D), lambda i: (i, 0)), pl.BlockSpec((1, D), lambda i: (0, 0))],
        out_specs=pl.BlockSpec((te, D), lambda i: (i, 0)),
        out_shape=jax.ShapeDtypeStruct((T, D), BF16),
        compiler_params=_params(("parallel",), blk),
    )(hs, gain)


def _rms_bwd_math(dn, h, g):
    r = lax.rsqrt(jnp.mean(h * h, axis=-1, keepdims=True) + EPS)
    xh = h * r
    dgain = jnp.sum(dn * xh, axis=0, keepdims=True)
    dxh = dn * g
    dh = r * (dxh - xh * jnp.mean(dxh * xh, axis=-1, keepdims=True))
    return dh, dgain


def _rms_bwd(dn, hs, gain, dres, scale, name):
    T, D = hs.shape
    te = _row_tile(T, 384)

    def body(dn_ref, h_ref, g_ref, dres_ref, dhs_ref, dhb_ref, dg_ref):
        dh, dgain = _rms_bwd_math(dn_ref[...], h_ref[...], g_ref[...])
        d = dres_ref[...] + dh
        dhs_ref[...] = d
        dhb_ref[...] = (scale * d).astype(BF16)

        @pl.when(pl.program_id(0) == 0)
        def _():
            dg_ref[...] = jnp.zeros_like(dg_ref)

        dg_ref[...] += dgain

    blk = _nbytes((te, D), F32) * 4 + _nbytes((te, D), BF16)
    row = lambda i: (i, 0)
    return _pallas(
        body, name=name, grid=(T // te,),
        in_specs=[pl.BlockSpec((te, D), row), pl.BlockSpec((te, D), row), pl.BlockSpec((1, D), lambda i: (0, 0)),
                  pl.BlockSpec((te, D), row)],
        out_specs=[pl.BlockSpec((te, D), row), pl.BlockSpec((te, D), row), pl.BlockSpec((1, D), lambda i: (0, 0))],
        out_shape=[jax.ShapeDtypeStruct((T, D), F32), jax.ShapeDtypeStruct((T, D), BF16),
                   jax.ShapeDtypeStruct((1, D), F32)],
        compiler_params=_params(("arbitrary",), blk),
    )(dn, hs, gain, dres)


def _rms_bwd_first(dn, hs, gain, dres, after):
    T, D = hs.shape
    S = T - TT

    def body(dn_ref, h_ref, g_ref, dres_ref, after_ref, gx_ref, gm_ref, dg_ref):
        i = pl.program_id(0)
        dh, dgain = _rms_bwd_math(dn_ref[...], h_ref[...], g_ref[...])
        d = dres_ref[...] + dh

        @pl.when(i == 0)
        def _():
            dg_ref[...] = jnp.zeros_like(dg_ref)
            gm_ref[...] = d[PAD:, :]

        @pl.when(i > 0)
        def _():
            gx_ref[...] = d

        dg_ref[...] += dgain

    blk = _nbytes((TT, D), F32) * 4
    row = lambda i: (i, 0)
    return _pallas(
        body, name="rms_bwd_ffn1", grid=(T // TT,),
        in_specs=[pl.BlockSpec((TT, D), row), pl.BlockSpec((TT, D), row), pl.BlockSpec((1, D), lambda i: (0, 0)),
                  pl.BlockSpec((TT, D), row), TOKEN],
        out_specs=[pl.BlockSpec((TT, D), lambda i: (jnp.maximum(i - 1, 0), 0)),
                   pl.BlockSpec((N_META, D), lambda i: (0, 0)), pl.BlockSpec((1, D), lambda i: (0, 0))],
        out_shape=[jax.ShapeDtypeStruct((S, D), F32), jax.ShapeDtypeStruct((N_META, D), F32),
                   jax.ShapeDtypeStruct((1, D), F32)],
        compiler_params=_params(("arbitrary",), blk),
    )(dn, hs, gain, dres, after)


def _final_loss(hs, gain, tgt):
    T, D = hs.shape

    def body(h_ref, g_ref, t_ref, dhs_ref, dhb_ref, loss_ref, dg_ref):
        i = pl.program_id(0)
        h = h_ref[...]
        g = g_ref[...]
        r = lax.rsqrt(jnp.mean(h * h, axis=-1, keepdims=True) + EPS)
        xh = h * r
        e = jnp.where(i > 0, xh * g - t_ref[...], 0.0)
        tile_loss = jnp.sum(jnp.sum(e * e, axis=1, keepdims=True), axis=0, keepdims=True) * (0.5 / D)
        dout = e * (1.0 / D)
        dgain = jnp.sum(dout * xh, axis=0, keepdims=True)
        dxh = dout * g
        d = r * (dxh - xh * jnp.mean(dxh * xh, axis=-1, keepdims=True))
        dhs_ref[...] = d
        dhb_ref[...] = (FFN_RES_SCALE * d).astype(BF16)

        @pl.when(i == 0)
        def _():
            loss_ref[...] = jnp.zeros_like(loss_ref)
            dg_ref[...] = jnp.zeros_like(dg_ref)

        loss_ref[...] += jnp.broadcast_to(tile_loss, loss_ref.shape)
        dg_ref[...] += dgain

    blk = _nbytes((TT, D), F32) * 3 + _nbytes((TT, D), BF16)
    row = lambda i: (i, 0)
    return _pallas(
        body, name="final_loss", grid=(T // TT,),
        in_specs=[pl.BlockSpec((TT, D), row), pl.BlockSpec((1, D), lambda i: (0, 0)),
                  pl.BlockSpec((TT, D), lambda i: (jnp.maximum(i - 1, 0), 0))],
        out_specs=[pl.BlockSpec((TT, D), row), pl.BlockSpec((TT, D), row),
                   pl.BlockSpec((1, 128), lambda i: (0, 0)), pl.BlockSpec((1, D), lambda i: (0, 0))],
        out_shape=[jax.ShapeDtypeStruct((T, D), F32), jax.ShapeDtypeStruct((T, D), BF16),
                   jax.ShapeDtypeStruct((1, 128), F32), jax.ShapeDtypeStruct((1, D), F32)],
        compiler_params=_params(("arbitrary",), blk),
    )(hs, gain, tgt)


MXU_COLS = 256


def _tm(T):
    return _row_tile(T, 704, 16)


def _col_chunks(n):
    return [(c, min(MXU_COLS, n - c)) for c in range(0, n, MXU_COLS)]


TOKEN = pl.BlockSpec((8, 128), lambda *_: (0, 0))


def _ffn_up(n, wg, wu, shards, prev, after, name):
    T, D = n.shape
    Fs = wg.shape[2]
    tm = _tm(T)
    nprev = 0 if prev is None else 3

    def body(shards_ref, n_ref, wg_ref, wu_ref, after_ref, *refs):
        g_ref, u_ref, a_ref = refs[nprev:]
        nn = n_ref[...]
        for c0, cw in _col_chunks(Fs):
            if 2 * cw == MXU_COLS:
                both = jnp.concatenate([wg_ref[:, c0:c0 + cw], wu_ref[:, c0:c0 + cw]], axis=1)
                gu = jnp.dot(nn, both, preferred_element_type=F32)
                g, u = gu[:, :cw], gu[:, cw:]
            else:
                g = jnp.dot(nn, wg_ref[:, c0:c0 + cw], preferred_element_type=F32)
                u = jnp.dot(nn, wu_ref[:, c0:c0 + cw], preferred_element_type=F32)
            g_ref[:, c0:c0 + cw] = g.astype(BF16)
            u_ref[:, c0:c0 + cw] = u.astype(BF16)
            a_ref[:, c0:c0 + cw] = (jax.nn.silu(g) * u).astype(BF16)

    blk = _nbytes((tm, D), BF16) + 2 * _nbytes((D, Fs), BF16) + 3 * _nbytes((tm, Fs), BF16)
    out = pl.BlockSpec((tm, Fs), lambda j, i, p: (i, p[j]))
    shp = jax.ShapeDtypeStruct((T, N_CHIPS * Fs), BF16)
    return _pallas(
        body, name=name,
        grid_spec=pltpu.PrefetchScalarGridSpec(
            num_scalar_prefetch=1, grid=(shards.shape[0], T // tm),
            in_specs=[pl.BlockSpec((tm, D), lambda j, i, p: (i, 0)),
                      pl.BlockSpec((None, D, Fs), lambda j, i, p: (p[j], 0, 0)),
                      pl.BlockSpec((None, D, Fs), lambda j, i, p: (p[j], 0, 0)), TOKEN] + [ANY] * nprev,
            out_specs=[out, out, out]),
        out_shape=[shp, shp, shp], input_output_aliases={5 + q: q for q in range(nprev)},
        compiler_params=_params(("arbitrary", "arbitrary"), blk),
    )(shards, n, wg, wu, after, *(prev or ()))


def _ffn_down(a, wd, hs, shards, name):
    T, F = a.shape
    _, Fs, D = wd.shape
    tm = _tm(T)
    tn = D // 2

    def body(shards_ref, a_ref, w_ref, h_ref, o_ref):
        part = FFN_RES_SCALE * jnp.dot(a_ref[...], w_ref[...], preferred_element_type=F32)

        @pl.when(pl.program_id(2) == 0)
        def _():
            o_ref[...] = h_ref[...] + part

        @pl.when(pl.program_id(2) > 0)
        def _():
            o_ref[...] += part

    blk = _nbytes((tm, Fs), BF16) + _nbytes((Fs, tn), BF16) + 3 * _nbytes((tm, tn), F32)
    return _pallas(
        body, name=name,
        grid_spec=pltpu.PrefetchScalarGridSpec(
            num_scalar_prefetch=1, grid=(D // tn, T // tm, shards.shape[0]),
            in_specs=[pl.BlockSpec((tm, Fs), lambda n, i, k, p: (i, p[k])),
                      pl.BlockSpec((None, Fs, tn), lambda n, i, k, p: (p[k], 0, n)),
                      pl.BlockSpec((tm, tn), lambda n, i, k, p: (i, n))],
            out_specs=pl.BlockSpec((tm, tn), lambda n, i, k, p: (i, n))),
        out_shape=jax.ShapeDtypeStruct((T, D), F32),
        compiler_params=_params(("parallel", "parallel", "arbitrary"), blk),
    )(shards, a, wd, hs)


def _mix_in(n, w, b):
    T, D = n.shape
    Ns = w.shape[2]
    tm = _tm(T)

    def body(n_ref, w_ref, b_ref, u_ref):
        u_ref[...] = jnp.dot(n_ref[...], w_ref[...], preferred_element_type=F32) + b_ref[...]

    blk = _nbytes((tm, D), BF16) + _nbytes((D, Ns), BF16) + 2 * _nbytes((tm, Ns), F32)
    return _pallas(
        body, name="mix_in", grid=(N_CHIPS, T // tm),
        in_specs=[pl.BlockSpec((tm, D), lambda j, i: (i, 0)), pl.BlockSpec((None, D, Ns), lambda j, i: (j, 0, 0)),
                  pl.BlockSpec((1, Ns), lambda j, i: (0, j))],
        out_specs=pl.BlockSpec((tm, Ns), lambda j, i: (i, j)),
        out_shape=jax.ShapeDtypeStruct((T, N_CHIPS * Ns), F32),
        compiler_params=_params(("parallel", "parallel"), blk),
    )(n, w, b)


def _mix_out(y, w, hs):
    T, D = y.shape
    tm = _tm(T)

    def body(y_ref, w_ref, h_ref, o_ref):
        o_ref[...] = h_ref[...] + jnp.dot(y_ref[...], w_ref[...], preferred_element_type=F32)

    blk = _nbytes((tm, D), BF16) + _nbytes((D, D), BF16) + 3 * _nbytes((tm, D), F32)
    return _pallas(
        body, name="mix_out", grid=(T // tm,),
        in_specs=[pl.BlockSpec((tm, D), lambda i: (i, 0)), pl.BlockSpec((D, D), lambda i: (0, 0)),
                  pl.BlockSpec((tm, D), lambda i: (i, 0))],
        out_specs=pl.BlockSpec((tm, D), lambda i: (i, 0)),
        out_shape=jax.ShapeDtypeStruct((T, D), F32),
        compiler_params=_params(("parallel",), blk),
    )(y, w, hs)


def _ffn_bwd_act(dfb, wd, g, u, after, name):
    T, D = dfb.shape
    F = wd.shape[0]
    tm = _row_tile(T, 1408, 16)
    tn = 2 * MXU_COLS

    tr = _tm(tm)

    def body(d_ref, w_ref, g_ref, u_ref, after_ref, dg_ref, du_ref):
        for r0 in range(0, tm, tr):
            dv = d_ref[r0:r0 + tr, :]
            for c0, cw in _col_chunks(tn):
                da = lax.dot_general(dv, w_ref[c0:c0 + cw, :], NT_DIMS, preferred_element_type=F32)
                gv = g_ref[r0:r0 + tr, c0:c0 + cw].astype(F32)
                uv = u_ref[r0:r0 + tr, c0:c0 + cw].astype(F32)
                s = _sigmoid(gv)
                du_ref[r0:r0 + tr, c0:c0 + cw] = (da * (gv * s)).astype(BF16)
                dg_ref[r0:r0 + tr, c0:c0 + cw] = (da * uv * _dsilu(gv, s)).astype(BF16)

    blk = _nbytes((tm, D), BF16) + _nbytes((tn, D), BF16) + 4 * _nbytes((tm, tn), BF16)
    io = pl.BlockSpec((tm, tn), lambda n, i: (i, n))
    shp = jax.ShapeDtypeStruct((T, F), BF16)
    return _pallas(
        body, name=name, grid=(F // tn, T // tm),
        in_specs=[pl.BlockSpec((tm, D), lambda n, i: (i, 0)), pl.BlockSpec((tn, D), lambda n, i: (n, 0)), io, io, TOKEN],
        out_specs=[io, io], out_shape=[shp, shp],
        compiler_params=_params(("parallel", "parallel"), blk),
    )(dfb, wd, g, u, after)


def _nt_panel(lhs_list, w_list, after, name):
    T = lhs_list[0].shape[0]
    nsh, Dout, Ks = w_list[0].shape
    npair = len(lhs_list)
    tm = _row_tile(T, 1408, 16)
    tn = Dout // 2

    def body(*refs):
        l_refs, w_refs, o_ref = refs[:npair], refs[npair:2 * npair], refs[2 * npair + 1]
        j = pl.program_id(2)
        k0 = Ks - Ks % MXU_COLS if npair == 2 and 2 * (Ks % MXU_COLS) == MXU_COLS else Ks
        acc = None
        for p in range(npair):
            part = lax.dot_general(l_refs[p][:, :k0], w_refs[p][:, :k0], NT_DIMS, preferred_element_type=F32)
            acc = part if acc is None else acc + part
        if k0 < Ks:
            lhs = jnp.concatenate([l_refs[p][:, k0:] for p in range(npair)], axis=1)
            rhs = jnp.concatenate([w_refs[p][:, k0:] for p in range(npair)], axis=1)
            acc = acc + lax.dot_general(lhs, rhs, NT_DIMS, preferred_element_type=F32)

        @pl.when(j == 0)
        def _():
            o_ref[...] = acc

        @pl.when(j > 0)
        def _():
            o_ref[...] += acc

    blk = npair * (_nbytes((tm, Ks), BF16) + _nbytes((tn, Ks), BF16)) + 2 * _nbytes((tm, tn), F32)
    return _pallas(
        body, name=name, grid=(Dout // tn, T // tm, nsh),
        in_specs=[pl.BlockSpec((tm, Ks), lambda n, i, j: (i, j))] * npair
                 + [pl.BlockSpec((None, tn, Ks), lambda n, i, j: (j, n, 0))] * npair + [TOKEN],
        out_specs=pl.BlockSpec((tm, tn), lambda n, i, j: (i, n)),
        out_shape=jax.ShapeDtypeStruct((T, Dout), F32),
        compiler_params=_params(("parallel", "parallel", "arbitrary"), blk),
    )(*lhs_list, *w_list, after)


def _tn_call(name, grid, lhs, lhs_spec, rhs_list, rhs_specs, out_shapes, out_specs, blk, after=None):
    nr = len(rhs_list)
    extra = [] if after is None else [after]

    def body(*refs):
        l_ref, r_refs, o_refs = refs[0], refs[1:1 + nr], refs[len(refs) - nr:]
        k = pl.program_id(len(grid) - 1)
        lv = l_ref[...]
        for q in range(nr):
            part = lax.dot_general(lv, r_refs[q][...], TN_DIMS, preferred_element_type=F32)
            part = part.reshape(o_refs[q].shape)

            @pl.when(k == 0)
            def _(o=o_refs[q], part=part):
                o[...] = part

            @pl.when(k > 0)
            def _(o=o_refs[q], part=part):
                o[...] += part

    return _pallas(
        body, name=name, grid=grid, in_specs=[lhs_spec] + rhs_specs + [TOKEN] * len(extra), out_specs=out_specs,
        out_shape=out_shapes, compiler_params=_params(("parallel",) * (len(grid) - 1) + ("arbitrary",), blk),
    )(lhs, *rhs_list, *extra)


def _tk(T):
    return _row_tile(T, 1408, 128)


def _wgrad_cols(n, rhs_list, name, after=None):
    T, D = n.shape
    Ns = rhs_list[0].shape[1] // N_CHIPS
    tk = _tk(T)
    nr = len(rhs_list)
    blk = _nbytes((tk, D // 2), BF16) + nr * (_nbytes((tk, Ns), BF16) + 2 * _nbytes((D // 2, Ns), F32))
    return _tn_call(
        name, (N_CHIPS, 2, T // tk), n, pl.BlockSpec((tk, D // 2), lambda j, m, k: (k, m)),
        rhs_list, [pl.BlockSpec((tk, Ns), lambda j, m, k: (k, j))] * nr,
        [jax.ShapeDtypeStruct((N_CHIPS, 2, D // 2, Ns), F32)] * nr,
        [pl.BlockSpec((None, None, D // 2, Ns), lambda j, m, k: (j, m, 0, 0))] * nr, blk, after)


def _wgrad_down(a, dfb, name):
    T, F = a.shape
    D = dfb.shape[1]
    Fs = F // N_CHIPS
    tk = _tk(T)
    tn = D // 2
    blk = _nbytes((tk, Fs), BF16) + _nbytes((tk, tn), BF16) + 2 * _nbytes((Fs, tn), F32)
    return _tn_call(
        name, (N_CHIPS, D // tn, T // tk), a, pl.BlockSpec((tk, Fs), lambda j, n, k: (k, j)),
        [dfb], [pl.BlockSpec((tk, tn), lambda j, n, k: (k, n))],
        [jax.ShapeDtypeStruct((N_CHIPS, 2, Fs // 2, D), F32)],
        [pl.BlockSpec((None, 2, Fs // 2, tn), lambda j, n, k: (j, 0, 0, n))], blk)[0]


def _wgrad_out(y, dmb):
    T, D = y.shape
    tk = _tk(T)
    tn = D // 2
    rows = D // (2 * N_CHIPS)
    blk = _nbytes((tk, D // 2), BF16) + _nbytes((tk, tn), BF16) + 2 * _nbytes((D // 2, tn), F32)
    return _tn_call(
        "wgrad_w_out", (2, D // tn, T // tk), y, pl.BlockSpec((tk, D // 2), lambda m, n, k: (k, m)),
        [dmb], [pl.BlockSpec((tk, tn), lambda m, n, k: (k, n))],
        [jax.ShapeDtypeStruct((N_CHIPS, 2, rows, D), F32)],
        [pl.BlockSpec((2, 2, rows, tn), lambda m, n, k: (m, 0, 0, n))], blk)[0]


def _row_masks(i, last):
    rows = i * TT + lax.broadcasted_iota(jnp.int32, (TT, 1), 0)
    prows = i * TT - HALO + lax.broadcasted_iota(jnp.int32, (HALO, 1), 0)
    return rows >= PAD, (prows >= PAD) & (i > 0), i < last


def _conv_inputs(u, up, mask_c, mask_p, zbuf, pbuf, C1):
    b, c, v, a, g = (u[:, k * C1:(k + 1) * C1] for k in range(5))
    cp, vp, ap, gp = (up[:, k * C1:(k + 1) * C1] for k in range(1, 5))
    sg = _sigmoid(g)
    pbuf[0:HALO, :] = jnp.where(mask_p, cp * vp, 0.0)
    pbuf[HALO:, :] = jnp.where(mask_c, c * v, 0.0)
    zbuf[0:HALO, :] = jnp.where(mask_p, ap * _sigmoid(gp), 0.0)
    zbuf[HALO:, :] = jnp.where(mask_c, a * sg, 0.0)
    return b, c, v, a, sg


SUBLANES = 8
SHIFT_ROWS = TT + HALO - SUBLANES


def _shifted_scratch(C1):
    return pltpu.VMEM((SUBLANES - 1, SHIFT_ROWS, C1), F32)


def _fill_shifted(buf, sh):
    for r in range(1, SUBLANES):
        sh[r - 1] = buf[r:r + SHIFT_ROWS, :]


LANES = 128


def _window(buf, sh, lo, c0):
    if sh is None or lo % SUBLANES == 0:
        return buf[lo:lo + TT, c0:c0 + LANES]
    q, r = divmod(lo, SUBLANES)
    return sh[r - 1, q * SUBLANES:q * SUBLANES + TT, c0:c0 + LANES]


def _tap_sum(w_ref, buf, sh, starts):
    chunks = []
    for c0 in range(0, buf.shape[1], LANES):
        acc = None
        for k, lo in enumerate(starts):
            term = w_ref[k:k + 1, c0:c0 + LANES] * _window(buf, sh, lo, c0)
            acc = term if acc is None else acc + term
        chunks.append(acc)
    return jnp.concatenate(chunks, axis=1)


def _causal_conv(w_ref, buf, sh=None):
    K = w_ref.shape[0]
    return _tap_sum(w_ref, buf, sh, [HALO - (K - 1) + k for k in range(K)])


def _anticausal_conv(w_ref, buf, sh=None):
    K = w_ref.shape[0]
    return _tap_sum(w_ref, buf, sh, [K - 1 - k for k in range(K)])


def _conv_weight_sums(dw_ref, dy, buf, sh=None):
    K = dw_ref.shape[0]
    for c0 in range(0, buf.shape[1], LANES):
        dyc = dy[:, c0:c0 + LANES]
        for k in range(K):
            prod = dyc * _window(buf, sh, HALO - (K - 1) + k, c0)
            dw_ref[k:k + 1, c0:c0 + LANES] += jnp.sum(prod, axis=0, keepdims=True)


def _layernorm_stats(z1):
    mu = jnp.mean(z1, axis=-1, keepdims=True)
    zc = z1 - mu
    rs = lax.rsqrt(jnp.mean(zc * zc, axis=-1, keepdims=True) + EPS)
    return zc * rs, rs


def _mixer_specs(T, DIN, C1, ksc, kcf):
    cur = pl.BlockSpec((TT, DIN), lambda i: (i, 0))
    prev = pl.BlockSpec((HALO, DIN), lambda i: (jnp.maximum(i * (TT // HALO) - 1, 0), 0))
    full = lambda r: pl.BlockSpec((r, C1), lambda i: (0, 0))
    return cur, prev, [full(ksc), full(kcf), full(1), full(1), full(1)]


def _mix_conv_fwd(u, wsc, wcf, bcf, lg, lb):
    T, DIN = u.shape
    C1 = DIN // 5
    last = T // TT - 1

    def body(u_ref, up_ref, wsc_ref, wcf_ref, bcf_ref, lg_ref, lb_ref, y_ref, zbuf, pbuf, zsh):
        i = pl.program_id(0)
        mask_c, mask_p, _ = _row_masks(i, last)
        b, _, _, _, _ = _conv_inputs(u_ref[...], up_ref[...], mask_c, mask_p, zbuf, pbuf, C1)
        _fill_shifted(zbuf, zsh)
        cs = _causal_conv(wsc_ref, pbuf)
        z1 = _causal_conv(wcf_ref, zbuf, zsh) + bcf_ref[...]
        zh, _ = _layernorm_stats(z1)
        ln = zh * lg_ref[...] + lb_ref[...]
        y_ref[:, 0:C1] = jnp.where(mask_c, b * cs, 0.0).astype(BF16)
        y_ref[:, C1:] = jnp.where(mask_c, jax.nn.silu(ln), 0.0).astype(BF16)

    cur, prev, small = _mixer_specs(T, DIN, C1, wsc.shape[0], wcf.shape[0])
    blk = _nbytes((TT + HALO, DIN), F32) + _nbytes((TT, 2 * C1), BF16) + 12 * _nbytes((TT + HALO, C1), F32)
    return _pallas(
        body, name="mix_conv_fwd", grid=(T // TT,),
        in_specs=[cur, prev] + small,
        out_specs=pl.BlockSpec((TT, 2 * C1), lambda i: (i, 0)),
        out_shape=jax.ShapeDtypeStruct((T, 2 * C1), BF16),
        scratch_shapes=[pltpu.VMEM((TT + HALO, C1), F32), pltpu.VMEM((TT + HALO, C1), F32), _shifted_scratch(C1)],
        compiler_params=_params(("arbitrary",), blk),
    )(u, u, wsc, wcf, bcf, lg, lb)


def _mix_conv_bwd1(u, dy, wsc, wcf, bcf, lg, lb):
    T, DIN = u.shape
    C1 = DIN // 5
    last = T // TT - 1

    def body(u_ref, up_ref, dy_ref, wsc_ref, wcf_ref, bcf_ref, lg_ref, lb_ref,
             dz1_ref, dcs_ref, db_ref, dlg_ref, dlb_ref, dbcf_ref, zbuf, pbuf, zsh):
        i = pl.program_id(0)
        mask_c, mask_p, _ = _row_masks(i, last)
        b, _, _, _, _ = _conv_inputs(u_ref[...], up_ref[...], mask_c, mask_p, zbuf, pbuf, C1)
        _fill_shifted(zbuf, zsh)
        cs = _causal_conv(wsc_ref, pbuf)
        z1 = _causal_conv(wcf_ref, zbuf, zsh) + bcf_ref[...]
        zh, rs = _layernorm_stats(z1)
        ln = zh * lg_ref[...] + lb_ref[...]
        dy = dy_ref[...]
        dysc = jnp.where(mask_c, dy[:, 0:C1], 0.0)
        dycf = jnp.where(mask_c, dy[:, C1:], 0.0)
        db_ref[...] = (dysc * cs).astype(BF16)
        dcs_ref[...] = dysc * b
        dl = dycf * _dsilu(ln, _sigmoid(ln))
        dzh = dl * lg_ref[...]
        dz1 = rs * (dzh - jnp.mean(dzh, axis=-1, keepdims=True) - zh * jnp.mean(dzh * zh, axis=-1, keepdims=True))
        dz1_ref[...] = dz1

        @pl.when(i == 0)
        def _():
            dlg_ref[...] = jnp.zeros_like(dlg_ref)
            dlb_ref[...] = jnp.zeros_like(dlb_ref)
            dbcf_ref[...] = jnp.zeros_like(dbcf_ref)

        dlg_ref[...] += jnp.sum(dl * zh, axis=0, keepdims=True)
        dlb_ref[...] += jnp.sum(dl, axis=0, keepdims=True)
        dbcf_ref[...] += jnp.sum(dz1, axis=0, keepdims=True)

    cur, prev, small = _mixer_specs(T, DIN, C1, wsc.shape[0], wcf.shape[0])
    tile = lambda: pl.BlockSpec((TT, C1), lambda i: (i, 0))
    vec = lambda: pl.BlockSpec((1, C1), lambda i: (0, 0))
    blk = _nbytes((TT + HALO, DIN), F32) + 4 * _nbytes((TT, C1), F32) + 16 * _nbytes((TT + HALO, C1), F32)
    return _pallas(
        body, name="mix_conv_bwd1", grid=(T // TT,),
        in_specs=[cur, prev, pl.BlockSpec((TT, 2 * C1), lambda i: (i, 0))] + small,
        out_specs=[tile(), tile(), tile(), vec(), vec(), vec()],
        out_shape=[jax.ShapeDtypeStruct((T, C1), F32), jax.ShapeDtypeStruct((T, C1), F32),
                   jax.ShapeDtypeStruct((T, C1), BF16)] + [jax.ShapeDtypeStruct((1, C1), F32)] * 3,
        scratch_shapes=[pltpu.VMEM((TT + HALO, C1), F32), pltpu.VMEM((TT + HALO, C1), F32), _shifted_scratch(C1)],
        compiler_params=_params(("arbitrary",), blk),
    )(u, u, dy, wsc, wcf, bcf, lg, lb)


def _mix_conv_bwd2(u, dz1, dcs, db, wsc, wcf):
    T, DIN = u.shape
    C1 = DIN // 5
    last = T // TT - 1
    ksc, kcf = wsc.shape[0], wcf.shape[0]

    def body(u_ref, up_ref, dz_ref, dzn_ref, dc_ref, dcn_ref, db_ref, wsc_ref, wcf_ref,
             du_ref, dbin_ref, dwsc_ref, dwcf_ref, zbuf, pbuf, dzbuf, dcbuf, zsh, dzsh):
        i = pl.program_id(0)
        mask_c, mask_p, has_next = _row_masks(i, last)
        _, c, v, a, sg = _conv_inputs(u_ref[...], up_ref[...], mask_c, mask_p, zbuf, pbuf, C1)
        dz1 = dz_ref[...]
        dcs = dc_ref[...]
        dzbuf[0:TT, :] = dz1
        dzbuf[TT:, :] = jnp.where(has_next, dzn_ref[...], 0.0)
        dcbuf[0:TT, :] = dcs
        dcbuf[TT:, :] = jnp.where(has_next, dcn_ref[...], 0.0)

        @pl.when(i == 0)
        def _():
            dbin_ref[...] = jnp.zeros_like(dbin_ref)
            dwsc_ref[...] = jnp.zeros_like(dwsc_ref)
            dwcf_ref[...] = jnp.zeros_like(dwcf_ref)

        _fill_shifted(zbuf, zsh)
        _fill_shifted(dzbuf, dzsh)
        _conv_weight_sums(dwcf_ref, dz1, zbuf, zsh)
        _conv_weight_sums(dwsc_ref, dcs, pbuf)
        dz0 = jnp.where(mask_c, _anticausal_conv(wcf_ref, dzbuf, dzsh), 0.0)
        dp = jnp.where(mask_c, _anticausal_conv(wsc_ref, dcbuf), 0.0)
        parts = (db_ref[...].astype(F32), dp * v, dp * c, dz0 * sg, dz0 * a * sg * (1.0 - sg))
        for k, part in enumerate(parts):
            du_ref[:, k * C1:(k + 1) * C1] = part.astype(BF16)
            dbin_ref[:, k * C1:(k + 1) * C1] += jnp.sum(part, axis=0, keepdims=True)

    cur, prev, small = _mixer_specs(T, DIN, C1, ksc, kcf)
    tile = lambda: pl.BlockSpec((TT, C1), lambda i: (i, 0))
    nxt = lambda: pl.BlockSpec((HALO, C1), lambda i: (jnp.minimum((i + 1) * (TT // HALO), T // HALO - 1), 0))
    blk = (_nbytes((TT + HALO, DIN), F32) + _nbytes((TT, DIN), BF16) + 5 * _nbytes((TT, C1), F32)
           + 16 * _nbytes((TT + HALO, C1), F32))
    buf = lambda: pltpu.VMEM((TT + HALO, C1), F32)
    return _pallas(
        body, name="mix_conv_bwd2", grid=(T // TT,),
        in_specs=[cur, prev, tile(), nxt(), tile(), nxt(), tile(), small[0], small[1]],
        out_specs=[pl.BlockSpec((TT, DIN), lambda i: (i, 0)), pl.BlockSpec((1, DIN), lambda i: (0, 0)),
                   pl.BlockSpec((ksc, C1), lambda i: (0, 0)), pl.BlockSpec((kcf, C1), lambda i: (0, 0))],
        out_shape=[jax.ShapeDtypeStruct((T, DIN), BF16), jax.ShapeDtypeStruct((1, DIN), F32),
                   jax.ShapeDtypeStruct((ksc, C1), F32), jax.ShapeDtypeStruct((kcf, C1), F32)],
        scratch_shapes=[buf(), buf(), buf(), buf(), _shifted_scratch(C1), _shifted_scratch(C1)],
        compiler_params=_params(("arbitrary",), blk),
    )(u, u, dz1, dz1, dcs, dcs, db, wsc, wcf)


def _place():
    x, y, c = lax.axis_index("x"), lax.axis_index("y"), lax.axis_index("c")
    chips = [(1 - x, y), (x, 1 - y), (1 - x, 1 - y)]
    return x, y, c, chips


ANY = pl.BlockSpec(memory_space=pl.ANY)


def _cast_own_block(place, w, name):
    R, C = w.shape
    tr = _row_tile(R // 2, 256, 16)
    nblk = R // 2 // tr

    def body(place_ref, w_ref, o_ref):
        o_ref[...] = w_ref[...].astype(BF16)

    return _pallas(
        body, name=name,
        grid_spec=pltpu.PrefetchScalarGridSpec(
            num_scalar_prefetch=1, grid=(2, nblk),
            in_specs=[pl.BlockSpec((tr, C), lambda h, i, p: (h * nblk + i, 0))],
            out_specs=pl.BlockSpec((None, None, tr, C), lambda h, i, p: (p[0], h, i, 0))),
        out_shape=jax.ShapeDtypeStruct((N_CHIPS, 2, R // 2, C), BF16),
        compiler_params=_params(("parallel", "parallel"), _nbytes((tr, C), F32) + _nbytes((tr, C), BF16)),
    )(place, w)


HBM = pl.BlockSpec(memory_space=pltpu.HBM)
SEM = pl.BlockSpec(memory_space=pltpu.SEMAPHORE)
EFFECT = pltpu.SideEffectType.DATAFLOW_SIDE_EFFECTING


def _gather_copies(refs, send, recv, rels=(0, 1, 2)):
    x, y, c, chips = _place()
    s = 2 * x + y
    n = len(rels)
    return [pltpu.make_async_remote_copy(src_ref=ref.at[s, c], dst_ref=ref.at[s, c], send_sem=send.at[n * w + k],
                                         recv_sem=recv.at[n * w + k], device_id=(*chips[r], c), device_id_type=MESH)
            for w, ref in enumerate(refs) for k, r in enumerate(rels)]


def _scatter_copies(refs, send, recv):
    x, y, c, chips = _place()
    nw = len(refs) // 2
    return [pltpu.make_async_remote_copy(src_ref=refs[w].at[2 * tx + ty], dst_ref=refs[nw + w].at[r],
                                         send_sem=send.at[3 * w + r], recv_sem=recv.at[3 * w + r],
                                         device_id=(tx, ty, c), device_id_type=MESH)
            for w in range(nw) for r, (tx, ty) in enumerate(chips)]


def _pair_copies(refs, send, recv):
    x, y, c, _ = _place()
    nw = len(refs) // 2
    return [pltpu.make_async_remote_copy(src_ref=refs[w].at[j, 1 - c], dst_ref=refs[nw + w].at[j],
                                         send_sem=send.at[N_CHIPS * w + j], recv_sem=recv.at[N_CHIPS * w + j],
                                         device_id=(x, y, 1 - c), device_id_type=MESH)
            for w in range(nw) for j in range(N_CHIPS)]


def _start_copies(bufs, after, ncopies, make_copies, name):
    n = len(bufs)

    def body(*refs):
        in_refs, send, recv, token = refs[:n], refs[n + 1], refs[n + 2], refs[2 * n + 3]
        for cp in make_copies(in_refs, send, recv):
            cp.start()
        token[...] = jnp.zeros_like(token)

    outs = _pallas(
        body, name=name, in_specs=[HBM] * n + [ANY],
        out_specs=[SEM, SEM] + [HBM] * n + [pl.BlockSpec(memory_space=pltpu.VMEM)],
        out_shape=[pltpu.SemaphoreType.DMA((ncopies,)), pltpu.SemaphoreType.DMA((ncopies,))]
                  + [pltpu.HBM(b.shape, b.dtype) for b in bufs] + [jax.ShapeDtypeStruct((8, 128), F32)],
        input_output_aliases={k: 2 + k for k in range(n)},
        compiler_params=pltpu.CompilerParams(has_side_effects=EFFECT),
    )(*[pltpu.with_memory_space_constraint(b, pltpu.HBM) for b in bufs], after)
    return outs[0], outs[1], list(outs[2:2 + n]), outs[2 + n]


def _wait_copies(send, recv, bufs, after, make_copies, name):
    n = len(bufs)

    def body(*refs):
        in_refs, send_ref, recv_ref = refs[:n], refs[n], refs[n + 1]
        for cp in make_copies(in_refs, send_ref, recv_ref):
            cp.wait_send()
            cp.wait_recv()

    outs = _pallas(
        body, name=name, in_specs=[HBM] * n + [SEM, SEM, ANY], out_specs=[HBM] * n,
        out_shape=[pltpu.HBM(b.shape, b.dtype) for b in bufs],
        input_output_aliases={k: k for k in range(n)},
        compiler_params=pltpu.CompilerParams(has_side_effects=EFFECT),
    )(*bufs, send, recv, after)
    return list(outs)


def _forward_halves(bufs, name, rels=(0, 1, 2)):
    nw = len(bufs)
    n = len(rels)

    def body(*refs):
        o_refs = refs[nw:2 * nw]
        send, recv = refs[2 * nw:]
        x, y, c, chips = _place()
        sib = (x, y, 1 - c)
        copies = []
        for w in range(nw):
            for k, r in enumerate(rels):
                tx, ty = chips[r]
                ref = o_refs[w].at[2 * tx + ty, c]
                cp = pltpu.make_async_remote_copy(src_ref=ref, dst_ref=ref, send_sem=send.at[n * w + k],
                                                  recv_sem=recv.at[n * w + k], device_id=sib, device_id_type=MESH)
                cp.start()
                copies.append(cp)
        for w in range(nw):
            for k, r in enumerate(rels):
                tx, ty = chips[r]
                ref = o_refs[w].at[2 * tx + ty, 1 - c]
                pltpu.make_async_remote_copy(src_ref=ref, dst_ref=ref, send_sem=send.at[n * w + k],
                                             recv_sem=recv.at[n * w + k], device_id=sib, device_id_type=MESH).wait_recv()
        for cp in copies:
            cp.wait_send()

    return _pallas(
        body, name=name, in_specs=[ANY] * nw, out_specs=[ANY] * nw,
        out_shape=[jax.ShapeDtypeStruct(b.shape, b.dtype) for b in bufs],
        input_output_aliases={w: w for w in range(nw)},
        scratch_shapes=[pltpu.SemaphoreType.DMA((n * nw,)), pltpu.SemaphoreType.DMA((n * nw,))],
    )(*bufs)


def _half_exchange(hs, name):
    nw = len(hs)

    def body(*refs):
        o_refs = refs[nw:2 * nw]
        send, recv = refs[2 * nw:]
        x, y, c, _ = _place()
        sib = (x, y, 1 - c)
        copies = []
        for w in range(nw):
            cp = pltpu.make_async_remote_copy(src_ref=o_refs[w].at[c], dst_ref=o_refs[w].at[c], send_sem=send.at[w],
                                              recv_sem=recv.at[w], device_id=sib, device_id_type=MESH)
            cp.start()
            copies.append(cp)
        for w, cp in enumerate(copies):
            cp.wait_send()
            pltpu.make_async_remote_copy(src_ref=o_refs[w].at[c], dst_ref=o_refs[w].at[1 - c], send_sem=send.at[w],
                                         recv_sem=recv.at[w], device_id=sib, device_id_type=MESH).wait_recv()

    return _pallas(
        body, name=name, in_specs=[ANY] * nw, out_specs=[ANY] * nw,
        out_shape=[jax.ShapeDtypeStruct(h.shape, F32) for h in hs],
        input_output_aliases={w: w for w in range(nw)},
        scratch_shapes=[pltpu.SemaphoreType.DMA((nw,)), pltpu.SemaphoreType.DMA((nw,))],
    )(*hs)


def _share_small(v, reduce, name):
    R, C = v.shape

    def body(v_ref, o_ref, *scratch):
        if reduce:
            all_ref, send, recv, lsem = scratch
        else:
            all_ref = o_ref
            send, recv, lsem = scratch
        x, y, c, _ = _place()
        me = 4 * x + 2 * y + c
        loc = pltpu.make_async_copy(v_ref, all_ref.at[me], lsem)
        loc.start()
        copies = []
        for k in range(1, N_DEV):
            kx, ky, kc = (k >> 2) & 1, (k >> 1) & 1, k & 1
            peer = (x ^ kx, y ^ ky, c ^ kc)
            cp = pltpu.make_async_remote_copy(src_ref=v_ref, dst_ref=all_ref.at[me], send_sem=send.at[k - 1],
                                              recv_sem=recv.at[k - 1], device_id=peer, device_id_type=MESH)
            cp.start()
            copies.append(cp)
        for k in range(1, N_DEV):
            kx, ky, kc = (k >> 2) & 1, (k >> 1) & 1, k & 1
            src = 4 * (x ^ kx) + 2 * (y ^ ky) + (c ^ kc)
            pltpu.make_async_remote_copy(src_ref=v_ref, dst_ref=all_ref.at[src], send_sem=send.at[k - 1],
                                         recv_sem=recv.at[k - 1], device_id=(x, y, c), device_id_type=MESH).wait_recv()
        for cp in copies:
            cp.wait_send()
        loc.wait()
        if reduce:
            total = all_ref[0]
            for d in range(1, N_DEV):
                total = total + all_ref[d]
            o_ref[...] = total

    vm = pl.BlockSpec(memory_space=pltpu.VMEM)
    sems = [pltpu.SemaphoreType.DMA((N_DEV - 1,)), pltpu.SemaphoreType.DMA((N_DEV - 1,)), pltpu.SemaphoreType.DMA]
    if reduce:
        out_shape = jax.ShapeDtypeStruct((R, C), F32)
        scratch = [pltpu.VMEM((N_DEV, R, C), F32)] + sems
    else:
        out_shape = jax.ShapeDtypeStruct((N_DEV, R, C), F32)
        scratch = sems
    return _pallas(
        body, name=name, in_specs=[vm], out_specs=vm, out_shape=out_shape, scratch_shapes=scratch,
        compiler_params=pltpu.CompilerParams(vmem_limit_bytes=int(min(4 * N_DEV * R * C * 4 + 2 ** 24, 2 ** 25 + 2 ** 24))),
    )(v)


def _pair_sum(place, g, rb, name):
    _, _, Rh, C = g.shape
    tr = _row_tile(Rh, 256, 16)

    def body(place_ref, g_ref, r_ref, q_ref):
        q_ref[...] = (g_ref[...] + r_ref[...]).astype(BF16)

    blk = 2 * _nbytes((tr, C), F32) + _nbytes((tr, C), BF16)
    return _pallas(
        body, name=name,
        grid_spec=pltpu.PrefetchScalarGridSpec(
            num_scalar_prefetch=1, grid=(N_CHIPS - 1, Rh // tr),
            in_specs=[pl.BlockSpec((None, None, tr, C), lambda j, i, p: (p[0] ^ (j + 1), p[1], i, 0)),
                      pl.BlockSpec((None, tr, C), lambda j, i, p: (p[0] ^ (j + 1), i, 0))],
            out_specs=pl.BlockSpec((None, tr, C), lambda j, i, p: (p[0] ^ (j + 1), i, 0))),
        out_shape=jax.ShapeDtypeStruct((N_CHIPS, Rh, C), BF16),
        compiler_params=_params(("parallel", "parallel"), blk),
    )(place, g, rb)


def _chip_sum(place, g, rb, rc, name):
    _, _, Rh, C = g.shape
    tr = _row_tile(Rh, 256, 16)

    def body(place_ref, g_ref, r_ref, rc_ref, o_ref):
        total = g_ref[...] + r_ref[...]
        for r in range(3):
            total = total + rc_ref[r].astype(F32)
        o_ref[...] = total

    blk = 3 * _nbytes((tr, C), F32) + 3 * _nbytes((tr, C), BF16)
    return _pallas(
        body, name=name,
        grid_spec=pltpu.PrefetchScalarGridSpec(
            num_scalar_prefetch=1, grid=(Rh // tr,),
            in_specs=[pl.BlockSpec((None, None, tr, C), lambda i, p: (p[0], p[1], i, 0)),
                      pl.BlockSpec((None, tr, C), lambda i, p: (p[0], i, 0)),
                      pl.BlockSpec((3, tr, C), lambda i, p: (0, i, 0))],
            out_specs=pl.BlockSpec((None, tr, C), lambda i, p: (p[1], i, 0))),
        out_shape=jax.ShapeDtypeStruct((2, Rh, C), F32),
        compiler_params=_params(("parallel",), blk),
    )(place, g, rb, rc)


def _adamw_math(w, g, m, v):
    m = ADAM_B1 * m + (1.0 - ADAM_B1) * g
    v = ADAM_B2 * v + (1.0 - ADAM_B2) * jnp.square(g)
    m_hat = m / (1.0 - ADAM_B1 ** ADAM_STEP)
    v_hat = v / (1.0 - ADAM_B2 ** ADAM_STEP)
    delta = -ADAM_LR * (m_hat / (jnp.sqrt(v_hat) + ADAM_EPS) + ADAM_WD * w)
    return delta, m, v


def _adamw(w, g, m, v, name):
    R, C = w.shape
    tr = _row_tile(R, 256)

    def body(w_ref, g_ref, m_ref, v_ref, go_ref, d_ref, nm_ref, nv_ref):
        gv = g_ref[...]
        d, nm, nv = _adamw_math(w_ref[...], gv, m_ref[...], v_ref[...])
        go_ref[...] = gv
        d_ref[...] = d
        nm_ref[...] = nm
        nv_ref[...] = nv

    spec = pl.BlockSpec((tr, C), lambda i: (i, 0))
    shp = jax.ShapeDtypeStruct((R, C), F32)
    return _pallas(
        body, name=name, grid=(R // tr,), in_specs=[spec] * 4, out_specs=[spec] * 4, out_shape=[shp] * 4,
        compiler_params=_params(("parallel",), 8 * _nbytes((tr, C), F32)),
    )(w, g, m, v)


def _adamw_small(ws, gs, ms, vs):
    n = len(ws)

    def body(*refs):
        for k in range(n):
            w_ref, g_ref, m_ref, v_ref = (refs[q * n + k] for q in range(4))
            d, nm, nv = _adamw_math(w_ref[...], g_ref[...], m_ref[...], v_ref[...])
            refs[4 * n + k][...] = d
            refs[5 * n + k][...] = nm
            refs[6 * n + k][...] = nv

    vm = pl.BlockSpec(memory_space=pltpu.VMEM)
    shapes = [jax.ShapeDtypeStruct(w.shape, F32) for w in ws]
    outs = _pallas(
        body, name="adamw_small", in_specs=[vm] * (4 * n), out_specs=[vm] * (3 * n), out_shape=shapes * 3,
    )(*ws, *gs, *ms, *vs)
    return outs[:n], outs[n:2 * n], outs[2 * n:]


def _pad_rows(a, rows):
    return jnp.pad(a, ((0, rows - a.shape[0]), (0, 0)))


def kernel(x, meta_tokens, ffn1_norm, ffn1_w_gate, ffn1_w_up, ffn1_w_down, mix_norm, w_in, b_in, conv_sc_w, conv_cf_w, conv_cf_b, ln_cf_g, ln_cf_b, w_out, ffn2_norm, ffn2_w_gate, ffn2_w_up, ffn2_w_down, final_norm, loss_target, m_meta_tokens, m_ffn1_norm, m_ffn1_w_gate, m_ffn1_w_up, m_ffn1_w_down, m_mix_norm, m_w_in, m_b_in, m_conv_sc_w, m_conv_cf_w, m_conv_cf_b, m_ln_cf_g, m_ln_cf_b, m_w_out, m_ffn2_norm, m_ffn2_w_gate, m_ffn2_w_up, m_ffn2_w_down, m_final_norm, v_meta_tokens, v_ffn1_norm, v_ffn1_w_gate, v_ffn1_w_up, v_ffn1_w_down, v_mix_norm, v_w_in, v_b_in, v_conv_sc_w, v_conv_cf_w, v_conv_cf_b, v_ln_cf_g, v_ln_cf_b, v_w_out, v_ffn2_norm, v_ffn2_w_gate, v_ffn2_w_up, v_ffn2_w_down, v_final_norm):
    xi, yi, ci = lax.axis_index("x"), lax.axis_index("y"), lax.axis_index("c")
    chip = 2 * xi + yi
    place = jnp.stack([chip, ci]).astype(jnp.int32)

    x2 = x[0]
    tgt = loss_target[0]
    S, D = x2.shape
    C1 = D // 2
    cs = conv_sc_w.shape[2]
    ksc, kcf = conv_sc_w.shape[1], conv_cf_w.shape[1]
    ms = meta_tokens.shape[1]

    rows_small = N_META + 8 + 32
    assert ksc <= 8 and kcf <= 32 and cs <= ms
    pack = jnp.concatenate([
        meta_tokens,
        jnp.pad(conv_sc_w[0], ((0, 8 - ksc), (0, ms - cs))),
        jnp.pad(conv_cf_w[0], ((0, 32 - kcf), (0, ms - cs)))], axis=0)
    everyone = _share_small(pack, False, "share_params")[0::2]
    meta_full = jnp.transpose(everyone[:, :N_META, :], (1, 0, 2)).reshape(N_META, D)
    wsc_full = jnp.transpose(everyone[:, N_META:N_META + ksc, :cs], (1, 0, 2)).reshape(ksc, C1)
    wcf_full = jnp.transpose(everyone[:, N_META + 8:N_META + 8 + kcf, :cs], (1, 0, 2)).reshape(kcf, C1)

    big = {"ffn1_w_gate": ffn1_w_gate, "ffn1_w_up": ffn1_w_up, "ffn1_w_down": ffn1_w_down, "w_in": w_in, "w_out": w_out,
           "ffn2_w_gate": ffn2_w_gate, "ffn2_w_up": ffn2_w_up, "ffn2_w_down": ffn2_w_down}
    big_m = {"ffn1_w_gate": m_ffn1_w_gate, "ffn1_w_up": m_ffn1_w_up, "ffn1_w_down": m_ffn1_w_down, "w_in": m_w_in,
             "w_out": m_w_out, "ffn2_w_gate": m_ffn2_w_gate, "ffn2_w_up": m_ffn2_w_up, "ffn2_w_down": m_ffn2_w_down}
    big_v = {"ffn1_w_gate": v_ffn1_w_gate, "ffn1_w_up": v_ffn1_w_up, "ffn1_w_down": v_ffn1_w_down, "w_in": v_w_in,
             "w_out": v_w_out, "ffn2_w_gate": v_ffn2_w_gate, "ffn2_w_up": v_ffn2_w_up, "ffn2_w_down": v_ffn2_w_down}
    buf = {nm: _cast_own_block(place, w[0], "cast_" + nm) for nm, w in big.items()}
    whole_weight = lambda g: g.reshape(N_CHIPS, 2 * g.shape[2], g.shape[3])
    corner = lambda a: a.reshape(-1, a.shape[-1])[:8, :128]

    NEAR, FAR = (0, 1), (2,)
    groups = {"ffn1_near": (["ffn1_w_gate", "ffn1_w_up", "ffn1_w_down"], NEAR),
              "ffn1_far": (["ffn1_w_gate", "ffn1_w_up", "ffn1_w_down"], FAR),
              "mix": (["w_in", "w_out"], NEAR + FAR),
              "ffn2_up": (["ffn2_w_gate", "ffn2_w_up"], NEAR + FAR),
              "ffn2_down": (["ffn2_w_down"], NEAR + FAR)}
    started = {}

    def start(tag, after):
        nms, rels = groups[tag]
        copies = functools.partial(_gather_copies, rels=rels)
        send, recv, thru, token = _start_copies([buf[nm] for nm in nms], after, len(rels) * len(nms), copies,
                                                "gather_start_" + tag)
        for nm, b in zip(nms, thru):
            buf[nm] = b
        started[tag] = (send, recv, copies)
        return token

    def arrive(tag, after, then=None):
        nms, rels = groups[tag]
        send, recv, copies = started[tag]
        got = _wait_copies(send, recv, [buf[nm] for nm in nms], corner(after), copies, "gather_wait_" + tag)
        for nm, b in zip(nms, got):
            buf[nm] = b
        if then is not None:
            start(then, corner(got[0]))
        for nm, b in zip(nms, _forward_halves([buf[nm] for nm in nms], "gather_forward_" + tag, rels)):
            buf[nm] = b

    token = start("ffn1_near", corner(everyone))

    ffn1 = lambda: [whole_weight(buf[nm]) for nm in ["ffn1_w_gate", "ffn1_w_up", "ffn1_w_down"]]
    own = chip[None].astype(jnp.int32)
    near = jnp.stack([chip ^ 2, chip ^ 1]).astype(jnp.int32)
    far = (chip ^ 3)[None].astype(jnp.int32)
    all_chips = jnp.arange(N_CHIPS, dtype=jnp.int32)

    hs0, n1 = _embed_rms(x2, meta_full, ffn1_norm)
    wg1, wu1, wd1 = ffn1()
    gua = _ffn_up(n1, wg1, wu1, own, None, token, "ffn1_up_own")
    hs1 = _ffn_down(gua[2], wd1, hs0, own, "ffn1_down_own")
    arrive("ffn1_near", hs1, "ffn1_far")
    wg1, wu1, wd1 = ffn1()
    gua = _ffn_up(n1, wg1, wu1, near, gua, token, "ffn1_up_near")
    hs1 = _ffn_down(gua[2], wd1, hs1, near, "ffn1_down_near")
    arrive("ffn1_far", hs1, "mix")
    wg1, wu1, wd1 = ffn1()
    g1, u1, a1 = _ffn_up(n1, wg1, wu1, far, gua, token, "ffn1_up_far")
    hs1 = _ffn_down(a1, wd1, hs1, far, "ffn1_down_far")
    F = N_CHIPS * wd1.shape[1]
    arrive("mix", hs1, "ffn2_up")
    win, wout = whole_weight(buf["w_in"]), whole_weight(buf["w_out"])
    n2 = _rms(hs1, mix_norm, "rms_mix")
    u = _mix_in(n2, win, b_in)
    y = _mix_conv_fwd(u, wsc_full, wcf_full, conv_cf_b, ln_cf_g, ln_cf_b)
    hs2 = _mix_out(y, wout.reshape(D, D), hs1)
    arrive("ffn2_up", hs2, "ffn2_down")
    wg2, wu2 = whole_weight(buf["ffn2_w_gate"]), whole_weight(buf["ffn2_w_up"])
    n3 = _rms(hs2, ffn2_norm, "rms_ffn2")
    g2, u2, a2 = _ffn_up(n3, wg2, wu2, all_chips, None, token, "ffn2_up")
    arrive("ffn2_down", a2)
    wd2 = whole_weight(buf["ffn2_w_down"])
    hs3 = _ffn_down(a2, wd2, hs2, all_chips, "ffn2_down")
    token_ffn2 = token

    def pair_start(group, after, tag):
        gs = [g for _, g in group]
        lands = [lax.empty((N_CHIPS,) + g.shape[2:], F32) for g in gs]
        send, recv, thru, token = _start_copies(gs + lands, after, N_CHIPS * len(gs), _pair_copies,
                                                "pair_start_" + tag)
        return (group, send, recv, thru, tag), token

    def scatter_start(state, after):
        group, send, recv, thru, tag = state
        thru = _wait_copies(send, recv, thru, corner(after), _pair_copies, "pair_wait_" + tag)
        gs, sib = thru[:len(group)], thru[len(group):]
        sums = [_pair_sum(place, g, rb, "pair_sum_" + nm) for (nm, _), g, rb in zip(group, gs, sib)]
        lands = [lax.empty((3,) + q.shape[1:], BF16) for q in sums]
        send, recv, thru, token = _start_copies(sums + lands, corner(sums[-1]), 3 * len(gs), _scatter_copies,
                                                "scatter_start_" + tag)
        return ([(nm, g) for (nm, _), g in zip(group, gs)], sib, send, recv, thru, tag), token

    def reduce_finish(state, after):
        group, sib, send, recv, thru, tag = state
        lands = _wait_copies(send, recv, thru, corner(after), _scatter_copies, "scatter_wait_" + tag)[len(group):]
        mine = [_chip_sum(place, g, rb, rc, "chip_sum_" + nm) for (nm, g), rb, rc in zip(group, sib, lands)]
        whole = _half_exchange(mine, "half_exchange_" + tag)
        out = {}
        for (nm, _), g in zip(group, whole):
            w = big[nm]
            g_out, d, new_m, new_v = _adamw(w[0], g.reshape(w.shape[1:]), big_m[nm][0], big_v[nm][0], "adamw_" + nm)
            out[nm] = (g_out[None], d[None], new_m[None], new_v[None])
        return out

    dhs3, df2, loss_row, d_final = _final_loss(hs3, final_norm.reshape(1, D), tgt)

    dg2, du2 = _ffn_bwd_act(df2, wd2.reshape(F, D), g2, u2, token_ffn2, "ffn2_bwd_act")
    gw_d2 = _wgrad_down(a2, df2, "wgrad_ffn2_down")
    gw_g2 = _wgrad_cols(n3, [dg2], "wgrad_ffn2_gate")[0]
    gw_u2 = _wgrad_cols(n3, [du2], "wgrad_ffn2_up")[0]
    pair_ffn2, token = pair_start([("ffn2_w_gate", gw_g2), ("ffn2_w_up", gw_u2), ("ffn2_w_down", gw_d2)],
                                  corner(gw_u2), "ffn2")
    dn3 = _nt_panel([dg2, du2], [wg2, wu2], token, "ffn2_bwd_in")
    red_ffn2, token = scatter_start(pair_ffn2, dn3)
    dhs2, dm, d_ffn2 = _rms_bwd(dn3, hs2, ffn2_norm, dhs3, 1.0, "rms_bwd_ffn2")

    dy = _nt_panel([dm], [wout.reshape(1, D, D)], token, "mix_bwd_out")
    gw_out = _wgrad_out(y, dm)
    dz1, dcs, db, d_lg, d_lb, d_bcf = _mix_conv_bwd1(u, dy, wsc_full, wcf_full, conv_cf_b, ln_cf_g, ln_cf_b)
    du, d_bin, d_wsc, d_wcf = _mix_conv_bwd2(u, dz1, dcs, db, wsc_full, wcf_full)
    gw_in = _wgrad_cols(n2, [du], "wgrad_w_in")[0]
    pair_mix, token = pair_start([("w_in", gw_in), ("w_out", gw_out)], corner(gw_in), "mix")
    dn2 = _nt_panel([du], [win], token, "mix_bwd_in")
    red_mix, token = scatter_start(pair_mix, dn2)
    dhs1, df1, d_mix = _rms_bwd(dn2, hs1, mix_norm, dhs2, FFN_RES_SCALE, "rms_bwd_mix")

    dg1, du1 = _ffn_bwd_act(df1, wd1.reshape(F, D), g1, u1, token, "ffn1_bwd_act")
    gw_d1 = _wgrad_down(a1, df1, "wgrad_ffn1_down")
    gw_g1 = _wgrad_cols(n1, [dg1], "wgrad_ffn1_gate")[0]
    pair_ffn1a, token = pair_start([("ffn1_w_down", gw_d1), ("ffn1_w_gate", gw_g1)], corner(gw_g1), "ffn1a")
    gw_u1 = _wgrad_cols(n1, [du1], "wgrad_ffn1_up", token)[0]
    red_ffn1a, token = scatter_start(pair_ffn1a, gw_u1)
    pair_ffn1b, token = pair_start([("ffn1_w_up", gw_u1)], token, "ffn1b")
    dn1 = _nt_panel([dg1, du1], [wg1, wu1], token, "ffn1_bwd_in")
    red_ffn1b, token = scatter_start(pair_ffn1b, dn1)
    grad_x, d_meta, d_ffn1 = _rms_bwd_first(dn1, hs0, ffn1_norm, dhs1, token)

    big_out = reduce_finish(red_ffn2, grad_x)
    big_out.update(reduce_finish(red_mix, big_out["ffn2_w_down"][1]))
    big_out.update(reduce_finish(red_ffn1a, big_out["w_out"][1]))
    big_out.update(reduce_finish(red_ffn1b, big_out["ffn1_w_gate"][1]))

    W = C1
    rows = lambda a: a.reshape(-1, W)
    parts = [rows(d_ffn1), rows(d_mix), rows(d_ffn2), rows(d_final), rows(d_bin), d_bcf, d_lg, d_lb,
             d_wsc, d_wcf, rows(d_meta), jnp.broadcast_to(loss_row[:, :1], (1, W))]
    sizes = [p.shape[0] for p in parts]
    total_rows = sum(sizes)
    packed = _pad_rows(jnp.concatenate(parts, axis=0), -(-total_rows // 8) * 8)
    summed = _share_small(packed, True, "sum_small")
    offs = [0]
    for n in sizes:
        offs.append(offs[-1] + n)
    piece = lambda k: summed[offs[k]:offs[k + 1]]
    loss = piece(11)[0, 0]
    g_ffn1, g_mix, g_ffn2 = (piece(k).reshape(1, D) for k in range(3))
    g_final = piece(3).reshape(1, D)
    g_bin = piece(4).reshape(1, -1)
    g_bcf, g_lg, g_lb = piece(5), piece(6), piece(7)
    g_wsc = lax.dynamic_slice_in_dim(piece(8), chip * cs, cs, axis=1)
    g_wcf = lax.dynamic_slice_in_dim(piece(9), chip * cs, cs, axis=1)
    g_meta = lax.dynamic_slice_in_dim(piece(10).reshape(N_META, D), chip * ms, ms, axis=1)

    small_names = ["meta_tokens", "ffn1_norm", "mix_norm", "b_in", "conv_sc_w", "conv_cf_w", "conv_cf_b", "ln_cf_g",
                   "ln_cf_b", "ffn2_norm", "final_norm"]
    small_w = [meta_tokens, ffn1_norm, mix_norm, b_in, conv_sc_w[0], conv_cf_w[0], conv_cf_b, ln_cf_g, ln_cf_b,
               ffn2_norm, final_norm.reshape(1, D)]
    small_g = [g_meta, g_ffn1, g_mix, g_bin, g_wsc, g_wcf, g_bcf, g_lg, g_lb, g_ffn2, g_final]
    small_m = [m_meta_tokens, m_ffn1_norm, m_mix_norm, m_b_in, m_conv_sc_w[0], m_conv_cf_w[0], m_conv_cf_b, m_ln_cf_g,
               m_ln_cf_b, m_ffn2_norm, m_final_norm.reshape(1, D)]
    small_v = [v_meta_tokens, v_ffn1_norm, v_mix_norm, v_b_in, v_conv_sc_w[0], v_conv_cf_w[0], v_conv_cf_b, v_ln_cf_g,
               v_ln_cf_b, v_ffn2_norm, v_final_norm.reshape(1, D)]
    s_d, s_m, s_v = _adamw_small(small_w, small_g, small_m, small_v)
    shapes = {"conv_sc_w": conv_sc_w.shape, "conv_cf_w": conv_cf_w.shape, "final_norm": final_norm.shape}
    small_out = {}
    for nm, g, d, m, v in zip(small_names, small_g, s_d, s_m, s_v):
        shp = shapes.get(nm, g.shape)
        small_out[nm] = tuple(t.reshape(shp) for t in (g, d, m, v))

    order = ["meta_tokens", "ffn1_norm", "ffn1_w_gate", "ffn1_w_up", "ffn1_w_down", "mix_norm", "w_in", "b_in",
             "conv_sc_w", "conv_cf_w", "conv_cf_b", "ln_cf_g", "ln_cf_b", "w_out", "ffn2_norm", "ffn2_w_gate",
             "ffn2_w_up", "ffn2_w_down", "final_norm"]
    res = {**big_out, **small_out}
    outs = [loss, grad_x[None]]
    for q in range(4):
        outs.extend(res[nm][q] for nm in order)
    return tuple(outs)
```

```python
import functools

import jax
import jax.numpy as jnp
from jax import lax
from jax.experimental import pallas as pl
from jax.experimental.pallas import tpu as pltpu

F32 = jnp.float32
BF16 = jnp.bfloat16
MESH = pl.DeviceIdType.MESH

N_META = 16
TT = 128
PAD = TT - N_META
HALO = 32
EPS = 1e-6
FFN_RES_SCALE = 0.5
N_CHIPS = 4
N_DEV = 8

ADAM_LR = 0.001
ADAM_B1 = 0.9
ADAM_B2 = 0.999
ADAM_EPS = 1e-08
ADAM_WD = 0.01
ADAM_STEP = 10

V7X_VMEM_BYTES = 64 * 2 ** 20
NT_DIMS = (((1,), (1,)), ((), ()))
TN_DIMS = (((0,), (0,)), ((), ()))


def _params(semantics, block_bytes):
    limit = min(2 * block_bytes + 16 * 2 ** 20, V7X_VMEM_BYTES - 6 * 2 ** 20)
    return pltpu.CompilerParams(dimension_semantics=semantics, vmem_limit_bytes=int(limit))


def _pallas(body, out_shape, **kw):
    if "grid" not in kw and "grid_spec" not in kw:
        return pl.pallas_call(body, out_shape=out_shape, **kw)
    big = lambda shape, dtype: jnp.issubdtype(dtype, jnp.floating) and len(shape) >= 2
    pin_out = lambda s: pltpu.HBM(s.shape, s.dtype) if big(s.shape, s.dtype) else s
    single = not isinstance(out_shape, (list, tuple))
    shapes = pin_out(out_shape) if single else [pin_out(s) for s in out_shape]
    call = pl.pallas_call(body, out_shape=shapes, **kw)
    pin = lambda a: pltpu.with_memory_space_constraint(a, pltpu.HBM) if big(a.shape, a.dtype) else a
    return lambda *operands: call(*[pin(a) for a in operands])


def _nbytes(shape, dtype):
    n = 1
    for d in shape:
        if d is not None:
            n *= d
    return n * jnp.dtype(dtype).itemsize


def _row_tile(rows, target, mult=8):
    best = None
    for t in range(mult, min(rows, target) + 1, mult):
        if rows % t == 0:
            best = t
    assert best is not None, (rows, target, mult)
    return best


def _sigmoid(v):
    return jax.nn.sigmoid(v)


def _dsilu(v, s):
    return s * (1.0 + v * (1.0 - s))


def _embed_rms(x2, meta, gain):
    S, D = x2.shape
    T = S + TT

    def body(x_ref, meta_ref, g_ref, hs_ref, n_ref):
        i = pl.program_id(0)

        @pl.when(i == 0)
        def _():
            hs_ref[...] = jnp.zeros_like(hs_ref)
            hs_ref[PAD:, :] = meta_ref[...]

        @pl.when(i > 0)
        def _():
            hs_ref[...] = x_ref[...]

        h = hs_ref[...]
        r = lax.rsqrt(jnp.mean(h * h, axis=-1, keepdims=True) + EPS)
        n_ref[...] = ((h * r) * g_ref[...]).astype(BF16)

    blk = _nbytes((TT, D), F32) * 2 + _nbytes((TT, D), BF16)
    return _pallas(
        body, name="embed_rms", grid=(T // TT,),
        in_specs=[pl.BlockSpec((TT, D), lambda i: (jnp.maximum(i - 1, 0), 0)),
                  pl.BlockSpec((N_META, D), lambda i: (0, 0)),
                  pl.BlockSpec((1, D), lambda i: (0, 0))],
        out_specs=[pl.BlockSpec((TT, D), lambda i: (i, 0)), pl.BlockSpec((TT, D), lambda i: (i, 0))],
        out_shape=[jax.ShapeDtypeStruct((T, D), F32), jax.ShapeDtypeStruct((T, D), BF16)],
        compiler_params=_params(("parallel",), blk),
    )(x2, meta, gain)


def _rms(hs, gain, name):
    T, D = hs.shape
    te = _row_tile(T, 384)

    def body(h_ref, g_ref, n_ref):
        h = h_ref[...]
        r = lax.rsqrt(jnp.mean(h * h, axis=-1, keepdims=True) + EPS)
        n_ref[...] = ((h * r) * g_ref[...]).astype(BF16)

    blk = _nbytes((te, D), F32) + _nbytes((te, D), BF16)
    return _pallas(
        body, name=name, grid=(T // te,),
        in_specs=[pl.BlockSpec((te, D), lambda i: (i, 0)), pl.BlockSpec((1, D), lambda i: (0, 0))],
        out_specs=pl.BlockSpec((te, D), lambda i: (i, 0)),
        out_shape=jax.ShapeDtypeStruct((T, D), BF16),
        compiler_params=_params(("parallel",), blk),
    )(hs, gain)


def _rms_bwd_math(dn, h, g):
    r = lax.rsqrt(jnp.mean(h * h, axis=-1, keepdims=True) + EPS)
    xh = h * r
    dgain = jnp.sum(dn * xh, axis=0, keepdims=True)
    dxh = dn * g
    dh = r * (dxh - xh * jnp.mean(dxh * xh, axis=-1, keepdims=True))
    return dh, dgain


def _rms_bwd(dn, hs, gain, dres, scale, name):
    T, D = hs.shape
    te = _row_tile(T, 384)

    def body(dn_ref, h_ref, g_ref, dres_ref, dhs_ref, dhb_ref, dg_ref):
        dh, dgain = _rms_bwd_math(dn_ref[...], h_ref[...], g_ref[...])
        d = dres_ref[...] + dh
        dhs_ref[...] = d
        dhb_ref[...] = (scale * d).astype(BF16)

        @pl.when(pl.program_id(0) == 0)
        def _():
            dg_ref[...] = jnp.zeros_like(dg_ref)

        dg_ref[...] += dgain

    blk = _nbytes((te, D), F32) * 4 + _nbytes((te, D), BF16)
    row = lambda i: (i, 0)
    return _pallas(
        body, name=name, grid=(T // te,),
        in_specs=[pl.BlockSpec((te, D), row), pl.BlockSpec((te, D), row), pl.BlockSpec((1, D), lambda i: (0, 0)),
                  pl.BlockSpec((te, D), row)],
        out_specs=[pl.BlockSpec((te, D), row), pl.BlockSpec((te, D), row), pl.BlockSpec((1, D), lambda i: (0, 0))],
        out_shape=[jax.ShapeDtypeStruct((T, D), F32), jax.ShapeDtypeStruct((T, D), BF16),
                   jax.ShapeDtypeStruct((1, D), F32)],
        compiler_params=_params(("arbitrary",), blk),
    )(dn, hs, gain, dres)


def _rms_bwd_first(dn, hs, gain, dres, after):
    T, D = hs.shape
    S = T - TT

    def body(dn_ref, h_ref, g_ref, dres_ref, after_ref, gx_ref, gm_ref, dg_ref):
        i = pl.program_id(0)
        dh, dgain = _rms_bwd_math(dn_ref[...], h_ref[...], g_ref[...])
        d = dres_ref[...] + dh

        @pl.when(i == 0)
        def _():
            dg_ref[...] = jnp.zeros_like(dg_ref)
            gm_ref[...] = d[PAD:, :]

        @pl.when(i > 0)
        def _():
            gx_ref[...] = d

        dg_ref[...] += dgain

    blk = _nbytes((TT, D), F32) * 4
    row = lambda i: (i, 0)
    return _pallas(
        body, name="rms_bwd_ffn1", grid=(T // TT,),
        in_specs=[pl.BlockSpec((TT, D), row), pl.BlockSpec((TT, D), row), pl.BlockSpec((1, D), lambda i: (0, 0)),
                  pl.BlockSpec((TT, D), row), TOKEN],
        out_specs=[pl.BlockSpec((TT, D), lambda i: (jnp.maximum(i - 1, 0), 0)),
                   pl.BlockSpec((N_META, D), lambda i: (0, 0)), pl.BlockSpec((1, D), lambda i: (0, 0))],
        out_shape=[jax.ShapeDtypeStruct((S, D), F32), jax.ShapeDtypeStruct((N_META, D), F32),
                   jax.ShapeDtypeStruct((1, D), F32)],
        compiler_params=_params(("arbitrary",), blk),
    )(dn, hs, gain, dres, after)


def _final_loss(hs, gain, tgt):
    T, D = hs.shape

    def body(h_ref, g_ref, t_ref, dhs_ref, dhb_ref, loss_ref, dg_ref):
        i = pl.program_id(0)
        h = h_ref[...]
        g = g_ref[...]
        r = lax.rsqrt(jnp.mean(h * h, axis=-1, keepdims=True) + EPS)
        xh = h * r
        e = jnp.where(i > 0, xh * g - t_ref[...], 0.0)
        tile_loss = jnp.sum(jnp.sum(e * e, axis=1, keepdims=True), axis=0, keepdims=True) * (0.5 / D)
        dout = e * (1.0 / D)
        dgain = jnp.sum(dout * xh, axis=0, keepdims=True)
        dxh = dout * g
        d = r * (dxh - xh * jnp.mean(dxh * xh, axis=-1, keepdims=True))
        dhs_ref[...] = d
        dhb_ref[...] = (FFN_RES_SCALE * d).astype(BF16)

        @pl.when(i == 0)
        def _():
            loss_ref[...] = jnp.zeros_like(loss_ref)
            dg_ref[...] = jnp.zeros_like(dg_ref)

        loss_ref[...] += jnp.broadcast_to(tile_loss, loss_ref.shape)
        dg_ref[...] += dgain

    blk = _nbytes((TT, D), F32) * 3 + _nbytes((TT, D), BF16)
    row = lambda i: (i, 0)
    return _pallas(
        body, name="final_loss", grid=(T // TT,),
        in_specs=[pl.BlockSpec((TT, D), row), pl.BlockSpec((1, D), lambda i: (0, 0)),
                  pl.BlockSpec((TT, D), lambda i: (jnp.maximum(i - 1, 0), 0))],
        out_specs=[pl.BlockSpec((TT, D), row), pl.BlockSpec((TT, D), row),
                   pl.BlockSpec((1, 128), lambda i: (0, 0)), pl.BlockSpec((1, D), lambda i: (0, 0))],
        out_shape=[jax.ShapeDtypeStruct((T, D), F32), jax.ShapeDtypeStruct((T, D), BF16),
                   jax.ShapeDtypeStruct((1, 128), F32), jax.ShapeDtypeStruct((1, D), F32)],
        compiler_params=_params(("arbitrary",), blk),
    )(hs, gain, tgt)


MXU_COLS = 256


def _tm(T):
    return _row_tile(T, 704, 16)


def _col_chunks(n):
    return [(c, min(MXU_COLS, n - c)) for c in range(0, n, MXU_COLS)]


TOKEN = pl.BlockSpec((8, 128), lambda *_: (0, 0))


def _ffn_up(n, wg, wu, shards, prev, after, name):
    T, D = n.shape
    Fs = wg.shape[2]
    tm = _tm(T)
    nprev = 0 if prev is None else 3

    def body(shards_ref, n_ref, wg_ref, wu_ref, after_ref, *refs):
        g_ref, u_ref, a_ref = refs[nprev:]
        nn = n_ref[...]
        for c0, cw in _col_chunks(Fs):
            if 2 * cw == MXU_COLS:
                both = jnp.concatenate([wg_ref[:, c0:c0 + cw], wu_ref[:, c0:c0 + cw]], axis=1)
                gu = jnp.dot(nn, both, preferred_element_type=F32)
                g, u = gu[:, :cw], gu[:, cw:]
            else:
                g = jnp.dot(nn, wg_ref[:, c0:c0 + cw], preferred_element_type=F32)
                u = jnp.dot(nn, wu_ref[:, c0:c0 + cw], preferred_element_type=F32)
            g_ref[:, c0:c0 + cw] = g.astype(BF16)
            u_ref[:, c0:c0 + cw] = u.astype(BF16)
            a_ref[:, c0:c0 + cw] = (jax.nn.silu(g) * u).astype(BF16)

    blk = _nbytes((tm, D), BF16) + 2 * _nbytes((D, Fs), BF16) + 3 * _nbytes((tm, Fs), BF16)
    out = pl.BlockSpec((tm, Fs), lambda j, i, p: (i, p[j]))
    shp = jax.ShapeDtypeStruct((T, N_CHIPS * Fs), BF16)
    return _pallas(
        body, name=name,
        grid_spec=pltpu.PrefetchScalarGridSpec(
            num_scalar_prefetch=1, grid=(shards.shape[0], T // tm),
            in_specs=[pl.BlockSpec((tm, D), lambda j, i, p: (i, 0)),
                      pl.BlockSpec((None, D, Fs), lambda j, i, p: (p[j], 0, 0)),
                      pl.BlockSpec((None, D, Fs), lambda j, i, p: (p[j], 0, 0)), TOKEN] + [ANY] * nprev,
            out_specs=[out, out, out]),
        out_shape=[shp, shp, shp], input_output_aliases={5 + q: q for q in range(nprev)},
        compiler_params=_params(("arbitrary", "arbitrary"), blk),
    )(shards, n, wg, wu, after, *(prev or ()))


def _ffn_down(a, wd, hs, shards, name):
    T, F = a.shape
    _, Fs, D = wd.shape
    tm = _tm(T)
    tn = D // 2

    def body(shards_ref, a_ref, w_ref, h_ref, o_ref):
        part = FFN_RES_SCALE * jnp.dot(a_ref[...], w_ref[...], preferred_element_type=F32)

        @pl.when(pl.program_id(2) == 0)
        def _():
            o_ref[...] = h_ref[...] + part

        @pl.when(pl.program_id(2) > 0)
        def _():
            o_ref[...] += part

    blk = _nbytes((tm, Fs), BF16) + _nbytes((Fs, tn), BF16) + 3 * _nbytes((tm, tn), F32)
    return _pallas(
        body, name=name,
        grid_spec=pltpu.PrefetchScalarGridSpec(
            num_scalar_prefetch=1, grid=(D // tn, T // tm, shards.shape[0]),
            in_specs=[pl.BlockSpec((tm, Fs), lambda n, i, k, p: (i, p[k])),
                      pl.BlockSpec((None, Fs, tn), lambda n, i, k, p: (p[k], 0, n)),
                      pl.BlockSpec((tm, tn), lambda n, i, k, p: (i, n))],
            out_specs=pl.BlockSpec((tm, tn), lambda n, i, k, p: (i, n))),
        out_shape=jax.ShapeDtypeStruct((T, D), F32),
        compiler_params=_params(("parallel", "parallel", "arbitrary"), blk),
    )(shards, a, wd, hs)


def _mix_in(n, w, b):
    T, D = n.shape
    Ns = w.shape[2]
    tm = _tm(T)

    def body(n_ref, w_ref, b_ref, u_ref):
        u_ref[...] = jnp.dot(n_ref[...], w_ref[...], preferred_element_type=F32) + b_ref[...]

    blk = _nbytes((tm, D), BF16) + _nbytes((D, Ns), BF16) + 2 * _nbytes((tm, Ns), F32)
    return _pallas(
        body, name="mix_in", grid=(N_CHIPS, T // tm),
        in_specs=[pl.BlockSpec((tm, D), lambda j, i: (i, 0)), pl.BlockSpec((None, D, Ns), lambda j, i: (j, 0, 0)),
                  pl.BlockSpec((1, Ns), lambda j, i: (0, j))],
        out_specs=pl.BlockSpec((tm, Ns), lambda j, i: (i, j)),
        out_shape=jax.ShapeDtypeStruct((T, N_CHIPS * Ns), F32),
        compiler_params=_params(("parallel", "parallel"), blk),
    )(n, w, b)


def _mix_out(y, w, hs):
    T, D = y.shape
    tm = _tm(T)

    def body(y_ref, w_ref, h_ref, o_ref):
        o_ref[...] = h_ref[...] + jnp.dot(y_ref[...], w_ref[...], preferred_element_type=F32)

    blk = _nbytes((tm, D), BF16) + _nbytes((D, D), BF16) + 3 * _nbytes((tm, D), F32)
    return _pallas(
        body, name="mix_out", grid=(T // tm,),
        in_specs=[pl.BlockSpec((tm, D), lambda i: (i, 0)), pl.BlockSpec((D, D), lambda i: (0, 0)),
                  pl.BlockSpec((tm, D), lambda i: (i, 0))],
        out_specs=pl.BlockSpec((tm, D), lambda i: (i, 0)),
        out_shape=jax.ShapeDtypeStruct((T, D), F32),
        compiler_params=_params(("parallel",), blk),
    )(y, w, hs)


def _ffn_bwd_act(dfb, wd, g, u, after, name):
    T, D = dfb.shape
    F = wd.shape[0]
    tm = _row_tile(T, 1408, 16)
    tn = 2 * MXU_COLS

    tr = _tm(tm)

    def body(d_ref, w_ref, g_ref, u_ref, after_ref, dg_ref, du_ref):
        for r0 in range(0, tm, tr):
            dv = d_ref[r0:r0 + tr, :]
            for c0, cw in _col_chunks(tn):
                da = lax.dot_general(dv, w_ref[c0:c0 + cw, :], NT_DIMS, preferred_element_type=F32)
                gv = g_ref[r0:r0 + tr, c0:c0 + cw].astype(F32)
                uv = u_ref[r0:r0 + tr, c0:c0 + cw].astype(F32)
                s = _sigmoid(gv)
                du_ref[r0:r0 + tr, c0:c0 + cw] = (da * (gv * s)).astype(BF16)
                dg_ref[r0:r0 + tr, c0:c0 + cw] = (da * uv * _dsilu(gv, s)).astype(BF16)

    blk = _nbytes((tm, D), BF16) + _nbytes((tn, D), BF16) + 4 * _nbytes((tm, tn), BF16)
    io = pl.BlockSpec((tm, tn), lambda n, i: (i, n))
    shp = jax.ShapeDtypeStruct((T, F), BF16)
    return _pallas(
        body, name=name, grid=(F // tn, T // tm),
        in_specs=[pl.BlockSpec((tm, D), lambda n, i: (i, 0)), pl.BlockSpec((tn, D), lambda n, i: (n, 0)), io, io, TOKEN],
        out_specs=[io, io], out_shape=[shp, shp],
        compiler_params=_params(("parallel", "parallel"), blk),
    )(dfb, wd, g, u, after)


def _nt_panel(lhs_list, w_list, after, name):
    T = lhs_list[0].shape[0]
    nsh, Dout, Ks = w_list[0].shape
    npair = len(lhs_list)
    tm = _row_tile(T, 1408, 16)
    tn = Dout // 2

    def body(*refs):
        l_refs, w_refs, o_ref = refs[:npair], refs[npair:2 * npair], refs[2 * npair + 1]
        j = pl.program_id(2)
        k0 = Ks - Ks % MXU_COLS if npair == 2 and 2 * (Ks % MXU_COLS) == MXU_COLS else Ks
        acc = None
        for p in range(npair):
            part = lax.dot_general(l_refs[p][:, :k0], w_refs[p][:, :k0], NT_DIMS, preferred_element_type=F32)
            acc = part if acc is None else acc + part
        if k0 < Ks:
            lhs = jnp.concatenate([l_refs[p][:, k0:] for p in range(npair)], axis=1)
            rhs = jnp.concatenate([w_refs[p][:, k0:] for p in range(npair)], axis=1)
            acc = acc + lax.dot_general(lhs, rhs, NT_DIMS, preferred_element_type=F32)

        @pl.when(j == 0)
        def _():
            o_ref[...] = acc

        @pl.when(j > 0)
        def _():
            o_ref[...] += acc

    blk = npair * (_nbytes((tm, Ks), BF16) + _nbytes((tn, Ks), BF16)) + 2 * _nbytes((tm, tn), F32)
    return _pallas(
        body, name=name, grid=(Dout // tn, T // tm, nsh),
        in_specs=[pl.BlockSpec((tm, Ks), lambda n, i, j: (i, j))] * npair
                 + [pl.BlockSpec((None, tn, Ks), lambda n, i, j: (j, n, 0))] * npair + [TOKEN],
        out_specs=pl.BlockSpec((tm, tn), lambda n, i, j: (i, n)),
        out_shape=jax.ShapeDtypeStruct((T, Dout), F32),
        compiler_params=_params(("parallel", "parallel", "arbitrary"), blk),
    )(*lhs_list, *w_list, after)


def _tn_call(name, grid, lhs, lhs_spec, rhs_list, rhs_specs, out_shapes, out_specs, blk, after=None):
    nr = len(rhs_list)
    extra = [] if after is None else [after]

    def body(*refs):
        l_ref, r_refs, o_refs = refs[0], refs[1:1 + nr], refs[len(refs) - nr:]
        k = pl.program_id(len(grid) - 1)
        lv = l_ref[...]
        for q in range(nr):
            part = lax.dot_general(lv, r_refs[q][...], TN_DIMS, preferred_element_type=F32)
            part = part.reshape(o_refs[q].shape)

            @pl.when(k == 0)
            def _(o=o_refs[q], part=part):
                o[...] = part

            @pl.when(k > 0)
            def _(o=o_refs[q], part=part):
                o[...] += part

    return _pallas(
        body, name=name, grid=grid, in_specs=[lhs_spec] + rhs_specs + [TOKEN] * len(extra), out_specs=out_specs,
        out_shape=out_shapes, compiler_params=_params(("parallel",) * (len(grid) - 1) + ("arbitrary",), blk),
    )(lhs, *rhs_list, *extra)


def _tk(T):
    return _row_tile(T, 1408, 128)


def _wgrad_cols(n, rhs_list, name, after=None):
    T, D = n.shape
    Ns = rhs_list[0].shape[1] // N_CHIPS
    tk = _tk(T)
    nr = len(rhs_list)
    blk = _nbytes((tk, D // 2), BF16) + nr * (_nbytes((tk, Ns), BF16) + 2 * _nbytes((D // 2, Ns), F32))
    return _tn_call(
        name, (N_CHIPS, 2, T // tk), n, pl.BlockSpec((tk, D // 2), lambda j, m, k: (k, m)),
        rhs_list, [pl.BlockSpec((tk, Ns), lambda j, m, k: (k, j))] * nr,
        [jax.ShapeDtypeStruct((N_CHIPS, 2, D // 2, Ns), F32)] * nr,
        [pl.BlockSpec((None, None, D // 2, Ns), lambda j, m, k: (j, m, 0, 0))] * nr, blk, after)


def _wgrad_down(a, dfb, name):
    T, F = a.shape
    D = dfb.shape[1]
    Fs = F // N_CHIPS
    tk = _tk(T)
    tn = D // 2
    blk = _nbytes((tk, Fs), BF16) + _nbytes((tk, tn), BF16) + 2 * _nbytes((Fs, tn), F32)
    return _tn_call(
        name, (N_CHIPS, D // tn, T // tk), a, pl.BlockSpec((tk, Fs), lambda j, n, k: (k, j)),
        [dfb], [pl.BlockSpec((tk, tn), lambda j, n, k: (k, n))],
        [jax.ShapeDtypeStruct((N_CHIPS, 2, Fs // 2, D), F32)],
        [pl.BlockSpec((None, 2, Fs // 2, tn), lambda j, n, k: (j, 0, 0, n))], blk)[0]


def _wgrad_out(y, dmb):
    T, D = y.shape
    tk = _tk(T)
    tn = D // 2
    rows = D // (2 * N_CHIPS)
    blk = _nbytes((tk, D // 2), BF16) + _nbytes((tk, tn), BF16) + 2 * _nbytes((D // 2, tn), F32)
    return _tn_call(
        "wgrad_w_out", (2, D // tn, T // tk), y, pl.BlockSpec((tk, D // 2), lambda m, n, k: (k, m)),
        [dmb], [pl.BlockSpec((tk, tn), lambda m, n, k: (k, n))],
        [jax.ShapeDtypeStruct((N_CHIPS, 2, rows, D), F32)],
        [pl.BlockSpec((2, 2, rows, tn), lambda m, n, k: (m, 0, 0, n))], blk)[0]


def _row_masks(i, last):
    rows = i * TT + lax.broadcasted_iota(jnp.int32, (TT, 1), 0)
    prows = i * TT - HALO + lax.broadcasted_iota(jnp.int32, (HALO, 1), 0)
    return rows >= PAD, (prows >= PAD) & (i > 0), i < last


def _conv_inputs(u, up, mask_c, mask_p, zbuf, pbuf, C1):
    b, c, v, a, g = (u[:, k * C1:(k + 1) * C1] for k in range(5))
    cp, vp, ap, gp = (up[:, k * C1:(k + 1) * C1] for k in range(1, 5))
    sg = _sigmoid(g)
    pbuf[0:HALO, :] = jnp.where(mask_p, cp * vp, 0.0)
    pbuf[HALO:, :] = jnp.where(mask_c, c * v, 0.0)
    zbuf[0:HALO, :] = jnp.where(mask_p, ap * _sigmoid(gp), 0.0)
    zbuf[HALO:, :] = jnp.where(mask_c, a * sg, 0.0)
    return b, c, v, a, sg


SUBLANES = 8
SHIFT_ROWS = TT + HALO - SUBLANES


def _shifted_scratch(C1):
    return pltpu.VMEM((SUBLANES - 1, SHIFT_ROWS, C1), F32)


def _fill_shifted(buf, sh):
    for r in range(1, SUBLANES):
        sh[r - 1] = buf[r:r + SHIFT_ROWS, :]


LANES = 128


def _window(buf, sh, lo, c0):
    if sh is None or lo % SUBLANES == 0:
        return buf[lo:lo + TT, c0:c0 + LANES]
    q, r = divmod(lo, SUBLANES)
    return sh[r - 1, q * SUBLANES:q * SUBLANES + TT, c0:c0 + LANES]


def _tap_sum(w_ref, buf, sh, starts):
    chunks = []
    for c0 in range(0, buf.shape[1], LANES):
        acc = None
        for k, lo in enumerate(starts):
            term = w_ref[k:k + 1, c0:c0 + LANES] * _window(buf, sh, lo, c0)
            acc = term if acc is None else acc + term
        chunks.append(acc)
    return jnp.concatenate(chunks, axis=1)


def _causal_conv(w_ref, buf, sh=None):
    K = w_ref.shape[0]
    return _tap_sum(w_ref, buf, sh, [HALO - (K - 1) + k for k in range(K)])


def _anticausal_conv(w_ref, buf, sh=None):
    K = w_ref.shape[0]
    return _tap_sum(w_ref, buf, sh, [K - 1 - k for k in range(K)])


def _conv_weight_sums(dw_ref, dy, buf, sh=None):
    K = dw_ref.shape[0]
    for c0 in range(0, buf.shape[1], LANES):
        dyc = dy[:, c0:c0 + LANES]
        for k in range(K):
            prod = dyc * _window(buf, sh, HALO - (K - 1) + k, c0)
            dw_ref[k:k + 1, c0:c0 + LANES] += jnp.sum(prod, axis=0, keepdims=True)


def _layernorm_stats(z1):
    mu = jnp.mean(z1, axis=-1, keepdims=True)
    zc = z1 - mu
    rs = lax.rsqrt(jnp.mean(zc * zc, axis=-1, keepdims=True) + EPS)
    return zc * rs, rs


def _mixer_specs(T, DIN, C1, ksc, kcf):
    cur = pl.BlockSpec((TT, DIN), lambda i: (i, 0))
    prev = pl.BlockSpec((HALO, DIN), lambda i: (jnp.maximum(i * (TT // HALO) - 1, 0), 0))
    full = lambda r: pl.BlockSpec((r, C1), lambda i: (0, 0))
    return cur, prev, [full(ksc), full(kcf), full(1), full(1), full(1)]


def _mix_conv_fwd(u, wsc, wcf, bcf, lg, lb):
    T, DIN = u.shape
    C1 = DIN // 5
    last = T // TT - 1

    def body(u_ref, up_ref, wsc_ref, wcf_ref, bcf_ref, lg_ref, lb_ref, y_ref, zbuf, pbuf, zsh):
        i = pl.program_id(0)
        mask_c, mask_p, _ = _row_masks(i, last)
        b, _, _, _, _ = _conv_inputs(u_ref[...], up_ref[...], mask_c, mask_p, zbuf, pbuf, C1)
        _fill_shifted(zbuf, zsh)
        cs = _causal_conv(wsc_ref, pbuf)
        z1 = _causal_conv(wcf_ref, zbuf, zsh) + bcf_ref[...]
        zh, _ = _layernorm_stats(z1)
        ln = zh * lg_ref[...] + lb_ref[...]
        y_ref[:, 0:C1] = jnp.where(mask_c, b * cs, 0.0).astype(BF16)
        y_ref[:, C1:] = jnp.where(mask_c, jax.nn.silu(ln), 0.0).astype(BF16)

    cur, prev, small = _mixer_specs(T, DIN, C1, wsc.shape[0], wcf.shape[0])
    blk = _nbytes((TT + HALO, DIN), F32) + _nbytes((TT, 2 * C1), BF16) + 12 * _nbytes((TT + HALO, C1), F32)
    return _pallas(
        body, name="mix_conv_fwd", grid=(T // TT,),
        in_specs=[cur, prev] + small,
        out_specs=pl.BlockSpec((TT, 2 * C1), lambda i: (i, 0)),
        out_shape=jax.ShapeDtypeStruct((T, 2 * C1), BF16),
        scratch_shapes=[pltpu.VMEM((TT + HALO, C1), F32), pltpu.VMEM((TT + HALO, C1), F32), _shifted_scratch(C1)],
        compiler_params=_params(("arbitrary",), blk),
    )(u, u, wsc, wcf, bcf, lg, lb)


def _mix_conv_bwd1(u, dy, wsc, wcf, bcf, lg, lb):
    T, DIN = u.shape
    C1 = DIN // 5
    last = T // TT - 1

    def body(u_ref, up_ref, dy_ref, wsc_ref, wcf_ref, bcf_ref, lg_ref, lb_ref,
             dz1_ref, dcs_ref, db_ref, dlg_ref, dlb_ref, dbcf_ref, zbuf, pbuf, zsh):
        i = pl.program_id(0)
        mask_c, mask_p, _ = _row_masks(i, last)
        b, _, _, _, _ = _conv_inputs(u_ref[...], up_ref[...], mask_c, mask_p, zbuf, pbuf, C1)
        _fill_shifted(zbuf, zsh)
        cs = _causal_conv(wsc_ref, pbuf)
        z1 = _causal_conv(wcf_ref, zbuf, zsh) + bcf_ref[...]
        zh, rs = _layernorm_stats(z1)
        ln = zh * lg_ref[...] + lb_ref[...]
        dy = dy_ref[...]
        dysc = jnp.where(mask_c, dy[:, 0:C1], 0.0)
        dycf = jnp.where(mask_c, dy[:, C1:], 0.0)
        db_ref[...] = (dysc * cs).astype(BF16)
        dcs_ref[...] = dysc * b
        dl = dycf * _dsilu(ln, _sigmoid(ln))
        dzh = dl * lg_ref[...]
        dz1 = rs * (dzh - jnp.mean(dzh, axis=-1, keepdims=True) - zh * jnp.mean(dzh * zh, axis=-1, keepdims=True))
        dz1_ref[...] = dz1

        @pl.when(i == 0)
        def _():
            dlg_ref[...] = jnp.zeros_like(dlg_ref)
            dlb_ref[...] = jnp.zeros_like(dlb_ref)
            dbcf_ref[...] = jnp.zeros_like(dbcf_ref)

        dlg_ref[...] += jnp.sum(dl * zh, axis=0, keepdims=True)
        dlb_ref[...] += jnp.sum(dl, axis=0, keepdims=True)
        dbcf_ref[...] += jnp.sum(dz1, axis=0, keepdims=True)

    cur, prev, small = _mixer_specs(T, DIN, C1, wsc.shape[0], wcf.shape[0])
    tile = lambda: pl.BlockSpec((TT, C1), lambda i: (i, 0))
    vec = lambda: pl.BlockSpec((1, C1), lambda i: (0, 0))
    blk = _nbytes((TT + HALO, DIN), F32) + 4 * _nbytes((TT, C1), F32) + 16 * _nbytes((TT + HALO, C1), F32)
    return _pallas(
        body, name="mix_conv_bwd1", grid=(T // TT,),
        in_specs=[cur, prev, pl.BlockSpec((TT, 2 * C1), lambda i: (i, 0))] + small,
        out_specs=[tile(), tile(), tile(), vec(), vec(), vec()],
        out_shape=[jax.ShapeDtypeStruct((T, C1), F32), jax.ShapeDtypeStruct((T, C1), F32),
                   jax.ShapeDtypeStruct((T, C1), BF16)] + [jax.ShapeDtypeStruct((1, C1), F32)] * 3,
        scratch_shapes=[pltpu.VMEM((TT + HALO, C1), F32), pltpu.VMEM((TT + HALO, C1), F32), _shifted_scratch(C1)],
        compiler_params=_params(("arbitrary",), blk),
    )(u, u, dy, wsc, wcf, bcf, lg, lb)


def _mix_conv_bwd2(u, dz1, dcs, db, wsc, wcf):
    T, DIN = u.shape
    C1 = DIN // 5
    last = T // TT - 1
    ksc, kcf = wsc.shape[0], wcf.shape[0]

    def body(u_ref, up_ref, dz_ref, dzn_ref, dc_ref, dcn_ref, db_ref, wsc_ref, wcf_ref,
             du_ref, dbin_ref, dwsc_ref, dwcf_ref, zbuf, pbuf, dzbuf, dcbuf, zsh, dzsh):
        i = pl.program_id(0)
        mask_c, mask_p, has_next = _row_masks(i, last)
        _, c, v, a, sg = _conv_inputs(u_ref[...], up_ref[...], mask_c, mask_p, zbuf, pbuf, C1)
        dz1 = dz_ref[...]
        dcs = dc_ref[...]
        dzbuf[0:TT, :] = dz1
        dzbuf[TT:, :] = jnp.where(has_next, dzn_ref[...], 0.0)
        dcbuf[0:TT, :] = dcs
        dcbuf[TT:, :] = jnp.where(has_next, dcn_ref[...], 0.0)

        @pl.when(i == 0)
        def _():
            dbin_ref[...] = jnp.zeros_like(dbin_ref)
            dwsc_ref[...] = jnp.zeros_like(dwsc_ref)
            dwcf_ref[...] = jnp.zeros_like(dwcf_ref)

        _fill_shifted(zbuf, zsh)
        _fill_shifted(dzbuf, dzsh)
        _conv_weight_sums(dwcf_ref, dz1, zbuf, zsh)
        _conv_weight_sums(dwsc_ref, dcs, pbuf)
        dz0 = jnp.where(mask_c, _anticausal_conv(wcf_ref, dzbuf, dzsh), 0.0)
        dp = jnp.where(mask_c, _anticausal_conv(wsc_ref, dcbuf), 0.0)
        parts = (db_ref[...].astype(F32), dp * v, dp * c, dz0 * sg, dz0 * a * sg * (1.0 - sg))
        for k, part in enumerate(parts):
            du_ref[:, k * C1:(k + 1) * C1] = part.astype(BF16)
            dbin_ref[:, k * C1:(k + 1) * C1] += jnp.sum(part, axis=0, keepdims=True)

    cur, prev, small = _mixer_specs(T, DIN, C1, ksc, kcf)
    tile = lambda: pl.BlockSpec((TT, C1), lambda i: (i, 0))
    nxt = lambda: pl.BlockSpec((HALO, C1), lambda i: (jnp.minimum((i + 1) * (TT // HALO), T // HALO - 1), 0))
    blk = (_nbytes((TT + HALO, DIN), F32) + _nbytes((TT, DIN), BF16) + 5 * _nbytes((TT, C1), F32)
           + 16 * _nbytes((TT + HALO, C1), F32))
    buf = lambda: pltpu.VMEM((TT + HALO, C1), F32)
    return _pallas(
        body, name="mix_conv_bwd2", grid=(T // TT,),
        in_specs=[cur, prev, tile(), nxt(), tile(), nxt(), tile(), small[0], small[1]],
        out_specs=[pl.BlockSpec((TT, DIN), lambda i: (i, 0)), pl.BlockSpec((1, DIN), lambda i: (0, 0)),
                   pl.BlockSpec((ksc, C1), lambda i: (0, 0)), pl.BlockSpec((kcf, C1), lambda i: (0, 0))],
        out_shape=[jax.ShapeDtypeStruct((T, DIN), BF16), jax.ShapeDtypeStruct((1, DIN), F32),
                   jax.ShapeDtypeStruct((ksc, C1), F32), jax.ShapeDtypeStruct((kcf, C1), F32)],
        scratch_shapes=[buf(), buf(), buf(), buf(), _shifted_scratch(C1), _shifted_scratch(C1)],
        compiler_params=_params(("arbitrary",), blk),
    )(u, u, dz1, dz1, dcs, dcs, db, wsc, wcf)


def _place():
    x, y, c = lax.axis_index("x"), lax.axis_index("y"), lax.axis_index("c")
    chips = [(1 - x, y), (x, 1 - y), (1 - x, 1 - y)]
    return x, y, c, chips


ANY = pl.BlockSpec(memory_space=pl.ANY)


def _cast_own_block(place, w, name):
    R, C = w.shape
    tr = _row_tile(R // 2, 256, 16)
    nblk = R // 2 // tr

    def body(place_ref, w_ref, o_ref):
        o_ref[...] = w_ref[...].astype(BF16)

    return _pallas(
        body, name=name,
        grid_spec=pltpu.PrefetchScalarGridSpec(
            num_scalar_prefetch=1, grid=(2, nblk),
            in_specs=[pl.BlockSpec((tr, C), lambda h, i, p: (h * nblk + i, 0))],
            out_specs=pl.BlockSpec((None, None, tr, C), lambda h, i, p: (p[0], h, i, 0))),
        out_shape=jax.ShapeDtypeStruct((N_CHIPS, 2, R // 2, C), BF16),
        compiler_params=_params(("parallel", "parallel"), _nbytes((tr, C), F32) + _nbytes((tr, C), BF16)),
    )(place, w)


HBM = pl.BlockSpec(memory_space=pltpu.HBM)
SEM = pl.BlockSpec(memory_space=pltpu.SEMAPHORE)
EFFECT = pltpu.SideEffectType.DATAFLOW_SIDE_EFFECTING


def _gather_copies(refs, send, recv, rels=(0, 1, 2)):
    x, y, c, chips = _place()
    s = 2 * x + y
    n = len(rels)
    return [pltpu.make_async_remote_copy(src_ref=ref.at[s, c], dst_ref=ref.at[s, c], send_sem=send.at[n * w + k],
                                         recv_sem=recv.at[n * w + k], device_id=(*chips[r], c), device_id_type=MESH)
            for w, ref in enumerate(refs) for k, r in enumerate(rels)]


def _scatter_copies(refs, send, recv):
    x, y, c, chips = _place()
    nw = len(refs) // 2
    return [pltpu.make_async_remote_copy(src_ref=refs[w].at[2 * tx + ty], dst_ref=refs[nw + w].at[r],
                                         send_sem=send.at[3 * w + r], recv_sem=recv.at[3 * w + r],
                                         device_id=(tx, ty, c), device_id_type=MESH)
            for w in range(nw) for r, (tx, ty) in enumerate(chips)]


def _pair_copies(refs, send, recv):
    x, y, c, _ = _place()
    nw = len(refs) // 2
    return [pltpu.make_async_remote_copy(src_ref=refs[w].at[j, 1 - c], dst_ref=refs[nw + w].at[j],
                                         send_sem=send.at[N_CHIPS * w + j], recv_sem=recv.at[N_CHIPS * w + j],
                                         device_id=(x, y, 1 - c), device_id_type=MESH)
            for w in range(nw) for j in range(N_CHIPS)]


def _start_copies(bufs, after, ncopies, make_copies, name):
    n = len(bufs)

    def body(*refs):
        in_refs, send, recv, token = refs[:n], refs[n + 1], refs[n + 2], refs[2 * n + 3]
        for cp in make_copies(in_refs, send, recv):
            cp.start()
        token[...] = jnp.zeros_like(token)

    outs = _pallas(
        body, name=name, in_specs=[HBM] * n + [ANY],
        out_specs=[SEM, SEM] + [HBM] * n + [pl.BlockSpec(memory_space=pltpu.VMEM)],
        out_shape=[pltpu.SemaphoreType.DMA((ncopies,)), pltpu.SemaphoreType.DMA((ncopies,))]
                  + [pltpu.HBM(b.shape, b.dtype) for b in bufs] + [jax.ShapeDtypeStruct((8, 128), F32)],
        input_output_aliases={k: 2 + k for k in range(n)},
        compiler_params=pltpu.CompilerParams(has_side_effects=EFFECT),
    )(*[pltpu.with_memory_space_constraint(b, pltpu.HBM) for b in bufs], after)
    return outs[0], outs[1], list(outs[2:2 + n]), outs[2 + n]


def _wait_copies(send, recv, bufs, after, make_copies, name):
    n = len(bufs)

    def body(*refs):
        in_refs, send_ref, recv_ref = refs[:n], refs[n], refs[n + 1]
        for cp in make_copies(in_refs, send_ref, recv_ref):
            cp.wait_send()
            cp.wait_recv()

    outs = _pallas(
        body, name=name, in_specs=[HBM] * n + [SEM, SEM, ANY], out_specs=[HBM] * n,
        out_shape=[pltpu.HBM(b.shape, b.dtype) for b in bufs],
        input_output_aliases={k: k for k in range(n)},
        compiler_params=pltpu.CompilerParams(has_side_effects=EFFECT),
    )(*bufs, send, recv, after)
    return list(outs)


def _forward_halves(bufs, name, rels=(0, 1, 2)):
    nw = len(bufs)
    n = len(rels)

    def body(*refs):
        o_refs = refs[nw:2 * nw]
        send, recv = refs[2 * nw:]
        x, y, c, chips = _place()
        sib = (x, y, 1 - c)
        copies = []
        for w in range(nw):
            for k, r in enumerate(rels):
                tx, ty = chips[r]
                ref = o_refs[w].at[2 * tx + ty, c]
                cp = pltpu.make_async_remote_copy(src_ref=ref, dst_ref=ref, send_sem=send.at[n * w + k],
                                                  recv_sem=recv.at[n * w + k], device_id=sib, device_id_type=MESH)
                cp.start()
                copies.append(cp)
        for w in range(nw):
            for k, r in enumerate(rels):
                tx, ty = chips[r]
                ref = o_refs[w].at[2 * tx + ty, 1 - c]
                pltpu.make_async_remote_copy(src_ref=ref, dst_ref=ref, send_sem=send.at[n * w + k],
                                             recv_sem=recv.at[n * w + k], device_id=sib, device_id_type=MESH).wait_recv()
        for cp in copies:
            cp.wait_send()

    return _pallas(
        body, name=name, in_specs=[ANY] * nw, out_specs=[ANY] * nw,
        out_shape=[jax.ShapeDtypeStruct(b.shape, b.dtype) for b in bufs],
        input_output_aliases={w: w for w in range(nw)},
        scratch_shapes=[pltpu.SemaphoreType.DMA((n * nw,)), pltpu.SemaphoreType.DMA((n * nw,))],
    )(*bufs)


def _half_exchange(hs, name):
    nw = len(hs)

    def body(*refs):
        o_refs = refs[nw:2 * nw]
        send, recv = refs[2 * nw:]
        x, y, c, _ = _place()
        sib = (x, y, 1 - c)
        copies = []
        for w in range(nw):
            cp = pltpu.make_async_remote_copy(src_ref=o_refs[w].at[c], dst_ref=o_refs[w].at[c], send_sem=send.at[w],
                                              recv_sem=recv.at[w], device_id=sib, device_id_type=MESH)
            cp.start()
            copies.append(cp)
        for w, cp in enumerate(copies):
            cp.wait_send()
            pltpu.make_async_remote_copy(src_ref=o_refs[w].at[c], dst_ref=o_refs[w].at[1 - c], send_sem=send.at[w],
                                         recv_sem=recv.at[w], device_id=sib, device_id_type=MESH).wait_recv()

    return _pallas(
        body, name=name, in_specs=[ANY] * nw, out_specs=[ANY] * nw,
        out_shape=[jax.ShapeDtypeStruct(h.shape, F32) for h in hs],
        input_output_aliases={w: w for w in range(nw)},
        scratch_shapes=[pltpu.SemaphoreType.DMA((nw,)), pltpu.SemaphoreType.DMA((nw,))],
    )(*hs)


def _share_small(v, reduce, name):
    R, C = v.shape

    def body(v_ref, o_ref, *scratch):
        if reduce:
            all_ref, send, recv, lsem = scratch
        else:
            all_ref = o_ref
            send, recv, lsem = scratch
        x, y, c, _ = _place()
        me = 4 * x + 2 * y + c
        loc = pltpu.make_async_copy(v_ref, all_ref.at[me], lsem)
        loc.start()
        copies = []
        for k in range(1, N_DEV):
            kx, ky, kc = (k >> 2) & 1, (k >> 1) & 1, k & 1
            peer = (x ^ kx, y ^ ky, c ^ kc)
            cp = pltpu.make_async_remote_copy(src_ref=v_ref, dst_ref=all_ref.at[me], send_sem=send.at[k - 1],
                                              recv_sem=recv.at[k - 1], device_id=peer, device_id_type=MESH)
            cp.start()
            copies.append(cp)
        for k in range(1, N_DEV):
            kx, ky, kc = (k >> 2) & 1, (k >> 1) & 1, k & 1
            src = 4 * (x ^ kx) + 2 * (y ^ ky) + (c ^ kc)
            pltpu.make_async_remote_copy(src_ref=v_ref, dst_ref=all_ref.at[src], send_sem=send.at[k - 1],
                                         recv_sem=recv.at[k - 1], device_id=(x, y, c), device_id_type=MESH).wait_recv()
        for cp in copies:
            cp.wait_send()
        loc.wait()
        if reduce:
            total = all_ref[0]
            for d in range(1, N_DEV):
                total = total + all_ref[d]
            o_ref[...] = total

    vm = pl.BlockSpec(memory_space=pltpu.VMEM)
    sems = [pltpu.SemaphoreType.DMA((N_DEV - 1,)), pltpu.SemaphoreType.DMA((N_DEV - 1,)), pltpu.SemaphoreType.DMA]
    if reduce:
        out_shape = jax.ShapeDtypeStruct((R, C), F32)
        scratch = [pltpu.VMEM((N_DEV, R, C), F32)] + sems
    else:
        out_shape = jax.ShapeDtypeStruct((N_DEV, R, C), F32)
        scratch = sems
    return _pallas(
        body, name=name, in_specs=[vm], out_specs=vm, out_shape=out_shape, scratch_shapes=scratch,
        compiler_params=pltpu.CompilerParams(vmem_limit_bytes=int(min(4 * N_DEV * R * C * 4 + 2 ** 24, 2 ** 25 + 2 ** 24))),
    )(v)


def _pair_sum(place, g, rb, name):
    _, _, Rh, C = g.shape
    tr = _row_tile(Rh, 256, 16)

    def body(place_ref, g_ref, r_ref, q_ref):
        q_ref[...] = (g_ref[...] + r_ref[...]).astype(BF16)

    blk = 2 * _nbytes((tr, C), F32) + _nbytes((tr, C), BF16)
    return _pallas(
        body, name=name,
        grid_spec=pltpu.PrefetchScalarGridSpec(
            num_scalar_prefetch=1, grid=(N_CHIPS - 1, Rh // tr),
            in_specs=[pl.BlockSpec((None, None, tr, C), lambda j, i, p: (p[0] ^ (j + 1), p[1], i, 0)),
                      pl.BlockSpec((None, tr, C), lambda j, i, p: (p[0] ^ (j + 1), i, 0))],
            out_specs=pl.BlockSpec((None, tr, C), lambda j, i, p: (p[0] ^ (j + 1), i, 0))),
        out_shape=jax.ShapeDtypeStruct((N_CHIPS, Rh, C), BF16),
        compiler_params=_params(("parallel", "parallel"), blk),
    )(place, g, rb)


def _chip_sum(place, g, rb, rc, name):
    _, _, Rh, C = g.shape
    tr = _row_tile(Rh, 256, 16)

    def body(place_ref, g_ref, r_ref, rc_ref, o_ref):
        total = g_ref[...] + r_ref[...]
        for r in range(3):
            total = total + rc_ref[r].astype(F32)
        o_ref[...] = total

    blk = 3 * _nbytes((tr, C), F32) + 3 * _nbytes((tr, C), BF16)
    return _pallas(
        body, name=name,
        grid_spec=pltpu.PrefetchScalarGridSpec(
            num_scalar_prefetch=1, grid=(Rh // tr,),
            in_specs=[pl.BlockSpec((None, None, tr, C), lambda i, p: (p[0], p[1], i, 0)),
                      pl.BlockSpec((None, tr, C), lambda i, p: (p[0], i, 0)),
                      pl.BlockSpec((3, tr, C), lambda i, p: (0, i, 0))],
            out_specs=pl.BlockSpec((None, tr, C), lambda i, p: (p[1], i, 0))),
        out_shape=jax.ShapeDtypeStruct((2, Rh, C), F32),
        compiler_params=_params(("parallel",), blk),
    )(place, g, rb, rc)


def _adamw_math(w, g, m, v):
    m = ADAM_B1 * m + (1.0 - ADAM_B1) * g
    v = ADAM_B2 * v + (1.0 - ADAM_B2) * jnp.square(g)
    m_hat = m / (1.0 - ADAM_B1 ** ADAM_STEP)
    v_hat = v / (1.0 - ADAM_B2 ** ADAM_STEP)
    delta = -ADAM_LR * (m_hat / (jnp.sqrt(v_hat) + ADAM_EPS) + ADAM_WD * w)
    return delta, m, v


def _adamw(w, g, m, v, name):
    R, C = w.shape
    tr = _row_tile(R, 256)

    def body(w_ref, g_ref, m_ref, v_ref, go_ref, d_ref, nm_ref, nv_ref):
        gv = g_ref[...]
        d, nm, nv = _adamw_math(w_ref[...], gv, m_ref[...], v_ref[...])
        go_ref[...] = gv
        d_ref[...] = d
        nm_ref[...] = nm
        nv_ref[...] = nv

    spec = pl.BlockSpec((tr, C), lambda i: (i, 0))
    shp = jax.ShapeDtypeStruct((R, C), F32)
    return _pallas(
        body, name=name, grid=(R // tr,), in_specs=[spec] * 4, out_specs=[spec] * 4, out_shape=[shp] * 4,
        compiler_params=_params(("parallel",), 8 * _nbytes((tr, C), F32)),
    )(w, g, m, v)


def _adamw_small(ws, gs, ms, vs):
    n = len(ws)

    def body(*refs):
        for k in range(n):
            w_ref, g_ref, m_ref, v_ref = (refs[q * n + k] for q in range(4))
            d, nm, nv = _adamw_math(w_ref[...], g_ref[...], m_ref[...], v_ref[...])
            refs[4 * n + k][...] = d
            refs[5 * n + k][...] = nm
            refs[6 * n + k][...] = nv

    vm = pl.BlockSpec(memory_space=pltpu.VMEM)
    shapes = [jax.ShapeDtypeStruct(w.shape, F32) for w in ws]
    outs = _pallas(
        body, name="adamw_small", in_specs=[vm] * (4 * n), out_specs=[vm] * (3 * n), out_shape=shapes * 3,
    )(*ws, *gs, *ms, *vs)
    return outs[:n], outs[n:2 * n], outs[2 * n:]


def _pad_rows(a, rows):
    return jnp.pad(a, ((0, rows - a.shape[0]), (0, 0)))


def kernel(x, meta_tokens, ffn1_norm, ffn1_w_gate, ffn1_w_up, ffn1_w_down, mix_norm, w_in, b_in, conv_sc_w, conv_cf_w, conv_cf_b, ln_cf_g, ln_cf_b, w_out, ffn2_norm, ffn2_w_gate, ffn2_w_up, ffn2_w_down, final_norm, loss_target, m_meta_tokens, m_ffn1_norm, m_ffn1_w_gate, m_ffn1_w_up, m_ffn1_w_down, m_mix_norm, m_w_in, m_b_in, m_conv_sc_w, m_conv_cf_w, m_conv_cf_b, m_ln_cf_g, m_ln_cf_b, m_w_out, m_ffn2_norm, m_ffn2_w_gate, m_ffn2_w_up, m_ffn2_w_down, m_final_norm, v_meta_tokens, v_ffn1_norm, v_ffn1_w_gate, v_ffn1_w_up, v_ffn1_w_down, v_mix_norm, v_w_in, v_b_in, v_conv_sc_w, v_conv_cf_w, v_conv_cf_b, v_ln_cf_g, v_ln_cf_b, v_w_out, v_ffn2_norm, v_ffn2_w_gate, v_ffn2_w_up, v_ffn2_w_down, v_final_norm):
    xi, yi, ci = lax.axis_index("x"), lax.axis_index("y"), lax.axis_index("c")
    chip = 2 * xi + yi
    place = jnp.stack([chip, ci]).astype(jnp.int32)

    x2 = x[0]
    tgt = loss_target[0]
    S, D = x2.shape
    C1 = D // 2
    cs = conv_sc_w.shape[2]
    ksc, kcf = conv_sc_w.shape[1], conv_cf_w.shape[1]
    ms = meta_tokens.shape[1]

    rows_small = N_META + 8 + 32
    assert ksc <= 8 and kcf <= 32 and cs <= ms
    pack = jnp.concatenate([
        meta_tokens,
        jnp.pad(conv_sc_w[0], ((0, 8 - ksc), (0, ms - cs))),
        jnp.pad(conv_cf_w[0], ((0, 32 - kcf), (0, ms - cs)))], axis=0)
    everyone = _share_small(pack, False, "share_params")[0::2]
    meta_full = jnp.transpose(everyone[:, :N_META, :], (1, 0, 2)).reshape(N_META, D)
    wsc_full = jnp.transpose(everyone[:, N_META:N_META + ksc, :cs], (1, 0, 2)).reshape(ksc, C1)
    wcf_full = jnp.transpose(everyone[:, N_META + 8:N_META + 8 + kcf, :cs], (1, 0, 2)).reshape(kcf, C1)

    big = {"ffn1_w_gate": ffn1_w_gate, "ffn1_w_up": ffn1_w_up, "ffn1_w_down": ffn1_w_down, "w_in": w_in, "w_out": w_out,
           "ffn2_w_gate": ffn2_w_gate, "ffn2_w_up": ffn2_w_up, "ffn2_w_down": ffn2_w_down}
    big_m = {"ffn1_w_gate": m_ffn1_w_gate, "ffn1_w_up": m_ffn1_w_up, "ffn1_w_down": m_ffn1_w_down, "w_in": m_w_in,
             "w_out": m_w_out, "ffn2_w_gate": m_ffn2_w_gate, "ffn2_w_up": m_ffn2_w_up, "ffn2_w_down": m_ffn2_w_down}
    big_v = {"ffn1_w_gate": v_ffn1_w_gate, "ffn1_w_up": v_ffn1_w_up, "ffn1_w_down": v_ffn1_w_down, "w_in": v_w_in,
             "w_out": v_w_out, "ffn2_w_gate": v_ffn2_w_gate, "ffn2_w_up": v_ffn2_w_up, "ffn2_w_down": v_ffn2_w_down}
    buf = {nm: _cast_own_block(place, w[0], "cast_" + nm) for nm, w in big.items()}
    whole_weight = lambda g: g.reshape(N_CHIPS, 2 * g.shape[2], g.shape[3])
    corner = lambda a: a.reshape(-1, a.shape[-1])[:8, :128]

    NEAR, FAR = (0, 1), (2,)
    groups = {"ffn1_near": (["ffn1_w_gate", "ffn1_w_up", "ffn1_w_down"], NEAR),
              "ffn1_far": (["ffn1_w_gate", "ffn1_w_up", "ffn1_w_down"], FAR),
              "mix": (["w_in", "w_out"], NEAR + FAR),
              "ffn2_up": (["ffn2_w_gate", "ffn2_w_up"], NEAR + FAR),
              "ffn2_down": (["ffn2_w_down"], NEAR + FAR)}
    started = {}

    def start(tag, after):
        nms, rels = groups[tag]
        copies = functools.partial(_gather_copies, rels=rels)
        send, recv, thru, token = _start_copies([buf[nm] for nm in nms], after, len(rels) * len(nms), copies,
                                                "gather_start_" + tag)
        for nm, b in zip(nms, thru):
            buf[nm] = b
        started[tag] = (send, recv, copies)
        return token

    def arrive(tag, after, then=None):
        nms, rels = groups[tag]
        send, recv, copies = started[tag]
        got = _wait_copies(send, recv, [buf[nm] for nm in nms], corner(after), copies, "gather_wait_" + tag)
        for nm, b in zip(nms, got):
            buf[nm] = b
        if then is not None:
            start(then, corner(got[0]))
        for nm, b in zip(nms, _forward_halves([buf[nm] for nm in nms], "gather_forward_" + tag, rels)):
            buf[nm] = b

    token = start("ffn1_near", corner(everyone))

    ffn1 = lambda: [whole_weight(buf[nm]) for nm in ["ffn1_w_gate", "ffn1_w_up", "ffn1_w_down"]]
    own = chip[None].astype(jnp.int32)
    near = jnp.stack([chip ^ 2, chip ^ 1]).astype(jnp.int32)
    far = (chip ^ 3)[None].astype(jnp.int32)
    all_chips = jnp.arange(N_CHIPS, dtype=jnp.int32)

    hs0, n1 = _embed_rms(x2, meta_full, ffn1_norm)
    wg1, wu1, wd1 = ffn1()
    gua = _ffn_up(n1, wg1, wu1, own, None, token, "ffn1_up_own")
    hs1 = _ffn_down(gua[2], wd1, hs0, own, "ffn1_down_own")
    arrive("ffn1_near", hs1, "ffn1_far")
    wg1, wu1, wd1 = ffn1()
    gua = _ffn_up(n1, wg1, wu1, near, gua, token, "ffn1_up_near")
    start("mix", corner(gua[2]))
    hs1 = _ffn_down(gua[2], wd1, hs1, near, "ffn1_down_near")
    arrive("ffn1_far", hs1)
    wg1, wu1, wd1 = ffn1()
    g1, u1, a1 = _ffn_up(n1, wg1, wu1, far, gua, token, "ffn1_up_far")
    hs1 = _ffn_down(a1, wd1, hs1, far, "ffn1_down_far")
    F = N_CHIPS * wd1.shape[1]
    start("ffn2_up", corner(hs1))
    arrive("mix", hs1)
    win, wout = whole_weight(buf["w_in"]), whole_weight(buf["w_out"])
    n2 = _rms(hs1, mix_norm, "rms_mix")
    u = _mix_in(n2, win, b_in)
    y = _mix_conv_fwd(u, wsc_full, wcf_full, conv_cf_b, ln_cf_g, ln_cf_b)
    start("ffn2_down", corner(y))
    hs2 = _mix_out(y, wout.reshape(D, D), hs1)
    arrive("ffn2_up", hs2)
    wg2, wu2 = whole_weight(buf["ffn2_w_gate"]), whole_weight(buf["ffn2_w_up"])
    n3 = _rms(hs2, ffn2_norm, "rms_ffn2")
    g2, u2, a2 = _ffn_up(n3, wg2, wu2, all_chips, None, token, "ffn2_up")
    arrive("ffn2_down", a2)
    wd2 = whole_weight(buf["ffn2_w_down"])
    hs3 = _ffn_down(a2, wd2, hs2, all_chips, "ffn2_down")
    token_ffn2 = token

    def pair_start(group, after, tag):
        gs = [g for _, g in group]
        lands = [lax.empty((N_CHIPS,) + g.shape[2:], F32) for g in gs]
        send, recv, thru, token = _start_copies(gs + lands, after, N_CHIPS * len(gs), _pair_copies,
                                                "pair_start_" + tag)
        return (group, send, recv, thru, tag), token

    def scatter_start(state, after):
        group, send, recv, thru, tag = state
        thru = _wait_copies(send, recv, thru, corner(after), _pair_copies, "pair_wait_" + tag)
        gs, sib = thru[:len(group)], thru[len(group):]
        sums = [_pair_sum(place, g, rb, "pair_sum_" + nm) for (nm, _), g, rb in zip(group, gs, sib)]
        lands = [lax.empty((3,) + q.shape[1:], BF16) for q in sums]
        send, recv, thru, token = _start_copies(sums + lands, corner(sums[-1]), 3 * len(gs), _scatter_copies,
                                                "scatter_start_" + tag)
        return ([(nm, g) for (nm, _), g in zip(group, gs)], sib, send, recv, thru, tag), token

    def reduce_finish(state, after):
        group, sib, send, recv, thru, tag = state
        lands = _wait_copies(send, recv, thru, corner(after), _scatter_copies, "scatter_wait_" + tag)[len(group):]
        mine = [_chip_sum(place, g, rb, rc, "chip_sum_" + nm) for (nm, g), rb, rc in zip(group, sib, lands)]
        whole = _half_exchange(mine, "half_exchange_" + tag)
        out = {}
        for (nm, _), g in zip(group, whole):
            w = big[nm]
            g_out, d, new_m, new_v = _adamw(w[0], g.reshape(w.shape[1:]), big_m[nm][0], big_v[nm][0], "adamw_" + nm)
            out[nm] = (g_out[None], d[None], new_m[None], new_v[None])
        return out

    dhs3, df2, loss_row, d_final = _final_loss(hs3, final_norm.reshape(1, D), tgt)

    dg2, du2 = _ffn_bwd_act(df2, wd2.reshape(F, D), g2, u2, token_ffn2, "ffn2_bwd_act")
    gw_d2 = _wgrad_down(a2, df2, "wgrad_ffn2_down")
    gw_g2 = _wgrad_cols(n3, [dg2], "wgrad_ffn2_gate")[0]
    gw_u2 = _wgrad_cols(n3, [du2], "wgrad_ffn2_up")[0]
    pair_ffn2, token = pair_start([("ffn2_w_gate", gw_g2), ("ffn2_w_up", gw_u2), ("ffn2_w_down", gw_d2)],
                                  corner(gw_u2), "ffn2")
    dn3 = _nt_panel([dg2, du2], [wg2, wu2], token, "ffn2_bwd_in")
    red_ffn2, token = scatter_start(pair_ffn2, dn3)
    dhs2, dm, d_ffn2 = _rms_bwd(dn3, hs2, ffn2_norm, dhs3, 1.0, "rms_bwd_ffn2")

    dy = _nt_panel([dm], [wout.reshape(1, D, D)], token, "mix_bwd_out")
    gw_out = _wgrad_out(y, dm)
    dz1, dcs, db, d_lg, d_lb, d_bcf = _mix_conv_bwd1(u, dy, wsc_full, wcf_full, conv_cf_b, ln_cf_g, ln_cf_b)
    du, d_bin, d_wsc, d_wcf = _mix_conv_bwd2(u, dz1, dcs, db, wsc_full, wcf_full)
    gw_in = _wgrad_cols(n2, [du], "wgrad_w_in")[0]
    pair_mix, token = pair_start([("w_in", gw_in), ("w_out", gw_out)], corner(gw_in), "mix")
    dn2 = _nt_panel([du], [win], token, "mix_bwd_in")
    red_mix, token = scatter_start(pair_mix, dn2)
    dhs1, df1, d_mix = _rms_bwd(dn2, hs1, mix_norm, dhs2, FFN_RES_SCALE, "rms_bwd_mix")

    dg1, du1 = _ffn_bwd_act(df1, wd1.reshape(F, D), g1, u1, token, "ffn1_bwd_act")
    gw_d1 = _wgrad_down(a1, df1, "wgrad_ffn1_down")
    gw_g1 = _wgrad_cols(n1, [dg1], "wgrad_ffn1_gate")[0]
    pair_ffn1a, token = pair_start([("ffn1_w_down", gw_d1), ("ffn1_w_gate", gw_g1)], corner(gw_g1), "ffn1a")
    gw_u1 = _wgrad_cols(n1, [du1], "wgrad_ffn1_up", token)[0]
    red_ffn1a, token = scatter_start(pair_ffn1a, gw_u1)
    pair_ffn1b, token = pair_start([("ffn1_w_up", gw_u1)], token, "ffn1b")
    dn1 = _nt_panel([dg1, du1], [wg1, wu1], token, "ffn1_bwd_in")
    red_ffn1b, token = scatter_start(pair_ffn1b, dn1)
    grad_x, d_meta, d_ffn1 = _rms_bwd_first(dn1, hs0, ffn1_norm, dhs1, token)

    big_out = reduce_finish(red_ffn2, grad_x)
    big_out.update(reduce_finish(red_mix, big_out["ffn2_w_down"][1]))
    big_out.update(reduce_finish(red_ffn1a, big_out["w_out"][1]))
    big_out.update(reduce_finish(red_ffn1b, big_out["ffn1_w_gate"][1]))

    W = C1
    rows = lambda a: a.reshape(-1, W)
    parts = [rows(d_ffn1), rows(d_mix), rows(d_ffn2), rows(d_final), rows(d_bin), d_bcf, d_lg, d_lb,
             d_wsc, d_wcf, rows(d_meta), jnp.broadcast_to(loss_row[:, :1], (1, W))]
    sizes = [p.shape[0] for p in parts]
    total_rows = sum(sizes)
    packed = _pad_rows(jnp.concatenate(parts, axis=0), -(-total_rows // 8) * 8)
    summed = _share_small(packed, True, "sum_small")
    offs = [0]
    for n in sizes:
        offs.append(offs[-1] + n)
    piece = lambda k: summed[offs[k]:offs[k + 1]]
    loss = piece(11)[0, 0]
    g_ffn1, g_mix, g_ffn2 = (piece(k).reshape(1, D) for k in range(3))
    g_final = piece(3).reshape(1, D)
    g_bin = piece(4).reshape(1, -1)
    g_bcf, g_lg, g_lb = piece(5), piece(6), piece(7)
    g_wsc = lax.dynamic_slice_in_dim(piece(8), chip * cs, cs, axis=1)
    g_wcf = lax.dynamic_slice_in_dim(piece(9), chip * cs, cs, axis=1)
    g_meta = lax.dynamic_slice_in_dim(piece(10).reshape(N_META, D), chip * ms, ms, axis=1)

    small_names = ["meta_tokens", "ffn1_norm", "mix_norm", "b_in", "conv_sc_w", "conv_cf_w", "conv_cf_b", "ln_cf_g",
                   "ln_cf_b", "ffn2_norm", "final_norm"]
    small_w = [meta_tokens, ffn1_norm, mix_norm, b_in, conv_sc_w[0], conv_cf_w[0], conv_cf_b, ln_cf_g, ln_cf_b,
               ffn2_norm, final_norm.reshape(1, D)]
    small_g = [g_meta, g_ffn1, g_mix, g_bin, g_wsc, g_wcf, g_bcf, g_lg, g_lb, g_ffn2, g_final]
    small_m = [m_meta_tokens, m_ffn1_norm, m_mix_norm, m_b_in, m_conv_sc_w[0], m_conv_cf_w[0], m_conv_cf_b, m_ln_cf_g,
               m_ln_cf_b, m_ffn2_norm, m_final_norm.reshape(1, D)]
    small_v = [v_meta_tokens, v_ffn1_norm, v_mix_norm, v_b_in, v_conv_sc_w[0], v_conv_cf_w[0], v_conv_cf_b, v_ln_cf_g,
               v_ln_cf_b, v_ffn2_norm, v_final_norm.reshape(1, D)]
    s_d, s_m, s_v = _adamw_small(small_w, small_g, small_m, small_v)
    shapes = {"conv_sc_w": conv_sc_w.shape, "conv_cf_w": conv_cf_w.shape, "final_norm": final_norm.shape}
    small_out = {}
    for nm, g, d, m, v in zip(small_names, small_g, s_d, s_m, s_v):
        shp = shapes.get(nm, g.shape)
        small_out[nm] = tuple(t.reshape(shp) for t in (g, d, m, v))

    order = ["meta_tokens", "ffn1_norm", "ffn1_w_gate", "ffn1_w_up", "ffn1_w_down", "mix_norm", "w_in", "b_in",
             "conv_sc_w", "conv_cf_w", "conv_cf_b", "ln_cf_g", "ln_cf_b", "w_out", "ffn2_norm", "ffn2_w_gate",
             "ffn2_w_up", "ffn2_w_down", "final_norm"]
    res = {**big_out, **small_out}
    outs = [loss, grad_x[None]]
    for q in range(4):
        outs.extend(res[nm][q] for nm in order)
    return tuple(outs)
```

```python
import functools

import jax
import jax.numpy as jnp
from jax import lax
from jax.experimental import pallas as pl
from jax.experimental.pallas import tpu as pltpu

F32 = jnp.float32
BF16 = jnp.bfloat16
MESH = pl.DeviceIdType.MESH

N_META = 16
TT = 128
PAD = TT - N_META
HALO = 32
EPS = 1e-6
FFN_RES_SCALE = 0.5
N_CHIPS = 4
N_DEV = 8

ADAM_LR = 0.001
ADAM_B1 = 0.9
ADAM_B2 = 0.999
ADAM_EPS = 1e-08
ADAM_WD = 0.01
ADAM_STEP = 10

V7X_VMEM_BYTES = 64 * 2 ** 20
NT_DIMS = (((1,), (1,)), ((), ()))
TN_DIMS = (((0,), (0,)), ((), ()))


def _params(semantics, block_bytes):
    limit = min(2 * block_bytes + 16 * 2 ** 20, V7X_VMEM_BYTES - 6 * 2 ** 20)
    return pltpu.CompilerParams(dimension_semantics=semantics, vmem_limit_bytes=int(limit))


def _pallas(body, out_shape, **kw):
    if "grid" not in kw and "grid_spec" not in kw:
        return pl.pallas_call(body, out_shape=out_shape, **kw)
    big = lambda shape, dtype: jnp.issubdtype(dtype, jnp.floating) and len(shape) >= 2
    pin_out = lambda s: pltpu.HBM(s.shape, s.dtype) if big(s.shape, s.dtype) else s
    single = not isinstance(out_shape, (list, tuple))
    shapes = pin_out(out_shape) if single else [pin_out(s) for s in out_shape]
    call = pl.pallas_call(body, out_shape=shapes, **kw)
    pin = lambda a: pltpu.with_memory_space_constraint(a, pltpu.HBM) if big(a.shape, a.dtype) else a
    return lambda *operands: call(*[pin(a) for a in operands])


def _nbytes(shape, dtype):
    n = 1
    for d in shape:
        if d is not None:
            n *= d
    return n * jnp.dtype(dtype).itemsize


def _row_tile(rows, target, mult=8):
    best = None
    for t in range(mult, min(rows, target) + 1, mult):
        if rows % t == 0:
            best = t
    assert best is not None, (rows, target, mult)
    return best


def _sigmoid(v):
    return jax.nn.sigmoid(v)


def _dsilu(v, s):
    return s * (1.0 + v * (1.0 - s))


def _embed_rms(x2, meta, gain):
    S, D = x2.shape
    T = S + TT

    def body(x_ref, meta_ref, g_ref, hs_ref, n_ref):
        i = pl.program_id(0)

        @pl.when(i == 0)
        def _():
            hs_ref[...] = jnp.zeros_like(hs_ref)
            hs_ref[PAD:, :] = meta_ref[...]

        @pl.when(i > 0)
        def _():
            hs_ref[...] = x_ref[...]

        h = hs_ref[...]
        r = lax.rsqrt(jnp.mean(h * h, axis=-1, keepdims=True) + EPS)
        n_ref[...] = ((h * r) * g_ref[...]).astype(BF16)

    blk = _nbytes((TT, D), F32) * 2 + _nbytes((TT, D), BF16)
    return _pallas(
        body, name="embed_rms", grid=(T // TT,),
        in_specs=[pl.BlockSpec((TT, D), lambda i: (jnp.maximum(i - 1, 0), 0)),
                  pl.BlockSpec((N_META, D), lambda i: (0, 0)),
                  pl.BlockSpec((1, D), lambda i: (0, 0))],
        out_specs=[pl.BlockSpec((TT, D), lambda i: (i, 0)), pl.BlockSpec((TT, D), lambda i: (i, 0))],
        out_shape=[jax.ShapeDtypeStruct((T, D), F32), jax.ShapeDtypeStruct((T, D), BF16)],
        compiler_params=_params(("parallel",), blk),
    )(x2, meta, gain)


def _rms(hs, gain, name):
    T, D = hs.shape
    te = _row_tile(T, 384)

    def body(h_ref, g_ref, n_ref):
        h = h_ref[...]
        r = lax.rsqrt(jnp.mean(h * h, axis=-1, keepdims=True) + EPS)
        n_ref[...] = ((h * r) * g_ref[...]).astype(BF16)

    blk = _nbytes((te, D), F32) + _nbytes((te, D), BF16)
    return _pallas(
        body, name=name, grid=(T // te,),
        in_specs=[pl.BlockSpec((te, D), lambda i: (i, 0)), pl.BlockSpec((1, D), lambda i: (0, 0))],
        out_specs=pl.BlockSpec((te, D), lambda i: (i, 0)),
        out_shape=jax.ShapeDtypeStruct((T, D), BF16),
        compiler_params=_params(("parallel",), blk),
    )(hs, gain)


def _rms_bwd_math(dn, h, g):
    r = lax.rsqrt(jnp.mean(h * h, axis=-1, keepdims=True) + EPS)
    xh = h * r
    dgain = jnp.sum(dn * xh, axis=0, keepdims=True)
    dxh = dn * g
    dh = r * (dxh - xh * jnp.mean(dxh * xh, axis=-1, keepdims=True))
    return dh, dgain


def _rms_bwd(dn, hs, gain, dres, scale, name):
    T, D = hs.shape
    te = _row_tile(T, 384)

    def body(dn_ref, h_ref, g_ref, dres_ref, dhs_ref, dhb_ref, dg_ref):
        dh, dgain = _rms_bwd_math(dn_ref[...], h_ref[...], g_ref[...])
        d = dres_ref[...] + dh
        dhs_ref[...] = d
        dhb_ref[...] = (scale * d).astype(BF16)

        @pl.when(pl.program_id(0) == 0)
        def _():
            dg_ref[...] = jnp.zeros_like(dg_ref)

        dg_ref[...] += dgain

    blk = _nbytes((te, D), F32) * 4 + _nbytes((te, D), BF16)
    row = lambda i: (i, 0)
    return _pallas(
        body, name=name, grid=(T // te,),
        in_specs=[pl.BlockSpec((te, D), row), pl.BlockSpec((te, D), row), pl.BlockSpec((1, D), lambda i: (0, 0)),
                  pl.BlockSpec((te, D), row)],
        out_specs=[pl.BlockSpec((te, D), row), pl.BlockSpec((te, D), row), pl.BlockSpec((1, D), lambda i: (0, 0))],
        out_shape=[jax.ShapeDtypeStruct((T, D), F32), jax.ShapeDtypeStruct((T, D), BF16),
                   jax.ShapeDtypeStruct((1, D), F32)],
        compiler_params=_params(("arbitrary",), blk),
    )(dn, hs, gain, dres)


def _rms_bwd_first(dn, hs, gain, dres, after):
    T, D = hs.shape
    S = T - TT

    def body(dn_ref, h_ref, g_ref, dres_ref, after_ref, gx_ref, gm_ref, dg_ref):
        i = pl.program_id(0)
        dh, dgain = _rms_bwd_math(dn_ref[...], h_ref[...], g_ref[...])
        d = dres_ref[...] + dh

        @pl.when(i == 0)
        def _():
            dg_ref[...] = jnp.zeros_like(dg_ref)
            gm_ref[...] = d[PAD:, :]

        @pl.when(i > 0)
        def _():
            gx_ref[...] = d

        dg_ref[...] += dgain

    blk = _nbytes((TT, D), F32) * 4
    row = lambda i: (i, 0)
    return _pallas(
        body, name="rms_bwd_ffn1", grid=(T // TT,),
        in_specs=[pl.BlockSpec((TT, D), row), pl.BlockSpec((TT, D), row), pl.BlockSpec((1, D), lambda i: (0, 0)),
                  pl.BlockSpec((TT, D), row), TOKEN],
        out_specs=[pl.BlockSpec((TT, D), lambda i: (jnp.maximum(i - 1, 0), 0)),
                   pl.BlockSpec((N_META, D), lambda i: (0, 0)), pl.BlockSpec((1, D), lambda i: (0, 0))],
        out_shape=[jax.ShapeDtypeStruct((S, D), F32), jax.ShapeDtypeStruct((N_META, D), F32),
                   jax.ShapeDtypeStruct((1, D), F32)],
        compiler_params=_params(("arbitrary",), blk),
    )(dn, hs, gain, dres, after)


def _final_loss(hs, gain, tgt):
    T, D = hs.shape

    def body(h_ref, g_ref, t_ref, dhs_ref, dhb_ref, loss_ref, dg_ref):
        i = pl.program_id(0)
        h = h_ref[...]
        g = g_ref[...]
        r = lax.rsqrt(jnp.mean(h * h, axis=-1, keepdims=True) + EPS)
        xh = h * r
        e = jnp.where(i > 0, xh * g - t_ref[...], 0.0)
        tile_loss = jnp.sum(jnp.sum(e * e, axis=1, keepdims=True), axis=0, keepdims=True) * (0.5 / D)
        dout = e * (1.0 / D)
        dgain = jnp.sum(dout * xh, axis=0, keepdims=True)
        dxh = dout * g
        d = r * (dxh - xh * jnp.mean(dxh * xh, axis=-1, keepdims=True))
        dhs_ref[...] = d
        dhb_ref[...] = (FFN_RES_SCALE * d).astype(BF16)

        @pl.when(i == 0)
        def _():
            loss_ref[...] = jnp.zeros_like(loss_ref)
            dg_ref[...] = jnp.zeros_like(dg_ref)

        loss_ref[...] += jnp.broadcast_to(tile_loss, loss_ref.shape)
        dg_ref[...] += dgain

    blk = _nbytes((TT, D), F32) * 3 + _nbytes((TT, D), BF16)
    row = lambda i: (i, 0)
    return _pallas(
        body, name="final_loss", grid=(T // TT,),
        in_specs=[pl.BlockSpec((TT, D), row), pl.BlockSpec((1, D), lambda i: (0, 0)),
                  pl.BlockSpec((TT, D), lambda i: (jnp.maximum(i - 1, 0), 0))],
        out_specs=[pl.BlockSpec((TT, D), row), pl.BlockSpec((TT, D), row),
                   pl.BlockSpec((1, 128), lambda i: (0, 0)), pl.BlockSpec((1, D), lambda i: (0, 0))],
        out_shape=[jax.ShapeDtypeStruct((T, D), F32), jax.ShapeDtypeStruct((T, D), BF16),
                   jax.ShapeDtypeStruct((1, 128), F32), jax.ShapeDtypeStruct((1, D), F32)],
        compiler_params=_params(("arbitrary",), blk),
    )(hs, gain, tgt)


MXU_COLS = 256


def _tm(T):
    return _row_tile(T, 704, 16)


def _col_chunks(n):
    return [(c, min(MXU_COLS, n - c)) for c in range(0, n, MXU_COLS)]


TOKEN = pl.BlockSpec((8, 128), lambda *_: (0, 0))


def _ffn_up(n, wg, wu, shards, prev, after, name):
    T, D = n.shape
    Fs = wg.shape[2]
    tm = _tm(T)
    nprev = 0 if prev is None else 3

    def body(shards_ref, n_ref, wg_ref, wu_ref, after_ref, *refs):
        g_ref, u_ref, a_ref = refs[nprev:]
        nn = n_ref[...]
        for c0, cw in _col_chunks(Fs):
            if 2 * cw == MXU_COLS:
                both = jnp.concatenate([wg_ref[:, c0:c0 + cw], wu_ref[:, c0:c0 + cw]], axis=1)
                gu = jnp.dot(nn, both, preferred_element_type=F32)
                g, u = gu[:, :cw], gu[:, cw:]
            else:
                g = jnp.dot(nn, wg_ref[:, c0:c0 + cw], preferred_element_type=F32)
                u = jnp.dot(nn, wu_ref[:, c0:c0 + cw], preferred_element_type=F32)
            g_ref[:, c0:c0 + cw] = g.astype(BF16)
            u_ref[:, c0:c0 + cw] = u.astype(BF16)
            a_ref[:, c0:c0 + cw] = (jax.nn.silu(g) * u).astype(BF16)

    blk = _nbytes((tm, D), BF16) + 2 * _nbytes((D, Fs), BF16) + 3 * _nbytes((tm, Fs), BF16)
    out = pl.BlockSpec((tm, Fs), lambda j, i, p: (i, p[j]))
    shp = jax.ShapeDtypeStruct((T, N_CHIPS * Fs), BF16)
    return _pallas(
        body, name=name,
        grid_spec=pltpu.PrefetchScalarGridSpec(
            num_scalar_prefetch=1, grid=(shards.shape[0], T // tm),
            in_specs=[pl.BlockSpec((tm, D), lambda j, i, p: (i, 0)),
                      pl.BlockSpec((None, D, Fs), lambda j, i, p: (p[j], 0, 0)),
                      pl.BlockSpec((None, D, Fs), lambda j, i, p: (p[j], 0, 0)), TOKEN] + [ANY] * nprev,
            out_specs=[out, out, out]),
        out_shape=[shp, shp, shp], input_output_aliases={5 + q: q for q in range(nprev)},
        compiler_params=_params(("arbitrary", "arbitrary"), blk),
    )(shards, n, wg, wu, after, *(prev or ()))


def _ffn_down(a, wd, hs, shards, name, after=None):
    T, F = a.shape
    _, Fs, D = wd.shape
    tm = _tm(T)
    tn = D // 2
    extra = [] if after is None else [after]

    def body(shards_ref, a_ref, w_ref, h_ref, *refs):
        o_ref = refs[-1]
        part = FFN_RES_SCALE * jnp.dot(a_ref[...], w_ref[...], preferred_element_type=F32)

        @pl.when(pl.program_id(2) == 0)
        def _():
            o_ref[...] = h_ref[...] + part

        @pl.when(pl.program_id(2) > 0)
        def _():
            o_ref[...] += part

    blk = _nbytes((tm, Fs), BF16) + _nbytes((Fs, tn), BF16) + 3 * _nbytes((tm, tn), F32)
    return _pallas(
        body, name=name,
        grid_spec=pltpu.PrefetchScalarGridSpec(
            num_scalar_prefetch=1, grid=(D // tn, T // tm, shards.shape[0]),
            in_specs=[pl.BlockSpec((tm, Fs), lambda n, i, k, p: (i, p[k])),
                      pl.BlockSpec((None, Fs, tn), lambda n, i, k, p: (p[k], 0, n)),
                      pl.BlockSpec((tm, tn), lambda n, i, k, p: (i, n))] + [TOKEN] * len(extra),
            out_specs=pl.BlockSpec((tm, tn), lambda n, i, k, p: (i, n))),
        out_shape=jax.ShapeDtypeStruct((T, D), F32),
        compiler_params=_params(("parallel", "parallel", "arbitrary"), blk),
    )(shards, a, wd, hs, *extra)


def _mix_in(n, w, b):
    T, D = n.shape
    Ns = w.shape[2]
    tm = _tm(T)

    def body(n_ref, w_ref, b_ref, u_ref):
        u_ref[...] = jnp.dot(n_ref[...], w_ref[...], preferred_element_type=F32) + b_ref[...]

    blk = _nbytes((tm, D), BF16) + _nbytes((D, Ns), BF16) + 2 * _nbytes((tm, Ns), F32)
    return _pallas(
        body, name="mix_in", grid=(N_CHIPS, T // tm),
        in_specs=[pl.BlockSpec((tm, D), lambda j, i: (i, 0)), pl.BlockSpec((None, D, Ns), lambda j, i: (j, 0, 0)),
                  pl.BlockSpec((1, Ns), lambda j, i: (0, j))],
        out_specs=pl.BlockSpec((tm, Ns), lambda j, i: (i, j)),
        out_shape=jax.ShapeDtypeStruct((T, N_CHIPS * Ns), F32),
        compiler_params=_params(("parallel", "parallel"), blk),
    )(n, w, b)


def _mix_out(y, w, hs, after):
    T, D = y.shape
    tm = _tm(T)

    def body(y_ref, w_ref, h_ref, after_ref, o_ref):
        o_ref[...] = h_ref[...] + jnp.dot(y_ref[...], w_ref[...], preferred_element_type=F32)

    blk = _nbytes((tm, D), BF16) + _nbytes((D, D), BF16) + 3 * _nbytes((tm, D), F32)
    return _pallas(
        body, name="mix_out", grid=(T // tm,),
        in_specs=[pl.BlockSpec((tm, D), lambda i: (i, 0)), pl.BlockSpec((D, D), lambda i: (0, 0)),
                  pl.BlockSpec((tm, D), lambda i: (i, 0)), TOKEN],
        out_specs=pl.BlockSpec((tm, D), lambda i: (i, 0)),
        out_shape=jax.ShapeDtypeStruct((T, D), F32),
        compiler_params=_params(("parallel",), blk),
    )(y, w, hs, after)


def _ffn_bwd_act(dfb, wd, g, u, after, name):
    T, D = dfb.shape
    F = wd.shape[0]
    tm = _row_tile(T, 1408, 16)
    tn = 2 * MXU_COLS

    tr = _tm(tm)

    def body(d_ref, w_ref, g_ref, u_ref, after_ref, dg_ref, du_ref):
        for r0 in range(0, tm, tr):
            dv = d_ref[r0:r0 + tr, :]
            for c0, cw in _col_chunks(tn):
                da = lax.dot_general(dv, w_ref[c0:c0 + cw, :], NT_DIMS, preferred_element_type=F32)
                gv = g_ref[r0:r0 + tr, c0:c0 + cw].astype(F32)
                uv = u_ref[r0:r0 + tr, c0:c0 + cw].astype(F32)
                s = _sigmoid(gv)
                du_ref[r0:r0 + tr, c0:c0 + cw] = (da * (gv * s)).astype(BF16)
                dg_ref[r0:r0 + tr, c0:c0 + cw] = (da * uv * _dsilu(gv, s)).astype(BF16)

    blk = _nbytes((tm, D), BF16) + _nbytes((tn, D), BF16) + 4 * _nbytes((tm, tn), BF16)
    io = pl.BlockSpec((tm, tn), lambda n, i: (i, n))
    shp = jax.ShapeDtypeStruct((T, F), BF16)
    return _pallas(
        body, name=name, grid=(F // tn, T // tm),
        in_specs=[pl.BlockSpec((tm, D), lambda n, i: (i, 0)), pl.BlockSpec((tn, D), lambda n, i: (n, 0)), io, io, TOKEN],
        out_specs=[io, io], out_shape=[shp, shp],
        compiler_params=_params(("parallel", "parallel"), blk),
    )(dfb, wd, g, u, after)


def _nt_panel(lhs_list, w_list, after, name):
    T = lhs_list[0].shape[0]
    nsh, Dout, Ks = w_list[0].shape
    npair = len(lhs_list)
    tm = _row_tile(T, 1408, 16)
    tn = Dout // 2

    def body(*refs):
        l_refs, w_refs, o_ref = refs[:npair], refs[npair:2 * npair], refs[2 * npair + 1]
        j = pl.program_id(2)
        k0 = Ks - Ks % MXU_COLS if npair == 2 and 2 * (Ks % MXU_COLS) == MXU_COLS else Ks
        acc = None
        for p in range(npair):
            part = lax.dot_general(l_refs[p][:, :k0], w_refs[p][:, :k0], NT_DIMS, preferred_element_type=F32)
            acc = part if acc is None else acc + part
        if k0 < Ks:
            lhs = jnp.concatenate([l_refs[p][:, k0:] for p in range(npair)], axis=1)
            rhs = jnp.concatenate([w_refs[p][:, k0:] for p in range(npair)], axis=1)
            acc = acc + lax.dot_general(lhs, rhs, NT_DIMS, preferred_element_type=F32)

        @pl.when(j == 0)
        def _():
            o_ref[...] = acc

        @pl.when(j > 0)
        def _():
            o_ref[...] += acc

    blk = npair * (_nbytes((tm, Ks), BF16) + _nbytes((tn, Ks), BF16)) + 2 * _nbytes((tm, tn), F32)
    return _pallas(
        body, name=name, grid=(Dout // tn, T // tm, nsh),
        in_specs=[pl.BlockSpec((tm, Ks), lambda n, i, j: (i, j))] * npair
                 + [pl.BlockSpec((None, tn, Ks), lambda n, i, j: (j, n, 0))] * npair + [TOKEN],
        out_specs=pl.BlockSpec((tm, tn), lambda n, i, j: (i, n)),
        out_shape=jax.ShapeDtypeStruct((T, Dout), F32),
        compiler_params=_params(("parallel", "parallel", "arbitrary"), blk),
    )(*lhs_list, *w_list, after)


def _tn_call(name, grid, lhs, lhs_spec, rhs_list, rhs_specs, out_shapes, out_specs, blk, after=None):
    nr = len(rhs_list)
    extra = [] if after is None else [after]

    def body(*refs):
        l_ref, r_refs, o_refs = refs[0], refs[1:1 + nr], refs[len(refs) - nr:]
        k = pl.program_id(len(grid) - 1)
        lv = l_ref[...]
        for q in range(nr):
            part = lax.dot_general(lv, r_refs[q][...], TN_DIMS, preferred_element_type=F32)
            part = part.reshape(o_refs[q].shape)

            @pl.when(k == 0)
            def _(o=o_refs[q], part=part):
                o[...] = part

            @pl.when(k > 0)
            def _(o=o_refs[q], part=part):
                o[...] += part

    return _pallas(
        body, name=name, grid=grid, in_specs=[lhs_spec] + rhs_specs + [TOKEN] * len(extra), out_specs=out_specs,
        out_shape=out_shapes, compiler_params=_params(("parallel",) * (len(grid) - 1) + ("arbitrary",), blk),
    )(lhs, *rhs_list, *extra)


def _tk(T):
    return _row_tile(T, 1408, 128)


def _wgrad_cols(n, rhs_list, name, after=None):
    T, D = n.shape
    Ns = rhs_list[0].shape[1] // N_CHIPS
    tk = _tk(T)
    nr = len(rhs_list)
    blk = _nbytes((tk, D // 2), BF16) + nr * (_nbytes((tk, Ns), BF16) + 2 * _nbytes((D // 2, Ns), F32))
    return _tn_call(
        name, (N_CHIPS, 2, T // tk), n, pl.BlockSpec((tk, D // 2), lambda j, m, k: (k, m)),
        rhs_list, [pl.BlockSpec((tk, Ns), lambda j, m, k: (k, j))] * nr,
        [jax.ShapeDtypeStruct((N_CHIPS, 2, D // 2, Ns), F32)] * nr,
        [pl.BlockSpec((None, None, D // 2, Ns), lambda j, m, k: (j, m, 0, 0))] * nr, blk, after)


def _wgrad_down(a, dfb, name):
    T, F = a.shape
    D = dfb.shape[1]
    Fs = F // N_CHIPS
    tk = _tk(T)
    tn = D // 2
    blk = _nbytes((tk, Fs), BF16) + _nbytes((tk, tn), BF16) + 2 * _nbytes((Fs, tn), F32)
    return _tn_call(
        name, (N_CHIPS, D // tn, T // tk), a, pl.BlockSpec((tk, Fs), lambda j, n, k: (k, j)),
        [dfb], [pl.BlockSpec((tk, tn), lambda j, n, k: (k, n))],
        [jax.ShapeDtypeStruct((N_CHIPS, 2, Fs // 2, D), F32)],
        [pl.BlockSpec((None, 2, Fs // 2, tn), lambda j, n, k: (j, 0, 0, n))], blk)[0]


def _wgrad_out(y, dmb):
    T, D = y.shape
    tk = _tk(T)
    tn = D // 2
    rows = D // (2 * N_CHIPS)
    blk = _nbytes((tk, D // 2), BF16) + _nbytes((tk, tn), BF16) + 2 * _nbytes((D // 2, tn), F32)
    return _tn_call(
        "wgrad_w_out", (2, D // tn, T // tk), y, pl.BlockSpec((tk, D // 2), lambda m, n, k: (k, m)),
        [dmb], [pl.BlockSpec((tk, tn), lambda m, n, k: (k, n))],
        [jax.ShapeDtypeStruct((N_CHIPS, 2, rows, D), F32)],
        [pl.BlockSpec((2, 2, rows, tn), lambda m, n, k: (m, 0, 0, n))], blk)[0]


def _row_masks(i, last):
    rows = i * TT + lax.broadcasted_iota(jnp.int32, (TT, 1), 0)
    prows = i * TT - HALO + lax.broadcasted_iota(jnp.int32, (HALO, 1), 0)
    return rows >= PAD, (prows >= PAD) & (i > 0), i < last


def _conv_inputs(u, up, mask_c, mask_p, zbuf, pbuf, C1):
    b, c, v, a, g = (u[:, k * C1:(k + 1) * C1] for k in range(5))
    cp, vp, ap, gp = (up[:, k * C1:(k + 1) * C1] for k in range(1, 5))
    sg = _sigmoid(g)
    pbuf[0:HALO, :] = jnp.where(mask_p, cp * vp, 0.0)
    pbuf[HALO:, :] = jnp.where(mask_c, c * v, 0.0)
    zbuf[0:HALO, :] = jnp.where(mask_p, ap * _sigmoid(gp), 0.0)
    zbuf[HALO:, :] = jnp.where(mask_c, a * sg, 0.0)
    return b, c, v, a, sg


SUBLANES = 8
SHIFT_ROWS = TT + HALO - SUBLANES


def _shifted_scratch(C1):
    return pltpu.VMEM((SUBLANES - 1, SHIFT_ROWS, C1), F32)


def _fill_shifted(buf, sh):
    for r in range(1, SUBLANES):
        sh[r - 1] = buf[r:r + SHIFT_ROWS, :]


LANES = 128


def _window(buf, sh, lo, c0):
    if sh is None or lo % SUBLANES == 0:
        return buf[lo:lo + TT, c0:c0 + LANES]
    q, r = divmod(lo, SUBLANES)
    return sh[r - 1, q * SUBLANES:q * SUBLANES + TT, c0:c0 + LANES]


def _tap_sum(w_ref, buf, sh, starts):
    chunks = []
    for c0 in range(0, buf.shape[1], LANES):
        acc = None
        for k, lo in enumerate(starts):
            term = w_ref[k:k + 1, c0:c0 + LANES] * _window(buf, sh, lo, c0)
            acc = term if acc is None else acc + term
        chunks.append(acc)
    return jnp.concatenate(chunks, axis=1)


def _causal_conv(w_ref, buf, sh=None):
    K = w_ref.shape[0]
    return _tap_sum(w_ref, buf, sh, [HALO - (K - 1) + k for k in range(K)])


def _anticausal_conv(w_ref, buf, sh=None):
    K = w_ref.shape[0]
    return _tap_sum(w_ref, buf, sh, [K - 1 - k for k in range(K)])


def _conv_weight_sums(dw_ref, dy, buf, sh=None):
    K = dw_ref.shape[0]
    for c0 in range(0, buf.shape[1], LANES):
        dyc = dy[:, c0:c0 + LANES]
        for k in range(K):
            prod = dyc * _window(buf, sh, HALO - (K - 1) + k, c0)
            dw_ref[k:k + 1, c0:c0 + LANES] += jnp.sum(prod, axis=0, keepdims=True)


def _layernorm_stats(z1):
    mu = jnp.mean(z1, axis=-1, keepdims=True)
    zc = z1 - mu
    rs = lax.rsqrt(jnp.mean(zc * zc, axis=-1, keepdims=True) + EPS)
    return zc * rs, rs


def _mixer_specs(T, DIN, C1, ksc, kcf):
    cur = pl.BlockSpec((TT, DIN), lambda i: (i, 0))
    prev = pl.BlockSpec((HALO, DIN), lambda i: (jnp.maximum(i * (TT // HALO) - 1, 0), 0))
    full = lambda r: pl.BlockSpec((r, C1), lambda i: (0, 0))
    return cur, prev, [full(ksc), full(kcf), full(1), full(1), full(1)]


def _mix_conv_fwd(u, wsc, wcf, bcf, lg, lb):
    T, DIN = u.shape
    C1 = DIN // 5
    last = T // TT - 1

    def body(u_ref, up_ref, wsc_ref, wcf_ref, bcf_ref, lg_ref, lb_ref, y_ref, zbuf, pbuf, zsh):
        i = pl.program_id(0)
        mask_c, mask_p, _ = _row_masks(i, last)
        b, _, _, _, _ = _conv_inputs(u_ref[...], up_ref[...], mask_c, mask_p, zbuf, pbuf, C1)
        _fill_shifted(zbuf, zsh)
        cs = _causal_conv(wsc_ref, pbuf)
        z1 = _causal_conv(wcf_ref, zbuf, zsh) + bcf_ref[...]
        zh, _ = _layernorm_stats(z1)
        ln = zh * lg_ref[...] + lb_ref[...]
        y_ref[:, 0:C1] = jnp.where(mask_c, b * cs, 0.0).astype(BF16)
        y_ref[:, C1:] = jnp.where(mask_c, jax.nn.silu(ln), 0.0).astype(BF16)

    cur, prev, small = _mixer_specs(T, DIN, C1, wsc.shape[0], wcf.shape[0])
    blk = _nbytes((TT + HALO, DIN), F32) + _nbytes((TT, 2 * C1), BF16) + 12 * _nbytes((TT + HALO, C1), F32)
    return _pallas(
        body, name="mix_conv_fwd", grid=(T // TT,),
        in_specs=[cur, prev] + small,
        out_specs=pl.BlockSpec((TT, 2 * C1), lambda i: (i, 0)),
        out_shape=jax.ShapeDtypeStruct((T, 2 * C1), BF16),
        scratch_shapes=[pltpu.VMEM((TT + HALO, C1), F32), pltpu.VMEM((TT + HALO, C1), F32), _shifted_scratch(C1)],
        compiler_params=_params(("arbitrary",), blk),
    )(u, u, wsc, wcf, bcf, lg, lb)


def _mix_conv_bwd1(u, dy, wsc, wcf, bcf, lg, lb):
    T, DIN = u.shape
    C1 = DIN // 5
    last = T // TT - 1

    def body(u_ref, up_ref, dy_ref, wsc_ref, wcf_ref, bcf_ref, lg_ref, lb_ref,
             dz1_ref, dcs_ref, db_ref, dlg_ref, dlb_ref, dbcf_ref, zbuf, pbuf, zsh):
        i = pl.program_id(0)
        mask_c, mask_p, _ = _row_masks(i, last)
        b, _, _, _, _ = _conv_inputs(u_ref[...], up_ref[...], mask_c, mask_p, zbuf, pbuf, C1)
        _fill_shifted(zbuf, zsh)
        cs = _causal_conv(wsc_ref, pbuf)
        z1 = _causal_conv(wcf_ref, zbuf, zsh) + bcf_ref[...]
        zh, rs = _layernorm_stats(z1)
        ln = zh * lg_ref[...] + lb_ref[...]
        dy = dy_ref[...]
        dysc = jnp.where(mask_c, dy[:, 0:C1], 0.0)
        dycf = jnp.where(mask_c, dy[:, C1:], 0.0)
        db_ref[...] = (dysc * cs).astype(BF16)
        dcs_ref[...] = dysc * b
        dl = dycf * _dsilu(ln, _sigmoid(ln))
        dzh = dl * lg_ref[...]
        dz1 = rs * (dzh - jnp.mean(dzh, axis=-1, keepdims=True) - zh * jnp.mean(dzh * zh, axis=-1, keepdims=True))
        dz1_ref[...] = dz1

        @pl.when(i == 0)
        def _():
            dlg_ref[...] = jnp.zeros_like(dlg_ref)
            dlb_ref[...] = jnp.zeros_like(dlb_ref)
            dbcf_ref[...] = jnp.zeros_like(dbcf_ref)

        dlg_ref[...] += jnp.sum(dl * zh, axis=0, keepdims=True)
        dlb_ref[...] += jnp.sum(dl, axis=0, keepdims=True)
        dbcf_ref[...] += jnp.sum(dz1, axis=0, keepdims=True)

    cur, prev, small = _mixer_specs(T, DIN, C1, wsc.shape[0], wcf.shape[0])
    tile = lambda: pl.BlockSpec((TT, C1), lambda i: (i, 0))
    vec = lambda: pl.BlockSpec((1, C1), lambda i: (0, 0))
    blk = _nbytes((TT + HALO, DIN), F32) + 4 * _nbytes((TT, C1), F32) + 16 * _nbytes((TT + HALO, C1), F32)
    return _pallas(
        body, name="mix_conv_bwd1", grid=(T // TT,),
        in_specs=[cur, prev, pl.BlockSpec((TT, 2 * C1), lambda i: (i, 0))] + small,
        out_specs=[tile(), tile(), tile(), vec(), vec(), vec()],
        out_shape=[jax.ShapeDtypeStruct((T, C1), F32), jax.ShapeDtypeStruct((T, C1), F32),
                   jax.ShapeDtypeStruct((T, C1), BF16)] + [jax.ShapeDtypeStruct((1, C1), F32)] * 3,
        scratch_shapes=[pltpu.VMEM((TT + HALO, C1), F32), pltpu.VMEM((TT + HALO, C1), F32), _shifted_scratch(C1)],
        compiler_params=_params(("arbitrary",), blk),
    )(u, u, dy, wsc, wcf, bcf, lg, lb)


def _mix_conv_bwd2(u, dz1, dcs, db, wsc, wcf):
    T, DIN = u.shape
    C1 = DIN // 5
    last = T // TT - 1
    ksc, kcf = wsc.shape[0], wcf.shape[0]

    def body(u_ref, up_ref, dz_ref, dzn_ref, dc_ref, dcn_ref, db_ref, wsc_ref, wcf_ref,
             du_ref, dbin_ref, dwsc_ref, dwcf_ref, zbuf, pbuf, dzbuf, dcbuf, zsh, dzsh):
        i = pl.program_id(0)
        mask_c, mask_p, has_next = _row_masks(i, last)
        _, c, v, a, sg = _conv_inputs(u_ref[...], up_ref[...], mask_c, mask_p, zbuf, pbuf, C1)
        dz1 = dz_ref[...]
        dcs = dc_ref[...]
        dzbuf[0:TT, :] = dz1
        dzbuf[TT:, :] = jnp.where(has_next, dzn_ref[...], 0.0)
        dcbuf[0:TT, :] = dcs
        dcbuf[TT:, :] = jnp.where(has_next, dcn_ref[...], 0.0)

        @pl.when(i == 0)
        def _():
            dbin_ref[...] = jnp.zeros_like(dbin_ref)
            dwsc_ref[...] = jnp.zeros_like(dwsc_ref)
            dwcf_ref[...] = jnp.zeros_like(dwcf_ref)

        _fill_shifted(zbuf, zsh)
        _fill_shifted(dzbuf, dzsh)
        _conv_weight_sums(dwcf_ref, dz1, zbuf, zsh)
        _conv_weight_sums(dwsc_ref, dcs, pbuf)
        dz0 = jnp.where(mask_c, _anticausal_conv(wcf_ref, dzbuf, dzsh), 0.0)
        dp = jnp.where(mask_c, _anticausal_conv(wsc_ref, dcbuf), 0.0)
        parts = (db_ref[...].astype(F32), dp * v, dp * c, dz0 * sg, dz0 * a * sg * (1.0 - sg))
        for k, part in enumerate(parts):
            du_ref[:, k * C1:(k + 1) * C1] = part.astype(BF16)
            dbin_ref[:, k * C1:(k + 1) * C1] += jnp.sum(part, axis=0, keepdims=True)

    cur, prev, small = _mixer_specs(T, DIN, C1, ksc, kcf)
    tile = lambda: pl.BlockSpec((TT, C1), lambda i: (i, 0))
    nxt = lambda: pl.BlockSpec((HALO, C1), lambda i: (jnp.minimum((i + 1) * (TT // HALO), T // HALO - 1), 0))
    blk = (_nbytes((TT + HALO, DIN), F32) + _nbytes((TT, DIN), BF16) + 5 * _nbytes((TT, C1), F32)
           + 16 * _nbytes((TT + HALO, C1), F32))
    buf = lambda: pltpu.VMEM((TT + HALO, C1), F32)
    return _pallas(
        body, name="mix_conv_bwd2", grid=(T // TT,),
        in_specs=[cur, prev, tile(), nxt(), tile(), nxt(), tile(), small[0], small[1]],
        out_specs=[pl.BlockSpec((TT, DIN), lambda i: (i, 0)), pl.BlockSpec((1, DIN), lambda i: (0, 0)),
                   pl.BlockSpec((ksc, C1), lambda i: (0, 0)), pl.BlockSpec((kcf, C1), lambda i: (0, 0))],
        out_shape=[jax.ShapeDtypeStruct((T, DIN), BF16), jax.ShapeDtypeStruct((1, DIN), F32),
                   jax.ShapeDtypeStruct((ksc, C1), F32), jax.ShapeDtypeStruct((kcf, C1), F32)],
        scratch_shapes=[buf(), buf(), buf(), buf(), _shifted_scratch(C1), _shifted_scratch(C1)],
        compiler_params=_params(("arbitrary",), blk),
    )(u, u, dz1, dz1, dcs, dcs, db, wsc, wcf)


def _place():
    x, y, c = lax.axis_index("x"), lax.axis_index("y"), lax.axis_index("c")
    chips = [(1 - x, y), (x, 1 - y), (1 - x, 1 - y)]
    return x, y, c, chips


ANY = pl.BlockSpec(memory_space=pl.ANY)


def _cast_own_block(place, w, name):
    R, C = w.shape
    tr = _row_tile(R // 2, 256, 16)
    nblk = R // 2 // tr

    def body(place_ref, w_ref, o_ref):
        o_ref[...] = w_ref[...].astype(BF16)

    return _pallas(
        body, name=name,
        grid_spec=pltpu.PrefetchScalarGridSpec(
            num_scalar_prefetch=1, grid=(2, nblk),
            in_specs=[pl.BlockSpec((tr, C), lambda h, i, p: (h * nblk + i, 0))],
            out_specs=pl.BlockSpec((None, None, tr, C), lambda h, i, p: (p[0], h, i, 0))),
        out_shape=jax.ShapeDtypeStruct((N_CHIPS, 2, R // 2, C), BF16),
        compiler_params=_params(("parallel", "parallel"), _nbytes((tr, C), F32) + _nbytes((tr, C), BF16)),
    )(place, w)


HBM = pl.BlockSpec(memory_space=pltpu.HBM)
SEM = pl.BlockSpec(memory_space=pltpu.SEMAPHORE)
EFFECT = pltpu.SideEffectType.DATAFLOW_SIDE_EFFECTING


def _gather_copies(refs, send, recv, rels=(0, 1, 2)):
    x, y, c, chips = _place()
    s = 2 * x + y
    n = len(rels)
    return [pltpu.make_async_remote_copy(src_ref=ref.at[s, c], dst_ref=ref.at[s, c], send_sem=send.at[n * w + k],
                                         recv_sem=recv.at[n * w + k], device_id=(*chips[r], c), device_id_type=MESH)
            for w, ref in enumerate(refs) for k, r in enumerate(rels)]


def _scatter_copies(refs, send, recv):
    x, y, c, chips = _place()
    nw = len(refs) // 2
    return [pltpu.make_async_remote_copy(src_ref=refs[w].at[2 * tx + ty], dst_ref=refs[nw + w].at[r],
                                         send_sem=send.at[3 * w + r], recv_sem=recv.at[3 * w + r],
                                         device_id=(tx, ty, c), device_id_type=MESH)
            for w in range(nw) for r, (tx, ty) in enumerate(chips)]


def _pair_copies(refs, send, recv):
    x, y, c, _ = _place()
    nw = len(refs) // 2
    return [pltpu.make_async_remote_copy(src_ref=refs[w].at[j, 1 - c], dst_ref=refs[nw + w].at[j],
                                         send_sem=send.at[N_CHIPS * w + j], recv_sem=recv.at[N_CHIPS * w + j],
                                         device_id=(x, y, 1 - c), device_id_type=MESH)
            for w in range(nw) for j in range(N_CHIPS)]


def _start_copies(bufs, after, ncopies, make_copies, name):
    n = len(bufs)

    def body(*refs):
        in_refs, send, recv, token = refs[:n], refs[n + 1], refs[n + 2], refs[2 * n + 3]
        for cp in make_copies(in_refs, send, recv):
            cp.start()
        token[...] = jnp.zeros_like(token)

    outs = _pallas(
        body, name=name, in_specs=[HBM] * n + [ANY],
        out_specs=[SEM, SEM] + [HBM] * n + [pl.BlockSpec(memory_space=pltpu.VMEM)],
        out_shape=[pltpu.SemaphoreType.DMA((ncopies,)), pltpu.SemaphoreType.DMA((ncopies,))]
                  + [pltpu.HBM(b.shape, b.dtype) for b in bufs] + [jax.ShapeDtypeStruct((8, 128), F32)],
        input_output_aliases={k: 2 + k for k in range(n)},
        compiler_params=pltpu.CompilerParams(has_side_effects=EFFECT),
    )(*[pltpu.with_memory_space_constraint(b, pltpu.HBM) for b in bufs], after)
    return outs[0], outs[1], list(outs[2:2 + n]), outs[2 + n]


def _wait_copies(send, recv, bufs, after, make_copies, name):
    n = len(bufs)

    def body(*refs):
        in_refs, send_ref, recv_ref = refs[:n], refs[n], refs[n + 1]
        for cp in make_copies(in_refs, send_ref, recv_ref):
            cp.wait_send()
            cp.wait_recv()

    outs = _pallas(
        body, name=name, in_specs=[HBM] * n + [SEM, SEM, ANY], out_specs=[HBM] * n,
        out_shape=[pltpu.HBM(b.shape, b.dtype) for b in bufs],
        input_output_aliases={k: k for k in range(n)},
        compiler_params=pltpu.CompilerParams(has_side_effects=EFFECT),
    )(*bufs, send, recv, after)
    return list(outs)


def _forward_halves(bufs, name, rels=(0, 1, 2)):
    nw = len(bufs)
    n = len(rels)

    def body(*refs):
        o_refs = refs[nw:2 * nw]
        send, recv = refs[2 * nw:]
        x, y, c, chips = _place()
        sib = (x, y, 1 - c)
        copies = []
        for w in range(nw):
            for k, r in enumerate(rels):
                tx, ty = chips[r]
                ref = o_refs[w].at[2 * tx + ty, c]
                cp = pltpu.make_async_remote_copy(src_ref=ref, dst_ref=ref, send_sem=send.at[n * w + k],
                                                  recv_sem=recv.at[n * w + k], device_id=sib, device_id_type=MESH)
                cp.start()
                copies.append(cp)
        for w in range(nw):
            for k, r in enumerate(rels):
                tx, ty = chips[r]
                ref = o_refs[w].at[2 * tx + ty, 1 - c]
                pltpu.make_async_remote_copy(src_ref=ref, dst_ref=ref, send_sem=send.at[n * w + k],
                                             recv_sem=recv.at[n * w + k], device_id=sib, device_id_type=MESH).wait_recv()
        for cp in copies:
            cp.wait_send()

    return _pallas(
        body, name=name, in_specs=[ANY] * nw, out_specs=[ANY] * nw,
        out_shape=[jax.ShapeDtypeStruct(b.shape, b.dtype) for b in bufs],
        input_output_aliases={w: w for w in range(nw)},
        scratch_shapes=[pltpu.SemaphoreType.DMA((n * nw,)), pltpu.SemaphoreType.DMA((n * nw,))],
    )(*bufs)


def _half_exchange(hs, name):
    nw = len(hs)

    def body(*refs):
        o_refs = refs[nw:2 * nw]
        send, recv = refs[2 * nw:]
        x, y, c, _ = _place()
        sib = (x, y, 1 - c)
        copies = []
        for w in range(nw):
            cp = pltpu.make_async_remote_copy(src_ref=o_refs[w].at[c], dst_ref=o_refs[w].at[c], send_sem=send.at[w],
                                              recv_sem=recv.at[w], device_id=sib, device_id_type=MESH)
            cp.start()
            copies.append(cp)
        for w, cp in enumerate(copies):
            cp.wait_send()
            pltpu.make_async_remote_copy(src_ref=o_refs[w].at[c], dst_ref=o_refs[w].at[1 - c], send_sem=send.at[w],
                                         recv_sem=recv.at[w], device_id=sib, device_id_type=MESH).wait_recv()

    return _pallas(
        body, name=name, in_specs=[ANY] * nw, out_specs=[ANY] * nw,
        out_shape=[jax.ShapeDtypeStruct(h.shape, F32) for h in hs],
        input_output_aliases={w: w for w in range(nw)},
        scratch_shapes=[pltpu.SemaphoreType.DMA((nw,)), pltpu.SemaphoreType.DMA((nw,))],
    )(*hs)


def _share_small(v, reduce, name):
    R, C = v.shape

    def body(v_ref, o_ref, *scratch):
        if reduce:
            all_ref, send, recv, lsem = scratch
        else:
            all_ref = o_ref
            send, recv, lsem = scratch
        x, y, c, _ = _place()
        me = 4 * x + 2 * y + c
        loc = pltpu.make_async_copy(v_ref, all_ref.at[me], lsem)
        loc.start()
        copies = []
        for k in range(1, N_DEV):
            kx, ky, kc = (k >> 2) & 1, (k >> 1) & 1, k & 1
            peer = (x ^ kx, y ^ ky, c ^ kc)
            cp = pltpu.make_async_remote_copy(src_ref=v_ref, dst_ref=all_ref.at[me], send_sem=send.at[k - 1],
                                              recv_sem=recv.at[k - 1], device_id=peer, device_id_type=MESH)
            cp.start()
            copies.append(cp)
        for k in range(1, N_DEV):
            kx, ky, kc = (k >> 2) & 1, (k >> 1) & 1, k & 1
            src = 4 * (x ^ kx) + 2 * (y ^ ky) + (c ^ kc)
            pltpu.make_async_remote_copy(src_ref=v_ref, dst_ref=all_ref.at[src], send_sem=send.at[k - 1],
                                         recv_sem=recv.at[k - 1], device_id=(x, y, c), device_id_type=MESH).wait_recv()
        for cp in copies:
            cp.wait_send()
        loc.wait()
        if reduce:
            total = all_ref[0]
            for d in range(1, N_DEV):
                total = total + all_ref[d]
            o_ref[...] = total

    vm = pl.BlockSpec(memory_space=pltpu.VMEM)
    sems = [pltpu.SemaphoreType.DMA((N_DEV - 1,)), pltpu.SemaphoreType.DMA((N_DEV - 1,)), pltpu.SemaphoreType.DMA]
    if reduce:
        out_shape = jax.ShapeDtypeStruct((R, C), F32)
        scratch = [pltpu.VMEM((N_DEV, R, C), F32)] + sems
    else:
        out_shape = jax.ShapeDtypeStruct((N_DEV, R, C), F32)
        scratch = sems
    return _pallas(
        body, name=name, in_specs=[vm], out_specs=vm, out_shape=out_shape, scratch_shapes=scratch,
        compiler_params=pltpu.CompilerParams(vmem_limit_bytes=int(min(4 * N_DEV * R * C * 4 + 2 ** 24, 2 ** 25 + 2 ** 24))),
    )(v)


def _pair_sum(place, g, rb, name):
    _, _, Rh, C = g.shape
    tr = _row_tile(Rh, 256, 16)

    def body(place_ref, g_ref, r_ref, q_ref):
        q_ref[...] = (g_ref[...] + r_ref[...]).astype(BF16)

    blk = 2 * _nbytes((tr, C), F32) + _nbytes((tr, C), BF16)
    return _pallas(
        body, name=name,
        grid_spec=pltpu.PrefetchScalarGridSpec(
            num_scalar_prefetch=1, grid=(N_CHIPS - 1, Rh // tr),
            in_specs=[pl.BlockSpec((None, None, tr, C), lambda j, i, p: (p[0] ^ (j + 1), p[1], i, 0)),
                      pl.BlockSpec((None, tr, C), lambda j, i, p: (p[0] ^ (j + 1), i, 0))],
            out_specs=pl.BlockSpec((None, tr, C), lambda j, i, p: (p[0] ^ (j + 1), i, 0))),
        out_shape=jax.ShapeDtypeStruct((N_CHIPS, Rh, C), BF16),
        compiler_params=_params(("parallel", "parallel"), blk),
    )(place, g, rb)


def _chip_sum(place, g, rb, rc, name):
    _, _, Rh, C = g.shape
    tr = _row_tile(Rh, 256, 16)

    def body(place_ref, g_ref, r_ref, rc_ref, o_ref):
        total = g_ref[...] + r_ref[...]
        for r in range(3):
            total = total + rc_ref[r].astype(F32)
        o_ref[...] = total

    blk = 3 * _nbytes((tr, C), F32) + 3 * _nbytes((tr, C), BF16)
    return _pallas(
        body, name=name,
        grid_spec=pltpu.PrefetchScalarGridSpec(
            num_scalar_prefetch=1, grid=(Rh // tr,),
            in_specs=[pl.BlockSpec((None, None, tr, C), lambda i, p: (p[0], p[1], i, 0)),
                      pl.BlockSpec((None, tr, C), lambda i, p: (p[0], i, 0)),
                      pl.BlockSpec((3, tr, C), lambda i, p: (0, i, 0))],
            out_specs=pl.BlockSpec((None, tr, C), lambda i, p: (p[1], i, 0))),
        out_shape=jax.ShapeDtypeStruct((2, Rh, C), F32),
        compiler_params=_params(("parallel",), blk),
    )(place, g, rb, rc)


def _adamw_math(w, g, m, v):
    m = ADAM_B1 * m + (1.0 - ADAM_B1) * g
    v = ADAM_B2 * v + (1.0 - ADAM_B2) * jnp.square(g)
    m_hat = m / (1.0 - ADAM_B1 ** ADAM_STEP)
    v_hat = v / (1.0 - ADAM_B2 ** ADAM_STEP)
    delta = -ADAM_LR * (m_hat / (jnp.sqrt(v_hat) + ADAM_EPS) + ADAM_WD * w)
    return delta, m, v


def _adamw(w, g, m, v, name):
    R, C = w.shape
    tr = _row_tile(R, 256)

    def body(w_ref, g_ref, m_ref, v_ref, go_ref, d_ref, nm_ref, nv_ref):
        gv = g_ref[...]
        d, nm, nv = _adamw_math(w_ref[...], gv, m_ref[...], v_ref[...])
        go_ref[...] = gv
        d_ref[...] = d
        nm_ref[...] = nm
        nv_ref[...] = nv

    spec = pl.BlockSpec((tr, C), lambda i: (i, 0))
    shp = jax.ShapeDtypeStruct((R, C), F32)
    return _pallas(
        body, name=name, grid=(R // tr,), in_specs=[spec] * 4, out_specs=[spec] * 4, out_shape=[shp] * 4,
        compiler_params=_params(("parallel",), 8 * _nbytes((tr, C), F32)),
    )(w, g, m, v)


def _adamw_small(ws, gs, ms, vs):
    n = len(ws)

    def body(*refs):
        for k in range(n):
            w_ref, g_ref, m_ref, v_ref = (refs[q * n + k] for q in range(4))
            d, nm, nv = _adamw_math(w_ref[...], g_ref[...], m_ref[...], v_ref[...])
            refs[4 * n + k][...] = d
            refs[5 * n + k][...] = nm
            refs[6 * n + k][...] = nv

    vm = pl.BlockSpec(memory_space=pltpu.VMEM)
    shapes = [jax.ShapeDtypeStruct(w.shape, F32) for w in ws]
    outs = _pallas(
        body, name="adamw_small", in_specs=[vm] * (4 * n), out_specs=[vm] * (3 * n), out_shape=shapes * 3,
    )(*ws, *gs, *ms, *vs)
    return outs[:n], outs[n:2 * n], outs[2 * n:]


def _pad_rows(a, rows):
    return jnp.pad(a, ((0, rows - a.shape[0]), (0, 0)))


def kernel(x, meta_tokens, ffn1_norm, ffn1_w_gate, ffn1_w_up, ffn1_w_down, mix_norm, w_in, b_in, conv_sc_w, conv_cf_w, conv_cf_b, ln_cf_g, ln_cf_b, w_out, ffn2_norm, ffn2_w_gate, ffn2_w_up, ffn2_w_down, final_norm, loss_target, m_meta_tokens, m_ffn1_norm, m_ffn1_w_gate, m_ffn1_w_up, m_ffn1_w_down, m_mix_norm, m_w_in, m_b_in, m_conv_sc_w, m_conv_cf_w, m_conv_cf_b, m_ln_cf_g, m_ln_cf_b, m_w_out, m_ffn2_norm, m_ffn2_w_gate, m_ffn2_w_up, m_ffn2_w_down, m_final_norm, v_meta_tokens, v_ffn1_norm, v_ffn1_w_gate, v_ffn1_w_up, v_ffn1_w_down, v_mix_norm, v_w_in, v_b_in, v_conv_sc_w, v_conv_cf_w, v_conv_cf_b, v_ln_cf_g, v_ln_cf_b, v_w_out, v_ffn2_norm, v_ffn2_w_gate, v_ffn2_w_up, v_ffn2_w_down, v_final_norm):
    xi, yi, ci = lax.axis_index("x"), lax.axis_index("y"), lax.axis_index("c")
    chip = 2 * xi + yi
    place = jnp.stack([chip, ci]).astype(jnp.int32)

    x2 = x[0]
    tgt = loss_target[0]
    S, D = x2.shape
    C1 = D // 2
    cs = conv_sc_w.shape[2]
    ksc, kcf = conv_sc_w.shape[1], conv_cf_w.shape[1]
    ms = meta_tokens.shape[1]

    rows_small = N_META + 8 + 32
    assert ksc <= 8 and kcf <= 32 and cs <= ms
    pack = jnp.concatenate([
        meta_tokens,
        jnp.pad(conv_sc_w[0], ((0, 8 - ksc), (0, ms - cs))),
        jnp.pad(conv_cf_w[0], ((0, 32 - kcf), (0, ms - cs)))], axis=0)
    everyone = _share_small(pack, False, "share_params")[0::2]
    meta_full = jnp.transpose(everyone[:, :N_META, :], (1, 0, 2)).reshape(N_META, D)
    wsc_full = jnp.transpose(everyone[:, N_META:N_META + ksc, :cs], (1, 0, 2)).reshape(ksc, C1)
    wcf_full = jnp.transpose(everyone[:, N_META + 8:N_META + 8 + kcf, :cs], (1, 0, 2)).reshape(kcf, C1)

    big = {"ffn1_w_gate": ffn1_w_gate, "ffn1_w_up": ffn1_w_up, "ffn1_w_down": ffn1_w_down, "w_in": w_in, "w_out": w_out,
           "ffn2_w_gate": ffn2_w_gate, "ffn2_w_up": ffn2_w_up, "ffn2_w_down": ffn2_w_down}
    big_m = {"ffn1_w_gate": m_ffn1_w_gate, "ffn1_w_up": m_ffn1_w_up, "ffn1_w_down": m_ffn1_w_down, "w_in": m_w_in,
             "w_out": m_w_out, "ffn2_w_gate": m_ffn2_w_gate, "ffn2_w_up": m_ffn2_w_up, "ffn2_w_down": m_ffn2_w_down}
    big_v = {"ffn1_w_gate": v_ffn1_w_gate, "ffn1_w_up": v_ffn1_w_up, "ffn1_w_down": v_ffn1_w_down, "w_in": v_w_in,
             "w_out": v_w_out, "ffn2_w_gate": v_ffn2_w_gate, "ffn2_w_up": v_ffn2_w_up, "ffn2_w_down": v_ffn2_w_down}
    buf = {nm: _cast_own_block(place, w[0], "cast_" + nm) for nm, w in big.items()}
    whole_weight = lambda g: g.reshape(N_CHIPS, 2 * g.shape[2], g.shape[3])
    corner = lambda a: a.reshape(-1, a.shape[-1])[:8, :128]

    NEAR, FAR = (0, 1), (2,)
    groups = {"ffn1_near": (["ffn1_w_gate", "ffn1_w_up", "ffn1_w_down"], NEAR),
              "ffn1_far": (["ffn1_w_gate", "ffn1_w_up", "ffn1_w_down"], FAR),
              "mix": (["w_in", "w_out"], NEAR + FAR),
              "ffn2_up": (["ffn2_w_gate", "ffn2_w_up"], NEAR + FAR),
              "ffn2_down": (["ffn2_w_down"], NEAR + FAR)}
    started = {}

    def start(tag, after):
        nms, rels = groups[tag]
        copies = functools.partial(_gather_copies, rels=rels)
        send, recv, thru, token = _start_copies([buf[nm] for nm in nms], after, len(rels) * len(nms), copies,
                                                "gather_start_" + tag)
        for nm, b in zip(nms, thru):
            buf[nm] = b
        started[tag] = (send, recv, copies)
        return token

    def arrive(tag, after, then=None):
        nms, rels = groups[tag]
        send, recv, copies = started[tag]
        got = _wait_copies(send, recv, [buf[nm] for nm in nms], corner(after), copies, "gather_wait_" + tag)
        for nm, b in zip(nms, got):
            buf[nm] = b
        if then is not None:
            start(then, corner(got[0]))
        for nm, b in zip(nms, _forward_halves([buf[nm] for nm in nms], "gather_forward_" + tag, rels)):
            buf[nm] = b

    token = start("ffn1_near", corner(everyone))

    ffn1 = lambda: [whole_weight(buf[nm]) for nm in ["ffn1_w_gate", "ffn1_w_up", "ffn1_w_down"]]
    own = chip[None].astype(jnp.int32)
    near = jnp.stack([chip ^ 2, chip ^ 1]).astype(jnp.int32)
    far = (chip ^ 3)[None].astype(jnp.int32)
    all_chips = jnp.arange(N_CHIPS, dtype=jnp.int32)

    hs0, n1 = _embed_rms(x2, meta_full, ffn1_norm)
    wg1, wu1, wd1 = ffn1()
    gua = _ffn_up(n1, wg1, wu1, own, None, token, "ffn1_up_own")
    hs1 = _ffn_down(gua[2], wd1, hs0, own, "ffn1_down_own")
    arrive("ffn1_near", hs1, "ffn1_far")
    wg1, wu1, wd1 = ffn1()
    gua = _ffn_up(n1, wg1, wu1, near, gua, token, "ffn1_up_near")
    tok = start("mix", corner(gua[2]))
    hs1 = _ffn_down(gua[2], wd1, hs1, near, "ffn1_down_near", tok)
    arrive("ffn1_far", hs1)
    wg1, wu1, wd1 = ffn1()
    g1, u1, a1 = _ffn_up(n1, wg1, wu1, far, gua, token, "ffn1_up_far")
    hs1 = _ffn_down(a1, wd1, hs1, far, "ffn1_down_far")
    F = N_CHIPS * wd1.shape[1]
    tok = start("ffn2_up", corner(hs1))
    arrive("mix", tok)
    win, wout = whole_weight(buf["w_in"]), whole_weight(buf["w_out"])
    n2 = _rms(hs1, mix_norm, "rms_mix")
    u = _mix_in(n2, win, b_in)
    y = _mix_conv_fwd(u, wsc_full, wcf_full, conv_cf_b, ln_cf_g, ln_cf_b)
    tok = start("ffn2_down", corner(y))
    hs2 = _mix_out(y, wout.reshape(D, D), hs1, tok)
    arrive("ffn2_up", hs2)
    wg2, wu2 = whole_weight(buf["ffn2_w_gate"]), whole_weight(buf["ffn2_w_up"])
    n3 = _rms(hs2, ffn2_norm, "rms_ffn2")
    g2, u2, a2 = _ffn_up(n3, wg2, wu2, all_chips, None, token, "ffn2_up")
    arrive("ffn2_down", a2)
    wd2 = whole_weight(buf["ffn2_w_down"])
    hs3 = _ffn_down(a2, wd2, hs2, all_chips, "ffn2_down")
    token_ffn2 = token

    def pair_start(group, after, tag):
        gs = [g for _, g in group]
        lands = [lax.empty((N_CHIPS,) + g.shape[2:], F32) for g in gs]
        send, recv, thru, token = _start_copies(gs + lands, after, N_CHIPS * len(gs), _pair_copies,
                                                "pair_start_" + tag)
        return (group, send, recv, thru, tag), token

    def scatter_start(state, after):
        group, send, recv, thru, tag = state
        thru = _wait_copies(send, recv, thru, corner(after), _pair_copies, "pair_wait_" + tag)
        gs, sib = thru[:len(group)], thru[len(group):]
        sums = [_pair_sum(place, g, rb, "pair_sum_" + nm) for (nm, _), g, rb in zip(group, gs, sib)]
        lands = [lax.empty((3,) + q.shape[1:], BF16) for q in sums]
        send, recv, thru, token = _start_copies(sums + lands, corner(sums[-1]), 3 * len(gs), _scatter_copies,
                                                "scatter_start_" + tag)
        return ([(nm, g) for (nm, _), g in zip(group, gs)], sib, send, recv, thru, tag), token

    def reduce_finish(state, after):
        group, sib, send, recv, thru, tag = state
        lands = _wait_copies(send, recv, thru, corner(after), _scatter_copies, "scatter_wait_" + tag)[len(group):]
        mine = [_chip_sum(place, g, rb, rc, "chip_sum_" + nm) for (nm, g), rb, rc in zip(group, sib, lands)]
        whole = _half_exchange(mine, "half_exchange_" + tag)
        out = {}
        for (nm, _), g in zip(group, whole):
            w = big[nm]
            g_out, d, new_m, new_v = _adamw(w[0], g.reshape(w.shape[1:]), big_m[nm][0], big_v[nm][0], "adamw_" + nm)
            out[nm] = (g_out[None], d[None], new_m[None], new_v[None])
        return out

    dhs3, df2, loss_row, d_final = _final_loss(hs3, final_norm.reshape(1, D), tgt)

    dg2, du2 = _ffn_bwd_act(df2, wd2.reshape(F, D), g2, u2, token_ffn2, "ffn2_bwd_act")
    gw_d2 = _wgrad_down(a2, df2, "wgrad_ffn2_down")
    gw_g2 = _wgrad_cols(n3, [dg2], "wgrad_ffn2_gate")[0]
    gw_u2 = _wgrad_cols(n3, [du2], "wgrad_ffn2_up")[0]
    pair_ffn2, token = pair_start([("ffn2_w_gate", gw_g2), ("ffn2_w_up", gw_u2), ("ffn2_w_down", gw_d2)],
                                  corner(gw_u2), "ffn2")
    dn3 = _nt_panel([dg2, du2], [wg2, wu2], token, "ffn2_bwd_in")
    red_ffn2, token = scatter_start(pair_ffn2, dn3)
    dhs2, dm, d_ffn2 = _rms_bwd(dn3, hs2, ffn2_norm, dhs3, 1.0, "rms_bwd_ffn2")

    dy = _nt_panel([dm], [wout.reshape(1, D, D)], token, "mix_bwd_out")
    gw_out = _wgrad_out(y, dm)
    dz1, dcs, db, d_lg, d_lb, d_bcf = _mix_conv_bwd1(u, dy, wsc_full, wcf_full, conv_cf_b, ln_cf_g, ln_cf_b)
    du, d_bin, d_wsc, d_wcf = _mix_conv_bwd2(u, dz1, dcs, db, wsc_full, wcf_full)
    gw_in = _wgrad_cols(n2, [du], "wgrad_w_in")[0]
    pair_mix, token = pair_start([("w_in", gw_in), ("w_out", gw_out)], corner(gw_in), "mix")
    dn2 = _nt_panel([du], [win], token, "mix_bwd_in")
    red_mix, token = scatter_start(pair_mix, dn2)
    dhs1, df1, d_mix = _rms_bwd(dn2, hs1, mix_norm, dhs2, FFN_RES_SCALE, "rms_bwd_mix")

    dg1, du1 = _ffn_bwd_act(df1, wd1.reshape(F, D), g1, u1, token, "ffn1_bwd_act")
    gw_d1 = _wgrad_down(a1, df1, "wgrad_ffn1_down")
    gw_g1 = _wgrad_cols(n1, [dg1], "wgrad_ffn1_gate")[0]
    pair_ffn1a, token = pair_start([("ffn1_w_down", gw_d1), ("ffn1_w_gate", gw_g1)], corner(gw_g1), "ffn1a")
    gw_u1 = _wgrad_cols(n1, [du1], "wgrad_ffn1_up", token)[0]
    red_ffn1a, token = scatter_start(pair_ffn1a, gw_u1)
    pair_ffn1b, token = pair_start([("ffn1_w_up", gw_u1)], token, "ffn1b")
    dn1 = _nt_panel([dg1, du1], [wg1, wu1], token, "ffn1_bwd_in")
    red_ffn1b, token = scatter_start(pair_ffn1b, dn1)
    grad_x, d_meta, d_ffn1 = _rms_bwd_first(dn1, hs0, ffn1_norm, dhs1, token)

    big_out = reduce_finish(red_ffn2, grad_x)
    big_out.update(reduce_finish(red_mix, big_out["ffn2_w_down"][1]))
    big_out.update(reduce_finish(red_ffn1a, big_out["w_out"][1]))
    big_out.update(reduce_finish(red_ffn1b, big_out["ffn1_w_gate"][1]))

    W = C1
    rows = lambda a: a.reshape(-1, W)
    parts = [rows(d_ffn1), rows(d_mix), rows(d_ffn2), rows(d_final), rows(d_bin), d_bcf, d_lg, d_lb,
             d_wsc, d_wcf, rows(d_meta), jnp.broadcast_to(loss_row[:, :1], (1, W))]
    sizes = [p.shape[0] for p in parts]
    total_rows = sum(sizes)
    packed = _pad_rows(jnp.concatenate(parts, axis=0), -(-total_rows // 8) * 8)
    summed = _share_small(packed, True, "sum_small")
    offs = [0]
    for n in sizes:
        offs.append(offs[-1] + n)
    piece = lambda k: summed[offs[k]:offs[k + 1]]
    loss = piece(11)[0, 0]
    g_ffn1, g_mix, g_ffn2 = (piece(k).reshape(1, D) for k in range(3))
    g_final = piece(3).reshape(1, D)
    g_bin = piece(4).reshape(1, -1)
    g_bcf, g_lg, g_lb = piece(5), piece(6), piece(7)
    g_wsc = lax.dynamic_slice_in_dim(piece(8), chip * cs, cs, axis=1)
    g_wcf = lax.dynamic_slice_in_dim(piece(9), chip * cs, cs, axis=1)
    g_meta = lax.dynamic_slice_in_dim(piece(10).reshape(N_META, D), chip * ms, ms, axis=1)

    small_names = ["meta_tokens", "ffn1_norm", "mix_norm", "b_in", "conv_sc_w", "conv_cf_w", "conv_cf_b", "ln_cf_g",
                   "ln_cf_b", "ffn2_norm", "final_norm"]
    small_w = [meta_tokens, ffn1_norm, mix_norm, b_in, conv_sc_w[0], conv_cf_w[0], conv_cf_b, ln_cf_g, ln_cf_b,
               ffn2_norm, final_norm.reshape(1, D)]
    small_g = [g_meta, g_ffn1, g_mix, g_bin, g_wsc, g_wcf, g_bcf, g_lg, g_lb, g_ffn2, g_final]
    small_m = [m_meta_tokens, m_ffn1_norm, m_mix_norm, m_b_in, m_conv_sc_w[0], m_conv_cf_w[0], m_conv_cf_b, m_ln_cf_g,
               m_ln_cf_b, m_ffn2_norm, m_final_norm.reshape(1, D)]
    small_v = [v_meta_tokens, v_ffn1_norm, v_mix_norm, v_b_in, v_conv_sc_w[0], v_conv_cf_w[0], v_conv_cf_b, v_ln_cf_g,
               v_ln_cf_b, v_ffn2_norm, v_final_norm.reshape(1, D)]
    s_d, s_m, s_v = _adamw_small(small_w, small_g, small_m, small_v)
    shapes = {"conv_sc_w": conv_sc_w.shape, "conv_cf_w": conv_cf_w.shape, "final_norm": final_norm.shape}
    small_out = {}
    for nm, g, d, m, v in zip(small_names, small_g, s_d, s_m, s_v):
        shp = shapes.get(nm, g.shape)
        small_out[nm] = tuple(t.reshape(shp) for t in (g, d, m, v))

    order = ["meta_tokens", "ffn1_norm", "ffn1_w_gate", "ffn1_w_up", "ffn1_w_down", "mix_norm", "w_in", "b_in",
             "conv_sc_w", "conv_cf_w", "conv_cf_b", "ln_cf_g", "ln_cf_b", "w_out", "ffn2_norm", "ffn2_w_gate",
             "ffn2_w_up", "ffn2_w_down", "final_norm"]
    res = {**big_out, **small_out}
    outs = [loss, grad_x[None]]
    for q in range(4):
        outs.extend(res[nm][q] for nm in order)
    return tuple(outs)
```

```python
import functools

import jax
import jax.numpy as jnp
from jax import lax
from jax.experimental import pallas as pl
from jax.experimental.pallas import tpu as pltpu

F32 = jnp.float32
BF16 = jnp.bfloat16
MESH = pl.DeviceIdType.MESH

N_META = 16
TT = 128
PAD = TT - N_META
HALO = 32
EPS = 1e-6
FFN_RES_SCALE = 0.5
N_CHIPS = 4
N_DEV = 8

ADAM_LR = 0.001
ADAM_B1 = 0.9
ADAM_B2 = 0.999
ADAM_EPS = 1e-08
ADAM_WD = 0.01
ADAM_STEP = 10

V7X_VMEM_BYTES = 64 * 2 ** 20
NT_DIMS = (((1,), (1,)), ((), ()))
TN_DIMS = (((0,), (0,)), ((), ()))


def _params(semantics, block_bytes):
    limit = min(2 * block_bytes + 16 * 2 ** 20, V7X_VMEM_BYTES - 6 * 2 ** 20)
    return pltpu.CompilerParams(dimension_semantics=semantics, vmem_limit_bytes=int(limit))


def _pallas(body, out_shape, **kw):
    if "grid" not in kw and "grid_spec" not in kw:
        return pl.pallas_call(body, out_shape=out_shape, **kw)
    big = lambda shape, dtype: jnp.issubdtype(dtype, jnp.floating) and len(shape) >= 2
    pin_out = lambda s: pltpu.HBM(s.shape, s.dtype) if big(s.shape, s.dtype) else s
    single = not isinstance(out_shape, (list, tuple))
    shapes = pin_out(out_shape) if single else [pin_out(s) for s in out_shape]
    call = pl.pallas_call(body, out_shape=shapes, **kw)
    pin = lambda a: pltpu.with_memory_space_constraint(a, pltpu.HBM) if big(a.shape, a.dtype) else a
    return lambda *operands: call(*[pin(a) for a in operands])


def _nbytes(shape, dtype):
    n = 1
    for d in shape:
        if d is not None:
            n *= d
    return n * jnp.dtype(dtype).itemsize


def _row_tile(rows, target, mult=8):
    best = None
    for t in range(mult, min(rows, target) + 1, mult):
        if rows % t == 0:
            best = t
    assert best is not None, (rows, target, mult)
    return best


def _sigmoid(v):
    return jax.nn.sigmoid(v)


def _dsilu(v, s):
    return s * (1.0 + v * (1.0 - s))


def _embed_rms(x2, meta, gain):
    S, D = x2.shape
    T = S + TT

    def body(x_ref, meta_ref, g_ref, hs_ref, n_ref):
        i = pl.program_id(0)

        @pl.when(i == 0)
        def _():
            hs_ref[...] = jnp.zeros_like(hs_ref)
            hs_ref[PAD:, :] = meta_ref[...]

        @pl.when(i > 0)
        def _():
            hs_ref[...] = x_ref[...]

        h = hs_ref[...]
        r = lax.rsqrt(jnp.mean(h * h, axis=-1, keepdims=True) + EPS)
        n_ref[...] = ((h * r) * g_ref[...]).astype(BF16)

    blk = _nbytes((TT, D), F32) * 2 + _nbytes((TT, D), BF16)
    return _pallas(
        body, name="embed_rms", grid=(T // TT,),
        in_specs=[pl.BlockSpec((TT, D), lambda i: (jnp.maximum(i - 1, 0), 0)),
                  pl.BlockSpec((N_META, D), lambda i: (0, 0)),
                  pl.BlockSpec((1, D), lambda i: (0, 0))],
        out_specs=[pl.BlockSpec((TT, D), lambda i: (i, 0)), pl.BlockSpec((TT, D), lambda i: (i, 0))],
        out_shape=[jax.ShapeDtypeStruct((T, D), F32), jax.ShapeDtypeStruct((T, D), BF16)],
        compiler_params=_params(("parallel",), blk),
    )(x2, meta, gain)


def _rms(hs, gain, name):
    T, D = hs.shape
    te = _row_tile(T, 384)

    def body(h_ref, g_ref, n_ref):
        h = h_ref[...]
        r = lax.rsqrt(jnp.mean(h * h, axis=-1, keepdims=True) + EPS)
        n_ref[...] = ((h * r) * g_ref[...]).astype(BF16)

    blk = _nbytes((te, D), F32) + _nbytes((te, D), BF16)
    return _pallas(
        body, name=name, grid=(T // te,),
        in_specs=[pl.BlockSpec((te, D), lambda i: (i, 0)), pl.BlockSpec((1, D), lambda i: (0, 0))],
        out_specs=pl.BlockSpec((te, D), lambda i: (i, 0)),
        out_shape=jax.ShapeDtypeStruct((T, D), BF16),
        compiler_params=_params(("parallel",), blk),
    )(hs, gain)


def _rms_bwd_math(dn, h, g):
    r = lax.rsqrt(jnp.mean(h * h, axis=-1, keepdims=True) + EPS)
    xh = h * r
    dgain = jnp.sum(dn * xh, axis=0, keepdims=True)
    dxh = dn * g
    dh = r * (dxh - xh * jnp.mean(dxh * xh, axis=-1, keepdims=True))
    return dh, dgain


def _rms_bwd(dn, hs, gain, dres, scale, name):
    T, D = hs.shape
    te = _row_tile(T, 384)

    def body(dn_ref, h_ref, g_ref, dres_ref, dhs_ref, dhb_ref, dg_ref):
        dh, dgain = _rms_bwd_math(dn_ref[...], h_ref[...], g_ref[...])
        d = dres_ref[...] + dh
        dhs_ref[...] = d
        dhb_ref[...] = (scale * d).astype(BF16)

        @pl.when(pl.program_id(0) == 0)
        def _():
            dg_ref[...] = jnp.zeros_like(dg_ref)

        dg_ref[...] += dgain

    blk = _nbytes((te, D), F32) * 4 + _nbytes((te, D), BF16)
    row = lambda i: (i, 0)
    return _pallas(
        body, name=name, grid=(T // te,),
        in_specs=[pl.BlockSpec((te, D), row), pl.BlockSpec((te, D), row), pl.BlockSpec((1, D), lambda i: (0, 0)),
                  pl.BlockSpec((te, D), row)],
        out_specs=[pl.BlockSpec((te, D), row), pl.BlockSpec((te, D), row), pl.BlockSpec((1, D), lambda i: (0, 0))],
        out_shape=[jax.ShapeDtypeStruct((T, D), F32), jax.ShapeDtypeStruct((T, D), BF16),
                   jax.ShapeDtypeStruct((1, D), F32)],
        compiler_params=_params(("arbitrary",), blk),
    )(dn, hs, gain, dres)


def _rms_bwd_first(dn, hs, gain, dres, after):
    T, D = hs.shape
    S = T - TT

    def body(dn_ref, h_ref, g_ref, dres_ref, after_ref, gx_ref, gm_ref, dg_ref):
        i = pl.program_id(0)
        dh, dgain = _rms_bwd_math(dn_ref[...], h_ref[...], g_ref[...])
        d = dres_ref[...] + dh

        @pl.when(i == 0)
        def _():
            dg_ref[...] = jnp.zeros_like(dg_ref)
            gm_ref[...] = d[PAD:, :]

        @pl.when(i > 0)
        def _():
            gx_ref[...] = d

        dg_ref[...] += dgain

    blk = _nbytes((TT, D), F32) * 4
    row = lambda i: (i, 0)
    return _pallas(
        body, name="rms_bwd_ffn1", grid=(T // TT,),
        in_specs=[pl.BlockSpec((TT, D), row), pl.BlockSpec((TT, D), row), pl.BlockSpec((1, D), lambda i: (0, 0)),
                  pl.BlockSpec((TT, D), row), TOKEN],
        out_specs=[pl.BlockSpec((TT, D), lambda i: (jnp.maximum(i - 1, 0), 0)),
                   pl.BlockSpec((N_META, D), lambda i: (0, 0)), pl.BlockSpec((1, D), lambda i: (0, 0))],
        out_shape=[jax.ShapeDtypeStruct((S, D), F32), jax.ShapeDtypeStruct((N_META, D), F32),
                   jax.ShapeDtypeStruct((1, D), F32)],
        compiler_params=_params(("arbitrary",), blk),
    )(dn, hs, gain, dres, after)


def _final_loss(hs, gain, tgt):
    T, D = hs.shape

    def body(h_ref, g_ref, t_ref, dhs_ref, dhb_ref, loss_ref, dg_ref):
        i = pl.program_id(0)
        h = h_ref[...]
        g = g_ref[...]
        r = lax.rsqrt(jnp.mean(h * h, axis=-1, keepdims=True) + EPS)
        xh = h * r
        e = jnp.where(i > 0, xh * g - t_ref[...], 0.0)
        tile_loss = jnp.sum(jnp.sum(e * e, axis=1, keepdims=True), axis=0, keepdims=True) * (0.5 / D)
        dout = e * (1.0 / D)
        dgain = jnp.sum(dout * xh, axis=0, keepdims=True)
        dxh = dout * g
        d = r * (dxh - xh * jnp.mean(dxh * xh, axis=-1, keepdims=True))
        dhs_ref[...] = d
        dhb_ref[...] = (FFN_RES_SCALE * d).astype(BF16)

        @pl.when(i == 0)
        def _():
            loss_ref[...] = jnp.zeros_like(loss_ref)
            dg_ref[...] = jnp.zeros_like(dg_ref)

        loss_ref[...] += jnp.broadcast_to(tile_loss, loss_ref.shape)
        dg_ref[...] += dgain

    blk = _nbytes((TT, D), F32) * 3 + _nbytes((TT, D), BF16)
    row = lambda i: (i, 0)
    return _pallas(
        body, name="final_loss", grid=(T // TT,),
        in_specs=[pl.BlockSpec((TT, D), row), pl.BlockSpec((1, D), lambda i: (0, 0)),
                  pl.BlockSpec((TT, D), lambda i: (jnp.maximum(i - 1, 0), 0))],
        out_specs=[pl.BlockSpec((TT, D), row), pl.BlockSpec((TT, D), row),
                   pl.BlockSpec((1, 128), lambda i: (0, 0)), pl.BlockSpec((1, D), lambda i: (0, 0))],
        out_shape=[jax.ShapeDtypeStruct((T, D), F32), jax.ShapeDtypeStruct((T, D), BF16),
                   jax.ShapeDtypeStruct((1, 128), F32), jax.ShapeDtypeStruct((1, D), F32)],
        compiler_params=_params(("arbitrary",), blk),
    )(hs, gain, tgt)


MXU_COLS = 256


def _tm(T):
    return _row_tile(T, 704, 16)


def _col_chunks(n):
    return [(c, min(MXU_COLS, n - c)) for c in range(0, n, MXU_COLS)]


TOKEN = pl.BlockSpec((8, 128), lambda *_: (0, 0))


def _ffn_up(n, wg, wu, shards, prev, after, name):
    T, D = n.shape
    Fs = wg.shape[2]
    tm = _tm(T)
    nprev = 0 if prev is None else 3

    def body(shards_ref, n_ref, wg_ref, wu_ref, after_ref, *refs):
        g_ref, u_ref, a_ref = refs[nprev:]
        nn = n_ref[...]
        for c0, cw in _col_chunks(Fs):
            if 2 * cw == MXU_COLS:
                both = jnp.concatenate([wg_ref[:, c0:c0 + cw], wu_ref[:, c0:c0 + cw]], axis=1)
                gu = jnp.dot(nn, both, preferred_element_type=F32)
                g, u = gu[:, :cw], gu[:, cw:]
            else:
                g = jnp.dot(nn, wg_ref[:, c0:c0 + cw], preferred_element_type=F32)
                u = jnp.dot(nn, wu_ref[:, c0:c0 + cw], preferred_element_type=F32)
            g_ref[:, c0:c0 + cw] = g.astype(BF16)
            u_ref[:, c0:c0 + cw] = u.astype(BF16)
            a_ref[:, c0:c0 + cw] = (jax.nn.silu(g) * u).astype(BF16)

    blk = _nbytes((tm, D), BF16) + 2 * _nbytes((D, Fs), BF16) + 3 * _nbytes((tm, Fs), BF16)
    out = pl.BlockSpec((tm, Fs), lambda j, i, p: (i, p[j]))
    shp = jax.ShapeDtypeStruct((T, N_CHIPS * Fs), BF16)
    return _pallas(
        body, name=name,
        grid_spec=pltpu.PrefetchScalarGridSpec(
            num_scalar_prefetch=1, grid=(shards.shape[0], T // tm),
            in_specs=[pl.BlockSpec((tm, D), lambda j, i, p: (i, 0)),
                      pl.BlockSpec((None, D, Fs), lambda j, i, p: (p[j], 0, 0)),
                      pl.BlockSpec((None, D, Fs), lambda j, i, p: (p[j], 0, 0)), TOKEN] + [ANY] * nprev,
            out_specs=[out, out, out]),
        out_shape=[shp, shp, shp], input_output_aliases={5 + q: q for q in range(nprev)},
        compiler_params=_params(("arbitrary", "arbitrary"), blk),
    )(shards, n, wg, wu, after, *(prev or ()))


def _ffn_down(a, wd, hs, shards, name, after=None):
    T, F = a.shape
    _, Fs, D = wd.shape
    tm = _tm(T)
    tn = D // 2
    extra = [] if after is None else [after]

    def body(shards_ref, a_ref, w_ref, h_ref, *refs):
        o_ref = refs[-1]
        part = FFN_RES_SCALE * jnp.dot(a_ref[...], w_ref[...], preferred_element_type=F32)

        @pl.when(pl.program_id(2) == 0)
        def _():
            o_ref[...] = h_ref[...] + part

        @pl.when(pl.program_id(2) > 0)
        def _():
            o_ref[...] += part

    blk = _nbytes((tm, Fs), BF16) + _nbytes((Fs, tn), BF16) + 3 * _nbytes((tm, tn), F32)
    return _pallas(
        body, name=name,
        grid_spec=pltpu.PrefetchScalarGridSpec(
            num_scalar_prefetch=1, grid=(D // tn, T // tm, shards.shape[0]),
            in_specs=[pl.BlockSpec((tm, Fs), lambda n, i, k, p: (i, p[k])),
                      pl.BlockSpec((None, Fs, tn), lambda n, i, k, p: (p[k], 0, n)),
                      pl.BlockSpec((tm, tn), lambda n, i, k, p: (i, n))] + [TOKEN] * len(extra),
            out_specs=pl.BlockSpec((tm, tn), lambda n, i, k, p: (i, n))),
        out_shape=jax.ShapeDtypeStruct((T, D), F32),
        compiler_params=_params(("parallel", "parallel", "arbitrary"), blk),
    )(shards, a, wd, hs, *extra)


def _ffn_down_whole(a, wd, hs, name):
    T, F = a.shape
    D = wd.shape[1]
    tm = _tm(T)
    tn = D // 4

    def body(a_ref, w_ref, h_ref, o_ref):
        o_ref[...] = h_ref[...] + FFN_RES_SCALE * jnp.dot(a_ref[...], w_ref[...], preferred_element_type=F32)

    blk = _nbytes((tm, F), BF16) + _nbytes((F, tn), BF16) + 3 * _nbytes((tm, tn), F32)
    return _pallas(
        body, name=name, grid=(D // tn, T // tm),
        in_specs=[pl.BlockSpec((tm, F), lambda n, i: (i, 0)), pl.BlockSpec((F, tn), lambda n, i: (0, n)),
                  pl.BlockSpec((tm, tn), lambda n, i: (i, n))],
        out_specs=pl.BlockSpec((tm, tn), lambda n, i: (i, n)),
        out_shape=jax.ShapeDtypeStruct((T, D), F32),
        compiler_params=_params(("parallel", "parallel"), blk),
    )(a, wd, hs)


def _mix_in(n, w, b):
    T, D = n.shape
    Ns = w.shape[2]
    tm = _tm(T)

    def body(n_ref, w_ref, b_ref, u_ref):
        u_ref[...] = jnp.dot(n_ref[...], w_ref[...], preferred_element_type=F32) + b_ref[...]

    blk = _nbytes((tm, D), BF16) + _nbytes((D, Ns), BF16) + 2 * _nbytes((tm, Ns), F32)
    return _pallas(
        body, name="mix_in", grid=(N_CHIPS, T // tm),
        in_specs=[pl.BlockSpec((tm, D), lambda j, i: (i, 0)), pl.BlockSpec((None, D, Ns), lambda j, i: (j, 0, 0)),
                  pl.BlockSpec((1, Ns), lambda j, i: (0, j))],
        out_specs=pl.BlockSpec((tm, Ns), lambda j, i: (i, j)),
        out_shape=jax.ShapeDtypeStruct((T, N_CHIPS * Ns), F32),
        compiler_params=_params(("parallel", "parallel"), blk),
    )(n, w, b)


def _mix_out(y, w, hs, after):
    T, D = y.shape
    tm = _tm(T)

    def body(y_ref, w_ref, h_ref, after_ref, o_ref):
        o_ref[...] = h_ref[...] + jnp.dot(y_ref[...], w_ref[...], preferred_element_type=F32)

    blk = _nbytes((tm, D), BF16) + _nbytes((D, D), BF16) + 3 * _nbytes((tm, D), F32)
    return _pallas(
        body, name="mix_out", grid=(T // tm,),
        in_specs=[pl.BlockSpec((tm, D), lambda i: (i, 0)), pl.BlockSpec((D, D), lambda i: (0, 0)),
                  pl.BlockSpec((tm, D), lambda i: (i, 0)), TOKEN],
        out_specs=pl.BlockSpec((tm, D), lambda i: (i, 0)),
        out_shape=jax.ShapeDtypeStruct((T, D), F32),
        compiler_params=_params(("parallel",), blk),
    )(y, w, hs, after)


def _ffn_bwd_act(dfb, wd, g, u, after, name):
    T, D = dfb.shape
    F = wd.shape[0]
    tm = _row_tile(T, 1408, 16)
    tn = 2 * MXU_COLS

    tr = _tm(tm)

    def body(d_ref, w_ref, g_ref, u_ref, after_ref, dg_ref, du_ref):
        for r0 in range(0, tm, tr):
            dv = d_ref[r0:r0 + tr, :]
            for c0, cw in _col_chunks(tn):
                da = lax.dot_general(dv, w_ref[c0:c0 + cw, :], NT_DIMS, preferred_element_type=F32)
                gv = g_ref[r0:r0 + tr, c0:c0 + cw].astype(F32)
                uv = u_ref[r0:r0 + tr, c0:c0 + cw].astype(F32)
                s = _sigmoid(gv)
                du_ref[r0:r0 + tr, c0:c0 + cw] = (da * (gv * s)).astype(BF16)
                dg_ref[r0:r0 + tr, c0:c0 + cw] = (da * uv * _dsilu(gv, s)).astype(BF16)

    blk = _nbytes((tm, D), BF16) + _nbytes((tn, D), BF16) + 4 * _nbytes((tm, tn), BF16)
    io = pl.BlockSpec((tm, tn), lambda n, i: (i, n))
    shp = jax.ShapeDtypeStruct((T, F), BF16)
    return _pallas(
        body, name=name, grid=(F // tn, T // tm),
        in_specs=[pl.BlockSpec((tm, D), lambda n, i: (i, 0)), pl.BlockSpec((tn, D), lambda n, i: (n, 0)), io, io, TOKEN],
        out_specs=[io, io], out_shape=[shp, shp],
        compiler_params=_params(("parallel", "parallel"), blk),
    )(dfb, wd, g, u, after)


def _nt_panel(lhs_list, w_list, after, name):
    T = lhs_list[0].shape[0]
    nsh, Dout, Ks = w_list[0].shape
    npair = len(lhs_list)
    tm = _row_tile(T, 1408, 16)
    tn = Dout // 2

    def body(*refs):
        l_refs, w_refs, o_ref = refs[:npair], refs[npair:2 * npair], refs[2 * npair + 1]
        j = pl.program_id(2)
        k0 = Ks - Ks % MXU_COLS if npair == 2 and 2 * (Ks % MXU_COLS) == MXU_COLS else Ks
        acc = None
        for p in range(npair):
            part = lax.dot_general(l_refs[p][:, :k0], w_refs[p][:, :k0], NT_DIMS, preferred_element_type=F32)
            acc = part if acc is None else acc + part
        if k0 < Ks:
            lhs = jnp.concatenate([l_refs[p][:, k0:] for p in range(npair)], axis=1)
            rhs = jnp.concatenate([w_refs[p][:, k0:] for p in range(npair)], axis=1)
            acc = acc + lax.dot_general(lhs, rhs, NT_DIMS, preferred_element_type=F32)

        @pl.when(j == 0)
        def _():
            o_ref[...] = acc

        @pl.when(j > 0)
        def _():
            o_ref[...] += acc

    blk = npair * (_nbytes((tm, Ks), BF16) + _nbytes((tn, Ks), BF16)) + 2 * _nbytes((tm, tn), F32)
    return _pallas(
        body, name=name, grid=(Dout // tn, T // tm, nsh),
        in_specs=[pl.BlockSpec((tm, Ks), lambda n, i, j: (i, j))] * npair
                 + [pl.BlockSpec((None, tn, Ks), lambda n, i, j: (j, n, 0))] * npair + [TOKEN],
        out_specs=pl.BlockSpec((tm, tn), lambda n, i, j: (i, n)),
        out_shape=jax.ShapeDtypeStruct((T, Dout), F32),
        compiler_params=_params(("parallel", "parallel", "arbitrary"), blk),
    )(*lhs_list, *w_list, after)


def _tn_call(name, grid, lhs, lhs_spec, rhs_list, rhs_specs, out_shapes, out_specs, blk, after=None):
    nr = len(rhs_list)
    extra = [] if after is None else [after]

    def body(*refs):
        l_ref, r_refs, o_refs = refs[0], refs[1:1 + nr], refs[len(refs) - nr:]
        k = pl.program_id(len(grid) - 1)
        lv = l_ref[...]
        for q in range(nr):
            part = lax.dot_general(lv, r_refs[q][...], TN_DIMS, preferred_element_type=F32)
            part = part.reshape(o_refs[q].shape)

            @pl.when(k == 0)
            def _(o=o_refs[q], part=part):
                o[...] = part

            @pl.when(k > 0)
            def _(o=o_refs[q], part=part):
                o[...] += part

    return _pallas(
        body, name=name, grid=grid, in_specs=[lhs_spec] + rhs_specs + [TOKEN] * len(extra), out_specs=out_specs,
        out_shape=out_shapes, compiler_params=_params(("parallel",) * (len(grid) - 1) + ("arbitrary",), blk),
    )(lhs, *rhs_list, *extra)


def _tk(T):
    return T


def _wgrad_cols(n, rhs_list, name, after=None):
    T, D = n.shape
    Ns = rhs_list[0].shape[1] // N_CHIPS
    tk = _tk(T)
    nr = len(rhs_list)
    tm = D // 4
    blk = _nbytes((tk, tm), BF16) + nr * (_nbytes((tk, Ns), BF16) + 2 * _nbytes((tm, Ns), F32))
    return _tn_call(
        name, (N_CHIPS, D // tm, T // tk), n, pl.BlockSpec((tk, tm), lambda j, m, k: (k, m)),
        rhs_list, [pl.BlockSpec((tk, Ns), lambda j, m, k: (k, j))] * nr,
        [jax.ShapeDtypeStruct((N_CHIPS, 2, D // 2, Ns), F32)] * nr,
        [pl.BlockSpec((None, None, tm, Ns), lambda j, m, k: (j, m // 2, m % 2, 0))] * nr, blk, after)


def _wgrad_down(a, dfb, name):
    T, F = a.shape
    D = dfb.shape[1]
    Fs = F // N_CHIPS
    tk = _tk(T)
    tn = D // 4
    blk = _nbytes((tk, Fs), BF16) + _nbytes((tk, tn), BF16) + 2 * _nbytes((Fs, tn), F32)
    return _tn_call(
        name, (N_CHIPS, D // tn, T // tk), a, pl.BlockSpec((tk, Fs), lambda j, n, k: (k, j)),
        [dfb], [pl.BlockSpec((tk, tn), lambda j, n, k: (k, n))],
        [jax.ShapeDtypeStruct((N_CHIPS, 2, Fs // 2, D), F32)],
        [pl.BlockSpec((None, 2, Fs // 2, tn), lambda j, n, k: (j, 0, 0, n))], blk)[0]


def _wgrad_out(y, dmb):
    T, D = y.shape
    tk = _tk(T)
    tn = D // 2
    rows = D // (2 * N_CHIPS)
    blk = _nbytes((tk, D // 2), BF16) + _nbytes((tk, tn), BF16) + 2 * _nbytes((D // 2, tn), F32)
    return _tn_call(
        "wgrad_w_out", (2, D // tn, T // tk), y, pl.BlockSpec((tk, D // 2), lambda m, n, k: (k, m)),
        [dmb], [pl.BlockSpec((tk, tn), lambda m, n, k: (k, n))],
        [jax.ShapeDtypeStruct((N_CHIPS, 2, rows, D), F32)],
        [pl.BlockSpec((2, 2, rows, tn), lambda m, n, k: (m, 0, 0, n))], blk)[0]


def _row_masks(i, last):
    rows = i * TT + lax.broadcasted_iota(jnp.int32, (TT, 1), 0)
    prows = i * TT - HALO + lax.broadcasted_iota(jnp.int32, (HALO, 1), 0)
    return rows >= PAD, (prows >= PAD) & (i > 0), i < last


def _conv_inputs(u, up, mask_c, mask_p, zbuf, pbuf, C1):
    b, c, v, a, g = (u[:, k * C1:(k + 1) * C1] for k in range(5))
    cp, vp, ap, gp = (up[:, k * C1:(k + 1) * C1] for k in range(1, 5))
    sg = _sigmoid(g)
    pbuf[0:HALO, :] = jnp.where(mask_p, cp * vp, 0.0)
    pbuf[HALO:, :] = jnp.where(mask_c, c * v, 0.0)
    zbuf[0:HALO, :] = jnp.where(mask_p, ap * _sigmoid(gp), 0.0)
    zbuf[HALO:, :] = jnp.where(mask_c, a * sg, 0.0)
    return b, c, v, a, sg


SUBLANES = 8
SHIFT_ROWS = TT + HALO - SUBLANES


def _shifted_scratch(C1):
    return pltpu.VMEM((SUBLANES - 1, SHIFT_ROWS, C1), F32)


def _fill_shifted(buf, sh):
    for r in range(1, SUBLANES):
        sh[r - 1] = buf[r:r + SHIFT_ROWS, :]


LANES = 128


def _window(buf, sh, lo, c0):
    if sh is None or lo % SUBLANES == 0:
        return buf[lo:lo + TT, c0:c0 + LANES]
    q, r = divmod(lo, SUBLANES)
    return sh[r - 1, q * SUBLANES:q * SUBLANES + TT, c0:c0 + LANES]


def _tap_sum(w_ref, buf, sh, starts):
    chunks = []
    for c0 in range(0, buf.shape[1], LANES):
        acc = None
        for k, lo in enumerate(starts):
            term = w_ref[k:k + 1, c0:c0 + LANES] * _window(buf, sh, lo, c0)
            acc = term if acc is None else acc + term
        chunks.append(acc)
    return jnp.concatenate(chunks, axis=1)


def _causal_conv(w_ref, buf, sh=None):
    K = w_ref.shape[0]
    return _tap_sum(w_ref, buf, sh, [HALO - (K - 1) + k for k in range(K)])


def _anticausal_conv(w_ref, buf, sh=None):
    K = w_ref.shape[0]
    return _tap_sum(w_ref, buf, sh, [K - 1 - k for k in range(K)])


def _conv_weight_sums(dw_ref, dy, buf, sh=None):
    K = dw_ref.shape[0]
    for c0 in range(0, buf.shape[1], LANES):
        dyc = dy[:, c0:c0 + LANES]
        for k in range(K):
            prod = dyc * _window(buf, sh, HALO - (K - 1) + k, c0)
            dw_ref[k:k + 1, c0:c0 + LANES] += jnp.sum(prod, axis=0, keepdims=True)


def _layernorm_stats(z1):
    mu = jnp.mean(z1, axis=-1, keepdims=True)
    zc = z1 - mu
    rs = lax.rsqrt(jnp.mean(zc * zc, axis=-1, keepdims=True) + EPS)
    return zc * rs, rs


def _mixer_specs(T, DIN, C1, ksc, kcf):
    cur = pl.BlockSpec((TT, DIN), lambda i: (i, 0))
    prev = pl.BlockSpec((HALO, DIN), lambda i: (jnp.maximum(i * (TT // HALO) - 1, 0), 0))
    full = lambda r: pl.BlockSpec((r, C1), lambda i: (0, 0))
    return cur, prev, [full(ksc), full(kcf), full(1), full(1), full(1)]


def _mix_conv_fwd(u, wsc, wcf, bcf, lg, lb):
    T, DIN = u.shape
    C1 = DIN // 5
    last = T // TT - 1

    def body(u_ref, up_ref, wsc_ref, wcf_ref, bcf_ref, lg_ref, lb_ref, y_ref, zbuf, pbuf, zsh):
        i = pl.program_id(0)
        mask_c, mask_p, _ = _row_masks(i, last)
        b, _, _, _, _ = _conv_inputs(u_ref[...], up_ref[...], mask_c, mask_p, zbuf, pbuf, C1)
        _fill_shifted(zbuf, zsh)
        cs = _causal_conv(wsc_ref, pbuf)
        z1 = _causal_conv(wcf_ref, zbuf, zsh) + bcf_ref[...]
        zh, _ = _layernorm_stats(z1)
        ln = zh * lg_ref[...] + lb_ref[...]
        y_ref[:, 0:C1] = jnp.where(mask_c, b * cs, 0.0).astype(BF16)
        y_ref[:, C1:] = jnp.where(mask_c, jax.nn.silu(ln), 0.0).astype(BF16)

    cur, prev, small = _mixer_specs(T, DIN, C1, wsc.shape[0], wcf.shape[0])
    blk = _nbytes((TT + HALO, DIN), F32) + _nbytes((TT, 2 * C1), BF16) + 12 * _nbytes((TT + HALO, C1), F32)
    return _pallas(
        body, name="mix_conv_fwd", grid=(T // TT,),
        in_specs=[cur, prev] + small,
        out_specs=pl.BlockSpec((TT, 2 * C1), lambda i: (i, 0)),
        out_shape=jax.ShapeDtypeStruct((T, 2 * C1), BF16),
        scratch_shapes=[pltpu.VMEM((TT + HALO, C1), F32), pltpu.VMEM((TT + HALO, C1), F32), _shifted_scratch(C1)],
        compiler_params=_params(("arbitrary",), blk),
    )(u, u, wsc, wcf, bcf, lg, lb)


def _mix_conv_bwd1(u, dy, wsc, wcf, bcf, lg, lb):
    T, DIN = u.shape
    C1 = DIN // 5
    last = T // TT - 1

    def body(u_ref, up_ref, dy_ref, wsc_ref, wcf_ref, bcf_ref, lg_ref, lb_ref,
             dz1_ref, dcs_ref, db_ref, dlg_ref, dlb_ref, dbcf_ref, zbuf, pbuf, zsh):
        i = pl.program_id(0)
        mask_c, mask_p, _ = _row_masks(i, last)
        b, _, _, _, _ = _conv_inputs(u_ref[...], up_ref[...], mask_c, mask_p, zbuf, pbuf, C1)
        _fill_shifted(zbuf, zsh)
        cs = _causal_conv(wsc_ref, pbuf)
        z1 = _causal_conv(wcf_ref, zbuf, zsh) + bcf_ref[...]
        zh, rs = _layernorm_stats(z1)
        ln = zh * lg_ref[...] + lb_ref[...]
        dy = dy_ref[...]
        dysc = jnp.where(mask_c, dy[:, 0:C1], 0.0)
        dycf = jnp.where(mask_c, dy[:, C1:], 0.0)
        db_ref[...] = (dysc * cs).astype(BF16)
        dcs_ref[...] = dysc * b
        dl = dycf * _dsilu(ln, _sigmoid(ln))
        dzh = dl * lg_ref[...]
        dz1 = rs * (dzh - jnp.mean(dzh, axis=-1, keepdims=True) - zh * jnp.mean(dzh * zh, axis=-1, keepdims=True))
        dz1_ref[...] = dz1

        @pl.when(i == 0)
        def _():
            dlg_ref[...] = jnp.zeros_like(dlg_ref)
            dlb_ref[...] = jnp.zeros_like(dlb_ref)
            dbcf_ref[...] = jnp.zeros_like(dbcf_ref)

        dlg_ref[...] += jnp.sum(dl * zh, axis=0, keepdims=True)
        dlb_ref[...] += jnp.sum(dl, axis=0, keepdims=True)
        dbcf_ref[...] += jnp.sum(dz1, axis=0, keepdims=True)

    cur, prev, small = _mixer_specs(T, DIN, C1, wsc.shape[0], wcf.shape[0])
    tile = lambda: pl.BlockSpec((TT, C1), lambda i: (i, 0))
    vec = lambda: pl.BlockSpec((1, C1), lambda i: (0, 0))
    blk = _nbytes((TT + HALO, DIN), F32) + 4 * _nbytes((TT, C1), F32) + 16 * _nbytes((TT + HALO, C1), F32)
    return _pallas(
        body, name="mix_conv_bwd1", grid=(T // TT,),
        in_specs=[cur, prev, pl.BlockSpec((TT, 2 * C1), lambda i: (i, 0))] + small,
        out_specs=[tile(), tile(), tile(), vec(), vec(), vec()],
        out_shape=[jax.ShapeDtypeStruct((T, C1), F32), jax.ShapeDtypeStruct((T, C1), F32),
                   jax.ShapeDtypeStruct((T, C1), BF16)] + [jax.ShapeDtypeStruct((1, C1), F32)] * 3,
        scratch_shapes=[pltpu.VMEM((TT + HALO, C1), F32), pltpu.VMEM((TT + HALO, C1), F32), _shifted_scratch(C1)],
        compiler_params=_params(("arbitrary",), blk),
    )(u, u, dy, wsc, wcf, bcf, lg, lb)


def _mix_conv_bwd2(u, dz1, dcs, db, wsc, wcf):
    T, DIN = u.shape
    C1 = DIN // 5
    last = T // TT - 1
    ksc, kcf = wsc.shape[0], wcf.shape[0]

    def body(u_ref, up_ref, dz_ref, dzn_ref, dc_ref, dcn_ref, db_ref, wsc_ref, wcf_ref,
             du_ref, dbin_ref, dwsc_ref, dwcf_ref, zbuf, pbuf, dzbuf, dcbuf, zsh, dzsh):
        i = pl.program_id(0)
        mask_c, mask_p, has_next = _row_masks(i, last)
        _, c, v, a, sg = _conv_inputs(u_ref[...], up_ref[...], mask_c, mask_p, zbuf, pbuf, C1)
        dz1 = dz_ref[...]
        dcs = dc_ref[...]
        dzbuf[0:TT, :] = dz1
        dzbuf[TT:, :] = jnp.where(has_next, dzn_ref[...], 0.0)
        dcbuf[0:TT, :] = dcs
        dcbuf[TT:, :] = jnp.where(has_next, dcn_ref[...], 0.0)

        @pl.when(i == 0)
        def _():
            dbin_ref[...] = jnp.zeros_like(dbin_ref)
            dwsc_ref[...] = jnp.zeros_like(dwsc_ref)
            dwcf_ref[...] = jnp.zeros_like(dwcf_ref)

        _fill_shifted(zbuf, zsh)
        _fill_shifted(dzbuf, dzsh)
        _conv_weight_sums(dwcf_ref, dz1, zbuf, zsh)
        _conv_weight_sums(dwsc_ref, dcs, pbuf)
        dz0 = jnp.where(mask_c, _anticausal_conv(wcf_ref, dzbuf, dzsh), 0.0)
        dp = jnp.where(mask_c, _anticausal_conv(wsc_ref, dcbuf), 0.0)
        parts = (db_ref[...].astype(F32), dp * v, dp * c, dz0 * sg, dz0 * a * sg * (1.0 - sg))
        for k, part in enumerate(parts):
            du_ref[:, k * C1:(k + 1) * C1] = part.astype(BF16)
            dbin_ref[:, k * C1:(k + 1) * C1] += jnp.sum(part, axis=0, keepdims=True)

    cur, prev, small = _mixer_specs(T, DIN, C1, ksc, kcf)
    tile = lambda: pl.BlockSpec((TT, C1), lambda i: (i, 0))
    nxt = lambda: pl.BlockSpec((HALO, C1), lambda i: (jnp.minimum((i + 1) * (TT // HALO), T // HALO - 1), 0))
    blk = (_nbytes((TT + HALO, DIN), F32) + _nbytes((TT, DIN), BF16) + 5 * _nbytes((TT, C1), F32)
           + 16 * _nbytes((TT + HALO, C1), F32))
    buf = lambda: pltpu.VMEM((TT + HALO, C1), F32)
    return _pallas(
        body, name="mix_conv_bwd2", grid=(T // TT,),
        in_specs=[cur, prev, tile(), nxt(), tile(), nxt(), tile(), small[0], small[1]],
        out_specs=[pl.BlockSpec((TT, DIN), lambda i: (i, 0)), pl.BlockSpec((1, DIN), lambda i: (0, 0)),
                   pl.BlockSpec((ksc, C1), lambda i: (0, 0)), pl.BlockSpec((kcf, C1), lambda i: (0, 0))],
        out_shape=[jax.ShapeDtypeStruct((T, DIN), BF16), jax.ShapeDtypeStruct((1, DIN), F32),
                   jax.ShapeDtypeStruct((ksc, C1), F32), jax.ShapeDtypeStruct((kcf, C1), F32)],
        scratch_shapes=[buf(), buf(), buf(), buf(), _shifted_scratch(C1), _shifted_scratch(C1)],
        compiler_params=_params(("arbitrary",), blk),
    )(u, u, dz1, dz1, dcs, dcs, db, wsc, wcf)


def _place():
    x, y, c = lax.axis_index("x"), lax.axis_index("y"), lax.axis_index("c")
    chips = [(1 - x, y), (x, 1 - y), (1 - x, 1 - y)]
    return x, y, c, chips


ANY = pl.BlockSpec(memory_space=pl.ANY)


def _cast_own_block(place, w, name):
    R, C = w.shape
    tr = _row_tile(R // 2, 256, 16)
    nblk = R // 2 // tr

    def body(place_ref, w_ref, o_ref):
        o_ref[...] = w_ref[...].astype(BF16)

    return _pallas(
        body, name=name,
        grid_spec=pltpu.PrefetchScalarGridSpec(
            num_scalar_prefetch=1, grid=(2, nblk),
            in_specs=[pl.BlockSpec((tr, C), lambda h, i, p: (h * nblk + i, 0))],
            out_specs=pl.BlockSpec((None, None, tr, C), lambda h, i, p: (p[0], h, i, 0))),
        out_shape=jax.ShapeDtypeStruct((N_CHIPS, 2, R // 2, C), BF16),
        compiler_params=_params(("parallel", "parallel"), _nbytes((tr, C), F32) + _nbytes((tr, C), BF16)),
    )(place, w)


HBM = pl.BlockSpec(memory_space=pltpu.HBM)
SEM = pl.BlockSpec(memory_space=pltpu.SEMAPHORE)
EFFECT = pltpu.SideEffectType.DATAFLOW_SIDE_EFFECTING


def _gather_copies(refs, send, recv, rels=(0, 1, 2)):
    x, y, c, chips = _place()
    s = 2 * x + y
    n = len(rels)
    return [pltpu.make_async_remote_copy(src_ref=ref.at[s, c], dst_ref=ref.at[s, c], send_sem=send.at[n * w + k],
                                         recv_sem=recv.at[n * w + k], device_id=(*chips[r], c), device_id_type=MESH)
            for w, ref in enumerate(refs) for k, r in enumerate(rels)]


def _scatter_copies(refs, send, recv):
    x, y, c, chips = _place()
    nw = len(refs) // 2
    return [pltpu.make_async_remote_copy(src_ref=refs[w].at[2 * tx + ty], dst_ref=refs[nw + w].at[r],
                                         send_sem=send.at[3 * w + r], recv_sem=recv.at[3 * w + r],
                                         device_id=(tx, ty, c), device_id_type=MESH)
            for w in range(nw) for r, (tx, ty) in enumerate(chips)]


def _pair_copies(refs, send, recv):
    x, y, c, _ = _place()
    nw = len(refs) // 2
    return [pltpu.make_async_remote_copy(src_ref=refs[w].at[j, 1 - c], dst_ref=refs[nw + w].at[j],
                                         send_sem=send.at[N_CHIPS * w + j], recv_sem=recv.at[N_CHIPS * w + j],
                                         device_id=(x, y, 1 - c), device_id_type=MESH)
            for w in range(nw) for j in range(N_CHIPS)]


def _start_copies(bufs, after, ncopies, make_copies, name):
    n = len(bufs)

    def body(*refs):
        in_refs, send, recv, token = refs[:n], refs[n + 1], refs[n + 2], refs[2 * n + 3]
        for cp in make_copies(in_refs, send, recv):
            cp.start()
        token[...] = jnp.zeros_like(token)

    outs = _pallas(
        body, name=name, in_specs=[HBM] * n + [ANY],
        out_specs=[SEM, SEM] + [HBM] * n + [pl.BlockSpec(memory_space=pltpu.VMEM)],
        out_shape=[pltpu.SemaphoreType.DMA((ncopies,)), pltpu.SemaphoreType.DMA((ncopies,))]
                  + [pltpu.HBM(b.shape, b.dtype) for b in bufs] + [jax.ShapeDtypeStruct((8, 128), F32)],
        input_output_aliases={k: 2 + k for k in range(n)},
        compiler_params=pltpu.CompilerParams(has_side_effects=EFFECT),
    )(*[pltpu.with_memory_space_constraint(b, pltpu.HBM) for b in bufs], after)
    return outs[0], outs[1], list(outs[2:2 + n]), outs[2 + n]


def _wait_copies(send, recv, bufs, after, make_copies, name):
    n = len(bufs)

    def body(*refs):
        in_refs, send_ref, recv_ref = refs[:n], refs[n], refs[n + 1]
        for cp in make_copies(in_refs, send_ref, recv_ref):
            cp.wait_send()
            cp.wait_recv()

    outs = _pallas(
        body, name=name, in_specs=[HBM] * n + [SEM, SEM, ANY], out_specs=[HBM] * n,
        out_shape=[pltpu.HBM(b.shape, b.dtype) for b in bufs],
        input_output_aliases={k: k for k in range(n)},
        compiler_params=pltpu.CompilerParams(has_side_effects=EFFECT),
    )(*bufs, send, recv, after)
    return list(outs)


def _forward_halves(bufs, name, rels=(0, 1, 2)):
    nw = len(bufs)
    n = len(rels)

    def body(*refs):
        o_refs = refs[nw:2 * nw]
        send, recv = refs[2 * nw:]
        x, y, c, chips = _place()
        sib = (x, y, 1 - c)
        copies = []
        for w in range(nw):
            for k, r in enumerate(rels):
                tx, ty = chips[r]
                ref = o_refs[w].at[2 * tx + ty, c]
                cp = pltpu.make_async_remote_copy(src_ref=ref, dst_ref=ref, send_sem=send.at[n * w + k],
                                                  recv_sem=recv.at[n * w + k], device_id=sib, device_id_type=MESH)
                cp.start()
                copies.append(cp)
        for w in range(nw):
            for k, r in enumerate(rels):
                tx, ty = chips[r]
                ref = o_refs[w].at[2 * tx + ty, 1 - c]
                pltpu.make_async_remote_copy(src_ref=ref, dst_ref=ref, send_sem=send.at[n * w + k],
                                             recv_sem=recv.at[n * w + k], device_id=sib, device_id_type=MESH).wait_recv()
        for cp in copies:
            cp.wait_send()

    return _pallas(
        body, name=name, in_specs=[ANY] * nw, out_specs=[ANY] * nw,
        out_shape=[jax.ShapeDtypeStruct(b.shape, b.dtype) for b in bufs],
        input_output_aliases={w: w for w in range(nw)},
        scratch_shapes=[pltpu.SemaphoreType.DMA((n * nw,)), pltpu.SemaphoreType.DMA((n * nw,))],
    )(*bufs)


def _half_exchange(hs, name):
    nw = len(hs)

    def body(*refs):
        o_refs = refs[nw:2 * nw]
        send, recv = refs[2 * nw:]
        x, y, c, _ = _place()
        sib = (x, y, 1 - c)
        copies = []
        for w in range(nw):
            cp = pltpu.make_async_remote_copy(src_ref=o_refs[w].at[c], dst_ref=o_refs[w].at[c], send_sem=send.at[w],
                                              recv_sem=recv.at[w], device_id=sib, device_id_type=MESH)
            cp.start()
            copies.append(cp)
        for w, cp in enumerate(copies):
            cp.wait_send()
            pltpu.make_async_remote_copy(src_ref=o_refs[w].at[c], dst_ref=o_refs[w].at[1 - c], send_sem=send.at[w],
                                         recv_sem=recv.at[w], device_id=sib, device_id_type=MESH).wait_recv()

    return _pallas(
        body, name=name, in_specs=[ANY] * nw, out_specs=[ANY] * nw,
        out_shape=[jax.ShapeDtypeStruct(h.shape, F32) for h in hs],
        input_output_aliases={w: w for w in range(nw)},
        scratch_shapes=[pltpu.SemaphoreType.DMA((nw,)), pltpu.SemaphoreType.DMA((nw,))],
    )(*hs)


def _share_small(v, reduce, name, after):
    R, C = v.shape

    def body(v_ref, after_ref, o_ref, *scratch):
        if reduce:
            all_ref, send, recv, lsem = scratch
        else:
            all_ref = o_ref
            send, recv, lsem = scratch
        x, y, c, _ = _place()
        me = 4 * x + 2 * y + c
        loc = pltpu.make_async_copy(v_ref, all_ref.at[me], lsem)
        loc.start()
        copies = []
        for k in range(1, N_DEV):
            kx, ky, kc = (k >> 2) & 1, (k >> 1) & 1, k & 1
            peer = (x ^ kx, y ^ ky, c ^ kc)
            cp = pltpu.make_async_remote_copy(src_ref=v_ref, dst_ref=all_ref.at[me], send_sem=send.at[k - 1],
                                              recv_sem=recv.at[k - 1], device_id=peer, device_id_type=MESH)
            cp.start()
            copies.append(cp)
        for k in range(1, N_DEV):
            kx, ky, kc = (k >> 2) & 1, (k >> 1) & 1, k & 1
            src = 4 * (x ^ kx) + 2 * (y ^ ky) + (c ^ kc)
            pltpu.make_async_remote_copy(src_ref=v_ref, dst_ref=all_ref.at[src], send_sem=send.at[k - 1],
                                         recv_sem=recv.at[k - 1], device_id=(x, y, c), device_id_type=MESH).wait_recv()
        for cp in copies:
            cp.wait_send()
        loc.wait()
        if reduce:
            total = all_ref[0]
            for d in range(1, N_DEV):
                total = total + all_ref[d]
            o_ref[...] = total

    vm = pl.BlockSpec(memory_space=pltpu.VMEM)
    sems = [pltpu.SemaphoreType.DMA((N_DEV - 1,)), pltpu.SemaphoreType.DMA((N_DEV - 1,)), pltpu.SemaphoreType.DMA]
    if reduce:
        out_shape = jax.ShapeDtypeStruct((R, C), F32)
        scratch = [pltpu.VMEM((N_DEV, R, C), F32)] + sems
    else:
        out_shape = jax.ShapeDtypeStruct((N_DEV, R, C), F32)
        scratch = sems
    return _pallas(
        body, name=name, in_specs=[vm, ANY], out_specs=vm, out_shape=out_shape, scratch_shapes=scratch,
        compiler_params=pltpu.CompilerParams(vmem_limit_bytes=int(min(4 * N_DEV * R * C * 4 + 2 ** 24, 2 ** 25 + 2 ** 24))),
    )(v, after)


def _pair_sum(place, g, rb, name):
    _, _, Rh, C = g.shape
    tr = _row_tile(Rh, 256, 16)

    def body(place_ref, g_ref, r_ref, q_ref):
        q_ref[...] = (g_ref[...] + r_ref[...]).astype(BF16)

    blk = 2 * _nbytes((tr, C), F32) + _nbytes((tr, C), BF16)
    return _pallas(
        body, name=name,
        grid_spec=pltpu.PrefetchScalarGridSpec(
            num_scalar_prefetch=1, grid=(N_CHIPS - 1, Rh // tr),
            in_specs=[pl.BlockSpec((None, None, tr, C), lambda j, i, p: (p[0] ^ (j + 1), p[1], i, 0)),
                      pl.BlockSpec((None, tr, C), lambda j, i, p: (p[0] ^ (j + 1), i, 0))],
            out_specs=pl.BlockSpec((None, tr, C), lambda j, i, p: (p[0] ^ (j + 1), i, 0))),
        out_shape=jax.ShapeDtypeStruct((N_CHIPS, Rh, C), BF16),
        compiler_params=_params(("parallel", "parallel"), blk),
    )(place, g, rb)


def _chip_sum(place, g, rb, rc, name):
    _, _, Rh, C = g.shape
    tr = _row_tile(Rh, 256, 16)

    def body(place_ref, g_ref, r_ref, rc_ref, o_ref):
        total = g_ref[...] + r_ref[...]
        for r in range(3):
            total = total + rc_ref[r].astype(F32)
        o_ref[...] = total

    blk = 3 * _nbytes((tr, C), F32) + 3 * _nbytes((tr, C), BF16)
    return _pallas(
        body, name=name,
        grid_spec=pltpu.PrefetchScalarGridSpec(
            num_scalar_prefetch=1, grid=(Rh // tr,),
            in_specs=[pl.BlockSpec((None, None, tr, C), lambda i, p: (p[0], p[1], i, 0)),
                      pl.BlockSpec((None, tr, C), lambda i, p: (p[0], i, 0)),
                      pl.BlockSpec((3, tr, C), lambda i, p: (0, i, 0))],
            out_specs=pl.BlockSpec((None, tr, C), lambda i, p: (p[1], i, 0))),
        out_shape=jax.ShapeDtypeStruct((2, Rh, C), F32),
        compiler_params=_params(("parallel",), blk),
    )(place, g, rb, rc)


def _adamw_math(w, g, m, v):
    m = ADAM_B1 * m + (1.0 - ADAM_B1) * g
    v = ADAM_B2 * v + (1.0 - ADAM_B2) * jnp.square(g)
    m_hat = m / (1.0 - ADAM_B1 ** ADAM_STEP)
    v_hat = v / (1.0 - ADAM_B2 ** ADAM_STEP)
    delta = -ADAM_LR * (m_hat / (jnp.sqrt(v_hat) + ADAM_EPS) + ADAM_WD * w)
    return delta, m, v


def _adamw(w, g, m, v, name):
    R, C = w.shape
    tr = _row_tile(R, 256)

    def body(w_ref, g_ref, m_ref, v_ref, go_ref, d_ref, nm_ref, nv_ref):
        gv = g_ref[...]
        d, nm, nv = _adamw_math(w_ref[...], gv, m_ref[...], v_ref[...])
        go_ref[...] = gv
        d_ref[...] = d
        nm_ref[...] = nm
        nv_ref[...] = nv

    spec = pl.BlockSpec((tr, C), lambda i: (i, 0))
    shp = jax.ShapeDtypeStruct((R, C), F32)
    return _pallas(
        body, name=name, grid=(R // tr,), in_specs=[spec] * 4, out_specs=[spec] * 4, out_shape=[shp] * 4,
        compiler_params=_params(("parallel",), 8 * _nbytes((tr, C), F32)),
    )(w, g, m, v)


def _adamw_small(ws, gs, ms, vs):
    n = len(ws)

    def body(*refs):
        for k in range(n):
            w_ref, g_ref, m_ref, v_ref = (refs[q * n + k] for q in range(4))
            d, nm, nv = _adamw_math(w_ref[...], g_ref[...], m_ref[...], v_ref[...])
            refs[4 * n + k][...] = d
            refs[5 * n + k][...] = nm
            refs[6 * n + k][...] = nv

    vm = pl.BlockSpec(memory_space=pltpu.VMEM)
    shapes = [jax.ShapeDtypeStruct(w.shape, F32) for w in ws]
    outs = _pallas(
        body, name="adamw_small", in_specs=[vm] * (4 * n), out_specs=[vm] * (3 * n), out_shape=shapes * 3,
    )(*ws, *gs, *ms, *vs)
    return outs[:n], outs[n:2 * n], outs[2 * n:]


def _pad_rows(a, rows):
    return jnp.pad(a, ((0, rows - a.shape[0]), (0, 0)))


def kernel(x, meta_tokens, ffn1_norm, ffn1_w_gate, ffn1_w_up, ffn1_w_down, mix_norm, w_in, b_in, conv_sc_w, conv_cf_w, conv_cf_b, ln_cf_g, ln_cf_b, w_out, ffn2_norm, ffn2_w_gate, ffn2_w_up, ffn2_w_down, final_norm, loss_target, m_meta_tokens, m_ffn1_norm, m_ffn1_w_gate, m_ffn1_w_up, m_ffn1_w_down, m_mix_norm, m_w_in, m_b_in, m_conv_sc_w, m_conv_cf_w, m_conv_cf_b, m_ln_cf_g, m_ln_cf_b, m_w_out, m_ffn2_norm, m_ffn2_w_gate, m_ffn2_w_up, m_ffn2_w_down, m_final_norm, v_meta_tokens, v_ffn1_norm, v_ffn1_w_gate, v_ffn1_w_up, v_ffn1_w_down, v_mix_norm, v_w_in, v_b_in, v_conv_sc_w, v_conv_cf_w, v_conv_cf_b, v_ln_cf_g, v_ln_cf_b, v_w_out, v_ffn2_norm, v_ffn2_w_gate, v_ffn2_w_up, v_ffn2_w_down, v_final_norm):
    xi, yi, ci = lax.axis_index("x"), lax.axis_index("y"), lax.axis_index("c")
    chip = 2 * xi + yi
    place = jnp.stack([chip, ci]).astype(jnp.int32)

    x2 = x[0]
    tgt = loss_target[0]
    S, D = x2.shape
    C1 = D // 2
    cs = conv_sc_w.shape[2]
    ksc, kcf = conv_sc_w.shape[1], conv_cf_w.shape[1]
    ms = meta_tokens.shape[1]

    rows_small = N_META + 8 + 32
    assert ksc <= 8 and kcf <= 32 and cs <= ms
    pack = jnp.concatenate([
        meta_tokens,
        jnp.pad(conv_sc_w[0], ((0, 8 - ksc), (0, ms - cs))),
        jnp.pad(conv_cf_w[0], ((0, 32 - kcf), (0, ms - cs)))], axis=0)
    everyone = _share_small(pack, False, "share_params", pack)[0::2]
    meta_full = jnp.transpose(everyone[:, :N_META, :], (1, 0, 2)).reshape(N_META, D)
    wsc_full = jnp.transpose(everyone[:, N_META:N_META + ksc, :cs], (1, 0, 2)).reshape(ksc, C1)
    wcf_full = jnp.transpose(everyone[:, N_META + 8:N_META + 8 + kcf, :cs], (1, 0, 2)).reshape(kcf, C1)

    big = {"ffn1_w_gate": ffn1_w_gate, "ffn1_w_up": ffn1_w_up, "ffn1_w_down": ffn1_w_down, "w_in": w_in, "w_out": w_out,
           "ffn2_w_gate": ffn2_w_gate, "ffn2_w_up": ffn2_w_up, "ffn2_w_down": ffn2_w_down}
    big_m = {"ffn1_w_gate": m_ffn1_w_gate, "ffn1_w_up": m_ffn1_w_up, "ffn1_w_down": m_ffn1_w_down, "w_in": m_w_in,
             "w_out": m_w_out, "ffn2_w_gate": m_ffn2_w_gate, "ffn2_w_up": m_ffn2_w_up, "ffn2_w_down": m_ffn2_w_down}
    big_v = {"ffn1_w_gate": v_ffn1_w_gate, "ffn1_w_up": v_ffn1_w_up, "ffn1_w_down": v_ffn1_w_down, "w_in": v_w_in,
             "w_out": v_w_out, "ffn2_w_gate": v_ffn2_w_gate, "ffn2_w_up": v_ffn2_w_up, "ffn2_w_down": v_ffn2_w_down}
    buf = {nm: _cast_own_block(place, w[0], "cast_" + nm) for nm, w in big.items()}
    whole_weight = lambda g: g.reshape(N_CHIPS, 2 * g.shape[2], g.shape[3])
    corner = lambda a: a.reshape(-1, a.shape[-1])[:8, :128]

    NEAR, FAR = (0, 1), (2,)
    groups = {"ffn1_near": (["ffn1_w_gate", "ffn1_w_up", "ffn1_w_down"], NEAR),
              "ffn1_far": (["ffn1_w_gate", "ffn1_w_up", "ffn1_w_down"], FAR),
              "mix": (["w_in", "w_out"], NEAR + FAR),
              "ffn2_up": (["ffn2_w_gate", "ffn2_w_up"], NEAR + FAR),
              "ffn2_down": (["ffn2_w_down"], NEAR + FAR)}
    started = {}

    def start(tag, after):
        nms, rels = groups[tag]
        copies = functools.partial(_gather_copies, rels=rels)
        send, recv, thru, token = _start_copies([buf[nm] for nm in nms], after, len(rels) * len(nms), copies,
                                                "gather_start_" + tag)
        for nm, b in zip(nms, thru):
            buf[nm] = b
        started[tag] = (send, recv, copies)
        return token

    def arrive(tag, after, then=None):
        nms, rels = groups[tag]
        send, recv, copies = started[tag]
        got = _wait_copies(send, recv, [buf[nm] for nm in nms], corner(after), copies, "gather_wait_" + tag)
        for nm, b in zip(nms, got):
            buf[nm] = b
        if then is not None:
            start(then, corner(got[0]))
        for nm, b in zip(nms, _forward_halves([buf[nm] for nm in nms], "gather_forward_" + tag, rels)):
            buf[nm] = b

    token = start("ffn1_near", corner(everyone))

    ffn1 = lambda: [whole_weight(buf[nm]) for nm in ["ffn1_w_gate", "ffn1_w_up", "ffn1_w_down"]]
    own = chip[None].astype(jnp.int32)
    near = jnp.stack([chip ^ 2, chip ^ 1]).astype(jnp.int32)
    far = (chip ^ 3)[None].astype(jnp.int32)
    all_chips = jnp.arange(N_CHIPS, dtype=jnp.int32)

    hs0, n1 = _embed_rms(x2, meta_full, ffn1_norm)
    wg1, wu1, wd1 = ffn1()
    gua = _ffn_up(n1, wg1, wu1, own, None, token, "ffn1_up_own")
    hs1 = _ffn_down(gua[2], wd1, hs0, own, "ffn1_down_own")
    arrive("ffn1_near", hs1, "ffn1_far")
    wg1, wu1, wd1 = ffn1()
    gua = _ffn_up(n1, wg1, wu1, near, gua, token, "ffn1_up_near")
    tok = start("mix", corner(gua[2]))
    hs1 = _ffn_down(gua[2], wd1, hs1, near, "ffn1_down_near", tok)
    arrive("ffn1_far", hs1)
    wg1, wu1, wd1 = ffn1()
    g1, u1, a1 = _ffn_up(n1, wg1, wu1, far, gua, token, "ffn1_up_far")
    hs1 = _ffn_down(a1, wd1, hs1, far, "ffn1_down_far")
    F = N_CHIPS * wd1.shape[1]
    tok = start("ffn2_up", corner(hs1))
    arrive("mix", tok)
    win, wout = whole_weight(buf["w_in"]), whole_weight(buf["w_out"])
    n2 = _rms(hs1, mix_norm, "rms_mix")
    u = _mix_in(n2, win, b_in)
    y = _mix_conv_fwd(u, wsc_full, wcf_full, conv_cf_b, ln_cf_g, ln_cf_b)
    tok = start("ffn2_down", corner(y))
    hs2 = _mix_out(y, wout.reshape(D, D), hs1, tok)
    arrive("ffn2_up", hs2)
    wg2, wu2 = whole_weight(buf["ffn2_w_gate"]), whole_weight(buf["ffn2_w_up"])
    n3 = _rms(hs2, ffn2_norm, "rms_ffn2")
    g2, u2, a2 = _ffn_up(n3, wg2, wu2, all_chips, None, token, "ffn2_up")
    arrive("ffn2_down", a2)
    wd2 = whole_weight(buf["ffn2_w_down"])
    hs3 = _ffn_down_whole(a2, wd2.reshape(F, D), hs2, "ffn2_down")
    token_ffn2 = token

    def pair_start(group, after, tag):
        gs = [g for _, g in group]
        lands = [lax.empty((N_CHIPS,) + g.shape[2:], F32) for g in gs]
        send, recv, thru, token = _start_copies(gs + lands, after, N_CHIPS * len(gs), _pair_copies,
                                                "pair_start_" + tag)
        return (group, send, recv, thru, tag), token

    def scatter_start(state, after):
        group, send, recv, thru, tag = state
        thru = _wait_copies(send, recv, thru, corner(after), _pair_copies, "pair_wait_" + tag)
        gs, sib = thru[:len(group)], thru[len(group):]
        sums = [_pair_sum(place, g, rb, "pair_sum_" + nm) for (nm, _), g, rb in zip(group, gs, sib)]
        lands = [lax.empty((3,) + q.shape[1:], BF16) for q in sums]
        send, recv, thru, token = _start_copies(sums + lands, corner(sums[-1]), 3 * len(gs), _scatter_copies,
                                                "scatter_start_" + tag)
        return ([(nm, g) for (nm, _), g in zip(group, gs)], sib, send, recv, thru, tag), token

    def reduce_finish(state, after):
        group, sib, send, recv, thru, tag = state
        lands = _wait_copies(send, recv, thru, corner(after), _scatter_copies, "scatter_wait_" + tag)[len(group):]
        mine = [_chip_sum(place, g, rb, rc, "chip_sum_" + nm) for (nm, g), rb, rc in zip(group, sib, lands)]
        whole = _half_exchange(mine, "half_exchange_" + tag)
        out = {}
        for (nm, _), g in zip(group, whole):
            w = big[nm]
            g_out, d, new_m, new_v = _adamw(w[0], g.reshape(w.shape[1:]), big_m[nm][0], big_v[nm][0], "adamw_" + nm)
            out[nm] = (g_out[None], d[None], new_m[None], new_v[None])
        return out

    dhs3, df2, loss_row, d_final = _final_loss(hs3, final_norm.reshape(1, D), tgt)

    dg2, du2 = _ffn_bwd_act(df2, wd2.reshape(F, D), g2, u2, token_ffn2, "ffn2_bwd_act")
    gw_d2 = _wgrad_down(a2, df2, "wgrad_ffn2_down")
    gw_g2 = _wgrad_cols(n3, [dg2], "wgrad_ffn2_gate")[0]
    gw_u2 = _wgrad_cols(n3, [du2], "wgrad_ffn2_up")[0]
    pair_ffn2, token = pair_start([("ffn2_w_gate", gw_g2), ("ffn2_w_up", gw_u2), ("ffn2_w_down", gw_d2)],
                                  corner(gw_u2), "ffn2")
    dn3 = _nt_panel([dg2, du2], [wg2, wu2], token, "ffn2_bwd_in")
    red_ffn2, token = scatter_start(pair_ffn2, dn3)
    dhs2, dm, d_ffn2 = _rms_bwd(dn3, hs2, ffn2_norm, dhs3, 1.0, "rms_bwd_ffn2")

    dy = _nt_panel([dm], [wout.reshape(1, D, D)], token, "mix_bwd_out")
    gw_out = _wgrad_out(y, dm)
    dz1, dcs, db, d_lg, d_lb, d_bcf = _mix_conv_bwd1(u, dy, wsc_full, wcf_full, conv_cf_b, ln_cf_g, ln_cf_b)
    du, d_bin, d_wsc, d_wcf = _mix_conv_bwd2(u, dz1, dcs, db, wsc_full, wcf_full)
    gw_in = _wgrad_cols(n2, [du], "wgrad_w_in")[0]
    pair_mix, token = pair_start([("w_in", gw_in), ("w_out", gw_out)], corner(gw_in), "mix")
    dn2 = _nt_panel([du], [win], token, "mix_bwd_in")
    red_mix, token = scatter_start(pair_mix, dn2)
    dhs1, df1, d_mix = _rms_bwd(dn2, hs1, mix_norm, dhs2, FFN_RES_SCALE, "rms_bwd_mix")

    dg1, du1 = _ffn_bwd_act(df1, wd1.reshape(F, D), g1, u1, token, "ffn1_bwd_act")
    gw_d1 = _wgrad_down(a1, df1, "wgrad_ffn1_down")
    gw_g1 = _wgrad_cols(n1, [dg1], "wgrad_ffn1_gate")[0]
    pair_ffn1a, token = pair_start([("ffn1_w_down", gw_d1), ("ffn1_w_gate", gw_g1)], corner(gw_g1), "ffn1a")
    gw_u1 = _wgrad_cols(n1, [du1], "wgrad_ffn1_up", token)[0]
    red_ffn1a, token = scatter_start(pair_ffn1a, gw_u1)
    pair_ffn1b, token = pair_start([("ffn1_w_up", gw_u1)], token, "ffn1b")
    dn1 = _nt_panel([dg1, du1], [wg1, wu1], token, "ffn1_bwd_in")
    red_ffn1b, token = scatter_start(pair_ffn1b, dn1)
    grad_x, d_meta, d_ffn1 = _rms_bwd_first(dn1, hs0, ffn1_norm, dhs1, token)

    big_out = reduce_finish(red_ffn2, grad_x)
    big_out.update(reduce_finish(red_mix, big_out["ffn2_w_down"][1]))
    big_out.update(reduce_finish(red_ffn1a, big_out["w_out"][1]))
    big_out.update(reduce_finish(red_ffn1b, big_out["ffn1_w_gate"][1]))

    W = C1
    rows = lambda a: a.reshape(-1, W)
    parts = [rows(d_ffn1), rows(d_mix), rows(d_ffn2), rows(d_final), rows(d_bin), d_bcf, d_lg, d_lb,
             d_wsc, d_wcf, rows(d_meta), jnp.broadcast_to(loss_row[:, :1], (1, W))]
    sizes = [p.shape[0] for p in parts]
    total_rows = sum(sizes)
    packed = _pad_rows(jnp.concatenate(parts, axis=0), -(-total_rows // 8) * 8)
    summed = _share_small(packed, True, "sum_small", big_out["ffn1_w_gate"][1])
    offs = [0]
    for n in sizes:
        offs.append(offs[-1] + n)
    piece = lambda k: summed[offs[k]:offs[k + 1]]
    loss = piece(11)[0, 0]
    g_ffn1, g_mix, g_ffn2 = (piece(k).reshape(1, D) for k in range(3))
    g_final = piece(3).reshape(1, D)
    g_bin = piece(4).reshape(1, -1)
    g_bcf, g_lg, g_lb = piece(5), piece(6), piece(7)
    g_wsc = lax.dynamic_slice_in_dim(piece(8), chip * cs, cs, axis=1)
    g_wcf = lax.dynamic_slice_in_dim(piece(9), chip * cs, cs, axis=1)
    g_meta = lax.dynamic_slice_in_dim(piece(10).reshape(N_META, D), chip * ms, ms, axis=1)

    small_names = ["meta_tokens", "ffn1_norm", "mix_norm", "b_in", "conv_sc_w", "conv_cf_w", "conv_cf_b", "ln_cf_g",
                   "ln_cf_b", "ffn2_norm", "final_norm"]
    small_w = [meta_tokens, ffn1_norm, mix_norm, b_in, conv_sc_w[0], conv_cf_w[0], conv_cf_b, ln_cf_g, ln_cf_b,
               ffn2_norm, final_norm.reshape(1, D)]
    small_g = [g_meta, g_ffn1, g_mix, g_bin, g_wsc, g_wcf, g_bcf, g_lg, g_lb, g_ffn2, g_final]
    small_m = [m_meta_tokens, m_ffn1_norm, m_mix_norm, m_b_in, m_conv_sc_w[0], m_conv_cf_w[0], m_conv_cf_b, m_ln_cf_g,
               m_ln_cf_b, m_ffn2_norm, m_final_norm.reshape(1, D)]
    small_v = [v_meta_tokens, v_ffn1_norm, v_mix_norm, v_b_in, v_conv_sc_w[0], v_conv_cf_w[0], v_conv_cf_b, v_ln_cf_g,
               v_ln_cf_b, v_ffn2_norm, v_final_norm.reshape(1, D)]
    s_d, s_m, s_v = _adamw_small(small_w, small_g, small_m, small_v)
    shapes = {"conv_sc_w": conv_sc_w.shape, "conv_cf_w": conv_cf_w.shape, "final_norm": final_norm.shape}
    small_out = {}
    for nm, g, d, m, v in zip(small_names, small_g, s_d, s_m, s_v):
        shp = shapes.get(nm, g.shape)
        small_out[nm] = tuple(t.reshape(shp) for t in (g, d, m, v))

    order = ["meta_tokens", "ffn1_norm", "ffn1_w_gate", "ffn1_w_up", "ffn1_w_down", "mix_norm", "w_in", "b_in",
             "conv_sc_w", "conv_cf_w", "conv_cf_b", "ln_cf_g", "ln_cf_b", "w_out", "ffn2_norm", "ffn2_w_gate",
             "ffn2_w_up", "ffn2_w_down", "final_norm"]
    res = {**big_out, **small_out}
    outs = [loss, grad_x[None]]
    for q in range(4):
        outs.extend(res[nm][q] for nm in order)
    return tuple(outs)
```

```python
import functools

import jax
import jax.numpy as jnp
from jax import lax
from jax.experimental import pallas as pl
from jax.experimental.pallas import tpu as pltpu

F32 = jnp.float32
BF16 = jnp.bfloat16
MESH = pl.DeviceIdType.MESH

N_META = 16
TT = 128
PAD = TT - N_META
HALO = 32
EPS = 1e-6
FFN_RES_SCALE = 0.5
N_CHIPS = 4
N_DEV = 8

ADAM_LR = 0.001
ADAM_B1 = 0.9
ADAM_B2 = 0.999
ADAM_EPS = 1e-08
ADAM_WD = 0.01
ADAM_STEP = 10

V7X_VMEM_BYTES = 64 * 2 ** 20
NT_DIMS = (((1,), (1,)), ((), ()))
TN_DIMS = (((0,), (0,)), ((), ()))


def _params(semantics, block_bytes):
    limit = min(2 * block_bytes + 16 * 2 ** 20, V7X_VMEM_BYTES - 6 * 2 ** 20)
    return pltpu.CompilerParams(dimension_semantics=semantics, vmem_limit_bytes=int(limit))


def _pallas(body, out_shape, **kw):
    if "grid" not in kw and "grid_spec" not in kw:
        return pl.pallas_call(body, out_shape=out_shape, **kw)
    big = lambda shape, dtype: jnp.issubdtype(dtype, jnp.floating) and len(shape) >= 2
    pin_out = lambda s: pltpu.HBM(s.shape, s.dtype) if big(s.shape, s.dtype) else s
    single = not isinstance(out_shape, (list, tuple))
    shapes = pin_out(out_shape) if single else [pin_out(s) for s in out_shape]
    call = pl.pallas_call(body, out_shape=shapes, **kw)
    pin = lambda a: pltpu.with_memory_space_constraint(a, pltpu.HBM) if big(a.shape, a.dtype) else a
    return lambda *operands: call(*[pin(a) for a in operands])


def _nbytes(shape, dtype):
    n = 1
    for d in shape:
        if d is not None:
            n *= d
    return n * jnp.dtype(dtype).itemsize


def _row_tile(rows, target, mult=8):
    best = None
    for t in range(mult, min(rows, target) + 1, mult):
        if rows % t == 0:
            best = t
    assert best is not None, (rows, target, mult)
    return best


def _sigmoid(v):
    return jax.nn.sigmoid(v)


def _dsilu(v, s):
    return s * (1.0 + v * (1.0 - s))


def _embed_rms(x2, meta, gain):
    S, D = x2.shape
    T = S + TT

    def body(x_ref, meta_ref, g_ref, hs_ref, n_ref):
        i = pl.program_id(0)

        @pl.when(i == 0)
        def _():
            hs_ref[...] = jnp.zeros_like(hs_ref)
            hs_ref[PAD:, :] = meta_ref[...]

        @pl.when(i > 0)
        def _():
            hs_ref[...] = x_ref[...]

        h = hs_ref[...]
        r = lax.rsqrt(jnp.mean(h * h, axis=-1, keepdims=True) + EPS)
        n_ref[...] = ((h * r) * g_ref[...]).astype(BF16)

    blk = _nbytes((TT, D), F32) * 2 + _nbytes((TT, D), BF16)
    return _pallas(
        body, name="embed_rms", grid=(T // TT,),
        in_specs=[pl.BlockSpec((TT, D), lambda i: (jnp.maximum(i - 1, 0), 0)),
                  pl.BlockSpec((N_META, D), lambda i: (0, 0)),
                  pl.BlockSpec((1, D), lambda i: (0, 0))],
        out_specs=[pl.BlockSpec((TT, D), lambda i: (i, 0)), pl.BlockSpec((TT, D), lambda i: (i, 0))],
        out_shape=[jax.ShapeDtypeStruct((T, D), F32), jax.ShapeDtypeStruct((T, D), BF16)],
        compiler_params=_params(("parallel",), blk),
    )(x2, meta, gain)


def _rms(hs, gain, name):
    T, D = hs.shape
    te = _row_tile(T, 384)

    def body(h_ref, g_ref, n_ref):
        h = h_ref[...]
        r = lax.rsqrt(jnp.mean(h * h, axis=-1, keepdims=True) + EPS)
        n_ref[...] = ((h * r) * g_ref[...]).astype(BF16)

    blk = _nbytes((te, D), F32) + _nbytes((te, D), BF16)
    return _pallas(
        body, name=name, grid=(T // te,),
        in_specs=[pl.BlockSpec((te, D), lambda i: (i, 0)), pl.BlockSpec((1, D), lambda i: (0, 0))],
        out_specs=pl.BlockSpec((te, D), lambda i: (i, 0)),
        out_shape=jax.ShapeDtypeStruct((T, D), BF16),
        compiler_params=_params(("parallel",), blk),
    )(hs, gain)


def _rms_bwd_math(dn, h, g):
    r = lax.rsqrt(jnp.mean(h * h, axis=-1, keepdims=True) + EPS)
    xh = h * r
    dgain = jnp.sum(dn * xh, axis=0, keepdims=True)
    dxh = dn * g
    dh = r * (dxh - xh * jnp.mean(dxh * xh, axis=-1, keepdims=True))
    return dh, dgain


def _rms_bwd(dn, hs, gain, dres, scale, name):
    T, D = hs.shape
    te = _row_tile(T, 384)

    def body(dn_ref, h_ref, g_ref, dres_ref, dhs_ref, dhb_ref, dg_ref):
        dh, dgain = _rms_bwd_math(dn_ref[...], h_ref[...], g_ref[...])
        d = dres_ref[...] + dh
        dhs_ref[...] = d
        dhb_ref[...] = (scale * d).astype(BF16)

        @pl.when(pl.program_id(0) == 0)
        def _():
            dg_ref[...] = jnp.zeros_like(dg_ref)

        dg_ref[...] += dgain

    blk = _nbytes((te, D), F32) * 4 + _nbytes((te, D), BF16)
    row = lambda i: (i, 0)
    return _pallas(
        body, name=name, grid=(T // te,),
        in_specs=[pl.BlockSpec((te, D), row), pl.BlockSpec((te, D), row), pl.BlockSpec((1, D), lambda i: (0, 0)),
                  pl.BlockSpec((te, D), row)],
        out_specs=[pl.BlockSpec((te, D), row), pl.BlockSpec((te, D), row), pl.BlockSpec((1, D), lambda i: (0, 0))],
        out_shape=[jax.ShapeDtypeStruct((T, D), F32), jax.ShapeDtypeStruct((T, D), BF16),
                   jax.ShapeDtypeStruct((1, D), F32)],
        compiler_params=_params(("arbitrary",), blk),
    )(dn, hs, gain, dres)


def _rms_bwd_first(dn, hs, gain, dres, after):
    T, D = hs.shape
    S = T - TT

    def body(dn_ref, h_ref, g_ref, dres_ref, after_ref, gx_ref, gm_ref, dg_ref):
        i = pl.program_id(0)
        dh, dgain = _rms_bwd_math(dn_ref[...], h_ref[...], g_ref[...])
        d = dres_ref[...] + dh

        @pl.when(i == 0)
        def _():
            dg_ref[...] = jnp.zeros_like(dg_ref)
            gm_ref[...] = d[PAD:, :]

        @pl.when(i > 0)
        def _():
            gx_ref[...] = d

        dg_ref[...] += dgain

    blk = _nbytes((TT, D), F32) * 4
    row = lambda i: (i, 0)
    return _pallas(
        body, name="rms_bwd_ffn1", grid=(T // TT,),
        in_specs=[pl.BlockSpec((TT, D), row), pl.BlockSpec((TT, D), row), pl.BlockSpec((1, D), lambda i: (0, 0)),
                  pl.BlockSpec((TT, D), row), TOKEN],
        out_specs=[pl.BlockSpec((TT, D), lambda i: (jnp.maximum(i - 1, 0), 0)),
                   pl.BlockSpec((N_META, D), lambda i: (0, 0)), pl.BlockSpec((1, D), lambda i: (0, 0))],
        out_shape=[jax.ShapeDtypeStruct((S, D), F32), jax.ShapeDtypeStruct((N_META, D), F32),
                   jax.ShapeDtypeStruct((1, D), F32)],
        compiler_params=_params(("arbitrary",), blk),
    )(dn, hs, gain, dres, after)


def _final_loss(hs, gain, tgt):
    T, D = hs.shape

    def body(h_ref, g_ref, t_ref, dhs_ref, dhb_ref, loss_ref, dg_ref):
        i = pl.program_id(0)
        h = h_ref[...]
        g = g_ref[...]
        r = lax.rsqrt(jnp.mean(h * h, axis=-1, keepdims=True) + EPS)
        xh = h * r
        e = jnp.where(i > 0, xh * g - t_ref[...], 0.0)
        tile_loss = jnp.sum(jnp.sum(e * e, axis=1, keepdims=True), axis=0, keepdims=True) * (0.5 / D)
        dout = e * (1.0 / D)
        dgain = jnp.sum(dout * xh, axis=0, keepdims=True)
        dxh = dout * g
        d = r * (dxh - xh * jnp.mean(dxh * xh, axis=-1, keepdims=True))
        dhs_ref[...] = d
        dhb_ref[...] = (FFN_RES_SCALE * d).astype(BF16)

        @pl.when(i == 0)
        def _():
            loss_ref[...] = jnp.zeros_like(loss_ref)
            dg_ref[...] = jnp.zeros_like(dg_ref)

        loss_ref[...] += jnp.broadcast_to(tile_loss, loss_ref.shape)
        dg_ref[...] += dgain

    blk = _nbytes((TT, D), F32) * 3 + _nbytes((TT, D), BF16)
    row = lambda i: (i, 0)
    return _pallas(
        body, name="final_loss", grid=(T // TT,),
        in_specs=[pl.BlockSpec((TT, D), row), pl.BlockSpec((1, D), lambda i: (0, 0)),
                  pl.BlockSpec((TT, D), lambda i: (jnp.maximum(i - 1, 0), 0))],
        out_specs=[pl.BlockSpec((TT, D), row), pl.BlockSpec((TT, D), row),
                   pl.BlockSpec((1, 128), lambda i: (0, 0)), pl.BlockSpec((1, D), lambda i: (0, 0))],
        out_shape=[jax.ShapeDtypeStruct((T, D), F32), jax.ShapeDtypeStruct((T, D), BF16),
                   jax.ShapeDtypeStruct((1, 128), F32), jax.ShapeDtypeStruct((1, D), F32)],
        compiler_params=_params(("arbitrary",), blk),
    )(hs, gain, tgt)


MXU_COLS = 256


def _tm(T):
    return _row_tile(T, 704, 16)


def _col_chunks(n):
    return [(c, min(MXU_COLS, n - c)) for c in range(0, n, MXU_COLS)]


TOKEN = pl.BlockSpec((8, 128), lambda *_: (0, 0))


def _ffn_up(n, wg, wu, shards, prev, after, name):
    T, D = n.shape
    Fs = wg.shape[2]
    tm = _tm(T)
    nprev = 0 if prev is None else 3

    def body(shards_ref, n_ref, wg_ref, wu_ref, after_ref, *refs):
        g_ref, u_ref, a_ref = refs[nprev:]
        nn = n_ref[...]
        for c0, cw in _col_chunks(Fs):
            if 2 * cw == MXU_COLS:
                both = jnp.concatenate([wg_ref[:, c0:c0 + cw], wu_ref[:, c0:c0 + cw]], axis=1)
                gu = jnp.dot(nn, both, preferred_element_type=F32)
                g, u = gu[:, :cw], gu[:, cw:]
            else:
                g = jnp.dot(nn, wg_ref[:, c0:c0 + cw], preferred_element_type=F32)
                u = jnp.dot(nn, wu_ref[:, c0:c0 + cw], preferred_element_type=F32)
            g_ref[:, c0:c0 + cw] = g.astype(BF16)
            u_ref[:, c0:c0 + cw] = u.astype(BF16)
            a_ref[:, c0:c0 + cw] = (jax.nn.silu(g) * u).astype(BF16)

    blk = _nbytes((tm, D), BF16) + 2 * _nbytes((D, Fs), BF16) + 3 * _nbytes((tm, Fs), BF16)
    out = pl.BlockSpec((tm, Fs), lambda j, i, p: (i, p[j]))
    shp = jax.ShapeDtypeStruct((T, N_CHIPS * Fs), BF16)
    return _pallas(
        body, name=name,
        grid_spec=pltpu.PrefetchScalarGridSpec(
            num_scalar_prefetch=1, grid=(shards.shape[0], T // tm),
            in_specs=[pl.BlockSpec((tm, D), lambda j, i, p: (i, 0)),
                      pl.BlockSpec((None, D, Fs), lambda j, i, p: (p[j], 0, 0)),
                      pl.BlockSpec((None, D, Fs), lambda j, i, p: (p[j], 0, 0)), TOKEN] + [ANY] * nprev,
            out_specs=[out, out, out]),
        out_shape=[shp, shp, shp], input_output_aliases={5 + q: q for q in range(nprev)},
        compiler_params=_params(("arbitrary", "arbitrary"), blk),
    )(shards, n, wg, wu, after, *(prev or ()))


def _ffn_down(a, wd, hs, shards, name, after=None):
    T, F = a.shape
    _, Fs, D = wd.shape
    tm = _tm(T)
    tn = D // 2
    extra = [] if after is None else [after]

    def body(shards_ref, a_ref, w_ref, h_ref, *refs):
        o_ref = refs[-1]
        part = FFN_RES_SCALE * jnp.dot(a_ref[...], w_ref[...], preferred_element_type=F32)

        @pl.when(pl.program_id(2) == 0)
        def _():
            o_ref[...] = h_ref[...] + part

        @pl.when(pl.program_id(2) > 0)
        def _():
            o_ref[...] += part

    blk = _nbytes((tm, Fs), BF16) + _nbytes((Fs, tn), BF16) + 3 * _nbytes((tm, tn), F32)
    return _pallas(
        body, name=name,
        grid_spec=pltpu.PrefetchScalarGridSpec(
            num_scalar_prefetch=1, grid=(D // tn, T // tm, shards.shape[0]),
            in_specs=[pl.BlockSpec((tm, Fs), lambda n, i, k, p: (i, p[k])),
                      pl.BlockSpec((None, Fs, tn), lambda n, i, k, p: (p[k], 0, n)),
                      pl.BlockSpec((tm, tn), lambda n, i, k, p: (i, n))] + [TOKEN] * len(extra),
            out_specs=pl.BlockSpec((tm, tn), lambda n, i, k, p: (i, n))),
        out_shape=jax.ShapeDtypeStruct((T, D), F32),
        compiler_params=_params(("parallel", "parallel", "arbitrary"), blk),
    )(shards, a, wd, hs, *extra)


def _ffn_down_whole(a, wd, hs, name):
    T, F = a.shape
    D = wd.shape[1]
    tm = _tm(T)
    tn = D // 4

    def body(a_ref, w_ref, h_ref, o_ref):
        o_ref[...] = h_ref[...] + FFN_RES_SCALE * jnp.dot(a_ref[...], w_ref[...], preferred_element_type=F32)

    blk = _nbytes((tm, F), BF16) + _nbytes((F, tn), BF16) + 3 * _nbytes((tm, tn), F32)
    return _pallas(
        body, name=name, grid=(D // tn, T // tm),
        in_specs=[pl.BlockSpec((tm, F), lambda n, i: (i, 0)), pl.BlockSpec((F, tn), lambda n, i: (0, n)),
                  pl.BlockSpec((tm, tn), lambda n, i: (i, n))],
        out_specs=pl.BlockSpec((tm, tn), lambda n, i: (i, n)),
        out_shape=jax.ShapeDtypeStruct((T, D), F32),
        compiler_params=_params(("parallel", "parallel"), blk),
    )(a, wd, hs)


def _mix_in(n, w, b):
    T, D = n.shape
    Ns = w.shape[2]
    tm = _tm(T)

    def body(n_ref, w_ref, b_ref, u_ref):
        u_ref[...] = jnp.dot(n_ref[...], w_ref[...], preferred_element_type=F32) + b_ref[...]

    blk = _nbytes((tm, D), BF16) + _nbytes((D, Ns), BF16) + 2 * _nbytes((tm, Ns), F32)
    return _pallas(
        body, name="mix_in", grid=(N_CHIPS, T // tm),
        in_specs=[pl.BlockSpec((tm, D), lambda j, i: (i, 0)), pl.BlockSpec((None, D, Ns), lambda j, i: (j, 0, 0)),
                  pl.BlockSpec((1, Ns), lambda j, i: (0, j))],
        out_specs=pl.BlockSpec((tm, Ns), lambda j, i: (i, j)),
        out_shape=jax.ShapeDtypeStruct((T, N_CHIPS * Ns), F32),
        compiler_params=_params(("parallel", "parallel"), blk),
    )(n, w, b)


def _mix_out(y, w, hs, after):
    T, D = y.shape
    tm = _tm(T)

    def body(y_ref, w_ref, h_ref, after_ref, o_ref):
        o_ref[...] = h_ref[...] + jnp.dot(y_ref[...], w_ref[...], preferred_element_type=F32)

    blk = _nbytes((tm, D), BF16) + _nbytes((D, D), BF16) + 3 * _nbytes((tm, D), F32)
    return _pallas(
        body, name="mix_out", grid=(T // tm,),
        in_specs=[pl.BlockSpec((tm, D), lambda i: (i, 0)), pl.BlockSpec((D, D), lambda i: (0, 0)),
                  pl.BlockSpec((tm, D), lambda i: (i, 0)), TOKEN],
        out_specs=pl.BlockSpec((tm, D), lambda i: (i, 0)),
        out_shape=jax.ShapeDtypeStruct((T, D), F32),
        compiler_params=_params(("parallel",), blk),
    )(y, w, hs, after)


def _ffn_bwd_act(dfb, wd, g, u, after, name):
    T, D = dfb.shape
    F = wd.shape[0]
    tm = _row_tile(T, 1408, 16)
    tn = 2 * MXU_COLS

    tr = _row_tile(tm, 352, 16)

    def body(d_ref, w_ref, g_ref, u_ref, after_ref, dg_ref, du_ref):
        for r0 in range(0, tm, tr):
            dv = d_ref[r0:r0 + tr, :]
            for c0, cw in _col_chunks(tn):
                da = lax.dot_general(dv, w_ref[c0:c0 + cw, :], NT_DIMS, preferred_element_type=F32)
                gv = g_ref[r0:r0 + tr, c0:c0 + cw].astype(F32)
                uv = u_ref[r0:r0 + tr, c0:c0 + cw].astype(F32)
                s = _sigmoid(gv)
                du_ref[r0:r0 + tr, c0:c0 + cw] = (da * (gv * s)).astype(BF16)
                dg_ref[r0:r0 + tr, c0:c0 + cw] = (da * uv * _dsilu(gv, s)).astype(BF16)

    blk = _nbytes((tm, D), BF16) + _nbytes((tn, D), BF16) + 4 * _nbytes((tm, tn), BF16)
    io = pl.BlockSpec((tm, tn), lambda n, i: (i, n))
    shp = jax.ShapeDtypeStruct((T, F), BF16)
    return _pallas(
        body, name=name, grid=(F // tn, T // tm),
        in_specs=[pl.BlockSpec((tm, D), lambda n, i: (i, 0)), pl.BlockSpec((tn, D), lambda n, i: (n, 0)), io, io, TOKEN],
        out_specs=[io, io], out_shape=[shp, shp],
        compiler_params=_params(("parallel", "parallel"), blk),
    )(dfb, wd, g, u, after)


def _nt_panel(lhs_list, w_list, after, name):
    T = lhs_list[0].shape[0]
    nsh, Dout, Ks = w_list[0].shape
    npair = len(lhs_list)
    tm = _row_tile(T, 1408, 16)
    tn = Dout // 2

    def body(*refs):
        l_refs, w_refs, o_ref = refs[:npair], refs[npair:2 * npair], refs[2 * npair + 1]
        j = pl.program_id(2)
        k0 = Ks - Ks % MXU_COLS if npair == 2 and 2 * (Ks % MXU_COLS) == MXU_COLS else Ks
        acc = None
        for p in range(npair):
            part = lax.dot_general(l_refs[p][:, :k0], w_refs[p][:, :k0], NT_DIMS, preferred_element_type=F32)
            acc = part if acc is None else acc + part
        if k0 < Ks:
            lhs = jnp.concatenate([l_refs[p][:, k0:] for p in range(npair)], axis=1)
            rhs = jnp.concatenate([w_refs[p][:, k0:] for p in range(npair)], axis=1)
            acc = acc + lax.dot_general(lhs, rhs, NT_DIMS, preferred_element_type=F32)

        @pl.when(j == 0)
        def _():
            o_ref[...] = acc

        @pl.when(j > 0)
        def _():
            o_ref[...] += acc

    blk = npair * (_nbytes((tm, Ks), BF16) + _nbytes((tn, Ks), BF16)) + 2 * _nbytes((tm, tn), F32)
    return _pallas(
        body, name=name, grid=(Dout // tn, T // tm, nsh),
        in_specs=[pl.BlockSpec((tm, Ks), lambda n, i, j: (i, j))] * npair
                 + [pl.BlockSpec((None, tn, Ks), lambda n, i, j: (j, n, 0))] * npair + [TOKEN],
        out_specs=pl.BlockSpec((tm, tn), lambda n, i, j: (i, n)),
        out_shape=jax.ShapeDtypeStruct((T, Dout), F32),
        compiler_params=_params(("parallel", "parallel", "arbitrary"), blk),
    )(*lhs_list, *w_list, after)


def _tn_call(name, grid, lhs, lhs_spec, rhs_list, rhs_specs, out_shapes, out_specs, blk, after=None):
    nr = len(rhs_list)
    extra = [] if after is None else [after]

    def body(*refs):
        l_ref, r_refs, o_refs = refs[0], refs[1:1 + nr], refs[len(refs) - nr:]
        k = pl.program_id(len(grid) - 1)
        lv = l_ref[...]
        for q in range(nr):
            part = lax.dot_general(lv, r_refs[q][...], TN_DIMS, preferred_element_type=F32)
            part = part.reshape(o_refs[q].shape)

            @pl.when(k == 0)
            def _(o=o_refs[q], part=part):
                o[...] = part

            @pl.when(k > 0)
            def _(o=o_refs[q], part=part):
                o[...] += part

    return _pallas(
        body, name=name, grid=grid, in_specs=[lhs_spec] + rhs_specs + [TOKEN] * len(extra), out_specs=out_specs,
        out_shape=out_shapes, compiler_params=_params(("parallel",) * (len(grid) - 1) + ("arbitrary",), blk),
    )(lhs, *rhs_list, *extra)


def _tk(T):
    return T


def _wgrad_cols(n, rhs_list, name, after=None):
    T, D = n.shape
    Ns = rhs_list[0].shape[1] // N_CHIPS
    tk = _tk(T)
    nr = len(rhs_list)
    tm = D // 4
    blk = _nbytes((tk, tm), BF16) + nr * (_nbytes((tk, Ns), BF16) + 2 * _nbytes((tm, Ns), F32))
    return _tn_call(
        name, (N_CHIPS, D // tm, T // tk), n, pl.BlockSpec((tk, tm), lambda j, m, k: (k, m)),
        rhs_list, [pl.BlockSpec((tk, Ns), lambda j, m, k: (k, j))] * nr,
        [jax.ShapeDtypeStruct((N_CHIPS, 2, D // 2, Ns), F32)] * nr,
        [pl.BlockSpec((None, None, tm, Ns), lambda j, m, k: (j, m // 2, m % 2, 0))] * nr, blk, after)


def _wgrad_down(a, dfb, name):
    T, F = a.shape
    D = dfb.shape[1]
    Fs = F // N_CHIPS
    tk = _tk(T)
    tn = D // 4
    blk = _nbytes((tk, Fs), BF16) + _nbytes((tk, tn), BF16) + 2 * _nbytes((Fs, tn), F32)
    return _tn_call(
        name, (N_CHIPS, D // tn, T // tk), a, pl.BlockSpec((tk, Fs), lambda j, n, k: (k, j)),
        [dfb], [pl.BlockSpec((tk, tn), lambda j, n, k: (k, n))],
        [jax.ShapeDtypeStruct((N_CHIPS, 2, Fs // 2, D), F32)],
        [pl.BlockSpec((None, 2, Fs // 2, tn), lambda j, n, k: (j, 0, 0, n))], blk)[0]


def _wgrad_out(y, dmb):
    T, D = y.shape
    tk = _tk(T)
    tn = D // 2
    rows = D // (2 * N_CHIPS)
    blk = _nbytes((tk, D // 2), BF16) + _nbytes((tk, tn), BF16) + 2 * _nbytes((D // 2, tn), F32)
    return _tn_call(
        "wgrad_w_out", (2, D // tn, T // tk), y, pl.BlockSpec((tk, D // 2), lambda m, n, k: (k, m)),
        [dmb], [pl.BlockSpec((tk, tn), lambda m, n, k: (k, n))],
        [jax.ShapeDtypeStruct((N_CHIPS, 2, rows, D), F32)],
        [pl.BlockSpec((2, 2, rows, tn), lambda m, n, k: (m, 0, 0, n))], blk)[0]


def _row_masks(i, last):
    rows = i * TT + lax.broadcasted_iota(jnp.int32, (TT, 1), 0)
    prows = i * TT - HALO + lax.broadcasted_iota(jnp.int32, (HALO, 1), 0)
    return rows >= PAD, (prows >= PAD) & (i > 0), i < last


def _conv_inputs(u, up, mask_c, mask_p, zbuf, pbuf, C1):
    b, c, v, a, g = (u[:, k * C1:(k + 1) * C1] for k in range(5))
    cp, vp, ap, gp = (up[:, k * C1:(k + 1) * C1] for k in range(1, 5))
    sg = _sigmoid(g)
    pbuf[0:HALO, :] = jnp.where(mask_p, cp * vp, 0.0)
    pbuf[HALO:, :] = jnp.where(mask_c, c * v, 0.0)
    zbuf[0:HALO, :] = jnp.where(mask_p, ap * _sigmoid(gp), 0.0)
    zbuf[HALO:, :] = jnp.where(mask_c, a * sg, 0.0)
    return b, c, v, a, sg


SUBLANES = 8
SHIFT_ROWS = TT + HALO - SUBLANES


def _shifted_scratch(C1):
    return pltpu.VMEM((SUBLANES - 1, SHIFT_ROWS, C1), F32)


def _fill_shifted(buf, sh):
    for r in range(1, SUBLANES):
        sh[r - 1] = buf[r:r + SHIFT_ROWS, :]


LANES = 128


def _window(buf, sh, lo, c0):
    if sh is None or lo % SUBLANES == 0:
        return buf[lo:lo + TT, c0:c0 + LANES]
    q, r = divmod(lo, SUBLANES)
    return sh[r - 1, q * SUBLANES:q * SUBLANES + TT, c0:c0 + LANES]


def _tap_sum(w_ref, buf, sh, starts):
    chunks = []
    for c0 in range(0, buf.shape[1], LANES):
        acc = None
        for k, lo in enumerate(starts):
            term = w_ref[k:k + 1, c0:c0 + LANES] * _window(buf, sh, lo, c0)
            acc = term if acc is None else acc + term
        chunks.append(acc)
    return jnp.concatenate(chunks, axis=1)


def _causal_conv(w_ref, buf, sh=None):
    K = w_ref.shape[0]
    return _tap_sum(w_ref, buf, sh, [HALO - (K - 1) + k for k in range(K)])


def _anticausal_conv(w_ref, buf, sh=None):
    K = w_ref.shape[0]
    return _tap_sum(w_ref, buf, sh, [K - 1 - k for k in range(K)])


def _conv_weight_sums(dw_ref, dy, buf, sh=None):
    K = dw_ref.shape[0]
    for c0 in range(0, buf.shape[1], LANES):
        dyc = dy[:, c0:c0 + LANES]
        for k in range(K):
            prod = dyc * _window(buf, sh, HALO - (K - 1) + k, c0)
            dw_ref[k:k + 1, c0:c0 + LANES] += jnp.sum(prod, axis=0, keepdims=True)


def _layernorm_stats(z1):
    mu = jnp.mean(z1, axis=-1, keepdims=True)
    zc = z1 - mu
    rs = lax.rsqrt(jnp.mean(zc * zc, axis=-1, keepdims=True) + EPS)
    return zc * rs, rs


def _mixer_specs(T, DIN, C1, ksc, kcf):
    cur = pl.BlockSpec((TT, DIN), lambda i: (i, 0))
    prev = pl.BlockSpec((HALO, DIN), lambda i: (jnp.maximum(i * (TT // HALO) - 1, 0), 0))
    full = lambda r: pl.BlockSpec((r, C1), lambda i: (0, 0))
    return cur, prev, [full(ksc), full(kcf), full(1), full(1), full(1)]


def _mix_conv_fwd(u, wsc, wcf, bcf, lg, lb):
    T, DIN = u.shape
    C1 = DIN // 5
    last = T // TT - 1

    def body(u_ref, up_ref, wsc_ref, wcf_ref, bcf_ref, lg_ref, lb_ref, y_ref, zbuf, pbuf, zsh):
        i = pl.program_id(0)
        mask_c, mask_p, _ = _row_masks(i, last)
        b, _, _, _, _ = _conv_inputs(u_ref[...], up_ref[...], mask_c, mask_p, zbuf, pbuf, C1)
        _fill_shifted(zbuf, zsh)
        cs = _causal_conv(wsc_ref, pbuf)
        z1 = _causal_conv(wcf_ref, zbuf, zsh) + bcf_ref[...]
        zh, _ = _layernorm_stats(z1)
        ln = zh * lg_ref[...] + lb_ref[...]
        y_ref[:, 0:C1] = jnp.where(mask_c, b * cs, 0.0).astype(BF16)
        y_ref[:, C1:] = jnp.where(mask_c, jax.nn.silu(ln), 0.0).astype(BF16)

    cur, prev, small = _mixer_specs(T, DIN, C1, wsc.shape[0], wcf.shape[0])
    blk = _nbytes((TT + HALO, DIN), F32) + _nbytes((TT, 2 * C1), BF16) + 12 * _nbytes((TT + HALO, C1), F32)
    return _pallas(
        body, name="mix_conv_fwd", grid=(T // TT,),
        in_specs=[cur, prev] + small,
        out_specs=pl.BlockSpec((TT, 2 * C1), lambda i: (i, 0)),
        out_shape=jax.ShapeDtypeStruct((T, 2 * C1), BF16),
        scratch_shapes=[pltpu.VMEM((TT + HALO, C1), F32), pltpu.VMEM((TT + HALO, C1), F32), _shifted_scratch(C1)],
        compiler_params=_params(("arbitrary",), blk),
    )(u, u, wsc, wcf, bcf, lg, lb)


def _mix_conv_bwd1(u, dy, wsc, wcf, bcf, lg, lb):
    T, DIN = u.shape
    C1 = DIN // 5
    last = T // TT - 1

    def body(u_ref, up_ref, dy_ref, wsc_ref, wcf_ref, bcf_ref, lg_ref, lb_ref,
             dz1_ref, dcs_ref, db_ref, dlg_ref, dlb_ref, dbcf_ref, zbuf, pbuf, zsh):
        i = pl.program_id(0)
        mask_c, mask_p, _ = _row_masks(i, last)
        b, _, _, _, _ = _conv_inputs(u_ref[...], up_ref[...], mask_c, mask_p, zbuf, pbuf, C1)
        _fill_shifted(zbuf, zsh)
        cs = _causal_conv(wsc_ref, pbuf)
        z1 = _causal_conv(wcf_ref, zbuf, zsh) + bcf_ref[...]
        zh, rs = _layernorm_stats(z1)
        ln = zh * lg_ref[...] + lb_ref[...]
        dy = dy_ref[...]
        dysc = jnp.where(mask_c, dy[:, 0:C1], 0.0)
        dycf = jnp.where(mask_c, dy[:, C1:], 0.0)
        db_ref[...] = (dysc * cs).astype(BF16)
        dcs_ref[...] = dysc * b
        dl = dycf * _dsilu(ln, _sigmoid(ln))
        dzh = dl * lg_ref[...]
        dz1 = rs * (dzh - jnp.mean(dzh, axis=-1, keepdims=True) - zh * jnp.mean(dzh * zh, axis=-1, keepdims=True))
        dz1_ref[...] = dz1

        @pl.when(i == 0)
        def _():
            dlg_ref[...] = jnp.zeros_like(dlg_ref)
            dlb_ref[...] = jnp.zeros_like(dlb_ref)
            dbcf_ref[...] = jnp.zeros_like(dbcf_ref)

        dlg_ref[...] += jnp.sum(dl * zh, axis=0, keepdims=True)
        dlb_ref[...] += jnp.sum(dl, axis=0, keepdims=True)
        dbcf_ref[...] += jnp.sum(dz1, axis=0, keepdims=True)

    cur, prev, small = _mixer_specs(T, DIN, C1, wsc.shape[0], wcf.shape[0])
    tile = lambda: pl.BlockSpec((TT, C1), lambda i: (i, 0))
    vec = lambda: pl.BlockSpec((1, C1), lambda i: (0, 0))
    blk = _nbytes((TT + HALO, DIN), F32) + 4 * _nbytes((TT, C1), F32) + 16 * _nbytes((TT + HALO, C1), F32)
    return _pallas(
        body, name="mix_conv_bwd1", grid=(T // TT,),
        in_specs=[cur, prev, pl.BlockSpec((TT, 2 * C1), lambda i: (i, 0))] + small,
        out_specs=[tile(), tile(), tile(), vec(), vec(), vec()],
        out_shape=[jax.ShapeDtypeStruct((T, C1), F32), jax.ShapeDtypeStruct((T, C1), F32),
                   jax.ShapeDtypeStruct((T, C1), BF16)] + [jax.ShapeDtypeStruct((1, C1), F32)] * 3,
        scratch_shapes=[pltpu.VMEM((TT + HALO, C1), F32), pltpu.VMEM((TT + HALO, C1), F32), _shifted_scratch(C1)],
        compiler_params=_params(("arbitrary",), blk),
    )(u, u, dy, wsc, wcf, bcf, lg, lb)


def _mix_conv_bwd2(u, dz1, dcs, db, wsc, wcf):
    T, DIN = u.shape
    C1 = DIN // 5
    last = T // TT - 1
    ksc, kcf = wsc.shape[0], wcf.shape[0]

    def body(u_ref, up_ref, dz_ref, dzn_ref, dc_ref, dcn_ref, db_ref, wsc_ref, wcf_ref,
             du_ref, dbin_ref, dwsc_ref, dwcf_ref, zbuf, pbuf, dzbuf, dcbuf, zsh, dzsh):
        i = pl.program_id(0)
        mask_c, mask_p, has_next = _row_masks(i, last)
        _, c, v, a, sg = _conv_inputs(u_ref[...], up_ref[...], mask_c, mask_p, zbuf, pbuf, C1)
        dz1 = dz_ref[...]
        dcs = dc_ref[...]
        dzbuf[0:TT, :] = dz1
        dzbuf[TT:, :] = jnp.where(has_next, dzn_ref[...], 0.0)
        dcbuf[0:TT, :] = dcs
        dcbuf[TT:, :] = jnp.where(has_next, dcn_ref[...], 0.0)

        @pl.when(i == 0)
        def _():
            dbin_ref[...] = jnp.zeros_like(dbin_ref)
            dwsc_ref[...] = jnp.zeros_like(dwsc_ref)
            dwcf_ref[...] = jnp.zeros_like(dwcf_ref)

        _fill_shifted(zbuf, zsh)
        _fill_shifted(dzbuf, dzsh)
        _conv_weight_sums(dwcf_ref, dz1, zbuf, zsh)
        _conv_weight_sums(dwsc_ref, dcs, pbuf)
        dz0 = jnp.where(mask_c, _anticausal_conv(wcf_ref, dzbuf, dzsh), 0.0)
        dp = jnp.where(mask_c, _anticausal_conv(wsc_ref, dcbuf), 0.0)
        parts = (db_ref[...].astype(F32), dp * v, dp * c, dz0 * sg, dz0 * a * sg * (1.0 - sg))
        for k, part in enumerate(parts):
            du_ref[:, k * C1:(k + 1) * C1] = part.astype(BF16)
            dbin_ref[:, k * C1:(k + 1) * C1] += jnp.sum(part, axis=0, keepdims=True)

    cur, prev, small = _mixer_specs(T, DIN, C1, ksc, kcf)
    tile = lambda: pl.BlockSpec((TT, C1), lambda i: (i, 0))
    nxt = lambda: pl.BlockSpec((HALO, C1), lambda i: (jnp.minimum((i + 1) * (TT // HALO), T // HALO - 1), 0))
    blk = (_nbytes((TT + HALO, DIN), F32) + _nbytes((TT, DIN), BF16) + 5 * _nbytes((TT, C1), F32)
           + 16 * _nbytes((TT + HALO, C1), F32))
    buf = lambda: pltpu.VMEM((TT + HALO, C1), F32)
    return _pallas(
        body, name="mix_conv_bwd2", grid=(T // TT,),
        in_specs=[cur, prev, tile(), nxt(), tile(), nxt(), tile(), small[0], small[1]],
        out_specs=[pl.BlockSpec((TT, DIN), lambda i: (i, 0)), pl.BlockSpec((1, DIN), lambda i: (0, 0)),
                   pl.BlockSpec((ksc, C1), lambda i: (0, 0)), pl.BlockSpec((kcf, C1), lambda i: (0, 0))],
        out_shape=[jax.ShapeDtypeStruct((T, DIN), BF16), jax.ShapeDtypeStruct((1, DIN), F32),
                   jax.ShapeDtypeStruct((ksc, C1), F32), jax.ShapeDtypeStruct((kcf, C1), F32)],
        scratch_shapes=[buf(), buf(), buf(), buf(), _shifted_scratch(C1), _shifted_scratch(C1)],
        compiler_params=_params(("arbitrary",), blk),
    )(u, u, dz1, dz1, dcs, dcs, db, wsc, wcf)


def _place():
    x, y, c = lax.axis_index("x"), lax.axis_index("y"), lax.axis_index("c")
    chips = [(1 - x, y), (x, 1 - y), (1 - x, 1 - y)]
    return x, y, c, chips


ANY = pl.BlockSpec(memory_space=pl.ANY)


def _cast_own_block(place, w, name):
    R, C = w.shape
    tr = _row_tile(R // 2, 256, 16)
    nblk = R // 2 // tr

    def body(place_ref, w_ref, o_ref):
        o_ref[...] = w_ref[...].astype(BF16)

    return _pallas(
        body, name=name,
        grid_spec=pltpu.PrefetchScalarGridSpec(
            num_scalar_prefetch=1, grid=(2, nblk),
            in_specs=[pl.BlockSpec((tr, C), lambda h, i, p: (h * nblk + i, 0))],
            out_specs=pl.BlockSpec((None, None, tr, C), lambda h, i, p: (p[0], h, i, 0))),
        out_shape=jax.ShapeDtypeStruct((N_CHIPS, 2, R // 2, C), BF16),
        compiler_params=_params(("parallel", "parallel"), _nbytes((tr, C), F32) + _nbytes((tr, C), BF16)),
    )(place, w)


HBM = pl.BlockSpec(memory_space=pltpu.HBM)
SEM = pl.BlockSpec(memory_space=pltpu.SEMAPHORE)
EFFECT = pltpu.SideEffectType.DATAFLOW_SIDE_EFFECTING


def _gather_copies(refs, send, recv, rels=(0, 1, 2)):
    x, y, c, chips = _place()
    s = 2 * x + y
    n = len(rels)
    return [pltpu.make_async_remote_copy(src_ref=ref.at[s, c], dst_ref=ref.at[s, c], send_sem=send.at[n * w + k],
                                         recv_sem=recv.at[n * w + k], device_id=(*chips[r], c), device_id_type=MESH)
            for w, ref in enumerate(refs) for k, r in enumerate(rels)]


def _scatter_copies(refs, send, recv):
    x, y, c, chips = _place()
    nw = len(refs) // 2
    return [pltpu.make_async_remote_copy(src_ref=refs[w].at[2 * tx + ty], dst_ref=refs[nw + w].at[r],
                                         send_sem=send.at[3 * w + r], recv_sem=recv.at[3 * w + r],
                                         device_id=(tx, ty, c), device_id_type=MESH)
            for w in range(nw) for r, (tx, ty) in enumerate(chips)]


def _pair_copies(refs, send, recv):
    x, y, c, _ = _place()
    nw = len(refs) // 2
    return [pltpu.make_async_remote_copy(src_ref=refs[w].at[j, 1 - c], dst_ref=refs[nw + w].at[j],
                                         send_sem=send.at[N_CHIPS * w + j], recv_sem=recv.at[N_CHIPS * w + j],
                                         device_id=(x, y, 1 - c), device_id_type=MESH)
            for w in range(nw) for j in range(N_CHIPS)]


def _forward_copies(refs, send, recv, rels=(0, 1, 2)):
    x, y, c, chips = _place()
    n = len(rels)
    copies = []
    for w, ref in enumerate(refs):
        for k, r in enumerate(rels):
            tx, ty = chips[r]
            blk = ref.at[2 * tx + ty, c]
            copies.append(pltpu.make_async_remote_copy(src_ref=blk, dst_ref=blk, send_sem=send.at[n * w + k],
                                                       recv_sem=recv.at[n * w + k], device_id=(x, y, 1 - c),
                                                       device_id_type=MESH))
    return copies


def _half_copies(refs, send, recv):
    x, y, c, _ = _place()
    return [pltpu.make_async_remote_copy(src_ref=ref.at[c], dst_ref=ref.at[c], send_sem=send.at[w], recv_sem=recv.at[w],
                                         device_id=(x, y, 1 - c), device_id_type=MESH)
            for w, ref in enumerate(refs)]


def _start_copies(bufs, after, ncopies, make_copies, name):
    n = len(bufs)

    def body(*refs):
        in_refs, send, recv, token = refs[:n], refs[n + 1], refs[n + 2], refs[2 * n + 3]
        for cp in make_copies(in_refs, send, recv):
            cp.start()
        token[...] = jnp.zeros_like(token)

    outs = _pallas(
        body, name=name, in_specs=[HBM] * n + [ANY],
        out_specs=[SEM, SEM] + [HBM] * n + [pl.BlockSpec(memory_space=pltpu.VMEM)],
        out_shape=[pltpu.SemaphoreType.DMA((ncopies,)), pltpu.SemaphoreType.DMA((ncopies,))]
                  + [pltpu.HBM(b.shape, b.dtype) for b in bufs] + [jax.ShapeDtypeStruct((8, 128), F32)],
        input_output_aliases={k: 2 + k for k in range(n)},
        compiler_params=pltpu.CompilerParams(has_side_effects=EFFECT),
    )(*[pltpu.with_memory_space_constraint(b, pltpu.HBM) for b in bufs], after)
    return outs[0], outs[1], list(outs[2:2 + n]), outs[2 + n]


def _wait_copies(send, recv, bufs, after, make_copies, name):
    n = len(bufs)

    def body(*refs):
        in_refs, send_ref, recv_ref = refs[:n], refs[n], refs[n + 1]
        for cp in make_copies(in_refs, send_ref, recv_ref):
            cp.wait_send()
            cp.wait_recv()

    outs = _pallas(
        body, name=name, in_specs=[HBM] * n + [SEM, SEM, ANY], out_specs=[HBM] * n,
        out_shape=[pltpu.HBM(b.shape, b.dtype) for b in bufs],
        input_output_aliases={k: k for k in range(n)},
        compiler_params=pltpu.CompilerParams(has_side_effects=EFFECT),
    )(*bufs, send, recv, after)
    return list(outs)


def _forward_halves(bufs, name, rels=(0, 1, 2)):
    nw = len(bufs)
    n = len(rels)

    def body(*refs):
        o_refs = refs[nw:2 * nw]
        send, recv = refs[2 * nw:]
        x, y, c, chips = _place()
        sib = (x, y, 1 - c)
        copies = []
        for w in range(nw):
            for k, r in enumerate(rels):
                tx, ty = chips[r]
                ref = o_refs[w].at[2 * tx + ty, c]
                cp = pltpu.make_async_remote_copy(src_ref=ref, dst_ref=ref, send_sem=send.at[n * w + k],
                                                  recv_sem=recv.at[n * w + k], device_id=sib, device_id_type=MESH)
                cp.start()
                copies.append(cp)
        for w in range(nw):
            for k, r in enumerate(rels):
                tx, ty = chips[r]
                ref = o_refs[w].at[2 * tx + ty, 1 - c]
                pltpu.make_async_remote_copy(src_ref=ref, dst_ref=ref, send_sem=send.at[n * w + k],
                                             recv_sem=recv.at[n * w + k], device_id=sib, device_id_type=MESH).wait_recv()
        for cp in copies:
            cp.wait_send()

    return _pallas(
        body, name=name, in_specs=[ANY] * nw, out_specs=[ANY] * nw,
        out_shape=[jax.ShapeDtypeStruct(b.shape, b.dtype) for b in bufs],
        input_output_aliases={w: w for w in range(nw)},
        scratch_shapes=[pltpu.SemaphoreType.DMA((n * nw,)), pltpu.SemaphoreType.DMA((n * nw,))],
    )(*bufs)


def _share_small(v, reduce, name, after):
    R, C = v.shape

    def body(v_ref, after_ref, o_ref, *scratch):
        if reduce:
            all_ref, send, recv, lsem = scratch
        else:
            all_ref = o_ref
            send, recv, lsem = scratch
        x, y, c, _ = _place()
        me = 4 * x + 2 * y + c
        loc = pltpu.make_async_copy(v_ref, all_ref.at[me], lsem)
        loc.start()
        copies = []
        for k in range(1, N_DEV):
            kx, ky, kc = (k >> 2) & 1, (k >> 1) & 1, k & 1
            peer = (x ^ kx, y ^ ky, c ^ kc)
            cp = pltpu.make_async_remote_copy(src_ref=v_ref, dst_ref=all_ref.at[me], send_sem=send.at[k - 1],
                                              recv_sem=recv.at[k - 1], device_id=peer, device_id_type=MESH)
            cp.start()
            copies.append(cp)
        for k in range(1, N_DEV):
            kx, ky, kc = (k >> 2) & 1, (k >> 1) & 1, k & 1
            src = 4 * (x ^ kx) + 2 * (y ^ ky) + (c ^ kc)
            pltpu.make_async_remote_copy(src_ref=v_ref, dst_ref=all_ref.at[src], send_sem=send.at[k - 1],
                                         recv_sem=recv.at[k - 1], device_id=(x, y, c), device_id_type=MESH).wait_recv()
        for cp in copies:
            cp.wait_send()
        loc.wait()
        if reduce:
            total = all_ref[0]
            for d in range(1, N_DEV):
                total = total + all_ref[d]
            o_ref[...] = total

    vm = pl.BlockSpec(memory_space=pltpu.VMEM)
    sems = [pltpu.SemaphoreType.DMA((N_DEV - 1,)), pltpu.SemaphoreType.DMA((N_DEV - 1,)), pltpu.SemaphoreType.DMA]
    if reduce:
        out_shape = jax.ShapeDtypeStruct((R, C), F32)
        scratch = [pltpu.VMEM((N_DEV, R, C), F32)] + sems
    else:
        out_shape = jax.ShapeDtypeStruct((N_DEV, R, C), F32)
        scratch = sems
    return _pallas(
        body, name=name, in_specs=[vm, ANY], out_specs=vm, out_shape=out_shape, scratch_shapes=scratch,
        compiler_params=pltpu.CompilerParams(vmem_limit_bytes=int(min(4 * N_DEV * R * C * 4 + 2 ** 24, 2 ** 25 + 2 ** 24))),
    )(v, after)


def _pair_sum(place, g, rb, name):
    _, _, Rh, C = g.shape
    tr = _row_tile(Rh, 256, 16)

    def body(place_ref, g_ref, r_ref, q_ref):
        q_ref[...] = (g_ref[...] + r_ref[...]).astype(BF16)

    blk = 2 * _nbytes((tr, C), F32) + _nbytes((tr, C), BF16)
    return _pallas(
        body, name=name,
        grid_spec=pltpu.PrefetchScalarGridSpec(
            num_scalar_prefetch=1, grid=(N_CHIPS - 1, Rh // tr),
            in_specs=[pl.BlockSpec((None, None, tr, C), lambda j, i, p: (p[0] ^ (j + 1), p[1], i, 0)),
                      pl.BlockSpec((None, tr, C), lambda j, i, p: (p[0] ^ (j + 1), i, 0))],
            out_specs=pl.BlockSpec((None, tr, C), lambda j, i, p: (p[0] ^ (j + 1), i, 0))),
        out_shape=jax.ShapeDtypeStruct((N_CHIPS, Rh, C), BF16),
        compiler_params=_params(("parallel", "parallel"), blk),
    )(place, g, rb)


def _chip_sum(place, g, rb, rc, name):
    _, _, Rh, C = g.shape
    tr = _row_tile(Rh, 256, 16)

    def body(place_ref, g_ref, r_ref, rc_ref, o_ref):
        total = g_ref[...] + r_ref[...]
        for r in range(3):
            total = total + rc_ref[r].astype(F32)
        o_ref[...] = total

    blk = 3 * _nbytes((tr, C), F32) + 3 * _nbytes((tr, C), BF16)
    return _pallas(
        body, name=name,
        grid_spec=pltpu.PrefetchScalarGridSpec(
            num_scalar_prefetch=1, grid=(Rh // tr,),
            in_specs=[pl.BlockSpec((None, None, tr, C), lambda i, p: (p[0], p[1], i, 0)),
                      pl.BlockSpec((None, tr, C), lambda i, p: (p[0], i, 0)),
                      pl.BlockSpec((3, tr, C), lambda i, p: (0, i, 0))],
            out_specs=pl.BlockSpec((None, tr, C), lambda i, p: (p[1], i, 0))),
        out_shape=jax.ShapeDtypeStruct((2, Rh, C), F32),
        compiler_params=_params(("parallel",), blk),
    )(place, g, rb, rc)


def _adamw_math(w, g, m, v):
    m = ADAM_B1 * m + (1.0 - ADAM_B1) * g
    v = ADAM_B2 * v + (1.0 - ADAM_B2) * jnp.square(g)
    m_hat = m / (1.0 - ADAM_B1 ** ADAM_STEP)
    v_hat = v / (1.0 - ADAM_B2 ** ADAM_STEP)
    delta = -ADAM_LR * (m_hat / (jnp.sqrt(v_hat) + ADAM_EPS) + ADAM_WD * w)
    return delta, m, v


def _adamw(w, g, m, v, name):
    R, C = w.shape
    tr = _row_tile(R, 256)

    def body(w_ref, g_ref, m_ref, v_ref, go_ref, d_ref, nm_ref, nv_ref):
        gv = g_ref[...]
        d, nm, nv = _adamw_math(w_ref[...], gv, m_ref[...], v_ref[...])
        go_ref[...] = gv
        d_ref[...] = d
        nm_ref[...] = nm
        nv_ref[...] = nv

    spec = pl.BlockSpec((tr, C), lambda i: (i, 0))
    shp = jax.ShapeDtypeStruct((R, C), F32)
    return _pallas(
        body, name=name, grid=(R // tr,), in_specs=[spec] * 4, out_specs=[spec] * 4, out_shape=[shp] * 4,
        compiler_params=_params(("parallel",), 8 * _nbytes((tr, C), F32)),
    )(w, g, m, v)


def _adamw_small(ws, gs, ms, vs):
    n = len(ws)

    def body(*refs):
        for k in range(n):
            w_ref, g_ref, m_ref, v_ref = (refs[q * n + k] for q in range(4))
            d, nm, nv = _adamw_math(w_ref[...], g_ref[...], m_ref[...], v_ref[...])
            refs[4 * n + k][...] = d
            refs[5 * n + k][...] = nm
            refs[6 * n + k][...] = nv

    vm = pl.BlockSpec(memory_space=pltpu.VMEM)
    shapes = [jax.ShapeDtypeStruct(w.shape, F32) for w in ws]
    outs = _pallas(
        body, name="adamw_small", in_specs=[vm] * (4 * n), out_specs=[vm] * (3 * n), out_shape=shapes * 3,
    )(*ws, *gs, *ms, *vs)
    return outs[:n], outs[n:2 * n], outs[2 * n:]


def _pad_rows(a, rows):
    return jnp.pad(a, ((0, rows - a.shape[0]), (0, 0)))


def kernel(x, meta_tokens, ffn1_norm, ffn1_w_gate, ffn1_w_up, ffn1_w_down, mix_norm, w_in, b_in, conv_sc_w, conv_cf_w, conv_cf_b, ln_cf_g, ln_cf_b, w_out, ffn2_norm, ffn2_w_gate, ffn2_w_up, ffn2_w_down, final_norm, loss_target, m_meta_tokens, m_ffn1_norm, m_ffn1_w_gate, m_ffn1_w_up, m_ffn1_w_down, m_mix_norm, m_w_in, m_b_in, m_conv_sc_w, m_conv_cf_w, m_conv_cf_b, m_ln_cf_g, m_ln_cf_b, m_w_out, m_ffn2_norm, m_ffn2_w_gate, m_ffn2_w_up, m_ffn2_w_down, m_final_norm, v_meta_tokens, v_ffn1_norm, v_ffn1_w_gate, v_ffn1_w_up, v_ffn1_w_down, v_mix_norm, v_w_in, v_b_in, v_conv_sc_w, v_conv_cf_w, v_conv_cf_b, v_ln_cf_g, v_ln_cf_b, v_w_out, v_ffn2_norm, v_ffn2_w_gate, v_ffn2_w_up, v_ffn2_w_down, v_final_norm):
    xi, yi, ci = lax.axis_index("x"), lax.axis_index("y"), lax.axis_index("c")
    chip = 2 * xi + yi
    place = jnp.stack([chip, ci]).astype(jnp.int32)

    x2 = x[0]
    tgt = loss_target[0]
    S, D = x2.shape
    C1 = D // 2
    cs = conv_sc_w.shape[2]
    ksc, kcf = conv_sc_w.shape[1], conv_cf_w.shape[1]
    ms = meta_tokens.shape[1]

    rows_small = N_META + 8 + 32
    assert ksc <= 8 and kcf <= 32 and cs <= ms
    pack = jnp.concatenate([
        meta_tokens,
        jnp.pad(conv_sc_w[0], ((0, 8 - ksc), (0, ms - cs))),
        jnp.pad(conv_cf_w[0], ((0, 32 - kcf), (0, ms - cs)))], axis=0)
    everyone = _share_small(pack, False, "share_params", pack)[0::2]
    meta_full = jnp.transpose(everyone[:, :N_META, :], (1, 0, 2)).reshape(N_META, D)
    wsc_full = jnp.transpose(everyone[:, N_META:N_META + ksc, :cs], (1, 0, 2)).reshape(ksc, C1)
    wcf_full = jnp.transpose(everyone[:, N_META + 8:N_META + 8 + kcf, :cs], (1, 0, 2)).reshape(kcf, C1)

    big = {"ffn1_w_gate": ffn1_w_gate, "ffn1_w_up": ffn1_w_up, "ffn1_w_down": ffn1_w_down, "w_in": w_in, "w_out": w_out,
           "ffn2_w_gate": ffn2_w_gate, "ffn2_w_up": ffn2_w_up, "ffn2_w_down": ffn2_w_down}
    big_m = {"ffn1_w_gate": m_ffn1_w_gate, "ffn1_w_up": m_ffn1_w_up, "ffn1_w_down": m_ffn1_w_down, "w_in": m_w_in,
             "w_out": m_w_out, "ffn2_w_gate": m_ffn2_w_gate, "ffn2_w_up": m_ffn2_w_up, "ffn2_w_down": m_ffn2_w_down}
    big_v = {"ffn1_w_gate": v_ffn1_w_gate, "ffn1_w_up": v_ffn1_w_up, "ffn1_w_down": v_ffn1_w_down, "w_in": v_w_in,
             "w_out": v_w_out, "ffn2_w_gate": v_ffn2_w_gate, "ffn2_w_up": v_ffn2_w_up, "ffn2_w_down": v_ffn2_w_down}
    buf = {nm: _cast_own_block(place, w[0], "cast_" + nm) for nm, w in big.items()}
    whole_weight = lambda g: g.reshape(N_CHIPS, 2 * g.shape[2], g.shape[3])
    corner = lambda a: a.reshape(-1, a.shape[-1])[:8, :128]

    NEAR, FAR = (0, 1), (2,)
    groups = {"ffn1_near": (["ffn1_w_gate", "ffn1_w_up", "ffn1_w_down"], NEAR),
              "ffn1_far": (["ffn1_w_gate", "ffn1_w_up", "ffn1_w_down"], FAR),
              "mix": (["w_in", "w_out"], NEAR + FAR),
              "ffn2_up": (["ffn2_w_gate", "ffn2_w_up"], NEAR + FAR),
              "ffn2_down": (["ffn2_w_down"], NEAR + FAR)}
    started = {}

    def start(tag, after):
        nms, rels = groups[tag]
        copies = functools.partial(_gather_copies, rels=rels)
        send, recv, thru, token = _start_copies([buf[nm] for nm in nms], after, len(rels) * len(nms), copies,
                                                "gather_start_" + tag)
        for nm, b in zip(nms, thru):
            buf[nm] = b
        started[tag] = (send, recv, copies)
        return token

    def arrive(tag, after, then=None):
        nms, rels = groups[tag]
        send, recv, copies = started[tag]
        got = _wait_copies(send, recv, [buf[nm] for nm in nms], corner(after), copies, "gather_wait_" + tag)
        for nm, b in zip(nms, got):
            buf[nm] = b
        if then is not None:
            start(then, corner(got[0]))
        for nm, b in zip(nms, _forward_halves([buf[nm] for nm in nms], "gather_forward_" + tag, rels)):
            buf[nm] = b

    def passing(tag, after):
        nms, rels = groups[tag]
        send, recv, copies = started[tag]
        got = _wait_copies(send, recv, [buf[nm] for nm in nms], corner(after), copies, "gather_wait_" + tag)
        copies = functools.partial(_forward_copies, rels=rels)
        send, recv, thru, token = _start_copies(got, corner(got[0]), len(rels) * len(nms), copies,
                                                "gather_pass_" + tag)
        for nm, b in zip(nms, thru):
            buf[nm] = b
        started[tag] = (send, recv, copies)
        return token

    def passed(tag, after):
        nms, _ = groups[tag]
        send, recv, copies = started[tag]
        for nm, b in zip(nms, _wait_copies(send, recv, [buf[nm] for nm in nms], corner(after), copies,
                                           "gather_passed_" + tag)):
            buf[nm] = b

    tokens = lambda *arrays: jnp.concatenate([corner(a).astype(F32) for a in arrays], axis=0)
    token = start("ffn1_near", corner(everyone))

    ffn1 = lambda: [whole_weight(buf[nm]) for nm in ["ffn1_w_gate", "ffn1_w_up", "ffn1_w_down"]]
    own = chip[None].astype(jnp.int32)
    near = jnp.stack([chip ^ 2, chip ^ 1]).astype(jnp.int32)
    far = (chip ^ 3)[None].astype(jnp.int32)
    all_chips = jnp.arange(N_CHIPS, dtype=jnp.int32)

    hs0, n1 = _embed_rms(x2, meta_full, ffn1_norm)
    wg1, wu1, wd1 = ffn1()
    gua = _ffn_up(n1, wg1, wu1, own, None, token, "ffn1_up_own")
    hs1 = _ffn_down(gua[2], wd1, hs0, own, "ffn1_down_own")
    later = [buf[nm] for nm in ["w_in", "w_out", "ffn2_w_gate", "ffn2_w_up", "ffn2_w_down"]]
    arrive("ffn1_near", tokens(hs1, *later), "ffn1_far")
    wg1, wu1, wd1 = ffn1()
    gua = _ffn_up(n1, wg1, wu1, near, gua, token, "ffn1_up_near")
    tok = start("mix", corner(gua[2]))
    tok = passing("ffn1_far", tok)
    hs1 = _ffn_down(gua[2], ffn1()[2], hs1, near, "ffn1_down_near", tok)
    passed("ffn1_far", hs1)
    wg1, wu1, wd1 = ffn1()
    g1, u1, a1 = _ffn_up(n1, wg1, wu1, far, gua, token, "ffn1_up_far")
    tok = passing("mix", a1)
    hs1 = _ffn_down(a1, wd1, hs1, far, "ffn1_down_far", tok)
    F = N_CHIPS * wd1.shape[1]
    tok = start("ffn2_up", corner(hs1))
    passed("mix", tok)
    win, wout = whole_weight(buf["w_in"]), whole_weight(buf["w_out"])
    n2 = _rms(hs1, mix_norm, "rms_mix")
    u = _mix_in(n2, win, b_in)
    y = _mix_conv_fwd(u, wsc_full, wcf_full, conv_cf_b, ln_cf_g, ln_cf_b)
    tok = start("ffn2_down", corner(y))
    tok = passing("ffn2_up", tok)
    hs2 = _mix_out(y, wout.reshape(D, D), hs1, tok)
    passed("ffn2_up", hs2)
    wg2, wu2 = whole_weight(buf["ffn2_w_gate"]), whole_weight(buf["ffn2_w_up"])
    n3 = _rms(hs2, ffn2_norm, "rms_ffn2")
    tok = passing("ffn2_down", n3)
    g2, u2, a2 = _ffn_up(n3, wg2, wu2, all_chips, None, tok, "ffn2_up")
    passed("ffn2_down", a2)
    wd2 = whole_weight(buf["ffn2_w_down"])
    hs3 = _ffn_down_whole(a2, wd2.reshape(F, D), hs2, "ffn2_down")
    token_ffn2 = token

    def pair_start(group, after, tag):
        gs = [g for _, g in group]
        lands = [lax.empty((N_CHIPS,) + g.shape[2:], F32) for g in gs]
        send, recv, thru, token = _start_copies(gs + lands, after, N_CHIPS * len(gs), _pair_copies,
                                                "pair_start_" + tag)
        return (group, send, recv, thru, tag), token

    def scatter_start(state, after):
        group, send, recv, thru, tag = state
        thru = _wait_copies(send, recv, thru, corner(after), _pair_copies, "pair_wait_" + tag)
        gs, sib = thru[:len(group)], thru[len(group):]
        sums = [_pair_sum(place, g, rb, "pair_sum_" + nm) for (nm, _), g, rb in zip(group, gs, sib)]
        lands = [lax.empty((3,) + q.shape[1:], BF16) for q in sums]
        send, recv, thru, token = _start_copies(sums + lands, corner(sums[-1]), 3 * len(gs), _scatter_copies,
                                                "scatter_start_" + tag)
        return ([(nm, g) for (nm, _), g in zip(group, gs)], sib, send, recv, thru, tag), token

    def finish_sum(state, after):
        group, sib, send, recv, thru, tag = state
        lands = _wait_copies(send, recv, thru, corner(after), _scatter_copies, "scatter_wait_" + tag)[len(group):]
        mine = [_chip_sum(place, g, rb, rc, "chip_sum_" + nm) for (nm, g), rb, rc in zip(group, sib, lands)]
        send, recv, thru, token = _start_copies(mine, corner(mine[-1]), len(mine), _half_copies, "half_start_" + tag)
        return (group, send, recv, thru, tag), token

    def finish_adam(state, after):
        group, send, recv, thru, tag = state
        whole = _wait_copies(send, recv, thru, corner(after), _half_copies, "half_wait_" + tag)
        out = {}
        for (nm, _), g in zip(group, whole):
            w = big[nm]
            g_out, d, new_m, new_v = _adamw(w[0], g.reshape(w.shape[1:]), big_m[nm][0], big_v[nm][0], "adamw_" + nm)
            out[nm] = (g_out[None], d[None], new_m[None], new_v[None])
        return out

    dhs3, df2, loss_row, d_final = _final_loss(hs3, final_norm.reshape(1, D), tgt)

    dg2, du2 = _ffn_bwd_act(df2, wd2.reshape(F, D), g2, u2, token_ffn2, "ffn2_bwd_act")
    gw_d2 = _wgrad_down(a2, df2, "wgrad_ffn2_down")
    gw_g2 = _wgrad_cols(n3, [dg2], "wgrad_ffn2_gate")[0]
    gw_u2 = _wgrad_cols(n3, [du2], "wgrad_ffn2_up")[0]
    pair_ffn2, token = pair_start([("ffn2_w_gate", gw_g2), ("ffn2_w_up", gw_u2), ("ffn2_w_down", gw_d2)],
                                  corner(gw_u2), "ffn2")
    dn3 = _nt_panel([dg2, du2], [wg2, wu2], token, "ffn2_bwd_in")
    red_ffn2, token = scatter_start(pair_ffn2, dn3)
    dhs2, dm, d_ffn2 = _rms_bwd(dn3, hs2, ffn2_norm, dhs3, 1.0, "rms_bwd_ffn2")

    dy = _nt_panel([dm], [wout.reshape(1, D, D)], token, "mix_bwd_out")
    gw_out = _wgrad_out(y, dm)
    dz1, dcs, db, d_lg, d_lb, d_bcf = _mix_conv_bwd1(u, dy, wsc_full, wcf_full, conv_cf_b, ln_cf_g, ln_cf_b)
    du, d_bin, d_wsc, d_wcf = _mix_conv_bwd2(u, dz1, dcs, db, wsc_full, wcf_full)
    gw_in = _wgrad_cols(n2, [du], "wgrad_w_in")[0]
    pair_mix, token = pair_start([("w_in", gw_in), ("w_out", gw_out)], corner(gw_in), "mix")
    dn2 = _nt_panel([du], [win], token, "mix_bwd_in")
    red_mix, token = scatter_start(pair_mix, dn2)
    dhs1, df1, d_mix = _rms_bwd(dn2, hs1, mix_norm, dhs2, FFN_RES_SCALE, "rms_bwd_mix")

    dg1, du1 = _ffn_bwd_act(df1, wd1.reshape(F, D), g1, u1, token, "ffn1_bwd_act")
    gw_d1 = _wgrad_down(a1, df1, "wgrad_ffn1_down")
    gw_g1 = _wgrad_cols(n1, [dg1], "wgrad_ffn1_gate")[0]
    pair_ffn1a, token = pair_start([("ffn1_w_down", gw_d1), ("ffn1_w_gate", gw_g1)], corner(gw_g1), "ffn1a")
    gw_u1 = _wgrad_cols(n1, [du1], "wgrad_ffn1_up", token)[0]
    red_ffn1a, token = scatter_start(pair_ffn1a, gw_u1)
    pair_ffn1b, token = pair_start([("ffn1_w_up", gw_u1)], token, "ffn1b")
    dn1 = _nt_panel([dg1, du1], [wg1, wu1], token, "ffn1_bwd_in")
    red_ffn1b, token = scatter_start(pair_ffn1b, dn1)
    grad_x, d_meta, d_ffn1 = _rms_bwd_first(dn1, hs0, ffn1_norm, dhs1, token)

    half_ffn2, tok = finish_sum(red_ffn2, grad_x)
    half_mix, tok = finish_sum(red_mix, tok)
    big_out = finish_adam(half_ffn2, tok)
    half_ffn1a, tok = finish_sum(red_ffn1a, big_out["ffn2_w_down"][1])
    big_out.update(finish_adam(half_mix, tok))
    half_ffn1b, tok = finish_sum(red_ffn1b, big_out["w_out"][1])
    big_out.update(finish_adam(half_ffn1a, tok))
    big_out.update(finish_adam(half_ffn1b, big_out["ffn1_w_gate"][1]))

    W = C1
    rows = lambda a: a.reshape(-1, W)
    parts = [rows(d_ffn1), rows(d_mix), rows(d_ffn2), rows(d_final), rows(d_bin), d_bcf, d_lg, d_lb,
             d_wsc, d_wcf, rows(d_meta), jnp.broadcast_to(loss_row[:, :1], (1, W))]
    sizes = [p.shape[0] for p in parts]
    total_rows = sum(sizes)
    packed = _pad_rows(jnp.concatenate(parts, axis=0), -(-total_rows // 8) * 8)
    summed = _share_small(packed, True, "sum_small", big_out["ffn1_w_up"][1])
    offs = [0]
    for n in sizes:
        offs.append(offs[-1] + n)
    piece = lambda k: summed[offs[k]:offs[k + 1]]
    loss = piece(11)[0, 0]
    g_ffn1, g_mix, g_ffn2 = (piece(k).reshape(1, D) for k in range(3))
    g_final = piece(3).reshape(1, D)
    g_bin = piece(4).reshape(1, -1)
    g_bcf, g_lg, g_lb = piece(5), piece(6), piece(7)
    g_wsc = lax.dynamic_slice_in_dim(piece(8), chip * cs, cs, axis=1)
    g_wcf = lax.dynamic_slice_in_dim(piece(9), chip * cs, cs, axis=1)
    g_meta = lax.dynamic_slice_in_dim(piece(10).reshape(N_META, D), chip * ms, ms, axis=1)

    small_names = ["meta_tokens", "ffn1_norm", "mix_norm", "b_in", "conv_sc_w", "conv_cf_w", "conv_cf_b", "ln_cf_g",
                   "ln_cf_b", "ffn2_norm", "final_norm"]
    small_w = [meta_tokens, ffn1_norm, mix_norm, b_in, conv_sc_w[0], conv_cf_w[0], conv_cf_b, ln_cf_g, ln_cf_b,
               ffn2_norm, final_norm.reshape(1, D)]
    small_g = [g_meta, g_ffn1, g_mix, g_bin, g_wsc, g_wcf, g_bcf, g_lg, g_lb, g_ffn2, g_final]
    small_m = [m_meta_tokens, m_ffn1_norm, m_mix_norm, m_b_in, m_conv_sc_w[0], m_conv_cf_w[0], m_conv_cf_b, m_ln_cf_g,
               m_ln_cf_b, m_ffn2_norm, m_final_norm.reshape(1, D)]
    small_v = [v_meta_tokens, v_ffn1_norm, v_mix_norm, v_b_in, v_conv_sc_w[0], v_conv_cf_w[0], v_conv_cf_b, v_ln_cf_g,
               v_ln_cf_b, v_ffn2_norm, v_final_norm.reshape(1, D)]
    s_d, s_m, s_v = _adamw_small(small_w, small_g, small_m, small_v)
    shapes = {"conv_sc_w": conv_sc_w.shape, "conv_cf_w": conv_cf_w.shape, "final_norm": final_norm.shape}
    small_out = {}
    for nm, g, d, m, v in zip(small_names, small_g, s_d, s_m, s_v):
        shp = shapes.get(nm, g.shape)
        small_out[nm] = tuple(t.reshape(shp) for t in (g, d, m, v))

    order = ["meta_tokens", "ffn1_norm", "ffn1_w_gate", "ffn1_w_up", "ffn1_w_down", "mix_norm", "w_in", "b_in",
             "conv_sc_w", "conv_cf_w", "conv_cf_b", "ln_cf_g", "ln_cf_b", "w_out", "ffn2_norm", "ffn2_w_gate",
             "ffn2_w_up", "ffn2_w_down", "final_norm"]
    res = {**big_out, **small_out}
    outs = [loss, grad_x[None]]
    for q in range(4):
        outs.extend(res[nm][q] for nm in order)
    return tuple(outs)
```

```python
import functools

import jax
import jax.numpy as jnp
from jax import lax
from jax.experimental import pallas as pl
from jax.experimental.pallas import tpu as pltpu

F32 = jnp.float32
BF16 = jnp.bfloat16
MESH = pl.DeviceIdType.MESH

N_META = 16
TT = 128
PAD = TT - N_META
HALO = 32
EPS = 1e-6
FFN_RES_SCALE = 0.5
N_CHIPS = 4
N_DEV = 8

ADAM_LR = 0.001
ADAM_B1 = 0.9
ADAM_B2 = 0.999
ADAM_EPS = 1e-08
ADAM_WD = 0.01
ADAM_STEP = 10

V7X_VMEM_BYTES = 64 * 2 ** 20
NT_DIMS = (((1,), (1,)), ((), ()))
TN_DIMS = (((0,), (0,)), ((), ()))


def _params(semantics, block_bytes):
    limit = min(2 * block_bytes + 16 * 2 ** 20, V7X_VMEM_BYTES - 6 * 2 ** 20)
    return pltpu.CompilerParams(dimension_semantics=semantics, vmem_limit_bytes=int(limit))


def _pallas(body, out_shape, **kw):
    if "grid" not in kw and "grid_spec" not in kw:
        return pl.pallas_call(body, out_shape=out_shape, **kw)
    big = lambda shape, dtype: jnp.issubdtype(dtype, jnp.floating) and len(shape) >= 2
    pin_out = lambda s: pltpu.HBM(s.shape, s.dtype) if big(s.shape, s.dtype) else s
    single = not isinstance(out_shape, (list, tuple))
    shapes = pin_out(out_shape) if single else [pin_out(s) for s in out_shape]
    call = pl.pallas_call(body, out_shape=shapes, **kw)
    pin = lambda a: pltpu.with_memory_space_constraint(a, pltpu.HBM) if big(a.shape, a.dtype) else a
    return lambda *operands: call(*[pin(a) for a in operands])


def _nbytes(shape, dtype):
    n = 1
    for d in shape:
        if d is not None:
            n *= d
    return n * jnp.dtype(dtype).itemsize


def _row_tile(rows, target, mult=8):
    best = None
    for t in range(mult, min(rows, target) + 1, mult):
        if rows % t == 0:
            best = t
    assert best is not None, (rows, target, mult)
    return best


def _sigmoid(v):
    return jax.nn.sigmoid(v)


def _sigmoid_fast(v):
    return pl.reciprocal(1.0 + jnp.exp(-v), approx=True)


def _dsilu(v, s):
    return s * (1.0 + v * (1.0 - s))


def _embed_rms(x2, meta, gain):
    S, D = x2.shape
    T = S + TT

    def body(x_ref, meta_ref, g_ref, hs_ref, n_ref):
        i = pl.program_id(0)

        @pl.when(i == 0)
        def _():
            hs_ref[...] = jnp.zeros_like(hs_ref)
            hs_ref[PAD:, :] = meta_ref[...]

        @pl.when(i > 0)
        def _():
            hs_ref[...] = x_ref[...]

        h = hs_ref[...]
        r = lax.rsqrt(jnp.mean(h * h, axis=-1, keepdims=True) + EPS)
        n_ref[...] = ((h * r) * g_ref[...]).astype(BF16)

    blk = _nbytes((TT, D), F32) * 2 + _nbytes((TT, D), BF16)
    return _pallas(
        body, name="embed_rms", grid=(T // TT,),
        in_specs=[pl.BlockSpec((TT, D), lambda i: (jnp.maximum(i - 1, 0), 0)),
                  pl.BlockSpec((N_META, D), lambda i: (0, 0)),
                  pl.BlockSpec((1, D), lambda i: (0, 0))],
        out_specs=[pl.BlockSpec((TT, D), lambda i: (i, 0)), pl.BlockSpec((TT, D), lambda i: (i, 0))],
        out_shape=[jax.ShapeDtypeStruct((T, D), F32), jax.ShapeDtypeStruct((T, D), BF16)],
        compiler_params=_params(("parallel",), blk),
    )(x2, meta, gain)


def _rms(hs, gain, name):
    T, D = hs.shape
    te = _row_tile(T, 384)

    def body(h_ref, g_ref, n_ref):
        h = h_ref[...]
        r = lax.rsqrt(jnp.mean(h * h, axis=-1, keepdims=True) + EPS)
        n_ref[...] = ((h * r) * g_ref[...]).astype(BF16)

    blk = _nbytes((te, D), F32) + _nbytes((te, D), BF16)
    return _pallas(
        body, name=name, grid=(T // te,),
        in_specs=[pl.BlockSpec((te, D), lambda i: (i, 0)), pl.BlockSpec((1, D), lambda i: (0, 0))],
        out_specs=pl.BlockSpec((te, D), lambda i: (i, 0)),
        out_shape=jax.ShapeDtypeStruct((T, D), BF16),
        compiler_params=_params(("parallel",), blk),
    )(hs, gain)


def _rms_bwd_math(dn, h, g):
    r = lax.rsqrt(jnp.mean(h * h, axis=-1, keepdims=True) + EPS)
    xh = h * r
    dgain = jnp.sum(dn * xh, axis=0, keepdims=True)
    dxh = dn * g
    dh = r * (dxh - xh * jnp.mean(dxh * xh, axis=-1, keepdims=True))
    return dh, dgain


def _rms_bwd(dn, hs, gain, dres, scale, name):
    T, D = hs.shape
    te = _row_tile(T, 384)

    def body(dn_ref, h_ref, g_ref, dres_ref, dhs_ref, dhb_ref, dg_ref):
        dh, dgain = _rms_bwd_math(dn_ref[...], h_ref[...], g_ref[...])
        d = dres_ref[...] + dh
        dhs_ref[...] = d
        dhb_ref[...] = (scale * d).astype(BF16)

        @pl.when(pl.program_id(0) == 0)
        def _():
            dg_ref[...] = jnp.zeros_like(dg_ref)

        dg_ref[...] += dgain

    blk = _nbytes((te, D), F32) * 4 + _nbytes((te, D), BF16)
    row = lambda i: (i, 0)
    return _pallas(
        body, name=name, grid=(T // te,),
        in_specs=[pl.BlockSpec((te, D), row), pl.BlockSpec((te, D), row), pl.BlockSpec((1, D), lambda i: (0, 0)),
                  pl.BlockSpec((te, D), row)],
        out_specs=[pl.BlockSpec((te, D), row), pl.BlockSpec((te, D), row), pl.BlockSpec((1, D), lambda i: (0, 0))],
        out_shape=[jax.ShapeDtypeStruct((T, D), F32), jax.ShapeDtypeStruct((T, D), BF16),
                   jax.ShapeDtypeStruct((1, D), F32)],
        compiler_params=_params(("arbitrary",), blk),
    )(dn, hs, gain, dres)


def _rms_bwd_first(dn, hs, gain, dres, after):
    T, D = hs.shape
    S = T - TT

    def body(dn_ref, h_ref, g_ref, dres_ref, after_ref, gx_ref, gm_ref, dg_ref):
        i = pl.program_id(0)
        dh, dgain = _rms_bwd_math(dn_ref[...], h_ref[...], g_ref[...])
        d = dres_ref[...] + dh

        @pl.when(i == 0)
        def _():
            dg_ref[...] = jnp.zeros_like(dg_ref)
            gm_ref[...] = d[PAD:, :]

        @pl.when(i > 0)
        def _():
            gx_ref[...] = d

        dg_ref[...] += dgain

    blk = _nbytes((TT, D), F32) * 4
    row = lambda i: (i, 0)
    return _pallas(
        body, name="rms_bwd_ffn1", grid=(T // TT,),
        in_specs=[pl.BlockSpec((TT, D), row), pl.BlockSpec((TT, D), row), pl.BlockSpec((1, D), lambda i: (0, 0)),
                  pl.BlockSpec((TT, D), row), TOKEN],
        out_specs=[pl.BlockSpec((TT, D), lambda i: (jnp.maximum(i - 1, 0), 0)),
                   pl.BlockSpec((N_META, D), lambda i: (0, 0)), pl.BlockSpec((1, D), lambda i: (0, 0))],
        out_shape=[jax.ShapeDtypeStruct((S, D), F32), jax.ShapeDtypeStruct((N_META, D), F32),
                   jax.ShapeDtypeStruct((1, D), F32)],
        compiler_params=_params(("arbitrary",), blk),
    )(dn, hs, gain, dres, after)


def _final_loss(hs, gain, tgt):
    T, D = hs.shape

    def body(h_ref, g_ref, t_ref, dhs_ref, dhb_ref, loss_ref, dg_ref):
        i = pl.program_id(0)
        h = h_ref[...]
        g = g_ref[...]
        r = lax.rsqrt(jnp.mean(h * h, axis=-1, keepdims=True) + EPS)
        xh = h * r
        e = jnp.where(i > 0, xh * g - t_ref[...], 0.0)
        tile_loss = jnp.sum(jnp.sum(e * e, axis=1, keepdims=True), axis=0, keepdims=True) * (0.5 / D)
        dout = e * (1.0 / D)
        dgain = jnp.sum(dout * xh, axis=0, keepdims=True)
        dxh = dout * g
        d = r * (dxh - xh * jnp.mean(dxh * xh, axis=-1, keepdims=True))
        dhs_ref[...] = d
        dhb_ref[...] = (FFN_RES_SCALE * d).astype(BF16)

        @pl.when(i == 0)
        def _():
            loss_ref[...] = jnp.zeros_like(loss_ref)
            dg_ref[...] = jnp.zeros_like(dg_ref)

        loss_ref[...] += jnp.broadcast_to(tile_loss, loss_ref.shape)
        dg_ref[...] += dgain

    blk = _nbytes((TT, D), F32) * 3 + _nbytes((TT, D), BF16)
    row = lambda i: (i, 0)
    return _pallas(
        body, name="final_loss", grid=(T // TT,),
        in_specs=[pl.BlockSpec((TT, D), row), pl.BlockSpec((1, D), lambda i: (0, 0)),
                  pl.BlockSpec((TT, D), lambda i: (jnp.maximum(i - 1, 0), 0))],
        out_specs=[pl.BlockSpec((TT, D), row), pl.BlockSpec((TT, D), row),
                   pl.BlockSpec((1, 128), lambda i: (0, 0)), pl.BlockSpec((1, D), lambda i: (0, 0))],
        out_shape=[jax.ShapeDtypeStruct((T, D), F32), jax.ShapeDtypeStruct((T, D), BF16),
                   jax.ShapeDtypeStruct((1, 128), F32), jax.ShapeDtypeStruct((1, D), F32)],
        compiler_params=_params(("arbitrary",), blk),
    )(hs, gain, tgt)


MXU_COLS = 256


def _tm(T):
    return _row_tile(T, 704, 16)


def _col_chunks(n):
    return [(c, min(MXU_COLS, n - c)) for c in range(0, n, MXU_COLS)]


TOKEN = pl.BlockSpec((8, 128), lambda *_: (0, 0))


def _ffn_up(n, wg, wu, shards, prev, after, name):
    T, D = n.shape
    Fs = wg.shape[2]
    tm = _tm(T)
    nprev = 0 if prev is None else 3

    def body(shards_ref, n_ref, wg_ref, wu_ref, after_ref, *refs):
        g_ref, u_ref, a_ref = refs[nprev:]
        nn = n_ref[...]
        for c0, cw in _col_chunks(Fs):
            if 2 * cw == MXU_COLS:
                both = jnp.concatenate([wg_ref[:, c0:c0 + cw], wu_ref[:, c0:c0 + cw]], axis=1)
                gu = jnp.dot(nn, both, preferred_element_type=F32)
                g, u = gu[:, :cw], gu[:, cw:]
            else:
                g = jnp.dot(nn, wg_ref[:, c0:c0 + cw], preferred_element_type=F32)
                u = jnp.dot(nn, wu_ref[:, c0:c0 + cw], preferred_element_type=F32)
            g_ref[:, c0:c0 + cw] = g.astype(BF16)
            u_ref[:, c0:c0 + cw] = u.astype(BF16)
            a_ref[:, c0:c0 + cw] = (g * _sigmoid_fast(g) * u).astype(BF16)

    blk = _nbytes((tm, D), BF16) + 2 * _nbytes((D, Fs), BF16) + 3 * _nbytes((tm, Fs), BF16)
    out = pl.BlockSpec((tm, Fs), lambda j, i, p: (i, p[j]))
    shp = jax.ShapeDtypeStruct((T, N_CHIPS * Fs), BF16)
    return _pallas(
        body, name=name,
        grid_spec=pltpu.PrefetchScalarGridSpec(
            num_scalar_prefetch=1, grid=(shards.shape[0], T // tm),
            in_specs=[pl.BlockSpec((tm, D), lambda j, i, p: (i, 0)),
                      pl.BlockSpec((None, D, Fs), lambda j, i, p: (p[j], 0, 0)),
                      pl.BlockSpec((None, D, Fs), lambda j, i, p: (p[j], 0, 0)), TOKEN] + [ANY] * nprev,
            out_specs=[out, out, out]),
        out_shape=[shp, shp, shp], input_output_aliases={5 + q: q for q in range(nprev)},
        compiler_params=_params(("arbitrary", "arbitrary"), blk),
    )(shards, n, wg, wu, after, *(prev or ()))


def _ffn_down(a, wd, hs, shards, name, after=None):
    T, F = a.shape
    _, Fs, D = wd.shape
    tm = _tm(T)
    tn = D // 2
    extra = [] if after is None else [after]

    def body(shards_ref, a_ref, w_ref, h_ref, *refs):
        o_ref = refs[-1]
        part = FFN_RES_SCALE * jnp.dot(a_ref[...], w_ref[...], preferred_element_type=F32)

        @pl.when(pl.program_id(2) == 0)
        def _():
            o_ref[...] = h_ref[...] + part

        @pl.when(pl.program_id(2) > 0)
        def _():
            o_ref[...] += part

    blk = _nbytes((tm, Fs), BF16) + _nbytes((Fs, tn), BF16) + 3 * _nbytes((tm, tn), F32)
    return _pallas(
        body, name=name,
        grid_spec=pltpu.PrefetchScalarGridSpec(
            num_scalar_prefetch=1, grid=(D // tn, T // tm, shards.shape[0]),
            in_specs=[pl.BlockSpec((tm, Fs), lambda n, i, k, p: (i, p[k])),
                      pl.BlockSpec((None, Fs, tn), lambda n, i, k, p: (p[k], 0, n)),
                      pl.BlockSpec((tm, tn), lambda n, i, k, p: (i, n))] + [TOKEN] * len(extra),
            out_specs=pl.BlockSpec((tm, tn), lambda n, i, k, p: (i, n))),
        out_shape=jax.ShapeDtypeStruct((T, D), F32),
        compiler_params=_params(("parallel", "parallel", "arbitrary"), blk),
    )(shards, a, wd, hs, *extra)


def _ffn_down_whole(a, wd, hs, name):
    T, F = a.shape
    D = wd.shape[1]
    tm = _tm(T)
    tn = D // 4

    def body(a_ref, w_ref, h_ref, o_ref):
        o_ref[...] = h_ref[...] + FFN_RES_SCALE * jnp.dot(a_ref[...], w_ref[...], preferred_element_type=F32)

    blk = _nbytes((tm, F), BF16) + _nbytes((F, tn), BF16) + 3 * _nbytes((tm, tn), F32)
    return _pallas(
        body, name=name, grid=(D // tn, T // tm),
        in_specs=[pl.BlockSpec((tm, F), lambda n, i: (i, 0)), pl.BlockSpec((F, tn), lambda n, i: (0, n)),
                  pl.BlockSpec((tm, tn), lambda n, i: (i, n))],
        out_specs=pl.BlockSpec((tm, tn), lambda n, i: (i, n)),
        out_shape=jax.ShapeDtypeStruct((T, D), F32),
        compiler_params=_params(("parallel", "parallel"), blk),
    )(a, wd, hs)


def _mix_in(n, w, b):
    T, D = n.shape
    Ns = w.shape[2]
    tm = _tm(T)

    def body(n_ref, w_ref, b_ref, u_ref):
        u_ref[...] = jnp.dot(n_ref[...], w_ref[...], preferred_element_type=F32) + b_ref[...]

    blk = _nbytes((tm, D), BF16) + _nbytes((D, Ns), BF16) + 2 * _nbytes((tm, Ns), F32)
    return _pallas(
        body, name="mix_in", grid=(N_CHIPS, T // tm),
        in_specs=[pl.BlockSpec((tm, D), lambda j, i: (i, 0)), pl.BlockSpec((None, D, Ns), lambda j, i: (j, 0, 0)),
                  pl.BlockSpec((1, Ns), lambda j, i: (0, j))],
        out_specs=pl.BlockSpec((tm, Ns), lambda j, i: (i, j)),
        out_shape=jax.ShapeDtypeStruct((T, N_CHIPS * Ns), F32),
        compiler_params=_params(("parallel", "parallel"), blk),
    )(n, w, b)


def _mix_out(y, w, hs, after):
    T, D = y.shape
    tm = _tm(T)

    def body(y_ref, w_ref, h_ref, after_ref, o_ref):
        o_ref[...] = h_ref[...] + jnp.dot(y_ref[...], w_ref[...], preferred_element_type=F32)

    blk = _nbytes((tm, D), BF16) + _nbytes((D, D), BF16) + 3 * _nbytes((tm, D), F32)
    return _pallas(
        body, name="mix_out", grid=(T // tm,),
        in_specs=[pl.BlockSpec((tm, D), lambda i: (i, 0)), pl.BlockSpec((D, D), lambda i: (0, 0)),
                  pl.BlockSpec((tm, D), lambda i: (i, 0)), TOKEN],
        out_specs=pl.BlockSpec((tm, D), lambda i: (i, 0)),
        out_shape=jax.ShapeDtypeStruct((T, D), F32),
        compiler_params=_params(("parallel",), blk),
    )(y, w, hs, after)


def _ffn_bwd_act(dfb, wd, g, u, after, name):
    T, D = dfb.shape
    F = wd.shape[0]
    tm = _row_tile(T, 1408, 16)
    tn = 2 * MXU_COLS

    tr = _row_tile(tm, 352, 16)

    def body(d_ref, w_ref, g_ref, u_ref, after_ref, dg_ref, du_ref):
        for r0 in range(0, tm, tr):
            dv = d_ref[r0:r0 + tr, :]
            for c0, cw in _col_chunks(tn):
                da = lax.dot_general(dv, w_ref[c0:c0 + cw, :], NT_DIMS, preferred_element_type=F32)
                gv = g_ref[r0:r0 + tr, c0:c0 + cw].astype(F32)
                uv = u_ref[r0:r0 + tr, c0:c0 + cw].astype(F32)
                s = _sigmoid_fast(gv)
                du_ref[r0:r0 + tr, c0:c0 + cw] = (da * (gv * s)).astype(BF16)
                dg_ref[r0:r0 + tr, c0:c0 + cw] = (da * uv * _dsilu(gv, s)).astype(BF16)

    blk = _nbytes((tm, D), BF16) + _nbytes((tn, D), BF16) + 4 * _nbytes((tm, tn), BF16)
    io = pl.BlockSpec((tm, tn), lambda n, i: (i, n))
    shp = jax.ShapeDtypeStruct((T, F), BF16)
    return _pallas(
        body, name=name, grid=(F // tn, T // tm),
        in_specs=[pl.BlockSpec((tm, D), lambda n, i: (i, 0)), pl.BlockSpec((tn, D), lambda n, i: (n, 0)), io, io, TOKEN],
        out_specs=[io, io], out_shape=[shp, shp],
        compiler_params=_params(("parallel", "parallel"), blk),
    )(dfb, wd, g, u, after)


def _nt_panel(lhs_list, w_list, after, name):
    T = lhs_list[0].shape[0]
    nsh, Dout, Ks = w_list[0].shape
    npair = len(lhs_list)
    tm = _row_tile(T, 1408, 16)
    tn = Dout // 2

    def body(*refs):
        l_refs, w_refs, o_ref = refs[:npair], refs[npair:2 * npair], refs[2 * npair + 1]
        j = pl.program_id(2)
        k0 = Ks - Ks % MXU_COLS if npair == 2 and 2 * (Ks % MXU_COLS) == MXU_COLS else Ks
        acc = None
        for p in range(npair):
            part = lax.dot_general(l_refs[p][:, :k0], w_refs[p][:, :k0], NT_DIMS, preferred_element_type=F32)
            acc = part if acc is None else acc + part
        if k0 < Ks:
            lhs = jnp.concatenate([l_refs[p][:, k0:] for p in range(npair)], axis=1)
            rhs = jnp.concatenate([w_refs[p][:, k0:] for p in range(npair)], axis=1)
            acc = acc + lax.dot_general(lhs, rhs, NT_DIMS, preferred_element_type=F32)

        @pl.when(j == 0)
        def _():
            o_ref[...] = acc

        @pl.when(j > 0)
        def _():
            o_ref[...] += acc

    blk = npair * (_nbytes((tm, Ks), BF16) + _nbytes((tn, Ks), BF16)) + 2 * _nbytes((tm, tn), F32)
    return _pallas(
        body, name=name, grid=(Dout // tn, T // tm, nsh),
        in_specs=[pl.BlockSpec((tm, Ks), lambda n, i, j: (i, j))] * npair
                 + [pl.BlockSpec((None, tn, Ks), lambda n, i, j: (j, n, 0))] * npair + [TOKEN],
        out_specs=pl.BlockSpec((tm, tn), lambda n, i, j: (i, n)),
        out_shape=jax.ShapeDtypeStruct((T, Dout), F32),
        compiler_params=_params(("parallel", "parallel", "arbitrary"), blk),
    )(*lhs_list, *w_list, after)


def _tn_call(name, grid, lhs, lhs_spec, rhs_list, rhs_specs, out_shapes, out_specs, blk, after=None):
    nr = len(rhs_list)
    extra = [] if after is None else [after]

    def body(*refs):
        l_ref, r_refs, o_refs = refs[0], refs[1:1 + nr], refs[len(refs) - nr:]
        k = pl.program_id(len(grid) - 1)
        lv = l_ref[...]
        for q in range(nr):
            part = lax.dot_general(lv, r_refs[q][...], TN_DIMS, preferred_element_type=F32)
            part = part.reshape(o_refs[q].shape)

            @pl.when(k == 0)
            def _(o=o_refs[q], part=part):
                o[...] = part

            @pl.when(k > 0)
            def _(o=o_refs[q], part=part):
                o[...] += part

    return _pallas(
        body, name=name, grid=grid, in_specs=[lhs_spec] + rhs_specs + [TOKEN] * len(extra), out_specs=out_specs,
        out_shape=out_shapes, compiler_params=_params(("parallel",) * (len(grid) - 1) + ("arbitrary",), blk),
    )(lhs, *rhs_list, *extra)


def _tk(T):
    return T


def _wgrad_cols(n, rhs_list, name, after=None):
    T, D = n.shape
    Ns = rhs_list[0].shape[1] // N_CHIPS
    tk = _tk(T)
    nr = len(rhs_list)
    tm = D // 4
    blk = _nbytes((tk, tm), BF16) + nr * (_nbytes((tk, Ns), BF16) + 2 * _nbytes((tm, Ns), F32))
    return _tn_call(
        name, (N_CHIPS, D // tm, T // tk), n, pl.BlockSpec((tk, tm), lambda j, m, k: (k, m)),
        rhs_list, [pl.BlockSpec((tk, Ns), lambda j, m, k: (k, j))] * nr,
        [jax.ShapeDtypeStruct((N_CHIPS, 2, D // 2, Ns), F32)] * nr,
        [pl.BlockSpec((None, None, tm, Ns), lambda j, m, k: (j, m // 2, m % 2, 0))] * nr, blk, after)


def _wgrad_down(a, dfb, name):
    T, F = a.shape
    D = dfb.shape[1]
    Fs = F // N_CHIPS
    tk = _tk(T)
    tn = D // 4
    blk = _nbytes((tk, Fs), BF16) + _nbytes((tk, tn), BF16) + 2 * _nbytes((Fs, tn), F32)
    return _tn_call(
        name, (N_CHIPS, D // tn, T // tk), a, pl.BlockSpec((tk, Fs), lambda j, n, k: (k, j)),
        [dfb], [pl.BlockSpec((tk, tn), lambda j, n, k: (k, n))],
        [jax.ShapeDtypeStruct((N_CHIPS, 2, Fs // 2, D), F32)],
        [pl.BlockSpec((None, 2, Fs // 2, tn), lambda j, n, k: (j, 0, 0, n))], blk)[0]


def _wgrad_out(y, dmb):
    T, D = y.shape
    tk = _tk(T)
    tn = D // 2
    rows = D // (2 * N_CHIPS)
    blk = _nbytes((tk, D // 2), BF16) + _nbytes((tk, tn), BF16) + 2 * _nbytes((D // 2, tn), F32)
    return _tn_call(
        "wgrad_w_out", (2, D // tn, T // tk), y, pl.BlockSpec((tk, D // 2), lambda m, n, k: (k, m)),
        [dmb], [pl.BlockSpec((tk, tn), lambda m, n, k: (k, n))],
        [jax.ShapeDtypeStruct((N_CHIPS, 2, rows, D), F32)],
        [pl.BlockSpec((2, 2, rows, tn), lambda m, n, k: (m, 0, 0, n))], blk)[0]


def _row_masks(i, last):
    rows = i * TT + lax.broadcasted_iota(jnp.int32, (TT, 1), 0)
    prows = i * TT - HALO + lax.broadcasted_iota(jnp.int32, (HALO, 1), 0)
    return rows >= PAD, (prows >= PAD) & (i > 0), i < last


def _conv_inputs(u, up, mask_c, mask_p, zbuf, pbuf, C1):
    b, c, v, a, g = (u[:, k * C1:(k + 1) * C1] for k in range(5))
    cp, vp, ap, gp = (up[:, k * C1:(k + 1) * C1] for k in range(1, 5))
    sg = _sigmoid(g)
    pbuf[0:HALO, :] = jnp.where(mask_p, cp * vp, 0.0)
    pbuf[HALO:, :] = jnp.where(mask_c, c * v, 0.0)
    if zbuf is not None:
        zbuf[0:HALO, :] = jnp.where(mask_p, ap * _sigmoid(gp), 0.0)
        zbuf[HALO:, :] = jnp.where(mask_c, a * sg, 0.0)
    return b, c, v, a, sg


SUBLANES = 8
SHIFT_ROWS = TT + HALO - SUBLANES


def _shifted_scratch(C1):
    return pltpu.VMEM((SUBLANES - 1, SHIFT_ROWS, C1), F32)


def _fill_shifted(buf, sh):
    for r in range(1, SUBLANES):
        sh[r - 1] = buf[r:r + SHIFT_ROWS, :]


LANES = 128


def _window(buf, sh, lo, c0):
    if sh is None or lo % SUBLANES == 0:
        return buf[lo:lo + TT, c0:c0 + LANES]
    q, r = divmod(lo, SUBLANES)
    return sh[r - 1, q * SUBLANES:q * SUBLANES + TT, c0:c0 + LANES]


def _tap_sum(w_ref, buf, sh, starts):
    chunks = []
    for c0 in range(0, buf.shape[1], LANES):
        acc = None
        for k, lo in enumerate(starts):
            term = w_ref[k:k + 1, c0:c0 + LANES] * _window(buf, sh, lo, c0)
            acc = term if acc is None else acc + term
        chunks.append(acc)
    return jnp.concatenate(chunks, axis=1)


def _causal_conv(w_ref, buf, sh=None):
    K = w_ref.shape[0]
    return _tap_sum(w_ref, buf, sh, [HALO - (K - 1) + k for k in range(K)])


def _anticausal_conv(w_ref, buf, sh=None):
    K = w_ref.shape[0]
    return _tap_sum(w_ref, buf, sh, [K - 1 - k for k in range(K)])


def _conv_weight_sums(dw_ref, dy, buf, sh=None):
    K = dw_ref.shape[0]
    for c0 in range(0, buf.shape[1], LANES):
        dyc = dy[:, c0:c0 + LANES]
        for k in range(K):
            prod = dyc * _window(buf, sh, HALO - (K - 1) + k, c0)
            dw_ref[k:k + 1, c0:c0 + LANES] += jnp.sum(prod, axis=0, keepdims=True)


def _layernorm_stats(z1):
    mu = jnp.mean(z1, axis=-1, keepdims=True)
    zc = z1 - mu
    rs = lax.rsqrt(jnp.mean(zc * zc, axis=-1, keepdims=True) + EPS)
    return zc * rs, rs


def _mixer_specs(T, DIN, C1, ksc, kcf):
    cur = pl.BlockSpec((TT, DIN), lambda i: (i, 0))
    prev = pl.BlockSpec((HALO, DIN), lambda i: (jnp.maximum(i * (TT // HALO) - 1, 0), 0))
    full = lambda r: pl.BlockSpec((r, C1), lambda i: (0, 0))
    return cur, prev, [full(ksc), full(kcf), full(1), full(1), full(1)]


def _mix_conv_fwd(u, wsc, wcf, bcf, lg, lb):
    T, DIN = u.shape
    C1 = DIN // 5
    last = T // TT - 1

    def body(u_ref, up_ref, wsc_ref, wcf_ref, bcf_ref, lg_ref, lb_ref, y_ref, z1_ref, zbuf, pbuf, zsh):
        i = pl.program_id(0)
        mask_c, mask_p, _ = _row_masks(i, last)
        b, _, _, _, _ = _conv_inputs(u_ref[...], up_ref[...], mask_c, mask_p, zbuf, pbuf, C1)
        _fill_shifted(zbuf, zsh)
        cs = _causal_conv(wsc_ref, pbuf)
        z1 = _causal_conv(wcf_ref, zbuf, zsh) + bcf_ref[...]
        z1_ref[...] = z1
        zh, _ = _layernorm_stats(z1)
        ln = zh * lg_ref[...] + lb_ref[...]
        y_ref[:, 0:C1] = jnp.where(mask_c, b * cs, 0.0).astype(BF16)
        y_ref[:, C1:] = jnp.where(mask_c, jax.nn.silu(ln), 0.0).astype(BF16)

    cur, prev, small = _mixer_specs(T, DIN, C1, wsc.shape[0], wcf.shape[0])
    blk = _nbytes((TT + HALO, DIN), F32) + _nbytes((TT, 2 * C1), BF16) + 12 * _nbytes((TT + HALO, C1), F32)
    return _pallas(
        body, name="mix_conv_fwd", grid=(T // TT,),
        in_specs=[cur, prev] + small,
        out_specs=[pl.BlockSpec((TT, 2 * C1), lambda i: (i, 0)), pl.BlockSpec((TT, C1), lambda i: (i, 0))],
        out_shape=[jax.ShapeDtypeStruct((T, 2 * C1), BF16), jax.ShapeDtypeStruct((T, C1), F32)],
        scratch_shapes=[pltpu.VMEM((TT + HALO, C1), F32), pltpu.VMEM((TT + HALO, C1), F32), _shifted_scratch(C1)],
        compiler_params=_params(("arbitrary",), blk),
    )(u, u, wsc, wcf, bcf, lg, lb)


def _mix_conv_bwd1(u, z1, dy, wsc, lg, lb):
    T, DIN = u.shape
    C1 = DIN // 5
    last = T // TT - 1

    def body(u_ref, up_ref, z1_ref, dy_ref, wsc_ref, lg_ref, lb_ref,
             dz1_ref, dcs_ref, db_ref, dlg_ref, dlb_ref, dbcf_ref, pbuf):
        i = pl.program_id(0)
        mask_c, mask_p, _ = _row_masks(i, last)
        b, _, _, _, _ = _conv_inputs(u_ref[...], up_ref[...], mask_c, mask_p, None, pbuf, C1)
        cs = _causal_conv(wsc_ref, pbuf)
        zh, rs = _layernorm_stats(z1_ref[...])
        ln = zh * lg_ref[...] + lb_ref[...]
        dy = dy_ref[...]
        dysc = jnp.where(mask_c, dy[:, 0:C1], 0.0)
        dycf = jnp.where(mask_c, dy[:, C1:], 0.0)
        db_ref[...] = (dysc * cs).astype(BF16)
        dcs_ref[...] = dysc * b
        dl = dycf * _dsilu(ln, _sigmoid(ln))
        dzh = dl * lg_ref[...]
        dz1 = rs * (dzh - jnp.mean(dzh, axis=-1, keepdims=True) - zh * jnp.mean(dzh * zh, axis=-1, keepdims=True))
        dz1_ref[...] = dz1

        @pl.when(i == 0)
        def _():
            dlg_ref[...] = jnp.zeros_like(dlg_ref)
            dlb_ref[...] = jnp.zeros_like(dlb_ref)
            dbcf_ref[...] = jnp.zeros_like(dbcf_ref)

        dlg_ref[...] += jnp.sum(dl * zh, axis=0, keepdims=True)
        dlb_ref[...] += jnp.sum(dl, axis=0, keepdims=True)
        dbcf_ref[...] += jnp.sum(dz1, axis=0, keepdims=True)

    cur, prev, small = _mixer_specs(T, DIN, C1, wsc.shape[0], 1)
    tile = lambda: pl.BlockSpec((TT, C1), lambda i: (i, 0))
    vec = lambda: pl.BlockSpec((1, C1), lambda i: (0, 0))
    blk = _nbytes((TT + HALO, DIN), F32) + 5 * _nbytes((TT, C1), F32) + 12 * _nbytes((TT + HALO, C1), F32)
    return _pallas(
        body, name="mix_conv_bwd1", grid=(T // TT,),
        in_specs=[cur, prev, tile(), pl.BlockSpec((TT, 2 * C1), lambda i: (i, 0)), small[0], small[3], small[4]],
        out_specs=[tile(), tile(), tile(), vec(), vec(), vec()],
        out_shape=[jax.ShapeDtypeStruct((T, C1), F32), jax.ShapeDtypeStruct((T, C1), F32),
                   jax.ShapeDtypeStruct((T, C1), BF16)] + [jax.ShapeDtypeStruct((1, C1), F32)] * 3,
        scratch_shapes=[pltpu.VMEM((TT + HALO, C1), F32)],
        compiler_params=_params(("arbitrary",), blk),
    )(u, u, z1, dy, wsc, lg, lb)


def _mix_conv_bwd2(u, dz1, dcs, db, wsc, wcf):
    T, DIN = u.shape
    C1 = DIN // 5
    last = T // TT - 1
    ksc, kcf = wsc.shape[0], wcf.shape[0]

    def body(u_ref, up_ref, dz_ref, dzn_ref, dc_ref, dcn_ref, db_ref, wsc_ref, wcf_ref,
             du_ref, dbin_ref, dwsc_ref, dwcf_ref, zbuf, pbuf, dzbuf, dcbuf, zsh, dzsh):
        i = pl.program_id(0)
        mask_c, mask_p, has_next = _row_masks(i, last)
        _, c, v, a, sg = _conv_inputs(u_ref[...], up_ref[...], mask_c, mask_p, zbuf, pbuf, C1)
        dz1 = dz_ref[...]
        dcs = dc_ref[...]
        dzbuf[0:TT, :] = dz1
        dzbuf[TT:, :] = jnp.where(has_next, dzn_ref[...], 0.0)
        dcbuf[0:TT, :] = dcs
        dcbuf[TT:, :] = jnp.where(has_next, dcn_ref[...], 0.0)

        @pl.when(i == 0)
        def _():
            dbin_ref[...] = jnp.zeros_like(dbin_ref)
            dwsc_ref[...] = jnp.zeros_like(dwsc_ref)
            dwcf_ref[...] = jnp.zeros_like(dwcf_ref)

        _fill_shifted(zbuf, zsh)
        _fill_shifted(dzbuf, dzsh)
        _conv_weight_sums(dwcf_ref, dz1, zbuf, zsh)
        _conv_weight_sums(dwsc_ref, dcs, pbuf)
        dz0 = jnp.where(mask_c, _anticausal_conv(wcf_ref, dzbuf, dzsh), 0.0)
        dp = jnp.where(mask_c, _anticausal_conv(wsc_ref, dcbuf), 0.0)
        parts = (db_ref[...].astype(F32), dp * v, dp * c, dz0 * sg, dz0 * a * sg * (1.0 - sg))
        for k, part in enumerate(parts):
            du_ref[:, k * C1:(k + 1) * C1] = part.astype(BF16)
            dbin_ref[:, k * C1:(k + 1) * C1] += jnp.sum(part, axis=0, keepdims=True)

    cur, prev, small = _mixer_specs(T, DIN, C1, ksc, kcf)
    tile = lambda: pl.BlockSpec((TT, C1), lambda i: (i, 0))
    nxt = lambda: pl.BlockSpec((HALO, C1), lambda i: (jnp.minimum((i + 1) * (TT // HALO), T // HALO - 1), 0))
    blk = (_nbytes((TT + HALO, DIN), F32) + _nbytes((TT, DIN), BF16) + 5 * _nbytes((TT, C1), F32)
           + 16 * _nbytes((TT + HALO, C1), F32))
    buf = lambda: pltpu.VMEM((TT + HALO, C1), F32)
    return _pallas(
        body, name="mix_conv_bwd2", grid=(T // TT,),
        in_specs=[cur, prev, tile(), nxt(), tile(), nxt(), tile(), small[0], small[1]],
        out_specs=[pl.BlockSpec((TT, DIN), lambda i: (i, 0)), pl.BlockSpec((1, DIN), lambda i: (0, 0)),
                   pl.BlockSpec((ksc, C1), lambda i: (0, 0)), pl.BlockSpec((kcf, C1), lambda i: (0, 0))],
        out_shape=[jax.ShapeDtypeStruct((T, DIN), BF16), jax.ShapeDtypeStruct((1, DIN), F32),
                   jax.ShapeDtypeStruct((ksc, C1), F32), jax.ShapeDtypeStruct((kcf, C1), F32)],
        scratch_shapes=[buf(), buf(), buf(), buf(), _shifted_scratch(C1), _shifted_scratch(C1)],
        compiler_params=_params(("arbitrary",), blk),
    )(u, u, dz1, dz1, dcs, dcs, db, wsc, wcf)


def _place():
    x, y, c = lax.axis_index("x"), lax.axis_index("y"), lax.axis_index("c")
    chips = [(1 - x, y), (x, 1 - y), (1 - x, 1 - y)]
    return x, y, c, chips


ANY = pl.BlockSpec(memory_space=pl.ANY)


def _cast_own_block(place, w, name):
    R, C = w.shape
    tr = _row_tile(R // 2, 256, 16)
    nblk = R // 2 // tr

    def body(place_ref, w_ref, o_ref):
        o_ref[...] = w_ref[...].astype(BF16)

    return _pallas(
        body, name=name,
        grid_spec=pltpu.PrefetchScalarGridSpec(
            num_scalar_prefetch=1, grid=(2, nblk),
            in_specs=[pl.BlockSpec((tr, C), lambda h, i, p: (h * nblk + i, 0))],
            out_specs=pl.BlockSpec((None, None, tr, C), lambda h, i, p: (p[0], h, i, 0))),
        out_shape=jax.ShapeDtypeStruct((N_CHIPS, 2, R // 2, C), BF16),
        compiler_params=_params(("parallel", "parallel"), _nbytes((tr, C), F32) + _nbytes((tr, C), BF16)),
    )(place, w)


HBM = pl.BlockSpec(memory_space=pltpu.HBM)
SEM = pl.BlockSpec(memory_space=pltpu.SEMAPHORE)
EFFECT = pltpu.SideEffectType.DATAFLOW_SIDE_EFFECTING


def _gather_copies(refs, send, recv, rels=(0, 1, 2)):
    x, y, c, chips = _place()
    s = 2 * x + y
    n = len(rels)
    return [pltpu.make_async_remote_copy(src_ref=ref.at[s, c], dst_ref=ref.at[s, c], send_sem=send.at[n * w + k],
                                         recv_sem=recv.at[n * w + k], device_id=(*chips[r], c), device_id_type=MESH)
            for w, ref in enumerate(refs) for k, r in enumerate(rels)]


def _scatter_copies(refs, send, recv):
    x, y, c, chips = _place()
    nw = len(refs) // 2
    return [pltpu.make_async_remote_copy(src_ref=refs[w].at[2 * tx + ty], dst_ref=refs[nw + w].at[r],
                                         send_sem=send.at[3 * w + r], recv_sem=recv.at[3 * w + r],
                                         device_id=(tx, ty, c), device_id_type=MESH)
            for w in range(nw) for r, (tx, ty) in enumerate(chips)]


def _pair_copies(refs, send, recv):
    x, y, c, _ = _place()
    nw = len(refs) // 2
    return [pltpu.make_async_remote_copy(src_ref=refs[w].at[j, 1 - c], dst_ref=refs[nw + w].at[j],
                                         send_sem=send.at[N_CHIPS * w + j], recv_sem=recv.at[N_CHIPS * w + j],
                                         device_id=(x, y, 1 - c), device_id_type=MESH)
            for w in range(nw) for j in range(N_CHIPS)]


def _forward_copies(refs, send, recv, rels=(0, 1, 2)):
    x, y, c, chips = _place()
    n = len(rels)
    copies = []
    for w, ref in enumerate(refs):
        for k, r in enumerate(rels):
            tx, ty = chips[r]
            blk = ref.at[2 * tx + ty, c]
            copies.append(pltpu.make_async_remote_copy(src_ref=blk, dst_ref=blk, send_sem=send.at[n * w + k],
                                                       recv_sem=recv.at[n * w + k], device_id=(x, y, 1 - c),
                                                       device_id_type=MESH))
    return copies


def _half_copies(refs, send, recv):
    x, y, c, _ = _place()
    return [pltpu.make_async_remote_copy(src_ref=ref.at[c], dst_ref=ref.at[c], send_sem=send.at[w], recv_sem=recv.at[w],
                                         device_id=(x, y, 1 - c), device_id_type=MESH)
            for w, ref in enumerate(refs)]


def _start_copies(bufs, after, ncopies, make_copies, name):
    n = len(bufs)

    def body(*refs):
        in_refs, send, recv, token = refs[:n], refs[n + 1], refs[n + 2], refs[2 * n + 3]
        for cp in make_copies(in_refs, send, recv):
            cp.start()
        token[...] = jnp.zeros_like(token)

    outs = _pallas(
        body, name=name, in_specs=[HBM] * n + [ANY],
        out_specs=[SEM, SEM] + [HBM] * n + [pl.BlockSpec(memory_space=pltpu.VMEM)],
        out_shape=[pltpu.SemaphoreType.DMA((ncopies,)), pltpu.SemaphoreType.DMA((ncopies,))]
                  + [pltpu.HBM(b.shape, b.dtype) for b in bufs] + [jax.ShapeDtypeStruct((8, 128), F32)],
        input_output_aliases={k: 2 + k for k in range(n)},
        compiler_params=pltpu.CompilerParams(has_side_effects=EFFECT),
    )(*[pltpu.with_memory_space_constraint(b, pltpu.HBM) for b in bufs], after)
    return outs[0], outs[1], list(outs[2:2 + n]), outs[2 + n]


def _wait_copies(send, recv, bufs, after, make_copies, name):
    n = len(bufs)

    def body(*refs):
        in_refs, send_ref, recv_ref = refs[:n], refs[n], refs[n + 1]
        for cp in make_copies(in_refs, send_ref, recv_ref):
            cp.wait_send()
            cp.wait_recv()

    outs = _pallas(
        body, name=name, in_specs=[HBM] * n + [SEM, SEM, ANY], out_specs=[HBM] * n,
        out_shape=[pltpu.HBM(b.shape, b.dtype) for b in bufs],
        input_output_aliases={k: k for k in range(n)},
        compiler_params=pltpu.CompilerParams(has_side_effects=EFFECT),
    )(*bufs, send, recv, after)
    return list(outs)


def _forward_halves(bufs, name, rels=(0, 1, 2)):
    nw = len(bufs)
    n = len(rels)

    def body(*refs):
        o_refs = refs[nw:2 * nw]
        send, recv = refs[2 * nw:]
        x, y, c, chips = _place()
        sib = (x, y, 1 - c)
        copies = []
        for w in range(nw):
            for k, r in enumerate(rels):
                tx, ty = chips[r]
                ref = o_refs[w].at[2 * tx + ty, c]
                cp = pltpu.make_async_remote_copy(src_ref=ref, dst_ref=ref, send_sem=send.at[n * w + k],
                                                  recv_sem=recv.at[n * w + k], device_id=sib, device_id_type=MESH)
                cp.start()
                copies.append(cp)
        for w in range(nw):
            for k, r in enumerate(rels):
                tx, ty = chips[r]
                ref = o_refs[w].at[2 * tx + ty, 1 - c]
                pltpu.make_async_remote_copy(src_ref=ref, dst_ref=ref, send_sem=send.at[n * w + k],
                                             recv_sem=recv.at[n * w + k], device_id=sib, device_id_type=MESH).wait_recv()
        for cp in copies:
            cp.wait_send()

    return _pallas(
        body, name=name, in_specs=[ANY] * nw, out_specs=[ANY] * nw,
        out_shape=[jax.ShapeDtypeStruct(b.shape, b.dtype) for b in bufs],
        input_output_aliases={w: w for w in range(nw)},
        scratch_shapes=[pltpu.SemaphoreType.DMA((n * nw,)), pltpu.SemaphoreType.DMA((n * nw,))],
    )(*bufs)


def _share_small(v, reduce, name, after):
    R, C = v.shape

    def body(v_ref, after_ref, o_ref, *scratch):
        if reduce:
            all_ref, send, recv, lsem = scratch
        else:
            all_ref = o_ref
            send, recv, lsem = scratch
        x, y, c, _ = _place()
        me = 4 * x + 2 * y + c
        loc = pltpu.make_async_copy(v_ref, all_ref.at[me], lsem)
        loc.start()
        copies = []
        for k in range(1, N_DEV):
            kx, ky, kc = (k >> 2) & 1, (k >> 1) & 1, k & 1
            peer = (x ^ kx, y ^ ky, c ^ kc)
            cp = pltpu.make_async_remote_copy(src_ref=v_ref, dst_ref=all_ref.at[me], send_sem=send.at[k - 1],
                                              recv_sem=recv.at[k - 1], device_id=peer, device_id_type=MESH)
            cp.start()
            copies.append(cp)
        for k in range(1, N_DEV):
            kx, ky, kc = (k >> 2) & 1, (k >> 1) & 1, k & 1
            src = 4 * (x ^ kx) + 2 * (y ^ ky) + (c ^ kc)
            pltpu.make_async_remote_copy(src_ref=v_ref, dst_ref=all_ref.at[src], send_sem=send.at[k - 1],
                                         recv_sem=recv.at[k - 1], device_id=(x, y, c), device_id_type=MESH).wait_recv()
        for cp in copies:
            cp.wait_send()
        loc.wait()
        if reduce:
            total = all_ref[0]
            for d in range(1, N_DEV):
                total = total + all_ref[d]
            o_ref[...] = total

    vm = pl.BlockSpec(memory_space=pltpu.VMEM)
    sems = [pltpu.SemaphoreType.DMA((N_DEV - 1,)), pltpu.SemaphoreType.DMA((N_DEV - 1,)), pltpu.SemaphoreType.DMA]
    if reduce:
        out_shape = jax.ShapeDtypeStruct((R, C), F32)
        scratch = [pltpu.VMEM((N_DEV, R, C), F32)] + sems
    else:
        out_shape = jax.ShapeDtypeStruct((N_DEV, R, C), F32)
        scratch = sems
    return _pallas(
        body, name=name, in_specs=[vm, ANY], out_specs=vm, out_shape=out_shape, scratch_shapes=scratch,
        compiler_params=pltpu.CompilerParams(vmem_limit_bytes=int(min(4 * N_DEV * R * C * 4 + 2 ** 24, 2 ** 25 + 2 ** 24))),
    )(v, after)


def _pair_sum(place, g, rb, name):
    _, _, Rh, C = g.shape
    tr = _row_tile(Rh, 256, 16)

    def body(place_ref, g_ref, r_ref, q_ref):
        q_ref[...] = (g_ref[...] + r_ref[...]).astype(BF16)

    blk = 2 * _nbytes((tr, C), F32) + _nbytes((tr, C), BF16)
    return _pallas(
        body, name=name,
        grid_spec=pltpu.PrefetchScalarGridSpec(
            num_scalar_prefetch=1, grid=(N_CHIPS - 1, Rh // tr),
            in_specs=[pl.BlockSpec((None, None, tr, C), lambda j, i, p: (p[0] ^ (j + 1), p[1], i, 0)),
                      pl.BlockSpec((None, tr, C), lambda j, i, p: (p[0] ^ (j + 1), i, 0))],
            out_specs=pl.BlockSpec((None, tr, C), lambda j, i, p: (p[0] ^ (j + 1), i, 0))),
        out_shape=jax.ShapeDtypeStruct((N_CHIPS, Rh, C), BF16),
        compiler_params=_params(("parallel", "parallel"), blk),
    )(place, g, rb)


def _chip_sum(place, g, rb, rc, name):
    _, _, Rh, C = g.shape
    tr = _row_tile(Rh, 256, 16)

    def body(place_ref, g_ref, r_ref, rc_ref, o_ref):
        total = g_ref[...] + r_ref[...]
        for r in range(3):
            total = total + rc_ref[r].astype(F32)
        o_ref[...] = total

    blk = 3 * _nbytes((tr, C), F32) + 3 * _nbytes((tr, C), BF16)
    return _pallas(
        body, name=name,
        grid_spec=pltpu.PrefetchScalarGridSpec(
            num_scalar_prefetch=1, grid=(Rh // tr,),
            in_specs=[pl.BlockSpec((None, None, tr, C), lambda i, p: (p[0], p[1], i, 0)),
                      pl.BlockSpec((None, tr, C), lambda i, p: (p[0], i, 0)),
                      pl.BlockSpec((3, tr, C), lambda i, p: (0, i, 0))],
            out_specs=pl.BlockSpec((None, tr, C), lambda i, p: (p[1], i, 0))),
        out_shape=jax.ShapeDtypeStruct((2, Rh, C), F32),
        compiler_params=_params(("parallel",), blk),
    )(place, g, rb, rc)


def _adamw_math(w, g, m, v):
    m = ADAM_B1 * m + (1.0 - ADAM_B1) * g
    v = ADAM_B2 * v + (1.0 - ADAM_B2) * jnp.square(g)
    m_hat = m / (1.0 - ADAM_B1 ** ADAM_STEP)
    v_hat = v / (1.0 - ADAM_B2 ** ADAM_STEP)
    delta = -ADAM_LR * (m_hat / (jnp.sqrt(v_hat) + ADAM_EPS) + ADAM_WD * w)
    return delta, m, v


def _adamw(w, g, m, v, name):
    R, C = w.shape
    tr = _row_tile(R, 256)

    def body(w_ref, g_ref, m_ref, v_ref, go_ref, d_ref, nm_ref, nv_ref):
        gv = g_ref[...]
        d, nm, nv = _adamw_math(w_ref[...], gv, m_ref[...], v_ref[...])
        go_ref[...] = gv
        d_ref[...] = d
        nm_ref[...] = nm
        nv_ref[...] = nv

    spec = pl.BlockSpec((tr, C), lambda i: (i, 0))
    shp = jax.ShapeDtypeStruct((R, C), F32)
    return _pallas(
        body, name=name, grid=(R // tr,), in_specs=[spec] * 4, out_specs=[spec] * 4, out_shape=[shp] * 4,
        compiler_params=_params(("parallel",), 8 * _nbytes((tr, C), F32)),
    )(w, g, m, v)


def _adamw_small(ws, gs, ms, vs):
    n = len(ws)

    def body(*refs):
        for k in range(n):
            w_ref, g_ref, m_ref, v_ref = (refs[q * n + k] for q in range(4))
            d, nm, nv = _adamw_math(w_ref[...], g_ref[...], m_ref[...], v_ref[...])
            refs[4 * n + k][...] = d
            refs[5 * n + k][...] = nm
            refs[6 * n + k][...] = nv

    vm = pl.BlockSpec(memory_space=pltpu.VMEM)
    shapes = [jax.ShapeDtypeStruct(w.shape, F32) for w in ws]
    outs = _pallas(
        body, name="adamw_small", in_specs=[vm] * (4 * n), out_specs=[vm] * (3 * n), out_shape=shapes * 3,
    )(*ws, *gs, *ms, *vs)
    return outs[:n], outs[n:2 * n], outs[2 * n:]


def _pad_rows(a, rows):
    return jnp.pad(a, ((0, rows - a.shape[0]), (0, 0)))


def kernel(x, meta_tokens, ffn1_norm, ffn1_w_gate, ffn1_w_up, ffn1_w_down, mix_norm, w_in, b_in, conv_sc_w, conv_cf_w, conv_cf_b, ln_cf_g, ln_cf_b, w_out, ffn2_norm, ffn2_w_gate, ffn2_w_up, ffn2_w_down, final_norm, loss_target, m_meta_tokens, m_ffn1_norm, m_ffn1_w_gate, m_ffn1_w_up, m_ffn1_w_down, m_mix_norm, m_w_in, m_b_in, m_conv_sc_w, m_conv_cf_w, m_conv_cf_b, m_ln_cf_g, m_ln_cf_b, m_w_out, m_ffn2_norm, m_ffn2_w_gate, m_ffn2_w_up, m_ffn2_w_down, m_final_norm, v_meta_tokens, v_ffn1_norm, v_ffn1_w_gate, v_ffn1_w_up, v_ffn1_w_down, v_mix_norm, v_w_in, v_b_in, v_conv_sc_w, v_conv_cf_w, v_conv_cf_b, v_ln_cf_g, v_ln_cf_b, v_w_out, v_ffn2_norm, v_ffn2_w_gate, v_ffn2_w_up, v_ffn2_w_down, v_final_norm):
    xi, yi, ci = lax.axis_index("x"), lax.axis_index("y"), lax.axis_index("c")
    chip = 2 * xi + yi
    place = jnp.stack([chip, ci]).astype(jnp.int32)

    x2 = x[0]
    tgt = loss_target[0]
    S, D = x2.shape
    C1 = D // 2
    cs = conv_sc_w.shape[2]
    ksc, kcf = conv_sc_w.shape[1], conv_cf_w.shape[1]
    ms = meta_tokens.shape[1]

    rows_small = N_META + 8 + 32
    assert ksc <= 8 and kcf <= 32 and cs <= ms
    pack = jnp.concatenate([
        meta_tokens,
        jnp.pad(conv_sc_w[0], ((0, 8 - ksc), (0, ms - cs))),
        jnp.pad(conv_cf_w[0], ((0, 32 - kcf), (0, ms - cs)))], axis=0)
    everyone = _share_small(pack, False, "share_params", pack)[0::2]
    meta_full = jnp.transpose(everyone[:, :N_META, :], (1, 0, 2)).reshape(N_META, D)
    wsc_full = jnp.transpose(everyone[:, N_META:N_META + ksc, :cs], (1, 0, 2)).reshape(ksc, C1)
    wcf_full = jnp.transpose(everyone[:, N_META + 8:N_META + 8 + kcf, :cs], (1, 0, 2)).reshape(kcf, C1)

    big = {"ffn1_w_gate": ffn1_w_gate, "ffn1_w_up": ffn1_w_up, "ffn1_w_down": ffn1_w_down, "w_in": w_in, "w_out": w_out,
           "ffn2_w_gate": ffn2_w_gate, "ffn2_w_up": ffn2_w_up, "ffn2_w_down": ffn2_w_down}
    big_m = {"ffn1_w_gate": m_ffn1_w_gate, "ffn1_w_up": m_ffn1_w_up, "ffn1_w_down": m_ffn1_w_down, "w_in": m_w_in,
             "w_out": m_w_out, "ffn2_w_gate": m_ffn2_w_gate, "ffn2_w_up": m_ffn2_w_up, "ffn2_w_down": m_ffn2_w_down}
    big_v = {"ffn1_w_gate": v_ffn1_w_gate, "ffn1_w_up": v_ffn1_w_up, "ffn1_w_down": v_ffn1_w_down, "w_in": v_w_in,
             "w_out": v_w_out, "ffn2_w_gate": v_ffn2_w_gate, "ffn2_w_up": v_ffn2_w_up, "ffn2_w_down": v_ffn2_w_down}
    buf = {nm: _cast_own_block(place, w[0], "cast_" + nm) for nm, w in big.items()}
    whole_weight = lambda g: g.reshape(N_CHIPS, 2 * g.shape[2], g.shape[3])
    corner = lambda a: a.reshape(-1, a.shape[-1])[:8, :128]

    NEAR, FAR = (0, 1), (2,)
    groups = {"ffn1_near": (["ffn1_w_gate", "ffn1_w_up", "ffn1_w_down"], NEAR),
              "ffn1_far": (["ffn1_w_gate", "ffn1_w_up", "ffn1_w_down"], FAR),
              "mix": (["w_in", "w_out"], NEAR + FAR),
              "ffn2_up": (["ffn2_w_gate", "ffn2_w_up"], NEAR + FAR),
              "ffn2_down": (["ffn2_w_down"], NEAR + FAR)}
    started = {}

    def start(tag, after):
        nms, rels = groups[tag]
        copies = functools.partial(_gather_copies, rels=rels)
        send, recv, thru, token = _start_copies([buf[nm] for nm in nms], after, len(rels) * len(nms), copies,
                                                "gather_start_" + tag)
        for nm, b in zip(nms, thru):
            buf[nm] = b
        started[tag] = (send, recv, copies)
        return token

    def arrive(tag, after, then=None):
        nms, rels = groups[tag]
        send, recv, copies = started[tag]
        got = _wait_copies(send, recv, [buf[nm] for nm in nms], corner(after), copies, "gather_wait_" + tag)
        for nm, b in zip(nms, got):
            buf[nm] = b
        if then is not None:
            start(then, corner(got[0]))
        for nm, b in zip(nms, _forward_halves([buf[nm] for nm in nms], "gather_forward_" + tag, rels)):
            buf[nm] = b

    def passing(tag, after):
        nms, rels = groups[tag]
        send, recv, copies = started[tag]
        got = _wait_copies(send, recv, [buf[nm] for nm in nms], corner(after), copies, "gather_wait_" + tag)
        copies = functools.partial(_forward_copies, rels=rels)
        send, recv, thru, token = _start_copies(got, corner(got[0]), len(rels) * len(nms), copies,
                                                "gather_pass_" + tag)
        for nm, b in zip(nms, thru):
            buf[nm] = b
        started[tag] = (send, recv, copies)
        return token

    def passed(tag, after):
        nms, _ = groups[tag]
        send, recv, copies = started[tag]
        for nm, b in zip(nms, _wait_copies(send, recv, [buf[nm] for nm in nms], corner(after), copies,
                                           "gather_passed_" + tag)):
            buf[nm] = b

    tokens = lambda *arrays: jnp.concatenate([corner(a).astype(F32) for a in arrays], axis=0)
    token = start("ffn1_near", corner(everyone))

    ffn1 = lambda: [whole_weight(buf[nm]) for nm in ["ffn1_w_gate", "ffn1_w_up", "ffn1_w_down"]]
    own = chip[None].astype(jnp.int32)
    near = jnp.stack([chip ^ 2, chip ^ 1]).astype(jnp.int32)
    far = (chip ^ 3)[None].astype(jnp.int32)
    all_chips = jnp.arange(N_CHIPS, dtype=jnp.int32)

    hs0, n1 = _embed_rms(x2, meta_full, ffn1_norm)
    wg1, wu1, wd1 = ffn1()
    gua = _ffn_up(n1, wg1, wu1, own, None, token, "ffn1_up_own")
    hs1 = _ffn_down(gua[2], wd1, hs0, own, "ffn1_down_own")
    later = [buf[nm] for nm in ["w_in", "w_out", "ffn2_w_gate", "ffn2_w_up", "ffn2_w_down"]]
    arrive("ffn1_near", tokens(hs1, *later), "ffn1_far")
    wg1, wu1, wd1 = ffn1()
    gua = _ffn_up(n1, wg1, wu1, near, gua, token, "ffn1_up_near")
    tok = start("mix", corner(gua[2]))
    tok = passing("ffn1_far", tok)
    hs1 = _ffn_down(gua[2], ffn1()[2], hs1, near, "ffn1_down_near", tok)
    passed("ffn1_far", hs1)
    wg1, wu1, wd1 = ffn1()
    g1, u1, a1 = _ffn_up(n1, wg1, wu1, far, gua, token, "ffn1_up_far")
    tok = passing("mix", a1)
    hs1 = _ffn_down(a1, wd1, hs1, far, "ffn1_down_far", tok)
    F = N_CHIPS * wd1.shape[1]
    tok = start("ffn2_up", corner(hs1))
    passed("mix", tok)
    win, wout = whole_weight(buf["w_in"]), whole_weight(buf["w_out"])
    n2 = _rms(hs1, mix_norm, "rms_mix")
    u = _mix_in(n2, win, b_in)
    y, z1 = _mix_conv_fwd(u, wsc_full, wcf_full, conv_cf_b, ln_cf_g, ln_cf_b)
    tok = start("ffn2_down", corner(y))
    tok = passing("ffn2_up", tok)
    hs2 = _mix_out(y, wout.reshape(D, D), hs1, tok)
    passed("ffn2_up", hs2)
    wg2, wu2 = whole_weight(buf["ffn2_w_gate"]), whole_weight(buf["ffn2_w_up"])
    n3 = _rms(hs2, ffn2_norm, "rms_ffn2")
    tok = passing("ffn2_down", n3)
    g2, u2, a2 = _ffn_up(n3, wg2, wu2, all_chips, None, tok, "ffn2_up")
    passed("ffn2_down", a2)
    wd2 = whole_weight(buf["ffn2_w_down"])
    hs3 = _ffn_down_whole(a2, wd2.reshape(F, D), hs2, "ffn2_down")
    token_ffn2 = token

    def pair_start(group, after, tag):
        gs = [g for _, g in group]
        lands = [lax.empty((N_CHIPS,) + g.shape[2:], F32) for g in gs]
        send, recv, thru, token = _start_copies(gs + lands, after, N_CHIPS * len(gs), _pair_copies,
                                                "pair_start_" + tag)
        return (group, send, recv, thru, tag), token

    def scatter_start(state, after):
        group, send, recv, thru, tag = state
        thru = _wait_copies(send, recv, thru, corner(after), _pair_copies, "pair_wait_" + tag)
        gs, sib = thru[:len(group)], thru[len(group):]
        sums = [_pair_sum(place, g, rb, "pair_sum_" + nm) for (nm, _), g, rb in zip(group, gs, sib)]
        lands = [lax.empty((3,) + q.shape[1:], BF16) for q in sums]
        send, recv, thru, token = _start_copies(sums + lands, corner(sums[-1]), 3 * len(gs), _scatter_copies,
                                                "scatter_start_" + tag)
        return ([(nm, g) for (nm, _), g in zip(group, gs)], sib, send, recv, thru, tag), token

    def finish_sum(state, after):
        group, sib, send, recv, thru, tag = state
        lands = _wait_copies(send, recv, thru, corner(after), _scatter_copies, "scatter_wait_" + tag)[len(group):]
        mine = [_chip_sum(place, g, rb, rc, "chip_sum_" + nm) for (nm, g), rb, rc in zip(group, sib, lands)]
        send, recv, thru, token = _start_copies(mine, corner(mine[-1]), len(mine), _half_copies, "half_start_" + tag)
        return (group, send, recv, thru, tag), token

    def finish_adam(state, after):
        group, send, recv, thru, tag = state
        whole = _wait_copies(send, recv, thru, corner(after), _half_copies, "half_wait_" + tag)
        out = {}
        for (nm, _), g in zip(group, whole):
            w = big[nm]
            g_out, d, new_m, new_v = _adamw(w[0], g.reshape(w.shape[1:]), big_m[nm][0], big_v[nm][0], "adamw_" + nm)
            out[nm] = (g_out[None], d[None], new_m[None], new_v[None])
        return out

    dhs3, df2, loss_row, d_final = _final_loss(hs3, final_norm.reshape(1, D), tgt)

    dg2, du2 = _ffn_bwd_act(df2, wd2.reshape(F, D), g2, u2, token_ffn2, "ffn2_bwd_act")
    gw_d2 = _wgrad_down(a2, df2, "wgrad_ffn2_down")
    gw_g2 = _wgrad_cols(n3, [dg2], "wgrad_ffn2_gate")[0]
    gw_u2 = _wgrad_cols(n3, [du2], "wgrad_ffn2_up")[0]
    pair_ffn2, token = pair_start([("ffn2_w_gate", gw_g2), ("ffn2_w_up", gw_u2), ("ffn2_w_down", gw_d2)],
                                  corner(gw_u2), "ffn2")
    dn3 = _nt_panel([dg2, du2], [wg2, wu2], token, "ffn2_bwd_in")
    red_ffn2, token = scatter_start(pair_ffn2, dn3)
    dhs2, dm, d_ffn2 = _rms_bwd(dn3, hs2, ffn2_norm, dhs3, 1.0, "rms_bwd_ffn2")

    dy = _nt_panel([dm], [wout.reshape(1, D, D)], token, "mix_bwd_out")
    gw_out = _wgrad_out(y, dm)
    dz1, dcs, db, d_lg, d_lb, d_bcf = _mix_conv_bwd1(u, z1, dy, wsc_full, ln_cf_g, ln_cf_b)
    du, d_bin, d_wsc, d_wcf = _mix_conv_bwd2(u, dz1, dcs, db, wsc_full, wcf_full)
    gw_in = _wgrad_cols(n2, [du], "wgrad_w_in")[0]
    pair_mix, token = pair_start([("w_in", gw_in), ("w_out", gw_out)], corner(gw_in), "mix")
    dn2 = _nt_panel([du], [win], token, "mix_bwd_in")
    red_mix, token = scatter_start(pair_mix, dn2)
    dhs1, df1, d_mix = _rms_bwd(dn2, hs1, mix_norm, dhs2, FFN_RES_SCALE, "rms_bwd_mix")

    dg1, du1 = _ffn_bwd_act(df1, wd1.reshape(F, D), g1, u1, token, "ffn1_bwd_act")
    gw_d1 = _wgrad_down(a1, df1, "wgrad_ffn1_down")
    gw_g1 = _wgrad_cols(n1, [dg1], "wgrad_ffn1_gate")[0]
    pair_ffn1a, token = pair_start([("ffn1_w_down", gw_d1), ("ffn1_w_gate", gw_g1)], corner(gw_g1), "ffn1a")
    gw_u1 = _wgrad_cols(n1, [du1], "wgrad_ffn1_up", token)[0]
    red_ffn1a, token = scatter_start(pair_ffn1a, gw_u1)
    pair_ffn1b, token = pair_start([("ffn1_w_up", gw_u1)], token, "ffn1b")
    dn1 = _nt_panel([dg1, du1], [wg1, wu1], token, "ffn1_bwd_in")
    red_ffn1b, token = scatter_start(pair_ffn1b, dn1)
    grad_x, d_meta, d_ffn1 = _rms_bwd_first(dn1, hs0, ffn1_norm, dhs1, token)

    half_ffn2, tok = finish_sum(red_ffn2, grad_x)
    half_mix, tok = finish_sum(red_mix, tok)
    big_out = finish_adam(half_ffn2, tok)
    half_ffn1a, tok = finish_sum(red_ffn1a, big_out["ffn2_w_down"][1])
    big_out.update(finish_adam(half_mix, tok))
    half_ffn1b, tok = finish_sum(red_ffn1b, big_out["w_out"][1])
    big_out.update(finish_adam(half_ffn1a, tok))
    big_out.update(finish_adam(half_ffn1b, big_out["ffn1_w_gate"][1]))

    W = C1
    rows = lambda a: a.reshape(-1, W)
    parts = [rows(d_ffn1), rows(d_mix), rows(d_ffn2), rows(d_final), rows(d_bin), d_bcf, d_lg, d_lb,
             d_wsc, d_wcf, rows(d_meta), jnp.broadcast_to(loss_row[:, :1], (1, W))]
    sizes = [p.shape[0] for p in parts]
    total_rows = sum(sizes)
    packed = _pad_rows(jnp.concatenate(parts, axis=0), -(-total_rows // 8) * 8)
    summed = _share_small(packed, True, "sum_small", big_out["ffn1_w_up"][1])
    offs = [0]
    for n in sizes:
        offs.append(offs[-1] + n)
    piece = lambda k: summed[offs[k]:offs[k + 1]]
    loss = piece(11)[0, 0]
    g_ffn1, g_mix, g_ffn2 = (piece(k).reshape(1, D) for k in range(3))
    g_final = piece(3).reshape(1, D)
    g_bin = piece(4).reshape(1, -1)
    g_bcf, g_lg, g_lb = piece(5), piece(6), piece(7)
    g_wsc = lax.dynamic_slice_in_dim(piece(8), chip * cs, cs, axis=1)
    g_wcf = lax.dynamic_slice_in_dim(piece(9), chip * cs, cs, axis=1)
    g_meta = lax.dynamic_slice_in_dim(piece(10).reshape(N_META, D), chip * ms, ms, axis=1)

    small_names = ["meta_tokens", "ffn1_norm", "mix_norm", "b_in", "conv_sc_w", "conv_cf_w", "conv_cf_b", "ln_cf_g",
                   "ln_cf_b", "ffn2_norm", "final_norm"]
    small_w = [meta_tokens, ffn1_norm, mix_norm, b_in, conv_sc_w[0], conv_cf_w[0], conv_cf_b, ln_cf_g, ln_cf_b,
               ffn2_norm, final_norm.reshape(1, D)]
    small_g = [g_meta, g_ffn1, g_mix, g_bin, g_wsc, g_wcf, g_bcf, g_lg, g_lb, g_ffn2, g_final]
    small_m = [m_meta_tokens, m_ffn1_norm, m_mix_norm, m_b_in, m_conv_sc_w[0], m_conv_cf_w[0], m_conv_cf_b, m_ln_cf_g,
               m_ln_cf_b, m_ffn2_norm, m_final_norm.reshape(1, D)]
    small_v = [v_meta_tokens, v_ffn1_norm, v_mix_norm, v_b_in, v_conv_sc_w[0], v_conv_cf_w[0], v_conv_cf_b, v_ln_cf_g,
               v_ln_cf_b, v_ffn2_norm, v_final_norm.reshape(1, D)]
    s_d, s_m, s_v = _adamw_small(small_w, small_g, small_m, small_v)
    shapes = {"conv_sc_w": conv_sc_w.shape, "conv_cf_w": conv_cf_w.shape, "final_norm": final_norm.shape}
    small_out = {}
    for nm, g, d, m, v in zip(small_names, small_g, s_d, s_m, s_v):
        shp = shapes.get(nm, g.shape)
        small_out[nm] = tuple(t.reshape(shp) for t in (g, d, m, v))

    order = ["meta_tokens", "ffn1_norm", "ffn1_w_gate", "ffn1_w_up", "ffn1_w_down", "mix_norm", "w_in", "b_in",
             "conv_sc_w", "conv_cf_w", "conv_cf_b", "ln_cf_g", "ln_cf_b", "w_out", "ffn2_norm", "ffn2_w_gate",
             "ffn2_w_up", "ffn2_w_down", "final_norm"]
    res = {**big_out, **small_out}
    outs = [loss, grad_x[None]]
    for q in range(4):
        outs.extend(res[nm][q] for nm in order)
    return tuple(outs)
```

```python
import functools

import jax
import jax.numpy as jnp
from jax import lax
from jax.experimental import pallas as pl
from jax.experimental.pallas import tpu as pltpu

F32 = jnp.float32
BF16 = jnp.bfloat16
MESH = pl.DeviceIdType.MESH

N_META = 16
TT = 128
PAD = TT - N_META
HALO = 32
EPS = 1e-6
FFN_RES_SCALE = 0.5
N_CHIPS = 4
N_DEV = 8

ADAM_LR = 0.001
ADAM_B1 = 0.9
ADAM_B2 = 0.999
ADAM_EPS = 1e-08
ADAM_WD = 0.01
ADAM_STEP = 10

V7X_VMEM_BYTES = 64 * 2 ** 20
NT_DIMS = (((1,), (1,)), ((), ()))
TN_DIMS = (((0,), (0,)), ((), ()))


def _params(semantics, block_bytes):
    limit = min(2 * block_bytes + 16 * 2 ** 20, V7X_VMEM_BYTES - 6 * 2 ** 20)
    return pltpu.CompilerParams(dimension_semantics=semantics, vmem_limit_bytes=int(limit))


def _pallas(body, out_shape, **kw):
    if "grid" not in kw and "grid_spec" not in kw:
        return pl.pallas_call(body, out_shape=out_shape, **kw)
    big = lambda shape, dtype: jnp.issubdtype(dtype, jnp.floating) and len(shape) >= 2
    pin_out = lambda s: pltpu.HBM(s.shape, s.dtype) if big(s.shape, s.dtype) else s
    single = not isinstance(out_shape, (list, tuple))
    shapes = pin_out(out_shape) if single else [pin_out(s) for s in out_shape]
    call = pl.pallas_call(body, out_shape=shapes, **kw)
    pin = lambda a: pltpu.with_memory_space_constraint(a, pltpu.HBM) if big(a.shape, a.dtype) else a
    return lambda *operands: call(*[pin(a) for a in operands])


def _nbytes(shape, dtype):
    n = 1
    for d in shape:
        if d is not None:
            n *= d
    return n * jnp.dtype(dtype).itemsize


def _row_tile(rows, target, mult=8):
    best = None
    for t in range(mult, min(rows, target) + 1, mult):
        if rows % t == 0:
            best = t
    assert best is not None, (rows, target, mult)
    return best


def _sigmoid(v):
    return jax.nn.sigmoid(v)


def _sigmoid_fast(v):
    return pl.reciprocal(1.0 + jnp.exp(-v), approx=True)


def _dsilu(v, s):
    return s * (1.0 + v * (1.0 - s))


def _embed_rms(x2, meta, gain):
    S, D = x2.shape
    T = S + TT

    def body(x_ref, meta_ref, g_ref, hs_ref, n_ref):
        i = pl.program_id(0)

        @pl.when(i == 0)
        def _():
            hs_ref[...] = jnp.zeros_like(hs_ref)
            hs_ref[PAD:, :] = meta_ref[...]

        @pl.when(i > 0)
        def _():
            hs_ref[...] = x_ref[...]

        h = hs_ref[...]
        r = lax.rsqrt(jnp.mean(h * h, axis=-1, keepdims=True) + EPS)
        n_ref[...] = ((h * r) * g_ref[...]).astype(BF16)

    blk = _nbytes((TT, D), F32) * 2 + _nbytes((TT, D), BF16)
    return _pallas(
        body, name="embed_rms", grid=(T // TT,),
        in_specs=[pl.BlockSpec((TT, D), lambda i: (jnp.maximum(i - 1, 0), 0)),
                  pl.BlockSpec((N_META, D), lambda i: (0, 0)),
                  pl.BlockSpec((1, D), lambda i: (0, 0))],
        out_specs=[pl.BlockSpec((TT, D), lambda i: (i, 0)), pl.BlockSpec((TT, D), lambda i: (i, 0))],
        out_shape=[jax.ShapeDtypeStruct((T, D), F32), jax.ShapeDtypeStruct((T, D), BF16)],
        compiler_params=_params(("parallel",), blk),
    )(x2, meta, gain)


def _rms(hs, gain, name):
    T, D = hs.shape
    te = _row_tile(T, 384)

    def body(h_ref, g_ref, n_ref):
        h = h_ref[...]
        r = lax.rsqrt(jnp.mean(h * h, axis=-1, keepdims=True) + EPS)
        n_ref[...] = ((h * r) * g_ref[...]).astype(BF16)

    blk = _nbytes((te, D), F32) + _nbytes((te, D), BF16)
    return _pallas(
        body, name=name, grid=(T // te,),
        in_specs=[pl.BlockSpec((te, D), lambda i: (i, 0)), pl.BlockSpec((1, D), lambda i: (0, 0))],
        out_specs=pl.BlockSpec((te, D), lambda i: (i, 0)),
        out_shape=jax.ShapeDtypeStruct((T, D), BF16),
        compiler_params=_params(("parallel",), blk),
    )(hs, gain)


def _rms_bwd_math(dn, h, g):
    r = lax.rsqrt(jnp.mean(h * h, axis=-1, keepdims=True) + EPS)
    xh = h * r
    dgain = jnp.sum(dn * xh, axis=0, keepdims=True)
    dxh = dn * g
    dh = r * (dxh - xh * jnp.mean(dxh * xh, axis=-1, keepdims=True))
    return dh, dgain


def _rms_bwd(dn, hs, gain, dres, scale, name):
    T, D = hs.shape
    te = _row_tile(T, 384)

    def body(dn_ref, h_ref, g_ref, dres_ref, dhs_ref, dhb_ref, dg_ref):
        dh, dgain = _rms_bwd_math(dn_ref[...], h_ref[...], g_ref[...])
        d = dres_ref[...] + dh
        dhs_ref[...] = d
        dhb_ref[...] = (scale * d).astype(BF16)

        @pl.when(pl.program_id(0) == 0)
        def _():
            dg_ref[...] = jnp.zeros_like(dg_ref)

        dg_ref[...] += dgain

    blk = _nbytes((te, D), F32) * 4 + _nbytes((te, D), BF16)
    row = lambda i: (i, 0)
    return _pallas(
        body, name=name, grid=(T // te,),
        in_specs=[pl.BlockSpec((te, D), row), pl.BlockSpec((te, D), row), pl.BlockSpec((1, D), lambda i: (0, 0)),
                  pl.BlockSpec((te, D), row)],
        out_specs=[pl.BlockSpec((te, D), row), pl.BlockSpec((te, D), row), pl.BlockSpec((1, D), lambda i: (0, 0))],
        out_shape=[jax.ShapeDtypeStruct((T, D), F32), jax.ShapeDtypeStruct((T, D), BF16),
                   jax.ShapeDtypeStruct((1, D), F32)],
        compiler_params=_params(("arbitrary",), blk),
    )(dn, hs, gain, dres)


def _rms_bwd_first(dn, hs, gain, dres, after):
    T, D = hs.shape
    S = T - TT

    def body(dn_ref, h_ref, g_ref, dres_ref, after_ref, gx_ref, gm_ref, dg_ref):
        i = pl.program_id(0)
        dh, dgain = _rms_bwd_math(dn_ref[...], h_ref[...], g_ref[...])
        d = dres_ref[...] + dh

        @pl.when(i == 0)
        def _():
            dg_ref[...] = jnp.zeros_like(dg_ref)
            gm_ref[...] = d[PAD:, :]

        @pl.when(i > 0)
        def _():
            gx_ref[...] = d

        dg_ref[...] += dgain

    blk = _nbytes((TT, D), F32) * 4
    row = lambda i: (i, 0)
    return _pallas(
        body, name="rms_bwd_ffn1", grid=(T // TT,),
        in_specs=[pl.BlockSpec((TT, D), row), pl.BlockSpec((TT, D), row), pl.BlockSpec((1, D), lambda i: (0, 0)),
                  pl.BlockSpec((TT, D), row), TOKEN],
        out_specs=[pl.BlockSpec((TT, D), lambda i: (jnp.maximum(i - 1, 0), 0)),
                   pl.BlockSpec((N_META, D), lambda i: (0, 0)), pl.BlockSpec((1, D), lambda i: (0, 0))],
        out_shape=[jax.ShapeDtypeStruct((S, D), F32), jax.ShapeDtypeStruct((N_META, D), F32),
                   jax.ShapeDtypeStruct((1, D), F32)],
        compiler_params=_params(("arbitrary",), blk),
    )(dn, hs, gain, dres, after)


def _final_loss(hs, gain, tgt):
    T, D = hs.shape

    def body(h_ref, g_ref, t_ref, dhs_ref, dhb_ref, loss_ref, dg_ref):
        i = pl.program_id(0)
        h = h_ref[...]
        g = g_ref[...]
        r = lax.rsqrt(jnp.mean(h * h, axis=-1, keepdims=True) + EPS)
        xh = h * r
        e = jnp.where(i > 0, xh * g - t_ref[...], 0.0)
        tile_loss = jnp.sum(jnp.sum(e * e, axis=1, keepdims=True), axis=0, keepdims=True) * (0.5 / D)
        dout = e * (1.0 / D)
        dgain = jnp.sum(dout * xh, axis=0, keepdims=True)
        dxh = dout * g
        d = r * (dxh - xh * jnp.mean(dxh * xh, axis=-1, keepdims=True))
        dhs_ref[...] = d
        dhb_ref[...] = (FFN_RES_SCALE * d).astype(BF16)

        @pl.when(i == 0)
        def _():
            loss_ref[...] = jnp.zeros_like(loss_ref)
            dg_ref[...] = jnp.zeros_like(dg_ref)

        loss_ref[...] += jnp.broadcast_to(tile_loss, loss_ref.shape)
        dg_ref[...] += dgain

    blk = _nbytes((TT, D), F32) * 3 + _nbytes((TT, D), BF16)
    row = lambda i: (i, 0)
    return _pallas(
        body, name="final_loss", grid=(T // TT,),
        in_specs=[pl.BlockSpec((TT, D), row), pl.BlockSpec((1, D), lambda i: (0, 0)),
                  pl.BlockSpec((TT, D), lambda i: (jnp.maximum(i - 1, 0), 0))],
        out_specs=[pl.BlockSpec((TT, D), row), pl.BlockSpec((TT, D), row),
                   pl.BlockSpec((1, 128), lambda i: (0, 0)), pl.BlockSpec((1, D), lambda i: (0, 0))],
        out_shape=[jax.ShapeDtypeStruct((T, D), F32), jax.ShapeDtypeStruct((T, D), BF16),
                   jax.ShapeDtypeStruct((1, 128), F32), jax.ShapeDtypeStruct((1, D), F32)],
        compiler_params=_params(("arbitrary",), blk),
    )(hs, gain, tgt)


MXU_COLS = 256


def _tm(T):
    return _row_tile(T, 704, 16)


def _col_chunks(n):
    return [(c, min(MXU_COLS, n - c)) for c in range(0, n, MXU_COLS)]


TOKEN = pl.BlockSpec((8, 128), lambda *_: (0, 0))


def _ffn_up(n, wg, wu, shards, prev, after, name):
    T, D = n.shape
    Fs = wg.shape[2]
    tm = _tm(T)
    nprev = 0 if prev is None else 3

    def body(shards_ref, n_ref, wg_ref, wu_ref, after_ref, *refs):
        g_ref, u_ref, a_ref = refs[nprev:]
        nn = n_ref[...]
        for c0, cw in _col_chunks(Fs):
            if 2 * cw == MXU_COLS:
                both = jnp.concatenate([wg_ref[:, c0:c0 + cw], wu_ref[:, c0:c0 + cw]], axis=1)
                gu = jnp.dot(nn, both, preferred_element_type=F32)
                g, u = gu[:, :cw], gu[:, cw:]
            else:
                g = jnp.dot(nn, wg_ref[:, c0:c0 + cw], preferred_element_type=F32)
                u = jnp.dot(nn, wu_ref[:, c0:c0 + cw], preferred_element_type=F32)
            g_ref[:, c0:c0 + cw] = g.astype(BF16)
            u_ref[:, c0:c0 + cw] = u.astype(BF16)
            a_ref[:, c0:c0 + cw] = (g * _sigmoid_fast(g) * u).astype(BF16)

    blk = _nbytes((tm, D), BF16) + 2 * _nbytes((D, Fs), BF16) + 3 * _nbytes((tm, Fs), BF16)
    out = pl.BlockSpec((tm, Fs), lambda j, i, p: (i, p[j]))
    shp = jax.ShapeDtypeStruct((T, N_CHIPS * Fs), BF16)
    return _pallas(
        body, name=name,
        grid_spec=pltpu.PrefetchScalarGridSpec(
            num_scalar_prefetch=1, grid=(shards.shape[0], T // tm),
            in_specs=[pl.BlockSpec((tm, D), lambda j, i, p: (i, 0)),
                      pl.BlockSpec((None, D, Fs), lambda j, i, p: (p[j], 0, 0)),
                      pl.BlockSpec((None, D, Fs), lambda j, i, p: (p[j], 0, 0)), TOKEN] + [ANY] * nprev,
            out_specs=[out, out, out]),
        out_shape=[shp, shp, shp], input_output_aliases={5 + q: q for q in range(nprev)},
        compiler_params=_params(("arbitrary", "arbitrary"), blk),
    )(shards, n, wg, wu, after, *(prev or ()))


def _ffn_down(a, wd, hs, shards, name, after=None):
    T, F = a.shape
    _, Fs, D = wd.shape
    tm = _tm(T)
    tn = D // 2
    extra = [] if after is None else [after]

    def body(shards_ref, a_ref, w_ref, h_ref, *refs):
        o_ref = refs[-1]
        part = FFN_RES_SCALE * jnp.dot(a_ref[...], w_ref[...], preferred_element_type=F32)

        @pl.when(pl.program_id(2) == 0)
        def _():
            o_ref[...] = h_ref[...] + part

        @pl.when(pl.program_id(2) > 0)
        def _():
            o_ref[...] += part

    blk = _nbytes((tm, Fs), BF16) + _nbytes((Fs, tn), BF16) + 3 * _nbytes((tm, tn), F32)
    return _pallas(
        body, name=name,
        grid_spec=pltpu.PrefetchScalarGridSpec(
            num_scalar_prefetch=1, grid=(D // tn, T // tm, shards.shape[0]),
            in_specs=[pl.BlockSpec((tm, Fs), lambda n, i, k, p: (i, p[k])),
                      pl.BlockSpec((None, Fs, tn), lambda n, i, k, p: (p[k], 0, n)),
                      pl.BlockSpec((tm, tn), lambda n, i, k, p: (i, n))] + [TOKEN] * len(extra),
            out_specs=pl.BlockSpec((tm, tn), lambda n, i, k, p: (i, n))),
        out_shape=jax.ShapeDtypeStruct((T, D), F32),
        compiler_params=_params(("parallel", "parallel", "arbitrary"), blk),
    )(shards, a, wd, hs, *extra)


def _ffn_down_whole(a, wd, hs, name):
    T, F = a.shape
    D = wd.shape[1]
    tm = _tm(T)
    tn = D // 4

    def body(a_ref, w_ref, h_ref, o_ref):
        o_ref[...] = h_ref[...] + FFN_RES_SCALE * jnp.dot(a_ref[...], w_ref[...], preferred_element_type=F32)

    blk = _nbytes((tm, F), BF16) + _nbytes((F, tn), BF16) + 3 * _nbytes((tm, tn), F32)
    return _pallas(
        body, name=name, grid=(D // tn, T // tm),
        in_specs=[pl.BlockSpec((tm, F), lambda n, i: (i, 0)), pl.BlockSpec((F, tn), lambda n, i: (0, n)),
                  pl.BlockSpec((tm, tn), lambda n, i: (i, n))],
        out_specs=pl.BlockSpec((tm, tn), lambda n, i: (i, n)),
        out_shape=jax.ShapeDtypeStruct((T, D), F32),
        compiler_params=_params(("parallel", "parallel"), blk),
    )(a, wd, hs)


def _mix_in(n, w, b):
    T, D = n.shape
    Ns = w.shape[2]
    tm = _tm(T)

    def body(n_ref, w_ref, b_ref, u_ref):
        u_ref[...] = jnp.dot(n_ref[...], w_ref[...], preferred_element_type=F32) + b_ref[...]

    blk = _nbytes((tm, D), BF16) + _nbytes((D, Ns), BF16) + 2 * _nbytes((tm, Ns), F32)
    return _pallas(
        body, name="mix_in", grid=(N_CHIPS, T // tm),
        in_specs=[pl.BlockSpec((tm, D), lambda j, i: (i, 0)), pl.BlockSpec((None, D, Ns), lambda j, i: (j, 0, 0)),
                  pl.BlockSpec((1, Ns), lambda j, i: (0, j))],
        out_specs=pl.BlockSpec((tm, Ns), lambda j, i: (i, j)),
        out_shape=jax.ShapeDtypeStruct((T, N_CHIPS * Ns), F32),
        compiler_params=_params(("parallel", "parallel"), blk),
    )(n, w, b)


def _mix_out(y, w, hs, after):
    T, D = y.shape
    tm = _tm(T)

    def body(y_ref, w_ref, h_ref, after_ref, o_ref):
        o_ref[...] = h_ref[...] + jnp.dot(y_ref[...], w_ref[...], preferred_element_type=F32)

    blk = _nbytes((tm, D), BF16) + _nbytes((D, D), BF16) + 3 * _nbytes((tm, D), F32)
    return _pallas(
        body, name="mix_out", grid=(T // tm,),
        in_specs=[pl.BlockSpec((tm, D), lambda i: (i, 0)), pl.BlockSpec((D, D), lambda i: (0, 0)),
                  pl.BlockSpec((tm, D), lambda i: (i, 0)), TOKEN],
        out_specs=pl.BlockSpec((tm, D), lambda i: (i, 0)),
        out_shape=jax.ShapeDtypeStruct((T, D), F32),
        compiler_params=_params(("parallel",), blk),
    )(y, w, hs, after)


def _ffn_bwd_act(dfb, wd, g, u, after, name):
    T, D = dfb.shape
    F = wd.shape[0]
    tm = _row_tile(T, 1408, 16)
    tn = 2 * MXU_COLS

    tr = _row_tile(tm, 352, 16)

    def body(d_ref, w_ref, g_ref, u_ref, after_ref, dg_ref, du_ref):
        for r0 in range(0, tm, tr):
            dv = d_ref[r0:r0 + tr, :]
            for c0, cw in _col_chunks(tn):
                da = lax.dot_general(dv, w_ref[c0:c0 + cw, :], NT_DIMS, preferred_element_type=F32)
                gv = g_ref[r0:r0 + tr, c0:c0 + cw].astype(F32)
                uv = u_ref[r0:r0 + tr, c0:c0 + cw].astype(F32)
                s = _sigmoid_fast(gv)
                du_ref[r0:r0 + tr, c0:c0 + cw] = (da * (gv * s)).astype(BF16)
                dg_ref[r0:r0 + tr, c0:c0 + cw] = (da * uv * _dsilu(gv, s)).astype(BF16)

    blk = _nbytes((tm, D), BF16) + _nbytes((tn, D), BF16) + 4 * _nbytes((tm, tn), BF16)
    io = pl.BlockSpec((tm, tn), lambda n, i: (i, n))
    shp = jax.ShapeDtypeStruct((T, F), BF16)
    return _pallas(
        body, name=name, grid=(F // tn, T // tm),
        in_specs=[pl.BlockSpec((tm, D), lambda n, i: (i, 0)), pl.BlockSpec((tn, D), lambda n, i: (n, 0)), io, io, TOKEN],
        out_specs=[io, io], out_shape=[shp, shp],
        compiler_params=_params(("parallel", "parallel"), blk),
    )(dfb, wd, g, u, after)


def _nt_panel(lhs_list, w_list, after, name):
    T = lhs_list[0].shape[0]
    nsh, Dout, Ks = w_list[0].shape
    npair = len(lhs_list)
    tm = _row_tile(T, 1408, 16)
    tn = Dout // 2

    def body(*refs):
        l_refs, w_refs, o_ref = refs[:npair], refs[npair:2 * npair], refs[2 * npair + 1]
        j = pl.program_id(2)
        k0 = Ks - Ks % MXU_COLS if npair == 2 and 2 * (Ks % MXU_COLS) == MXU_COLS else Ks
        acc = None
        for p in range(npair):
            part = lax.dot_general(l_refs[p][:, :k0], w_refs[p][:, :k0], NT_DIMS, preferred_element_type=F32)
            acc = part if acc is None else acc + part
        if k0 < Ks:
            lhs = jnp.concatenate([l_refs[p][:, k0:] for p in range(npair)], axis=1)
            rhs = jnp.concatenate([w_refs[p][:, k0:] for p in range(npair)], axis=1)
            acc = acc + lax.dot_general(lhs, rhs, NT_DIMS, preferred_element_type=F32)

        @pl.when(j == 0)
        def _():
            o_ref[...] = acc

        @pl.when(j > 0)
        def _():
            o_ref[...] += acc

    blk = npair * (_nbytes((tm, Ks), BF16) + _nbytes((tn, Ks), BF16)) + 2 * _nbytes((tm, tn), F32)
    return _pallas(
        body, name=name, grid=(Dout // tn, T // tm, nsh),
        in_specs=[pl.BlockSpec((tm, Ks), lambda n, i, j: (i, j))] * npair
                 + [pl.BlockSpec((None, tn, Ks), lambda n, i, j: (j, n, 0))] * npair + [TOKEN],
        out_specs=pl.BlockSpec((tm, tn), lambda n, i, j: (i, n)),
        out_shape=jax.ShapeDtypeStruct((T, Dout), F32),
        compiler_params=_params(("parallel", "parallel", "arbitrary"), blk),
    )(*lhs_list, *w_list, after)


def _tn_call(name, grid, lhs, lhs_spec, rhs_list, rhs_specs, out_shapes, out_specs, blk, after=None, rider=None):
    nr = len(rhs_list)
    extra = [] if after is None else [after]
    ride = [] if rider is None else list(rider)
    ride_specs, ride_shapes = [], []
    if ride:
        steps = 1
        for extent in grid:
            steps *= extent
        R, C = ride[0].shape
        rb = R // steps
        assert R % steps == 0 and rb % SUBLANES == 0, (R, steps)

        def step(*idx):
            lin = idx[0]
            for extent, i in zip(grid[1:], idx[1:]):
                lin = lin * extent + i
            return lin, 0

        ride_specs = [pl.BlockSpec((rb, C), step)] * 4
        ride_shapes = [jax.ShapeDtypeStruct((R, C), F32)] * 4
        blk = blk + 8 * _nbytes((rb, C), F32)
    nin = 1 + nr + len(extra) + len(ride)

    def body(*refs):
        l_ref, r_refs, o_refs = refs[0], refs[1:1 + nr], refs[nin:nin + nr]
        k = pl.program_id(len(grid) - 1)
        lv = l_ref[...]
        for q in range(nr):
            part = lax.dot_general(lv, r_refs[q][...], TN_DIMS, preferred_element_type=F32)
            part = part.reshape(o_refs[q].shape)

            @pl.when(k == 0)
            def _(o=o_refs[q], part=part):
                o[...] = part

            @pl.when(k > 0)
            def _(o=o_refs[q], part=part):
                o[...] += part

        if ride:
            w_ref, g_ref, m_ref, v_ref = refs[nin - 4:nin]
            go_ref, d_ref, nm_ref, nv_ref = refs[nin + nr:]
            gv = g_ref[...]
            d, nm, nv = _adamw_math(w_ref[...], gv, m_ref[...], v_ref[...])
            go_ref[...] = gv
            d_ref[...] = d
            nm_ref[...] = nm
            nv_ref[...] = nv

    return _pallas(
        body, name=name, grid=grid, in_specs=[lhs_spec] + rhs_specs + [TOKEN] * len(extra) + ride_specs,
        out_specs=list(out_specs) + ride_specs, out_shape=list(out_shapes) + ride_shapes,
        compiler_params=_params(("parallel",) * (len(grid) - 1) + ("arbitrary",), blk),
    )(lhs, *rhs_list, *extra, *ride)


def _tk(T):
    return T


def _wgrad_cols(n, rhs_list, name, after=None, rider=None):
    T, D = n.shape
    Ns = rhs_list[0].shape[1] // N_CHIPS
    tk = _tk(T)
    nr = len(rhs_list)
    tm = D // 4 if rider is None else D // 8
    per = D // 2 // tm
    blk = _nbytes((tk, tm), BF16) + nr * (_nbytes((tk, Ns), BF16) + 2 * _nbytes((tm, Ns), F32))
    return _tn_call(
        name, (N_CHIPS, D // tm, T // tk), n, pl.BlockSpec((tk, tm), lambda j, m, k: (k, m)),
        rhs_list, [pl.BlockSpec((tk, Ns), lambda j, m, k: (k, j))] * nr,
        [jax.ShapeDtypeStruct((N_CHIPS, 2, D // 2, Ns), F32)] * nr,
        [pl.BlockSpec((None, None, tm, Ns), lambda j, m, k: (j, m // per, m % per, 0))] * nr, blk, after, rider)


def _wgrad_down(a, dfb, name, rider=None):
    T, F = a.shape
    D = dfb.shape[1]
    Fs = F // N_CHIPS
    tk = _tk(T)
    tn = D // 4 if rider is None else D // 8
    blk = _nbytes((tk, Fs), BF16) + _nbytes((tk, tn), BF16) + 2 * _nbytes((Fs, tn), F32)
    return _tn_call(
        name, (N_CHIPS, D // tn, T // tk), a, pl.BlockSpec((tk, Fs), lambda j, n, k: (k, j)),
        [dfb], [pl.BlockSpec((tk, tn), lambda j, n, k: (k, n))],
        [jax.ShapeDtypeStruct((N_CHIPS, 2, Fs // 2, D), F32)],
        [pl.BlockSpec((None, 2, Fs // 2, tn), lambda j, n, k: (j, 0, 0, n))], blk, None, rider)


def _wgrad_out(y, dmb):
    T, D = y.shape
    tk = _tk(T)
    tn = D // 2
    rows = D // (2 * N_CHIPS)
    blk = _nbytes((tk, D // 2), BF16) + _nbytes((tk, tn), BF16) + 2 * _nbytes((D // 2, tn), F32)
    return _tn_call(
        "wgrad_w_out", (2, D // tn, T // tk), y, pl.BlockSpec((tk, D // 2), lambda m, n, k: (k, m)),
        [dmb], [pl.BlockSpec((tk, tn), lambda m, n, k: (k, n))],
        [jax.ShapeDtypeStruct((N_CHIPS, 2, rows, D), F32)],
        [pl.BlockSpec((2, 2, rows, tn), lambda m, n, k: (m, 0, 0, n))], blk)[0]


def _row_masks(i, last):
    rows = i * TT + lax.broadcasted_iota(jnp.int32, (TT, 1), 0)
    prows = i * TT - HALO + lax.broadcasted_iota(jnp.int32, (HALO, 1), 0)
    return rows >= PAD, (prows >= PAD) & (i > 0), i < last


def _conv_inputs(u, up, mask_c, mask_p, zbuf, pbuf, C1):
    b, c, v, a, g = (u[:, k * C1:(k + 1) * C1] for k in range(5))
    cp, vp, ap, gp = (up[:, k * C1:(k + 1) * C1] for k in range(1, 5))
    sg = _sigmoid(g)
    pbuf[0:HALO, :] = jnp.where(mask_p, cp * vp, 0.0)
    pbuf[HALO:, :] = jnp.where(mask_c, c * v, 0.0)
    if zbuf is not None:
        zbuf[0:HALO, :] = jnp.where(mask_p, ap * _sigmoid(gp), 0.0)
        zbuf[HALO:, :] = jnp.where(mask_c, a * sg, 0.0)
    return b, c, v, a, sg


SUBLANES = 8
SHIFT_ROWS = TT + HALO - SUBLANES


def _shifted_scratch(C1):
    return pltpu.VMEM((SUBLANES - 1, SHIFT_ROWS, C1), F32)


def _fill_shifted(buf, sh):
    for r in range(1, SUBLANES):
        sh[r - 1] = buf[r:r + SHIFT_ROWS, :]


LANES = 128


def _window(buf, sh, lo, c0):
    if sh is None or lo % SUBLANES == 0:
        return buf[lo:lo + TT, c0:c0 + LANES]
    q, r = divmod(lo, SUBLANES)
    return sh[r - 1, q * SUBLANES:q * SUBLANES + TT, c0:c0 + LANES]


def _tap_sum(w_ref, buf, sh, starts):
    chunks = []
    for c0 in range(0, buf.shape[1], LANES):
        acc = None
        for k, lo in enumerate(starts):
            term = w_ref[k:k + 1, c0:c0 + LANES] * _window(buf, sh, lo, c0)
            acc = term if acc is None else acc + term
        chunks.append(acc)
    return jnp.concatenate(chunks, axis=1)


def _causal_conv(w_ref, buf, sh=None):
    K = w_ref.shape[0]
    return _tap_sum(w_ref, buf, sh, [HALO - (K - 1) + k for k in range(K)])


def _anticausal_conv(w_ref, buf, sh=None):
    K = w_ref.shape[0]
    return _tap_sum(w_ref, buf, sh, [K - 1 - k for k in range(K)])


def _conv_weight_sums(dw_ref, dy, buf, sh=None):
    K = dw_ref.shape[0]
    for c0 in range(0, buf.shape[1], LANES):
        dyc = dy[:, c0:c0 + LANES]
        for k in range(K):
            prod = dyc * _window(buf, sh, HALO - (K - 1) + k, c0)
            dw_ref[k:k + 1, c0:c0 + LANES] += jnp.sum(prod, axis=0, keepdims=True)


def _layernorm_stats(z1):
    mu = jnp.mean(z1, axis=-1, keepdims=True)
    zc = z1 - mu
    rs = lax.rsqrt(jnp.mean(zc * zc, axis=-1, keepdims=True) + EPS)
    return zc * rs, rs


def _mixer_specs(T, DIN, C1, ksc, kcf):
    cur = pl.BlockSpec((TT, DIN), lambda i: (i, 0))
    prev = pl.BlockSpec((HALO, DIN), lambda i: (jnp.maximum(i * (TT // HALO) - 1, 0), 0))
    full = lambda r: pl.BlockSpec((r, C1), lambda i: (0, 0))
    return cur, prev, [full(ksc), full(kcf), full(1), full(1), full(1)]


def _mix_conv_fwd(u, wsc, wcf, bcf, lg, lb):
    T, DIN = u.shape
    C1 = DIN // 5
    last = T // TT - 1

    def body(u_ref, up_ref, wsc_ref, wcf_ref, bcf_ref, lg_ref, lb_ref, y_ref, z1_ref, zbuf, pbuf, zsh):
        i = pl.program_id(0)
        mask_c, mask_p, _ = _row_masks(i, last)
        b, _, _, _, _ = _conv_inputs(u_ref[...], up_ref[...], mask_c, mask_p, zbuf, pbuf, C1)
        _fill_shifted(zbuf, zsh)
        cs = _causal_conv(wsc_ref, pbuf)
        z1 = _causal_conv(wcf_ref, zbuf, zsh) + bcf_ref[...]
        z1_ref[...] = z1
        zh, _ = _layernorm_stats(z1)
        ln = zh * lg_ref[...] + lb_ref[...]
        y_ref[:, 0:C1] = jnp.where(mask_c, b * cs, 0.0).astype(BF16)
        y_ref[:, C1:] = jnp.where(mask_c, jax.nn.silu(ln), 0.0).astype(BF16)

    cur, prev, small = _mixer_specs(T, DIN, C1, wsc.shape[0], wcf.shape[0])
    blk = _nbytes((TT + HALO, DIN), F32) + _nbytes((TT, 2 * C1), BF16) + 12 * _nbytes((TT + HALO, C1), F32)
    return _pallas(
        body, name="mix_conv_fwd", grid=(T // TT,),
        in_specs=[cur, prev] + small,
        out_specs=[pl.BlockSpec((TT, 2 * C1), lambda i: (i, 0)), pl.BlockSpec((TT, C1), lambda i: (i, 0))],
        out_shape=[jax.ShapeDtypeStruct((T, 2 * C1), BF16), jax.ShapeDtypeStruct((T, C1), F32)],
        scratch_shapes=[pltpu.VMEM((TT + HALO, C1), F32), pltpu.VMEM((TT + HALO, C1), F32), _shifted_scratch(C1)],
        compiler_params=_params(("arbitrary",), blk),
    )(u, u, wsc, wcf, bcf, lg, lb)


def _mix_conv_bwd1(u, z1, dy, wsc, lg, lb):
    T, DIN = u.shape
    C1 = DIN // 5
    last = T // TT - 1

    def body(u_ref, up_ref, z1_ref, dy_ref, wsc_ref, lg_ref, lb_ref,
             dz1_ref, dcs_ref, db_ref, dlg_ref, dlb_ref, dbcf_ref, pbuf):
        i = pl.program_id(0)
        mask_c, mask_p, _ = _row_masks(i, last)
        b, _, _, _, _ = _conv_inputs(u_ref[...], up_ref[...], mask_c, mask_p, None, pbuf, C1)
        cs = _causal_conv(wsc_ref, pbuf)
        zh, rs = _layernorm_stats(z1_ref[...])
        ln = zh * lg_ref[...] + lb_ref[...]
        dy = dy_ref[...]
        dysc = jnp.where(mask_c, dy[:, 0:C1], 0.0)
        dycf = jnp.where(mask_c, dy[:, C1:], 0.0)
        db_ref[...] = (dysc * cs).astype(BF16)
        dcs_ref[...] = dysc * b
        dl = dycf * _dsilu(ln, _sigmoid(ln))
        dzh = dl * lg_ref[...]
        dz1 = rs * (dzh - jnp.mean(dzh, axis=-1, keepdims=True) - zh * jnp.mean(dzh * zh, axis=-1, keepdims=True))
        dz1_ref[...] = dz1

        @pl.when(i == 0)
        def _():
            dlg_ref[...] = jnp.zeros_like(dlg_ref)
            dlb_ref[...] = jnp.zeros_like(dlb_ref)
            dbcf_ref[...] = jnp.zeros_like(dbcf_ref)

        dlg_ref[...] += jnp.sum(dl * zh, axis=0, keepdims=True)
        dlb_ref[...] += jnp.sum(dl, axis=0, keepdims=True)
        dbcf_ref[...] += jnp.sum(dz1, axis=0, keepdims=True)

    cur, prev, small = _mixer_specs(T, DIN, C1, wsc.shape[0], 1)
    tile = lambda: pl.BlockSpec((TT, C1), lambda i: (i, 0))
    vec = lambda: pl.BlockSpec((1, C1), lambda i: (0, 0))
    blk = _nbytes((TT + HALO, DIN), F32) + 5 * _nbytes((TT, C1), F32) + 12 * _nbytes((TT + HALO, C1), F32)
    return _pallas(
        body, name="mix_conv_bwd1", grid=(T // TT,),
        in_specs=[cur, prev, tile(), pl.BlockSpec((TT, 2 * C1), lambda i: (i, 0)), small[0], small[3], small[4]],
        out_specs=[tile(), tile(), tile(), vec(), vec(), vec()],
        out_shape=[jax.ShapeDtypeStruct((T, C1), F32), jax.ShapeDtypeStruct((T, C1), F32),
                   jax.ShapeDtypeStruct((T, C1), BF16)] + [jax.ShapeDtypeStruct((1, C1), F32)] * 3,
        scratch_shapes=[pltpu.VMEM((TT + HALO, C1), F32)],
        compiler_params=_params(("arbitrary",), blk),
    )(u, u, z1, dy, wsc, lg, lb)


def _mix_conv_bwd2(u, dz1, dcs, db, wsc, wcf):
    T, DIN = u.shape
    C1 = DIN // 5
    last = T // TT - 1
    ksc, kcf = wsc.shape[0], wcf.shape[0]

    def body(u_ref, up_ref, dz_ref, dzn_ref, dc_ref, dcn_ref, db_ref, wsc_ref, wcf_ref,
             du_ref, dbin_ref, dwsc_ref, dwcf_ref, zbuf, pbuf, dzbuf, dcbuf, zsh, dzsh):
        i = pl.program_id(0)
        mask_c, mask_p, has_next = _row_masks(i, last)
        _, c, v, a, sg = _conv_inputs(u_ref[...], up_ref[...], mask_c, mask_p, zbuf, pbuf, C1)
        dz1 = dz_ref[...]
        dcs = dc_ref[...]
        dzbuf[0:TT, :] = dz1
        dzbuf[TT:, :] = jnp.where(has_next, dzn_ref[...], 0.0)
        dcbuf[0:TT, :] = dcs
        dcbuf[TT:, :] = jnp.where(has_next, dcn_ref[...], 0.0)

        @pl.when(i == 0)
        def _():
            dbin_ref[...] = jnp.zeros_like(dbin_ref)
            dwsc_ref[...] = jnp.zeros_like(dwsc_ref)
            dwcf_ref[...] = jnp.zeros_like(dwcf_ref)

        _fill_shifted(zbuf, zsh)
        _fill_shifted(dzbuf, dzsh)
        _conv_weight_sums(dwcf_ref, dz1, zbuf, zsh)
        _conv_weight_sums(dwsc_ref, dcs, pbuf)
        dz0 = jnp.where(mask_c, _anticausal_conv(wcf_ref, dzbuf, dzsh), 0.0)
        dp = jnp.where(mask_c, _anticausal_conv(wsc_ref, dcbuf), 0.0)
        parts = (db_ref[...].astype(F32), dp * v, dp * c, dz0 * sg, dz0 * a * sg * (1.0 - sg))
        for k, part in enumerate(parts):
            du_ref[:, k * C1:(k + 1) * C1] = part.astype(BF16)
            dbin_ref[:, k * C1:(k + 1) * C1] += jnp.sum(part, axis=0, keepdims=True)

    cur, prev, small = _mixer_specs(T, DIN, C1, ksc, kcf)
    tile = lambda: pl.BlockSpec((TT, C1), lambda i: (i, 0))
    nxt = lambda: pl.BlockSpec((HALO, C1), lambda i: (jnp.minimum((i + 1) * (TT // HALO), T // HALO - 1), 0))
    blk = (_nbytes((TT + HALO, DIN), F32) + _nbytes((TT, DIN), BF16) + 5 * _nbytes((TT, C1), F32)
           + 16 * _nbytes((TT + HALO, C1), F32))
    buf = lambda: pltpu.VMEM((TT + HALO, C1), F32)
    return _pallas(
        body, name="mix_conv_bwd2", grid=(T // TT,),
        in_specs=[cur, prev, tile(), nxt(), tile(), nxt(), tile(), small[0], small[1]],
        out_specs=[pl.BlockSpec((TT, DIN), lambda i: (i, 0)), pl.BlockSpec((1, DIN), lambda i: (0, 0)),
                   pl.BlockSpec((ksc, C1), lambda i: (0, 0)), pl.BlockSpec((kcf, C1), lambda i: (0, 0))],
        out_shape=[jax.ShapeDtypeStruct((T, DIN), BF16), jax.ShapeDtypeStruct((1, DIN), F32),
                   jax.ShapeDtypeStruct((ksc, C1), F32), jax.ShapeDtypeStruct((kcf, C1), F32)],
        scratch_shapes=[buf(), buf(), buf(), buf(), _shifted_scratch(C1), _shifted_scratch(C1)],
        compiler_params=_params(("arbitrary",), blk),
    )(u, u, dz1, dz1, dcs, dcs, db, wsc, wcf)


def _place():
    x, y, c = lax.axis_index("x"), lax.axis_index("y"), lax.axis_index("c")
    chips = [(1 - x, y), (x, 1 - y), (1 - x, 1 - y)]
    return x, y, c, chips


ANY = pl.BlockSpec(memory_space=pl.ANY)


def _cast_own_block(place, w, name):
    R, C = w.shape
    tr = _row_tile(R // 2, 256, 16)
    nblk = R // 2 // tr

    def body(place_ref, w_ref, o_ref):
        o_ref[...] = w_ref[...].astype(BF16)

    return _pallas(
        body, name=name,
        grid_spec=pltpu.PrefetchScalarGridSpec(
            num_scalar_prefetch=1, grid=(2, nblk),
            in_specs=[pl.BlockSpec((tr, C), lambda h, i, p: (h * nblk + i, 0))],
            out_specs=pl.BlockSpec((None, None, tr, C), lambda h, i, p: (p[0], h, i, 0))),
        out_shape=jax.ShapeDtypeStruct((N_CHIPS, 2, R // 2, C), BF16),
        compiler_params=_params(("parallel", "parallel"), _nbytes((tr, C), F32) + _nbytes((tr, C), BF16)),
    )(place, w)


HBM = pl.BlockSpec(memory_space=pltpu.HBM)
SEM = pl.BlockSpec(memory_space=pltpu.SEMAPHORE)
EFFECT = pltpu.SideEffectType.DATAFLOW_SIDE_EFFECTING


def _gather_copies(refs, send, recv, rels=(0, 1, 2)):
    x, y, c, chips = _place()
    s = 2 * x + y
    n = len(rels)
    return [pltpu.make_async_remote_copy(src_ref=ref.at[s, c], dst_ref=ref.at[s, c], send_sem=send.at[n * w + k],
                                         recv_sem=recv.at[n * w + k], device_id=(*chips[r], c), device_id_type=MESH)
            for w, ref in enumerate(refs) for k, r in enumerate(rels)]


def _scatter_copies(refs, send, recv):
    x, y, c, chips = _place()
    nw = len(refs) // 2
    return [pltpu.make_async_remote_copy(src_ref=refs[w].at[2 * tx + ty], dst_ref=refs[nw + w].at[r],
                                         send_sem=send.at[3 * w + r], recv_sem=recv.at[3 * w + r],
                                         device_id=(tx, ty, c), device_id_type=MESH)
            for w in range(nw) for r, (tx, ty) in enumerate(chips)]


def _pair_copies(refs, send, recv):
    x, y, c, _ = _place()
    nw = len(refs) // 2
    return [pltpu.make_async_remote_copy(src_ref=refs[w].at[j, 1 - c], dst_ref=refs[nw + w].at[j],
                                         send_sem=send.at[N_CHIPS * w + j], recv_sem=recv.at[N_CHIPS * w + j],
                                         device_id=(x, y, 1 - c), device_id_type=MESH)
            for w in range(nw) for j in range(N_CHIPS)]


def _forward_copies(refs, send, recv, rels=(0, 1, 2)):
    x, y, c, chips = _place()
    n = len(rels)
    copies = []
    for w, ref in enumerate(refs):
        for k, r in enumerate(rels):
            tx, ty = chips[r]
            blk = ref.at[2 * tx + ty, c]
            copies.append(pltpu.make_async_remote_copy(src_ref=blk, dst_ref=blk, send_sem=send.at[n * w + k],
                                                       recv_sem=recv.at[n * w + k], device_id=(x, y, 1 - c),
                                                       device_id_type=MESH))
    return copies


def _half_copies(refs, send, recv):
    x, y, c, _ = _place()
    return [pltpu.make_async_remote_copy(src_ref=ref.at[c], dst_ref=ref.at[c], send_sem=send.at[w], recv_sem=recv.at[w],
                                         device_id=(x, y, 1 - c), device_id_type=MESH)
            for w, ref in enumerate(refs)]


def _start_copies(bufs, after, ncopies, make_copies, name):
    n = len(bufs)

    def body(*refs):
        in_refs, send, recv, token = refs[:n], refs[n + 1], refs[n + 2], refs[2 * n + 3]
        for cp in make_copies(in_refs, send, recv):
            cp.start()
        token[...] = jnp.zeros_like(token)

    outs = _pallas(
        body, name=name, in_specs=[HBM] * n + [ANY],
        out_specs=[SEM, SEM] + [HBM] * n + [pl.BlockSpec(memory_space=pltpu.VMEM)],
        out_shape=[pltpu.SemaphoreType.DMA((ncopies,)), pltpu.SemaphoreType.DMA((ncopies,))]
                  + [pltpu.HBM(b.shape, b.dtype) for b in bufs] + [jax.ShapeDtypeStruct((8, 128), F32)],
        input_output_aliases={k: 2 + k for k in range(n)},
        compiler_params=pltpu.CompilerParams(has_side_effects=EFFECT),
    )(*[pltpu.with_memory_space_constraint(b, pltpu.HBM) for b in bufs], after)
    return outs[0], outs[1], list(outs[2:2 + n]), outs[2 + n]


def _wait_copies(send, recv, bufs, after, make_copies, name):
    n = len(bufs)

    def body(*refs):
        in_refs, send_ref, recv_ref = refs[:n], refs[n], refs[n + 1]
        for cp in make_copies(in_refs, send_ref, recv_ref):
            cp.wait_send()
            cp.wait_recv()

    outs = _pallas(
        body, name=name, in_specs=[HBM] * n + [SEM, SEM, ANY], out_specs=[HBM] * n,
        out_shape=[pltpu.HBM(b.shape, b.dtype) for b in bufs],
        input_output_aliases={k: k for k in range(n)},
        compiler_params=pltpu.CompilerParams(has_side_effects=EFFECT),
    )(*bufs, send, recv, after)
    return list(outs)


def _forward_halves(bufs, name, rels=(0, 1, 2)):
    nw = len(bufs)
    n = len(rels)

    def body(*refs):
        o_refs = refs[nw:2 * nw]
        send, recv = refs[2 * nw:]
        x, y, c, chips = _place()
        sib = (x, y, 1 - c)
        copies = []
        for w in range(nw):
            for k, r in enumerate(rels):
                tx, ty = chips[r]
                ref = o_refs[w].at[2 * tx + ty, c]
                cp = pltpu.make_async_remote_copy(src_ref=ref, dst_ref=ref, send_sem=send.at[n * w + k],
                                                  recv_sem=recv.at[n * w + k], device_id=sib, device_id_type=MESH)
                cp.start()
                copies.append(cp)
        for w in range(nw):
            for k, r in enumerate(rels):
                tx, ty = chips[r]
                ref = o_refs[w].at[2 * tx + ty, 1 - c]
                pltpu.make_async_remote_copy(src_ref=ref, dst_ref=ref, send_sem=send.at[n * w + k],
                                             recv_sem=recv.at[n * w + k], device_id=sib, device_id_type=MESH).wait_recv()
        for cp in copies:
            cp.wait_send()

    return _pallas(
        body, name=name, in_specs=[ANY] * nw, out_specs=[ANY] * nw,
        out_shape=[jax.ShapeDtypeStruct(b.shape, b.dtype) for b in bufs],
        input_output_aliases={w: w for w in range(nw)},
        scratch_shapes=[pltpu.SemaphoreType.DMA((n * nw,)), pltpu.SemaphoreType.DMA((n * nw,))],
    )(*bufs)


def _share_small(v, reduce, name, after):
    R, C = v.shape

    def body(v_ref, after_ref, o_ref, *scratch):
        if reduce:
            all_ref, send, recv, lsem = scratch
        else:
            all_ref = o_ref
            send, recv, lsem = scratch
        x, y, c, _ = _place()
        me = 4 * x + 2 * y + c
        loc = pltpu.make_async_copy(v_ref, all_ref.at[me], lsem)
        loc.start()
        copies = []
        for k in range(1, N_DEV):
            kx, ky, kc = (k >> 2) & 1, (k >> 1) & 1, k & 1
            peer = (x ^ kx, y ^ ky, c ^ kc)
            cp = pltpu.make_async_remote_copy(src_ref=v_ref, dst_ref=all_ref.at[me], send_sem=send.at[k - 1],
                                              recv_sem=recv.at[k - 1], device_id=peer, device_id_type=MESH)
            cp.start()
            copies.append(cp)
        for k in range(1, N_DEV):
            kx, ky, kc = (k >> 2) & 1, (k >> 1) & 1, k & 1
            src = 4 * (x ^ kx) + 2 * (y ^ ky) + (c ^ kc)
            pltpu.make_async_remote_copy(src_ref=v_ref, dst_ref=all_ref.at[src], send_sem=send.at[k - 1],
                                         recv_sem=recv.at[k - 1], device_id=(x, y, c), device_id_type=MESH).wait_recv()
        for cp in copies:
            cp.wait_send()
        loc.wait()
        if reduce:
            total = all_ref[0]
            for d in range(1, N_DEV):
                total = total + all_ref[d]
            o_ref[...] = total

    vm = pl.BlockSpec(memory_space=pltpu.VMEM)
    sems = [pltpu.SemaphoreType.DMA((N_DEV - 1,)), pltpu.SemaphoreType.DMA((N_DEV - 1,)), pltpu.SemaphoreType.DMA]
    if reduce:
        out_shape = jax.ShapeDtypeStruct((R, C), F32)
        scratch = [pltpu.VMEM((N_DEV, R, C), F32)] + sems
    else:
        out_shape = jax.ShapeDtypeStruct((N_DEV, R, C), F32)
        scratch = sems
    return _pallas(
        body, name=name, in_specs=[vm, ANY], out_specs=vm, out_shape=out_shape, scratch_shapes=scratch,
        compiler_params=pltpu.CompilerParams(vmem_limit_bytes=int(min(4 * N_DEV * R * C * 4 + 2 ** 24, 2 ** 25 + 2 ** 24))),
    )(v, after)


def _pair_sum(place, g, rb, name):
    _, _, Rh, C = g.shape
    tr = _row_tile(Rh, 256, 16)

    def body(place_ref, g_ref, r_ref, q_ref):
        q_ref[...] = (g_ref[...] + r_ref[...]).astype(BF16)

    blk = 2 * _nbytes((tr, C), F32) + _nbytes((tr, C), BF16)
    return _pallas(
        body, name=name,
        grid_spec=pltpu.PrefetchScalarGridSpec(
            num_scalar_prefetch=1, grid=(N_CHIPS - 1, Rh // tr),
            in_specs=[pl.BlockSpec((None, None, tr, C), lambda j, i, p: (p[0] ^ (j + 1), p[1], i, 0)),
                      pl.BlockSpec((None, tr, C), lambda j, i, p: (p[0] ^ (j + 1), i, 0))],
            out_specs=pl.BlockSpec((None, tr, C), lambda j, i, p: (p[0] ^ (j + 1), i, 0))),
        out_shape=jax.ShapeDtypeStruct((N_CHIPS, Rh, C), BF16),
        compiler_params=_params(("parallel", "parallel"), blk),
    )(place, g, rb)


def _chip_sum(place, g, rb, rc, name):
    _, _, Rh, C = g.shape
    tr = _row_tile(Rh, 256, 16)

    def body(place_ref, g_ref, r_ref, rc_ref, o_ref):
        total = g_ref[...] + r_ref[...]
        for r in range(3):
            total = total + rc_ref[r].astype(F32)
        o_ref[...] = total

    blk = 3 * _nbytes((tr, C), F32) + 3 * _nbytes((tr, C), BF16)
    return _pallas(
        body, name=name,
        grid_spec=pltpu.PrefetchScalarGridSpec(
            num_scalar_prefetch=1, grid=(Rh // tr,),
            in_specs=[pl.BlockSpec((None, None, tr, C), lambda i, p: (p[0], p[1], i, 0)),
                      pl.BlockSpec((None, tr, C), lambda i, p: (p[0], i, 0)),
                      pl.BlockSpec((3, tr, C), lambda i, p: (0, i, 0))],
            out_specs=pl.BlockSpec((None, tr, C), lambda i, p: (p[1], i, 0))),
        out_shape=jax.ShapeDtypeStruct((2, Rh, C), F32),
        compiler_params=_params(("parallel",), blk),
    )(place, g, rb, rc)


def _adamw_math(w, g, m, v):
    m = ADAM_B1 * m + (1.0 - ADAM_B1) * g
    v = ADAM_B2 * v + (1.0 - ADAM_B2) * jnp.square(g)
    m_hat = m / (1.0 - ADAM_B1 ** ADAM_STEP)
    v_hat = v / (1.0 - ADAM_B2 ** ADAM_STEP)
    delta = -ADAM_LR * (m_hat / (jnp.sqrt(v_hat) + ADAM_EPS) + ADAM_WD * w)
    return delta, m, v


def _adamw(w, g, m, v, name):
    R, C = w.shape
    tr = _row_tile(R, 256)

    def body(w_ref, g_ref, m_ref, v_ref, go_ref, d_ref, nm_ref, nv_ref):
        gv = g_ref[...]
        d, nm, nv = _adamw_math(w_ref[...], gv, m_ref[...], v_ref[...])
        go_ref[...] = gv
        d_ref[...] = d
        nm_ref[...] = nm
        nv_ref[...] = nv

    spec = pl.BlockSpec((tr, C), lambda i: (i, 0))
    shp = jax.ShapeDtypeStruct((R, C), F32)
    return _pallas(
        body, name=name, grid=(R // tr,), in_specs=[spec] * 4, out_specs=[spec] * 4, out_shape=[shp] * 4,
        compiler_params=_params(("parallel",), 8 * _nbytes((tr, C), F32)),
    )(w, g, m, v)


def _adamw_small(ws, gs, ms, vs):
    n = len(ws)

    def body(*refs):
        for k in range(n):
            w_ref, g_ref, m_ref, v_ref = (refs[q * n + k] for q in range(4))
            d, nm, nv = _adamw_math(w_ref[...], g_ref[...], m_ref[...], v_ref[...])
            refs[4 * n + k][...] = d
            refs[5 * n + k][...] = nm
            refs[6 * n + k][...] = nv

    vm = pl.BlockSpec(memory_space=pltpu.VMEM)
    shapes = [jax.ShapeDtypeStruct(w.shape, F32) for w in ws]
    outs = _pallas(
        body, name="adamw_small", in_specs=[vm] * (4 * n), out_specs=[vm] * (3 * n), out_shape=shapes * 3,
    )(*ws, *gs, *ms, *vs)
    return outs[:n], outs[n:2 * n], outs[2 * n:]


def _pad_rows(a, rows):
    return jnp.pad(a, ((0, rows - a.shape[0]), (0, 0)))


def kernel(x, meta_tokens, ffn1_norm, ffn1_w_gate, ffn1_w_up, ffn1_w_down, mix_norm, w_in, b_in, conv_sc_w, conv_cf_w, conv_cf_b, ln_cf_g, ln_cf_b, w_out, ffn2_norm, ffn2_w_gate, ffn2_w_up, ffn2_w_down, final_norm, loss_target, m_meta_tokens, m_ffn1_norm, m_ffn1_w_gate, m_ffn1_w_up, m_ffn1_w_down, m_mix_norm, m_w_in, m_b_in, m_conv_sc_w, m_conv_cf_w, m_conv_cf_b, m_ln_cf_g, m_ln_cf_b, m_w_out, m_ffn2_norm, m_ffn2_w_gate, m_ffn2_w_up, m_ffn2_w_down, m_final_norm, v_meta_tokens, v_ffn1_norm, v_ffn1_w_gate, v_ffn1_w_up, v_ffn1_w_down, v_mix_norm, v_w_in, v_b_in, v_conv_sc_w, v_conv_cf_w, v_conv_cf_b, v_ln_cf_g, v_ln_cf_b, v_w_out, v_ffn2_norm, v_ffn2_w_gate, v_ffn2_w_up, v_ffn2_w_down, v_final_norm):
    xi, yi, ci = lax.axis_index("x"), lax.axis_index("y"), lax.axis_index("c")
    chip = 2 * xi + yi
    place = jnp.stack([chip, ci]).astype(jnp.int32)

    x2 = x[0]
    tgt = loss_target[0]
    S, D = x2.shape
    C1 = D // 2
    cs = conv_sc_w.shape[2]
    ksc, kcf = conv_sc_w.shape[1], conv_cf_w.shape[1]
    ms = meta_tokens.shape[1]

    big = {"ffn1_w_gate": ffn1_w_gate, "ffn1_w_up": ffn1_w_up, "ffn1_w_down": ffn1_w_down, "w_in": w_in, "w_out": w_out,
           "ffn2_w_gate": ffn2_w_gate, "ffn2_w_up": ffn2_w_up, "ffn2_w_down": ffn2_w_down}
    big_m = {"ffn1_w_gate": m_ffn1_w_gate, "ffn1_w_up": m_ffn1_w_up, "ffn1_w_down": m_ffn1_w_down, "w_in": m_w_in,
             "w_out": m_w_out, "ffn2_w_gate": m_ffn2_w_gate, "ffn2_w_up": m_ffn2_w_up, "ffn2_w_down": m_ffn2_w_down}
    big_v = {"ffn1_w_gate": v_ffn1_w_gate, "ffn1_w_up": v_ffn1_w_up, "ffn1_w_down": v_ffn1_w_down, "w_in": v_w_in,
             "w_out": v_w_out, "ffn2_w_gate": v_ffn2_w_gate, "ffn2_w_up": v_ffn2_w_up, "ffn2_w_down": v_ffn2_w_down}
    buf = {nm: _cast_own_block(place, w[0], "cast_" + nm) for nm, w in big.items()}
    whole_weight = lambda g: g.reshape(N_CHIPS, 2 * g.shape[2], g.shape[3])
    corner = lambda a: a.reshape(-1, a.shape[-1])[:8, :128]

    NEAR, FAR = (0, 1), (2,)
    groups = {"ffn1_near": (["ffn1_w_gate", "ffn1_w_up", "ffn1_w_down"], NEAR),
              "ffn1_far": (["ffn1_w_gate", "ffn1_w_up", "ffn1_w_down"], FAR),
              "mix": (["w_in", "w_out"], NEAR + FAR),
              "ffn2_up": (["ffn2_w_gate", "ffn2_w_up"], NEAR + FAR),
              "ffn2_down": (["ffn2_w_down"], NEAR + FAR)}
    started = {}

    def start(tag, after):
        nms, rels = groups[tag]
        copies = functools.partial(_gather_copies, rels=rels)
        send, recv, thru, token = _start_copies([buf[nm] for nm in nms], after, len(rels) * len(nms), copies,
                                                "gather_start_" + tag)
        for nm, b in zip(nms, thru):
            buf[nm] = b
        started[tag] = (send, recv, copies)
        return token

    def arrive(tag, after, then=None):
        nms, rels = groups[tag]
        send, recv, copies = started[tag]
        got = _wait_copies(send, recv, [buf[nm] for nm in nms], corner(after), copies, "gather_wait_" + tag)
        for nm, b in zip(nms, got):
            buf[nm] = b
        if then is not None:
            start(then, corner(got[0]))
        for nm, b in zip(nms, _forward_halves([buf[nm] for nm in nms], "gather_forward_" + tag, rels)):
            buf[nm] = b

    def passing(tag, after):
        nms, rels = groups[tag]
        send, recv, copies = started[tag]
        got = _wait_copies(send, recv, [buf[nm] for nm in nms], corner(after), copies, "gather_wait_" + tag)
        copies = functools.partial(_forward_copies, rels=rels)
        send, recv, thru, token = _start_copies(got, corner(got[0]), len(rels) * len(nms), copies,
                                                "gather_pass_" + tag)
        for nm, b in zip(nms, thru):
            buf[nm] = b
        started[tag] = (send, recv, copies)
        return token

    def passed(tag, after):
        nms, _ = groups[tag]
        send, recv, copies = started[tag]
        for nm, b in zip(nms, _wait_copies(send, recv, [buf[nm] for nm in nms], corner(after), copies,
                                           "gather_passed_" + tag)):
            buf[nm] = b

    tokens = lambda *arrays: jnp.concatenate([corner(a).astype(F32) for a in arrays], axis=0)
    token = start("ffn1_near", corner(buf["ffn1_w_down"]))

    assert ksc <= 8 and kcf <= 32 and cs <= ms
    pack = jnp.concatenate([
        meta_tokens,
        jnp.pad(conv_sc_w[0], ((0, 8 - ksc), (0, ms - cs))),
        jnp.pad(conv_cf_w[0], ((0, 32 - kcf), (0, ms - cs)))], axis=0)
    everyone = _share_small(pack, False, "share_params", token)[0::2]
    meta_full = jnp.transpose(everyone[:, :N_META, :], (1, 0, 2)).reshape(N_META, D)
    wsc_full = jnp.transpose(everyone[:, N_META:N_META + ksc, :cs], (1, 0, 2)).reshape(ksc, C1)
    wcf_full = jnp.transpose(everyone[:, N_META + 8:N_META + 8 + kcf, :cs], (1, 0, 2)).reshape(kcf, C1)

    ffn1 = lambda: [whole_weight(buf[nm]) for nm in ["ffn1_w_gate", "ffn1_w_up", "ffn1_w_down"]]
    own = chip[None].astype(jnp.int32)
    near = jnp.stack([chip ^ 2, chip ^ 1]).astype(jnp.int32)
    far = (chip ^ 3)[None].astype(jnp.int32)
    all_chips = jnp.arange(N_CHIPS, dtype=jnp.int32)

    hs0, n1 = _embed_rms(x2, meta_full, ffn1_norm)
    wg1, wu1, wd1 = ffn1()
    gua = _ffn_up(n1, wg1, wu1, own, None, token, "ffn1_up_own")
    hs1 = _ffn_down(gua[2], wd1, hs0, own, "ffn1_down_own")
    later = [buf[nm] for nm in ["w_in", "w_out", "ffn2_w_gate", "ffn2_w_up", "ffn2_w_down"]]
    arrive("ffn1_near", tokens(hs1, *later), "ffn1_far")
    wg1, wu1, wd1 = ffn1()
    gua = _ffn_up(n1, wg1, wu1, near, gua, token, "ffn1_up_near")
    tok = start("mix", corner(gua[2]))
    tok = passing("ffn1_far", tok)
    hs1 = _ffn_down(gua[2], ffn1()[2], hs1, near, "ffn1_down_near", tok)
    passed("ffn1_far", hs1)
    wg1, wu1, wd1 = ffn1()
    g1, u1, a1 = _ffn_up(n1, wg1, wu1, far, gua, token, "ffn1_up_far")
    tok = passing("mix", a1)
    hs1 = _ffn_down(a1, wd1, hs1, far, "ffn1_down_far", tok)
    F = N_CHIPS * wd1.shape[1]
    tok = start("ffn2_up", corner(hs1))
    passed("mix", tok)
    win, wout = whole_weight(buf["w_in"]), whole_weight(buf["w_out"])
    n2 = _rms(hs1, mix_norm, "rms_mix")
    u = _mix_in(n2, win, b_in)
    y, z1 = _mix_conv_fwd(u, wsc_full, wcf_full, conv_cf_b, ln_cf_g, ln_cf_b)
    tok = start("ffn2_down", corner(y))
    tok = passing("ffn2_up", tok)
    hs2 = _mix_out(y, wout.reshape(D, D), hs1, tok)
    passed("ffn2_up", hs2)
    wg2, wu2 = whole_weight(buf["ffn2_w_gate"]), whole_weight(buf["ffn2_w_up"])
    n3 = _rms(hs2, ffn2_norm, "rms_ffn2")
    g2, u2, a2 = _ffn_up(n3, wg2, wu2, all_chips, None, token, "ffn2_up")
    passed("ffn2_down", passing("ffn2_down", a2))
    wd2 = whole_weight(buf["ffn2_w_down"])
    hs3 = _ffn_down_whole(a2, wd2.reshape(F, D), hs2, "ffn2_down")
    token_ffn2 = token

    def pair_start(group, after, tag):
        gs = [g for _, g in group]
        lands = [lax.empty((N_CHIPS,) + g.shape[2:], F32) for g in gs]
        send, recv, thru, token = _start_copies(gs + lands, after, N_CHIPS * len(gs), _pair_copies,
                                                "pair_start_" + tag)
        return (group, send, recv, thru, tag), token

    def scatter_start(state, after):
        group, send, recv, thru, tag = state
        thru = _wait_copies(send, recv, thru, corner(after), _pair_copies, "pair_wait_" + tag)
        gs, sib = thru[:len(group)], thru[len(group):]
        sums = [_pair_sum(place, g, rb, "pair_sum_" + nm) for (nm, _), g, rb in zip(group, gs, sib)]
        lands = [lax.empty((3,) + q.shape[1:], BF16) for q in sums]
        send, recv, thru, token = _start_copies(sums + lands, corner(sums[-1]), 3 * len(gs), _scatter_copies,
                                                "scatter_start_" + tag)
        return ([(nm, g) for (nm, _), g in zip(group, gs)], sib, send, recv, thru, tag), token

    def finish_sum(state, after):
        group, sib, send, recv, thru, tag = state
        lands = _wait_copies(send, recv, thru, corner(after), _scatter_copies, "scatter_wait_" + tag)[len(group):]
        mine = [_chip_sum(place, g, rb, rc, "chip_sum_" + nm) for (nm, g), rb, rc in zip(group, sib, lands)]
        send, recv, thru, token = _start_copies(mine, corner(mine[-1]), len(mine), _half_copies, "half_start_" + tag)
        return (group, send, recv, thru, tag), token

    def swapped(state, after):
        group, send, recv, thru, tag = state
        whole = _wait_copies(send, recv, thru, corner(after), _half_copies, "half_wait_" + tag)
        return {nm: g.reshape(big[nm].shape[1:]) for (nm, _), g in zip(group, whole)}

    def finish_adam(state, after):
        out = {}
        for nm, g in swapped(state, after).items():
            g_out, d, new_m, new_v = _adamw(big[nm][0], g, big_m[nm][0], big_v[nm][0], "adamw_" + nm)
            out[nm] = (g_out[None], d[None], new_m[None], new_v[None])
        return out

    def riding(nm, g, steps):
        R, C = g.shape
        fold = 1 if R % (steps * SUBLANES) == 0 else 2
        assert (fold * R) % (steps * SUBLANES) == 0 and C % (fold * LANES) == 0, (nm, R, C, steps)
        return tuple(a.reshape(fold * R, C // fold) for a in (big[nm][0], g, big_m[nm][0], big_v[nm][0]))

    def ridden(nm, results):
        return tuple(r.reshape(big[nm].shape) for r in results)

    dhs3, df2, loss_row, d_final = _final_loss(hs3, final_norm.reshape(1, D), tgt)

    dg2, du2 = _ffn_bwd_act(df2, wd2.reshape(F, D), g2, u2, token_ffn2, "ffn2_bwd_act")
    gw_d2 = _wgrad_down(a2, df2, "wgrad_ffn2_down")[0]
    gw_g2 = _wgrad_cols(n3, [dg2], "wgrad_ffn2_gate")[0]
    gw_u2 = _wgrad_cols(n3, [du2], "wgrad_ffn2_up")[0]
    pair_ffn2, token = pair_start([("ffn2_w_gate", gw_g2), ("ffn2_w_up", gw_u2), ("ffn2_w_down", gw_d2)],
                                  corner(gw_u2), "ffn2")
    dn3 = _nt_panel([dg2, du2], [wg2, wu2], token, "ffn2_bwd_in")
    red_ffn2, token = scatter_start(pair_ffn2, dn3)
    dhs2, dm, d_ffn2 = _rms_bwd(dn3, hs2, ffn2_norm, dhs3, 1.0, "rms_bwd_ffn2")

    dy = _nt_panel([dm], [wout.reshape(1, D, D)], token, "mix_bwd_out")
    gw_out = _wgrad_out(y, dm)
    dz1, dcs, db, d_lg, d_lb, d_bcf = _mix_conv_bwd1(u, z1, dy, wsc_full, ln_cf_g, ln_cf_b)
    du, d_bin, d_wsc, d_wcf = _mix_conv_bwd2(u, dz1, dcs, db, wsc_full, wcf_full)
    gw_in = _wgrad_cols(n2, [du], "wgrad_w_in")[0]
    pair_mix, token = pair_start([("w_in", gw_in), ("w_out", gw_out)], corner(gw_in), "mix")
    dn2 = _nt_panel([du], [win], token, "mix_bwd_in")
    red_mix, token = scatter_start(pair_mix, dn2)
    dhs1, df1, d_mix = _rms_bwd(dn2, hs1, mix_norm, dhs2, FFN_RES_SCALE, "rms_bwd_mix")

    half_ffn2, tok = finish_sum(red_ffn2, token)
    dg1, du1 = _ffn_bwd_act(df1, wd1.reshape(F, D), g1, u1, tok, "ffn1_bwd_act")
    g_ffn2 = swapped(half_ffn2, dg1)
    steps = N_CHIPS * 8
    gw_d1, *upd = _wgrad_down(a1, df1, "wgrad_ffn1_down", riding("ffn2_w_down", g_ffn2["ffn2_w_down"], steps))
    big_out = {"ffn2_w_down": ridden("ffn2_w_down", upd)}
    gw_g1, *upd = _wgrad_cols(n1, [dg1], "wgrad_ffn1_gate", None, riding("ffn2_w_gate", g_ffn2["ffn2_w_gate"], steps))
    big_out["ffn2_w_gate"] = ridden("ffn2_w_gate", upd)
    pair_ffn1a, token = pair_start([("ffn1_w_down", gw_d1), ("ffn1_w_gate", gw_g1)], corner(gw_g1), "ffn1a")
    gw_u1, *upd = _wgrad_cols(n1, [du1], "wgrad_ffn1_up", token, riding("ffn2_w_up", g_ffn2["ffn2_w_up"], steps))
    big_out["ffn2_w_up"] = ridden("ffn2_w_up", upd)
    red_ffn1a, token = scatter_start(pair_ffn1a, gw_u1)
    pair_ffn1b, token = pair_start([("ffn1_w_up", gw_u1)], token, "ffn1b")
    dn1 = _nt_panel([dg1, du1], [wg1, wu1], token, "ffn1_bwd_in")
    red_ffn1b, token = scatter_start(pair_ffn1b, dn1)
    grad_x, d_meta, d_ffn1 = _rms_bwd_first(dn1, hs0, ffn1_norm, dhs1, token)

    half_mix, tok = finish_sum(red_mix, grad_x)
    half_ffn1a, tok = finish_sum(red_ffn1a, tok)
    big_out.update(finish_adam(half_mix, tok))
    half_ffn1b, tok = finish_sum(red_ffn1b, big_out["w_out"][1])
    big_out.update(finish_adam(half_ffn1a, tok))
    big_out.update(finish_adam(half_ffn1b, big_out["ffn1_w_gate"][1]))

    W = C1
    rows = lambda a: a.reshape(-1, W)
    parts = [rows(d_ffn1), rows(d_mix), rows(d_ffn2), rows(d_final), rows(d_bin), d_bcf, d_lg, d_lb,
             d_wsc, d_wcf, rows(d_meta), jnp.broadcast_to(loss_row[:, :1], (1, W))]
    sizes = [p.shape[0] for p in parts]
    total_rows = sum(sizes)
    packed = _pad_rows(jnp.concatenate(parts, axis=0), -(-total_rows // 8) * 8)
    summed = _share_small(packed, True, "sum_small", big_out["ffn1_w_up"][1])
    offs = [0]
    for n in sizes:
        offs.append(offs[-1] + n)
    piece = lambda k: summed[offs[k]:offs[k + 1]]
    loss = piece(11)[0, 0]
    g_ffn1, g_mix, g_ffn2 = (piece(k).reshape(1, D) for k in range(3))
    g_final = piece(3).reshape(1, D)
    g_bin = piece(4).reshape(1, -1)
    g_bcf, g_lg, g_lb = piece(5), piece(6), piece(7)
    g_wsc = lax.dynamic_slice_in_dim(piece(8), chip * cs, cs, axis=1)
    g_wcf = lax.dynamic_slice_in_dim(piece(9), chip * cs, cs, axis=1)
    g_meta = lax.dynamic_slice_in_dim(piece(10).reshape(N_META, D), chip * ms, ms, axis=1)

    small_names = ["meta_tokens", "ffn1_norm", "mix_norm", "b_in", "conv_sc_w", "conv_cf_w", "conv_cf_b", "ln_cf_g",
                   "ln_cf_b", "ffn2_norm", "final_norm"]
    small_w = [meta_tokens, ffn1_norm, mix_norm, b_in, conv_sc_w[0], conv_cf_w[0], conv_cf_b, ln_cf_g, ln_cf_b,
               ffn2_norm, final_norm.reshape(1, D)]
    small_g = [g_meta, g_ffn1, g_mix, g_bin, g_wsc, g_wcf, g_bcf, g_lg, g_lb, g_ffn2, g_final]
    small_m = [m_meta_tokens, m_ffn1_norm, m_mix_norm, m_b_in, m_conv_sc_w[0], m_conv_cf_w[0], m_conv_cf_b, m_ln_cf_g,
               m_ln_cf_b, m_ffn2_norm, m_final_norm.reshape(1, D)]
    small_v = [v_meta_tokens, v_ffn1_norm, v_mix_norm, v_b_in, v_conv_sc_w[0], v_conv_cf_w[0], v_conv_cf_b, v_ln_cf_g,
               v_ln_cf_b, v_ffn2_norm, v_final_norm.reshape(1, D)]
    s_d, s_m, s_v = _adamw_small(small_w, small_g, small_m, small_v)
    shapes = {"conv_sc_w": conv_sc_w.shape, "conv_cf_w": conv_cf_w.shape, "final_norm": final_norm.shape}
    small_out = {}
    for nm, g, d, m, v in zip(small_names, small_g, s_d, s_m, s_v):
        shp = shapes.get(nm, g.shape)
        small_out[nm] = tuple(t.reshape(shp) for t in (g, d, m, v))

    order = ["meta_tokens", "ffn1_norm", "ffn1_w_gate", "ffn1_w_up", "ffn1_w_down", "mix_norm", "w_in", "b_in",
             "conv_sc_w", "conv_cf_w", "conv_cf_b", "ln_cf_g", "ln_cf_b", "w_out", "ffn2_norm", "ffn2_w_gate",
             "ffn2_w_up", "ffn2_w_down", "final_norm"]
    res = {**big_out, **small_out}
    outs = [loss, grad_x[None]]
    for q in range(4):
        outs.extend(res[nm][q] for nm in order)
    return tuple(outs)
```

```python
import functools

import jax
import jax.numpy as jnp
from jax import lax
from jax.experimental import pallas as pl
from jax.experimental.pallas import tpu as pltpu

F32 = jnp.float32
BF16 = jnp.bfloat16
MESH = pl.DeviceIdType.MESH

N_META = 16
TT = 128
PAD = TT - N_META
HALO = 32
EPS = 1e-6
FFN_RES_SCALE = 0.5
N_CHIPS = 4
N_DEV = 8

ADAM_LR = 0.001
ADAM_B1 = 0.9
ADAM_B2 = 0.999
ADAM_EPS = 1e-08
ADAM_WD = 0.01
ADAM_STEP = 10

V7X_VMEM_BYTES = 64 * 2 ** 20
NT_DIMS = (((1,), (1,)), ((), ()))
TN_DIMS = (((0,), (0,)), ((), ()))


def _params(semantics, block_bytes):
    limit = min(2 * block_bytes + 16 * 2 ** 20, V7X_VMEM_BYTES - 6 * 2 ** 20)
    return pltpu.CompilerParams(dimension_semantics=semantics, vmem_limit_bytes=int(limit))


def _pallas(body, out_shape, **kw):
    if "grid" not in kw and "grid_spec" not in kw:
        return pl.pallas_call(body, out_shape=out_shape, **kw)
    big = lambda shape, dtype: jnp.issubdtype(dtype, jnp.floating) and len(shape) >= 2
    pin_out = lambda s: pltpu.HBM(s.shape, s.dtype) if big(s.shape, s.dtype) else s
    single = not isinstance(out_shape, (list, tuple))
    shapes = pin_out(out_shape) if single else [pin_out(s) for s in out_shape]
    call = pl.pallas_call(body, out_shape=shapes, **kw)
    pin = lambda a: pltpu.with_memory_space_constraint(a, pltpu.HBM) if big(a.shape, a.dtype) else a
    return lambda *operands: call(*[pin(a) for a in operands])


def _nbytes(shape, dtype):
    n = 1
    for d in shape:
        if d is not None:
            n *= d
    return n * jnp.dtype(dtype).itemsize


def _row_tile(rows, target, mult=8):
    best = None
    for t in range(mult, min(rows, target) + 1, mult):
        if rows % t == 0:
            best = t
    assert best is not None, (rows, target, mult)
    return best


def _sigmoid(v):
    return jax.nn.sigmoid(v)


def _dsilu(v, s):
    return s * (1.0 + v * (1.0 - s))


def _embed_rms(x2, meta, gain):
    S, D = x2.shape
    T = S + TT

    def body(x_ref, meta_ref, g_ref, hs_ref, n_ref):
        i = pl.program_id(0)

        @pl.when(i == 0)
        def _():
            hs_ref[...] = jnp.zeros_like(hs_ref)
            hs_ref[PAD:, :] = meta_ref[...]

        @pl.when(i > 0)
        def _():
            hs_ref[...] = x_ref[...]

        h = hs_ref[...]
        r = lax.rsqrt(jnp.mean(h * h, axis=-1, keepdims=True) + EPS)
        n_ref[...] = ((h * r) * g_ref[...]).astype(BF16)

    blk = _nbytes((TT, D), F32) * 2 + _nbytes((TT, D), BF16)
    return _pallas(
        body, name="embed_rms", grid=(T // TT,),
        in_specs=[pl.BlockSpec((TT, D), lambda i: (jnp.maximum(i - 1, 0), 0)),
                  pl.BlockSpec((N_META, D), lambda i: (0, 0)),
                  pl.BlockSpec((1, D), lambda i: (0, 0))],
        out_specs=[pl.BlockSpec((TT, D), lambda i: (i, 0)), pl.BlockSpec((TT, D), lambda i: (i, 0))],
        out_shape=[jax.ShapeDtypeStruct((T, D), F32), jax.ShapeDtypeStruct((T, D), BF16)],
        compiler_params=_params(("parallel",), blk),
    )(x2, meta, gain)


def _rms(hs, gain, name):
    T, D = hs.shape
    te = _row_tile(T, 384)

    def body(h_ref, g_ref, n_ref):
        h = h_ref[...]
        r = lax.rsqrt(jnp.mean(h * h, axis=-1, keepdims=True) + EPS)
        n_ref[...] = ((h * r) * g_ref[...]).astype(BF16)

    blk = _nbytes((te, D), F32) + _nbytes((te, D), BF16)
    return _pallas(
        body, name=name, grid=(T // te,),
        in_specs=[pl.BlockSpec((te, D), lambda i: (i, 0)), pl.BlockSpec((1, D), lambda i: (0, 0))],
        out_specs=pl.BlockSpec((te, D), lambda i: (i, 0)),
        out_shape=jax.ShapeDtypeStruct((T, D), BF16),
        compiler_params=_params(("parallel",), blk),
    )(hs, gain)


def _rms_bwd_math(dn, h, g):
    r = lax.rsqrt(jnp.mean(h * h, axis=-1, keepdims=True) + EPS)
    xh = h * r
    dgain = jnp.sum(dn * xh, axis=0, keepdims=True)
    dxh = dn * g
    dh = r * (dxh - xh * jnp.mean(dxh * xh, axis=-1, keepdims=True))
    return dh, dgain


def _rms_bwd(dn, hs, gain, dres, scale, name):
    T, D = hs.shape
    te = _row_tile(T, 384)

    def body(dn_ref, h_ref, g_ref, dres_ref, dhs_ref, dhb_ref, dg_ref):
        dh, dgain = _rms_bwd_math(dn_ref[...], h_ref[...], g_ref[...])
        d = dres_ref[...] + dh
        dhs_ref[...] = d
        dhb_ref[...] = (scale * d).astype(BF16)

        @pl.when(pl.program_id(0) == 0)
        def _():
            dg_ref[...] = jnp.zeros_like(dg_ref)

        dg_ref[...] += dgain

    blk = _nbytes((te, D), F32) * 4 + _nbytes((te, D), BF16)
    row = lambda i: (i, 0)
    return _pallas(
        body, name=name, grid=(T // te,),
        in_specs=[pl.BlockSpec((te, D), row), pl.BlockSpec((te, D), row), pl.BlockSpec((1, D), lambda i: (0, 0)),
                  pl.BlockSpec((te, D), row)],
        out_specs=[pl.BlockSpec((te, D), row), pl.BlockSpec((te, D), row), pl.BlockSpec((1, D), lambda i: (0, 0))],
        out_shape=[jax.ShapeDtypeStruct((T, D), F32), jax.ShapeDtypeStruct((T, D), BF16),
                   jax.ShapeDtypeStruct((1, D), F32)],
        compiler_params=_params(("arbitrary",), blk),
    )(dn, hs, gain, dres)


def _rms_bwd_first(dn, hs, gain, dres, after):
    T, D = hs.shape
    S = T - TT

    def body(dn_ref, h_ref, g_ref, dres_ref, after_ref, gx_ref, gm_ref, dg_ref):
        i = pl.program_id(0)
        dh, dgain = _rms_bwd_math(dn_ref[...], h_ref[...], g_ref[...])
        d = dres_ref[...] + dh

        @pl.when(i == 0)
        def _():
            dg_ref[...] = jnp.zeros_like(dg_ref)
            gm_ref[...] = d[PAD:, :]

        @pl.when(i > 0)
        def _():
            gx_ref[...] = d

        dg_ref[...] += dgain

    blk = _nbytes((TT, D), F32) * 4
    row = lambda i: (i, 0)
    return _pallas(
        body, name="rms_bwd_ffn1", grid=(T // TT,),
        in_specs=[pl.BlockSpec((TT, D), row), pl.BlockSpec((TT, D), row), pl.BlockSpec((1, D), lambda i: (0, 0)),
                  pl.BlockSpec((TT, D), row), TOKEN],
        out_specs=[pl.BlockSpec((TT, D), lambda i: (jnp.maximum(i - 1, 0), 0)),
                   pl.BlockSpec((N_META, D), lambda i: (0, 0)), pl.BlockSpec((1, D), lambda i: (0, 0))],
        out_shape=[jax.ShapeDtypeStruct((S, D), F32), jax.ShapeDtypeStruct((N_META, D), F32),
                   jax.ShapeDtypeStruct((1, D), F32)],
        compiler_params=_params(("arbitrary",), blk),
    )(dn, hs, gain, dres, after)


def _final_loss(hs, gain, tgt):
    T, D = hs.shape

    def body(h_ref, g_ref, t_ref, dhs_ref, dhb_ref, loss_ref, dg_ref):
        i = pl.program_id(0)
        h = h_ref[...]
        g = g_ref[...]
        r = lax.rsqrt(jnp.mean(h * h, axis=-1, keepdims=True) + EPS)
        xh = h * r
        e = jnp.where(i > 0, xh * g - t_ref[...], 0.0)
        tile_loss = jnp.sum(jnp.sum(e * e, axis=1, keepdims=True), axis=0, keepdims=True) * (0.5 / D)
        dout = e * (1.0 / D)
        dgain = jnp.sum(dout * xh, axis=0, keepdims=True)
        dxh = dout * g
        d = r * (dxh - xh * jnp.mean(dxh * xh, axis=-1, keepdims=True))
        dhs_ref[...] = d
        dhb_ref[...] = (FFN_RES_SCALE * d).astype(BF16)

        @pl.when(i == 0)
        def _():
            loss_ref[...] = jnp.zeros_like(loss_ref)
            dg_ref[...] = jnp.zeros_like(dg_ref)

        loss_ref[...] += jnp.broadcast_to(tile_loss, loss_ref.shape)
        dg_ref[...] += dgain

    blk = _nbytes((TT, D), F32) * 3 + _nbytes((TT, D), BF16)
    row = lambda i: (i, 0)
    return _pallas(
        body, name="final_loss", grid=(T // TT,),
        in_specs=[pl.BlockSpec((TT, D), row), pl.BlockSpec((1, D), lambda i: (0, 0)),
                  pl.BlockSpec((TT, D), lambda i: (jnp.maximum(i - 1, 0), 0))],
        out_specs=[pl.BlockSpec((TT, D), row), pl.BlockSpec((TT, D), row),
                   pl.BlockSpec((1, 128), lambda i: (0, 0)), pl.BlockSpec((1, D), lambda i: (0, 0))],
        out_shape=[jax.ShapeDtypeStruct((T, D), F32), jax.ShapeDtypeStruct((T, D), BF16),
                   jax.ShapeDtypeStruct((1, 128), F32), jax.ShapeDtypeStruct((1, D), F32)],
        compiler_params=_params(("arbitrary",), blk),
    )(hs, gain, tgt)


MXU_COLS = 256


def _tm(T):
    return _row_tile(T, 704, 16)


def _col_chunks(n):
    return [(c, min(MXU_COLS, n - c)) for c in range(0, n, MXU_COLS)]


TOKEN = pl.BlockSpec((8, 128), lambda *_: (0, 0))


def _ffn_up(n, wg, wu, shards, prev, after, name):
    T, D = n.shape
    Fs = wg.shape[2]
    tm = _tm(T)
    nprev = 0 if prev is None else 3

    def body(shards_ref, n_ref, wg_ref, wu_ref, after_ref, *refs):
        g_ref, u_ref, a_ref = refs[nprev:]
        nn = n_ref[...]
        for c0, cw in _col_chunks(Fs):
            if 2 * cw == MXU_COLS:
                both = jnp.concatenate([wg_ref[:, c0:c0 + cw], wu_ref[:, c0:c0 + cw]], axis=1)
                gu = jnp.dot(nn, both, preferred_element_type=F32)
                g, u = gu[:, :cw], gu[:, cw:]
            else:
                g = jnp.dot(nn, wg_ref[:, c0:c0 + cw], preferred_element_type=F32)
                u = jnp.dot(nn, wu_ref[:, c0:c0 + cw], preferred_element_type=F32)
            g_ref[:, c0:c0 + cw] = g.astype(BF16)
            u_ref[:, c0:c0 + cw] = u.astype(BF16)
            a_ref[:, c0:c0 + cw] = (jax.nn.silu(g) * u).astype(BF16)

    blk = _nbytes((tm, D), BF16) + 2 * _nbytes((D, Fs), BF16) + 3 * _nbytes((tm, Fs), BF16)
    out = pl.BlockSpec((tm, Fs), lambda j, i, p: (i, p[j]))
    shp = jax.ShapeDtypeStruct((T, N_CHIPS * Fs), BF16)
    return _pallas(
        body, name=name,
        grid_spec=pltpu.PrefetchScalarGridSpec(
            num_scalar_prefetch=1, grid=(shards.shape[0], T // tm),
            in_specs=[pl.BlockSpec((tm, D), lambda j, i, p: (i, 0)),
                      pl.BlockSpec((None, D, Fs), lambda j, i, p: (p[j], 0, 0)),
                      pl.BlockSpec((None, D, Fs), lambda j, i, p: (p[j], 0, 0)), TOKEN] + [ANY] * nprev,
            out_specs=[out, out, out]),
        out_shape=[shp, shp, shp], input_output_aliases={5 + q: q for q in range(nprev)},
        compiler_params=_params(("arbitrary", "arbitrary"), blk),
    )(shards, n, wg, wu, after, *(prev or ()))


def _ffn_down(a, wd, hs, shards, name, after=None):
    T, F = a.shape
    _, Fs, D = wd.shape
    tm = _tm(T)
    tn = D // 2
    extra = [] if after is None else [after]

    def body(shards_ref, a_ref, w_ref, h_ref, *refs):
        o_ref = refs[-1]
        part = FFN_RES_SCALE * jnp.dot(a_ref[...], w_ref[...], preferred_element_type=F32)

        @pl.when(pl.program_id(2) == 0)
        def _():
            o_ref[...] = h_ref[...] + part

        @pl.when(pl.program_id(2) > 0)
        def _():
            o_ref[...] += part

    blk = _nbytes((tm, Fs), BF16) + _nbytes((Fs, tn), BF16) + 3 * _nbytes((tm, tn), F32)
    return _pallas(
        body, name=name,
        grid_spec=pltpu.PrefetchScalarGridSpec(
            num_scalar_prefetch=1, grid=(D // tn, T // tm, shards.shape[0]),
            in_specs=[pl.BlockSpec((tm, Fs), lambda n, i, k, p: (i, p[k])),
                      pl.BlockSpec((None, Fs, tn), lambda n, i, k, p: (p[k], 0, n)),
                      pl.BlockSpec((tm, tn), lambda n, i, k, p: (i, n))] + [TOKEN] * len(extra),
            out_specs=pl.BlockSpec((tm, tn), lambda n, i, k, p: (i, n))),
        out_shape=jax.ShapeDtypeStruct((T, D), F32),
        compiler_params=_params(("parallel", "parallel", "arbitrary"), blk),
    )(shards, a, wd, hs, *extra)


def _ffn_down_whole(a, wd, hs, name):
    T, F = a.shape
    D = wd.shape[1]
    tm = _tm(T)
    tn = D // 4

    def body(a_ref, w_ref, h_ref, o_ref):
        o_ref[...] = h_ref[...] + FFN_RES_SCALE * jnp.dot(a_ref[...], w_ref[...], preferred_element_type=F32)

    blk = _nbytes((tm, F), BF16) + _nbytes((F, tn), BF16) + 3 * _nbytes((tm, tn), F32)
    return _pallas(
        body, name=name, grid=(D // tn, T // tm),
        in_specs=[pl.BlockSpec((tm, F), lambda n, i: (i, 0)), pl.BlockSpec((F, tn), lambda n, i: (0, n)),
                  pl.BlockSpec((tm, tn), lambda n, i: (i, n))],
        out_specs=pl.BlockSpec((tm, tn), lambda n, i: (i, n)),
        out_shape=jax.ShapeDtypeStruct((T, D), F32),
        compiler_params=_params(("parallel", "parallel"), blk),
    )(a, wd, hs)


def _mix_in(n, w, b):
    T, D = n.shape
    Ns = w.shape[2]
    tm = _tm(T)

    def body(n_ref, w_ref, b_ref, u_ref):
        u_ref[...] = jnp.dot(n_ref[...], w_ref[...], preferred_element_type=F32) + b_ref[...]

    blk = _nbytes((tm, D), BF16) + _nbytes((D, Ns), BF16) + 2 * _nbytes((tm, Ns), F32)
    return _pallas(
        body, name="mix_in", grid=(N_CHIPS, T // tm),
        in_specs=[pl.BlockSpec((tm, D), lambda j, i: (i, 0)), pl.BlockSpec((None, D, Ns), lambda j, i: (j, 0, 0)),
                  pl.BlockSpec((1, Ns), lambda j, i: (0, j))],
        out_specs=pl.BlockSpec((tm, Ns), lambda j, i: (i, j)),
        out_shape=jax.ShapeDtypeStruct((T, N_CHIPS * Ns), F32),
        compiler_params=_params(("parallel", "parallel"), blk),
    )(n, w, b)


def _mix_out(y, w, hs, after):
    T, D = y.shape
    tm = _tm(T)

    def body(y_ref, w_ref, h_ref, after_ref, o_ref):
        o_ref[...] = h_ref[...] + jnp.dot(y_ref[...], w_ref[...], preferred_element_type=F32)

    blk = _nbytes((tm, D), BF16) + _nbytes((D, D), BF16) + 3 * _nbytes((tm, D), F32)
    return _pallas(
        body, name="mix_out", grid=(T // tm,),
        in_specs=[pl.BlockSpec((tm, D), lambda i: (i, 0)), pl.BlockSpec((D, D), lambda i: (0, 0)),
                  pl.BlockSpec((tm, D), lambda i: (i, 0)), TOKEN],
        out_specs=pl.BlockSpec((tm, D), lambda i: (i, 0)),
        out_shape=jax.ShapeDtypeStruct((T, D), F32),
        compiler_params=_params(("parallel",), blk),
    )(y, w, hs, after)


def _ffn_bwd_act(dfb, wd, g, u, after, name):
    T, D = dfb.shape
    F = wd.shape[0]
    tm = _row_tile(T, 1408, 16)
    tn = 2 * MXU_COLS

    tr = _row_tile(tm, 352, 16)

    def body(d_ref, w_ref, g_ref, u_ref, after_ref, dg_ref, du_ref):
        for r0 in range(0, tm, tr):
            dv = d_ref[r0:r0 + tr, :]
            for c0, cw in _col_chunks(tn):
                da = lax.dot_general(dv, w_ref[c0:c0 + cw, :], NT_DIMS, preferred_element_type=F32)
                gv = g_ref[r0:r0 + tr, c0:c0 + cw].astype(F32)
                uv = u_ref[r0:r0 + tr, c0:c0 + cw].astype(F32)
                s = _sigmoid(gv)
                du_ref[r0:r0 + tr, c0:c0 + cw] = (da * (gv * s)).astype(BF16)
                dg_ref[r0:r0 + tr, c0:c0 + cw] = (da * uv * _dsilu(gv, s)).astype(BF16)

    blk = _nbytes((tm, D), BF16) + _nbytes((tn, D), BF16) + 4 * _nbytes((tm, tn), BF16)
    io = pl.BlockSpec((tm, tn), lambda n, i: (i, n))
    shp = jax.ShapeDtypeStruct((T, F), BF16)
    return _pallas(
        body, name=name, grid=(F // tn, T // tm),
        in_specs=[pl.BlockSpec((tm, D), lambda n, i: (i, 0)), pl.BlockSpec((tn, D), lambda n, i: (n, 0)), io, io, TOKEN],
        out_specs=[io, io], out_shape=[shp, shp],
        compiler_params=_params(("parallel", "parallel"), blk),
    )(dfb, wd, g, u, after)


def _nt_panel(lhs_list, w_list, after, name):
    T = lhs_list[0].shape[0]
    nsh, Dout, Ks = w_list[0].shape
    npair = len(lhs_list)
    tm = _row_tile(T, 1408, 16)
    tn = Dout // 2

    def body(*refs):
        l_refs, w_refs, o_ref = refs[:npair], refs[npair:2 * npair], refs[2 * npair + 1]
        j = pl.program_id(2)
        k0 = Ks - Ks % MXU_COLS if npair == 2 and 2 * (Ks % MXU_COLS) == MXU_COLS else Ks
        acc = None
        for p in range(npair):
            part = lax.dot_general(l_refs[p][:, :k0], w_refs[p][:, :k0], NT_DIMS, preferred_element_type=F32)
            acc = part if acc is None else acc + part
        if k0 < Ks:
            lhs = jnp.concatenate([l_refs[p][:, k0:] for p in range(npair)], axis=1)
            rhs = jnp.concatenate([w_refs[p][:, k0:] for p in range(npair)], axis=1)
            acc = acc + lax.dot_general(lhs, rhs, NT_DIMS, preferred_element_type=F32)

        @pl.when(j == 0)
        def _():
            o_ref[...] = acc

        @pl.when(j > 0)
        def _():
            o_ref[...] += acc

    blk = npair * (_nbytes((tm, Ks), BF16) + _nbytes((tn, Ks), BF16)) + 2 * _nbytes((tm, tn), F32)
    return _pallas(
        body, name=name, grid=(Dout // tn, T // tm, nsh),
        in_specs=[pl.BlockSpec((tm, Ks), lambda n, i, j: (i, j))] * npair
                 + [pl.BlockSpec((None, tn, Ks), lambda n, i, j: (j, n, 0))] * npair + [TOKEN],
        out_specs=pl.BlockSpec((tm, tn), lambda n, i, j: (i, n)),
        out_shape=jax.ShapeDtypeStruct((T, Dout), F32),
        compiler_params=_params(("parallel", "parallel", "arbitrary"), blk),
    )(*lhs_list, *w_list, after)


def _tn_call(name, grid, lhs, lhs_spec, rhs_list, rhs_specs, out_shapes, out_specs, blk, after=None):
    nr = len(rhs_list)
    extra = [] if after is None else [after]

    def body(*refs):
        l_ref, r_refs, o_refs = refs[0], refs[1:1 + nr], refs[len(refs) - nr:]
        k = pl.program_id(len(grid) - 1)
        lv = l_ref[...]
        for q in range(nr):
            part = lax.dot_general(lv, r_refs[q][...], TN_DIMS, preferred_element_type=F32)
            part = part.reshape(o_refs[q].shape)

            @pl.when(k == 0)
            def _(o=o_refs[q], part=part):
                o[...] = part

            @pl.when(k > 0)
            def _(o=o_refs[q], part=part):
                o[...] += part

    return _pallas(
        body, name=name, grid=grid, in_specs=[lhs_spec] + rhs_specs + [TOKEN] * len(extra), out_specs=out_specs,
        out_shape=out_shapes, compiler_params=_params(("parallel",) * (len(grid) - 1) + ("arbitrary",), blk),
    )(lhs, *rhs_list, *extra)


def _tk(T):
    return T


def _wgrad_cols(n, rhs_list, name, after=None):
    T, D = n.shape
    Ns = rhs_list[0].shape[1] // N_CHIPS
    tk = _tk(T)
    nr = len(rhs_list)
    tm = D // 4
    blk = _nbytes((tk, tm), BF16) + nr * (_nbytes((tk, Ns), BF16) + 2 * _nbytes((tm, Ns), F32))
    return _tn_call(
        name, (N_CHIPS, D // tm, T // tk), n, pl.BlockSpec((tk, tm), lambda j, m, k: (k, m)),
        rhs_list, [pl.BlockSpec((tk, Ns), lambda j, m, k: (k, j))] * nr,
        [jax.ShapeDtypeStruct((N_CHIPS, 2, D // 2, Ns), F32)] * nr,
        [pl.BlockSpec((None, None, tm, Ns), lambda j, m, k: (j, m // 2, m % 2, 0))] * nr, blk, after)


def _wgrad_down(a, dfb, name):
    T, F = a.shape
    D = dfb.shape[1]
    Fs = F // N_CHIPS
    tk = _tk(T)
    tn = D // 4
    blk = _nbytes((tk, Fs), BF16) + _nbytes((tk, tn), BF16) + 2 * _nbytes((Fs, tn), F32)
    return _tn_call(
        name, (N_CHIPS, D // tn, T // tk), a, pl.BlockSpec((tk, Fs), lambda j, n, k: (k, j)),
        [dfb], [pl.BlockSpec((tk, tn), lambda j, n, k: (k, n))],
        [jax.ShapeDtypeStruct((N_CHIPS, 2, Fs // 2, D), F32)],
        [pl.BlockSpec((None, 2, Fs // 2, tn), lambda j, n, k: (j, 0, 0, n))], blk)[0]


def _wgrad_out(y, dmb):
    T, D = y.shape
    tk = _tk(T)
    tn = D // 2
    rows = D // (2 * N_CHIPS)
    blk = _nbytes((tk, D // 2), BF16) + _nbytes((tk, tn), BF16) + 2 * _nbytes((D // 2, tn), F32)
    return _tn_call(
        "wgrad_w_out", (2, D // tn, T // tk), y, pl.BlockSpec((tk, D // 2), lambda m, n, k: (k, m)),
        [dmb], [pl.BlockSpec((tk, tn), lambda m, n, k: (k, n))],
        [jax.ShapeDtypeStruct((N_CHIPS, 2, rows, D), F32)],
        [pl.BlockSpec((2, 2, rows, tn), lambda m, n, k: (m, 0, 0, n))], blk)[0]


def _row_masks(i, last):
    rows = i * TT + lax.broadcasted_iota(jnp.int32, (TT, 1), 0)
    prows = i * TT - HALO + lax.broadcasted_iota(jnp.int32, (HALO, 1), 0)
    return rows >= PAD, (prows >= PAD) & (i > 0), i < last


def _conv_inputs(u, up, mask_c, mask_p, zbuf, pbuf, C1):
    b, c, v, a, g = (u[:, k * C1:(k + 1) * C1] for k in range(5))
    cp, vp, ap, gp = (up[:, k * C1:(k + 1) * C1] for k in range(1, 5))
    sg = _sigmoid(g)
    pbuf[0:HALO, :] = jnp.where(mask_p, cp * vp, 0.0)
    pbuf[HALO:, :] = jnp.where(mask_c, c * v, 0.0)
    if zbuf is not None:
        zbuf[0:HALO, :] = jnp.where(mask_p, ap * _sigmoid(gp), 0.0)
        zbuf[HALO:, :] = jnp.where(mask_c, a * sg, 0.0)
    return b, c, v, a, sg


SUBLANES = 8
SHIFT_ROWS = TT + HALO - SUBLANES


def _shifted_scratch(C1):
    return pltpu.VMEM((SUBLANES - 1, SHIFT_ROWS, C1), F32)


def _fill_shifted(buf, sh):
    for r in range(1, SUBLANES):
        sh[r - 1] = buf[r:r + SHIFT_ROWS, :]


LANES = 128


def _window(buf, sh, lo, c0):
    if sh is None or lo % SUBLANES == 0:
        return buf[lo:lo + TT, c0:c0 + LANES]
    q, r = divmod(lo, SUBLANES)
    return sh[r - 1, q * SUBLANES:q * SUBLANES + TT, c0:c0 + LANES]


def _tap_sum(w_ref, buf, sh, starts):
    chunks = []
    for c0 in range(0, buf.shape[1], LANES):
        acc = None
        for k, lo in enumerate(starts):
            term = w_ref[k:k + 1, c0:c0 + LANES] * _window(buf, sh, lo, c0)
            acc = term if acc is None else acc + term
        chunks.append(acc)
    return jnp.concatenate(chunks, axis=1)


def _causal_conv(w_ref, buf, sh=None):
    K = w_ref.shape[0]
    return _tap_sum(w_ref, buf, sh, [HALO - (K - 1) + k for k in range(K)])


def _anticausal_conv(w_ref, buf, sh=None):
    K = w_ref.shape[0]
    return _tap_sum(w_ref, buf, sh, [K - 1 - k for k in range(K)])


def _conv_weight_sums(dw_ref, dy, buf, sh=None):
    K = dw_ref.shape[0]
    for c0 in range(0, buf.shape[1], LANES):
        dyc = dy[:, c0:c0 + LANES]
        for k in range(K):
            prod = dyc * _window(buf, sh, HALO - (K - 1) + k, c0)
            dw_ref[k:k + 1, c0:c0 + LANES] += jnp.sum(prod, axis=0, keepdims=True)


def _layernorm_stats(z1):
    mu = jnp.mean(z1, axis=-1, keepdims=True)
    zc = z1 - mu
    rs = lax.rsqrt(jnp.mean(zc * zc, axis=-1, keepdims=True) + EPS)
    return zc * rs, rs


def _mixer_specs(T, DIN, C1, ksc, kcf):
    cur = pl.BlockSpec((TT, DIN), lambda i: (i, 0))
    prev = pl.BlockSpec((HALO, DIN), lambda i: (jnp.maximum(i * (TT // HALO) - 1, 0), 0))
    full = lambda r: pl.BlockSpec((r, C1), lambda i: (0, 0))
    return cur, prev, [full(ksc), full(kcf), full(1), full(1), full(1)]


def _mix_conv_fwd(u, wsc, wcf, bcf, lg, lb):
    T, DIN = u.shape
    C1 = DIN // 5
    last = T // TT - 1

    def body(u_ref, up_ref, wsc_ref, wcf_ref, bcf_ref, lg_ref, lb_ref, y_ref, z1_ref, zbuf, pbuf, zsh):
        i = pl.program_id(0)
        mask_c, mask_p, _ = _row_masks(i, last)
        b, _, _, _, _ = _conv_inputs(u_ref[...], up_ref[...], mask_c, mask_p, zbuf, pbuf, C1)
        _fill_shifted(zbuf, zsh)
        cs = _causal_conv(wsc_ref, pbuf)
        z1 = _causal_conv(wcf_ref, zbuf, zsh) + bcf_ref[...]
        z1_ref[...] = z1
        zh, _ = _layernorm_stats(z1)
        ln = zh * lg_ref[...] + lb_ref[...]
        y_ref[:, 0:C1] = jnp.where(mask_c, b * cs, 0.0).astype(BF16)
        y_ref[:, C1:] = jnp.where(mask_c, jax.nn.silu(ln), 0.0).astype(BF16)

    cur, prev, small = _mixer_specs(T, DIN, C1, wsc.shape[0], wcf.shape[0])
    blk = _nbytes((TT + HALO, DIN), F32) + _nbytes((TT, 2 * C1), BF16) + 12 * _nbytes((TT + HALO, C1), F32)
    return _pallas(
        body, name="mix_conv_fwd", grid=(T // TT,),
        in_specs=[cur, prev] + small,
        out_specs=[pl.BlockSpec((TT, 2 * C1), lambda i: (i, 0)), pl.BlockSpec((TT, C1), lambda i: (i, 0))],
        out_shape=[jax.ShapeDtypeStruct((T, 2 * C1), BF16), jax.ShapeDtypeStruct((T, C1), F32)],
        scratch_shapes=[pltpu.VMEM((TT + HALO, C1), F32), pltpu.VMEM((TT + HALO, C1), F32), _shifted_scratch(C1)],
        compiler_params=_params(("arbitrary",), blk),
    )(u, u, wsc, wcf, bcf, lg, lb)


def _mix_conv_bwd1(u, z1, dy, wsc, lg, lb):
    T, DIN = u.shape
    C1 = DIN // 5
    last = T // TT - 1

    def body(u_ref, up_ref, z1_ref, dy_ref, wsc_ref, lg_ref, lb_ref,
             dz1_ref, dcs_ref, db_ref, dlg_ref, dlb_ref, dbcf_ref, pbuf):
        i = pl.program_id(0)
        mask_c, mask_p, _ = _row_masks(i, last)
        b, _, _, _, _ = _conv_inputs(u_ref[...], up_ref[...], mask_c, mask_p, None, pbuf, C1)
        cs = _causal_conv(wsc_ref, pbuf)
        zh, rs = _layernorm_stats(z1_ref[...])
        ln = zh * lg_ref[...] + lb_ref[...]
        dy = dy_ref[...]
        dysc = jnp.where(mask_c, dy[:, 0:C1], 0.0)
        dycf = jnp.where(mask_c, dy[:, C1:], 0.0)
        db_ref[...] = (dysc * cs).astype(BF16)
        dcs_ref[...] = dysc * b
        dl = dycf * _dsilu(ln, _sigmoid(ln))
        dzh = dl * lg_ref[...]
        dz1 = rs * (dzh - jnp.mean(dzh, axis=-1, keepdims=True) - zh * jnp.mean(dzh * zh, axis=-1, keepdims=True))
        dz1_ref[...] = dz1

        @pl.when(i == 0)
        def _():
            dlg_ref[...] = jnp.zeros_like(dlg_ref)
            dlb_ref[...] = jnp.zeros_like(dlb_ref)
            dbcf_ref[...] = jnp.zeros_like(dbcf_ref)

        dlg_ref[...] += jnp.sum(dl * zh, axis=0, keepdims=True)
        dlb_ref[...] += jnp.sum(dl, axis=0, keepdims=True)
        dbcf_ref[...] += jnp.sum(dz1, axis=0, keepdims=True)

    cur, prev, small = _mixer_specs(T, DIN, C1, wsc.shape[0], 1)
    tile = lambda: pl.BlockSpec((TT, C1), lambda i: (i, 0))
    vec = lambda: pl.BlockSpec((1, C1), lambda i: (0, 0))
    blk = _nbytes((TT + HALO, DIN), F32) + 5 * _nbytes((TT, C1), F32) + 12 * _nbytes((TT + HALO, C1), F32)
    return _pallas(
        body, name="mix_conv_bwd1", grid=(T // TT,),
        in_specs=[cur, prev, tile(), pl.BlockSpec((TT, 2 * C1), lambda i: (i, 0)), small[0], small[3], small[4]],
        out_specs=[tile(), tile(), tile(), vec(), vec(), vec()],
        out_shape=[jax.ShapeDtypeStruct((T, C1), F32), jax.ShapeDtypeStruct((T, C1), F32),
                   jax.ShapeDtypeStruct((T, C1), BF16)] + [jax.ShapeDtypeStruct((1, C1), F32)] * 3,
        scratch_shapes=[pltpu.VMEM((TT + HALO, C1), F32)],
        compiler_params=_params(("arbitrary",), blk),
    )(u, u, z1, dy, wsc, lg, lb)


def _mix_conv_bwd2(u, dz1, dcs, db, wsc, wcf):
    T, DIN = u.shape
    C1 = DIN // 5
    last = T // TT - 1
    ksc, kcf = wsc.shape[0], wcf.shape[0]

    def body(u_ref, up_ref, dz_ref, dzn_ref, dc_ref, dcn_ref, db_ref, wsc_ref, wcf_ref,
             du_ref, dbin_ref, dwsc_ref, dwcf_ref, zbuf, pbuf, dzbuf, dcbuf, zsh, dzsh):
        i = pl.program_id(0)
        mask_c, mask_p, has_next = _row_masks(i, last)
        _, c, v, a, sg = _conv_inputs(u_ref[...], up_ref[...], mask_c, mask_p, zbuf, pbuf, C1)
        dz1 = dz_ref[...]
        dcs = dc_ref[...]
        dzbuf[0:TT, :] = dz1
        dzbuf[TT:, :] = jnp.where(has_next, dzn_ref[...], 0.0)
        dcbuf[0:TT, :] = dcs
        dcbuf[TT:, :] = jnp.where(has_next, dcn_ref[...], 0.0)

        @pl.when(i == 0)
        def _():
            dbin_ref[...] = jnp.zeros_like(dbin_ref)
            dwsc_ref[...] = jnp.zeros_like(dwsc_ref)
            dwcf_ref[...] = jnp.zeros_like(dwcf_ref)

        _fill_shifted(zbuf, zsh)
        _fill_shifted(dzbuf, dzsh)
        _conv_weight_sums(dwcf_ref, dz1, zbuf, zsh)
        _conv_weight_sums(dwsc_ref, dcs, pbuf)
        dz0 = jnp.where(mask_c, _anticausal_conv(wcf_ref, dzbuf, dzsh), 0.0)
        dp = jnp.where(mask_c, _anticausal_conv(wsc_ref, dcbuf), 0.0)
        parts = (db_ref[...].astype(F32), dp * v, dp * c, dz0 * sg, dz0 * a * sg * (1.0 - sg))
        for k, part in enumerate(parts):
            du_ref[:, k * C1:(k + 1) * C1] = part.astype(BF16)
            dbin_ref[:, k * C1:(k + 1) * C1] += jnp.sum(part, axis=0, keepdims=True)

    cur, prev, small = _mixer_specs(T, DIN, C1, ksc, kcf)
    tile = lambda: pl.BlockSpec((TT, C1), lambda i: (i, 0))
    nxt = lambda: pl.BlockSpec((HALO, C1), lambda i: (jnp.minimum((i + 1) * (TT // HALO), T // HALO - 1), 0))
    blk = (_nbytes((TT + HALO, DIN), F32) + _nbytes((TT, DIN), BF16) + 5 * _nbytes((TT, C1), F32)
           + 16 * _nbytes((TT + HALO, C1), F32))
    buf = lambda: pltpu.VMEM((TT + HALO, C1), F32)
    return _pallas(
        body, name="mix_conv_bwd2", grid=(T // TT,),
        in_specs=[cur, prev, tile(), nxt(), tile(), nxt(), tile(), small[0], small[1]],
        out_specs=[pl.BlockSpec((TT, DIN), lambda i: (i, 0)), pl.BlockSpec((1, DIN), lambda i: (0, 0)),
                   pl.BlockSpec((ksc, C1), lambda i: (0, 0)), pl.BlockSpec((kcf, C1), lambda i: (0, 0))],
        out_shape=[jax.ShapeDtypeStruct((T, DIN), BF16), jax.ShapeDtypeStruct((1, DIN), F32),
                   jax.ShapeDtypeStruct((ksc, C1), F32), jax.ShapeDtypeStruct((kcf, C1), F32)],
        scratch_shapes=[buf(), buf(), buf(), buf(), _shifted_scratch(C1), _shifted_scratch(C1)],
        compiler_params=_params(("arbitrary",), blk),
    )(u, u, dz1, dz1, dcs, dcs, db, wsc, wcf)


def _place():
    x, y, c = lax.axis_index("x"), lax.axis_index("y"), lax.axis_index("c")
    chips = [(1 - x, y), (x, 1 - y), (1 - x, 1 - y)]
    return x, y, c, chips


ANY = pl.BlockSpec(memory_space=pl.ANY)


def _cast_own_block(place, w, name):
    R, C = w.shape
    tr = _row_tile(R // 2, 256, 16)
    nblk = R // 2 // tr

    def body(place_ref, w_ref, o_ref):
        o_ref[...] = w_ref[...].astype(BF16)

    return _pallas(
        body, name=name,
        grid_spec=pltpu.PrefetchScalarGridSpec(
            num_scalar_prefetch=1, grid=(2, nblk),
            in_specs=[pl.BlockSpec((tr, C), lambda h, i, p: (h * nblk + i, 0))],
            out_specs=pl.BlockSpec((None, None, tr, C), lambda h, i, p: (p[0], h, i, 0))),
        out_shape=jax.ShapeDtypeStruct((N_CHIPS, 2, R // 2, C), BF16),
        compiler_params=_params(("parallel", "parallel"), _nbytes((tr, C), F32) + _nbytes((tr, C), BF16)),
    )(place, w)


HBM = pl.BlockSpec(memory_space=pltpu.HBM)
SEM = pl.BlockSpec(memory_space=pltpu.SEMAPHORE)
EFFECT = pltpu.SideEffectType.DATAFLOW_SIDE_EFFECTING


def _gather_copies(refs, send, recv, rels=(0, 1, 2)):
    x, y, c, chips = _place()
    s = 2 * x + y
    n = len(rels)
    return [pltpu.make_async_remote_copy(src_ref=ref.at[s, c], dst_ref=ref.at[s, c], send_sem=send.at[n * w + k],
                                         recv_sem=recv.at[n * w + k], device_id=(*chips[r], c), device_id_type=MESH)
            for w, ref in enumerate(refs) for k, r in enumerate(rels)]


def _scatter_copies(refs, send, recv):
    x, y, c, chips = _place()
    nw = len(refs) // 2
    return [pltpu.make_async_remote_copy(src_ref=refs[w].at[2 * tx + ty], dst_ref=refs[nw + w].at[r],
                                         send_sem=send.at[3 * w + r], recv_sem=recv.at[3 * w + r],
                                         device_id=(tx, ty, c), device_id_type=MESH)
            for w in range(nw) for r, (tx, ty) in enumerate(chips)]


def _pair_copies(refs, send, recv):
    x, y, c, _ = _place()
    nw = len(refs) // 2
    return [pltpu.make_async_remote_copy(src_ref=refs[w].at[j, 1 - c], dst_ref=refs[nw + w].at[j],
                                         send_sem=send.at[N_CHIPS * w + j], recv_sem=recv.at[N_CHIPS * w + j],
                                         device_id=(x, y, 1 - c), device_id_type=MESH)
            for w in range(nw) for j in range(N_CHIPS)]


def _forward_copies(refs, send, recv, rels=(0, 1, 2)):
    x, y, c, chips = _place()
    n = len(rels)
    copies = []
    for w, ref in enumerate(refs):
        for k, r in enumerate(rels):
            tx, ty = chips[r]
            blk = ref.at[2 * tx + ty, c]
            copies.append(pltpu.make_async_remote_copy(src_ref=blk, dst_ref=blk, send_sem=send.at[n * w + k],
                                                       recv_sem=recv.at[n * w + k], device_id=(x, y, 1 - c),
                                                       device_id_type=MESH))
    return copies


def _half_copies(refs, send, recv):
    x, y, c, _ = _place()
    return [pltpu.make_async_remote_copy(src_ref=ref.at[c], dst_ref=ref.at[c], send_sem=send.at[w], recv_sem=recv.at[w],
                                         device_id=(x, y, 1 - c), device_id_type=MESH)
            for w, ref in enumerate(refs)]


def _start_copies(bufs, after, ncopies, make_copies, name):
    n = len(bufs)

    def body(*refs):
        in_refs, send, recv, token = refs[:n], refs[n + 1], refs[n + 2], refs[2 * n + 3]
        for cp in make_copies(in_refs, send, recv):
            cp.start()
        token[...] = jnp.zeros_like(token)

    outs = _pallas(
        body, name=name, in_specs=[HBM] * n + [ANY],
        out_specs=[SEM, SEM] + [HBM] * n + [pl.BlockSpec(memory_space=pltpu.VMEM)],
        out_shape=[pltpu.SemaphoreType.DMA((ncopies,)), pltpu.SemaphoreType.DMA((ncopies,))]
                  + [pltpu.HBM(b.shape, b.dtype) for b in bufs] + [jax.ShapeDtypeStruct((8, 128), F32)],
        input_output_aliases={k: 2 + k for k in range(n)},
        compiler_params=pltpu.CompilerParams(has_side_effects=EFFECT),
    )(*[pltpu.with_memory_space_constraint(b, pltpu.HBM) for b in bufs], after)
    return outs[0], outs[1], list(outs[2:2 + n]), outs[2 + n]


def _wait_copies(send, recv, bufs, after, make_copies, name):
    n = len(bufs)

    def body(*refs):
        in_refs, send_ref, recv_ref = refs[:n], refs[n], refs[n + 1]
        for cp in make_copies(in_refs, send_ref, recv_ref):
            cp.wait_send()
            cp.wait_recv()

    outs = _pallas(
        body, name=name, in_specs=[HBM] * n + [SEM, SEM, ANY], out_specs=[HBM] * n,
        out_shape=[pltpu.HBM(b.shape, b.dtype) for b in bufs],
        input_output_aliases={k: k for k in range(n)},
        compiler_params=pltpu.CompilerParams(has_side_effects=EFFECT),
    )(*bufs, send, recv, after)
    return list(outs)


def _forward_halves(bufs, name, rels=(0, 1, 2)):
    nw = len(bufs)
    n = len(rels)

    def body(*refs):
        o_refs = refs[nw:2 * nw]
        send, recv = refs[2 * nw:]
        x, y, c, chips = _place()
        sib = (x, y, 1 - c)
        copies = []
        for w in range(nw):
            for k, r in enumerate(rels):
                tx, ty = chips[r]
                ref = o_refs[w].at[2 * tx + ty, c]
                cp = pltpu.make_async_remote_copy(src_ref=ref, dst_ref=ref, send_sem=send.at[n * w + k],
                                                  recv_sem=recv.at[n * w + k], device_id=sib, device_id_type=MESH)
                cp.start()
                copies.append(cp)
        for w in range(nw):
            for k, r in enumerate(rels):
                tx, ty = chips[r]
                ref = o_refs[w].at[2 * tx + ty, 1 - c]
                pltpu.make_async_remote_copy(src_ref=ref, dst_ref=ref, send_sem=send.at[n * w + k],
                                             recv_sem=recv.at[n * w + k], device_id=sib, device_id_type=MESH).wait_recv()
        for cp in copies:
            cp.wait_send()

    return _pallas(
        body, name=name, in_specs=[ANY] * nw, out_specs=[ANY] * nw,
        out_shape=[jax.ShapeDtypeStruct(b.shape, b.dtype) for b in bufs],
        input_output_aliases={w: w for w in range(nw)},
        scratch_shapes=[pltpu.SemaphoreType.DMA((n * nw,)), pltpu.SemaphoreType.DMA((n * nw,))],
    )(*bufs)


def _share_small(v, reduce, name, after):
    R, C = v.shape

    def body(v_ref, after_ref, o_ref, *scratch):
        if reduce:
            all_ref, send, recv, lsem = scratch
        else:
            all_ref = o_ref
            send, recv, lsem = scratch
        x, y, c, _ = _place()
        me = 4 * x + 2 * y + c
        loc = pltpu.make_async_copy(v_ref, all_ref.at[me], lsem)
        loc.start()
        copies = []
        for k in range(1, N_DEV):
            kx, ky, kc = (k >> 2) & 1, (k >> 1) & 1, k & 1
            peer = (x ^ kx, y ^ ky, c ^ kc)
            cp = pltpu.make_async_remote_copy(src_ref=v_ref, dst_ref=all_ref.at[me], send_sem=send.at[k - 1],
                                              recv_sem=recv.at[k - 1], device_id=peer, device_id_type=MESH)
            cp.start()
            copies.append(cp)
        for k in range(1, N_DEV):
            kx, ky, kc = (k >> 2) & 1, (k >> 1) & 1, k & 1
            src = 4 * (x ^ kx) + 2 * (y ^ ky) + (c ^ kc)
            pltpu.make_async_remote_copy(src_ref=v_ref, dst_ref=all_ref.at[src], send_sem=send.at[k - 1],
                                         recv_sem=recv.at[k - 1], device_id=(x, y, c), device_id_type=MESH).wait_recv()
        for cp in copies:
            cp.wait_send()
        loc.wait()
        if reduce:
            total = all_ref[0]
            for d in range(1, N_DEV):
                total = total + all_ref[d]
            o_ref[...] = total

    vm = pl.BlockSpec(memory_space=pltpu.VMEM)
    sems = [pltpu.SemaphoreType.DMA((N_DEV - 1,)), pltpu.SemaphoreType.DMA((N_DEV - 1,)), pltpu.SemaphoreType.DMA]
    if reduce:
        out_shape = jax.ShapeDtypeStruct((R, C), F32)
        scratch = [pltpu.VMEM((N_DEV, R, C), F32)] + sems
    else:
        out_shape = jax.ShapeDtypeStruct((N_DEV, R, C), F32)
        scratch = sems
    return _pallas(
        body, name=name, in_specs=[vm, ANY], out_specs=vm, out_shape=out_shape, scratch_shapes=scratch,
        compiler_params=pltpu.CompilerParams(vmem_limit_bytes=int(min(4 * N_DEV * R * C * 4 + 2 ** 24, 2 ** 25 + 2 ** 24))),
    )(v, after)


def _pair_sum(place, g, rb, name):
    _, _, Rh, C = g.shape
    tr = _row_tile(Rh, 256, 16)

    def body(place_ref, g_ref, r_ref, q_ref):
        q_ref[...] = (g_ref[...] + r_ref[...]).astype(BF16)

    blk = 2 * _nbytes((tr, C), F32) + _nbytes((tr, C), BF16)
    return _pallas(
        body, name=name,
        grid_spec=pltpu.PrefetchScalarGridSpec(
            num_scalar_prefetch=1, grid=(N_CHIPS - 1, Rh // tr),
            in_specs=[pl.BlockSpec((None, None, tr, C), lambda j, i, p: (p[0] ^ (j + 1), p[1], i, 0)),
                      pl.BlockSpec((None, tr, C), lambda j, i, p: (p[0] ^ (j + 1), i, 0))],
            out_specs=pl.BlockSpec((None, tr, C), lambda j, i, p: (p[0] ^ (j + 1), i, 0))),
        out_shape=jax.ShapeDtypeStruct((N_CHIPS, Rh, C), BF16),
        compiler_params=_params(("parallel", "parallel"), blk),
    )(place, g, rb)


def _chip_sum(place, g, rb, rc, name):
    _, _, Rh, C = g.shape
    tr = _row_tile(Rh, 256, 16)

    def body(place_ref, g_ref, r_ref, rc_ref, o_ref):
        total = g_ref[...] + r_ref[...]
        for r in range(3):
            total = total + rc_ref[r].astype(F32)
        o_ref[...] = total

    blk = 3 * _nbytes((tr, C), F32) + 3 * _nbytes((tr, C), BF16)
    return _pallas(
        body, name=name,
        grid_spec=pltpu.PrefetchScalarGridSpec(
            num_scalar_prefetch=1, grid=(Rh // tr,),
            in_specs=[pl.BlockSpec((None, None, tr, C), lambda i, p: (p[0], p[1], i, 0)),
                      pl.BlockSpec((None, tr, C), lambda i, p: (p[0], i, 0)),
                      pl.BlockSpec((3, tr, C), lambda i, p: (0, i, 0))],
            out_specs=pl.BlockSpec((None, tr, C), lambda i, p: (p[1], i, 0))),
        out_shape=jax.ShapeDtypeStruct((2, Rh, C), F32),
        compiler_params=_params(("parallel",), blk),
    )(place, g, rb, rc)


def _adamw_math(w, g, m, v):
    m = ADAM_B1 * m + (1.0 - ADAM_B1) * g
    v = ADAM_B2 * v + (1.0 - ADAM_B2) * jnp.square(g)
    m_hat = m / (1.0 - ADAM_B1 ** ADAM_STEP)
    v_hat = v / (1.0 - ADAM_B2 ** ADAM_STEP)
    delta = -ADAM_LR * (m_hat / (jnp.sqrt(v_hat) + ADAM_EPS) + ADAM_WD * w)
    return delta, m, v


def _adamw(w, g, m, v, name):
    R, C = w.shape
    tr = _row_tile(R, 256)

    def body(w_ref, g_ref, m_ref, v_ref, go_ref, d_ref, nm_ref, nv_ref):
        gv = g_ref[...]
        d, nm, nv = _adamw_math(w_ref[...], gv, m_ref[...], v_ref[...])
        go_ref[...] = gv
        d_ref[...] = d
        nm_ref[...] = nm
        nv_ref[...] = nv

    spec = pl.BlockSpec((tr, C), lambda i: (i, 0))
    shp = jax.ShapeDtypeStruct((R, C), F32)
    return _pallas(
        body, name=name, grid=(R // tr,), in_specs=[spec] * 4, out_specs=[spec] * 4, out_shape=[shp] * 4,
        compiler_params=_params(("parallel",), 8 * _nbytes((tr, C), F32)),
    )(w, g, m, v)


def _adamw_small(ws, gs, ms, vs):
    n = len(ws)

    def body(*refs):
        for k in range(n):
            w_ref, g_ref, m_ref, v_ref = (refs[q * n + k] for q in range(4))
            d, nm, nv = _adamw_math(w_ref[...], g_ref[...], m_ref[...], v_ref[...])
            refs[4 * n + k][...] = d
            refs[5 * n + k][...] = nm
            refs[6 * n + k][...] = nv

    vm = pl.BlockSpec(memory_space=pltpu.VMEM)
    shapes = [jax.ShapeDtypeStruct(w.shape, F32) for w in ws]
    outs = _pallas(
        body, name="adamw_small", in_specs=[vm] * (4 * n), out_specs=[vm] * (3 * n), out_shape=shapes * 3,
    )(*ws, *gs, *ms, *vs)
    return outs[:n], outs[n:2 * n], outs[2 * n:]


def _pad_rows(a, rows):
    return jnp.pad(a, ((0, rows - a.shape[0]), (0, 0)))


def kernel(x, meta_tokens, ffn1_norm, ffn1_w_gate, ffn1_w_up, ffn1_w_down, mix_norm, w_in, b_in, conv_sc_w, conv_cf_w, conv_cf_b, ln_cf_g, ln_cf_b, w_out, ffn2_norm, ffn2_w_gate, ffn2_w_up, ffn2_w_down, final_norm, loss_target, m_meta_tokens, m_ffn1_norm, m_ffn1_w_gate, m_ffn1_w_up, m_ffn1_w_down, m_mix_norm, m_w_in, m_b_in, m_conv_sc_w, m_conv_cf_w, m_conv_cf_b, m_ln_cf_g, m_ln_cf_b, m_w_out, m_ffn2_norm, m_ffn2_w_gate, m_ffn2_w_up, m_ffn2_w_down, m_final_norm, v_meta_tokens, v_ffn1_norm, v_ffn1_w_gate, v_ffn1_w_up, v_ffn1_w_down, v_mix_norm, v_w_in, v_b_in, v_conv_sc_w, v_conv_cf_w, v_conv_cf_b, v_ln_cf_g, v_ln_cf_b, v_w_out, v_ffn2_norm, v_ffn2_w_gate, v_ffn2_w_up, v_ffn2_w_down, v_final_norm):
    xi, yi, ci = lax.axis_index("x"), lax.axis_index("y"), lax.axis_index("c")
    chip = 2 * xi + yi
    place = jnp.stack([chip, ci]).astype(jnp.int32)

    x2 = x[0]
    tgt = loss_target[0]
    S, D = x2.shape
    C1 = D // 2
    cs = conv_sc_w.shape[2]
    ksc, kcf = conv_sc_w.shape[1], conv_cf_w.shape[1]
    ms = meta_tokens.shape[1]

    big = {"ffn1_w_gate": ffn1_w_gate, "ffn1_w_up": ffn1_w_up, "ffn1_w_down": ffn1_w_down, "w_in": w_in, "w_out": w_out,
           "ffn2_w_gate": ffn2_w_gate, "ffn2_w_up": ffn2_w_up, "ffn2_w_down": ffn2_w_down}
    big_m = {"ffn1_w_gate": m_ffn1_w_gate, "ffn1_w_up": m_ffn1_w_up, "ffn1_w_down": m_ffn1_w_down, "w_in": m_w_in,
             "w_out": m_w_out, "ffn2_w_gate": m_ffn2_w_gate, "ffn2_w_up": m_ffn2_w_up, "ffn2_w_down": m_ffn2_w_down}
    big_v = {"ffn1_w_gate": v_ffn1_w_gate, "ffn1_w_up": v_ffn1_w_up, "ffn1_w_down": v_ffn1_w_down, "w_in": v_w_in,
             "w_out": v_w_out, "ffn2_w_gate": v_ffn2_w_gate, "ffn2_w_up": v_ffn2_w_up, "ffn2_w_down": v_ffn2_w_down}
    buf = {nm: _cast_own_block(place, w[0], "cast_" + nm) for nm, w in big.items()}
    whole_weight = lambda g: g.reshape(N_CHIPS, 2 * g.shape[2], g.shape[3])
    corner = lambda a: a.reshape(-1, a.shape[-1])[:8, :128]

    NEAR, FAR = (0, 1), (2,)
    groups = {"ffn1_near": (["ffn1_w_gate", "ffn1_w_up", "ffn1_w_down"], NEAR),
              "ffn1_far": (["ffn1_w_gate", "ffn1_w_up", "ffn1_w_down"], FAR),
              "mix": (["w_in", "w_out"], NEAR + FAR),
              "ffn2_up": (["ffn2_w_gate", "ffn2_w_up"], NEAR + FAR),
              "ffn2_down": (["ffn2_w_down"], NEAR + FAR)}
    started = {}

    def start(tag, after):
        nms, rels = groups[tag]
        copies = functools.partial(_gather_copies, rels=rels)
        send, recv, thru, token = _start_copies([buf[nm] for nm in nms], after, len(rels) * len(nms), copies,
                                                "gather_start_" + tag)
        for nm, b in zip(nms, thru):
            buf[nm] = b
        started[tag] = (send, recv, copies)
        return token

    def arrive(tag, after, then=None):
        nms, rels = groups[tag]
        send, recv, copies = started[tag]
        got = _wait_copies(send, recv, [buf[nm] for nm in nms], corner(after), copies, "gather_wait_" + tag)
        for nm, b in zip(nms, got):
            buf[nm] = b
        if then is not None:
            start(then, corner(got[0]))
        for nm, b in zip(nms, _forward_halves([buf[nm] for nm in nms], "gather_forward_" + tag, rels)):
            buf[nm] = b

    def passing(tag, after):
        nms, rels = groups[tag]
        send, recv, copies = started[tag]
        got = _wait_copies(send, recv, [buf[nm] for nm in nms], corner(after), copies, "gather_wait_" + tag)
        copies = functools.partial(_forward_copies, rels=rels)
        send, recv, thru, token = _start_copies(got, corner(got[0]), len(rels) * len(nms), copies,
                                                "gather_pass_" + tag)
        for nm, b in zip(nms, thru):
            buf[nm] = b
        started[tag] = (send, recv, copies)
        return token

    def passed(tag, after):
        nms, _ = groups[tag]
        send, recv, copies = started[tag]
        for nm, b in zip(nms, _wait_copies(send, recv, [buf[nm] for nm in nms], corner(after), copies,
                                           "gather_passed_" + tag)):
            buf[nm] = b

    tokens = lambda *arrays: jnp.concatenate([corner(a).astype(F32) for a in arrays], axis=0)
    assert ksc <= 8 and kcf <= 32 and cs <= ms
    pack = jnp.concatenate([
        meta_tokens,
        jnp.pad(conv_sc_w[0], ((0, 8 - ksc), (0, ms - cs))),
        jnp.pad(conv_cf_w[0], ((0, 32 - kcf), (0, ms - cs)))], axis=0)
    everyone = _share_small(pack, False, "share_params", pack)[0::2]
    meta_full = jnp.transpose(everyone[:, :N_META, :], (1, 0, 2)).reshape(N_META, D)
    wsc_full = jnp.transpose(everyone[:, N_META:N_META + ksc, :cs], (1, 0, 2)).reshape(ksc, C1)
    wcf_full = jnp.transpose(everyone[:, N_META + 8:N_META + 8 + kcf, :cs], (1, 0, 2)).reshape(kcf, C1)

    token = start("ffn1_near", corner(everyone))

    ffn1 = lambda: [whole_weight(buf[nm]) for nm in ["ffn1_w_gate", "ffn1_w_up", "ffn1_w_down"]]
    own = chip[None].astype(jnp.int32)
    near = jnp.stack([chip ^ 2, chip ^ 1]).astype(jnp.int32)
    far = (chip ^ 3)[None].astype(jnp.int32)
    all_chips = jnp.arange(N_CHIPS, dtype=jnp.int32)

    hs0, n1 = _embed_rms(x2, meta_full, ffn1_norm)
    wg1, wu1, wd1 = ffn1()
    gua = _ffn_up(n1, wg1, wu1, own, None, token, "ffn1_up_own")
    hs1 = _ffn_down(gua[2], wd1, hs0, own, "ffn1_down_own")
    later = [buf[nm] for nm in ["w_in", "w_out", "ffn2_w_gate", "ffn2_w_up", "ffn2_w_down"]]
    arrive("ffn1_near", tokens(hs1, *later), "ffn1_far")
    wg1, wu1, wd1 = ffn1()
    gua = _ffn_up(n1, wg1, wu1, near, gua, token, "ffn1_up_near")
    tok = start("mix", corner(gua[2]))
    tok = passing("ffn1_far", tok)
    hs1 = _ffn_down(gua[2], ffn1()[2], hs1, near, "ffn1_down_near", tok)
    passed("ffn1_far", hs1)
    wg1, wu1, wd1 = ffn1()
    g1, u1, a1 = _ffn_up(n1, wg1, wu1, far, gua, token, "ffn1_up_far")
    tok = passing("mix", a1)
    hs1 = _ffn_down(a1, wd1, hs1, far, "ffn1_down_far", tok)
    F = N_CHIPS * wd1.shape[1]
    tok = start("ffn2_up", corner(hs1))
    passed("mix", tok)
    win, wout = whole_weight(buf["w_in"]), whole_weight(buf["w_out"])
    n2 = _rms(hs1, mix_norm, "rms_mix")
    u = _mix_in(n2, win, b_in)
    y, z1 = _mix_conv_fwd(u, wsc_full, wcf_full, conv_cf_b, ln_cf_g, ln_cf_b)
    tok = start("ffn2_down", corner(y))
    tok = passing("ffn2_up", tok)
    hs2 = _mix_out(y, wout.reshape(D, D), hs1, tok)
    passed("ffn2_up", hs2)
    wg2, wu2 = whole_weight(buf["ffn2_w_gate"]), whole_weight(buf["ffn2_w_up"])
    n3 = _rms(hs2, ffn2_norm, "rms_ffn2")
    g2, u2, a2 = _ffn_up(n3, wg2, wu2, all_chips, None, token, "ffn2_up")
    passed("ffn2_down", passing("ffn2_down", a2))
    wd2 = whole_weight(buf["ffn2_w_down"])
    hs3 = _ffn_down_whole(a2, wd2.reshape(F, D), hs2, "ffn2_down")
    token_ffn2 = token

    def pair_start(group, after, tag):
        gs = [g for _, g in group]
        lands = [lax.empty((N_CHIPS,) + g.shape[2:], F32) for g in gs]
        send, recv, thru, token = _start_copies(gs + lands, after, N_CHIPS * len(gs), _pair_copies,
                                                "pair_start_" + tag)
        return (group, send, recv, thru, tag), token

    def scatter_start(state, after):
        group, send, recv, thru, tag = state
        thru = _wait_copies(send, recv, thru, corner(after), _pair_copies, "pair_wait_" + tag)
        gs, sib = thru[:len(group)], thru[len(group):]
        sums = [_pair_sum(place, g, rb, "pair_sum_" + nm) for (nm, _), g, rb in zip(group, gs, sib)]
        lands = [lax.empty((3,) + q.shape[1:], BF16) for q in sums]
        send, recv, thru, token = _start_copies(sums + lands, corner(sums[-1]), 3 * len(gs), _scatter_copies,
                                                "scatter_start_" + tag)
        return ([(nm, g) for (nm, _), g in zip(group, gs)], sib, send, recv, thru, tag), token

    def finish_sum(state, after):
        group, sib, send, recv, thru, tag = state
        lands = _wait_copies(send, recv, thru, corner(after), _scatter_copies, "scatter_wait_" + tag)[len(group):]
        mine = [_chip_sum(place, g, rb, rc, "chip_sum_" + nm) for (nm, g), rb, rc in zip(group, sib, lands)]
        send, recv, thru, token = _start_copies(mine, corner(mine[-1]), len(mine), _half_copies, "half_start_" + tag)
        return (group, send, recv, thru, tag), token

    def finish_adam(state, after):
        group, send, recv, thru, tag = state
        whole = _wait_copies(send, recv, thru, corner(after), _half_copies, "half_wait_" + tag)
        out = {}
        for (nm, _), g in zip(group, whole):
            w = big[nm]
            g_out, d, new_m, new_v = _adamw(w[0], g.reshape(w.shape[1:]), big_m[nm][0], big_v[nm][0], "adamw_" + nm)
            out[nm] = (g_out[None], d[None], new_m[None], new_v[None])
        return out

    dhs3, df2, loss_row, d_final = _final_loss(hs3, final_norm.reshape(1, D), tgt)

    dg2, du2 = _ffn_bwd_act(df2, wd2.reshape(F, D), g2, u2, token_ffn2, "ffn2_bwd_act")
    gw_d2 = _wgrad_down(a2, df2, "wgrad_ffn2_down")
    gw_g2 = _wgrad_cols(n3, [dg2], "wgrad_ffn2_gate")[0]
    gw_u2 = _wgrad_cols(n3, [du2], "wgrad_ffn2_up")[0]
    pair_ffn2, token = pair_start([("ffn2_w_gate", gw_g2), ("ffn2_w_up", gw_u2), ("ffn2_w_down", gw_d2)],
                                  corner(gw_u2), "ffn2")
    dn3 = _nt_panel([dg2, du2], [wg2, wu2], token, "ffn2_bwd_in")
    red_ffn2, token = scatter_start(pair_ffn2, dn3)
    dhs2, dm, d_ffn2 = _rms_bwd(dn3, hs2, ffn2_norm, dhs3, 1.0, "rms_bwd_ffn2")

    dy = _nt_panel([dm], [wout.reshape(1, D, D)], token, "mix_bwd_out")
    gw_out = _wgrad_out(y, dm)
    dz1, dcs, db, d_lg, d_lb, d_bcf = _mix_conv_bwd1(u, z1, dy, wsc_full, ln_cf_g, ln_cf_b)
    du, d_bin, d_wsc, d_wcf = _mix_conv_bwd2(u, dz1, dcs, db, wsc_full, wcf_full)
    gw_in = _wgrad_cols(n2, [du], "wgrad_w_in")[0]
    pair_mix, token = pair_start([("w_in", gw_in), ("w_out", gw_out)], corner(gw_in), "mix")
    dn2 = _nt_panel([du], [win], token, "mix_bwd_in")
    red_mix, token = scatter_start(pair_mix, dn2)
    dhs1, df1, d_mix = _rms_bwd(dn2, hs1, mix_norm, dhs2, FFN_RES_SCALE, "rms_bwd_mix")

    dg1, du1 = _ffn_bwd_act(df1, wd1.reshape(F, D), g1, u1, token, "ffn1_bwd_act")
    gw_d1 = _wgrad_down(a1, df1, "wgrad_ffn1_down")
    gw_g1 = _wgrad_cols(n1, [dg1], "wgrad_ffn1_gate")[0]
    pair_ffn1a, token = pair_start([("ffn1_w_down", gw_d1), ("ffn1_w_gate", gw_g1)], corner(gw_g1), "ffn1a")
    gw_u1 = _wgrad_cols(n1, [du1], "wgrad_ffn1_up", token)[0]
    red_ffn1a, token = scatter_start(pair_ffn1a, gw_u1)
    pair_ffn1b, token = pair_start([("ffn1_w_up", gw_u1)], token, "ffn1b")
    dn1 = _nt_panel([dg1, du1], [wg1, wu1], token, "ffn1_bwd_in")
    red_ffn1b, token = scatter_start(pair_ffn1b, dn1)
    grad_x, d_meta, d_ffn1 = _rms_bwd_first(dn1, hs0, ffn1_norm, dhs1, token)

    half_ffn2, tok = finish_sum(red_ffn2, grad_x)
    half_mix, tok = finish_sum(red_mix, tok)
    big_out = finish_adam(half_ffn2, tok)
    half_ffn1a, tok = finish_sum(red_ffn1a, big_out["ffn2_w_down"][1])
    big_out.update(finish_adam(half_mix, tok))
    half_ffn1b, tok = finish_sum(red_ffn1b, big_out["w_out"][1])
    big_out.update(finish_adam(half_ffn1a, tok))
    big_out.update(finish_adam(half_ffn1b, big_out["ffn1_w_gate"][1]))

    W = C1
    rows = lambda a: a.reshape(-1, W)
    parts = [rows(d_ffn1), rows(d_mix), rows(d_ffn2), rows(d_final), rows(d_bin), d_bcf, d_lg, d_lb,
             d_wsc, d_wcf, rows(d_meta), jnp.broadcast_to(loss_row[:, :1], (1, W))]
    sizes = [p.shape[0] for p in parts]
    total_rows = sum(sizes)
    packed = _pad_rows(jnp.concatenate(parts, axis=0), -(-total_rows // 8) * 8)
    summed = _share_small(packed, True, "sum_small", big_out["ffn1_w_up"][1])
    offs = [0]
    for n in sizes:
        offs.append(offs[-1] + n)
    piece = lambda k: summed[offs[k]:offs[k + 1]]
    loss = piece(11)[0, 0]
    g_ffn1, g_mix, g_ffn2 = (piece(k).reshape(1, D) for k in range(3))
    g_final = piece(3).reshape(1, D)
    g_bin = piece(4).reshape(1, -1)
    g_bcf, g_lg, g_lb = piece(5), piece(6), piece(7)
    g_wsc = lax.dynamic_slice_in_dim(piece(8), chip * cs, cs, axis=1)
    g_wcf = lax.dynamic_slice_in_dim(piece(9), chip * cs, cs, axis=1)
    g_meta = lax.dynamic_slice_in_dim(piece(10).reshape(N_META, D), chip * ms, ms, axis=1)

    small_names = ["meta_tokens", "ffn1_norm", "mix_norm", "b_in", "conv_sc_w", "conv_cf_w", "conv_cf_b", "ln_cf_g",
                   "ln_cf_b", "ffn2_norm", "final_norm"]
    small_w = [meta_tokens, ffn1_norm, mix_norm, b_in, conv_sc_w[0], conv_cf_w[0], conv_cf_b, ln_cf_g, ln_cf_b,
               ffn2_norm, final_norm.reshape(1, D)]
    small_g = [g_meta, g_ffn1, g_mix, g_bin, g_wsc, g_wcf, g_bcf, g_lg, g_lb, g_ffn2, g_final]
    small_m = [m_meta_tokens, m_ffn1_norm, m_mix_norm, m_b_in, m_conv_sc_w[0], m_conv_cf_w[0], m_conv_cf_b, m_ln_cf_g,
               m_ln_cf_b, m_ffn2_norm, m_final_norm.reshape(1, D)]
    small_v = [v_meta_tokens, v_ffn1_norm, v_mix_norm, v_b_in, v_conv_sc_w[0], v_conv_cf_w[0], v_conv_cf_b, v_ln_cf_g,
               v_ln_cf_b, v_ffn2_norm, v_final_norm.reshape(1, D)]
    s_d, s_m, s_v = _adamw_small(small_w, small_g, small_m, small_v)
    shapes = {"conv_sc_w": conv_sc_w.shape, "conv_cf_w": conv_cf_w.shape, "final_norm": final_norm.shape}
    small_out = {}
    for nm, g, d, m, v in zip(small_names, small_g, s_d, s_m, s_v):
        shp = shapes.get(nm, g.shape)
        small_out[nm] = tuple(t.reshape(shp) for t in (g, d, m, v))

    order = ["meta_tokens", "ffn1_norm", "ffn1_w_gate", "ffn1_w_up", "ffn1_w_down", "mix_norm", "w_in", "b_in",
             "conv_sc_w", "conv_cf_w", "conv_cf_b", "ln_cf_g", "ln_cf_b", "w_out", "ffn2_norm", "ffn2_w_gate",
             "ffn2_w_up", "ffn2_w_down", "final_norm"]
    res = {**big_out, **small_out}
    outs = [loss, grad_x[None]]
    for q in range(4):
        outs.extend(res[nm][q] for nm in order)
    return tuple(outs)
```

```python
import functools

import jax
import jax.numpy as jnp
from jax import lax
from jax.experimental import pallas as pl
from jax.experimental.pallas import tpu as pltpu

F32 = jnp.float32
BF16 = jnp.bfloat16
MESH = pl.DeviceIdType.MESH

N_META = 16
TT = 128
PAD = TT - N_META
HALO = 32
EPS = 1e-6
FFN_RES_SCALE = 0.5
N_CHIPS = 4
N_DEV = 8

ADAM_LR = 0.001
ADAM_B1 = 0.9
ADAM_B2 = 0.999
ADAM_EPS = 1e-08
ADAM_WD = 0.01
ADAM_STEP = 10

V7X_VMEM_BYTES = 64 * 2 ** 20
NT_DIMS = (((1,), (1,)), ((), ()))
TN_DIMS = (((0,), (0,)), ((), ()))


def _params(semantics, block_bytes):
    limit = min(2 * block_bytes + 16 * 2 ** 20, V7X_VMEM_BYTES - 6 * 2 ** 20)
    return pltpu.CompilerParams(dimension_semantics=semantics, vmem_limit_bytes=int(limit))


def _pallas(body, out_shape, **kw):
    if "grid" not in kw and "grid_spec" not in kw:
        return pl.pallas_call(body, out_shape=out_shape, **kw)
    big = lambda shape, dtype: jnp.issubdtype(dtype, jnp.floating) and len(shape) >= 2
    pin_out = lambda s: pltpu.HBM(s.shape, s.dtype) if big(s.shape, s.dtype) else s
    single = not isinstance(out_shape, (list, tuple))
    shapes = pin_out(out_shape) if single else [pin_out(s) for s in out_shape]
    call = pl.pallas_call(body, out_shape=shapes, **kw)
    pin = lambda a: pltpu.with_memory_space_constraint(a, pltpu.HBM) if big(a.shape, a.dtype) else a
    return lambda *operands: call(*[pin(a) for a in operands])


def _nbytes(shape, dtype):
    n = 1
    for d in shape:
        if d is not None:
            n *= d
    return n * jnp.dtype(dtype).itemsize


def _row_tile(rows, target, mult=8):
    best = None
    for t in range(mult, min(rows, target) + 1, mult):
        if rows % t == 0:
            best = t
    assert best is not None, (rows, target, mult)
    return best


def _sigmoid(v):
    return jax.nn.sigmoid(v)


def _dsilu(v, s):
    return s * (1.0 + v * (1.0 - s))


def _embed_rms(x2, meta, gain):
    S, D = x2.shape
    T = S + TT

    def body(x_ref, meta_ref, g_ref, hs_ref, n_ref):
        i = pl.program_id(0)

        @pl.when(i == 0)
        def _():
            hs_ref[...] = jnp.zeros_like(hs_ref)
            hs_ref[PAD:, :] = meta_ref[...]

        @pl.when(i > 0)
        def _():
            hs_ref[...] = x_ref[...]

        h = hs_ref[...]
        r = lax.rsqrt(jnp.mean(h * h, axis=-1, keepdims=True) + EPS)
        n_ref[...] = ((h * r) * g_ref[...]).astype(BF16)

    blk = _nbytes((TT, D), F32) * 2 + _nbytes((TT, D), BF16)
    return _pallas(
        body, name="embed_rms", grid=(T // TT,),
        in_specs=[pl.BlockSpec((TT, D), lambda i: (jnp.maximum(i - 1, 0), 0)),
                  pl.BlockSpec((N_META, D), lambda i: (0, 0)),
                  pl.BlockSpec((1, D), lambda i: (0, 0))],
        out_specs=[pl.BlockSpec((TT, D), lambda i: (i, 0)), pl.BlockSpec((TT, D), lambda i: (i, 0))],
        out_shape=[jax.ShapeDtypeStruct((T, D), F32), jax.ShapeDtypeStruct((T, D), BF16)],
        compiler_params=_params(("parallel",), blk),
    )(x2, meta, gain)


def _rms(hs, gain, name):
    T, D = hs.shape
    te = _row_tile(T, 384)

    def body(h_ref, g_ref, n_ref):
        h = h_ref[...]
        r = lax.rsqrt(jnp.mean(h * h, axis=-1, keepdims=True) + EPS)
        n_ref[...] = ((h * r) * g_ref[...]).astype(BF16)

    blk = _nbytes((te, D), F32) + _nbytes((te, D), BF16)
    return _pallas(
        body, name=name, grid=(T // te,),
        in_specs=[pl.BlockSpec((te, D), lambda i: (i, 0)), pl.BlockSpec((1, D), lambda i: (0, 0))],
        out_specs=pl.BlockSpec((te, D), lambda i: (i, 0)),
        out_shape=jax.ShapeDtypeStruct((T, D), BF16),
        compiler_params=_params(("parallel",), blk),
    )(hs, gain)


def _rms_bwd_math(dn, h, g):
    r = lax.rsqrt(jnp.mean(h * h, axis=-1, keepdims=True) + EPS)
    xh = h * r
    dgain = jnp.sum(dn * xh, axis=0, keepdims=True)
    dxh = dn * g
    dh = r * (dxh - xh * jnp.mean(dxh * xh, axis=-1, keepdims=True))
    return dh, dgain


def _rms_bwd(dn, hs, gain, dres, scale, name):
    T, D = hs.shape
    te = _row_tile(T, 384)

    def body(dn_ref, h_ref, g_ref, dres_ref, dhs_ref, dhb_ref, dg_ref):
        dh, dgain = _rms_bwd_math(dn_ref[...], h_ref[...], g_ref[...])
        d = dres_ref[...] + dh
        dhs_ref[...] = d
        dhb_ref[...] = (scale * d).astype(BF16)

        @pl.when(pl.program_id(0) == 0)
        def _():
            dg_ref[...] = jnp.zeros_like(dg_ref)

        dg_ref[...] += dgain

    blk = _nbytes((te, D), F32) * 4 + _nbytes((te, D), BF16)
    row = lambda i: (i, 0)
    return _pallas(
        body, name=name, grid=(T // te,),
        in_specs=[pl.BlockSpec((te, D), row), pl.BlockSpec((te, D), row), pl.BlockSpec((1, D), lambda i: (0, 0)),
                  pl.BlockSpec((te, D), row)],
        out_specs=[pl.BlockSpec((te, D), row), pl.BlockSpec((te, D), row), pl.BlockSpec((1, D), lambda i: (0, 0))],
        out_shape=[jax.ShapeDtypeStruct((T, D), F32), jax.ShapeDtypeStruct((T, D), BF16),
                   jax.ShapeDtypeStruct((1, D), F32)],
        compiler_params=_params(("arbitrary",), blk),
    )(dn, hs, gain, dres)


def _rms_bwd_first(dn, hs, gain, dres, after):
    T, D = hs.shape
    S = T - TT

    def body(dn_ref, h_ref, g_ref, dres_ref, after_ref, gx_ref, gm_ref, dg_ref):
        i = pl.program_id(0)
        dh, dgain = _rms_bwd_math(dn_ref[...], h_ref[...], g_ref[...])
        d = dres_ref[...] + dh

        @pl.when(i == 0)
        def _():
            dg_ref[...] = jnp.zeros_like(dg_ref)
            gm_ref[...] = d[PAD:, :]

        @pl.when(i > 0)
        def _():
            gx_ref[...] = d

        dg_ref[...] += dgain

    blk = _nbytes((TT, D), F32) * 4
    row = lambda i: (i, 0)
    return _pallas(
        body, name="rms_bwd_ffn1", grid=(T // TT,),
        in_specs=[pl.BlockSpec((TT, D), row), pl.BlockSpec((TT, D), row), pl.BlockSpec((1, D), lambda i: (0, 0)),
                  pl.BlockSpec((TT, D), row), TOKEN],
        out_specs=[pl.BlockSpec((TT, D), lambda i: (jnp.maximum(i - 1, 0), 0)),
                   pl.BlockSpec((N_META, D), lambda i: (0, 0)), pl.BlockSpec((1, D), lambda i: (0, 0))],
        out_shape=[jax.ShapeDtypeStruct((S, D), F32), jax.ShapeDtypeStruct((N_META, D), F32),
                   jax.ShapeDtypeStruct((1, D), F32)],
        compiler_params=_params(("arbitrary",), blk),
    )(dn, hs, gain, dres, after)


def _final_loss(hs, gain, tgt):
    T, D = hs.shape

    def body(h_ref, g_ref, t_ref, dhs_ref, dhb_ref, loss_ref, dg_ref):
        i = pl.program_id(0)
        h = h_ref[...]
        g = g_ref[...]
        r = lax.rsqrt(jnp.mean(h * h, axis=-1, keepdims=True) + EPS)
        xh = h * r
        e = jnp.where(i > 0, xh * g - t_ref[...], 0.0)
        tile_loss = jnp.sum(jnp.sum(e * e, axis=1, keepdims=True), axis=0, keepdims=True) * (0.5 / D)
        dout = e * (1.0 / D)
        dgain = jnp.sum(dout * xh, axis=0, keepdims=True)
        dxh = dout * g
        d = r * (dxh - xh * jnp.mean(dxh * xh, axis=-1, keepdims=True))
        dhs_ref[...] = d
        dhb_ref[...] = (FFN_RES_SCALE * d).astype(BF16)

        @pl.when(i == 0)
        def _():
            loss_ref[...] = jnp.zeros_like(loss_ref)
            dg_ref[...] = jnp.zeros_like(dg_ref)

        loss_ref[...] += jnp.broadcast_to(tile_loss, loss_ref.shape)
        dg_ref[...] += dgain

    blk = _nbytes((TT, D), F32) * 3 + _nbytes((TT, D), BF16)
    row = lambda i: (i, 0)
    return _pallas(
        body, name="final_loss", grid=(T // TT,),
        in_specs=[pl.BlockSpec((TT, D), row), pl.BlockSpec((1, D), lambda i: (0, 0)),
                  pl.BlockSpec((TT, D), lambda i: (jnp.maximum(i - 1, 0), 0))],
        out_specs=[pl.BlockSpec((TT, D), row), pl.BlockSpec((TT, D), row),
                   pl.BlockSpec((1, 128), lambda i: (0, 0)), pl.BlockSpec((1, D), lambda i: (0, 0))],
        out_shape=[jax.ShapeDtypeStruct((T, D), F32), jax.ShapeDtypeStruct((T, D), BF16),
                   jax.ShapeDtypeStruct((1, 128), F32), jax.ShapeDtypeStruct((1, D), F32)],
        compiler_params=_params(("arbitrary",), blk),
    )(hs, gain, tgt)


MXU_COLS = 256


def _tm(T):
    return _row_tile(T, 704, 16)


def _col_chunks(n):
    return [(c, min(MXU_COLS, n - c)) for c in range(0, n, MXU_COLS)]


TOKEN = pl.BlockSpec((8, 128), lambda *_: (0, 0))


def _ffn_up(n, wg, wu, shards, prev, after, name):
    T, D = n.shape
    Fs = wg.shape[2]
    tm = _tm(T)
    nprev = 0 if prev is None else 3

    def body(shards_ref, n_ref, wg_ref, wu_ref, after_ref, *refs):
        g_ref, u_ref, a_ref = refs[nprev:]
        nn = n_ref[...]
        for c0, cw in _col_chunks(Fs):
            if 2 * cw == MXU_COLS:
                both = jnp.concatenate([wg_ref[:, c0:c0 + cw], wu_ref[:, c0:c0 + cw]], axis=1)
                gu = jnp.dot(nn, both, preferred_element_type=F32)
                g, u = gu[:, :cw], gu[:, cw:]
            else:
                g = jnp.dot(nn, wg_ref[:, c0:c0 + cw], preferred_element_type=F32)
                u = jnp.dot(nn, wu_ref[:, c0:c0 + cw], preferred_element_type=F32)
            g_ref[:, c0:c0 + cw] = g.astype(BF16)
            u_ref[:, c0:c0 + cw] = u.astype(BF16)
            a_ref[:, c0:c0 + cw] = (jax.nn.silu(g) * u).astype(BF16)

    blk = _nbytes((tm, D), BF16) + 2 * _nbytes((D, Fs), BF16) + 3 * _nbytes((tm, Fs), BF16)
    out = pl.BlockSpec((tm, Fs), lambda j, i, p: (i, p[j]))
    shp = jax.ShapeDtypeStruct((T, N_CHIPS * Fs), BF16)
    return _pallas(
        body, name=name,
        grid_spec=pltpu.PrefetchScalarGridSpec(
            num_scalar_prefetch=1, grid=(shards.shape[0], T // tm),
            in_specs=[pl.BlockSpec((tm, D), lambda j, i, p: (i, 0)),
                      pl.BlockSpec((None, D, Fs), lambda j, i, p: (p[j], 0, 0)),
                      pl.BlockSpec((None, D, Fs), lambda j, i, p: (p[j], 0, 0)), TOKEN] + [ANY] * nprev,
            out_specs=[out, out, out]),
        out_shape=[shp, shp, shp], input_output_aliases={5 + q: q for q in range(nprev)},
        compiler_params=_params(("arbitrary", "arbitrary"), blk),
    )(shards, n, wg, wu, after, *(prev or ()))


def _ffn_down(a, wd, hs, shards, name, after=None):
    T, F = a.shape
    _, Fs, D = wd.shape
    tm = _tm(T)
    tn = D // 2
    extra = [] if after is None else [after]

    def body(shards_ref, a_ref, w_ref, h_ref, *refs):
        o_ref = refs[-1]
        part = FFN_RES_SCALE * jnp.dot(a_ref[...], w_ref[...], preferred_element_type=F32)

        @pl.when(pl.program_id(2) == 0)
        def _():
            o_ref[...] = h_ref[...] + part

        @pl.when(pl.program_id(2) > 0)
        def _():
            o_ref[...] += part

    blk = _nbytes((tm, Fs), BF16) + _nbytes((Fs, tn), BF16) + 3 * _nbytes((tm, tn), F32)
    return _pallas(
        body, name=name,
        grid_spec=pltpu.PrefetchScalarGridSpec(
            num_scalar_prefetch=1, grid=(D // tn, T // tm, shards.shape[0]),
            in_specs=[pl.BlockSpec((tm, Fs), lambda n, i, k, p: (i, p[k])),
                      pl.BlockSpec((None, Fs, tn), lambda n, i, k, p: (p[k], 0, n)),
                      pl.BlockSpec((tm, tn), lambda n, i, k, p: (i, n))] + [TOKEN] * len(extra),
            out_specs=pl.BlockSpec((tm, tn), lambda n, i, k, p: (i, n))),
        out_shape=jax.ShapeDtypeStruct((T, D), F32),
        compiler_params=_params(("parallel", "parallel", "arbitrary"), blk),
    )(shards, a, wd, hs, *extra)


def _ffn_down_whole(a, wd, hs, name):
    T, F = a.shape
    D = wd.shape[1]
    tm = _tm(T)
    tn = D // 4

    def body(a_ref, w_ref, h_ref, o_ref):
        o_ref[...] = h_ref[...] + FFN_RES_SCALE * jnp.dot(a_ref[...], w_ref[...], preferred_element_type=F32)

    blk = _nbytes((tm, F), BF16) + _nbytes((F, tn), BF16) + 3 * _nbytes((tm, tn), F32)
    return _pallas(
        body, name=name, grid=(D // tn, T // tm),
        in_specs=[pl.BlockSpec((tm, F), lambda n, i: (i, 0)), pl.BlockSpec((F, tn), lambda n, i: (0, n)),
                  pl.BlockSpec((tm, tn), lambda n, i: (i, n))],
        out_specs=pl.BlockSpec((tm, tn), lambda n, i: (i, n)),
        out_shape=jax.ShapeDtypeStruct((T, D), F32),
        compiler_params=_params(("parallel", "parallel"), blk),
    )(a, wd, hs)


def _mix_in(n, w, b):
    T, D = n.shape
    Ns = w.shape[2]
    tm = _tm(T)

    def body(n_ref, w_ref, b_ref, u_ref):
        u_ref[...] = jnp.dot(n_ref[...], w_ref[...], preferred_element_type=F32) + b_ref[...]

    blk = _nbytes((tm, D), BF16) + _nbytes((D, Ns), BF16) + 2 * _nbytes((tm, Ns), F32)
    return _pallas(
        body, name="mix_in", grid=(N_CHIPS, T // tm),
        in_specs=[pl.BlockSpec((tm, D), lambda j, i: (i, 0)), pl.BlockSpec((None, D, Ns), lambda j, i: (j, 0, 0)),
                  pl.BlockSpec((1, Ns), lambda j, i: (0, j))],
        out_specs=pl.BlockSpec((tm, Ns), lambda j, i: (i, j)),
        out_shape=jax.ShapeDtypeStruct((T, N_CHIPS * Ns), F32),
        compiler_params=_params(("parallel", "parallel"), blk),
    )(n, w, b)


def _mix_out(y, w, hs, after):
    T, D = y.shape
    tm = _tm(T)

    def body(y_ref, w_ref, h_ref, after_ref, o_ref):
        o_ref[...] = h_ref[...] + jnp.dot(y_ref[...], w_ref[...], preferred_element_type=F32)

    blk = _nbytes((tm, D), BF16) + _nbytes((D, D), BF16) + 3 * _nbytes((tm, D), F32)
    return _pallas(
        body, name="mix_out", grid=(T // tm,),
        in_specs=[pl.BlockSpec((tm, D), lambda i: (i, 0)), pl.BlockSpec((D, D), lambda i: (0, 0)),
                  pl.BlockSpec((tm, D), lambda i: (i, 0)), TOKEN],
        out_specs=pl.BlockSpec((tm, D), lambda i: (i, 0)),
        out_shape=jax.ShapeDtypeStruct((T, D), F32),
        compiler_params=_params(("parallel",), blk),
    )(y, w, hs, after)


def _ffn_bwd_act(dfb, wd, g, u, after, name):
    T, D = dfb.shape
    F = wd.shape[0]
    tm = _row_tile(T, 1408, 16)
    tn = 2 * MXU_COLS

    tr = _row_tile(tm, 352, 16)

    def body(d_ref, w_ref, g_ref, u_ref, after_ref, dg_ref, du_ref):
        for r0 in range(0, tm, tr):
            dv = d_ref[r0:r0 + tr, :]
            for c0, cw in _col_chunks(tn):
                da = lax.dot_general(dv, w_ref[c0:c0 + cw, :], NT_DIMS, preferred_element_type=F32)
                gv = g_ref[r0:r0 + tr, c0:c0 + cw].astype(F32)
                uv = u_ref[r0:r0 + tr, c0:c0 + cw].astype(F32)
                s = _sigmoid(gv)
                du_ref[r0:r0 + tr, c0:c0 + cw] = (da * (gv * s)).astype(BF16)
                dg_ref[r0:r0 + tr, c0:c0 + cw] = (da * uv * _dsilu(gv, s)).astype(BF16)

    blk = _nbytes((tm, D), BF16) + _nbytes((tn, D), BF16) + 4 * _nbytes((tm, tn), BF16)
    io = pl.BlockSpec((tm, tn), lambda n, i: (i, n))
    shp = jax.ShapeDtypeStruct((T, F), BF16)
    return _pallas(
        body, name=name, grid=(F // tn, T // tm),
        in_specs=[pl.BlockSpec((tm, D), lambda n, i: (i, 0)), pl.BlockSpec((tn, D), lambda n, i: (n, 0)), io, io, TOKEN],
        out_specs=[io, io], out_shape=[shp, shp],
        compiler_params=_params(("parallel", "parallel"), blk),
    )(dfb, wd, g, u, after)


def _nt_panel(lhs_list, w_list, after, name):
    T = lhs_list[0].shape[0]
    nsh, Dout, Ks = w_list[0].shape
    npair = len(lhs_list)
    tm = _row_tile(T, 1408, 16)
    tn = Dout // 2

    def body(*refs):
        l_refs, w_refs, o_ref = refs[:npair], refs[npair:2 * npair], refs[2 * npair + 1]
        j = pl.program_id(2)
        k0 = Ks - Ks % MXU_COLS if npair == 2 and 2 * (Ks % MXU_COLS) == MXU_COLS else Ks
        acc = None
        for p in range(npair):
            part = lax.dot_general(l_refs[p][:, :k0], w_refs[p][:, :k0], NT_DIMS, preferred_element_type=F32)
            acc = part if acc is None else acc + part
        if k0 < Ks:
            lhs = jnp.concatenate([l_refs[p][:, k0:] for p in range(npair)], axis=1)
            rhs = jnp.concatenate([w_refs[p][:, k0:] for p in range(npair)], axis=1)
            acc = acc + lax.dot_general(lhs, rhs, NT_DIMS, preferred_element_type=F32)

        @pl.when(j == 0)
        def _():
            o_ref[...] = acc

        @pl.when(j > 0)
        def _():
            o_ref[...] += acc

    blk = npair * (_nbytes((tm, Ks), BF16) + _nbytes((tn, Ks), BF16)) + 2 * _nbytes((tm, tn), F32)
    return _pallas(
        body, name=name, grid=(Dout // tn, T // tm, nsh),
        in_specs=[pl.BlockSpec((tm, Ks), lambda n, i, j: (i, j))] * npair
                 + [pl.BlockSpec((None, tn, Ks), lambda n, i, j: (j, n, 0))] * npair + [TOKEN],
        out_specs=pl.BlockSpec((tm, tn), lambda n, i, j: (i, n)),
        out_shape=jax.ShapeDtypeStruct((T, Dout), F32),
        compiler_params=_params(("parallel", "parallel", "arbitrary"), blk),
    )(*lhs_list, *w_list, after)


def _tn_call(name, grid, lhs, lhs_spec, rhs_list, rhs_specs, out_shapes, out_specs, blk, after=None):
    nr = len(rhs_list)
    extra = [] if after is None else [after]

    def body(*refs):
        l_ref, r_refs, o_refs = refs[0], refs[1:1 + nr], refs[len(refs) - nr:]
        k = pl.program_id(len(grid) - 1)
        lv = l_ref[...]
        for q in range(nr):
            part = lax.dot_general(lv, r_refs[q][...], TN_DIMS, preferred_element_type=F32)
            part = part.reshape(o_refs[q].shape)

            @pl.when(k == 0)
            def _(o=o_refs[q], part=part):
                o[...] = part

            @pl.when(k > 0)
            def _(o=o_refs[q], part=part):
                o[...] += part

    return _pallas(
        body, name=name, grid=grid, in_specs=[lhs_spec] + rhs_specs + [TOKEN] * len(extra), out_specs=out_specs,
        out_shape=out_shapes, compiler_params=_params(("parallel",) * (len(grid) - 1) + ("arbitrary",), blk),
    )(lhs, *rhs_list, *extra)


def _tk(T):
    return T


def _wgrad_cols(n, rhs_list, name, after=None):
    T, D = n.shape
    Ns = rhs_list[0].shape[1] // N_CHIPS
    tk = _tk(T)
    nr = len(rhs_list)
    tm = D // 4
    blk = _nbytes((tk, tm), BF16) + nr * (_nbytes((tk, Ns), BF16) + 2 * _nbytes((tm, Ns), F32))
    return _tn_call(
        name, (N_CHIPS, D // tm, T // tk), n, pl.BlockSpec((tk, tm), lambda j, m, k: (k, m)),
        rhs_list, [pl.BlockSpec((tk, Ns), lambda j, m, k: (k, j))] * nr,
        [jax.ShapeDtypeStruct((N_CHIPS, 2, D // 2, Ns), F32)] * nr,
        [pl.BlockSpec((None, None, tm, Ns), lambda j, m, k: (j, m // 2, m % 2, 0))] * nr, blk, after)


def _wgrad_down(a, dfb, name):
    T, F = a.shape
    D = dfb.shape[1]
    Fs = F // N_CHIPS
    tk = _tk(T)
    tn = D // 4
    blk = _nbytes((tk, Fs), BF16) + _nbytes((tk, tn), BF16) + 2 * _nbytes((Fs, tn), F32)
    return _tn_call(
        name, (N_CHIPS, D // tn, T // tk), a, pl.BlockSpec((tk, Fs), lambda j, n, k: (k, j)),
        [dfb], [pl.BlockSpec((tk, tn), lambda j, n, k: (k, n))],
        [jax.ShapeDtypeStruct((N_CHIPS, 2, Fs // 2, D), F32)],
        [pl.BlockSpec((None, 2, Fs // 2, tn), lambda j, n, k: (j, 0, 0, n))], blk)[0]


def _wgrad_out(y, dmb):
    T, D = y.shape
    tk = _tk(T)
    tn = D // 2
    rows = D // (2 * N_CHIPS)
    blk = _nbytes((tk, D // 2), BF16) + _nbytes((tk, tn), BF16) + 2 * _nbytes((D // 2, tn), F32)
    return _tn_call(
        "wgrad_w_out", (2, D // tn, T // tk), y, pl.BlockSpec((tk, D // 2), lambda m, n, k: (k, m)),
        [dmb], [pl.BlockSpec((tk, tn), lambda m, n, k: (k, n))],
        [jax.ShapeDtypeStruct((N_CHIPS, 2, rows, D), F32)],
        [pl.BlockSpec((2, 2, rows, tn), lambda m, n, k: (m, 0, 0, n))], blk)[0]


def _row_masks(i, last):
    rows = i * TT + lax.broadcasted_iota(jnp.int32, (TT, 1), 0)
    prows = i * TT - HALO + lax.broadcasted_iota(jnp.int32, (HALO, 1), 0)
    return rows >= PAD, (prows >= PAD) & (i > 0), i < last


def _conv_inputs(u, up, mask_c, mask_p, zbuf, pbuf, C1):
    b, c, v, a, g = (u[:, k * C1:(k + 1) * C1] for k in range(5))
    cp, vp, ap, gp = (up[:, k * C1:(k + 1) * C1] for k in range(1, 5))
    sg = _sigmoid(g)
    pbuf[0:HALO, :] = jnp.where(mask_p, cp * vp, 0.0)
    pbuf[HALO:, :] = jnp.where(mask_c, c * v, 0.0)
    if zbuf is not None:
        zbuf[0:HALO, :] = jnp.where(mask_p, ap * _sigmoid(gp), 0.0)
        zbuf[HALO:, :] = jnp.where(mask_c, a * sg, 0.0)
    return b, c, v, a, sg


SUBLANES = 8
SHIFT_ROWS = TT + HALO - SUBLANES


def _shifted_scratch(C1):
    return pltpu.VMEM((SUBLANES - 1, SHIFT_ROWS, C1), F32)


def _fill_shifted(buf, sh):
    for r in range(1, SUBLANES):
        sh[r - 1] = buf[r:r + SHIFT_ROWS, :]


LANES = 128


def _window(buf, sh, lo, c0):
    if sh is None or lo % SUBLANES == 0:
        return buf[lo:lo + TT, c0:c0 + LANES]
    q, r = divmod(lo, SUBLANES)
    return sh[r - 1, q * SUBLANES:q * SUBLANES + TT, c0:c0 + LANES]


def _tap_sum(w_ref, buf, sh, starts):
    chunks = []
    for c0 in range(0, buf.shape[1], LANES):
        acc = None
        for k, lo in enumerate(starts):
            term = w_ref[k:k + 1, c0:c0 + LANES] * _window(buf, sh, lo, c0)
            acc = term if acc is None else acc + term
        chunks.append(acc)
    return jnp.concatenate(chunks, axis=1)


def _causal_conv(w_ref, buf, sh=None):
    K = w_ref.shape[0]
    return _tap_sum(w_ref, buf, sh, [HALO - (K - 1) + k for k in range(K)])


def _anticausal_conv(w_ref, buf, sh=None):
    K = w_ref.shape[0]
    return _tap_sum(w_ref, buf, sh, [K - 1 - k for k in range(K)])


def _conv_weight_sums(dw_ref, dy, buf, sh=None):
    K = dw_ref.shape[0]
    for c0 in range(0, buf.shape[1], LANES):
        dyc = dy[:, c0:c0 + LANES]
        for k in range(K):
            prod = dyc * _window(buf, sh, HALO - (K - 1) + k, c0)
            dw_ref[k:k + 1, c0:c0 + LANES] += jnp.sum(prod, axis=0, keepdims=True)


def _layernorm_stats(z1):
    mu = jnp.mean(z1, axis=-1, keepdims=True)
    zc = z1 - mu
    rs = lax.rsqrt(jnp.mean(zc * zc, axis=-1, keepdims=True) + EPS)
    return zc * rs, rs


def _mixer_specs(T, DIN, C1, ksc, kcf):
    cur = pl.BlockSpec((TT, DIN), lambda i: (i, 0))
    prev = pl.BlockSpec((HALO, DIN), lambda i: (jnp.maximum(i * (TT // HALO) - 1, 0), 0))
    full = lambda r: pl.BlockSpec((r, C1), lambda i: (0, 0))
    return cur, prev, [full(ksc), full(kcf), full(1), full(1), full(1)]


def _mix_conv_fwd(u, wsc, wcf, bcf, lg, lb):
    T, DIN = u.shape
    C1 = DIN // 5
    last = T // TT - 1

    def body(u_ref, up_ref, wsc_ref, wcf_ref, bcf_ref, lg_ref, lb_ref, y_ref, z1_ref, zbuf, pbuf, zsh):
        i = pl.program_id(0)
        mask_c, mask_p, _ = _row_masks(i, last)
        b, _, _, _, _ = _conv_inputs(u_ref[...], up_ref[...], mask_c, mask_p, zbuf, pbuf, C1)
        _fill_shifted(zbuf, zsh)
        cs = _causal_conv(wsc_ref, pbuf)
        z1 = _causal_conv(wcf_ref, zbuf, zsh) + bcf_ref[...]
        z1_ref[...] = z1
        zh, _ = _layernorm_stats(z1)
        ln = zh * lg_ref[...] + lb_ref[...]
        y_ref[:, 0:C1] = jnp.where(mask_c, b * cs, 0.0).astype(BF16)
        y_ref[:, C1:] = jnp.where(mask_c, jax.nn.silu(ln), 0.0).astype(BF16)

    cur, prev, small = _mixer_specs(T, DIN, C1, wsc.shape[0], wcf.shape[0])
    blk = _nbytes((TT + HALO, DIN), F32) + _nbytes((TT, 2 * C1), BF16) + 12 * _nbytes((TT + HALO, C1), F32)
    return _pallas(
        body, name="mix_conv_fwd", grid=(T // TT,),
        in_specs=[cur, prev] + small,
        out_specs=[pl.BlockSpec((TT, 2 * C1), lambda i: (i, 0)), pl.BlockSpec((TT, C1), lambda i: (i, 0))],
        out_shape=[jax.ShapeDtypeStruct((T, 2 * C1), BF16), jax.ShapeDtypeStruct((T, C1), F32)],
        scratch_shapes=[pltpu.VMEM((TT + HALO, C1), F32), pltpu.VMEM((TT + HALO, C1), F32), _shifted_scratch(C1)],
        compiler_params=_params(("arbitrary",), blk),
    )(u, u, wsc, wcf, bcf, lg, lb)


def _mix_conv_bwd1(u, z1, dy, wsc, lg, lb):
    T, DIN = u.shape
    C1 = DIN // 5
    last = T // TT - 1

    def body(u_ref, up_ref, z1_ref, dy_ref, wsc_ref, lg_ref, lb_ref,
             dz1_ref, dcs_ref, db_ref, dlg_ref, dlb_ref, dbcf_ref, pbuf):
        i = pl.program_id(0)
        mask_c, mask_p, _ = _row_masks(i, last)
        b, _, _, _, _ = _conv_inputs(u_ref[...], up_ref[...], mask_c, mask_p, None, pbuf, C1)
        cs = _causal_conv(wsc_ref, pbuf)
        zh, rs = _layernorm_stats(z1_ref[...])
        ln = zh * lg_ref[...] + lb_ref[...]
        dy = dy_ref[...]
        dysc = jnp.where(mask_c, dy[:, 0:C1], 0.0)
        dycf = jnp.where(mask_c, dy[:, C1:], 0.0)
        db_ref[...] = (dysc * cs).astype(BF16)
        dcs_ref[...] = dysc * b
        dl = dycf * _dsilu(ln, _sigmoid(ln))
        dzh = dl * lg_ref[...]
        dz1 = rs * (dzh - jnp.mean(dzh, axis=-1, keepdims=True) - zh * jnp.mean(dzh * zh, axis=-1, keepdims=True))
        dz1_ref[...] = dz1

        @pl.when(i == 0)
        def _():
            dlg_ref[...] = jnp.zeros_like(dlg_ref)
            dlb_ref[...] = jnp.zeros_like(dlb_ref)
            dbcf_ref[...] = jnp.zeros_like(dbcf_ref)

        dlg_ref[...] += jnp.sum(dl * zh, axis=0, keepdims=True)
        dlb_ref[...] += jnp.sum(dl, axis=0, keepdims=True)
        dbcf_ref[...] += jnp.sum(dz1, axis=0, keepdims=True)

    cur, prev, small = _mixer_specs(T, DIN, C1, wsc.shape[0], 1)
    tile = lambda: pl.BlockSpec((TT, C1), lambda i: (i, 0))
    vec = lambda: pl.BlockSpec((1, C1), lambda i: (0, 0))
    blk = _nbytes((TT + HALO, DIN), F32) + 5 * _nbytes((TT, C1), F32) + 12 * _nbytes((TT + HALO, C1), F32)
    return _pallas(
        body, name="mix_conv_bwd1", grid=(T // TT,),
        in_specs=[cur, prev, tile(), pl.BlockSpec((TT, 2 * C1), lambda i: (i, 0)), small[0], small[3], small[4]],
        out_specs=[tile(), tile(), tile(), vec(), vec(), vec()],
        out_shape=[jax.ShapeDtypeStruct((T, C1), F32), jax.ShapeDtypeStruct((T, C1), F32),
                   jax.ShapeDtypeStruct((T, C1), BF16)] + [jax.ShapeDtypeStruct((1, C1), F32)] * 3,
        scratch_shapes=[pltpu.VMEM((TT + HALO, C1), F32)],
        compiler_params=_params(("arbitrary",), blk),
    )(u, u, z1, dy, wsc, lg, lb)


def _mix_conv_bwd2(u, dz1, dcs, db, wsc, wcf):
    T, DIN = u.shape
    C1 = DIN // 5
    last = T // TT - 1
    ksc, kcf = wsc.shape[0], wcf.shape[0]

    def body(u_ref, up_ref, dz_ref, dzn_ref, dc_ref, dcn_ref, db_ref, wsc_ref, wcf_ref,
             du_ref, dbin_ref, dwsc_ref, dwcf_ref, zbuf, pbuf, dzbuf, dcbuf, zsh, dzsh):
        i = pl.program_id(0)
        mask_c, mask_p, has_next = _row_masks(i, last)
        _, c, v, a, sg = _conv_inputs(u_ref[...], up_ref[...], mask_c, mask_p, zbuf, pbuf, C1)
        dz1 = dz_ref[...]
        dcs = dc_ref[...]
        dzbuf[0:TT, :] = dz1
        dzbuf[TT:, :] = jnp.where(has_next, dzn_ref[...], 0.0)
        dcbuf[0:TT, :] = dcs
        dcbuf[TT:, :] = jnp.where(has_next, dcn_ref[...], 0.0)

        @pl.when(i == 0)
        def _():
            dbin_ref[...] = jnp.zeros_like(dbin_ref)
            dwsc_ref[...] = jnp.zeros_like(dwsc_ref)
            dwcf_ref[...] = jnp.zeros_like(dwcf_ref)

        _fill_shifted(zbuf, zsh)
        _fill_shifted(dzbuf, dzsh)
        _conv_weight_sums(dwcf_ref, dz1, zbuf, zsh)
        _conv_weight_sums(dwsc_ref, dcs, pbuf)
        dz0 = jnp.where(mask_c, _anticausal_conv(wcf_ref, dzbuf, dzsh), 0.0)
        dp = jnp.where(mask_c, _anticausal_conv(wsc_ref, dcbuf), 0.0)
        parts = (db_ref[...].astype(F32), dp * v, dp * c, dz0 * sg, dz0 * a * sg * (1.0 - sg))
        for k, part in enumerate(parts):
            du_ref[:, k * C1:(k + 1) * C1] = part.astype(BF16)
            dbin_ref[:, k * C1:(k + 1) * C1] += jnp.sum(part, axis=0, keepdims=True)

    cur, prev, small = _mixer_specs(T, DIN, C1, ksc, kcf)
    tile = lambda: pl.BlockSpec((TT, C1), lambda i: (i, 0))
    nxt = lambda: pl.BlockSpec((HALO, C1), lambda i: (jnp.minimum((i + 1) * (TT // HALO), T // HALO - 1), 0))
    blk = (_nbytes((TT + HALO, DIN), F32) + _nbytes((TT, DIN), BF16) + 5 * _nbytes((TT, C1), F32)
           + 16 * _nbytes((TT + HALO, C1), F32))
    buf = lambda: pltpu.VMEM((TT + HALO, C1), F32)
    return _pallas(
        body, name="mix_conv_bwd2", grid=(T // TT,),
        in_specs=[cur, prev, tile(), nxt(), tile(), nxt(), tile(), small[0], small[1]],
        out_specs=[pl.BlockSpec((TT, DIN), lambda i: (i, 0)), pl.BlockSpec((1, DIN), lambda i: (0, 0)),
                   pl.BlockSpec((ksc, C1), lambda i: (0, 0)), pl.BlockSpec((kcf, C1), lambda i: (0, 0))],
        out_shape=[jax.ShapeDtypeStruct((T, DIN), BF16), jax.ShapeDtypeStruct((1, DIN), F32),
                   jax.ShapeDtypeStruct((ksc, C1), F32), jax.ShapeDtypeStruct((kcf, C1), F32)],
        scratch_shapes=[buf(), buf(), buf(), buf(), _shifted_scratch(C1), _shifted_scratch(C1)],
        compiler_params=_params(("arbitrary",), blk),
    )(u, u, dz1, dz1, dcs, dcs, db, wsc, wcf)


def _place():
    x, y, c = lax.axis_index("x"), lax.axis_index("y"), lax.axis_index("c")
    chips = [(1 - x, y), (x, 1 - y), (1 - x, 1 - y)]
    return x, y, c, chips


ANY = pl.BlockSpec(memory_space=pl.ANY)


def _cast_own_block(place, w, name):
    R, C = w.shape
    tr = _row_tile(R // 2, 512, 16)
    nblk = R // 2 // tr

    def body(place_ref, w_ref, o_ref):
        o_ref[...] = w_ref[...].astype(BF16)

    return _pallas(
        body, name=name,
        grid_spec=pltpu.PrefetchScalarGridSpec(
            num_scalar_prefetch=1, grid=(2, nblk),
            in_specs=[pl.BlockSpec((tr, C), lambda h, i, p: (h * nblk + i, 0))],
            out_specs=pl.BlockSpec((None, None, tr, C), lambda h, i, p: (p[0], h, i, 0))),
        out_shape=jax.ShapeDtypeStruct((N_CHIPS, 2, R // 2, C), BF16),
        compiler_params=_params(("parallel", "parallel"), _nbytes((tr, C), F32) + _nbytes((tr, C), BF16)),
    )(place, w)


HBM = pl.BlockSpec(memory_space=pltpu.HBM)
SEM = pl.BlockSpec(memory_space=pltpu.SEMAPHORE)
EFFECT = pltpu.SideEffectType.DATAFLOW_SIDE_EFFECTING


def _gather_copies(refs, send, recv, rels=(0, 1, 2)):
    x, y, c, chips = _place()
    s = 2 * x + y
    n = len(rels)
    return [pltpu.make_async_remote_copy(src_ref=ref.at[s, c], dst_ref=ref.at[s, c], send_sem=send.at[n * w + k],
                                         recv_sem=recv.at[n * w + k], device_id=(*chips[r], c), device_id_type=MESH)
            for w, ref in enumerate(refs) for k, r in enumerate(rels)]


def _scatter_copies(refs, send, recv):
    x, y, c, chips = _place()
    nw = len(refs) // 2
    return [pltpu.make_async_remote_copy(src_ref=refs[w].at[2 * tx + ty], dst_ref=refs[nw + w].at[r],
                                         send_sem=send.at[3 * w + r], recv_sem=recv.at[3 * w + r],
                                         device_id=(tx, ty, c), device_id_type=MESH)
            for w in range(nw) for r, (tx, ty) in enumerate(chips)]


def _pair_copies(refs, send, recv):
    x, y, c, _ = _place()
    nw = len(refs) // 2
    return [pltpu.make_async_remote_copy(src_ref=refs[w].at[j, 1 - c], dst_ref=refs[nw + w].at[j],
                                         send_sem=send.at[N_CHIPS * w + j], recv_sem=recv.at[N_CHIPS * w + j],
                                         device_id=(x, y, 1 - c), device_id_type=MESH)
            for w in range(nw) for j in range(N_CHIPS)]


def _forward_copies(refs, send, recv, rels=(0, 1, 2)):
    x, y, c, chips = _place()
    n = len(rels)
    copies = []
    for w, ref in enumerate(refs):
        for k, r in enumerate(rels):
            tx, ty = chips[r]
            blk = ref.at[2 * tx + ty, c]
            copies.append(pltpu.make_async_remote_copy(src_ref=blk, dst_ref=blk, send_sem=send.at[n * w + k],
                                                       recv_sem=recv.at[n * w + k], device_id=(x, y, 1 - c),
                                                       device_id_type=MESH))
    return copies


def _half_copies(refs, send, recv):
    x, y, c, _ = _place()
    return [pltpu.make_async_remote_copy(src_ref=ref.at[c], dst_ref=ref.at[c], send_sem=send.at[w], recv_sem=recv.at[w],
                                         device_id=(x, y, 1 - c), device_id_type=MESH)
            for w, ref in enumerate(refs)]


def _start_copies(bufs, after, ncopies, make_copies, name):
    n = len(bufs)

    def body(*refs):
        in_refs, send, recv, token = refs[:n], refs[n + 1], refs[n + 2], refs[2 * n + 3]
        for cp in make_copies(in_refs, send, recv):
            cp.start()
        token[...] = jnp.zeros_like(token)

    outs = _pallas(
        body, name=name, in_specs=[HBM] * n + [ANY],
        out_specs=[SEM, SEM] + [HBM] * n + [pl.BlockSpec(memory_space=pltpu.VMEM)],
        out_shape=[pltpu.SemaphoreType.DMA((ncopies,)), pltpu.SemaphoreType.DMA((ncopies,))]
                  + [pltpu.HBM(b.shape, b.dtype) for b in bufs] + [jax.ShapeDtypeStruct((8, 128), F32)],
        input_output_aliases={k: 2 + k for k in range(n)},
        compiler_params=pltpu.CompilerParams(has_side_effects=EFFECT),
    )(*[pltpu.with_memory_space_constraint(b, pltpu.HBM) for b in bufs], after)
    return outs[0], outs[1], list(outs[2:2 + n]), outs[2 + n]


def _wait_copies(send, recv, bufs, after, make_copies, name):
    n = len(bufs)

    def body(*refs):
        in_refs, send_ref, recv_ref = refs[:n], refs[n], refs[n + 1]
        for cp in make_copies(in_refs, send_ref, recv_ref):
            cp.wait_send()
            cp.wait_recv()

    outs = _pallas(
        body, name=name, in_specs=[HBM] * n + [SEM, SEM, ANY], out_specs=[HBM] * n,
        out_shape=[pltpu.HBM(b.shape, b.dtype) for b in bufs],
        input_output_aliases={k: k for k in range(n)},
        compiler_params=pltpu.CompilerParams(has_side_effects=EFFECT),
    )(*bufs, send, recv, after)
    return list(outs)


def _forward_halves(bufs, name, rels=(0, 1, 2)):
    nw = len(bufs)
    n = len(rels)

    def body(*refs):
        o_refs = refs[nw:2 * nw]
        send, recv = refs[2 * nw:]
        x, y, c, chips = _place()
        sib = (x, y, 1 - c)
        copies = []
        for w in range(nw):
            for k, r in enumerate(rels):
                tx, ty = chips[r]
                ref = o_refs[w].at[2 * tx + ty, c]
                cp = pltpu.make_async_remote_copy(src_ref=ref, dst_ref=ref, send_sem=send.at[n * w + k],
                                                  recv_sem=recv.at[n * w + k], device_id=sib, device_id_type=MESH)
                cp.start()
                copies.append(cp)
        for w in range(nw):
            for k, r in enumerate(rels):
                tx, ty = chips[r]
                ref = o_refs[w].at[2 * tx + ty, 1 - c]
                pltpu.make_async_remote_copy(src_ref=ref, dst_ref=ref, send_sem=send.at[n * w + k],
                                             recv_sem=recv.at[n * w + k], device_id=sib, device_id_type=MESH).wait_recv()
        for cp in copies:
            cp.wait_send()

    return _pallas(
        body, name=name, in_specs=[ANY] * nw, out_specs=[ANY] * nw,
        out_shape=[jax.ShapeDtypeStruct(b.shape, b.dtype) for b in bufs],
        input_output_aliases={w: w for w in range(nw)},
        scratch_shapes=[pltpu.SemaphoreType.DMA((n * nw,)), pltpu.SemaphoreType.DMA((n * nw,))],
    )(*bufs)


def _share_small(v, reduce, name, after):
    R, C = v.shape

    def body(v_ref, after_ref, o_ref, *scratch):
        if reduce:
            all_ref, send, recv, lsem = scratch
        else:
            all_ref = o_ref
            send, recv, lsem = scratch
        x, y, c, _ = _place()
        me = 4 * x + 2 * y + c
        loc = pltpu.make_async_copy(v_ref, all_ref.at[me], lsem)
        loc.start()
        copies = []
        for k in range(1, N_DEV):
            kx, ky, kc = (k >> 2) & 1, (k >> 1) & 1, k & 1
            peer = (x ^ kx, y ^ ky, c ^ kc)
            cp = pltpu.make_async_remote_copy(src_ref=v_ref, dst_ref=all_ref.at[me], send_sem=send.at[k - 1],
                                              recv_sem=recv.at[k - 1], device_id=peer, device_id_type=MESH)
            cp.start()
            copies.append(cp)
        for k in range(1, N_DEV):
            kx, ky, kc = (k >> 2) & 1, (k >> 1) & 1, k & 1
            src = 4 * (x ^ kx) + 2 * (y ^ ky) + (c ^ kc)
            pltpu.make_async_remote_copy(src_ref=v_ref, dst_ref=all_ref.at[src], send_sem=send.at[k - 1],
                                         recv_sem=recv.at[k - 1], device_id=(x, y, c), device_id_type=MESH).wait_recv()
        for cp in copies:
            cp.wait_send()
        loc.wait()
        if reduce:
            total = all_ref[0]
            for d in range(1, N_DEV):
                total = total + all_ref[d]
            o_ref[...] = total

    vm = pl.BlockSpec(memory_space=pltpu.VMEM)
    sems = [pltpu.SemaphoreType.DMA((N_DEV - 1,)), pltpu.SemaphoreType.DMA((N_DEV - 1,)), pltpu.SemaphoreType.DMA]
    if reduce:
        out_shape = jax.ShapeDtypeStruct((R, C), F32)
        scratch = [pltpu.VMEM((N_DEV, R, C), F32)] + sems
    else:
        out_shape = jax.ShapeDtypeStruct((N_DEV, R, C), F32)
        scratch = sems
    return _pallas(
        body, name=name, in_specs=[vm, ANY], out_specs=vm, out_shape=out_shape, scratch_shapes=scratch,
        compiler_params=pltpu.CompilerParams(vmem_limit_bytes=int(min(4 * N_DEV * R * C * 4 + 2 ** 24, 2 ** 25 + 2 ** 24))),
    )(v, after)


def _pair_sum(place, g, rb, name):
    _, _, Rh, C = g.shape
    tr = _row_tile(Rh, 512, 16)

    def body(place_ref, g_ref, r_ref, q_ref):
        q_ref[...] = (g_ref[...] + r_ref[...]).astype(BF16)

    blk = 2 * _nbytes((tr, C), F32) + _nbytes((tr, C), BF16)
    return _pallas(
        body, name=name,
        grid_spec=pltpu.PrefetchScalarGridSpec(
            num_scalar_prefetch=1, grid=(N_CHIPS - 1, Rh // tr),
            in_specs=[pl.BlockSpec((None, None, tr, C), lambda j, i, p: (p[0] ^ (j + 1), p[1], i, 0)),
                      pl.BlockSpec((None, tr, C), lambda j, i, p: (p[0] ^ (j + 1), i, 0))],
            out_specs=pl.BlockSpec((None, tr, C), lambda j, i, p: (p[0] ^ (j + 1), i, 0))),
        out_shape=jax.ShapeDtypeStruct((N_CHIPS, Rh, C), BF16),
        compiler_params=_params(("parallel", "parallel"), blk),
    )(place, g, rb)


def _chip_sum(place, g, rb, rc, name):
    _, _, Rh, C = g.shape
    tr = _row_tile(Rh, 512, 16)

    def body(place_ref, g_ref, r_ref, rc_ref, o_ref):
        total = g_ref[...] + r_ref[...]
        for r in range(3):
            total = total + rc_ref[r].astype(F32)
        o_ref[...] = total

    blk = 3 * _nbytes((tr, C), F32) + 3 * _nbytes((tr, C), BF16)
    return _pallas(
        body, name=name,
        grid_spec=pltpu.PrefetchScalarGridSpec(
            num_scalar_prefetch=1, grid=(Rh // tr,),
            in_specs=[pl.BlockSpec((None, None, tr, C), lambda i, p: (p[0], p[1], i, 0)),
                      pl.BlockSpec((None, tr, C), lambda i, p: (p[0], i, 0)),
                      pl.BlockSpec((3, tr, C), lambda i, p: (0, i, 0))],
            out_specs=pl.BlockSpec((None, tr, C), lambda i, p: (p[1], i, 0))),
        out_shape=jax.ShapeDtypeStruct((2, Rh, C), F32),
        compiler_params=_params(("parallel",), blk),
    )(place, g, rb, rc)


def _adamw_math(w, g, m, v):
    m = ADAM_B1 * m + (1.0 - ADAM_B1) * g
    v = ADAM_B2 * v + (1.0 - ADAM_B2) * jnp.square(g)
    m_hat = m / (1.0 - ADAM_B1 ** ADAM_STEP)
    v_hat = v / (1.0 - ADAM_B2 ** ADAM_STEP)
    delta = -ADAM_LR * (m_hat / (jnp.sqrt(v_hat) + ADAM_EPS) + ADAM_WD * w)
    return delta, m, v


def _adamw(w, g, m, v, name):
    R, C = w.shape
    tr = _row_tile(R, 512)

    def body(w_ref, g_ref, m_ref, v_ref, go_ref, d_ref, nm_ref, nv_ref):
        gv = g_ref[...]
        d, nm, nv = _adamw_math(w_ref[...], gv, m_ref[...], v_ref[...])
        go_ref[...] = gv
        d_ref[...] = d
        nm_ref[...] = nm
        nv_ref[...] = nv

    spec = pl.BlockSpec((tr, C), lambda i: (i, 0))
    shp = jax.ShapeDtypeStruct((R, C), F32)
    return _pallas(
        body, name=name, grid=(R // tr,), in_specs=[spec] * 4, out_specs=[spec] * 4, out_shape=[shp] * 4,
        compiler_params=_params(("parallel",), 8 * _nbytes((tr, C), F32)),
    )(w, g, m, v)


def _adamw_small(ws, gs, ms, vs):
    n = len(ws)

    def body(*refs):
        for k in range(n):
            w_ref, g_ref, m_ref, v_ref = (refs[q * n + k] for q in range(4))
            d, nm, nv = _adamw_math(w_ref[...], g_ref[...], m_ref[...], v_ref[...])
            refs[4 * n + k][...] = d
            refs[5 * n + k][...] = nm
            refs[6 * n + k][...] = nv

    vm = pl.BlockSpec(memory_space=pltpu.VMEM)
    shapes = [jax.ShapeDtypeStruct(w.shape, F32) for w in ws]
    outs = _pallas(
        body, name="adamw_small", in_specs=[vm] * (4 * n), out_specs=[vm] * (3 * n), out_shape=shapes * 3,
    )(*ws, *gs, *ms, *vs)
    return outs[:n], outs[n:2 * n], outs[2 * n:]


def _pad_rows(a, rows):
    return jnp.pad(a, ((0, rows - a.shape[0]), (0, 0)))


def kernel(x, meta_tokens, ffn1_norm, ffn1_w_gate, ffn1_w_up, ffn1_w_down, mix_norm, w_in, b_in, conv_sc_w, conv_cf_w, conv_cf_b, ln_cf_g, ln_cf_b, w_out, ffn2_norm, ffn2_w_gate, ffn2_w_up, ffn2_w_down, final_norm, loss_target, m_meta_tokens, m_ffn1_norm, m_ffn1_w_gate, m_ffn1_w_up, m_ffn1_w_down, m_mix_norm, m_w_in, m_b_in, m_conv_sc_w, m_conv_cf_w, m_conv_cf_b, m_ln_cf_g, m_ln_cf_b, m_w_out, m_ffn2_norm, m_ffn2_w_gate, m_ffn2_w_up, m_ffn2_w_down, m_final_norm, v_meta_tokens, v_ffn1_norm, v_ffn1_w_gate, v_ffn1_w_up, v_ffn1_w_down, v_mix_norm, v_w_in, v_b_in, v_conv_sc_w, v_conv_cf_w, v_conv_cf_b, v_ln_cf_g, v_ln_cf_b, v_w_out, v_ffn2_norm, v_ffn2_w_gate, v_ffn2_w_up, v_ffn2_w_down, v_final_norm):
    xi, yi, ci = lax.axis_index("x"), lax.axis_index("y"), lax.axis_index("c")
    chip = 2 * xi + yi
    place = jnp.stack([chip, ci]).astype(jnp.int32)

    x2 = x[0]
    tgt = loss_target[0]
    S, D = x2.shape
    C1 = D // 2
    cs = conv_sc_w.shape[2]
    ksc, kcf = conv_sc_w.shape[1], conv_cf_w.shape[1]
    ms = meta_tokens.shape[1]

    big = {"ffn1_w_gate": ffn1_w_gate, "ffn1_w_up": ffn1_w_up, "ffn1_w_down": ffn1_w_down, "w_in": w_in, "w_out": w_out,
           "ffn2_w_gate": ffn2_w_gate, "ffn2_w_up": ffn2_w_up, "ffn2_w_down": ffn2_w_down}
    big_m = {"ffn1_w_gate": m_ffn1_w_gate, "ffn1_w_up": m_ffn1_w_up, "ffn1_w_down": m_ffn1_w_down, "w_in": m_w_in,
             "w_out": m_w_out, "ffn2_w_gate": m_ffn2_w_gate, "ffn2_w_up": m_ffn2_w_up, "ffn2_w_down": m_ffn2_w_down}
    big_v = {"ffn1_w_gate": v_ffn1_w_gate, "ffn1_w_up": v_ffn1_w_up, "ffn1_w_down": v_ffn1_w_down, "w_in": v_w_in,
             "w_out": v_w_out, "ffn2_w_gate": v_ffn2_w_gate, "ffn2_w_up": v_ffn2_w_up, "ffn2_w_down": v_ffn2_w_down}
    buf = {nm: _cast_own_block(place, w[0], "cast_" + nm) for nm, w in big.items()}
    whole_weight = lambda g: g.reshape(N_CHIPS, 2 * g.shape[2], g.shape[3])
    corner = lambda a: a.reshape(-1, a.shape[-1])[:8, :128]

    NEAR, FAR = (0, 1), (2,)
    groups = {"ffn1_near": (["ffn1_w_gate", "ffn1_w_up", "ffn1_w_down"], NEAR),
              "ffn1_far": (["ffn1_w_gate", "ffn1_w_up", "ffn1_w_down"], FAR),
              "mix": (["w_in", "w_out"], NEAR + FAR),
              "ffn2_up": (["ffn2_w_gate", "ffn2_w_up"], NEAR + FAR),
              "ffn2_down": (["ffn2_w_down"], NEAR + FAR)}
    started = {}

    def start(tag, after):
        nms, rels = groups[tag]
        copies = functools.partial(_gather_copies, rels=rels)
        send, recv, thru, token = _start_copies([buf[nm] for nm in nms], after, len(rels) * len(nms), copies,
                                                "gather_start_" + tag)
        for nm, b in zip(nms, thru):
            buf[nm] = b
        started[tag] = (send, recv, copies)
        return token

    def arrive(tag, after, then=None):
        nms, rels = groups[tag]
        send, recv, copies = started[tag]
        got = _wait_copies(send, recv, [buf[nm] for nm in nms], corner(after), copies, "gather_wait_" + tag)
        for nm, b in zip(nms, got):
            buf[nm] = b
        if then is not None:
            start(then, corner(got[0]))
        for nm, b in zip(nms, _forward_halves([buf[nm] for nm in nms], "gather_forward_" + tag, rels)):
            buf[nm] = b

    def passing(tag, after):
        nms, rels = groups[tag]
        send, recv, copies = started[tag]
        got = _wait_copies(send, recv, [buf[nm] for nm in nms], corner(after), copies, "gather_wait_" + tag)
        copies = functools.partial(_forward_copies, rels=rels)
        send, recv, thru, token = _start_copies(got, corner(got[0]), len(rels) * len(nms), copies,
                                                "gather_pass_" + tag)
        for nm, b in zip(nms, thru):
            buf[nm] = b
        started[tag] = (send, recv, copies)
        return token

    def passed(tag, after):
        nms, _ = groups[tag]
        send, recv, copies = started[tag]
        for nm, b in zip(nms, _wait_copies(send, recv, [buf[nm] for nm in nms], corner(after), copies,
                                           "gather_passed_" + tag)):
            buf[nm] = b

    tokens = lambda *arrays: jnp.concatenate([corner(a).astype(F32) for a in arrays], axis=0)
    assert ksc <= 8 and kcf <= 32 and cs <= ms
    pack = jnp.concatenate([
        meta_tokens,
        jnp.pad(conv_sc_w[0], ((0, 8 - ksc), (0, ms - cs))),
        jnp.pad(conv_cf_w[0], ((0, 32 - kcf), (0, ms - cs)))], axis=0)
    everyone = _share_small(pack, False, "share_params", pack)[0::2]
    meta_full = jnp.transpose(everyone[:, :N_META, :], (1, 0, 2)).reshape(N_META, D)
    wsc_full = jnp.transpose(everyone[:, N_META:N_META + ksc, :cs], (1, 0, 2)).reshape(ksc, C1)
    wcf_full = jnp.transpose(everyone[:, N_META + 8:N_META + 8 + kcf, :cs], (1, 0, 2)).reshape(kcf, C1)

    token = start("ffn1_near", corner(everyone))

    ffn1 = lambda: [whole_weight(buf[nm]) for nm in ["ffn1_w_gate", "ffn1_w_up", "ffn1_w_down"]]
    own = chip[None].astype(jnp.int32)
    near = jnp.stack([chip ^ 2, chip ^ 1]).astype(jnp.int32)
    far = (chip ^ 3)[None].astype(jnp.int32)
    all_chips = jnp.arange(N_CHIPS, dtype=jnp.int32)

    hs0, n1 = _embed_rms(x2, meta_full, ffn1_norm)
    wg1, wu1, wd1 = ffn1()
    gua = _ffn_up(n1, wg1, wu1, own, None, token, "ffn1_up_own")
    hs1 = _ffn_down(gua[2], wd1, hs0, own, "ffn1_down_own")
    later = [buf[nm] for nm in ["w_in", "w_out", "ffn2_w_gate", "ffn2_w_up", "ffn2_w_down"]]
    arrive("ffn1_near", tokens(hs1, *later), "ffn1_far")
    wg1, wu1, wd1 = ffn1()
    gua = _ffn_up(n1, wg1, wu1, near, gua, token, "ffn1_up_near")
    tok = start("mix", corner(gua[2]))
    tok = passing("ffn1_far", tok)
    hs1 = _ffn_down(gua[2], ffn1()[2], hs1, near, "ffn1_down_near", tok)
    passed("ffn1_far", hs1)
    wg1, wu1, wd1 = ffn1()
    g1, u1, a1 = _ffn_up(n1, wg1, wu1, far, gua, token, "ffn1_up_far")
    tok = passing("mix", a1)
    hs1 = _ffn_down(a1, wd1, hs1, far, "ffn1_down_far", tok)
    F = N_CHIPS * wd1.shape[1]
    tok = start("ffn2_up", corner(hs1))
    passed("mix", tok)
    win, wout = whole_weight(buf["w_in"]), whole_weight(buf["w_out"])
    n2 = _rms(hs1, mix_norm, "rms_mix")
    u = _mix_in(n2, win, b_in)
    y, z1 = _mix_conv_fwd(u, wsc_full, wcf_full, conv_cf_b, ln_cf_g, ln_cf_b)
    tok = start("ffn2_down", corner(y))
    tok = passing("ffn2_up", tok)
    hs2 = _mix_out(y, wout.reshape(D, D), hs1, tok)
    passed("ffn2_up", hs2)
    wg2, wu2 = whole_weight(buf["ffn2_w_gate"]), whole_weight(buf["ffn2_w_up"])
    n3 = _rms(hs2, ffn2_norm, "rms_ffn2")
    g2, u2, a2 = _ffn_up(n3, wg2, wu2, all_chips, None, token, "ffn2_up")
    passed("ffn2_down", passing("ffn2_down", a2))
    wd2 = whole_weight(buf["ffn2_w_down"])
    hs3 = _ffn_down_whole(a2, wd2.reshape(F, D), hs2, "ffn2_down")
    token_ffn2 = token

    def pair_start(group, after, tag):
        gs = [g for _, g in group]
        lands = [lax.empty((N_CHIPS,) + g.shape[2:], F32) for g in gs]
        send, recv, thru, token = _start_copies(gs + lands, after, N_CHIPS * len(gs), _pair_copies,
                                                "pair_start_" + tag)
        return (group, send, recv, thru, tag), token

    def scatter_start(state, after):
        group, send, recv, thru, tag = state
        thru = _wait_copies(send, recv, thru, corner(after), _pair_copies, "pair_wait_" + tag)
        gs, sib = thru[:len(group)], thru[len(group):]
        sums = [_pair_sum(place, g, rb, "pair_sum_" + nm) for (nm, _), g, rb in zip(group, gs, sib)]
        lands = [lax.empty((3,) + q.shape[1:], BF16) for q in sums]
        send, recv, thru, token = _start_copies(sums + lands, corner(sums[-1]), 3 * len(gs), _scatter_copies,
                                                "scatter_start_" + tag)
        return ([(nm, g) for (nm, _), g in zip(group, gs)], sib, send, recv, thru, tag), token

    def finish_sum(state, after):
        group, sib, send, recv, thru, tag = state
        lands = _wait_copies(send, recv, thru, corner(after), _scatter_copies, "scatter_wait_" + tag)[len(group):]
        mine = [_chip_sum(place, g, rb, rc, "chip_sum_" + nm) for (nm, g), rb, rc in zip(group, sib, lands)]
        send, recv, thru, token = _start_copies(mine, corner(mine[-1]), len(mine), _half_copies, "half_start_" + tag)
        return (group, send, recv, thru, tag), token

    def finish_adam(state, after):
        group, send, recv, thru, tag = state
        whole = _wait_copies(send, recv, thru, corner(after), _half_copies, "half_wait_" + tag)
        out = {}
        for (nm, _), g in zip(group, whole):
            w = big[nm]
            g_out, d, new_m, new_v = _adamw(w[0], g.reshape(w.shape[1:]), big_m[nm][0], big_v[nm][0], "adamw_" + nm)
            out[nm] = (g_out[None], d[None], new_m[None], new_v[None])
        return out

    dhs3, df2, loss_row, d_final = _final_loss(hs3, final_norm.reshape(1, D), tgt)

    dg2, du2 = _ffn_bwd_act(df2, wd2.reshape(F, D), g2, u2, token_ffn2, "ffn2_bwd_act")
    gw_d2 = _wgrad_down(a2, df2, "wgrad_ffn2_down")
    gw_g2 = _wgrad_cols(n3, [dg2], "wgrad_ffn2_gate")[0]
    gw_u2 = _wgrad_cols(n3, [du2], "wgrad_ffn2_up")[0]
    pair_ffn2, token = pair_start([("ffn2_w_gate", gw_g2), ("ffn2_w_up", gw_u2), ("ffn2_w_down", gw_d2)],
                                  corner(gw_u2), "ffn2")
    dn3 = _nt_panel([dg2, du2], [wg2, wu2], token, "ffn2_bwd_in")
    red_ffn2, token = scatter_start(pair_ffn2, dn3)
    dhs2, dm, d_ffn2 = _rms_bwd(dn3, hs2, ffn2_norm, dhs3, 1.0, "rms_bwd_ffn2")

    dy = _nt_panel([dm], [wout.reshape(1, D, D)], token, "mix_bwd_out")
    gw_out = _wgrad_out(y, dm)
    dz1, dcs, db, d_lg, d_lb, d_bcf = _mix_conv_bwd1(u, z1, dy, wsc_full, ln_cf_g, ln_cf_b)
    du, d_bin, d_wsc, d_wcf = _mix_conv_bwd2(u, dz1, dcs, db, wsc_full, wcf_full)
    gw_in = _wgrad_cols(n2, [du], "wgrad_w_in")[0]
    pair_mix, token = pair_start([("w_in", gw_in), ("w_out", gw_out)], corner(gw_in), "mix")
    dn2 = _nt_panel([du], [win], token, "mix_bwd_in")
    red_mix, token = scatter_start(pair_mix, dn2)
    dhs1, df1, d_mix = _rms_bwd(dn2, hs1, mix_norm, dhs2, FFN_RES_SCALE, "rms_bwd_mix")

    dg1, du1 = _ffn_bwd_act(df1, wd1.reshape(F, D), g1, u1, token, "ffn1_bwd_act")
    gw_d1 = _wgrad_down(a1, df1, "wgrad_ffn1_down")
    gw_g1 = _wgrad_cols(n1, [dg1], "wgrad_ffn1_gate")[0]
    pair_ffn1a, token = pair_start([("ffn1_w_down", gw_d1), ("ffn1_w_gate", gw_g1)], corner(gw_g1), "ffn1a")
    gw_u1 = _wgrad_cols(n1, [du1], "wgrad_ffn1_up", token)[0]
    red_ffn1a, token = scatter_start(pair_ffn1a, gw_u1)
    pair_ffn1b, token = pair_start([("ffn1_w_up", gw_u1)], token, "ffn1b")
    dn1 = _nt_panel([dg1, du1], [wg1, wu1], token, "ffn1_bwd_in")
    red_ffn1b, token = scatter_start(pair_ffn1b, dn1)
    grad_x, d_meta, d_ffn1 = _rms_bwd_first(dn1, hs0, ffn1_norm, dhs1, token)

    half_ffn2, tok = finish_sum(red_ffn2, grad_x)
    half_mix, tok = finish_sum(red_mix, tok)
    big_out = finish_adam(half_ffn2, tok)
    half_ffn1a, tok = finish_sum(red_ffn1a, big_out["ffn2_w_down"][1])
    big_out.update(finish_adam(half_mix, tok))
    half_ffn1b, tok = finish_sum(red_ffn1b, big_out["w_out"][1])
    big_out.update(finish_adam(half_ffn1a, tok))
    big_out.update(finish_adam(half_ffn1b, big_out["ffn1_w_gate"][1]))

    W = C1
    rows = lambda a: a.reshape(-1, W)
    parts = [rows(d_ffn1), rows(d_mix), rows(d_ffn2), rows(d_final), rows(d_bin), d_bcf, d_lg, d_lb,
             d_wsc, d_wcf, rows(d_meta), jnp.broadcast_to(loss_row[:, :1], (1, W))]
    sizes = [p.shape[0] for p in parts]
    total_rows = sum(sizes)
    packed = _pad_rows(jnp.concatenate(parts, axis=0), -(-total_rows // 8) * 8)
    summed = _share_small(packed, True, "sum_small", big_out["ffn1_w_up"][1])
    offs = [0]
    for n in sizes:
        offs.append(offs[-1] + n)
    piece = lambda k: summed[offs[k]:offs[k + 1]]
    loss = piece(11)[0, 0]
    g_ffn1, g_mix, g_ffn2 = (piece(k).reshape(1, D) for k in range(3))
    g_final = piece(3).reshape(1, D)
    g_bin = piece(4).reshape(1, -1)
    g_bcf, g_lg, g_lb = piece(5), piece(6), piece(7)
    g_wsc = lax.dynamic_slice_in_dim(piece(8), chip * cs, cs, axis=1)
    g_wcf = lax.dynamic_slice_in_dim(piece(9), chip * cs, cs, axis=1)
    g_meta = lax.dynamic_slice_in_dim(piece(10).reshape(N_META, D), chip * ms, ms, axis=1)

    small_names = ["meta_tokens", "ffn1_norm", "mix_norm", "b_in", "conv_sc_w", "conv_cf_w", "conv_cf_b", "ln_cf_g",
                   "ln_cf_b", "ffn2_norm", "final_norm"]
    small_w = [meta_tokens, ffn1_norm, mix_norm, b_in, conv_sc_w[0], conv_cf_w[0], conv_cf_b, ln_cf_g, ln_cf_b,
               ffn2_norm, final_norm.reshape(1, D)]
    small_g = [g_meta, g_ffn1, g_mix, g_bin, g_wsc, g_wcf, g_bcf, g_lg, g_lb, g_ffn2, g_final]
    small_m = [m_meta_tokens, m_ffn1_norm, m_mix_norm, m_b_in, m_conv_sc_w[0], m_conv_cf_w[0], m_conv_cf_b, m_ln_cf_g,
               m_ln_cf_b, m_ffn2_norm, m_final_norm.reshape(1, D)]
    small_v = [v_meta_tokens, v_ffn1_norm, v_mix_norm, v_b_in, v_conv_sc_w[0], v_conv_cf_w[0], v_conv_cf_b, v_ln_cf_g,
               v_ln_cf_b, v_ffn2_norm, v_final_norm.reshape(1, D)]
    s_d, s_m, s_v = _adamw_small(small_w, small_g, small_m, small_v)
    shapes = {"conv_sc_w": conv_sc_w.shape, "conv_cf_w": conv_cf_w.shape, "final_norm": final_norm.shape}
    small_out = {}
    for nm, g, d, m, v in zip(small_names, small_g, s_d, s_m, s_v):
        shp = shapes.get(nm, g.shape)
        small_out[nm] = tuple(t.reshape(shp) for t in (g, d, m, v))

    order = ["meta_tokens", "ffn1_norm", "ffn1_w_gate", "ffn1_w_up", "ffn1_w_down", "mix_norm", "w_in", "b_in",
             "conv_sc_w", "conv_cf_w", "conv_cf_b", "ln_cf_g", "ln_cf_b", "w_out", "ffn2_norm", "ffn2_w_gate",
             "ffn2_w_up", "ffn2_w_down", "final_norm"]
    res = {**big_out, **small_out}
    outs = [loss, grad_x[None]]
    for q in range(4):
        outs.extend(res[nm][q] for nm in order)
    return tuple(outs)
```

```python
import functools

import jax
import jax.numpy as jnp
from jax import lax
from jax.experimental import pallas as pl
from jax.experimental.pallas import tpu as pltpu

F32 = jnp.float32
BF16 = jnp.bfloat16
MESH = pl.DeviceIdType.MESH

N_META = 16
TT = 128
PAD = TT - N_META
HALO = 32
EPS = 1e-6
FFN_RES_SCALE = 0.5
N_CHIPS = 4
N_DEV = 8

ADAM_LR = 0.001
ADAM_B1 = 0.9
ADAM_B2 = 0.999
ADAM_EPS = 1e-08
ADAM_WD = 0.01
ADAM_STEP = 10

V7X_VMEM_BYTES = 64 * 2 ** 20
NT_DIMS = (((1,), (1,)), ((), ()))
TN_DIMS = (((0,), (0,)), ((), ()))


def _params(semantics, block_bytes):
    limit = min(2 * block_bytes + 16 * 2 ** 20, V7X_VMEM_BYTES - 6 * 2 ** 20)
    return pltpu.CompilerParams(dimension_semantics=semantics, vmem_limit_bytes=int(limit))


def _pallas(body, out_shape, **kw):
    if "grid" not in kw and "grid_spec" not in kw:
        return pl.pallas_call(body, out_shape=out_shape, **kw)
    big = lambda shape, dtype: jnp.issubdtype(dtype, jnp.floating) and len(shape) >= 2
    pin_out = lambda s: pltpu.HBM(s.shape, s.dtype) if big(s.shape, s.dtype) else s
    single = not isinstance(out_shape, (list, tuple))
    shapes = pin_out(out_shape) if single else [pin_out(s) for s in out_shape]
    call = pl.pallas_call(body, out_shape=shapes, **kw)
    pin = lambda a: pltpu.with_memory_space_constraint(a, pltpu.HBM) if big(a.shape, a.dtype) else a
    return lambda *operands: call(*[pin(a) for a in operands])


def _nbytes(shape, dtype):
    n = 1
    for d in shape:
        if d is not None:
            n *= d
    return n * jnp.dtype(dtype).itemsize


def _row_tile(rows, target, mult=8):
    best = None
    for t in range(mult, min(rows, target) + 1, mult):
        if rows % t == 0:
            best = t
    assert best is not None, (rows, target, mult)
    return best


def _sigmoid(v):
    return jax.nn.sigmoid(v)


def _dsilu(v, s):
    return s * (1.0 + v * (1.0 - s))


def _embed_rms(x2, meta, gain):
    S, D = x2.shape
    T = S + TT

    def body(x_ref, meta_ref, g_ref, hs_ref, n_ref):
        i = pl.program_id(0)

        @pl.when(i == 0)
        def _():
            hs_ref[...] = jnp.zeros_like(hs_ref)
            hs_ref[PAD:, :] = meta_ref[...]

        @pl.when(i > 0)
        def _():
            hs_ref[...] = x_ref[...]

        h = hs_ref[...]
        r = lax.rsqrt(jnp.mean(h * h, axis=-1, keepdims=True) + EPS)
        n_ref[...] = ((h * r) * g_ref[...]).astype(BF16)

    blk = _nbytes((TT, D), F32) * 2 + _nbytes((TT, D), BF16)
    return _pallas(
        body, name="embed_rms", grid=(T // TT,),
        in_specs=[pl.BlockSpec((TT, D), lambda i: (jnp.maximum(i - 1, 0), 0)),
                  pl.BlockSpec((N_META, D), lambda i: (0, 0)),
                  pl.BlockSpec((1, D), lambda i: (0, 0))],
        out_specs=[pl.BlockSpec((TT, D), lambda i: (i, 0)), pl.BlockSpec((TT, D), lambda i: (i, 0))],
        out_shape=[jax.ShapeDtypeStruct((T, D), F32), jax.ShapeDtypeStruct((T, D), BF16)],
        compiler_params=_params(("parallel",), blk),
    )(x2, meta, gain)


def _rms(hs, gain, name):
    T, D = hs.shape
    te = _row_tile(T, 384)

    def body(h_ref, g_ref, n_ref):
        h = h_ref[...]
        r = lax.rsqrt(jnp.mean(h * h, axis=-1, keepdims=True) + EPS)
        n_ref[...] = ((h * r) * g_ref[...]).astype(BF16)

    blk = _nbytes((te, D), F32) + _nbytes((te, D), BF16)
    return _pallas(
        body, name=name, grid=(T // te,),
        in_specs=[pl.BlockSpec((te, D), lambda i: (i, 0)), pl.BlockSpec((1, D), lambda i: (0, 0))],
        out_specs=pl.BlockSpec((te, D), lambda i: (i, 0)),
        out_shape=jax.ShapeDtypeStruct((T, D), BF16),
        compiler_params=_params(("parallel",), blk),
    )(hs, gain)


def _rms_bwd_math(dn, h, g):
    r = lax.rsqrt(jnp.mean(h * h, axis=-1, keepdims=True) + EPS)
    xh = h * r
    dgain = jnp.sum(dn * xh, axis=0, keepdims=True)
    dxh = dn * g
    dh = r * (dxh - xh * jnp.mean(dxh * xh, axis=-1, keepdims=True))
    return dh, dgain


def _rms_bwd(dn, hs, gain, dres, scale, name):
    T, D = hs.shape
    te = _row_tile(T, 384)

    def body(dn_ref, h_ref, g_ref, dres_ref, dhs_ref, dhb_ref, dg_ref):
        dh, dgain = _rms_bwd_math(dn_ref[...], h_ref[...], g_ref[...])
        d = dres_ref[...] + dh
        dhs_ref[...] = d
        dhb_ref[...] = (scale * d).astype(BF16)

        @pl.when(pl.program_id(0) == 0)
        def _():
            dg_ref[...] = jnp.zeros_like(dg_ref)

        dg_ref[...] += dgain

    blk = _nbytes((te, D), F32) * 4 + _nbytes((te, D), BF16)
    row = lambda i: (i, 0)
    return _pallas(
        body, name=name, grid=(T // te,),
        in_specs=[pl.BlockSpec((te, D), row), pl.BlockSpec((te, D), row), pl.BlockSpec((1, D), lambda i: (0, 0)),
                  pl.BlockSpec((te, D), row)],
        out_specs=[pl.BlockSpec((te, D), row), pl.BlockSpec((te, D), row), pl.BlockSpec((1, D), lambda i: (0, 0))],
        out_shape=[jax.ShapeDtypeStruct((T, D), F32), jax.ShapeDtypeStruct((T, D), BF16),
                   jax.ShapeDtypeStruct((1, D), F32)],
        compiler_params=_params(("arbitrary",), blk),
    )(dn, hs, gain, dres)


def _rms_bwd_first(dn, hs, gain, dres, after):
    T, D = hs.shape
    S = T - TT

    def body(dn_ref, h_ref, g_ref, dres_ref, after_ref, gx_ref, gm_ref, dg_ref):
        i = pl.program_id(0)
        dh, dgain = _rms_bwd_math(dn_ref[...], h_ref[...], g_ref[...])
        d = dres_ref[...] + dh

        @pl.when(i == 0)
        def _():
            dg_ref[...] = jnp.zeros_like(dg_ref)
            gm_ref[...] = d[PAD:, :]

        @pl.when(i > 0)
        def _():
            gx_ref[...] = d

        dg_ref[...] += dgain

    blk = _nbytes((TT, D), F32) * 4
    row = lambda i: (i, 0)
    return _pallas(
        body, name="rms_bwd_ffn1", grid=(T // TT,),
        in_specs=[pl.BlockSpec((TT, D), row), pl.BlockSpec((TT, D), row), pl.BlockSpec((1, D), lambda i: (0, 0)),
                  pl.BlockSpec((TT, D), row), TOKEN],
        out_specs=[pl.BlockSpec((TT, D), lambda i: (jnp.maximum(i - 1, 0), 0)),
                   pl.BlockSpec((N_META, D), lambda i: (0, 0)), pl.BlockSpec((1, D), lambda i: (0, 0))],
        out_shape=[jax.ShapeDtypeStruct((S, D), F32), jax.ShapeDtypeStruct((N_META, D), F32),
                   jax.ShapeDtypeStruct((1, D), F32)],
        compiler_params=_params(("arbitrary",), blk),
    )(dn, hs, gain, dres, after)


def _final_loss(hs, gain, tgt):
    T, D = hs.shape

    def body(h_ref, g_ref, t_ref, dhs_ref, dhb_ref, loss_ref, dg_ref):
        i = pl.program_id(0)
        h = h_ref[...]
        g = g_ref[...]
        r = lax.rsqrt(jnp.mean(h * h, axis=-1, keepdims=True) + EPS)
        xh = h * r
        e = jnp.where(i > 0, xh * g - t_ref[...], 0.0)
        tile_loss = jnp.sum(jnp.sum(e * e, axis=1, keepdims=True), axis=0, keepdims=True) * (0.5 / D)
        dout = e * (1.0 / D)
        dgain = jnp.sum(dout * xh, axis=0, keepdims=True)
        dxh = dout * g
        d = r * (dxh - xh * jnp.mean(dxh * xh, axis=-1, keepdims=True))
        dhs_ref[...] = d
        dhb_ref[...] = (FFN_RES_SCALE * d).astype(BF16)

        @pl.when(i == 0)
        def _():
            loss_ref[...] = jnp.zeros_like(loss_ref)
            dg_ref[...] = jnp.zeros_like(dg_ref)

        loss_ref[...] += jnp.broadcast_to(tile_loss, loss_ref.shape)
        dg_ref[...] += dgain

    blk = _nbytes((TT, D), F32) * 3 + _nbytes((TT, D), BF16)
    row = lambda i: (i, 0)
    return _pallas(
        body, name="final_loss", grid=(T // TT,),
        in_specs=[pl.BlockSpec((TT, D), row), pl.BlockSpec((1, D), lambda i: (0, 0)),
                  pl.BlockSpec((TT, D), lambda i: (jnp.maximum(i - 1, 0), 0))],
        out_specs=[pl.BlockSpec((TT, D), row), pl.BlockSpec((TT, D), row),
                   pl.BlockSpec((1, 128), lambda i: (0, 0)), pl.BlockSpec((1, D), lambda i: (0, 0))],
        out_shape=[jax.ShapeDtypeStruct((T, D), F32), jax.ShapeDtypeStruct((T, D), BF16),
                   jax.ShapeDtypeStruct((1, 128), F32), jax.ShapeDtypeStruct((1, D), F32)],
        compiler_params=_params(("arbitrary",), blk),
    )(hs, gain, tgt)


MXU_COLS = 256


def _tm(T):
    return _row_tile(T, 704, 16)


def _col_chunks(n):
    return [(c, min(MXU_COLS, n - c)) for c in range(0, n, MXU_COLS)]


TOKEN = pl.BlockSpec((8, 128), lambda *_: (0, 0))


def _ffn_up(n, wg, wu, shards, prev, after, name):
    T, D = n.shape
    Fs = wg.shape[2]
    tm = _tm(T)
    nprev = 0 if prev is None else 3

    def body(shards_ref, n_ref, wg_ref, wu_ref, after_ref, *refs):
        g_ref, u_ref, a_ref = refs[nprev:]
        nn = n_ref[...]
        for c0, cw in _col_chunks(Fs):
            if 2 * cw == MXU_COLS:
                both = jnp.concatenate([wg_ref[:, c0:c0 + cw], wu_ref[:, c0:c0 + cw]], axis=1)
                gu = jnp.dot(nn, both, preferred_element_type=F32)
                g, u = gu[:, :cw], gu[:, cw:]
            else:
                g = jnp.dot(nn, wg_ref[:, c0:c0 + cw], preferred_element_type=F32)
                u = jnp.dot(nn, wu_ref[:, c0:c0 + cw], preferred_element_type=F32)
            g_ref[:, c0:c0 + cw] = g.astype(BF16)
            u_ref[:, c0:c0 + cw] = u.astype(BF16)
            a_ref[:, c0:c0 + cw] = (jax.nn.silu(g) * u).astype(BF16)

    blk = _nbytes((tm, D), BF16) + 2 * _nbytes((D, Fs), BF16) + 3 * _nbytes((tm, Fs), BF16)
    out = pl.BlockSpec((tm, Fs), lambda j, i, p: (i, p[j]))
    shp = jax.ShapeDtypeStruct((T, N_CHIPS * Fs), BF16)
    return _pallas(
        body, name=name,
        grid_spec=pltpu.PrefetchScalarGridSpec(
            num_scalar_prefetch=1, grid=(shards.shape[0], T // tm),
            in_specs=[pl.BlockSpec((tm, D), lambda j, i, p: (i, 0)),
                      pl.BlockSpec((None, D, Fs), lambda j, i, p: (p[j], 0, 0)),
                      pl.BlockSpec((None, D, Fs), lambda j, i, p: (p[j], 0, 0)), TOKEN] + [ANY] * nprev,
            out_specs=[out, out, out]),
        out_shape=[shp, shp, shp], input_output_aliases={5 + q: q for q in range(nprev)},
        compiler_params=_params(("arbitrary", "arbitrary"), blk),
    )(shards, n, wg, wu, after, *(prev or ()))


def _ffn_down(a, wd, hs, shards, name, after=None):
    T, F = a.shape
    _, Fs, D = wd.shape
    tm = _tm(T)
    tn = D // 2
    extra = [] if after is None else [after]

    def body(shards_ref, a_ref, w_ref, h_ref, *refs):
        o_ref = refs[-1]
        part = FFN_RES_SCALE * jnp.dot(a_ref[...], w_ref[...], preferred_element_type=F32)

        @pl.when(pl.program_id(2) == 0)
        def _():
            o_ref[...] = h_ref[...] + part

        @pl.when(pl.program_id(2) > 0)
        def _():
            o_ref[...] += part

    blk = _nbytes((tm, Fs), BF16) + _nbytes((Fs, tn), BF16) + 3 * _nbytes((tm, tn), F32)
    return _pallas(
        body, name=name,
        grid_spec=pltpu.PrefetchScalarGridSpec(
            num_scalar_prefetch=1, grid=(D // tn, T // tm, shards.shape[0]),
            in_specs=[pl.BlockSpec((tm, Fs), lambda n, i, k, p: (i, p[k])),
                      pl.BlockSpec((None, Fs, tn), lambda n, i, k, p: (p[k], 0, n)),
                      pl.BlockSpec((tm, tn), lambda n, i, k, p: (i, n))] + [TOKEN] * len(extra),
            out_specs=pl.BlockSpec((tm, tn), lambda n, i, k, p: (i, n))),
        out_shape=jax.ShapeDtypeStruct((T, D), F32),
        compiler_params=_params(("parallel", "parallel", "arbitrary"), blk),
    )(shards, a, wd, hs, *extra)


def _ffn_down_whole(a, wd, hs, name):
    T, F = a.shape
    D = wd.shape[1]
    tm = _tm(T)
    tn = D // 4

    def body(a_ref, w_ref, h_ref, o_ref):
        o_ref[...] = h_ref[...] + FFN_RES_SCALE * jnp.dot(a_ref[...], w_ref[...], preferred_element_type=F32)

    blk = _nbytes((tm, F), BF16) + _nbytes((F, tn), BF16) + 3 * _nbytes((tm, tn), F32)
    return _pallas(
        body, name=name, grid=(D // tn, T // tm),
        in_specs=[pl.BlockSpec((tm, F), lambda n, i: (i, 0)), pl.BlockSpec((F, tn), lambda n, i: (0, n)),
                  pl.BlockSpec((tm, tn), lambda n, i: (i, n))],
        out_specs=pl.BlockSpec((tm, tn), lambda n, i: (i, n)),
        out_shape=jax.ShapeDtypeStruct((T, D), F32),
        compiler_params=_params(("parallel", "parallel"), blk),
    )(a, wd, hs)


def _mix_in(n, w, b):
    T, D = n.shape
    Ns = w.shape[2]
    tm = _tm(T)

    def body(n_ref, w_ref, b_ref, u_ref):
        u_ref[...] = jnp.dot(n_ref[...], w_ref[...], preferred_element_type=F32) + b_ref[...]

    blk = _nbytes((tm, D), BF16) + _nbytes((D, Ns), BF16) + 2 * _nbytes((tm, Ns), F32)
    return _pallas(
        body, name="mix_in", grid=(N_CHIPS, T // tm),
        in_specs=[pl.BlockSpec((tm, D), lambda j, i: (i, 0)), pl.BlockSpec((None, D, Ns), lambda j, i: (j, 0, 0)),
                  pl.BlockSpec((1, Ns), lambda j, i: (0, j))],
        out_specs=pl.BlockSpec((tm, Ns), lambda j, i: (i, j)),
        out_shape=jax.ShapeDtypeStruct((T, N_CHIPS * Ns), F32),
        compiler_params=_params(("parallel", "parallel"), blk),
    )(n, w, b)


def _mix_out(y, w, hs, gain, after):
    T, D = y.shape
    tm = _row_tile(T, 352, 16)

    def body(y_ref, w_ref, h_ref, g_ref, after_ref, o_ref, n_ref):
        o = h_ref[...] + jnp.dot(y_ref[...], w_ref[...], preferred_element_type=F32)
        o_ref[...] = o
        r = lax.rsqrt(jnp.mean(o * o, axis=-1, keepdims=True) + EPS)
        n_ref[...] = ((o * r) * g_ref[...]).astype(BF16)

    blk = 2 * _nbytes((tm, D), BF16) + _nbytes((D, D), BF16) + 3 * _nbytes((tm, D), F32)
    row = pl.BlockSpec((tm, D), lambda i: (i, 0))
    return _pallas(
        body, name="mix_out", grid=(T // tm,),
        in_specs=[row, pl.BlockSpec((D, D), lambda i: (0, 0)), row, pl.BlockSpec((1, D), lambda i: (0, 0)), TOKEN],
        out_specs=[row, row],
        out_shape=[jax.ShapeDtypeStruct((T, D), F32), jax.ShapeDtypeStruct((T, D), BF16)],
        compiler_params=_params(("parallel",), blk),
    )(y, w, hs, gain, after)


def _ffn_bwd_act(dfb, wd, g, u, after, name):
    T, D = dfb.shape
    F = wd.shape[0]
    tm = _row_tile(T, 1408, 16)
    tn = 2 * MXU_COLS

    tr = _row_tile(tm, 352, 16)

    def body(d_ref, w_ref, g_ref, u_ref, after_ref, dg_ref, du_ref):
        for r0 in range(0, tm, tr):
            dv = d_ref[r0:r0 + tr, :]
            for c0, cw in _col_chunks(tn):
                da = lax.dot_general(dv, w_ref[c0:c0 + cw, :], NT_DIMS, preferred_element_type=F32)
                gv = g_ref[r0:r0 + tr, c0:c0 + cw].astype(F32)
                uv = u_ref[r0:r0 + tr, c0:c0 + cw].astype(F32)
                s = _sigmoid(gv)
                du_ref[r0:r0 + tr, c0:c0 + cw] = (da * (gv * s)).astype(BF16)
                dg_ref[r0:r0 + tr, c0:c0 + cw] = (da * uv * _dsilu(gv, s)).astype(BF16)

    blk = _nbytes((tm, D), BF16) + _nbytes((tn, D), BF16) + 4 * _nbytes((tm, tn), BF16)
    io = pl.BlockSpec((tm, tn), lambda n, i: (i, n))
    shp = jax.ShapeDtypeStruct((T, F), BF16)
    return _pallas(
        body, name=name, grid=(F // tn, T // tm),
        in_specs=[pl.BlockSpec((tm, D), lambda n, i: (i, 0)), pl.BlockSpec((tn, D), lambda n, i: (n, 0)), io, io, TOKEN],
        out_specs=[io, io], out_shape=[shp, shp],
        compiler_params=_params(("parallel", "parallel"), blk),
    )(dfb, wd, g, u, after)


def _nt_panel(lhs_list, w_list, after, name):
    T = lhs_list[0].shape[0]
    nsh, Dout, Ks = w_list[0].shape
    npair = len(lhs_list)
    tm = _row_tile(T, 1408, 16)
    tn = Dout // 2

    def body(*refs):
        l_refs, w_refs, o_ref = refs[:npair], refs[npair:2 * npair], refs[2 * npair + 1]
        j = pl.program_id(2)
        k0 = Ks - Ks % MXU_COLS if npair == 2 and 2 * (Ks % MXU_COLS) == MXU_COLS else Ks
        acc = None
        for p in range(npair):
            part = lax.dot_general(l_refs[p][:, :k0], w_refs[p][:, :k0], NT_DIMS, preferred_element_type=F32)
            acc = part if acc is None else acc + part
        if k0 < Ks:
            lhs = jnp.concatenate([l_refs[p][:, k0:] for p in range(npair)], axis=1)
            rhs = jnp.concatenate([w_refs[p][:, k0:] for p in range(npair)], axis=1)
            acc = acc + lax.dot_general(lhs, rhs, NT_DIMS, preferred_element_type=F32)

        @pl.when(j == 0)
        def _():
            o_ref[...] = acc

        @pl.when(j > 0)
        def _():
            o_ref[...] += acc

    blk = npair * (_nbytes((tm, Ks), BF16) + _nbytes((tn, Ks), BF16)) + 2 * _nbytes((tm, tn), F32)
    return _pallas(
        body, name=name, grid=(Dout // tn, T // tm, nsh),
        in_specs=[pl.BlockSpec((tm, Ks), lambda n, i, j: (i, j))] * npair
                 + [pl.BlockSpec((None, tn, Ks), lambda n, i, j: (j, n, 0))] * npair + [TOKEN],
        out_specs=pl.BlockSpec((tm, tn), lambda n, i, j: (i, n)),
        out_shape=jax.ShapeDtypeStruct((T, Dout), F32),
        compiler_params=_params(("parallel", "parallel", "arbitrary"), blk),
    )(*lhs_list, *w_list, after)


def _tn_call(name, grid, lhs, lhs_spec, rhs_list, rhs_specs, out_shapes, out_specs, blk, after=None):
    nr = len(rhs_list)
    extra = [] if after is None else [after]

    def body(*refs):
        l_ref, r_refs, o_refs = refs[0], refs[1:1 + nr], refs[len(refs) - nr:]
        k = pl.program_id(len(grid) - 1)
        lv = l_ref[...]
        for q in range(nr):
            part = lax.dot_general(lv, r_refs[q][...], TN_DIMS, preferred_element_type=F32)
            part = part.reshape(o_refs[q].shape)

            @pl.when(k == 0)
            def _(o=o_refs[q], part=part):
                o[...] = part

            @pl.when(k > 0)
            def _(o=o_refs[q], part=part):
                o[...] += part

    return _pallas(
        body, name=name, grid=grid, in_specs=[lhs_spec] + rhs_specs + [TOKEN] * len(extra), out_specs=out_specs,
        out_shape=out_shapes, compiler_params=_params(("parallel",) * (len(grid) - 1) + ("arbitrary",), blk),
    )(lhs, *rhs_list, *extra)


def _tk(T):
    return T


def _wgrad_cols(n, rhs_list, name, after=None):
    T, D = n.shape
    Ns = rhs_list[0].shape[1] // N_CHIPS
    tk = _tk(T)
    nr = len(rhs_list)
    tm = D // 4
    blk = _nbytes((tk, tm), BF16) + nr * (_nbytes((tk, Ns), BF16) + 2 * _nbytes((tm, Ns), F32))
    return _tn_call(
        name, (N_CHIPS, D // tm, T // tk), n, pl.BlockSpec((tk, tm), lambda j, m, k: (k, m)),
        rhs_list, [pl.BlockSpec((tk, Ns), lambda j, m, k: (k, j))] * nr,
        [jax.ShapeDtypeStruct((N_CHIPS, 2, D // 2, Ns), F32)] * nr,
        [pl.BlockSpec((None, None, tm, Ns), lambda j, m, k: (j, m // 2, m % 2, 0))] * nr, blk, after)


def _wgrad_down(a, dfb, name):
    T, F = a.shape
    D = dfb.shape[1]
    Fs = F // N_CHIPS
    tk = _tk(T)
    tn = D // 4
    blk = _nbytes((tk, Fs), BF16) + _nbytes((tk, tn), BF16) + 2 * _nbytes((Fs, tn), F32)
    return _tn_call(
        name, (N_CHIPS, D // tn, T // tk), a, pl.BlockSpec((tk, Fs), lambda j, n, k: (k, j)),
        [dfb], [pl.BlockSpec((tk, tn), lambda j, n, k: (k, n))],
        [jax.ShapeDtypeStruct((N_CHIPS, 2, Fs // 2, D), F32)],
        [pl.BlockSpec((None, 2, Fs // 2, tn), lambda j, n, k: (j, 0, 0, n))], blk)[0]


def _wgrad_out(y, dmb):
    T, D = y.shape
    tk = _tk(T)
    tn = D // 2
    rows = D // (2 * N_CHIPS)
    blk = _nbytes((tk, D // 2), BF16) + _nbytes((tk, tn), BF16) + 2 * _nbytes((D // 2, tn), F32)
    return _tn_call(
        "wgrad_w_out", (2, D // tn, T // tk), y, pl.BlockSpec((tk, D // 2), lambda m, n, k: (k, m)),
        [dmb], [pl.BlockSpec((tk, tn), lambda m, n, k: (k, n))],
        [jax.ShapeDtypeStruct((N_CHIPS, 2, rows, D), F32)],
        [pl.BlockSpec((2, 2, rows, tn), lambda m, n, k: (m, 0, 0, n))], blk)[0]


def _row_masks(i, last):
    rows = i * TT + lax.broadcasted_iota(jnp.int32, (TT, 1), 0)
    prows = i * TT - HALO + lax.broadcasted_iota(jnp.int32, (HALO, 1), 0)
    return rows >= PAD, (prows >= PAD) & (i > 0), i < last


def _conv_inputs(u, up, mask_c, mask_p, zbuf, pbuf, C1):
    b, c, v, a, g = (u[:, k * C1:(k + 1) * C1] for k in range(5))
    cp, vp, ap, gp = (up[:, k * C1:(k + 1) * C1] for k in range(1, 5))
    sg = _sigmoid(g)
    pbuf[0:HALO, :] = jnp.where(mask_p, cp * vp, 0.0)
    pbuf[HALO:, :] = jnp.where(mask_c, c * v, 0.0)
    if zbuf is not None:
        zbuf[0:HALO, :] = jnp.where(mask_p, ap * _sigmoid(gp), 0.0)
        zbuf[HALO:, :] = jnp.where(mask_c, a * sg, 0.0)
    return b, c, v, a, sg


SUBLANES = 8
SHIFT_ROWS = TT + HALO - SUBLANES


def _shifted_scratch(C1):
    return pltpu.VMEM((SUBLANES - 1, SHIFT_ROWS, C1), F32)


def _fill_shifted(buf, sh):
    for r in range(1, SUBLANES):
        sh[r - 1] = buf[r:r + SHIFT_ROWS, :]


LANES = 128


def _window(buf, sh, lo, c0):
    if sh is None or lo % SUBLANES == 0:
        return buf[lo:lo + TT, c0:c0 + LANES]
    q, r = divmod(lo, SUBLANES)
    return sh[r - 1, q * SUBLANES:q * SUBLANES + TT, c0:c0 + LANES]


def _tap_sum(w_ref, buf, sh, starts):
    chunks = []
    for c0 in range(0, buf.shape[1], LANES):
        acc = None
        for k, lo in enumerate(starts):
            term = w_ref[k:k + 1, c0:c0 + LANES] * _window(buf, sh, lo, c0)
            acc = term if acc is None else acc + term
        chunks.append(acc)
    return jnp.concatenate(chunks, axis=1)


def _causal_conv(w_ref, buf, sh=None):
    K = w_ref.shape[0]
    return _tap_sum(w_ref, buf, sh, [HALO - (K - 1) + k for k in range(K)])


def _anticausal_conv(w_ref, buf, sh=None):
    K = w_ref.shape[0]
    return _tap_sum(w_ref, buf, sh, [K - 1 - k for k in range(K)])


def _conv_weight_sums(dw_ref, dy, buf, sh=None):
    K = dw_ref.shape[0]
    for c0 in range(0, buf.shape[1], LANES):
        dyc = dy[:, c0:c0 + LANES]
        for k in range(K):
            prod = dyc * _window(buf, sh, HALO - (K - 1) + k, c0)
            dw_ref[k:k + 1, c0:c0 + LANES] += jnp.sum(prod, axis=0, keepdims=True)


def _layernorm_stats(z1):
    mu = jnp.mean(z1, axis=-1, keepdims=True)
    zc = z1 - mu
    rs = lax.rsqrt(jnp.mean(zc * zc, axis=-1, keepdims=True) + EPS)
    return zc * rs, rs


def _mixer_specs(T, DIN, C1, ksc, kcf):
    cur = pl.BlockSpec((TT, DIN), lambda i: (i, 0))
    prev = pl.BlockSpec((HALO, DIN), lambda i: (jnp.maximum(i * (TT // HALO) - 1, 0), 0))
    full = lambda r: pl.BlockSpec((r, C1), lambda i: (0, 0))
    return cur, prev, [full(ksc), full(kcf), full(1), full(1), full(1)]


def _mix_conv_fwd(u, wsc, wcf, bcf, lg, lb):
    T, DIN = u.shape
    C1 = DIN // 5
    last = T // TT - 1

    def body(u_ref, up_ref, wsc_ref, wcf_ref, bcf_ref, lg_ref, lb_ref, y_ref, z1_ref, zbuf, pbuf, zsh):
        i = pl.program_id(0)
        mask_c, mask_p, _ = _row_masks(i, last)
        b, _, _, _, _ = _conv_inputs(u_ref[...], up_ref[...], mask_c, mask_p, zbuf, pbuf, C1)
        _fill_shifted(zbuf, zsh)
        cs = _causal_conv(wsc_ref, pbuf)
        z1 = _causal_conv(wcf_ref, zbuf, zsh) + bcf_ref[...]
        z1_ref[...] = z1
        zh, _ = _layernorm_stats(z1)
        ln = zh * lg_ref[...] + lb_ref[...]
        y_ref[:, 0:C1] = jnp.where(mask_c, b * cs, 0.0).astype(BF16)
        y_ref[:, C1:] = jnp.where(mask_c, jax.nn.silu(ln), 0.0).astype(BF16)

    cur, prev, small = _mixer_specs(T, DIN, C1, wsc.shape[0], wcf.shape[0])
    blk = _nbytes((TT + HALO, DIN), F32) + _nbytes((TT, 2 * C1), BF16) + 12 * _nbytes((TT + HALO, C1), F32)
    return _pallas(
        body, name="mix_conv_fwd", grid=(T // TT,),
        in_specs=[cur, prev] + small,
        out_specs=[pl.BlockSpec((TT, 2 * C1), lambda i: (i, 0)), pl.BlockSpec((TT, C1), lambda i: (i, 0))],
        out_shape=[jax.ShapeDtypeStruct((T, 2 * C1), BF16), jax.ShapeDtypeStruct((T, C1), F32)],
        scratch_shapes=[pltpu.VMEM((TT + HALO, C1), F32), pltpu.VMEM((TT + HALO, C1), F32), _shifted_scratch(C1)],
        compiler_params=_params(("arbitrary",), blk),
    )(u, u, wsc, wcf, bcf, lg, lb)


def _mix_conv_bwd1(u, z1, dy, wsc, lg, lb):
    T, DIN = u.shape
    C1 = DIN // 5
    last = T // TT - 1

    def body(u_ref, up_ref, z1_ref, dy_ref, wsc_ref, lg_ref, lb_ref,
             dz1_ref, dcs_ref, db_ref, dlg_ref, dlb_ref, dbcf_ref, pbuf):
        i = pl.program_id(0)
        mask_c, mask_p, _ = _row_masks(i, last)
        b, _, _, _, _ = _conv_inputs(u_ref[...], up_ref[...], mask_c, mask_p, None, pbuf, C1)
        cs = _causal_conv(wsc_ref, pbuf)
        zh, rs = _layernorm_stats(z1_ref[...])
        ln = zh * lg_ref[...] + lb_ref[...]
        dy = dy_ref[...]
        dysc = jnp.where(mask_c, dy[:, 0:C1], 0.0)
        dycf = jnp.where(mask_c, dy[:, C1:], 0.0)
        db_ref[...] = (dysc * cs).astype(BF16)
        dcs_ref[...] = dysc * b
        dl = dycf * _dsilu(ln, _sigmoid(ln))
        dzh = dl * lg_ref[...]
        dz1 = rs * (dzh - jnp.mean(dzh, axis=-1, keepdims=True) - zh * jnp.mean(dzh * zh, axis=-1, keepdims=True))
        dz1_ref[...] = dz1

        @pl.when(i == 0)
        def _():
            dlg_ref[...] = jnp.zeros_like(dlg_ref)
            dlb_ref[...] = jnp.zeros_like(dlb_ref)
            dbcf_ref[...] = jnp.zeros_like(dbcf_ref)

        dlg_ref[...] += jnp.sum(dl * zh, axis=0, keepdims=True)
        dlb_ref[...] += jnp.sum(dl, axis=0, keepdims=True)
        dbcf_ref[...] += jnp.sum(dz1, axis=0, keepdims=True)

    cur, prev, small = _mixer_specs(T, DIN, C1, wsc.shape[0], 1)
    tile = lambda: pl.BlockSpec((TT, C1), lambda i: (i, 0))
    vec = lambda: pl.BlockSpec((1, C1), lambda i: (0, 0))
    blk = _nbytes((TT + HALO, DIN), F32) + 5 * _nbytes((TT, C1), F32) + 12 * _nbytes((TT + HALO, C1), F32)
    return _pallas(
        body, name="mix_conv_bwd1", grid=(T // TT,),
        in_specs=[cur, prev, tile(), pl.BlockSpec((TT, 2 * C1), lambda i: (i, 0)), small[0], small[3], small[4]],
        out_specs=[tile(), tile(), tile(), vec(), vec(), vec()],
        out_shape=[jax.ShapeDtypeStruct((T, C1), F32), jax.ShapeDtypeStruct((T, C1), F32),
                   jax.ShapeDtypeStruct((T, C1), BF16)] + [jax.ShapeDtypeStruct((1, C1), F32)] * 3,
        scratch_shapes=[pltpu.VMEM((TT + HALO, C1), F32)],
        compiler_params=_params(("arbitrary",), blk),
    )(u, u, z1, dy, wsc, lg, lb)


def _mix_conv_bwd2(u, dz1, dcs, db, wsc, wcf):
    T, DIN = u.shape
    C1 = DIN // 5
    last = T // TT - 1
    ksc, kcf = wsc.shape[0], wcf.shape[0]

    def body(u_ref, up_ref, dz_ref, dzn_ref, dc_ref, dcn_ref, db_ref, wsc_ref, wcf_ref,
             du_ref, dbin_ref, dwsc_ref, dwcf_ref, zbuf, pbuf, dzbuf, dcbuf, zsh, dzsh):
        i = pl.program_id(0)
        mask_c, mask_p, has_next = _row_masks(i, last)
        _, c, v, a, sg = _conv_inputs(u_ref[...], up_ref[...], mask_c, mask_p, zbuf, pbuf, C1)
        dz1 = dz_ref[...]
        dcs = dc_ref[...]
        dzbuf[0:TT, :] = dz1
        dzbuf[TT:, :] = jnp.where(has_next, dzn_ref[...], 0.0)
        dcbuf[0:TT, :] = dcs
        dcbuf[TT:, :] = jnp.where(has_next, dcn_ref[...], 0.0)

        @pl.when(i == 0)
        def _():
            dbin_ref[...] = jnp.zeros_like(dbin_ref)
            dwsc_ref[...] = jnp.zeros_like(dwsc_ref)
            dwcf_ref[...] = jnp.zeros_like(dwcf_ref)

        _fill_shifted(zbuf, zsh)
        _fill_shifted(dzbuf, dzsh)
        _conv_weight_sums(dwcf_ref, dz1, zbuf, zsh)
        _conv_weight_sums(dwsc_ref, dcs, pbuf)
        dz0 = jnp.where(mask_c, _anticausal_conv(wcf_ref, dzbuf, dzsh), 0.0)
        dp = jnp.where(mask_c, _anticausal_conv(wsc_ref, dcbuf), 0.0)
        parts = (db_ref[...].astype(F32), dp * v, dp * c, dz0 * sg, dz0 * a * sg * (1.0 - sg))
        for k, part in enumerate(parts):
            du_ref[:, k * C1:(k + 1) * C1] = part.astype(BF16)
            dbin_ref[:, k * C1:(k + 1) * C1] += jnp.sum(part, axis=0, keepdims=True)

    cur, prev, small = _mixer_specs(T, DIN, C1, ksc, kcf)
    tile = lambda: pl.BlockSpec((TT, C1), lambda i: (i, 0))
    nxt = lambda: pl.BlockSpec((HALO, C1), lambda i: (jnp.minimum((i + 1) * (TT // HALO), T // HALO - 1), 0))
    blk = (_nbytes((TT + HALO, DIN), F32) + _nbytes((TT, DIN), BF16) + 5 * _nbytes((TT, C1), F32)
           + 16 * _nbytes((TT + HALO, C1), F32))
    buf = lambda: pltpu.VMEM((TT + HALO, C1), F32)
    return _pallas(
        body, name="mix_conv_bwd2", grid=(T // TT,),
        in_specs=[cur, prev, tile(), nxt(), tile(), nxt(), tile(), small[0], small[1]],
        out_specs=[pl.BlockSpec((TT, DIN), lambda i: (i, 0)), pl.BlockSpec((1, DIN), lambda i: (0, 0)),
                   pl.BlockSpec((ksc, C1), lambda i: (0, 0)), pl.BlockSpec((kcf, C1), lambda i: (0, 0))],
        out_shape=[jax.ShapeDtypeStruct((T, DIN), BF16), jax.ShapeDtypeStruct((1, DIN), F32),
                   jax.ShapeDtypeStruct((ksc, C1), F32), jax.ShapeDtypeStruct((kcf, C1), F32)],
        scratch_shapes=[buf(), buf(), buf(), buf(), _shifted_scratch(C1), _shifted_scratch(C1)],
        compiler_params=_params(("arbitrary",), blk),
    )(u, u, dz1, dz1, dcs, dcs, db, wsc, wcf)


def _place():
    x, y, c = lax.axis_index("x"), lax.axis_index("y"), lax.axis_index("c")
    chips = [(1 - x, y), (x, 1 - y), (1 - x, 1 - y)]
    return x, y, c, chips


ANY = pl.BlockSpec(memory_space=pl.ANY)


def _cast_own_block(place, w, name):
    R, C = w.shape
    tr = _row_tile(R // 2, 1024, 16)
    nblk = R // 2 // tr

    def body(place_ref, w_ref, o_ref):
        o_ref[...] = w_ref[...].astype(BF16)

    return _pallas(
        body, name=name,
        grid_spec=pltpu.PrefetchScalarGridSpec(
            num_scalar_prefetch=1, grid=(2, nblk),
            in_specs=[pl.BlockSpec((tr, C), lambda h, i, p: (h * nblk + i, 0))],
            out_specs=pl.BlockSpec((None, None, tr, C), lambda h, i, p: (p[0], h, i, 0))),
        out_shape=jax.ShapeDtypeStruct((N_CHIPS, 2, R // 2, C), BF16),
        compiler_params=_params(("parallel", "parallel"), _nbytes((tr, C), F32) + _nbytes((tr, C), BF16)),
    )(place, w)


HBM = pl.BlockSpec(memory_space=pltpu.HBM)
SEM = pl.BlockSpec(memory_space=pltpu.SEMAPHORE)
EFFECT = pltpu.SideEffectType.DATAFLOW_SIDE_EFFECTING


def _gather_copies(refs, send, recv, rels=(0, 1, 2)):
    x, y, c, chips = _place()
    s = 2 * x + y
    n = len(rels)
    return [pltpu.make_async_remote_copy(src_ref=ref.at[s, c], dst_ref=ref.at[s, c], send_sem=send.at[n * w + k],
                                         recv_sem=recv.at[n * w + k], device_id=(*chips[r], c), device_id_type=MESH)
            for w, ref in enumerate(refs) for k, r in enumerate(rels)]


def _scatter_copies(refs, send, recv):
    x, y, c, chips = _place()
    nw = len(refs) // 2
    return [pltpu.make_async_remote_copy(src_ref=refs[w].at[2 * tx + ty], dst_ref=refs[nw + w].at[r],
                                         send_sem=send.at[3 * w + r], recv_sem=recv.at[3 * w + r],
                                         device_id=(tx, ty, c), device_id_type=MESH)
            for w in range(nw) for r, (tx, ty) in enumerate(chips)]


def _pair_copies(refs, send, recv):
    x, y, c, _ = _place()
    nw = len(refs) // 2
    return [pltpu.make_async_remote_copy(src_ref=refs[w].at[j, 1 - c], dst_ref=refs[nw + w].at[j],
                                         send_sem=send.at[N_CHIPS * w + j], recv_sem=recv.at[N_CHIPS * w + j],
                                         device_id=(x, y, 1 - c), device_id_type=MESH)
            for w in range(nw) for j in range(N_CHIPS)]


def _forward_copies(refs, send, recv, rels=(0, 1, 2)):
    x, y, c, chips = _place()
    n = len(rels)
    copies = []
    for w, ref in enumerate(refs):
        for k, r in enumerate(rels):
            tx, ty = chips[r]
            blk = ref.at[2 * tx + ty, c]
            copies.append(pltpu.make_async_remote_copy(src_ref=blk, dst_ref=blk, send_sem=send.at[n * w + k],
                                                       recv_sem=recv.at[n * w + k], device_id=(x, y, 1 - c),
                                                       device_id_type=MESH))
    return copies


def _half_copies(refs, send, recv):
    x, y, c, _ = _place()
    return [pltpu.make_async_remote_copy(src_ref=ref.at[c], dst_ref=ref.at[c], send_sem=send.at[w], recv_sem=recv.at[w],
                                         device_id=(x, y, 1 - c), device_id_type=MESH)
            for w, ref in enumerate(refs)]


def _start_copies(bufs, after, ncopies, make_copies, name):
    n = len(bufs)

    def body(*refs):
        in_refs, send, recv, token = refs[:n], refs[n + 1], refs[n + 2], refs[2 * n + 3]
        for cp in make_copies(in_refs, send, recv):
            cp.start()
        token[...] = jnp.zeros_like(token)

    outs = _pallas(
        body, name=name, in_specs=[HBM] * n + [ANY],
        out_specs=[SEM, SEM] + [HBM] * n + [pl.BlockSpec(memory_space=pltpu.VMEM)],
        out_shape=[pltpu.SemaphoreType.DMA((ncopies,)), pltpu.SemaphoreType.DMA((ncopies,))]
                  + [pltpu.HBM(b.shape, b.dtype) for b in bufs] + [jax.ShapeDtypeStruct((8, 128), F32)],
        input_output_aliases={k: 2 + k for k in range(n)},
        compiler_params=pltpu.CompilerParams(has_side_effects=EFFECT),
    )(*[pltpu.with_memory_space_constraint(b, pltpu.HBM) for b in bufs], after)
    return outs[0], outs[1], list(outs[2:2 + n]), outs[2 + n]


def _wait_copies(send, recv, bufs, after, make_copies, name):
    n = len(bufs)

    def body(*refs):
        in_refs, send_ref, recv_ref = refs[:n], refs[n], refs[n + 1]
        for cp in make_copies(in_refs, send_ref, recv_ref):
            cp.wait_send()
            cp.wait_recv()

    outs = _pallas(
        body, name=name, in_specs=[HBM] * n + [SEM, SEM, ANY], out_specs=[HBM] * n,
        out_shape=[pltpu.HBM(b.shape, b.dtype) for b in bufs],
        input_output_aliases={k: k for k in range(n)},
        compiler_params=pltpu.CompilerParams(has_side_effects=EFFECT),
    )(*bufs, send, recv, after)
    return list(outs)


def _forward_halves(bufs, name, rels=(0, 1, 2)):
    nw = len(bufs)
    n = len(rels)

    def body(*refs):
        o_refs = refs[nw:2 * nw]
        send, recv = refs[2 * nw:]
        x, y, c, chips = _place()
        sib = (x, y, 1 - c)
        copies = []
        for w in range(nw):
            for k, r in enumerate(rels):
                tx, ty = chips[r]
                ref = o_refs[w].at[2 * tx + ty, c]
                cp = pltpu.make_async_remote_copy(src_ref=ref, dst_ref=ref, send_sem=send.at[n * w + k],
                                                  recv_sem=recv.at[n * w + k], device_id=sib, device_id_type=MESH)
                cp.start()
                copies.append(cp)
        for w in range(nw):
            for k, r in enumerate(rels):
                tx, ty = chips[r]
                ref = o_refs[w].at[2 * tx + ty, 1 - c]
                pltpu.make_async_remote_copy(src_ref=ref, dst_ref=ref, send_sem=send.at[n * w + k],
                                             recv_sem=recv.at[n * w + k], device_id=sib, device_id_type=MESH).wait_recv()
        for cp in copies:
            cp.wait_send()

    return _pallas(
        body, name=name, in_specs=[ANY] * nw, out_specs=[ANY] * nw,
        out_shape=[jax.ShapeDtypeStruct(b.shape, b.dtype) for b in bufs],
        input_output_aliases={w: w for w in range(nw)},
        scratch_shapes=[pltpu.SemaphoreType.DMA((n * nw,)), pltpu.SemaphoreType.DMA((n * nw,))],
    )(*bufs)


def _share_small(v, reduce, name, after):
    R, C = v.shape

    def body(v_ref, after_ref, o_ref, *scratch):
        if reduce:
            all_ref, send, recv, lsem = scratch
        else:
            all_ref = o_ref
            send, recv, lsem = scratch
        x, y, c, _ = _place()
        me = 4 * x + 2 * y + c
        loc = pltpu.make_async_copy(v_ref, all_ref.at[me], lsem)
        loc.start()
        copies = []
        for k in range(1, N_DEV):
            kx, ky, kc = (k >> 2) & 1, (k >> 1) & 1, k & 1
            peer = (x ^ kx, y ^ ky, c ^ kc)
            cp = pltpu.make_async_remote_copy(src_ref=v_ref, dst_ref=all_ref.at[me], send_sem=send.at[k - 1],
                                              recv_sem=recv.at[k - 1], device_id=peer, device_id_type=MESH)
            cp.start()
            copies.append(cp)
        for k in range(1, N_DEV):
            kx, ky, kc = (k >> 2) & 1, (k >> 1) & 1, k & 1
            src = 4 * (x ^ kx) + 2 * (y ^ ky) + (c ^ kc)
            pltpu.make_async_remote_copy(src_ref=v_ref, dst_ref=all_ref.at[src], send_sem=send.at[k - 1],
                                         recv_sem=recv.at[k - 1], device_id=(x, y, c), device_id_type=MESH).wait_recv()
        for cp in copies:
            cp.wait_send()
        loc.wait()
        if reduce:
            total = all_ref[0]
            for d in range(1, N_DEV):
                total = total + all_ref[d]
            o_ref[...] = total

    vm = pl.BlockSpec(memory_space=pltpu.VMEM)
    sems = [pltpu.SemaphoreType.DMA((N_DEV - 1,)), pltpu.SemaphoreType.DMA((N_DEV - 1,)), pltpu.SemaphoreType.DMA]
    if reduce:
        out_shape = jax.ShapeDtypeStruct((R, C), F32)
        scratch = [pltpu.VMEM((N_DEV, R, C), F32)] + sems
    else:
        out_shape = jax.ShapeDtypeStruct((N_DEV, R, C), F32)
        scratch = sems
    return _pallas(
        body, name=name, in_specs=[vm, ANY], out_specs=vm, out_shape=out_shape, scratch_shapes=scratch,
        compiler_params=pltpu.CompilerParams(vmem_limit_bytes=int(min(4 * N_DEV * R * C * 4 + 2 ** 24, 2 ** 25 + 2 ** 24))),
    )(v, after)


def _pair_sum(place, g, rb, name):
    _, _, Rh, C = g.shape
    tr = _row_tile(Rh, 512, 16)

    def body(place_ref, g_ref, r_ref, q_ref):
        q_ref[...] = (g_ref[...] + r_ref[...]).astype(BF16)

    blk = 2 * _nbytes((tr, C), F32) + _nbytes((tr, C), BF16)
    return _pallas(
        body, name=name,
        grid_spec=pltpu.PrefetchScalarGridSpec(
            num_scalar_prefetch=1, grid=(N_CHIPS - 1, Rh // tr),
            in_specs=[pl.BlockSpec((None, None, tr, C), lambda j, i, p: (p[0] ^ (j + 1), p[1], i, 0)),
                      pl.BlockSpec((None, tr, C), lambda j, i, p: (p[0] ^ (j + 1), i, 0))],
            out_specs=pl.BlockSpec((None, tr, C), lambda j, i, p: (p[0] ^ (j + 1), i, 0))),
        out_shape=jax.ShapeDtypeStruct((N_CHIPS, Rh, C), BF16),
        compiler_params=_params(("parallel", "parallel"), blk),
    )(place, g, rb)


def _chip_sum(place, g, rb, rc, name):
    _, _, Rh, C = g.shape
    tr = _row_tile(Rh, 512, 16)

    def body(place_ref, g_ref, r_ref, rc_ref, o_ref):
        total = g_ref[...] + r_ref[...]
        for r in range(3):
            total = total + rc_ref[r].astype(F32)
        o_ref[...] = total

    blk = 3 * _nbytes((tr, C), F32) + 3 * _nbytes((tr, C), BF16)
    return _pallas(
        body, name=name,
        grid_spec=pltpu.PrefetchScalarGridSpec(
            num_scalar_prefetch=1, grid=(Rh // tr,),
            in_specs=[pl.BlockSpec((None, None, tr, C), lambda i, p: (p[0], p[1], i, 0)),
                      pl.BlockSpec((None, tr, C), lambda i, p: (p[0], i, 0)),
                      pl.BlockSpec((3, tr, C), lambda i, p: (0, i, 0))],
            out_specs=pl.BlockSpec((None, tr, C), lambda i, p: (p[1], i, 0))),
        out_shape=jax.ShapeDtypeStruct((2, Rh, C), F32),
        compiler_params=_params(("parallel",), blk),
    )(place, g, rb, rc)


def _adamw_math(w, g, m, v):
    m = ADAM_B1 * m + (1.0 - ADAM_B1) * g
    v = ADAM_B2 * v + (1.0 - ADAM_B2) * jnp.square(g)
    m_hat = m / (1.0 - ADAM_B1 ** ADAM_STEP)
    v_hat = v / (1.0 - ADAM_B2 ** ADAM_STEP)
    delta = -ADAM_LR * (m_hat / (jnp.sqrt(v_hat) + ADAM_EPS) + ADAM_WD * w)
    return delta, m, v


def _adamw(w, g, m, v, name):
    R, C = w.shape
    tr = _row_tile(R, 512)

    def body(w_ref, g_ref, m_ref, v_ref, go_ref, d_ref, nm_ref, nv_ref):
        gv = g_ref[...]
        d, nm, nv = _adamw_math(w_ref[...], gv, m_ref[...], v_ref[...])
        go_ref[...] = gv
        d_ref[...] = d
        nm_ref[...] = nm
        nv_ref[...] = nv

    spec = pl.BlockSpec((tr, C), lambda i: (i, 0))
    shp = jax.ShapeDtypeStruct((R, C), F32)
    return _pallas(
        body, name=name, grid=(R // tr,), in_specs=[spec] * 4, out_specs=[spec] * 4, out_shape=[shp] * 4,
        compiler_params=_params(("parallel",), 8 * _nbytes((tr, C), F32)),
    )(w, g, m, v)


def _adamw_small(ws, gs, ms, vs):
    n = len(ws)

    def body(*refs):
        for k in range(n):
            w_ref, g_ref, m_ref, v_ref = (refs[q * n + k] for q in range(4))
            d, nm, nv = _adamw_math(w_ref[...], g_ref[...], m_ref[...], v_ref[...])
            refs[4 * n + k][...] = d
            refs[5 * n + k][...] = nm
            refs[6 * n + k][...] = nv

    vm = pl.BlockSpec(memory_space=pltpu.VMEM)
    shapes = [jax.ShapeDtypeStruct(w.shape, F32) for w in ws]
    outs = _pallas(
        body, name="adamw_small", in_specs=[vm] * (4 * n), out_specs=[vm] * (3 * n), out_shape=shapes * 3,
    )(*ws, *gs, *ms, *vs)
    return outs[:n], outs[n:2 * n], outs[2 * n:]


def _pad_rows(a, rows):
    return jnp.pad(a, ((0, rows - a.shape[0]), (0, 0)))


def kernel(x, meta_tokens, ffn1_norm, ffn1_w_gate, ffn1_w_up, ffn1_w_down, mix_norm, w_in, b_in, conv_sc_w, conv_cf_w, conv_cf_b, ln_cf_g, ln_cf_b, w_out, ffn2_norm, ffn2_w_gate, ffn2_w_up, ffn2_w_down, final_norm, loss_target, m_meta_tokens, m_ffn1_norm, m_ffn1_w_gate, m_ffn1_w_up, m_ffn1_w_down, m_mix_norm, m_w_in, m_b_in, m_conv_sc_w, m_conv_cf_w, m_conv_cf_b, m_ln_cf_g, m_ln_cf_b, m_w_out, m_ffn2_norm, m_ffn2_w_gate, m_ffn2_w_up, m_ffn2_w_down, m_final_norm, v_meta_tokens, v_ffn1_norm, v_ffn1_w_gate, v_ffn1_w_up, v_ffn1_w_down, v_mix_norm, v_w_in, v_b_in, v_conv_sc_w, v_conv_cf_w, v_conv_cf_b, v_ln_cf_g, v_ln_cf_b, v_w_out, v_ffn2_norm, v_ffn2_w_gate, v_ffn2_w_up, v_ffn2_w_down, v_final_norm):
    xi, yi, ci = lax.axis_index("x"), lax.axis_index("y"), lax.axis_index("c")
    chip = 2 * xi + yi
    place = jnp.stack([chip, ci]).astype(jnp.int32)

    x2 = x[0]
    tgt = loss_target[0]
    S, D = x2.shape
    C1 = D // 2
    cs = conv_sc_w.shape[2]
    ksc, kcf = conv_sc_w.shape[1], conv_cf_w.shape[1]
    ms = meta_tokens.shape[1]

    big = {"ffn1_w_gate": ffn1_w_gate, "ffn1_w_up": ffn1_w_up, "ffn1_w_down": ffn1_w_down, "w_in": w_in, "w_out": w_out,
           "ffn2_w_gate": ffn2_w_gate, "ffn2_w_up": ffn2_w_up, "ffn2_w_down": ffn2_w_down}
    big_m = {"ffn1_w_gate": m_ffn1_w_gate, "ffn1_w_up": m_ffn1_w_up, "ffn1_w_down": m_ffn1_w_down, "w_in": m_w_in,
             "w_out": m_w_out, "ffn2_w_gate": m_ffn2_w_gate, "ffn2_w_up": m_ffn2_w_up, "ffn2_w_down": m_ffn2_w_down}
    big_v = {"ffn1_w_gate": v_ffn1_w_gate, "ffn1_w_up": v_ffn1_w_up, "ffn1_w_down": v_ffn1_w_down, "w_in": v_w_in,
             "w_out": v_w_out, "ffn2_w_gate": v_ffn2_w_gate, "ffn2_w_up": v_ffn2_w_up, "ffn2_w_down": v_ffn2_w_down}
    buf = {nm: _cast_own_block(place, w[0], "cast_" + nm) for nm, w in big.items()}
    whole_weight = lambda g: g.reshape(N_CHIPS, 2 * g.shape[2], g.shape[3])
    corner = lambda a: a.reshape(-1, a.shape[-1])[:8, :128]

    NEAR, FAR = (0, 1), (2,)
    groups = {"ffn1_near": (["ffn1_w_gate", "ffn1_w_up", "ffn1_w_down"], NEAR),
              "ffn1_far": (["ffn1_w_gate", "ffn1_w_up", "ffn1_w_down"], FAR),
              "mix": (["w_in", "w_out"], NEAR + FAR),
              "ffn2_up": (["ffn2_w_gate", "ffn2_w_up"], NEAR + FAR),
              "ffn2_down": (["ffn2_w_down"], NEAR + FAR)}
    started = {}

    def start(tag, after):
        nms, rels = groups[tag]
        copies = functools.partial(_gather_copies, rels=rels)
        send, recv, thru, token = _start_copies([buf[nm] for nm in nms], after, len(rels) * len(nms), copies,
                                                "gather_start_" + tag)
        for nm, b in zip(nms, thru):
            buf[nm] = b
        started[tag] = (send, recv, copies)
        return token

    def arrive(tag, after, then=None):
        nms, rels = groups[tag]
        send, recv, copies = started[tag]
        got = _wait_copies(send, recv, [buf[nm] for nm in nms], corner(after), copies, "gather_wait_" + tag)
        for nm, b in zip(nms, got):
            buf[nm] = b
        if then is not None:
            start(then, corner(got[0]))
        for nm, b in zip(nms, _forward_halves([buf[nm] for nm in nms], "gather_forward_" + tag, rels)):
            buf[nm] = b

    def passing(tag, after):
        nms, rels = groups[tag]
        send, recv, copies = started[tag]
        got = _wait_copies(send, recv, [buf[nm] for nm in nms], corner(after), copies, "gather_wait_" + tag)
        copies = functools.partial(_forward_copies, rels=rels)
        send, recv, thru, token = _start_copies(got, corner(got[0]), len(rels) * len(nms), copies,
                                                "gather_pass_" + tag)
        for nm, b in zip(nms, thru):
            buf[nm] = b
        started[tag] = (send, recv, copies)
        return token

    def passed(tag, after):
        nms, _ = groups[tag]
        send, recv, copies = started[tag]
        for nm, b in zip(nms, _wait_copies(send, recv, [buf[nm] for nm in nms], corner(after), copies,
                                           "gather_passed_" + tag)):
            buf[nm] = b

    tokens = lambda *arrays: jnp.concatenate([corner(a).astype(F32) for a in arrays], axis=0)
    assert ksc <= 8 and kcf <= 32 and cs <= ms
    pack = jnp.concatenate([
        meta_tokens,
        jnp.pad(conv_sc_w[0], ((0, 8 - ksc), (0, ms - cs))),
        jnp.pad(conv_cf_w[0], ((0, 32 - kcf), (0, ms - cs)))], axis=0)
    everyone = _share_small(pack, False, "share_params", pack)[0::2]
    meta_full = jnp.transpose(everyone[:, :N_META, :], (1, 0, 2)).reshape(N_META, D)
    wsc_full = jnp.transpose(everyone[:, N_META:N_META + ksc, :cs], (1, 0, 2)).reshape(ksc, C1)
    wcf_full = jnp.transpose(everyone[:, N_META + 8:N_META + 8 + kcf, :cs], (1, 0, 2)).reshape(kcf, C1)

    token = start("ffn1_near", corner(everyone))

    ffn1 = lambda: [whole_weight(buf[nm]) for nm in ["ffn1_w_gate", "ffn1_w_up", "ffn1_w_down"]]
    own = chip[None].astype(jnp.int32)
    near = jnp.stack([chip ^ 2, chip ^ 1]).astype(jnp.int32)
    far = (chip ^ 3)[None].astype(jnp.int32)
    all_chips = jnp.arange(N_CHIPS, dtype=jnp.int32)

    hs0, n1 = _embed_rms(x2, meta_full, ffn1_norm)
    wg1, wu1, wd1 = ffn1()
    gua = _ffn_up(n1, wg1, wu1, own, None, token, "ffn1_up_own")
    hs1 = _ffn_down(gua[2], wd1, hs0, own, "ffn1_down_own")
    later = [buf[nm] for nm in ["w_in", "w_out", "ffn2_w_gate", "ffn2_w_up", "ffn2_w_down"]]
    arrive("ffn1_near", tokens(hs1, *later), "ffn1_far")
    wg1, wu1, wd1 = ffn1()
    gua = _ffn_up(n1, wg1, wu1, near, gua, token, "ffn1_up_near")
    tok = start("mix", corner(gua[2]))
    tok = passing("ffn1_far", tok)
    hs1 = _ffn_down(gua[2], ffn1()[2], hs1, near, "ffn1_down_near", tok)
    passed("ffn1_far", hs1)
    wg1, wu1, wd1 = ffn1()
    g1, u1, a1 = _ffn_up(n1, wg1, wu1, far, gua, token, "ffn1_up_far")
    tok = passing("mix", a1)
    hs1 = _ffn_down(a1, wd1, hs1, far, "ffn1_down_far", tok)
    F = N_CHIPS * wd1.shape[1]
    tok = start("ffn2_up", corner(hs1))
    passed("mix", tok)
    win, wout = whole_weight(buf["w_in"]), whole_weight(buf["w_out"])
    n2 = _rms(hs1, mix_norm, "rms_mix")
    u = _mix_in(n2, win, b_in)
    y, z1 = _mix_conv_fwd(u, wsc_full, wcf_full, conv_cf_b, ln_cf_g, ln_cf_b)
    tok = start("ffn2_down", corner(y))
    tok = passing("ffn2_up", tok)
    hs2, n3 = _mix_out(y, wout.reshape(D, D), hs1, ffn2_norm, tok)
    passed("ffn2_up", hs2)
    wg2, wu2 = whole_weight(buf["ffn2_w_gate"]), whole_weight(buf["ffn2_w_up"])
    g2, u2, a2 = _ffn_up(n3, wg2, wu2, all_chips, None, token, "ffn2_up")
    passed("ffn2_down", passing("ffn2_down", a2))
    wd2 = whole_weight(buf["ffn2_w_down"])
    hs3 = _ffn_down_whole(a2, wd2.reshape(F, D), hs2, "ffn2_down")
    token_ffn2 = token

    def pair_start(group, after, tag):
        gs = [g for _, g in group]
        lands = [lax.empty((N_CHIPS,) + g.shape[2:], F32) for g in gs]
        send, recv, thru, token = _start_copies(gs + lands, after, N_CHIPS * len(gs), _pair_copies,
                                                "pair_start_" + tag)
        return (group, send, recv, thru, tag), token

    def scatter_start(state, after):
        group, send, recv, thru, tag = state
        thru = _wait_copies(send, recv, thru, corner(after), _pair_copies, "pair_wait_" + tag)
        gs, sib = thru[:len(group)], thru[len(group):]
        sums = [_pair_sum(place, g, rb, "pair_sum_" + nm) for (nm, _), g, rb in zip(group, gs, sib)]
        lands = [lax.empty((3,) + q.shape[1:], BF16) for q in sums]
        send, recv, thru, token = _start_copies(sums + lands, corner(sums[-1]), 3 * len(gs), _scatter_copies,
                                                "scatter_start_" + tag)
        return ([(nm, g) for (nm, _), g in zip(group, gs)], sib, send, recv, thru, tag), token

    def finish_sum(state, after):
        group, sib, send, recv, thru, tag = state
        lands = _wait_copies(send, recv, thru, corner(after), _scatter_copies, "scatter_wait_" + tag)[len(group):]
        mine = [_chip_sum(place, g, rb, rc, "chip_sum_" + nm) for (nm, g), rb, rc in zip(group, sib, lands)]
        send, recv, thru, token = _start_copies(mine, corner(mine[-1]), len(mine), _half_copies, "half_start_" + tag)
        return (group, send, recv, thru, tag), token

    def finish_adam(state, after):
        group, send, recv, thru, tag = state
        whole = _wait_copies(send, recv, thru, corner(after), _half_copies, "half_wait_" + tag)
        out = {}
        for (nm, _), g in zip(group, whole):
            w = big[nm]
            g_out, d, new_m, new_v = _adamw(w[0], g.reshape(w.shape[1:]), big_m[nm][0], big_v[nm][0], "adamw_" + nm)
            out[nm] = (g_out[None], d[None], new_m[None], new_v[None])
        return out

    dhs3, df2, loss_row, d_final = _final_loss(hs3, final_norm.reshape(1, D), tgt)

    dg2, du2 = _ffn_bwd_act(df2, wd2.reshape(F, D), g2, u2, token_ffn2, "ffn2_bwd_act")
    gw_d2 = _wgrad_down(a2, df2, "wgrad_ffn2_down")
    gw_g2 = _wgrad_cols(n3, [dg2], "wgrad_ffn2_gate")[0]
    gw_u2 = _wgrad_cols(n3, [du2], "wgrad_ffn2_up")[0]
    pair_ffn2, token = pair_start([("ffn2_w_gate", gw_g2), ("ffn2_w_up", gw_u2), ("ffn2_w_down", gw_d2)],
                                  corner(gw_u2), "ffn2")
    dn3 = _nt_panel([dg2, du2], [wg2, wu2], token, "ffn2_bwd_in")
    red_ffn2, token = scatter_start(pair_ffn2, dn3)
    dhs2, dm, d_ffn2 = _rms_bwd(dn3, hs2, ffn2_norm, dhs3, 1.0, "rms_bwd_ffn2")

    dy = _nt_panel([dm], [wout.reshape(1, D, D)], token, "mix_bwd_out")
    gw_out = _wgrad_out(y, dm)
    dz1, dcs, db, d_lg, d_lb, d_bcf = _mix_conv_bwd1(u, z1, dy, wsc_full, ln_cf_g, ln_cf_b)
    du, d_bin, d_wsc, d_wcf = _mix_conv_bwd2(u, dz1, dcs, db, wsc_full, wcf_full)
    gw_in = _wgrad_cols(n2, [du], "wgrad_w_in")[0]
    pair_mix, token = pair_start([("w_in", gw_in), ("w_out", gw_out)], corner(gw_in), "mix")
    dn2 = _nt_panel([du], [win], token, "mix_bwd_in")
    red_mix, token = scatter_start(pair_mix, dn2)
    dhs1, df1, d_mix = _rms_bwd(dn2, hs1, mix_norm, dhs2, FFN_RES_SCALE, "rms_bwd_mix")

    dg1, du1 = _ffn_bwd_act(df1, wd1.reshape(F, D), g1, u1, token, "ffn1_bwd_act")
    gw_d1 = _wgrad_down(a1, df1, "wgrad_ffn1_down")
    gw_g1 = _wgrad_cols(n1, [dg1], "wgrad_ffn1_gate")[0]
    pair_ffn1a, token = pair_start([("ffn1_w_down", gw_d1), ("ffn1_w_gate", gw_g1)], corner(gw_g1), "ffn1a")
    gw_u1 = _wgrad_cols(n1, [du1], "wgrad_ffn1_up", token)[0]
    red_ffn1a, token = scatter_start(pair_ffn1a, gw_u1)
    pair_ffn1b, token = pair_start([("ffn1_w_up", gw_u1)], token, "ffn1b")
    dn1 = _nt_panel([dg1, du1], [wg1, wu1], token, "ffn1_bwd_in")
    red_ffn1b, token = scatter_start(pair_ffn1b, dn1)
    grad_x, d_meta, d_ffn1 = _rms_bwd_first(dn1, hs0, ffn1_norm, dhs1, token)

    half_ffn2, tok = finish_sum(red_ffn2, grad_x)
    half_mix, tok = finish_sum(red_mix, tok)
    big_out = finish_adam(half_ffn2, tok)
    half_ffn1a, tok = finish_sum(red_ffn1a, big_out["ffn2_w_down"][1])
    big_out.update(finish_adam(half_mix, tok))
    half_ffn1b, tok = finish_sum(red_ffn1b, big_out["w_out"][1])
    big_out.update(finish_adam(half_ffn1a, tok))
    big_out.update(finish_adam(half_ffn1b, big_out["ffn1_w_gate"][1]))

    W = C1
    rows = lambda a: a.reshape(-1, W)
    parts = [rows(d_ffn1), rows(d_mix), rows(d_ffn2), rows(d_final), rows(d_bin), d_bcf, d_lg, d_lb,
             d_wsc, d_wcf, rows(d_meta), jnp.broadcast_to(loss_row[:, :1], (1, W))]
    sizes = [p.shape[0] for p in parts]
    total_rows = sum(sizes)
    packed = _pad_rows(jnp.concatenate(parts, axis=0), -(-total_rows // 8) * 8)
    summed = _share_small(packed, True, "sum_small", big_out["ffn1_w_up"][1])
    offs = [0]
    for n in sizes:
        offs.append(offs[-1] + n)
    piece = lambda k: summed[offs[k]:offs[k + 1]]
    loss = piece(11)[0, 0]
    g_ffn1, g_mix, g_ffn2 = (piece(k).reshape(1, D) for k in range(3))
    g_final = piece(3).reshape(1, D)
    g_bin = piece(4).reshape(1, -1)
    g_bcf, g_lg, g_lb = piece(5), piece(6), piece(7)
    g_wsc = lax.dynamic_slice_in_dim(piece(8), chip * cs, cs, axis=1)
    g_wcf = lax.dynamic_slice_in_dim(piece(9), chip * cs, cs, axis=1)
    g_meta = lax.dynamic_slice_in_dim(piece(10).reshape(N_META, D), chip * ms, ms, axis=1)

    small_names = ["meta_tokens", "ffn1_norm", "mix_norm", "b_in", "conv_sc_w", "conv_cf_w", "conv_cf_b", "ln_cf_g",
                   "ln_cf_b", "ffn2_norm", "final_norm"]
    small_w = [meta_tokens, ffn1_norm, mix_norm, b_in, conv_sc_w[0], conv_cf_w[0], conv_cf_b, ln_cf_g, ln_cf_b,
               ffn2_norm, final_norm.reshape(1, D)]
    small_g = [g_meta, g_ffn1, g_mix, g_bin, g_wsc, g_wcf, g_bcf, g_lg, g_lb, g_ffn2, g_final]
    small_m = [m_meta_tokens, m_ffn1_norm, m_mix_norm, m_b_in, m_conv_sc_w[0], m_conv_cf_w[0], m_conv_cf_b, m_ln_cf_g,
               m_ln_cf_b, m_ffn2_norm, m_final_norm.reshape(1, D)]
    small_v = [v_meta_tokens, v_ffn1_norm, v_mix_norm, v_b_in, v_conv_sc_w[0], v_conv_cf_w[0], v_conv_cf_b, v_ln_cf_g,
               v_ln_cf_b, v_ffn2_norm, v_final_norm.reshape(1, D)]
    s_d, s_m, s_v = _adamw_small(small_w, small_g, small_m, small_v)
    shapes = {"conv_sc_w": conv_sc_w.shape, "conv_cf_w": conv_cf_w.shape, "final_norm": final_norm.shape}
    small_out = {}
    for nm, g, d, m, v in zip(small_names, small_g, s_d, s_m, s_v):
        shp = shapes.get(nm, g.shape)
        small_out[nm] = tuple(t.reshape(shp) for t in (g, d, m, v))

    order = ["meta_tokens", "ffn1_norm", "ffn1_w_gate", "ffn1_w_up", "ffn1_w_down", "mix_norm", "w_in", "b_in",
             "conv_sc_w", "conv_cf_w", "conv_cf_b", "ln_cf_g", "ln_cf_b", "w_out", "ffn2_norm", "ffn2_w_gate",
             "ffn2_w_up", "ffn2_w_down", "final_norm"]
    res = {**big_out, **small_out}
    outs = [loss, grad_x[None]]
    for q in range(4):
        outs.extend(res[nm][q] for nm in order)
    return tuple(outs)
```

```python
import functools

import jax
import jax.numpy as jnp
from jax import lax
from jax.experimental import pallas as pl
from jax.experimental.pallas import tpu as pltpu

F32 = jnp.float32
BF16 = jnp.bfloat16
MESH = pl.DeviceIdType.MESH

N_META = 16
TT = 128
PAD = TT - N_META
HALO = 32
EPS = 1e-6
FFN_RES_SCALE = 0.5
N_CHIPS = 4
N_DEV = 8

ADAM_LR = 0.001
ADAM_B1 = 0.9
ADAM_B2 = 0.999
ADAM_EPS = 1e-08
ADAM_WD = 0.01
ADAM_STEP = 10

V7X_VMEM_BYTES = 64 * 2 ** 20
NT_DIMS = (((1,), (1,)), ((), ()))
TN_DIMS = (((0,), (0,)), ((), ()))


def _params(semantics, block_bytes):
    limit = min(2 * block_bytes + 16 * 2 ** 20, V7X_VMEM_BYTES - 6 * 2 ** 20)
    return pltpu.CompilerParams(dimension_semantics=semantics, vmem_limit_bytes=int(limit))


def _pallas(body, out_shape, **kw):
    if "grid" not in kw and "grid_spec" not in kw:
        return pl.pallas_call(body, out_shape=out_shape, **kw)
    big = lambda shape, dtype: jnp.issubdtype(dtype, jnp.floating) and len(shape) >= 2
    pin_out = lambda s: pltpu.HBM(s.shape, s.dtype) if big(s.shape, s.dtype) else s
    single = not isinstance(out_shape, (list, tuple))
    shapes = pin_out(out_shape) if single else [pin_out(s) for s in out_shape]
    call = pl.pallas_call(body, out_shape=shapes, **kw)
    pin = lambda a: pltpu.with_memory_space_constraint(a, pltpu.HBM) if big(a.shape, a.dtype) else a
    return lambda *operands: call(*[pin(a) for a in operands])


def _nbytes(shape, dtype):
    n = 1
    for d in shape:
        if d is not None:
            n *= d
    return n * jnp.dtype(dtype).itemsize


def _row_tile(rows, target, mult=8):
    best = None
    for t in range(mult, min(rows, target) + 1, mult):
        if rows % t == 0:
            best = t
    assert best is not None, (rows, target, mult)
    return best


def _sigmoid(v):
    return jax.nn.sigmoid(v)


def _dsilu(v, s):
    return s * (1.0 + v * (1.0 - s))


def _embed_rms(x2, meta, gain):
    S, D = x2.shape
    T = S + TT

    def body(x_ref, meta_ref, g_ref, hs_ref, n_ref):
        i = pl.program_id(0)

        @pl.when(i == 0)
        def _():
            hs_ref[...] = jnp.zeros_like(hs_ref)
            hs_ref[PAD:, :] = meta_ref[...]

        @pl.when(i > 0)
        def _():
            hs_ref[...] = x_ref[...]

        h = hs_ref[...]
        r = lax.rsqrt(jnp.mean(h * h, axis=-1, keepdims=True) + EPS)
        n_ref[...] = ((h * r) * g_ref[...]).astype(BF16)

    blk = _nbytes((TT, D), F32) * 2 + _nbytes((TT, D), BF16)
    return _pallas(
        body, name="embed_rms", grid=(T // TT,),
        in_specs=[pl.BlockSpec((TT, D), lambda i: (jnp.maximum(i - 1, 0), 0)),
                  pl.BlockSpec((N_META, D), lambda i: (0, 0)),
                  pl.BlockSpec((1, D), lambda i: (0, 0))],
        out_specs=[pl.BlockSpec((TT, D), lambda i: (i, 0)), pl.BlockSpec((TT, D), lambda i: (i, 0))],
        out_shape=[jax.ShapeDtypeStruct((T, D), F32), jax.ShapeDtypeStruct((T, D), BF16)],
        compiler_params=_params(("parallel",), blk),
    )(x2, meta, gain)


def _rms_bwd_math(dn, h, g):
    r = lax.rsqrt(jnp.mean(h * h, axis=-1, keepdims=True) + EPS)
    xh = h * r
    dgain = jnp.sum(dn * xh, axis=0, keepdims=True)
    dxh = dn * g
    dh = r * (dxh - xh * jnp.mean(dxh * xh, axis=-1, keepdims=True))
    return dh, dgain


def _rms_bwd(dn, hs, gain, dres, scale, name):
    T, D = hs.shape
    te = _row_tile(T, 384)

    def body(dn_ref, h_ref, g_ref, dres_ref, dhs_ref, dhb_ref, dg_ref):
        dh, dgain = _rms_bwd_math(dn_ref[...], h_ref[...], g_ref[...])
        d = dres_ref[...] + dh
        dhs_ref[...] = d
        dhb_ref[...] = (scale * d).astype(BF16)

        @pl.when(pl.program_id(0) == 0)
        def _():
            dg_ref[...] = jnp.zeros_like(dg_ref)

        dg_ref[...] += dgain

    blk = _nbytes((te, D), F32) * 4 + _nbytes((te, D), BF16)
    row = lambda i: (i, 0)
    return _pallas(
        body, name=name, grid=(T // te,),
        in_specs=[pl.BlockSpec((te, D), row), pl.BlockSpec((te, D), row), pl.BlockSpec((1, D), lambda i: (0, 0)),
                  pl.BlockSpec((te, D), row)],
        out_specs=[pl.BlockSpec((te, D), row), pl.BlockSpec((te, D), row), pl.BlockSpec((1, D), lambda i: (0, 0))],
        out_shape=[jax.ShapeDtypeStruct((T, D), F32), jax.ShapeDtypeStruct((T, D), BF16),
                   jax.ShapeDtypeStruct((1, D), F32)],
        compiler_params=_params(("arbitrary",), blk),
    )(dn, hs, gain, dres)


def _rms_bwd_first(dn, hs, gain, dres, after):
    T, D = hs.shape
    S = T - TT

    def body(dn_ref, h_ref, g_ref, dres_ref, after_ref, gx_ref, gm_ref, dg_ref):
        i = pl.program_id(0)
        dh, dgain = _rms_bwd_math(dn_ref[...], h_ref[...], g_ref[...])
        d = dres_ref[...] + dh

        @pl.when(i == 0)
        def _():
            dg_ref[...] = jnp.zeros_like(dg_ref)
            gm_ref[...] = d[PAD:, :]

        @pl.when(i > 0)
        def _():
            gx_ref[...] = d

        dg_ref[...] += dgain

    blk = _nbytes((TT, D), F32) * 4
    row = lambda i: (i, 0)
    return _pallas(
        body, name="rms_bwd_ffn1", grid=(T // TT,),
        in_specs=[pl.BlockSpec((TT, D), row), pl.BlockSpec((TT, D), row), pl.BlockSpec((1, D), lambda i: (0, 0)),
                  pl.BlockSpec((TT, D), row), TOKEN],
        out_specs=[pl.BlockSpec((TT, D), lambda i: (jnp.maximum(i - 1, 0), 0)),
                   pl.BlockSpec((N_META, D), lambda i: (0, 0)), pl.BlockSpec((1, D), lambda i: (0, 0))],
        out_shape=[jax.ShapeDtypeStruct((S, D), F32), jax.ShapeDtypeStruct((N_META, D), F32),
                   jax.ShapeDtypeStruct((1, D), F32)],
        compiler_params=_params(("arbitrary",), blk),
    )(dn, hs, gain, dres, after)


def _final_loss(hs, gain, tgt):
    T, D = hs.shape

    def body(h_ref, g_ref, t_ref, dhs_ref, dhb_ref, loss_ref, dg_ref):
        i = pl.program_id(0)
        h = h_ref[...]
        g = g_ref[...]
        r = lax.rsqrt(jnp.mean(h * h, axis=-1, keepdims=True) + EPS)
        xh = h * r
        e = jnp.where(i > 0, xh * g - t_ref[...], 0.0)
        tile_loss = jnp.sum(jnp.sum(e * e, axis=1, keepdims=True), axis=0, keepdims=True) * (0.5 / D)
        dout = e * (1.0 / D)
        dgain = jnp.sum(dout * xh, axis=0, keepdims=True)
        dxh = dout * g
        d = r * (dxh - xh * jnp.mean(dxh * xh, axis=-1, keepdims=True))
        dhs_ref[...] = d
        dhb_ref[...] = (FFN_RES_SCALE * d).astype(BF16)

        @pl.when(i == 0)
        def _():
            loss_ref[...] = jnp.zeros_like(loss_ref)
            dg_ref[...] = jnp.zeros_like(dg_ref)

        loss_ref[...] += jnp.broadcast_to(tile_loss, loss_ref.shape)
        dg_ref[...] += dgain

    blk = _nbytes((TT, D), F32) * 3 + _nbytes((TT, D), BF16)
    row = lambda i: (i, 0)
    return _pallas(
        body, name="final_loss", grid=(T // TT,),
        in_specs=[pl.BlockSpec((TT, D), row), pl.BlockSpec((1, D), lambda i: (0, 0)),
                  pl.BlockSpec((TT, D), lambda i: (jnp.maximum(i - 1, 0), 0))],
        out_specs=[pl.BlockSpec((TT, D), row), pl.BlockSpec((TT, D), row),
                   pl.BlockSpec((1, 128), lambda i: (0, 0)), pl.BlockSpec((1, D), lambda i: (0, 0))],
        out_shape=[jax.ShapeDtypeStruct((T, D), F32), jax.ShapeDtypeStruct((T, D), BF16),
                   jax.ShapeDtypeStruct((1, 128), F32), jax.ShapeDtypeStruct((1, D), F32)],
        compiler_params=_params(("arbitrary",), blk),
    )(hs, gain, tgt)


MXU_COLS = 256


def _tm(T):
    return _row_tile(T, 704, 16)


def _col_chunks(n):
    return [(c, min(MXU_COLS, n - c)) for c in range(0, n, MXU_COLS)]


TOKEN = pl.BlockSpec((8, 128), lambda *_: (0, 0))


def _ffn_up(n, wg, wu, shards, prev, after, name):
    T, D = n.shape
    Fs = wg.shape[2]
    tm = _tm(T)
    nprev = 0 if prev is None else 3

    def body(shards_ref, n_ref, wg_ref, wu_ref, after_ref, *refs):
        g_ref, u_ref, a_ref = refs[nprev:]
        nn = n_ref[...]
        for c0, cw in _col_chunks(Fs):
            if 2 * cw == MXU_COLS:
                both = jnp.concatenate([wg_ref[:, c0:c0 + cw], wu_ref[:, c0:c0 + cw]], axis=1)
                gu = jnp.dot(nn, both, preferred_element_type=F32)
                g, u = gu[:, :cw], gu[:, cw:]
            else:
                g = jnp.dot(nn, wg_ref[:, c0:c0 + cw], preferred_element_type=F32)
                u = jnp.dot(nn, wu_ref[:, c0:c0 + cw], preferred_element_type=F32)
            g_ref[:, c0:c0 + cw] = g.astype(BF16)
            u_ref[:, c0:c0 + cw] = u.astype(BF16)
            a_ref[:, c0:c0 + cw] = (jax.nn.silu(g) * u).astype(BF16)

    blk = _nbytes((tm, D), BF16) + 2 * _nbytes((D, Fs), BF16) + 3 * _nbytes((tm, Fs), BF16)
    out = pl.BlockSpec((tm, Fs), lambda j, i, p: (i, p[j]))
    shp = jax.ShapeDtypeStruct((T, N_CHIPS * Fs), BF16)
    return _pallas(
        body, name=name,
        grid_spec=pltpu.PrefetchScalarGridSpec(
            num_scalar_prefetch=1, grid=(shards.shape[0], T // tm),
            in_specs=[pl.BlockSpec((tm, D), lambda j, i, p: (i, 0)),
                      pl.BlockSpec((None, D, Fs), lambda j, i, p: (p[j], 0, 0)),
                      pl.BlockSpec((None, D, Fs), lambda j, i, p: (p[j], 0, 0)), TOKEN] + [ANY] * nprev,
            out_specs=[out, out, out]),
        out_shape=[shp, shp, shp], input_output_aliases={5 + q: q for q in range(nprev)},
        compiler_params=_params(("arbitrary", "arbitrary"), blk),
    )(shards, n, wg, wu, after, *(prev or ()))


def _ffn_down(a, wd, hs, shards, name, after=None, gain=None):
    T, F = a.shape
    _, Fs, D = wd.shape
    tm = _tm(T) if gain is None else _row_tile(T, 352, 16)
    tn = D // 2 if gain is None else D
    extra = [] if after is None else [after]
    gains = [] if gain is None else [gain]
    nk = shards.shape[0]
    nrest = len(extra) + len(gains)

    def body(shards_ref, a_ref, w_ref, h_ref, *refs):
        o_ref = refs[nrest]
        k = pl.program_id(2)
        part = FFN_RES_SCALE * jnp.dot(a_ref[...], w_ref[...], preferred_element_type=F32)

        @pl.when(k == 0)
        def _():
            o_ref[...] = h_ref[...] + part

        @pl.when(k > 0)
        def _():
            o_ref[...] += part

        if gains:
            @pl.when(k == nk - 1)
            def _():
                o = o_ref[...]
                r = lax.rsqrt(jnp.mean(o * o, axis=-1, keepdims=True) + EPS)
                refs[nrest + 1][...] = ((o * r) * refs[nrest - 1][...]).astype(BF16)

    blk = _nbytes((tm, Fs), BF16) + _nbytes((Fs, tn), BF16) + 3 * _nbytes((tm, tn), F32) + _nbytes((tm, tn), BF16)
    tile = pl.BlockSpec((tm, tn), lambda n, i, k, p: (i, n))
    res = _pallas(
        body, name=name,
        grid_spec=pltpu.PrefetchScalarGridSpec(
            num_scalar_prefetch=1, grid=(D // tn, T // tm, nk),
            in_specs=[pl.BlockSpec((tm, Fs), lambda n, i, k, p: (i, p[k])),
                      pl.BlockSpec((None, Fs, tn), lambda n, i, k, p: (p[k], 0, n)), tile]
                     + [TOKEN] * len(extra) + [pl.BlockSpec((1, D), lambda n, i, k, p: (0, 0))] * len(gains),
            out_specs=[tile] * (1 + len(gains))),
        out_shape=[jax.ShapeDtypeStruct((T, D), F32)] + [jax.ShapeDtypeStruct((T, D), BF16)] * len(gains),
        compiler_params=_params(("parallel", "parallel", "arbitrary"), blk),
    )(shards, a, wd, hs, *extra, *gains)
    return res[0] if gain is None else tuple(res)


def _ffn_down_whole(a, wd, hs, name):
    T, F = a.shape
    D = wd.shape[1]
    tm = _tm(T)
    tn = D // 4

    def body(a_ref, w_ref, h_ref, o_ref):
        o_ref[...] = h_ref[...] + FFN_RES_SCALE * jnp.dot(a_ref[...], w_ref[...], preferred_element_type=F32)

    blk = _nbytes((tm, F), BF16) + _nbytes((F, tn), BF16) + 3 * _nbytes((tm, tn), F32)
    return _pallas(
        body, name=name, grid=(D // tn, T // tm),
        in_specs=[pl.BlockSpec((tm, F), lambda n, i: (i, 0)), pl.BlockSpec((F, tn), lambda n, i: (0, n)),
                  pl.BlockSpec((tm, tn), lambda n, i: (i, n))],
        out_specs=pl.BlockSpec((tm, tn), lambda n, i: (i, n)),
        out_shape=jax.ShapeDtypeStruct((T, D), F32),
        compiler_params=_params(("parallel", "parallel"), blk),
    )(a, wd, hs)


def _mix_in(n, w, b):
    T, D = n.shape
    Ns = w.shape[2]
    tm = _tm(T)

    def body(n_ref, w_ref, b_ref, u_ref):
        u_ref[...] = jnp.dot(n_ref[...], w_ref[...], preferred_element_type=F32) + b_ref[...]

    blk = _nbytes((tm, D), BF16) + _nbytes((D, Ns), BF16) + 2 * _nbytes((tm, Ns), F32)
    return _pallas(
        body, name="mix_in", grid=(N_CHIPS, T // tm),
        in_specs=[pl.BlockSpec((tm, D), lambda j, i: (i, 0)), pl.BlockSpec((None, D, Ns), lambda j, i: (j, 0, 0)),
                  pl.BlockSpec((1, Ns), lambda j, i: (0, j))],
        out_specs=pl.BlockSpec((tm, Ns), lambda j, i: (i, j)),
        out_shape=jax.ShapeDtypeStruct((T, N_CHIPS * Ns), F32),
        compiler_params=_params(("parallel", "parallel"), blk),
    )(n, w, b)


def _mix_out(y, w, hs, gain, after):
    T, D = y.shape
    tm = _row_tile(T, 352, 16)

    def body(y_ref, w_ref, h_ref, g_ref, after_ref, o_ref, n_ref):
        o = h_ref[...] + jnp.dot(y_ref[...], w_ref[...], preferred_element_type=F32)
        o_ref[...] = o
        r = lax.rsqrt(jnp.mean(o * o, axis=-1, keepdims=True) + EPS)
        n_ref[...] = ((o * r) * g_ref[...]).astype(BF16)

    blk = 2 * _nbytes((tm, D), BF16) + _nbytes((D, D), BF16) + 3 * _nbytes((tm, D), F32)
    row = pl.BlockSpec((tm, D), lambda i: (i, 0))
    return _pallas(
        body, name="mix_out", grid=(T // tm,),
        in_specs=[row, pl.BlockSpec((D, D), lambda i: (0, 0)), row, pl.BlockSpec((1, D), lambda i: (0, 0)), TOKEN],
        out_specs=[row, row],
        out_shape=[jax.ShapeDtypeStruct((T, D), F32), jax.ShapeDtypeStruct((T, D), BF16)],
        compiler_params=_params(("parallel",), blk),
    )(y, w, hs, gain, after)


def _ffn_bwd_act(dfb, wd, g, u, after, name):
    T, D = dfb.shape
    F = wd.shape[0]
    tm = _row_tile(T, 1408, 16)
    tn = 2 * MXU_COLS

    tr = _row_tile(tm, 352, 16)

    def body(d_ref, w_ref, g_ref, u_ref, after_ref, dg_ref, du_ref):
        for r0 in range(0, tm, tr):
            dv = d_ref[r0:r0 + tr, :]
            for c0, cw in _col_chunks(tn):
                da = lax.dot_general(dv, w_ref[c0:c0 + cw, :], NT_DIMS, preferred_element_type=F32)
                gv = g_ref[r0:r0 + tr, c0:c0 + cw].astype(F32)
                uv = u_ref[r0:r0 + tr, c0:c0 + cw].astype(F32)
                s = _sigmoid(gv)
                du_ref[r0:r0 + tr, c0:c0 + cw] = (da * (gv * s)).astype(BF16)
                dg_ref[r0:r0 + tr, c0:c0 + cw] = (da * uv * _dsilu(gv, s)).astype(BF16)

    blk = _nbytes((tm, D), BF16) + _nbytes((tn, D), BF16) + 4 * _nbytes((tm, tn), BF16)
    io = pl.BlockSpec((tm, tn), lambda n, i: (i, n))
    shp = jax.ShapeDtypeStruct((T, F), BF16)
    return _pallas(
        body, name=name, grid=(F // tn, T // tm),
        in_specs=[pl.BlockSpec((tm, D), lambda n, i: (i, 0)), pl.BlockSpec((tn, D), lambda n, i: (n, 0)), io, io, TOKEN],
        out_specs=[io, io], out_shape=[shp, shp],
        compiler_params=_params(("parallel", "parallel"), blk),
    )(dfb, wd, g, u, after)


def _nt_panel(lhs_list, w_list, after, name):
    T = lhs_list[0].shape[0]
    nsh, Dout, Ks = w_list[0].shape
    npair = len(lhs_list)
    tm = _row_tile(T, 1408, 16)
    tn = Dout // 2

    def body(*refs):
        l_refs, w_refs, o_ref = refs[:npair], refs[npair:2 * npair], refs[2 * npair + 1]
        j = pl.program_id(2)
        k0 = Ks - Ks % MXU_COLS if npair == 2 and 2 * (Ks % MXU_COLS) == MXU_COLS else Ks
        acc = None
        for p in range(npair):
            part = lax.dot_general(l_refs[p][:, :k0], w_refs[p][:, :k0], NT_DIMS, preferred_element_type=F32)
            acc = part if acc is None else acc + part
        if k0 < Ks:
            lhs = jnp.concatenate([l_refs[p][:, k0:] for p in range(npair)], axis=1)
            rhs = jnp.concatenate([w_refs[p][:, k0:] for p in range(npair)], axis=1)
            acc = acc + lax.dot_general(lhs, rhs, NT_DIMS, preferred_element_type=F32)

        @pl.when(j == 0)
        def _():
            o_ref[...] = acc

        @pl.when(j > 0)
        def _():
            o_ref[...] += acc

    blk = npair * (_nbytes((tm, Ks), BF16) + _nbytes((tn, Ks), BF16)) + 2 * _nbytes((tm, tn), F32)
    return _pallas(
        body, name=name, grid=(Dout // tn, T // tm, nsh),
        in_specs=[pl.BlockSpec((tm, Ks), lambda n, i, j: (i, j))] * npair
                 + [pl.BlockSpec((None, tn, Ks), lambda n, i, j: (j, n, 0))] * npair + [TOKEN],
        out_specs=pl.BlockSpec((tm, tn), lambda n, i, j: (i, n)),
        out_shape=jax.ShapeDtypeStruct((T, Dout), F32),
        compiler_params=_params(("parallel", "parallel", "arbitrary"), blk),
    )(*lhs_list, *w_list, after)


def _tn_call(name, grid, lhs, lhs_spec, rhs_list, rhs_specs, out_shapes, out_specs, blk, after=None):
    nr = len(rhs_list)
    extra = [] if after is None else [after]

    def body(*refs):
        l_ref, r_refs, o_refs = refs[0], refs[1:1 + nr], refs[len(refs) - nr:]
        k = pl.program_id(len(grid) - 1)
        lv = l_ref[...]
        for q in range(nr):
            part = lax.dot_general(lv, r_refs[q][...], TN_DIMS, preferred_element_type=F32)
            part = part.reshape(o_refs[q].shape)

            @pl.when(k == 0)
            def _(o=o_refs[q], part=part):
                o[...] = part

            @pl.when(k > 0)
            def _(o=o_refs[q], part=part):
                o[...] += part

    return _pallas(
        body, name=name, grid=grid, in_specs=[lhs_spec] + rhs_specs + [TOKEN] * len(extra), out_specs=out_specs,
        out_shape=out_shapes, compiler_params=_params(("parallel",) * (len(grid) - 1) + ("arbitrary",), blk),
    )(lhs, *rhs_list, *extra)


def _tk(T):
    return T


def _wgrad_cols(n, rhs_list, name, after=None):
    T, D = n.shape
    Ns = rhs_list[0].shape[1] // N_CHIPS
    tk = _tk(T)
    nr = len(rhs_list)
    tm = D // 4
    blk = _nbytes((tk, tm), BF16) + nr * (_nbytes((tk, Ns), BF16) + 2 * _nbytes((tm, Ns), F32))
    return _tn_call(
        name, (N_CHIPS, D // tm, T // tk), n, pl.BlockSpec((tk, tm), lambda j, m, k: (k, m)),
        rhs_list, [pl.BlockSpec((tk, Ns), lambda j, m, k: (k, j))] * nr,
        [jax.ShapeDtypeStruct((N_CHIPS, 2, D // 2, Ns), F32)] * nr,
        [pl.BlockSpec((None, None, tm, Ns), lambda j, m, k: (j, m // 2, m % 2, 0))] * nr, blk, after)


def _wgrad_down(a, dfb, name):
    T, F = a.shape
    D = dfb.shape[1]
    Fs = F // N_CHIPS
    tk = _tk(T)
    tn = D // 4
    blk = _nbytes((tk, Fs), BF16) + _nbytes((tk, tn), BF16) + 2 * _nbytes((Fs, tn), F32)
    return _tn_call(
        name, (N_CHIPS, D // tn, T // tk), a, pl.BlockSpec((tk, Fs), lambda j, n, k: (k, j)),
        [dfb], [pl.BlockSpec((tk, tn), lambda j, n, k: (k, n))],
        [jax.ShapeDtypeStruct((N_CHIPS, 2, Fs // 2, D), F32)],
        [pl.BlockSpec((None, 2, Fs // 2, tn), lambda j, n, k: (j, 0, 0, n))], blk)[0]


def _wgrad_out(y, dmb):
    T, D = y.shape
    tk = _tk(T)
    tn = D // 2
    rows = D // (2 * N_CHIPS)
    blk = _nbytes((tk, D // 2), BF16) + _nbytes((tk, tn), BF16) + 2 * _nbytes((D // 2, tn), F32)
    return _tn_call(
        "wgrad_w_out", (2, D // tn, T // tk), y, pl.BlockSpec((tk, D // 2), lambda m, n, k: (k, m)),
        [dmb], [pl.BlockSpec((tk, tn), lambda m, n, k: (k, n))],
        [jax.ShapeDtypeStruct((N_CHIPS, 2, rows, D), F32)],
        [pl.BlockSpec((2, 2, rows, tn), lambda m, n, k: (m, 0, 0, n))], blk)[0]


def _row_masks(i, last):
    rows = i * TT + lax.broadcasted_iota(jnp.int32, (TT, 1), 0)
    prows = i * TT - HALO + lax.broadcasted_iota(jnp.int32, (HALO, 1), 0)
    return rows >= PAD, (prows >= PAD) & (i > 0), i < last


def _conv_inputs(u, up, mask_c, mask_p, zbuf, pbuf, C1):
    b, c, v, a, g = (u[:, k * C1:(k + 1) * C1] for k in range(5))
    cp, vp, ap, gp = (up[:, k * C1:(k + 1) * C1] for k in range(1, 5))
    sg = _sigmoid(g)
    pbuf[0:HALO, :] = jnp.where(mask_p, cp * vp, 0.0)
    pbuf[HALO:, :] = jnp.where(mask_c, c * v, 0.0)
    if zbuf is not None:
        zbuf[0:HALO, :] = jnp.where(mask_p, ap * _sigmoid(gp), 0.0)
        zbuf[HALO:, :] = jnp.where(mask_c, a * sg, 0.0)
    return b, c, v, a, sg


SUBLANES = 8
SHIFT_ROWS = TT + HALO - SUBLANES


def _shifted_scratch(C1):
    return pltpu.VMEM((SUBLANES - 1, SHIFT_ROWS, C1), F32)


def _fill_shifted(buf, sh):
    for r in range(1, SUBLANES):
        sh[r - 1] = buf[r:r + SHIFT_ROWS, :]


LANES = 128


def _window(buf, sh, lo, c0):
    if sh is None or lo % SUBLANES == 0:
        return buf[lo:lo + TT, c0:c0 + LANES]
    q, r = divmod(lo, SUBLANES)
    return sh[r - 1, q * SUBLANES:q * SUBLANES + TT, c0:c0 + LANES]


def _tap_sum(w_ref, buf, sh, starts):
    chunks = []
    for c0 in range(0, buf.shape[1], LANES):
        acc = None
        for k, lo in enumerate(starts):
            term = w_ref[k:k + 1, c0:c0 + LANES] * _window(buf, sh, lo, c0)
            acc = term if acc is None else acc + term
        chunks.append(acc)
    return jnp.concatenate(chunks, axis=1)


def _causal_conv(w_ref, buf, sh=None):
    K = w_ref.shape[0]
    return _tap_sum(w_ref, buf, sh, [HALO - (K - 1) + k for k in range(K)])


def _anticausal_conv(w_ref, buf, sh=None):
    K = w_ref.shape[0]
    return _tap_sum(w_ref, buf, sh, [K - 1 - k for k in range(K)])


def _conv_weight_sums(dw_ref, dy, buf, sh=None):
    K = dw_ref.shape[0]
    for c0 in range(0, buf.shape[1], LANES):
        dyc = dy[:, c0:c0 + LANES]
        for k in range(K):
            prod = dyc * _window(buf, sh, HALO - (K - 1) + k, c0)
            dw_ref[k:k + 1, c0:c0 + LANES] += jnp.sum(prod, axis=0, keepdims=True)


def _layernorm_stats(z1):
    mu = jnp.mean(z1, axis=-1, keepdims=True)
    zc = z1 - mu
    rs = lax.rsqrt(jnp.mean(zc * zc, axis=-1, keepdims=True) + EPS)
    return zc * rs, rs


def _mixer_specs(T, DIN, C1, ksc, kcf):
    cur = pl.BlockSpec((TT, DIN), lambda i: (i, 0))
    prev = pl.BlockSpec((HALO, DIN), lambda i: (jnp.maximum(i * (TT // HALO) - 1, 0), 0))
    full = lambda r: pl.BlockSpec((r, C1), lambda i: (0, 0))
    return cur, prev, [full(ksc), full(kcf), full(1), full(1), full(1)]


def _mix_conv_fwd(u, wsc, wcf, bcf, lg, lb):
    T, DIN = u.shape
    C1 = DIN // 5
    last = T // TT - 1

    def body(u_ref, up_ref, wsc_ref, wcf_ref, bcf_ref, lg_ref, lb_ref, y_ref, z1_ref, zbuf, pbuf, zsh):
        i = pl.program_id(0)
        mask_c, mask_p, _ = _row_masks(i, last)
        b, _, _, _, _ = _conv_inputs(u_ref[...], up_ref[...], mask_c, mask_p, zbuf, pbuf, C1)
        _fill_shifted(zbuf, zsh)
        cs = _causal_conv(wsc_ref, pbuf)
        z1 = _causal_conv(wcf_ref, zbuf, zsh) + bcf_ref[...]
        z1_ref[...] = z1
        zh, _ = _layernorm_stats(z1)
        ln = zh * lg_ref[...] + lb_ref[...]
        y_ref[:, 0:C1] = jnp.where(mask_c, b * cs, 0.0).astype(BF16)
        y_ref[:, C1:] = jnp.where(mask_c, jax.nn.silu(ln), 0.0).astype(BF16)

    cur, prev, small = _mixer_specs(T, DIN, C1, wsc.shape[0], wcf.shape[0])
    blk = _nbytes((TT + HALO, DIN), F32) + _nbytes((TT, 2 * C1), BF16) + 12 * _nbytes((TT + HALO, C1), F32)
    return _pallas(
        body, name="mix_conv_fwd", grid=(T // TT,),
        in_specs=[cur, prev] + small,
        out_specs=[pl.BlockSpec((TT, 2 * C1), lambda i: (i, 0)), pl.BlockSpec((TT, C1), lambda i: (i, 0))],
        out_shape=[jax.ShapeDtypeStruct((T, 2 * C1), BF16), jax.ShapeDtypeStruct((T, C1), F32)],
        scratch_shapes=[pltpu.VMEM((TT + HALO, C1), F32), pltpu.VMEM((TT + HALO, C1), F32), _shifted_scratch(C1)],
        compiler_params=_params(("arbitrary",), blk),
    )(u, u, wsc, wcf, bcf, lg, lb)


def _mix_conv_bwd1(u, z1, dy, wsc, lg, lb):
    T, DIN = u.shape
    C1 = DIN // 5
    last = T // TT - 1

    def body(u_ref, up_ref, z1_ref, dy_ref, wsc_ref, lg_ref, lb_ref,
             dz1_ref, dcs_ref, db_ref, dlg_ref, dlb_ref, dbcf_ref, pbuf):
        i = pl.program_id(0)
        mask_c, mask_p, _ = _row_masks(i, last)
        b, _, _, _, _ = _conv_inputs(u_ref[...], up_ref[...], mask_c, mask_p, None, pbuf, C1)
        cs = _causal_conv(wsc_ref, pbuf)
        zh, rs = _layernorm_stats(z1_ref[...])
        ln = zh * lg_ref[...] + lb_ref[...]
        dy = dy_ref[...]
        dysc = jnp.where(mask_c, dy[:, 0:C1], 0.0)
        dycf = jnp.where(mask_c, dy[:, C1:], 0.0)
        db_ref[...] = (dysc * cs).astype(BF16)
        dcs_ref[...] = dysc * b
        dl = dycf * _dsilu(ln, _sigmoid(ln))
        dzh = dl * lg_ref[...]
        dz1 = rs * (dzh - jnp.mean(dzh, axis=-1, keepdims=True) - zh * jnp.mean(dzh * zh, axis=-1, keepdims=True))
        dz1_ref[...] = dz1

        @pl.when(i == 0)
        def _():
            dlg_ref[...] = jnp.zeros_like(dlg_ref)
            dlb_ref[...] = jnp.zeros_like(dlb_ref)
            dbcf_ref[...] = jnp.zeros_like(dbcf_ref)

        dlg_ref[...] += jnp.sum(dl * zh, axis=0, keepdims=True)
        dlb_ref[...] += jnp.sum(dl, axis=0, keepdims=True)
        dbcf_ref[...] += jnp.sum(dz1, axis=0, keepdims=True)

    cur, prev, small = _mixer_specs(T, DIN, C1, wsc.shape[0], 1)
    tile = lambda: pl.BlockSpec((TT, C1), lambda i: (i, 0))
    vec = lambda: pl.BlockSpec((1, C1), lambda i: (0, 0))
    blk = _nbytes((TT + HALO, DIN), F32) + 5 * _nbytes((TT, C1), F32) + 12 * _nbytes((TT + HALO, C1), F32)
    return _pallas(
        body, name="mix_conv_bwd1", grid=(T // TT,),
        in_specs=[cur, prev, tile(), pl.BlockSpec((TT, 2 * C1), lambda i: (i, 0)), small[0], small[3], small[4]],
        out_specs=[tile(), tile(), tile(), vec(), vec(), vec()],
        out_shape=[jax.ShapeDtypeStruct((T, C1), F32), jax.ShapeDtypeStruct((T, C1), F32),
                   jax.ShapeDtypeStruct((T, C1), BF16)] + [jax.ShapeDtypeStruct((1, C1), F32)] * 3,
        scratch_shapes=[pltpu.VMEM((TT + HALO, C1), F32)],
        compiler_params=_params(("arbitrary",), blk),
    )(u, u, z1, dy, wsc, lg, lb)


def _mix_conv_bwd2(u, dz1, dcs, db, wsc, wcf):
    T, DIN = u.shape
    C1 = DIN // 5
    last = T // TT - 1
    ksc, kcf = wsc.shape[0], wcf.shape[0]

    def body(u_ref, up_ref, dz_ref, dzn_ref, dc_ref, dcn_ref, db_ref, wsc_ref, wcf_ref,
             du_ref, dbin_ref, dwsc_ref, dwcf_ref, zbuf, pbuf, dzbuf, dcbuf, zsh, dzsh):
        i = pl.program_id(0)
        mask_c, mask_p, has_next = _row_masks(i, last)
        _, c, v, a, sg = _conv_inputs(u_ref[...], up_ref[...], mask_c, mask_p, zbuf, pbuf, C1)
        dz1 = dz_ref[...]
        dcs = dc_ref[...]
        dzbuf[0:TT, :] = dz1
        dzbuf[TT:, :] = jnp.where(has_next, dzn_ref[...], 0.0)
        dcbuf[0:TT, :] = dcs
        dcbuf[TT:, :] = jnp.where(has_next, dcn_ref[...], 0.0)

        @pl.when(i == 0)
        def _():
            dbin_ref[...] = jnp.zeros_like(dbin_ref)
            dwsc_ref[...] = jnp.zeros_like(dwsc_ref)
            dwcf_ref[...] = jnp.zeros_like(dwcf_ref)

        _fill_shifted(zbuf, zsh)
        _fill_shifted(dzbuf, dzsh)
        _conv_weight_sums(dwcf_ref, dz1, zbuf, zsh)
        _conv_weight_sums(dwsc_ref, dcs, pbuf)
        dz0 = jnp.where(mask_c, _anticausal_conv(wcf_ref, dzbuf, dzsh), 0.0)
        dp = jnp.where(mask_c, _anticausal_conv(wsc_ref, dcbuf), 0.0)
        parts = (db_ref[...].astype(F32), dp * v, dp * c, dz0 * sg, dz0 * a * sg * (1.0 - sg))
        for k, part in enumerate(parts):
            du_ref[:, k * C1:(k + 1) * C1] = part.astype(BF16)
            dbin_ref[:, k * C1:(k + 1) * C1] += jnp.sum(part, axis=0, keepdims=True)

    cur, prev, small = _mixer_specs(T, DIN, C1, ksc, kcf)
    tile = lambda: pl.BlockSpec((TT, C1), lambda i: (i, 0))
    nxt = lambda: pl.BlockSpec((HALO, C1), lambda i: (jnp.minimum((i + 1) * (TT // HALO), T // HALO - 1), 0))
    blk = (_nbytes((TT + HALO, DIN), F32) + _nbytes((TT, DIN), BF16) + 5 * _nbytes((TT, C1), F32)
           + 16 * _nbytes((TT + HALO, C1), F32))
    buf = lambda: pltpu.VMEM((TT + HALO, C1), F32)
    return _pallas(
        body, name="mix_conv_bwd2", grid=(T // TT,),
        in_specs=[cur, prev, tile(), nxt(), tile(), nxt(), tile(), small[0], small[1]],
        out_specs=[pl.BlockSpec((TT, DIN), lambda i: (i, 0)), pl.BlockSpec((1, DIN), lambda i: (0, 0)),
                   pl.BlockSpec((ksc, C1), lambda i: (0, 0)), pl.BlockSpec((kcf, C1), lambda i: (0, 0))],
        out_shape=[jax.ShapeDtypeStruct((T, DIN), BF16), jax.ShapeDtypeStruct((1, DIN), F32),
                   jax.ShapeDtypeStruct((ksc, C1), F32), jax.ShapeDtypeStruct((kcf, C1), F32)],
        scratch_shapes=[buf(), buf(), buf(), buf(), _shifted_scratch(C1), _shifted_scratch(C1)],
        compiler_params=_params(("arbitrary",), blk),
    )(u, u, dz1, dz1, dcs, dcs, db, wsc, wcf)


def _place():
    x, y, c = lax.axis_index("x"), lax.axis_index("y"), lax.axis_index("c")
    chips = [(1 - x, y), (x, 1 - y), (1 - x, 1 - y)]
    return x, y, c, chips


ANY = pl.BlockSpec(memory_space=pl.ANY)


def _cast_own_block(place, w, name):
    R, C = w.shape
    tr = _row_tile(R // 2, 1024, 16)
    nblk = R // 2 // tr

    def body(place_ref, w_ref, o_ref):
        o_ref[...] = w_ref[...].astype(BF16)

    return _pallas(
        body, name=name,
        grid_spec=pltpu.PrefetchScalarGridSpec(
            num_scalar_prefetch=1, grid=(2, nblk),
            in_specs=[pl.BlockSpec((tr, C), lambda h, i, p: (h * nblk + i, 0))],
            out_specs=pl.BlockSpec((None, None, tr, C), lambda h, i, p: (p[0], h, i, 0))),
        out_shape=jax.ShapeDtypeStruct((N_CHIPS, 2, R // 2, C), BF16),
        compiler_params=_params(("parallel", "parallel"), _nbytes((tr, C), F32) + _nbytes((tr, C), BF16)),
    )(place, w)


HBM = pl.BlockSpec(memory_space=pltpu.HBM)
SEM = pl.BlockSpec(memory_space=pltpu.SEMAPHORE)
EFFECT = pltpu.SideEffectType.DATAFLOW_SIDE_EFFECTING


def _gather_copies(refs, send, recv, rels=(0, 1, 2)):
    x, y, c, chips = _place()
    s = 2 * x + y
    n = len(rels)
    return [pltpu.make_async_remote_copy(src_ref=ref.at[s, c], dst_ref=ref.at[s, c], send_sem=send.at[n * w + k],
                                         recv_sem=recv.at[n * w + k], device_id=(*chips[r], c), device_id_type=MESH)
            for w, ref in enumerate(refs) for k, r in enumerate(rels)]


def _scatter_copies(refs, send, recv):
    x, y, c, chips = _place()
    nw = len(refs) // 2
    return [pltpu.make_async_remote_copy(src_ref=refs[w].at[2 * tx + ty], dst_ref=refs[nw + w].at[r],
                                         send_sem=send.at[3 * w + r], recv_sem=recv.at[3 * w + r],
                                         device_id=(tx, ty, c), device_id_type=MESH)
            for w in range(nw) for r, (tx, ty) in enumerate(chips)]


def _pair_copies(refs, send, recv):
    x, y, c, _ = _place()
    nw = len(refs) // 2
    return [pltpu.make_async_remote_copy(src_ref=refs[w].at[j, 1 - c], dst_ref=refs[nw + w].at[j],
                                         send_sem=send.at[N_CHIPS * w + j], recv_sem=recv.at[N_CHIPS * w + j],
                                         device_id=(x, y, 1 - c), device_id_type=MESH)
            for w in range(nw) for j in range(N_CHIPS)]


def _forward_copies(refs, send, recv, rels=(0, 1, 2)):
    x, y, c, chips = _place()
    n = len(rels)
    copies = []
    for w, ref in enumerate(refs):
        for k, r in enumerate(rels):
            tx, ty = chips[r]
            blk = ref.at[2 * tx + ty, c]
            copies.append(pltpu.make_async_remote_copy(src_ref=blk, dst_ref=blk, send_sem=send.at[n * w + k],
                                                       recv_sem=recv.at[n * w + k], device_id=(x, y, 1 - c),
                                                       device_id_type=MESH))
    return copies


def _half_copies(refs, send, recv):
    x, y, c, _ = _place()
    return [pltpu.make_async_remote_copy(src_ref=ref.at[c], dst_ref=ref.at[c], send_sem=send.at[w], recv_sem=recv.at[w],
                                         device_id=(x, y, 1 - c), device_id_type=MESH)
            for w, ref in enumerate(refs)]


def _start_copies(bufs, after, ncopies, make_copies, name):
    n = len(bufs)

    def body(*refs):
        in_refs, send, recv, token = refs[:n], refs[n + 1], refs[n + 2], refs[2 * n + 3]
        for cp in make_copies(in_refs, send, recv):
            cp.start()
        token[...] = jnp.zeros_like(token)

    outs = _pallas(
        body, name=name, in_specs=[HBM] * n + [ANY],
        out_specs=[SEM, SEM] + [HBM] * n + [pl.BlockSpec(memory_space=pltpu.VMEM)],
        out_shape=[pltpu.SemaphoreType.DMA((ncopies,)), pltpu.SemaphoreType.DMA((ncopies,))]
                  + [pltpu.HBM(b.shape, b.dtype) for b in bufs] + [jax.ShapeDtypeStruct((8, 128), F32)],
        input_output_aliases={k: 2 + k for k in range(n)},
        compiler_params=pltpu.CompilerParams(has_side_effects=EFFECT),
    )(*[pltpu.with_memory_space_constraint(b, pltpu.HBM) for b in bufs], after)
    return outs[0], outs[1], list(outs[2:2 + n]), outs[2 + n]


def _wait_copies(send, recv, bufs, after, make_copies, name):
    n = len(bufs)

    def body(*refs):
        in_refs, send_ref, recv_ref = refs[:n], refs[n], refs[n + 1]
        for cp in make_copies(in_refs, send_ref, recv_ref):
            cp.wait_send()
            cp.wait_recv()

    outs = _pallas(
        body, name=name, in_specs=[HBM] * n + [SEM, SEM, ANY], out_specs=[HBM] * n,
        out_shape=[pltpu.HBM(b.shape, b.dtype) for b in bufs],
        input_output_aliases={k: k for k in range(n)},
        compiler_params=pltpu.CompilerParams(has_side_effects=EFFECT),
    )(*bufs, send, recv, after)
    return list(outs)


def _forward_halves(bufs, name, rels=(0, 1, 2)):
    nw = len(bufs)
    n = len(rels)

    def body(*refs):
        o_refs = refs[nw:2 * nw]
        send, recv = refs[2 * nw:]
        x, y, c, chips = _place()
        sib = (x, y, 1 - c)
        copies = []
        for w in range(nw):
            for k, r in enumerate(rels):
                tx, ty = chips[r]
                ref = o_refs[w].at[2 * tx + ty, c]
                cp = pltpu.make_async_remote_copy(src_ref=ref, dst_ref=ref, send_sem=send.at[n * w + k],
                                                  recv_sem=recv.at[n * w + k], device_id=sib, device_id_type=MESH)
                cp.start()
                copies.append(cp)
        for w in range(nw):
            for k, r in enumerate(rels):
                tx, ty = chips[r]
                ref = o_refs[w].at[2 * tx + ty, 1 - c]
                pltpu.make_async_remote_copy(src_ref=ref, dst_ref=ref, send_sem=send.at[n * w + k],
                                             recv_sem=recv.at[n * w + k], device_id=sib, device_id_type=MESH).wait_recv()
        for cp in copies:
            cp.wait_send()

    return _pallas(
        body, name=name, in_specs=[ANY] * nw, out_specs=[ANY] * nw,
        out_shape=[jax.ShapeDtypeStruct(b.shape, b.dtype) for b in bufs],
        input_output_aliases={w: w for w in range(nw)},
        scratch_shapes=[pltpu.SemaphoreType.DMA((n * nw,)), pltpu.SemaphoreType.DMA((n * nw,))],
    )(*bufs)


def _share_small(v, reduce, name, after):
    R, C = v.shape

    def body(v_ref, after_ref, o_ref, *scratch):
        if reduce:
            all_ref, send, recv, lsem = scratch
        else:
            all_ref = o_ref
            send, recv, lsem = scratch
        x, y, c, _ = _place()
        me = 4 * x + 2 * y + c
        loc = pltpu.make_async_copy(v_ref, all_ref.at[me], lsem)
        loc.start()
        copies = []
        for k in range(1, N_DEV):
            kx, ky, kc = (k >> 2) & 1, (k >> 1) & 1, k & 1
            peer = (x ^ kx, y ^ ky, c ^ kc)
            cp = pltpu.make_async_remote_copy(src_ref=v_ref, dst_ref=all_ref.at[me], send_sem=send.at[k - 1],
                                              recv_sem=recv.at[k - 1], device_id=peer, device_id_type=MESH)
            cp.start()
            copies.append(cp)
        for k in range(1, N_DEV):
            kx, ky, kc = (k >> 2) & 1, (k >> 1) & 1, k & 1
            src = 4 * (x ^ kx) + 2 * (y ^ ky) + (c ^ kc)
            pltpu.make_async_remote_copy(src_ref=v_ref, dst_ref=all_ref.at[src], send_sem=send.at[k - 1],
                                         recv_sem=recv.at[k - 1], device_id=(x, y, c), device_id_type=MESH).wait_recv()
        for cp in copies:
            cp.wait_send()
        loc.wait()
        if reduce:
            total = all_ref[0]
            for d in range(1, N_DEV):
                total = total + all_ref[d]
            o_ref[...] = total

    vm = pl.BlockSpec(memory_space=pltpu.VMEM)
    sems = [pltpu.SemaphoreType.DMA((N_DEV - 1,)), pltpu.SemaphoreType.DMA((N_DEV - 1,)), pltpu.SemaphoreType.DMA]
    if reduce:
        out_shape = jax.ShapeDtypeStruct((R, C), F32)
        scratch = [pltpu.VMEM((N_DEV, R, C), F32)] + sems
    else:
        out_shape = jax.ShapeDtypeStruct((N_DEV, R, C), F32)
        scratch = sems
    return _pallas(
        body, name=name, in_specs=[vm, ANY], out_specs=vm, out_shape=out_shape, scratch_shapes=scratch,
        compiler_params=pltpu.CompilerParams(vmem_limit_bytes=int(min(4 * N_DEV * R * C * 4 + 2 ** 24, 2 ** 25 + 2 ** 24))),
    )(v, after)


def _pair_sum(place, g, rb, name):
    _, _, Rh, C = g.shape
    tr = _row_tile(Rh, 512, 16)

    def body(place_ref, g_ref, r_ref, q_ref):
        q_ref[...] = (g_ref[...] + r_ref[...]).astype(BF16)

    blk = 2 * _nbytes((tr, C), F32) + _nbytes((tr, C), BF16)
    return _pallas(
        body, name=name,
        grid_spec=pltpu.PrefetchScalarGridSpec(
            num_scalar_prefetch=1, grid=(N_CHIPS - 1, Rh // tr),
            in_specs=[pl.BlockSpec((None, None, tr, C), lambda j, i, p: (p[0] ^ (j + 1), p[1], i, 0)),
                      pl.BlockSpec((None, tr, C), lambda j, i, p: (p[0] ^ (j + 1), i, 0))],
            out_specs=pl.BlockSpec((None, tr, C), lambda j, i, p: (p[0] ^ (j + 1), i, 0))),
        out_shape=jax.ShapeDtypeStruct((N_CHIPS, Rh, C), BF16),
        compiler_params=_params(("parallel", "parallel"), blk),
    )(place, g, rb)


def _chip_sum(place, g, rb, rc, name):
    _, _, Rh, C = g.shape
    tr = _row_tile(Rh, 512, 16)

    def body(place_ref, g_ref, r_ref, rc_ref, o_ref):
        total = g_ref[...] + r_ref[...]
        for r in range(3):
            total = total + rc_ref[r].astype(F32)
        o_ref[...] = total

    blk = 3 * _nbytes((tr, C), F32) + 3 * _nbytes((tr, C), BF16)
    return _pallas(
        body, name=name,
        grid_spec=pltpu.PrefetchScalarGridSpec(
            num_scalar_prefetch=1, grid=(Rh // tr,),
            in_specs=[pl.BlockSpec((None, None, tr, C), lambda i, p: (p[0], p[1], i, 0)),
                      pl.BlockSpec((None, tr, C), lambda i, p: (p[0], i, 0)),
                      pl.BlockSpec((3, tr, C), lambda i, p: (0, i, 0))],
            out_specs=pl.BlockSpec((None, tr, C), lambda i, p: (p[1], i, 0))),
        out_shape=jax.ShapeDtypeStruct((2, Rh, C), F32),
        compiler_params=_params(("parallel",), blk),
    )(place, g, rb, rc)


def _adamw_math(w, g, m, v):
    m = ADAM_B1 * m + (1.0 - ADAM_B1) * g
    v = ADAM_B2 * v + (1.0 - ADAM_B2) * jnp.square(g)
    m_hat = m / (1.0 - ADAM_B1 ** ADAM_STEP)
    v_hat = v / (1.0 - ADAM_B2 ** ADAM_STEP)
    delta = -ADAM_LR * (m_hat / (jnp.sqrt(v_hat) + ADAM_EPS) + ADAM_WD * w)
    return delta, m, v


def _adamw(w, g, m, v, name):
    R, C = w.shape
    tr = _row_tile(R, 512)

    def body(w_ref, g_ref, m_ref, v_ref, go_ref, d_ref, nm_ref, nv_ref):
        gv = g_ref[...]
        d, nm, nv = _adamw_math(w_ref[...], gv, m_ref[...], v_ref[...])
        go_ref[...] = gv
        d_ref[...] = d
        nm_ref[...] = nm
        nv_ref[...] = nv

    spec = pl.BlockSpec((tr, C), lambda i: (i, 0))
    shp = jax.ShapeDtypeStruct((R, C), F32)
    return _pallas(
        body, name=name, grid=(R // tr,), in_specs=[spec] * 4, out_specs=[spec] * 4, out_shape=[shp] * 4,
        compiler_params=_params(("parallel",), 8 * _nbytes((tr, C), F32)),
    )(w, g, m, v)


def _adamw_small(ws, gs, ms, vs):
    n = len(ws)

    def body(*refs):
        for k in range(n):
            w_ref, g_ref, m_ref, v_ref = (refs[q * n + k] for q in range(4))
            d, nm, nv = _adamw_math(w_ref[...], g_ref[...], m_ref[...], v_ref[...])
            refs[4 * n + k][...] = d
            refs[5 * n + k][...] = nm
            refs[6 * n + k][...] = nv

    vm = pl.BlockSpec(memory_space=pltpu.VMEM)
    shapes = [jax.ShapeDtypeStruct(w.shape, F32) for w in ws]
    outs = _pallas(
        body, name="adamw_small", in_specs=[vm] * (4 * n), out_specs=[vm] * (3 * n), out_shape=shapes * 3,
    )(*ws, *gs, *ms, *vs)
    return outs[:n], outs[n:2 * n], outs[2 * n:]


def _pad_rows(a, rows):
    return jnp.pad(a, ((0, rows - a.shape[0]), (0, 0)))


def kernel(x, meta_tokens, ffn1_norm, ffn1_w_gate, ffn1_w_up, ffn1_w_down, mix_norm, w_in, b_in, conv_sc_w, conv_cf_w, conv_cf_b, ln_cf_g, ln_cf_b, w_out, ffn2_norm, ffn2_w_gate, ffn2_w_up, ffn2_w_down, final_norm, loss_target, m_meta_tokens, m_ffn1_norm, m_ffn1_w_gate, m_ffn1_w_up, m_ffn1_w_down, m_mix_norm, m_w_in, m_b_in, m_conv_sc_w, m_conv_cf_w, m_conv_cf_b, m_ln_cf_g, m_ln_cf_b, m_w_out, m_ffn2_norm, m_ffn2_w_gate, m_ffn2_w_up, m_ffn2_w_down, m_final_norm, v_meta_tokens, v_ffn1_norm, v_ffn1_w_gate, v_ffn1_w_up, v_ffn1_w_down, v_mix_norm, v_w_in, v_b_in, v_conv_sc_w, v_conv_cf_w, v_conv_cf_b, v_ln_cf_g, v_ln_cf_b, v_w_out, v_ffn2_norm, v_ffn2_w_gate, v_ffn2_w_up, v_ffn2_w_down, v_final_norm):
    xi, yi, ci = lax.axis_index("x"), lax.axis_index("y"), lax.axis_index("c")
    chip = 2 * xi + yi
    place = jnp.stack([chip, ci]).astype(jnp.int32)

    x2 = x[0]
    tgt = loss_target[0]
    S, D = x2.shape
    C1 = D // 2
    cs = conv_sc_w.shape[2]
    ksc, kcf = conv_sc_w.shape[1], conv_cf_w.shape[1]
    ms = meta_tokens.shape[1]

    big = {"ffn1_w_gate": ffn1_w_gate, "ffn1_w_up": ffn1_w_up, "ffn1_w_down": ffn1_w_down, "w_in": w_in, "w_out": w_out,
           "ffn2_w_gate": ffn2_w_gate, "ffn2_w_up": ffn2_w_up, "ffn2_w_down": ffn2_w_down}
    big_m = {"ffn1_w_gate": m_ffn1_w_gate, "ffn1_w_up": m_ffn1_w_up, "ffn1_w_down": m_ffn1_w_down, "w_in": m_w_in,
             "w_out": m_w_out, "ffn2_w_gate": m_ffn2_w_gate, "ffn2_w_up": m_ffn2_w_up, "ffn2_w_down": m_ffn2_w_down}
    big_v = {"ffn1_w_gate": v_ffn1_w_gate, "ffn1_w_up": v_ffn1_w_up, "ffn1_w_down": v_ffn1_w_down, "w_in": v_w_in,
             "w_out": v_w_out, "ffn2_w_gate": v_ffn2_w_gate, "ffn2_w_up": v_ffn2_w_up, "ffn2_w_down": v_ffn2_w_down}
    buf = {nm: _cast_own_block(place, w[0], "cast_" + nm) for nm, w in big.items()}
    whole_weight = lambda g: g.reshape(N_CHIPS, 2 * g.shape[2], g.shape[3])
    corner = lambda a: a.reshape(-1, a.shape[-1])[:8, :128]

    NEAR, FAR = (0, 1), (2,)
    groups = {"ffn1_near": (["ffn1_w_gate", "ffn1_w_up", "ffn1_w_down"], NEAR),
              "ffn1_far": (["ffn1_w_gate", "ffn1_w_up", "ffn1_w_down"], FAR),
              "mix": (["w_in", "w_out"], NEAR + FAR),
              "ffn2_up": (["ffn2_w_gate", "ffn2_w_up"], NEAR + FAR),
              "ffn2_down": (["ffn2_w_down"], NEAR + FAR)}
    started = {}

    def start(tag, after):
        nms, rels = groups[tag]
        copies = functools.partial(_gather_copies, rels=rels)
        send, recv, thru, token = _start_copies([buf[nm] for nm in nms], after, len(rels) * len(nms), copies,
                                                "gather_start_" + tag)
        for nm, b in zip(nms, thru):
            buf[nm] = b
        started[tag] = (send, recv, copies)
        return token

    def arrive(tag, after, then=None):
        nms, rels = groups[tag]
        send, recv, copies = started[tag]
        got = _wait_copies(send, recv, [buf[nm] for nm in nms], corner(after), copies, "gather_wait_" + tag)
        for nm, b in zip(nms, got):
            buf[nm] = b
        if then is not None:
            start(then, corner(got[0]))
        for nm, b in zip(nms, _forward_halves([buf[nm] for nm in nms], "gather_forward_" + tag, rels)):
            buf[nm] = b

    def passing(tag, after):
        nms, rels = groups[tag]
        send, recv, copies = started[tag]
        got = _wait_copies(send, recv, [buf[nm] for nm in nms], corner(after), copies, "gather_wait_" + tag)
        copies = functools.partial(_forward_copies, rels=rels)
        send, recv, thru, token = _start_copies(got, corner(got[0]), len(rels) * len(nms), copies,
                                                "gather_pass_" + tag)
        for nm, b in zip(nms, thru):
            buf[nm] = b
        started[tag] = (send, recv, copies)
        return token

    def passed(tag, after):
        nms, _ = groups[tag]
        send, recv, copies = started[tag]
        for nm, b in zip(nms, _wait_copies(send, recv, [buf[nm] for nm in nms], corner(after), copies,
                                           "gather_passed_" + tag)):
            buf[nm] = b

    tokens = lambda *arrays: jnp.concatenate([corner(a).astype(F32) for a in arrays], axis=0)
    assert ksc <= 8 and kcf <= 32 and cs <= ms
    pack = jnp.concatenate([
        meta_tokens,
        jnp.pad(conv_sc_w[0], ((0, 8 - ksc), (0, ms - cs))),
        jnp.pad(conv_cf_w[0], ((0, 32 - kcf), (0, ms - cs)))], axis=0)
    everyone = _share_small(pack, False, "share_params", pack)[0::2]
    meta_full = jnp.transpose(everyone[:, :N_META, :], (1, 0, 2)).reshape(N_META, D)
    wsc_full = jnp.transpose(everyone[:, N_META:N_META + ksc, :cs], (1, 0, 2)).reshape(ksc, C1)
    wcf_full = jnp.transpose(everyone[:, N_META + 8:N_META + 8 + kcf, :cs], (1, 0, 2)).reshape(kcf, C1)

    token = start("ffn1_near", corner(everyone))

    ffn1 = lambda: [whole_weight(buf[nm]) for nm in ["ffn1_w_gate", "ffn1_w_up", "ffn1_w_down"]]
    own = chip[None].astype(jnp.int32)
    near = jnp.stack([chip ^ 2, chip ^ 1]).astype(jnp.int32)
    far = (chip ^ 3)[None].astype(jnp.int32)
    all_chips = jnp.arange(N_CHIPS, dtype=jnp.int32)

    hs0, n1 = _embed_rms(x2, meta_full, ffn1_norm)
    wg1, wu1, wd1 = ffn1()
    gua = _ffn_up(n1, wg1, wu1, own, None, token, "ffn1_up_own")
    hs1 = _ffn_down(gua[2], wd1, hs0, own, "ffn1_down_own")
    later = [buf[nm] for nm in ["w_in", "w_out", "ffn2_w_gate", "ffn2_w_up", "ffn2_w_down"]]
    arrive("ffn1_near", tokens(hs1, *later), "ffn1_far")
    wg1, wu1, wd1 = ffn1()
    gua = _ffn_up(n1, wg1, wu1, near, gua, token, "ffn1_up_near")
    tok = start("mix", corner(gua[2]))
    tok = passing("ffn1_far", tok)
    hs1 = _ffn_down(gua[2], ffn1()[2], hs1, near, "ffn1_down_near", tok)
    passed("ffn1_far", hs1)
    wg1, wu1, wd1 = ffn1()
    g1, u1, a1 = _ffn_up(n1, wg1, wu1, far, gua, token, "ffn1_up_far")
    tok = passing("mix", a1)
    hs1, n2 = _ffn_down(a1, wd1, hs1, far, "ffn1_down_far", tok, mix_norm)
    F = N_CHIPS * wd1.shape[1]
    tok = start("ffn2_up", corner(hs1))
    passed("mix", tok)
    win, wout = whole_weight(buf["w_in"]), whole_weight(buf["w_out"])
    u = _mix_in(n2, win, b_in)
    y, z1 = _mix_conv_fwd(u, wsc_full, wcf_full, conv_cf_b, ln_cf_g, ln_cf_b)
    tok = start("ffn2_down", corner(y))
    tok = passing("ffn2_up", tok)
    hs2, n3 = _mix_out(y, wout.reshape(D, D), hs1, ffn2_norm, tok)
    passed("ffn2_up", hs2)
    wg2, wu2 = whole_weight(buf["ffn2_w_gate"]), whole_weight(buf["ffn2_w_up"])
    g2, u2, a2 = _ffn_up(n3, wg2, wu2, all_chips, None, token, "ffn2_up")
    passed("ffn2_down", passing("ffn2_down", a2))
    wd2 = whole_weight(buf["ffn2_w_down"])
    hs3 = _ffn_down_whole(a2, wd2.reshape(F, D), hs2, "ffn2_down")
    token_ffn2 = token

    def pair_start(group, after, tag):
        gs = [g for _, g in group]
        lands = [lax.empty((N_CHIPS,) + g.shape[2:], F32) for g in gs]
        send, recv, thru, token = _start_copies(gs + lands, after, N_CHIPS * len(gs), _pair_copies,
                                                "pair_start_" + tag)
        return (group, send, recv, thru, tag), token

    def scatter_start(state, after):
        group, send, recv, thru, tag = state
        thru = _wait_copies(send, recv, thru, corner(after), _pair_copies, "pair_wait_" + tag)
        gs, sib = thru[:len(group)], thru[len(group):]
        sums = [_pair_sum(place, g, rb, "pair_sum_" + nm) for (nm, _), g, rb in zip(group, gs, sib)]
        lands = [lax.empty((3,) + q.shape[1:], BF16) for q in sums]
        send, recv, thru, token = _start_copies(sums + lands, corner(sums[-1]), 3 * len(gs), _scatter_copies,
                                                "scatter_start_" + tag)
        return ([(nm, g) for (nm, _), g in zip(group, gs)], sib, send, recv, thru, tag), token

    def finish_sum(state, after):
        group, sib, send, recv, thru, tag = state
        lands = _wait_copies(send, recv, thru, corner(after), _scatter_copies, "scatter_wait_" + tag)[len(group):]
        mine = [_chip_sum(place, g, rb, rc, "chip_sum_" + nm) for (nm, g), rb, rc in zip(group, sib, lands)]
        send, recv, thru, token = _start_copies(mine, corner(mine[-1]), len(mine), _half_copies, "half_start_" + tag)
        return (group, send, recv, thru, tag), token

    def finish_adam(state, after):
        group, send, recv, thru, tag = state
        whole = _wait_copies(send, recv, thru, corner(after), _half_copies, "half_wait_" + tag)
        out = {}
        for (nm, _), g in zip(group, whole):
            w = big[nm]
            g_out, d, new_m, new_v = _adamw(w[0], g.reshape(w.shape[1:]), big_m[nm][0], big_v[nm][0], "adamw_" + nm)
            out[nm] = (g_out[None], d[None], new_m[None], new_v[None])
        return out

    dhs3, df2, loss_row, d_final = _final_loss(hs3, final_norm.reshape(1, D), tgt)

    dg2, du2 = _ffn_bwd_act(df2, wd2.reshape(F, D), g2, u2, token_ffn2, "ffn2_bwd_act")
    gw_d2 = _wgrad_down(a2, df2, "wgrad_ffn2_down")
    gw_g2 = _wgrad_cols(n3, [dg2], "wgrad_ffn2_gate")[0]
    gw_u2 = _wgrad_cols(n3, [du2], "wgrad_ffn2_up")[0]
    pair_ffn2, token = pair_start([("ffn2_w_gate", gw_g2), ("ffn2_w_up", gw_u2), ("ffn2_w_down", gw_d2)],
                                  corner(gw_u2), "ffn2")
    dn3 = _nt_panel([dg2, du2], [wg2, wu2], token, "ffn2_bwd_in")
    red_ffn2, token = scatter_start(pair_ffn2, dn3)
    dhs2, dm, d_ffn2 = _rms_bwd(dn3, hs2, ffn2_norm, dhs3, 1.0, "rms_bwd_ffn2")

    dy = _nt_panel([dm], [wout.reshape(1, D, D)], token, "mix_bwd_out")
    gw_out = _wgrad_out(y, dm)
    dz1, dcs, db, d_lg, d_lb, d_bcf = _mix_conv_bwd1(u, z1, dy, wsc_full, ln_cf_g, ln_cf_b)
    du, d_bin, d_wsc, d_wcf = _mix_conv_bwd2(u, dz1, dcs, db, wsc_full, wcf_full)
    gw_in = _wgrad_cols(n2, [du], "wgrad_w_in")[0]
    pair_mix, token = pair_start([("w_in", gw_in), ("w_out", gw_out)], corner(gw_in), "mix")
    dn2 = _nt_panel([du], [win], token, "mix_bwd_in")
    red_mix, token = scatter_start(pair_mix, dn2)
    dhs1, df1, d_mix = _rms_bwd(dn2, hs1, mix_norm, dhs2, FFN_RES_SCALE, "rms_bwd_mix")

    dg1, du1 = _ffn_bwd_act(df1, wd1.reshape(F, D), g1, u1, token, "ffn1_bwd_act")
    gw_d1 = _wgrad_down(a1, df1, "wgrad_ffn1_down")
    gw_g1 = _wgrad_cols(n1, [dg1], "wgrad_ffn1_gate")[0]
    pair_ffn1a, token = pair_start([("ffn1_w_down", gw_d1), ("ffn1_w_gate", gw_g1)], corner(gw_g1), "ffn1a")
    gw_u1 = _wgrad_cols(n1, [du1], "wgrad_ffn1_up", token)[0]
    red_ffn1a, token = scatter_start(pair_ffn1a, gw_u1)
    pair_ffn1b, token = pair_start([("ffn1_w_up", gw_u1)], token, "ffn1b")
    dn1 = _nt_panel([dg1, du1], [wg1, wu1], token, "ffn1_bwd_in")
    red_ffn1b, token = scatter_start(pair_ffn1b, dn1)
    grad_x, d_meta, d_ffn1 = _rms_bwd_first(dn1, hs0, ffn1_norm, dhs1, token)

    half_ffn2, tok = finish_sum(red_ffn2, grad_x)
    half_mix, tok = finish_sum(red_mix, tok)
    big_out = finish_adam(half_ffn2, tok)
    half_ffn1a, tok = finish_sum(red_ffn1a, big_out["ffn2_w_down"][1])
    big_out.update(finish_adam(half_mix, tok))
    half_ffn1b, tok = finish_sum(red_ffn1b, big_out["w_out"][1])
    big_out.update(finish_adam(half_ffn1a, tok))
    big_out.update(finish_adam(half_ffn1b, big_out["ffn1_w_gate"][1]))

    W = C1
    rows = lambda a: a.reshape(-1, W)
    parts = [rows(d_ffn1), rows(d_mix), rows(d_ffn2), rows(d_final), rows(d_bin), d_bcf, d_lg, d_lb,
             d_wsc, d_wcf, rows(d_meta), jnp.broadcast_to(loss_row[:, :1], (1, W))]
    sizes = [p.shape[0] for p in parts]
    total_rows = sum(sizes)
    packed = _pad_rows(jnp.concatenate(parts, axis=0), -(-total_rows // 8) * 8)
    summed = _share_small(packed, True, "sum_small", big_out["ffn1_w_up"][1])
    offs = [0]
    for n in sizes:
        offs.append(offs[-1] + n)
    piece = lambda k: summed[offs[k]:offs[k + 1]]
    loss = piece(11)[0, 0]
    g_ffn1, g_mix, g_ffn2 = (piece(k).reshape(1, D) for k in range(3))
    g_final = piece(3).reshape(1, D)
    g_bin = piece(4).reshape(1, -1)
    g_bcf, g_lg, g_lb = piece(5), piece(6), piece(7)
    g_wsc = lax.dynamic_slice_in_dim(piece(8), chip * cs, cs, axis=1)
    g_wcf = lax.dynamic_slice_in_dim(piece(9), chip * cs, cs, axis=1)
    g_meta = lax.dynamic_slice_in_dim(piece(10).reshape(N_META, D), chip * ms, ms, axis=1)

    small_names = ["meta_tokens", "ffn1_norm", "mix_norm", "b_in", "conv_sc_w", "conv_cf_w", "conv_cf_b", "ln_cf_g",
                   "ln_cf_b", "ffn2_norm", "final_norm"]
    small_w = [meta_tokens, ffn1_norm, mix_norm, b_in, conv_sc_w[0], conv_cf_w[0], conv_cf_b, ln_cf_g, ln_cf_b,
               ffn2_norm, final_norm.reshape(1, D)]
    small_g = [g_meta, g_ffn1, g_mix, g_bin, g_wsc, g_wcf, g_bcf, g_lg, g_lb, g_ffn2, g_final]
    small_m = [m_meta_tokens, m_ffn1_norm, m_mix_norm, m_b_in, m_conv_sc_w[0], m_conv_cf_w[0], m_conv_cf_b, m_ln_cf_g,
               m_ln_cf_b, m_ffn2_norm, m_final_norm.reshape(1, D)]
    small_v = [v_meta_tokens, v_ffn1_norm, v_mix_norm, v_b_in, v_conv_sc_w[0], v_conv_cf_w[0], v_conv_cf_b, v_ln_cf_g,
               v_ln_cf_b, v_ffn2_norm, v_final_norm.reshape(1, D)]
    s_d, s_m, s_v = _adamw_small(small_w, small_g, small_m, small_v)
    shapes = {"conv_sc_w": conv_sc_w.shape, "conv_cf_w": conv_cf_w.shape, "final_norm": final_norm.shape}
    small_out = {}
    for nm, g, d, m, v in zip(small_names, small_g, s_d, s_m, s_v):
        shp = shapes.get(nm, g.shape)
        small_out[nm] = tuple(t.reshape(shp) for t in (g, d, m, v))

    order = ["meta_tokens", "ffn1_norm", "ffn1_w_gate", "ffn1_w_up", "ffn1_w_down", "mix_norm", "w_in", "b_in",
             "conv_sc_w", "conv_cf_w", "conv_cf_b", "ln_cf_g", "ln_cf_b", "w_out", "ffn2_norm", "ffn2_w_gate",
             "ffn2_w_up", "ffn2_w_down", "final_norm"]
    res = {**big_out, **small_out}
    outs = [loss, grad_x[None]]
    for q in range(4):
        outs.extend(res[nm][q] for nm in order)
    return tuple(outs)
```

```python
import functools

import jax
import jax.numpy as jnp
from jax import lax
from jax.experimental import pallas as pl
from jax.experimental.pallas import tpu as pltpu

F32 = jnp.float32
BF16 = jnp.bfloat16
MESH = pl.DeviceIdType.MESH

N_META = 16
TT = 128
PAD = TT - N_META
HALO = 32
EPS = 1e-6
FFN_RES_SCALE = 0.5
N_CHIPS = 4
N_DEV = 8

ADAM_LR = 0.001
ADAM_B1 = 0.9
ADAM_B2 = 0.999
ADAM_EPS = 1e-08
ADAM_WD = 0.01
ADAM_STEP = 10

V7X_VMEM_BYTES = 64 * 2 ** 20
NT_DIMS = (((1,), (1,)), ((), ()))
TN_DIMS = (((0,), (0,)), ((), ()))


def _params(semantics, block_bytes):
    limit = min(2 * block_bytes + 16 * 2 ** 20, V7X_VMEM_BYTES - 6 * 2 ** 20)
    return pltpu.CompilerParams(dimension_semantics=semantics, vmem_limit_bytes=int(limit))


def _pallas(body, out_shape, **kw):
    if "grid" not in kw and "grid_spec" not in kw:
        return pl.pallas_call(body, out_shape=out_shape, **kw)
    big = lambda shape, dtype: jnp.issubdtype(dtype, jnp.floating) and len(shape) >= 2
    pin_out = lambda s: pltpu.HBM(s.shape, s.dtype) if big(s.shape, s.dtype) else s
    single = not isinstance(out_shape, (list, tuple))
    shapes = pin_out(out_shape) if single else [pin_out(s) for s in out_shape]
    call = pl.pallas_call(body, out_shape=shapes, **kw)
    pin = lambda a: pltpu.with_memory_space_constraint(a, pltpu.HBM) if big(a.shape, a.dtype) else a
    return lambda *operands: call(*[pin(a) for a in operands])


def _nbytes(shape, dtype):
    n = 1
    for d in shape:
        if d is not None:
            n *= d
    return n * jnp.dtype(dtype).itemsize


def _row_tile(rows, target, mult=8):
    best = None
    for t in range(mult, min(rows, target) + 1, mult):
        if rows % t == 0:
            best = t
    assert best is not None, (rows, target, mult)
    return best


def _sigmoid(v):
    return jax.nn.sigmoid(v)


def _dsilu(v, s):
    return s * (1.0 + v * (1.0 - s))


def _embed_rms(x2, meta, gain):
    S, D = x2.shape
    T = S + TT

    def body(x_ref, meta_ref, g_ref, hs_ref, n_ref):
        i = pl.program_id(0)

        @pl.when(i == 0)
        def _():
            hs_ref[...] = jnp.zeros_like(hs_ref)
            hs_ref[PAD:, :] = meta_ref[...]

        @pl.when(i > 0)
        def _():
            hs_ref[...] = x_ref[...]

        h = hs_ref[...]
        r = lax.rsqrt(jnp.mean(h * h, axis=-1, keepdims=True) + EPS)
        n_ref[...] = ((h * r) * g_ref[...]).astype(BF16)

    blk = _nbytes((TT, D), F32) * 2 + _nbytes((TT, D), BF16)
    return _pallas(
        body, name="embed_rms", grid=(T // TT,),
        in_specs=[pl.BlockSpec((TT, D), lambda i: (jnp.maximum(i - 1, 0), 0)),
                  pl.BlockSpec((N_META, D), lambda i: (0, 0)),
                  pl.BlockSpec((1, D), lambda i: (0, 0))],
        out_specs=[pl.BlockSpec((TT, D), lambda i: (i, 0)), pl.BlockSpec((TT, D), lambda i: (i, 0))],
        out_shape=[jax.ShapeDtypeStruct((T, D), F32), jax.ShapeDtypeStruct((T, D), BF16)],
        compiler_params=_params(("parallel",), blk),
    )(x2, meta, gain)


def _rms_bwd_math(dn, h, g):
    r = lax.rsqrt(jnp.mean(h * h, axis=-1, keepdims=True) + EPS)
    xh = h * r
    dgain = jnp.sum(dn * xh, axis=0, keepdims=True)
    dxh = dn * g
    dh = r * (dxh - xh * jnp.mean(dxh * xh, axis=-1, keepdims=True))
    return dh, dgain


def _rms_bwd(dn, hs, gain, dres, scale, name):
    T, D = hs.shape
    te = _row_tile(T, 384)

    def body(dn_ref, h_ref, g_ref, dres_ref, dhs_ref, dhb_ref, dg_ref):
        dh, dgain = _rms_bwd_math(dn_ref[...], h_ref[...], g_ref[...])
        d = dres_ref[...] + dh
        dhs_ref[...] = d
        dhb_ref[...] = (scale * d).astype(BF16)

        @pl.when(pl.program_id(0) == 0)
        def _():
            dg_ref[...] = jnp.zeros_like(dg_ref)

        dg_ref[...] += dgain

    blk = _nbytes((te, D), F32) * 4 + _nbytes((te, D), BF16)
    row = lambda i: (i, 0)
    return _pallas(
        body, name=name, grid=(T // te,),
        in_specs=[pl.BlockSpec((te, D), row), pl.BlockSpec((te, D), row), pl.BlockSpec((1, D), lambda i: (0, 0)),
                  pl.BlockSpec((te, D), row)],
        out_specs=[pl.BlockSpec((te, D), row), pl.BlockSpec((te, D), row), pl.BlockSpec((1, D), lambda i: (0, 0))],
        out_shape=[jax.ShapeDtypeStruct((T, D), F32), jax.ShapeDtypeStruct((T, D), BF16),
                   jax.ShapeDtypeStruct((1, D), F32)],
        compiler_params=_params(("arbitrary",), blk),
    )(dn, hs, gain, dres)


def _rms_bwd_first(dn, hs, gain, dres, after):
    T, D = hs.shape
    S = T - TT

    def body(dn_ref, h_ref, g_ref, dres_ref, after_ref, gx_ref, gm_ref, dg_ref):
        i = pl.program_id(0)
        dh, dgain = _rms_bwd_math(dn_ref[...], h_ref[...], g_ref[...])
        d = dres_ref[...] + dh

        @pl.when(i == 0)
        def _():
            dg_ref[...] = jnp.zeros_like(dg_ref)
            gm_ref[...] = d[PAD:, :]

        @pl.when(i > 0)
        def _():
            gx_ref[...] = d

        dg_ref[...] += dgain

    blk = _nbytes((TT, D), F32) * 4
    row = lambda i: (i, 0)
    return _pallas(
        body, name="rms_bwd_ffn1", grid=(T // TT,),
        in_specs=[pl.BlockSpec((TT, D), row), pl.BlockSpec((TT, D), row), pl.BlockSpec((1, D), lambda i: (0, 0)),
                  pl.BlockSpec((TT, D), row), TOKEN],
        out_specs=[pl.BlockSpec((TT, D), lambda i: (jnp.maximum(i - 1, 0), 0)),
                   pl.BlockSpec((N_META, D), lambda i: (0, 0)), pl.BlockSpec((1, D), lambda i: (0, 0))],
        out_shape=[jax.ShapeDtypeStruct((S, D), F32), jax.ShapeDtypeStruct((N_META, D), F32),
                   jax.ShapeDtypeStruct((1, D), F32)],
        compiler_params=_params(("arbitrary",), blk),
    )(dn, hs, gain, dres, after)


def _final_loss(hs, gain, tgt):
    T, D = hs.shape

    def body(h_ref, g_ref, t_ref, dhs_ref, dhb_ref, loss_ref, dg_ref):
        i = pl.program_id(0)
        h = h_ref[...]
        g = g_ref[...]
        r = lax.rsqrt(jnp.mean(h * h, axis=-1, keepdims=True) + EPS)
        xh = h * r
        e = jnp.where(i > 0, xh * g - t_ref[...], 0.0)
        tile_loss = jnp.sum(jnp.sum(e * e, axis=1, keepdims=True), axis=0, keepdims=True) * (0.5 / D)
        dout = e * (1.0 / D)
        dgain = jnp.sum(dout * xh, axis=0, keepdims=True)
        dxh = dout * g
        d = r * (dxh - xh * jnp.mean(dxh * xh, axis=-1, keepdims=True))
        dhs_ref[...] = d
        dhb_ref[...] = (FFN_RES_SCALE * d).astype(BF16)

        @pl.when(i == 0)
        def _():
            loss_ref[...] = jnp.zeros_like(loss_ref)
            dg_ref[...] = jnp.zeros_like(dg_ref)

        loss_ref[...] += jnp.broadcast_to(tile_loss, loss_ref.shape)
        dg_ref[...] += dgain

    blk = _nbytes((TT, D), F32) * 3 + _nbytes((TT, D), BF16)
    row = lambda i: (i, 0)
    return _pallas(
        body, name="final_loss", grid=(T // TT,),
        in_specs=[pl.BlockSpec((TT, D), row), pl.BlockSpec((1, D), lambda i: (0, 0)),
                  pl.BlockSpec((TT, D), lambda i: (jnp.maximum(i - 1, 0), 0))],
        out_specs=[pl.BlockSpec((TT, D), row), pl.BlockSpec((TT, D), row),
                   pl.BlockSpec((1, 128), lambda i: (0, 0)), pl.BlockSpec((1, D), lambda i: (0, 0))],
        out_shape=[jax.ShapeDtypeStruct((T, D), F32), jax.ShapeDtypeStruct((T, D), BF16),
                   jax.ShapeDtypeStruct((1, 128), F32), jax.ShapeDtypeStruct((1, D), F32)],
        compiler_params=_params(("arbitrary",), blk),
    )(hs, gain, tgt)


MXU_COLS = 256


def _tm(T):
    return _row_tile(T, 704, 16)


def _col_chunks(n):
    return [(c, min(MXU_COLS, n - c)) for c in range(0, n, MXU_COLS)]


TOKEN = pl.BlockSpec((8, 128), lambda *_: (0, 0))


def _ffn_up(n, wg, wu, shards, prev, after, name):
    T, D = n.shape
    Fs = wg.shape[2]
    tm = _tm(T)
    nprev = 0 if prev is None else 3

    def body(shards_ref, n_ref, wg_ref, wu_ref, after_ref, *refs):
        g_ref, u_ref, a_ref = refs[nprev:]
        nn = n_ref[...]
        for c0, cw in _col_chunks(Fs):
            if 2 * cw == MXU_COLS:
                both = jnp.concatenate([wg_ref[:, c0:c0 + cw], wu_ref[:, c0:c0 + cw]], axis=1)
                gu = jnp.dot(nn, both, preferred_element_type=F32)
                g, u = gu[:, :cw], gu[:, cw:]
            else:
                g = jnp.dot(nn, wg_ref[:, c0:c0 + cw], preferred_element_type=F32)
                u = jnp.dot(nn, wu_ref[:, c0:c0 + cw], preferred_element_type=F32)
            g_ref[:, c0:c0 + cw] = g.astype(BF16)
            u_ref[:, c0:c0 + cw] = u.astype(BF16)
            a_ref[:, c0:c0 + cw] = (jax.nn.silu(g) * u).astype(BF16)

    blk = _nbytes((tm, D), BF16) + 2 * _nbytes((D, Fs), BF16) + 3 * _nbytes((tm, Fs), BF16)
    out = pl.BlockSpec((tm, Fs), lambda j, i, p: (i, p[j]))
    shp = jax.ShapeDtypeStruct((T, N_CHIPS * Fs), BF16)
    return _pallas(
        body, name=name,
        grid_spec=pltpu.PrefetchScalarGridSpec(
            num_scalar_prefetch=1, grid=(shards.shape[0], T // tm),
            in_specs=[pl.BlockSpec((tm, D), lambda j, i, p: (i, 0)),
                      pl.BlockSpec((None, D, Fs), lambda j, i, p: (p[j], 0, 0)),
                      pl.BlockSpec((None, D, Fs), lambda j, i, p: (p[j], 0, 0)), TOKEN] + [ANY] * nprev,
            out_specs=[out, out, out]),
        out_shape=[shp, shp, shp], input_output_aliases={5 + q: q for q in range(nprev)},
        compiler_params=_params(("arbitrary", "arbitrary"), blk),
    )(shards, n, wg, wu, after, *(prev or ()))


def _ffn_down(a, wd, hs, shards, name, after=None, gain=None):
    T, F = a.shape
    _, Fs, D = wd.shape
    tm = _tm(T) if gain is None else _row_tile(T, 352, 16)
    tn = D // 2 if gain is None else D
    extra = [] if after is None else [after]
    gains = [] if gain is None else [gain]
    nk = shards.shape[0]
    nrest = len(extra) + len(gains)

    def body(shards_ref, a_ref, w_ref, h_ref, *refs):
        o_ref = refs[nrest]
        k = pl.program_id(2)
        part = FFN_RES_SCALE * jnp.dot(a_ref[...], w_ref[...], preferred_element_type=F32)

        @pl.when(k == 0)
        def _():
            o_ref[...] = h_ref[...] + part

        @pl.when(k > 0)
        def _():
            o_ref[...] += part

        if gains:
            @pl.when(k == nk - 1)
            def _():
                o = o_ref[...]
                r = lax.rsqrt(jnp.mean(o * o, axis=-1, keepdims=True) + EPS)
                refs[nrest + 1][...] = ((o * r) * refs[nrest - 1][...]).astype(BF16)

    blk = _nbytes((tm, Fs), BF16) + _nbytes((Fs, tn), BF16) + 3 * _nbytes((tm, tn), F32) + _nbytes((tm, tn), BF16)
    tile = pl.BlockSpec((tm, tn), lambda n, i, k, p: (i, n))
    res = _pallas(
        body, name=name,
        grid_spec=pltpu.PrefetchScalarGridSpec(
            num_scalar_prefetch=1, grid=(D // tn, T // tm, nk),
            in_specs=[pl.BlockSpec((tm, Fs), lambda n, i, k, p: (i, p[k])),
                      pl.BlockSpec((None, Fs, tn), lambda n, i, k, p: (p[k], 0, n)), tile]
                     + [TOKEN] * len(extra) + [pl.BlockSpec((1, D), lambda n, i, k, p: (0, 0))] * len(gains),
            out_specs=[tile] * (1 + len(gains))),
        out_shape=[jax.ShapeDtypeStruct((T, D), F32)] + [jax.ShapeDtypeStruct((T, D), BF16)] * len(gains),
        compiler_params=_params(("parallel", "parallel", "arbitrary"), blk),
    )(shards, a, wd, hs, *extra, *gains)
    return res[0] if gain is None else tuple(res)


def _ffn_down_whole(a, wd, hs, name):
    T, F = a.shape
    D = wd.shape[1]
    tm = _tm(T)
    tn = D // 4

    def body(a_ref, w_ref, h_ref, o_ref):
        o_ref[...] = h_ref[...] + FFN_RES_SCALE * jnp.dot(a_ref[...], w_ref[...], preferred_element_type=F32)

    blk = _nbytes((tm, F), BF16) + _nbytes((F, tn), BF16) + 3 * _nbytes((tm, tn), F32)
    return _pallas(
        body, name=name, grid=(D // tn, T // tm),
        in_specs=[pl.BlockSpec((tm, F), lambda n, i: (i, 0)), pl.BlockSpec((F, tn), lambda n, i: (0, n)),
                  pl.BlockSpec((tm, tn), lambda n, i: (i, n))],
        out_specs=pl.BlockSpec((tm, tn), lambda n, i: (i, n)),
        out_shape=jax.ShapeDtypeStruct((T, D), F32),
        compiler_params=_params(("parallel", "parallel"), blk),
    )(a, wd, hs)


def _mix_in(n, w, b):
    T, D = n.shape
    Ns = w.shape[2]
    tm = _tm(T)

    def body(n_ref, w_ref, b_ref, u_ref):
        u_ref[...] = jnp.dot(n_ref[...], w_ref[...], preferred_element_type=F32) + b_ref[...]

    blk = _nbytes((tm, D), BF16) + _nbytes((D, Ns), BF16) + 2 * _nbytes((tm, Ns), F32)
    return _pallas(
        body, name="mix_in", grid=(N_CHIPS, T // tm),
        in_specs=[pl.BlockSpec((tm, D), lambda j, i: (i, 0)), pl.BlockSpec((None, D, Ns), lambda j, i: (j, 0, 0)),
                  pl.BlockSpec((1, Ns), lambda j, i: (0, j))],
        out_specs=pl.BlockSpec((tm, Ns), lambda j, i: (i, j)),
        out_shape=jax.ShapeDtypeStruct((T, N_CHIPS * Ns), F32),
        compiler_params=_params(("parallel", "parallel"), blk),
    )(n, w, b)


def _mix_out(y, w, hs, gain, after):
    T, D = y.shape
    tm = _row_tile(T, 352, 16)

    def body(y_ref, w_ref, h_ref, g_ref, after_ref, o_ref, n_ref):
        o = h_ref[...] + jnp.dot(y_ref[...], w_ref[...], preferred_element_type=F32)
        o_ref[...] = o
        r = lax.rsqrt(jnp.mean(o * o, axis=-1, keepdims=True) + EPS)
        n_ref[...] = ((o * r) * g_ref[...]).astype(BF16)

    blk = 2 * _nbytes((tm, D), BF16) + _nbytes((D, D), BF16) + 3 * _nbytes((tm, D), F32)
    row = pl.BlockSpec((tm, D), lambda i: (i, 0))
    return _pallas(
        body, name="mix_out", grid=(T // tm,),
        in_specs=[row, pl.BlockSpec((D, D), lambda i: (0, 0)), row, pl.BlockSpec((1, D), lambda i: (0, 0)), TOKEN],
        out_specs=[row, row],
        out_shape=[jax.ShapeDtypeStruct((T, D), F32), jax.ShapeDtypeStruct((T, D), BF16)],
        compiler_params=_params(("parallel",), blk),
    )(y, w, hs, gain, after)


def _ffn_bwd_act(dfb, wd, g, u, after, name):
    T, D = dfb.shape
    F = wd.shape[0]
    tm = _row_tile(T, 1408, 16)
    tn = 2 * MXU_COLS

    tr = _row_tile(tm, 352, 16)

    def body(d_ref, w_ref, g_ref, u_ref, after_ref, dg_ref, du_ref):
        for r0 in range(0, tm, tr):
            dv = d_ref[r0:r0 + tr, :]
            for c0, cw in _col_chunks(tn):
                da = lax.dot_general(dv, w_ref[c0:c0 + cw, :], NT_DIMS, preferred_element_type=F32)
                gv = g_ref[r0:r0 + tr, c0:c0 + cw].astype(F32)
                uv = u_ref[r0:r0 + tr, c0:c0 + cw].astype(F32)
                s = _sigmoid(gv)
                du_ref[r0:r0 + tr, c0:c0 + cw] = (da * (gv * s)).astype(BF16)
                dg_ref[r0:r0 + tr, c0:c0 + cw] = (da * uv * _dsilu(gv, s)).astype(BF16)

    blk = _nbytes((tm, D), BF16) + _nbytes((tn, D), BF16) + 4 * _nbytes((tm, tn), BF16)
    io = pl.BlockSpec((tm, tn), lambda i, n: (i, n))
    shp = jax.ShapeDtypeStruct((T, F), BF16)
    return _pallas(
        body, name=name, grid=(T // tm, F // tn),
        in_specs=[pl.BlockSpec((tm, D), lambda i, n: (i, 0)), pl.BlockSpec((tn, D), lambda i, n: (n, 0)), io, io, TOKEN],
        out_specs=[io, io], out_shape=[shp, shp],
        compiler_params=_params(("parallel", "parallel"), blk),
    )(dfb, wd, g, u, after)


def _nt_panel(lhs_list, w_list, after, name):
    T = lhs_list[0].shape[0]
    nsh, Dout, Ks = w_list[0].shape
    npair = len(lhs_list)
    tm = _row_tile(T, 1408, 16)
    tn = Dout // 2

    def body(*refs):
        l_refs, w_refs, o_ref = refs[:npair], refs[npair:2 * npair], refs[2 * npair + 1]
        j = pl.program_id(2)
        k0 = Ks - Ks % MXU_COLS if npair == 2 and 2 * (Ks % MXU_COLS) == MXU_COLS else Ks
        acc = None
        for p in range(npair):
            part = lax.dot_general(l_refs[p][:, :k0], w_refs[p][:, :k0], NT_DIMS, preferred_element_type=F32)
            acc = part if acc is None else acc + part
        if k0 < Ks:
            lhs = jnp.concatenate([l_refs[p][:, k0:] for p in range(npair)], axis=1)
            rhs = jnp.concatenate([w_refs[p][:, k0:] for p in range(npair)], axis=1)
            acc = acc + lax.dot_general(lhs, rhs, NT_DIMS, preferred_element_type=F32)

        @pl.when(j == 0)
        def _():
            o_ref[...] = acc

        @pl.when(j > 0)
        def _():
            o_ref[...] += acc

    blk = npair * (_nbytes((tm, Ks), BF16) + _nbytes((tn, Ks), BF16)) + 2 * _nbytes((tm, tn), F32)
    return _pallas(
        body, name=name, grid=(Dout // tn, T // tm, nsh),
        in_specs=[pl.BlockSpec((tm, Ks), lambda n, i, j: (i, j))] * npair
                 + [pl.BlockSpec((None, tn, Ks), lambda n, i, j: (j, n, 0))] * npair + [TOKEN],
        out_specs=pl.BlockSpec((tm, tn), lambda n, i, j: (i, n)),
        out_shape=jax.ShapeDtypeStruct((T, Dout), F32),
        compiler_params=_params(("parallel", "parallel", "arbitrary"), blk),
    )(*lhs_list, *w_list, after)


def _tn_call(name, grid, lhs, lhs_spec, rhs_list, rhs_specs, out_shapes, out_specs, blk, after=None):
    nr = len(rhs_list)
    extra = [] if after is None else [after]

    def body(*refs):
        l_ref, r_refs, o_refs = refs[0], refs[1:1 + nr], refs[len(refs) - nr:]
        k = pl.program_id(len(grid) - 1)
        lv = l_ref[...]
        for q in range(nr):
            part = lax.dot_general(lv, r_refs[q][...], TN_DIMS, preferred_element_type=F32)
            part = part.reshape(o_refs[q].shape)

            @pl.when(k == 0)
            def _(o=o_refs[q], part=part):
                o[...] = part

            @pl.when(k > 0)
            def _(o=o_refs[q], part=part):
                o[...] += part

    return _pallas(
        body, name=name, grid=grid, in_specs=[lhs_spec] + rhs_specs + [TOKEN] * len(extra), out_specs=out_specs,
        out_shape=out_shapes, compiler_params=_params(("parallel",) * (len(grid) - 1) + ("arbitrary",), blk),
    )(lhs, *rhs_list, *extra)


def _tk(T):
    return T


def _wgrad_cols(n, rhs_list, name, after=None):
    T, D = n.shape
    Ns = rhs_list[0].shape[1] // N_CHIPS
    tk = _tk(T)
    nr = len(rhs_list)
    tm = D // 4
    blk = _nbytes((tk, tm), BF16) + nr * (_nbytes((tk, Ns), BF16) + 2 * _nbytes((tm, Ns), F32))
    return _tn_call(
        name, (N_CHIPS, D // tm, T // tk), n, pl.BlockSpec((tk, tm), lambda j, m, k: (k, m)),
        rhs_list, [pl.BlockSpec((tk, Ns), lambda j, m, k: (k, j))] * nr,
        [jax.ShapeDtypeStruct((N_CHIPS, 2, D // 2, Ns), F32)] * nr,
        [pl.BlockSpec((None, None, tm, Ns), lambda j, m, k: (j, m // 2, m % 2, 0))] * nr, blk, after)


def _wgrad_down(a, dfb, name):
    T, F = a.shape
    D = dfb.shape[1]
    Fs = F // N_CHIPS
    tk = _tk(T)
    tn = D // 4
    blk = _nbytes((tk, Fs), BF16) + _nbytes((tk, tn), BF16) + 2 * _nbytes((Fs, tn), F32)
    return _tn_call(
        name, (N_CHIPS, D // tn, T // tk), a, pl.BlockSpec((tk, Fs), lambda j, n, k: (k, j)),
        [dfb], [pl.BlockSpec((tk, tn), lambda j, n, k: (k, n))],
        [jax.ShapeDtypeStruct((N_CHIPS, 2, Fs // 2, D), F32)],
        [pl.BlockSpec((None, 2, Fs // 2, tn), lambda j, n, k: (j, 0, 0, n))], blk)[0]


def _wgrad_out(y, dmb):
    T, D = y.shape
    tk = _tk(T)
    tn = D // 2
    rows = D // (2 * N_CHIPS)
    blk = _nbytes((tk, D // 2), BF16) + _nbytes((tk, tn), BF16) + 2 * _nbytes((D // 2, tn), F32)
    return _tn_call(
        "wgrad_w_out", (2, D // tn, T // tk), y, pl.BlockSpec((tk, D // 2), lambda m, n, k: (k, m)),
        [dmb], [pl.BlockSpec((tk, tn), lambda m, n, k: (k, n))],
        [jax.ShapeDtypeStruct((N_CHIPS, 2, rows, D), F32)],
        [pl.BlockSpec((2, 2, rows, tn), lambda m, n, k: (m, 0, 0, n))], blk)[0]


def _row_masks(i, last):
    rows = i * TT + lax.broadcasted_iota(jnp.int32, (TT, 1), 0)
    prows = i * TT - HALO + lax.broadcasted_iota(jnp.int32, (HALO, 1), 0)
    return rows >= PAD, (prows >= PAD) & (i > 0), i < last


def _conv_inputs(u, up, mask_c, mask_p, zbuf, pbuf, C1):
    b, c, v, a, g = (u[:, k * C1:(k + 1) * C1] for k in range(5))
    cp, vp, ap, gp = (up[:, k * C1:(k + 1) * C1] for k in range(1, 5))
    sg = _sigmoid(g)
    pbuf[0:HALO, :] = jnp.where(mask_p, cp * vp, 0.0)
    pbuf[HALO:, :] = jnp.where(mask_c, c * v, 0.0)
    if zbuf is not None:
        zbuf[0:HALO, :] = jnp.where(mask_p, ap * _sigmoid(gp), 0.0)
        zbuf[HALO:, :] = jnp.where(mask_c, a * sg, 0.0)
    return b, c, v, a, sg


SUBLANES = 8
SHIFT_ROWS = TT + HALO - SUBLANES


def _shifted_scratch(C1):
    return pltpu.VMEM((SUBLANES - 1, SHIFT_ROWS, C1), F32)


def _fill_shifted(buf, sh):
    for r in range(1, SUBLANES):
        sh[r - 1] = buf[r:r + SHIFT_ROWS, :]


LANES = 128


def _window(buf, sh, lo, c0):
    if sh is None or lo % SUBLANES == 0:
        return buf[lo:lo + TT, c0:c0 + LANES]
    q, r = divmod(lo, SUBLANES)
    return sh[r - 1, q * SUBLANES:q * SUBLANES + TT, c0:c0 + LANES]


def _tap_sum(w_ref, buf, sh, starts):
    chunks = []
    for c0 in range(0, buf.shape[1], LANES):
        acc = None
        for k, lo in enumerate(starts):
            term = w_ref[k:k + 1, c0:c0 + LANES] * _window(buf, sh, lo, c0)
            acc = term if acc is None else acc + term
        chunks.append(acc)
    return jnp.concatenate(chunks, axis=1)


def _causal_conv(w_ref, buf, sh=None):
    K = w_ref.shape[0]
    return _tap_sum(w_ref, buf, sh, [HALO - (K - 1) + k for k in range(K)])


def _anticausal_conv(w_ref, buf, sh=None):
    K = w_ref.shape[0]
    return _tap_sum(w_ref, buf, sh, [K - 1 - k for k in range(K)])


def _conv_weight_sums(dw_ref, dy, buf, sh=None):
    K = dw_ref.shape[0]
    for c0 in range(0, buf.shape[1], LANES):
        dyc = dy[:, c0:c0 + LANES]
        for k in range(K):
            prod = dyc * _window(buf, sh, HALO - (K - 1) + k, c0)
            dw_ref[k:k + 1, c0:c0 + LANES] += jnp.sum(prod, axis=0, keepdims=True)


def _layernorm_stats(z1):
    mu = jnp.mean(z1, axis=-1, keepdims=True)
    zc = z1 - mu
    rs = lax.rsqrt(jnp.mean(zc * zc, axis=-1, keepdims=True) + EPS)
    return zc * rs, rs


def _mixer_specs(T, DIN, C1, ksc, kcf):
    cur = pl.BlockSpec((TT, DIN), lambda i: (i, 0))
    prev = pl.BlockSpec((HALO, DIN), lambda i: (jnp.maximum(i * (TT // HALO) - 1, 0), 0))
    full = lambda r: pl.BlockSpec((r, C1), lambda i: (0, 0))
    return cur, prev, [full(ksc), full(kcf), full(1), full(1), full(1)]


def _mix_conv_fwd(u, wsc, wcf, bcf, lg, lb):
    T, DIN = u.shape
    C1 = DIN // 5
    last = T // TT - 1

    def body(u_ref, up_ref, wsc_ref, wcf_ref, bcf_ref, lg_ref, lb_ref, y_ref, z1_ref, zbuf, pbuf, zsh):
        i = pl.program_id(0)
        mask_c, mask_p, _ = _row_masks(i, last)
        b, _, _, _, _ = _conv_inputs(u_ref[...], up_ref[...], mask_c, mask_p, zbuf, pbuf, C1)
        _fill_shifted(zbuf, zsh)
        cs = _causal_conv(wsc_ref, pbuf)
        z1 = _causal_conv(wcf_ref, zbuf, zsh) + bcf_ref[...]
        z1_ref[...] = z1
        zh, _ = _layernorm_stats(z1)
        ln = zh * lg_ref[...] + lb_ref[...]
        y_ref[:, 0:C1] = jnp.where(mask_c, b * cs, 0.0).astype(BF16)
        y_ref[:, C1:] = jnp.where(mask_c, jax.nn.silu(ln), 0.0).astype(BF16)

    cur, prev, small = _mixer_specs(T, DIN, C1, wsc.shape[0], wcf.shape[0])
    blk = _nbytes((TT + HALO, DIN), F32) + _nbytes((TT, 2 * C1), BF16) + 12 * _nbytes((TT + HALO, C1), F32)
    return _pallas(
        body, name="mix_conv_fwd", grid=(T // TT,),
        in_specs=[cur, prev] + small,
        out_specs=[pl.BlockSpec((TT, 2 * C1), lambda i: (i, 0)), pl.BlockSpec((TT, C1), lambda i: (i, 0))],
        out_shape=[jax.ShapeDtypeStruct((T, 2 * C1), BF16), jax.ShapeDtypeStruct((T, C1), F32)],
        scratch_shapes=[pltpu.VMEM((TT + HALO, C1), F32), pltpu.VMEM((TT + HALO, C1), F32), _shifted_scratch(C1)],
        compiler_params=_params(("arbitrary",), blk),
    )(u, u, wsc, wcf, bcf, lg, lb)


def _mix_conv_bwd1(u, z1, dy, wsc, lg, lb):
    T, DIN = u.shape
    C1 = DIN // 5
    last = T // TT - 1

    def body(u_ref, up_ref, z1_ref, dy_ref, wsc_ref, lg_ref, lb_ref,
             dz1_ref, dcs_ref, db_ref, dlg_ref, dlb_ref, dbcf_ref, pbuf):
        i = pl.program_id(0)
        mask_c, mask_p, _ = _row_masks(i, last)
        b, _, _, _, _ = _conv_inputs(u_ref[...], up_ref[...], mask_c, mask_p, None, pbuf, C1)
        cs = _causal_conv(wsc_ref, pbuf)
        zh, rs = _layernorm_stats(z1_ref[...])
        ln = zh * lg_ref[...] + lb_ref[...]
        dy = dy_ref[...]
        dysc = jnp.where(mask_c, dy[:, 0:C1], 0.0)
        dycf = jnp.where(mask_c, dy[:, C1:], 0.0)
        db_ref[...] = (dysc * cs).astype(BF16)
        dcs_ref[...] = dysc * b
        dl = dycf * _dsilu(ln, _sigmoid(ln))
        dzh = dl * lg_ref[...]
        dz1 = rs * (dzh - jnp.mean(dzh, axis=-1, keepdims=True) - zh * jnp.mean(dzh * zh, axis=-1, keepdims=True))
        dz1_ref[...] = dz1

        @pl.when(i == 0)
        def _():
            dlg_ref[...] = jnp.zeros_like(dlg_ref)
            dlb_ref[...] = jnp.zeros_like(dlb_ref)
            dbcf_ref[...] = jnp.zeros_like(dbcf_ref)

        dlg_ref[...] += jnp.sum(dl * zh, axis=0, keepdims=True)
        dlb_ref[...] += jnp.sum(dl, axis=0, keepdims=True)
        dbcf_ref[...] += jnp.sum(dz1, axis=0, keepdims=True)

    cur, prev, small = _mixer_specs(T, DIN, C1, wsc.shape[0], 1)
    tile = lambda: pl.BlockSpec((TT, C1), lambda i: (i, 0))
    vec = lambda: pl.BlockSpec((1, C1), lambda i: (0, 0))
    blk = _nbytes((TT + HALO, DIN), F32) + 5 * _nbytes((TT, C1), F32) + 12 * _nbytes((TT + HALO, C1), F32)
    return _pallas(
        body, name="mix_conv_bwd1", grid=(T // TT,),
        in_specs=[cur, prev, tile(), pl.BlockSpec((TT, 2 * C1), lambda i: (i, 0)), small[0], small[3], small[4]],
        out_specs=[tile(), tile(), tile(), vec(), vec(), vec()],
        out_shape=[jax.ShapeDtypeStruct((T, C1), F32), jax.ShapeDtypeStruct((T, C1), F32),
                   jax.ShapeDtypeStruct((T, C1), BF16)] + [jax.ShapeDtypeStruct((1, C1), F32)] * 3,
        scratch_shapes=[pltpu.VMEM((TT + HALO, C1), F32)],
        compiler_params=_params(("arbitrary",), blk),
    )(u, u, z1, dy, wsc, lg, lb)


def _mix_conv_bwd2(u, dz1, dcs, db, wsc, wcf):
    T, DIN = u.shape
    C1 = DIN // 5
    last = T // TT - 1
    ksc, kcf = wsc.shape[0], wcf.shape[0]

    def body(u_ref, up_ref, dz_ref, dzn_ref, dc_ref, dcn_ref, db_ref, wsc_ref, wcf_ref,
             du_ref, dbin_ref, dwsc_ref, dwcf_ref, zbuf, pbuf, dzbuf, dcbuf, zsh, dzsh):
        i = pl.program_id(0)
        mask_c, mask_p, has_next = _row_masks(i, last)
        _, c, v, a, sg = _conv_inputs(u_ref[...], up_ref[...], mask_c, mask_p, zbuf, pbuf, C1)
        dz1 = dz_ref[...]
        dcs = dc_ref[...]
        dzbuf[0:TT, :] = dz1
        dzbuf[TT:, :] = jnp.where(has_next, dzn_ref[...], 0.0)
        dcbuf[0:TT, :] = dcs
        dcbuf[TT:, :] = jnp.where(has_next, dcn_ref[...], 0.0)

        @pl.when(i == 0)
        def _():
            dbin_ref[...] = jnp.zeros_like(dbin_ref)
            dwsc_ref[...] = jnp.zeros_like(dwsc_ref)
            dwcf_ref[...] = jnp.zeros_like(dwcf_ref)

        _fill_shifted(zbuf, zsh)
        _fill_shifted(dzbuf, dzsh)
        _conv_weight_sums(dwcf_ref, dz1, zbuf, zsh)
        _conv_weight_sums(dwsc_ref, dcs, pbuf)
        dz0 = jnp.where(mask_c, _anticausal_conv(wcf_ref, dzbuf, dzsh), 0.0)
        dp = jnp.where(mask_c, _anticausal_conv(wsc_ref, dcbuf), 0.0)
        parts = (db_ref[...].astype(F32), dp * v, dp * c, dz0 * sg, dz0 * a * sg * (1.0 - sg))
        for k, part in enumerate(parts):
            du_ref[:, k * C1:(k + 1) * C1] = part.astype(BF16)
            dbin_ref[:, k * C1:(k + 1) * C1] += jnp.sum(part, axis=0, keepdims=True)

    cur, prev, small = _mixer_specs(T, DIN, C1, ksc, kcf)
    tile = lambda: pl.BlockSpec((TT, C1), lambda i: (i, 0))
    nxt = lambda: pl.BlockSpec((HALO, C1), lambda i: (jnp.minimum((i + 1) * (TT // HALO), T // HALO - 1), 0))
    blk = (_nbytes((TT + HALO, DIN), F32) + _nbytes((TT, DIN), BF16) + 5 * _nbytes((TT, C1), F32)
           + 16 * _nbytes((TT + HALO, C1), F32))
    buf = lambda: pltpu.VMEM((TT + HALO, C1), F32)
    return _pallas(
        body, name="mix_conv_bwd2", grid=(T // TT,),
        in_specs=[cur, prev, tile(), nxt(), tile(), nxt(), tile(), small[0], small[1]],
        out_specs=[pl.BlockSpec((TT, DIN), lambda i: (i, 0)), pl.BlockSpec((1, DIN), lambda i: (0, 0)),
                   pl.BlockSpec((ksc, C1), lambda i: (0, 0)), pl.BlockSpec((kcf, C1), lambda i: (0, 0))],
        out_shape=[jax.ShapeDtypeStruct((T, DIN), BF16), jax.ShapeDtypeStruct((1, DIN), F32),
                   jax.ShapeDtypeStruct((ksc, C1), F32), jax.ShapeDtypeStruct((kcf, C1), F32)],
        scratch_shapes=[buf(), buf(), buf(), buf(), _shifted_scratch(C1), _shifted_scratch(C1)],
        compiler_params=_params(("arbitrary",), blk),
    )(u, u, dz1, dz1, dcs, dcs, db, wsc, wcf)


def _place():
    x, y, c = lax.axis_index("x"), lax.axis_index("y"), lax.axis_index("c")
    chips = [(1 - x, y), (x, 1 - y), (1 - x, 1 - y)]
    return x, y, c, chips


ANY = pl.BlockSpec(memory_space=pl.ANY)


def _cast_own_block(place, w, name):
    R, C = w.shape
    tr = _row_tile(R // 2, 1024, 16)
    nblk = R // 2 // tr

    def body(place_ref, w_ref, o_ref):
        o_ref[...] = w_ref[...].astype(BF16)

    return _pallas(
        body, name=name,
        grid_spec=pltpu.PrefetchScalarGridSpec(
            num_scalar_prefetch=1, grid=(2, nblk),
            in_specs=[pl.BlockSpec((tr, C), lambda h, i, p: (h * nblk + i, 0))],
            out_specs=pl.BlockSpec((None, None, tr, C), lambda h, i, p: (p[0], h, i, 0))),
        out_shape=jax.ShapeDtypeStruct((N_CHIPS, 2, R // 2, C), BF16),
        compiler_params=_params(("parallel", "parallel"), _nbytes((tr, C), F32) + _nbytes((tr, C), BF16)),
    )(place, w)


HBM = pl.BlockSpec(memory_space=pltpu.HBM)
SEM = pl.BlockSpec(memory_space=pltpu.SEMAPHORE)
EFFECT = pltpu.SideEffectType.DATAFLOW_SIDE_EFFECTING


def _gather_copies(refs, send, recv, rels=(0, 1, 2)):
    x, y, c, chips = _place()
    s = 2 * x + y
    n = len(rels)
    return [pltpu.make_async_remote_copy(src_ref=ref.at[s, c], dst_ref=ref.at[s, c], send_sem=send.at[n * w + k],
                                         recv_sem=recv.at[n * w + k], device_id=(*chips[r], c), device_id_type=MESH)
            for w, ref in enumerate(refs) for k, r in enumerate(rels)]


def _scatter_copies(refs, send, recv):
    x, y, c, chips = _place()
    nw = len(refs) // 2
    return [pltpu.make_async_remote_copy(src_ref=refs[w].at[2 * tx + ty], dst_ref=refs[nw + w].at[r],
                                         send_sem=send.at[3 * w + r], recv_sem=recv.at[3 * w + r],
                                         device_id=(tx, ty, c), device_id_type=MESH)
            for w in range(nw) for r, (tx, ty) in enumerate(chips)]


def _pair_copies(refs, send, recv):
    x, y, c, _ = _place()
    nw = len(refs) // 2
    return [pltpu.make_async_remote_copy(src_ref=refs[w].at[j, 1 - c], dst_ref=refs[nw + w].at[j],
                                         send_sem=send.at[N_CHIPS * w + j], recv_sem=recv.at[N_CHIPS * w + j],
                                         device_id=(x, y, 1 - c), device_id_type=MESH)
            for w in range(nw) for j in range(N_CHIPS)]


def _forward_copies(refs, send, recv, rels=(0, 1, 2)):
    x, y, c, chips = _place()
    n = len(rels)
    copies = []
    for w, ref in enumerate(refs):
        for k, r in enumerate(rels):
            tx, ty = chips[r]
            blk = ref.at[2 * tx + ty, c]
            copies.append(pltpu.make_async_remote_copy(src_ref=blk, dst_ref=blk, send_sem=send.at[n * w + k],
                                                       recv_sem=recv.at[n * w + k], device_id=(x, y, 1 - c),
                                                       device_id_type=MESH))
    return copies


def _half_copies(refs, send, recv):
    x, y, c, _ = _place()
    return [pltpu.make_async_remote_copy(src_ref=ref.at[c], dst_ref=ref.at[c], send_sem=send.at[w], recv_sem=recv.at[w],
                                         device_id=(x, y, 1 - c), device_id_type=MESH)
            for w, ref in enumerate(refs)]


def _start_copies(bufs, after, ncopies, make_copies, name):
    n = len(bufs)

    def body(*refs):
        in_refs, send, recv, token = refs[:n], refs[n + 1], refs[n + 2], refs[2 * n + 3]
        for cp in make_copies(in_refs, send, recv):
            cp.start()
        token[...] = jnp.zeros_like(token)

    outs = _pallas(
        body, name=name, in_specs=[HBM] * n + [ANY],
        out_specs=[SEM, SEM] + [HBM] * n + [pl.BlockSpec(memory_space=pltpu.VMEM)],
        out_shape=[pltpu.SemaphoreType.DMA((ncopies,)), pltpu.SemaphoreType.DMA((ncopies,))]
                  + [pltpu.HBM(b.shape, b.dtype) for b in bufs] + [jax.ShapeDtypeStruct((8, 128), F32)],
        input_output_aliases={k: 2 + k for k in range(n)},
        compiler_params=pltpu.CompilerParams(has_side_effects=EFFECT),
    )(*[pltpu.with_memory_space_constraint(b, pltpu.HBM) for b in bufs], after)
    return outs[0], outs[1], list(outs[2:2 + n]), outs[2 + n]


def _wait_copies(send, recv, bufs, after, make_copies, name):
    n = len(bufs)

    def body(*refs):
        in_refs, send_ref, recv_ref = refs[:n], refs[n], refs[n + 1]
        for cp in make_copies(in_refs, send_ref, recv_ref):
            cp.wait_send()
            cp.wait_recv()

    outs = _pallas(
        body, name=name, in_specs=[HBM] * n + [SEM, SEM, ANY], out_specs=[HBM] * n,
        out_shape=[pltpu.HBM(b.shape, b.dtype) for b in bufs],
        input_output_aliases={k: k for k in range(n)},
        compiler_params=pltpu.CompilerParams(has_side_effects=EFFECT),
    )(*bufs, send, recv, after)
    return list(outs)


def _forward_halves(bufs, name, rels=(0, 1, 2)):
    nw = len(bufs)
    n = len(rels)

    def body(*refs):
        o_refs = refs[nw:2 * nw]
        send, recv = refs[2 * nw:]
        x, y, c, chips = _place()
        sib = (x, y, 1 - c)
        copies = []
        for w in range(nw):
            for k, r in enumerate(rels):
                tx, ty = chips[r]
                ref = o_refs[w].at[2 * tx + ty, c]
                cp = pltpu.make_async_remote_copy(src_ref=ref, dst_ref=ref, send_sem=send.at[n * w + k],
                                                  recv_sem=recv.at[n * w + k], device_id=sib, device_id_type=MESH)
                cp.start()
                copies.append(cp)
        for w in range(nw):
            for k, r in enumerate(rels):
                tx, ty = chips[r]
                ref = o_refs[w].at[2 * tx + ty, 1 - c]
                pltpu.make_async_remote_copy(src_ref=ref, dst_ref=ref, send_sem=send.at[n * w + k],
                                             recv_sem=recv.at[n * w + k], device_id=sib, device_id_type=MESH).wait_recv()
        for cp in copies:
            cp.wait_send()

    return _pallas(
        body, name=name, in_specs=[ANY] * nw, out_specs=[ANY] * nw,
        out_shape=[jax.ShapeDtypeStruct(b.shape, b.dtype) for b in bufs],
        input_output_aliases={w: w for w in range(nw)},
        scratch_shapes=[pltpu.SemaphoreType.DMA((n * nw,)), pltpu.SemaphoreType.DMA((n * nw,))],
    )(*bufs)


def _share_small(v, reduce, name, after):
    R, C = v.shape

    def body(v_ref, after_ref, o_ref, *scratch):
        if reduce:
            all_ref, send, recv, lsem = scratch
        else:
            all_ref = o_ref
            send, recv, lsem = scratch
        x, y, c, _ = _place()
        me = 4 * x + 2 * y + c
        loc = pltpu.make_async_copy(v_ref, all_ref.at[me], lsem)
        loc.start()
        copies = []
        for k in range(1, N_DEV):
            kx, ky, kc = (k >> 2) & 1, (k >> 1) & 1, k & 1
            peer = (x ^ kx, y ^ ky, c ^ kc)
            cp = pltpu.make_async_remote_copy(src_ref=v_ref, dst_ref=all_ref.at[me], send_sem=send.at[k - 1],
                                              recv_sem=recv.at[k - 1], device_id=peer, device_id_type=MESH)
            cp.start()
            copies.append(cp)
        for k in range(1, N_DEV):
            kx, ky, kc = (k >> 2) & 1, (k >> 1) & 1, k & 1
            src = 4 * (x ^ kx) + 2 * (y ^ ky) + (c ^ kc)
            pltpu.make_async_remote_copy(src_ref=v_ref, dst_ref=all_ref.at[src], send_sem=send.at[k - 1],
                                         recv_sem=recv.at[k - 1], device_id=(x, y, c), device_id_type=MESH).wait_recv()
        for cp in copies:
            cp.wait_send()
        loc.wait()
        if reduce:
            total = all_ref[0]
            for d in range(1, N_DEV):
                total = total + all_ref[d]
            o_ref[...] = total

    vm = pl.BlockSpec(memory_space=pltpu.VMEM)
    sems = [pltpu.SemaphoreType.DMA((N_DEV - 1,)), pltpu.SemaphoreType.DMA((N_DEV - 1,)), pltpu.SemaphoreType.DMA]
    if reduce:
        out_shape = jax.ShapeDtypeStruct((R, C), F32)
        scratch = [pltpu.VMEM((N_DEV, R, C), F32)] + sems
    else:
        out_shape = jax.ShapeDtypeStruct((N_DEV, R, C), F32)
        scratch = sems
    return _pallas(
        body, name=name, in_specs=[vm, ANY], out_specs=vm, out_shape=out_shape, scratch_shapes=scratch,
        compiler_params=pltpu.CompilerParams(vmem_limit_bytes=int(min(4 * N_DEV * R * C * 4 + 2 ** 24, 2 ** 25 + 2 ** 24))),
    )(v, after)


def _pair_sum(place, g, rb, name):
    _, _, Rh, C = g.shape
    tr = _row_tile(Rh, 1024, 16)

    def body(place_ref, g_ref, r_ref, q_ref):
        q_ref[...] = (g_ref[...] + r_ref[...]).astype(BF16)

    blk = 2 * _nbytes((tr, C), F32) + _nbytes((tr, C), BF16)
    return _pallas(
        body, name=name,
        grid_spec=pltpu.PrefetchScalarGridSpec(
            num_scalar_prefetch=1, grid=(N_CHIPS - 1, Rh // tr),
            in_specs=[pl.BlockSpec((None, None, tr, C), lambda j, i, p: (p[0] ^ (j + 1), p[1], i, 0)),
                      pl.BlockSpec((None, tr, C), lambda j, i, p: (p[0] ^ (j + 1), i, 0))],
            out_specs=pl.BlockSpec((None, tr, C), lambda j, i, p: (p[0] ^ (j + 1), i, 0))),
        out_shape=jax.ShapeDtypeStruct((N_CHIPS, Rh, C), BF16),
        compiler_params=_params(("parallel", "parallel"), blk),
    )(place, g, rb)


def _chip_sum(place, g, rb, rc, name):
    _, _, Rh, C = g.shape
    tr = _row_tile(Rh, 512, 16)

    def body(place_ref, g_ref, r_ref, rc_ref, o_ref):
        total = g_ref[...] + r_ref[...]
        for r in range(3):
            total = total + rc_ref[r].astype(F32)
        o_ref[...] = total

    blk = 3 * _nbytes((tr, C), F32) + 3 * _nbytes((tr, C), BF16)
    return _pallas(
        body, name=name,
        grid_spec=pltpu.PrefetchScalarGridSpec(
            num_scalar_prefetch=1, grid=(Rh // tr,),
            in_specs=[pl.BlockSpec((None, None, tr, C), lambda i, p: (p[0], p[1], i, 0)),
                      pl.BlockSpec((None, tr, C), lambda i, p: (p[0], i, 0)),
                      pl.BlockSpec((3, tr, C), lambda i, p: (0, i, 0))],
            out_specs=pl.BlockSpec((None, tr, C), lambda i, p: (p[1], i, 0))),
        out_shape=jax.ShapeDtypeStruct((2, Rh, C), F32),
        compiler_params=_params(("parallel",), blk),
    )(place, g, rb, rc)


def _adamw_math(w, g, m, v):
    m = ADAM_B1 * m + (1.0 - ADAM_B1) * g
    v = ADAM_B2 * v + (1.0 - ADAM_B2) * jnp.square(g)
    m_hat = m / (1.0 - ADAM_B1 ** ADAM_STEP)
    v_hat = v / (1.0 - ADAM_B2 ** ADAM_STEP)
    delta = -ADAM_LR * (m_hat / (jnp.sqrt(v_hat) + ADAM_EPS) + ADAM_WD * w)
    return delta, m, v


def _adamw(w, g, m, v, name):
    R, C = w.shape
    tr = _row_tile(R, 512)

    def body(w_ref, g_ref, m_ref, v_ref, go_ref, d_ref, nm_ref, nv_ref):
        gv = g_ref[...]
        d, nm, nv = _adamw_math(w_ref[...], gv, m_ref[...], v_ref[...])
        go_ref[...] = gv
        d_ref[...] = d
        nm_ref[...] = nm
        nv_ref[...] = nv

    spec = pl.BlockSpec((tr, C), lambda i: (i, 0))
    shp = jax.ShapeDtypeStruct((R, C), F32)
    return _pallas(
        body, name=name, grid=(R // tr,), in_specs=[spec] * 4, out_specs=[spec] * 4, out_shape=[shp] * 4,
        compiler_params=_params(("parallel",), 8 * _nbytes((tr, C), F32)),
    )(w, g, m, v)


def _adamw_small(ws, gs, ms, vs):
    n = len(ws)

    def body(*refs):
        for k in range(n):
            w_ref, g_ref, m_ref, v_ref = (refs[q * n + k] for q in range(4))
            d, nm, nv = _adamw_math(w_ref[...], g_ref[...], m_ref[...], v_ref[...])
            refs[4 * n + k][...] = d
            refs[5 * n + k][...] = nm
            refs[6 * n + k][...] = nv

    vm = pl.BlockSpec(memory_space=pltpu.VMEM)
    shapes = [jax.ShapeDtypeStruct(w.shape, F32) for w in ws]
    outs = _pallas(
        body, name="adamw_small", in_specs=[vm] * (4 * n), out_specs=[vm] * (3 * n), out_shape=shapes * 3,
    )(*ws, *gs, *ms, *vs)
    return outs[:n], outs[n:2 * n], outs[2 * n:]


def _pad_rows(a, rows):
    return jnp.pad(a, ((0, rows - a.shape[0]), (0, 0)))


def kernel(x, meta_tokens, ffn1_norm, ffn1_w_gate, ffn1_w_up, ffn1_w_down, mix_norm, w_in, b_in, conv_sc_w, conv_cf_w, conv_cf_b, ln_cf_g, ln_cf_b, w_out, ffn2_norm, ffn2_w_gate, ffn2_w_up, ffn2_w_down, final_norm, loss_target, m_meta_tokens, m_ffn1_norm, m_ffn1_w_gate, m_ffn1_w_up, m_ffn1_w_down, m_mix_norm, m_w_in, m_b_in, m_conv_sc_w, m_conv_cf_w, m_conv_cf_b, m_ln_cf_g, m_ln_cf_b, m_w_out, m_ffn2_norm, m_ffn2_w_gate, m_ffn2_w_up, m_ffn2_w_down, m_final_norm, v_meta_tokens, v_ffn1_norm, v_ffn1_w_gate, v_ffn1_w_up, v_ffn1_w_down, v_mix_norm, v_w_in, v_b_in, v_conv_sc_w, v_conv_cf_w, v_conv_cf_b, v_ln_cf_g, v_ln_cf_b, v_w_out, v_ffn2_norm, v_ffn2_w_gate, v_ffn2_w_up, v_ffn2_w_down, v_final_norm):
    xi, yi, ci = lax.axis_index("x"), lax.axis_index("y"), lax.axis_index("c")
    chip = 2 * xi + yi
    place = jnp.stack([chip, ci]).astype(jnp.int32)

    x2 = x[0]
    tgt = loss_target[0]
    S, D = x2.shape
    C1 = D // 2
    cs = conv_sc_w.shape[2]
    ksc, kcf = conv_sc_w.shape[1], conv_cf_w.shape[1]
    ms = meta_tokens.shape[1]

    big = {"ffn1_w_gate": ffn1_w_gate, "ffn1_w_up": ffn1_w_up, "ffn1_w_down": ffn1_w_down, "w_in": w_in, "w_out": w_out,
           "ffn2_w_gate": ffn2_w_gate, "ffn2_w_up": ffn2_w_up, "ffn2_w_down": ffn2_w_down}
    big_m = {"ffn1_w_gate": m_ffn1_w_gate, "ffn1_w_up": m_ffn1_w_up, "ffn1_w_down": m_ffn1_w_down, "w_in": m_w_in,
             "w_out": m_w_out, "ffn2_w_gate": m_ffn2_w_gate, "ffn2_w_up": m_ffn2_w_up, "ffn2_w_down": m_ffn2_w_down}
    big_v = {"ffn1_w_gate": v_ffn1_w_gate, "ffn1_w_up": v_ffn1_w_up, "ffn1_w_down": v_ffn1_w_down, "w_in": v_w_in,
             "w_out": v_w_out, "ffn2_w_gate": v_ffn2_w_gate, "ffn2_w_up": v_ffn2_w_up, "ffn2_w_down": v_ffn2_w_down}
    buf = {nm: _cast_own_block(place, w[0], "cast_" + nm) for nm, w in big.items()}
    whole_weight = lambda g: g.reshape(N_CHIPS, 2 * g.shape[2], g.shape[3])
    corner = lambda a: a.reshape(-1, a.shape[-1])[:8, :128]

    NEAR, FAR = (0, 1), (2,)
    groups = {"ffn1_near": (["ffn1_w_gate", "ffn1_w_up", "ffn1_w_down"], NEAR),
              "ffn1_far": (["ffn1_w_gate", "ffn1_w_up", "ffn1_w_down"], FAR),
              "mix": (["w_in", "w_out"], NEAR + FAR),
              "ffn2_up": (["ffn2_w_gate", "ffn2_w_up"], NEAR + FAR),
              "ffn2_down": (["ffn2_w_down"], NEAR + FAR)}
    started = {}

    def start(tag, after):
        nms, rels = groups[tag]
        copies = functools.partial(_gather_copies, rels=rels)
        send, recv, thru, token = _start_copies([buf[nm] for nm in nms], after, len(rels) * len(nms), copies,
                                                "gather_start_" + tag)
        for nm, b in zip(nms, thru):
            buf[nm] = b
        started[tag] = (send, recv, copies)
        return token

    def arrive(tag, after, then=None):
        nms, rels = groups[tag]
        send, recv, copies = started[tag]
        got = _wait_copies(send, recv, [buf[nm] for nm in nms], corner(after), copies, "gather_wait_" + tag)
        for nm, b in zip(nms, got):
            buf[nm] = b
        if then is not None:
            start(then, corner(got[0]))
        for nm, b in zip(nms, _forward_halves([buf[nm] for nm in nms], "gather_forward_" + tag, rels)):
            buf[nm] = b

    def passing(tag, after):
        nms, rels = groups[tag]
        send, recv, copies = started[tag]
        got = _wait_copies(send, recv, [buf[nm] for nm in nms], corner(after), copies, "gather_wait_" + tag)
        copies = functools.partial(_forward_copies, rels=rels)
        send, recv, thru, token = _start_copies(got, corner(got[0]), len(rels) * len(nms), copies,
                                                "gather_pass_" + tag)
        for nm, b in zip(nms, thru):
            buf[nm] = b
        started[tag] = (send, recv, copies)
        return token

    def passed(tag, after):
        nms, _ = groups[tag]
        send, recv, copies = started[tag]
        for nm, b in zip(nms, _wait_copies(send, recv, [buf[nm] for nm in nms], corner(after), copies,
                                           "gather_passed_" + tag)):
            buf[nm] = b

    tokens = lambda *arrays: jnp.concatenate([corner(a).astype(F32) for a in arrays], axis=0)
    assert ksc <= 8 and kcf <= 32 and cs <= ms
    pack = jnp.concatenate([
        meta_tokens,
        jnp.pad(conv_sc_w[0], ((0, 8 - ksc), (0, ms - cs))),
        jnp.pad(conv_cf_w[0], ((0, 32 - kcf), (0, ms - cs)))], axis=0)
    everyone = _share_small(pack, False, "share_params", pack)[0::2]
    meta_full = jnp.transpose(everyone[:, :N_META, :], (1, 0, 2)).reshape(N_META, D)
    wsc_full = jnp.transpose(everyone[:, N_META:N_META + ksc, :cs], (1, 0, 2)).reshape(ksc, C1)
    wcf_full = jnp.transpose(everyone[:, N_META + 8:N_META + 8 + kcf, :cs], (1, 0, 2)).reshape(kcf, C1)

    token = start("ffn1_near", corner(everyone))

    ffn1 = lambda: [whole_weight(buf[nm]) for nm in ["ffn1_w_gate", "ffn1_w_up", "ffn1_w_down"]]
    own = chip[None].astype(jnp.int32)
    near = jnp.stack([chip ^ 2, chip ^ 1]).astype(jnp.int32)
    far = (chip ^ 3)[None].astype(jnp.int32)
    all_chips = jnp.arange(N_CHIPS, dtype=jnp.int32)

    hs0, n1 = _embed_rms(x2, meta_full, ffn1_norm)
    wg1, wu1, wd1 = ffn1()
    gua = _ffn_up(n1, wg1, wu1, own, None, token, "ffn1_up_own")
    hs1 = _ffn_down(gua[2], wd1, hs0, own, "ffn1_down_own")
    later = [buf[nm] for nm in ["w_in", "w_out", "ffn2_w_gate", "ffn2_w_up", "ffn2_w_down"]]
    arrive("ffn1_near", tokens(hs1, *later), "ffn1_far")
    wg1, wu1, wd1 = ffn1()
    gua = _ffn_up(n1, wg1, wu1, near, gua, token, "ffn1_up_near")
    tok = start("mix", corner(gua[2]))
    tok = passing("ffn1_far", tok)
    hs1 = _ffn_down(gua[2], ffn1()[2], hs1, near, "ffn1_down_near", tok)
    passed("ffn1_far", hs1)
    wg1, wu1, wd1 = ffn1()
    g1, u1, a1 = _ffn_up(n1, wg1, wu1, far, gua, token, "ffn1_up_far")
    tok = passing("mix", a1)
    hs1, n2 = _ffn_down(a1, wd1, hs1, far, "ffn1_down_far", tok, mix_norm)
    F = N_CHIPS * wd1.shape[1]
    tok = start("ffn2_up", corner(hs1))
    passed("mix", tok)
    win, wout = whole_weight(buf["w_in"]), whole_weight(buf["w_out"])
    u = _mix_in(n2, win, b_in)
    y, z1 = _mix_conv_fwd(u, wsc_full, wcf_full, conv_cf_b, ln_cf_g, ln_cf_b)
    tok = start("ffn2_down", corner(y))
    tok = passing("ffn2_up", tok)
    hs2, n3 = _mix_out(y, wout.reshape(D, D), hs1, ffn2_norm, tok)
    passed("ffn2_up", hs2)
    wg2, wu2 = whole_weight(buf["ffn2_w_gate"]), whole_weight(buf["ffn2_w_up"])
    g2, u2, a2 = _ffn_up(n3, wg2, wu2, all_chips, None, token, "ffn2_up")
    passed("ffn2_down", passing("ffn2_down", a2))
    wd2 = whole_weight(buf["ffn2_w_down"])
    hs3 = _ffn_down_whole(a2, wd2.reshape(F, D), hs2, "ffn2_down")
    token_ffn2 = token

    def pair_start(group, after, tag):
        gs = [g for _, g in group]
        lands = [lax.empty((N_CHIPS,) + g.shape[2:], F32) for g in gs]
        send, recv, thru, token = _start_copies(gs + lands, after, N_CHIPS * len(gs), _pair_copies,
                                                "pair_start_" + tag)
        return (group, send, recv, thru, tag), token

    def scatter_start(state, after):
        group, send, recv, thru, tag = state
        thru = _wait_copies(send, recv, thru, corner(after), _pair_copies, "pair_wait_" + tag)
        gs, sib = thru[:len(group)], thru[len(group):]
        sums = [_pair_sum(place, g, rb, "pair_sum_" + nm) for (nm, _), g, rb in zip(group, gs, sib)]
        lands = [lax.empty((3,) + q.shape[1:], BF16) for q in sums]
        send, recv, thru, token = _start_copies(sums + lands, corner(sums[-1]), 3 * len(gs), _scatter_copies,
                                                "scatter_start_" + tag)
        return ([(nm, g) for (nm, _), g in zip(group, gs)], sib, send, recv, thru, tag), token

    def finish_sum(state, after):
        group, sib, send, recv, thru, tag = state
        lands = _wait_copies(send, recv, thru, corner(after), _scatter_copies, "scatter_wait_" + tag)[len(group):]
        mine = [_chip_sum(place, g, rb, rc, "chip_sum_" + nm) for (nm, g), rb, rc in zip(group, sib, lands)]
        send, recv, thru, token = _start_copies(mine, corner(mine[-1]), len(mine), _half_copies, "half_start_" + tag)
        return (group, send, recv, thru, tag), token

    def finish_adam(state, after):
        group, send, recv, thru, tag = state
        whole = _wait_copies(send, recv, thru, corner(after), _half_copies, "half_wait_" + tag)
        out = {}
        for (nm, _), g in zip(group, whole):
            w = big[nm]
            g_out, d, new_m, new_v = _adamw(w[0], g.reshape(w.shape[1:]), big_m[nm][0], big_v[nm][0], "adamw_" + nm)
            out[nm] = (g_out[None], d[None], new_m[None], new_v[None])
        return out

    dhs3, df2, loss_row, d_final = _final_loss(hs3, final_norm.reshape(1, D), tgt)

    dg2, du2 = _ffn_bwd_act(df2, wd2.reshape(F, D), g2, u2, token_ffn2, "ffn2_bwd_act")
    gw_d2 = _wgrad_down(a2, df2, "wgrad_ffn2_down")
    gw_g2 = _wgrad_cols(n3, [dg2], "wgrad_ffn2_gate")[0]
    gw_u2 = _wgrad_cols(n3, [du2], "wgrad_ffn2_up")[0]
    pair_ffn2, token = pair_start([("ffn2_w_gate", gw_g2), ("ffn2_w_up", gw_u2), ("ffn2_w_down", gw_d2)],
                                  corner(gw_u2), "ffn2")
    dn3 = _nt_panel([dg2, du2], [wg2, wu2], token, "ffn2_bwd_in")
    red_ffn2, token = scatter_start(pair_ffn2, dn3)
    dhs2, dm, d_ffn2 = _rms_bwd(dn3, hs2, ffn2_norm, dhs3, 1.0, "rms_bwd_ffn2")

    dy = _nt_panel([dm], [wout.reshape(1, D, D)], token, "mix_bwd_out")
    gw_out = _wgrad_out(y, dm)
    dz1, dcs, db, d_lg, d_lb, d_bcf = _mix_conv_bwd1(u, z1, dy, wsc_full, ln_cf_g, ln_cf_b)
    du, d_bin, d_wsc, d_wcf = _mix_conv_bwd2(u, dz1, dcs, db, wsc_full, wcf_full)
    gw_in = _wgrad_cols(n2, [du], "wgrad_w_in")[0]
    pair_mix, token = pair_start([("w_in", gw_in), ("w_out", gw_out)], corner(gw_in), "mix")
    dn2 = _nt_panel([du], [win], token, "mix_bwd_in")
    red_mix, token = scatter_start(pair_mix, dn2)
    dhs1, df1, d_mix = _rms_bwd(dn2, hs1, mix_norm, dhs2, FFN_RES_SCALE, "rms_bwd_mix")

    dg1, du1 = _ffn_bwd_act(df1, wd1.reshape(F, D), g1, u1, token, "ffn1_bwd_act")
    gw_d1 = _wgrad_down(a1, df1, "wgrad_ffn1_down")
    gw_g1 = _wgrad_cols(n1, [dg1], "wgrad_ffn1_gate")[0]
    pair_ffn1a, token = pair_start([("ffn1_w_down", gw_d1), ("ffn1_w_gate", gw_g1)], corner(gw_g1), "ffn1a")
    gw_u1 = _wgrad_cols(n1, [du1], "wgrad_ffn1_up", token)[0]
    red_ffn1a, token = scatter_start(pair_ffn1a, gw_u1)
    pair_ffn1b, token = pair_start([("ffn1_w_up", gw_u1)], token, "ffn1b")
    dn1 = _nt_panel([dg1, du1], [wg1, wu1], token, "ffn1_bwd_in")
    red_ffn1b, token = scatter_start(pair_ffn1b, dn1)
    grad_x, d_meta, d_ffn1 = _rms_bwd_first(dn1, hs0, ffn1_norm, dhs1, token)

    half_ffn2, tok = finish_sum(red_ffn2, grad_x)
    half_mix, tok = finish_sum(red_mix, tok)
    big_out = finish_adam(half_ffn2, tok)
    half_ffn1a, tok = finish_sum(red_ffn1a, big_out["ffn2_w_down"][1])
    big_out.update(finish_adam(half_mix, tok))
    half_ffn1b, tok = finish_sum(red_ffn1b, big_out["w_out"][1])
    big_out.update(finish_adam(half_ffn1a, tok))
    big_out.update(finish_adam(half_ffn1b, big_out["ffn1_w_gate"][1]))

    W = C1
    rows = lambda a: a.reshape(-1, W)
    parts = [rows(d_ffn1), rows(d_mix), rows(d_ffn2), rows(d_final), rows(d_bin), d_bcf, d_lg, d_lb,
             d_wsc, d_wcf, rows(d_meta), jnp.broadcast_to(loss_row[:, :1], (1, W))]
    sizes = [p.shape[0] for p in parts]
    total_rows = sum(sizes)
    packed = _pad_rows(jnp.concatenate(parts, axis=0), -(-total_rows // 8) * 8)
    summed = _share_small(packed, True, "sum_small", big_out["ffn1_w_up"][1])
    offs = [0]
    for n in sizes:
        offs.append(offs[-1] + n)
    piece = lambda k: summed[offs[k]:offs[k + 1]]
    loss = piece(11)[0, 0]
    g_ffn1, g_mix, g_ffn2 = (piece(k).reshape(1, D) for k in range(3))
    g_final = piece(3).reshape(1, D)
    g_bin = piece(4).reshape(1, -1)
    g_bcf, g_lg, g_lb = piece(5), piece(6), piece(7)
    g_wsc = lax.dynamic_slice_in_dim(piece(8), chip * cs, cs, axis=1)
    g_wcf = lax.dynamic_slice_in_dim(piece(9), chip * cs, cs, axis=1)
    g_meta = lax.dynamic_slice_in_dim(piece(10).reshape(N_META, D), chip * ms, ms, axis=1)

    small_names = ["meta_tokens", "ffn1_norm", "mix_norm", "b_in", "conv_sc_w", "conv_cf_w", "conv_cf_b", "ln_cf_g",
                   "ln_cf_b", "ffn2_norm", "final_norm"]
    small_w = [meta_tokens, ffn1_norm, mix_norm, b_in, conv_sc_w[0], conv_cf_w[0], conv_cf_b, ln_cf_g, ln_cf_b,
               ffn2_norm, final_norm.reshape(1, D)]
    small_g = [g_meta, g_ffn1, g_mix, g_bin, g_wsc, g_wcf, g_bcf, g_lg, g_lb, g_ffn2, g_final]
    small_m = [m_meta_tokens, m_ffn1_norm, m_mix_norm, m_b_in, m_conv_sc_w[0], m_conv_cf_w[0], m_conv_cf_b, m_ln_cf_g,
               m_ln_cf_b, m_ffn2_norm, m_final_norm.reshape(1, D)]
    small_v = [v_meta_tokens, v_ffn1_norm, v_mix_norm, v_b_in, v_conv_sc_w[0], v_conv_cf_w[0], v_conv_cf_b, v_ln_cf_g,
               v_ln_cf_b, v_ffn2_norm, v_final_norm.reshape(1, D)]
    s_d, s_m, s_v = _adamw_small(small_w, small_g, small_m, small_v)
    shapes = {"conv_sc_w": conv_sc_w.shape, "conv_cf_w": conv_cf_w.shape, "final_norm": final_norm.shape}
    small_out = {}
    for nm, g, d, m, v in zip(small_names, small_g, s_d, s_m, s_v):
        shp = shapes.get(nm, g.shape)
        small_out[nm] = tuple(t.reshape(shp) for t in (g, d, m, v))

    order = ["meta_tokens", "ffn1_norm", "ffn1_w_gate", "ffn1_w_up", "ffn1_w_down", "mix_norm", "w_in", "b_in",
             "conv_sc_w", "conv_cf_w", "conv_cf_b", "ln_cf_g", "ln_cf_b", "w_out", "ffn2_norm", "ffn2_w_gate",
             "ffn2_w_up", "ffn2_w_down", "final_norm"]
    res = {**big_out, **small_out}
    outs = [loss, grad_x[None]]
    for q in range(4):
        outs.extend(res[nm][q] for nm in order)
    return tuple(outs)
```

```python
import functools

import jax
import jax.numpy as jnp
from jax import lax
from jax.experimental import pallas as pl
from jax.experimental.pallas import tpu as pltpu

F32 = jnp.float32
BF16 = jnp.bfloat16
MESH = pl.DeviceIdType.MESH

N_META = 16
TT = 128
PAD = TT - N_META
HALO = 32
EPS = 1e-6
FFN_RES_SCALE = 0.5
N_CHIPS = 4
N_DEV = 8

ADAM_LR = 0.001
ADAM_B1 = 0.9
ADAM_B2 = 0.999
ADAM_EPS = 1e-08
ADAM_WD = 0.01
ADAM_STEP = 10

V7X_VMEM_BYTES = 64 * 2 ** 20
NT_DIMS = (((1,), (1,)), ((), ()))
TN_DIMS = (((0,), (0,)), ((), ()))


def _params(semantics, block_bytes):
    limit = min(2 * block_bytes + 16 * 2 ** 20, V7X_VMEM_BYTES - 6 * 2 ** 20)
    return pltpu.CompilerParams(dimension_semantics=semantics, vmem_limit_bytes=int(limit))


def _pallas(body, out_shape, **kw):
    if "grid" not in kw and "grid_spec" not in kw:
        return pl.pallas_call(body, out_shape=out_shape, **kw)
    big = lambda shape, dtype: jnp.issubdtype(dtype, jnp.floating) and len(shape) >= 2
    pin_out = lambda s: pltpu.HBM(s.shape, s.dtype) if big(s.shape, s.dtype) else s
    single = not isinstance(out_shape, (list, tuple))
    shapes = pin_out(out_shape) if single else [pin_out(s) for s in out_shape]
    call = pl.pallas_call(body, out_shape=shapes, **kw)
    pin = lambda a: pltpu.with_memory_space_constraint(a, pltpu.HBM) if big(a.shape, a.dtype) else a
    return lambda *operands: call(*[pin(a) for a in operands])


def _nbytes(shape, dtype):
    n = 1
    for d in shape:
        if d is not None:
            n *= d
    return n * jnp.dtype(dtype).itemsize


def _row_tile(rows, target, mult=8):
    best = None
    for t in range(mult, min(rows, target) + 1, mult):
        if rows % t == 0:
            best = t
    assert best is not None, (rows, target, mult)
    return best


def _sigmoid(v):
    return jax.nn.sigmoid(v)


def _dsilu(v, s):
    return s * (1.0 + v * (1.0 - s))


def _embed_rms(x2, meta, gain):
    S, D = x2.shape
    T = S + TT

    def body(x_ref, meta_ref, g_ref, hs_ref, n_ref):
        i = pl.program_id(0)

        @pl.when(i == 0)
        def _():
            hs_ref[...] = jnp.zeros_like(hs_ref)
            hs_ref[PAD:, :] = meta_ref[...]

        @pl.when(i > 0)
        def _():
            hs_ref[...] = x_ref[...]

        h = hs_ref[...]
        r = lax.rsqrt(jnp.mean(h * h, axis=-1, keepdims=True) + EPS)
        n_ref[...] = ((h * r) * g_ref[...]).astype(BF16)

    blk = _nbytes((TT, D), F32) * 2 + _nbytes((TT, D), BF16)
    return _pallas(
        body, name="embed_rms", grid=(T // TT,),
        in_specs=[pl.BlockSpec((TT, D), lambda i: (jnp.maximum(i - 1, 0), 0)),
                  pl.BlockSpec((N_META, D), lambda i: (0, 0)),
                  pl.BlockSpec((1, D), lambda i: (0, 0))],
        out_specs=[pl.BlockSpec((TT, D), lambda i: (i, 0)), pl.BlockSpec((TT, D), lambda i: (i, 0))],
        out_shape=[jax.ShapeDtypeStruct((T, D), F32), jax.ShapeDtypeStruct((T, D), BF16)],
        compiler_params=_params(("parallel",), blk),
    )(x2, meta, gain)


def _rms_bwd_math(dn, h, g):
    r = lax.rsqrt(jnp.mean(h * h, axis=-1, keepdims=True) + EPS)
    xh = h * r
    dgain = jnp.sum(dn * xh, axis=0, keepdims=True)
    dxh = dn * g
    dh = r * (dxh - xh * jnp.mean(dxh * xh, axis=-1, keepdims=True))
    return dh, dgain


def _rms_bwd(dn, hs, gain, dres, scale, name):
    T, D = hs.shape
    te = _row_tile(T, 384)

    def body(dn_ref, h_ref, g_ref, dres_ref, dhs_ref, dhb_ref, dg_ref):
        dh, dgain = _rms_bwd_math(dn_ref[...], h_ref[...], g_ref[...])
        d = dres_ref[...] + dh
        dhs_ref[...] = d
        dhb_ref[...] = (scale * d).astype(BF16)

        @pl.when(pl.program_id(0) == 0)
        def _():
            dg_ref[...] = jnp.zeros_like(dg_ref)

        dg_ref[...] += dgain

    blk = _nbytes((te, D), F32) * 4 + _nbytes((te, D), BF16)
    row = lambda i: (i, 0)
    return _pallas(
        body, name=name, grid=(T // te,),
        in_specs=[pl.BlockSpec((te, D), row), pl.BlockSpec((te, D), row), pl.BlockSpec((1, D), lambda i: (0, 0)),
                  pl.BlockSpec((te, D), row)],
        out_specs=[pl.BlockSpec((te, D), row), pl.BlockSpec((te, D), row), pl.BlockSpec((1, D), lambda i: (0, 0))],
        out_shape=[jax.ShapeDtypeStruct((T, D), F32), jax.ShapeDtypeStruct((T, D), BF16),
                   jax.ShapeDtypeStruct((1, D), F32)],
        compiler_params=_params(("arbitrary",), blk),
    )(dn, hs, gain, dres)


def _rms_bwd_first(dn, hs, gain, dres, after):
    T, D = hs.shape
    S = T - TT

    def body(dn_ref, h_ref, g_ref, dres_ref, after_ref, gx_ref, gm_ref, dg_ref):
        i = pl.program_id(0)
        dh, dgain = _rms_bwd_math(dn_ref[...], h_ref[...], g_ref[...])
        d = dres_ref[...] + dh

        @pl.when(i == 0)
        def _():
            dg_ref[...] = jnp.zeros_like(dg_ref)
            gm_ref[...] = d[PAD:, :]

        @pl.when(i > 0)
        def _():
            gx_ref[...] = d

        dg_ref[...] += dgain

    blk = _nbytes((TT, D), F32) * 4
    row = lambda i: (i, 0)
    return _pallas(
        body, name="rms_bwd_ffn1", grid=(T // TT,),
        in_specs=[pl.BlockSpec((TT, D), row), pl.BlockSpec((TT, D), row), pl.BlockSpec((1, D), lambda i: (0, 0)),
                  pl.BlockSpec((TT, D), row), TOKEN],
        out_specs=[pl.BlockSpec((TT, D), lambda i: (jnp.maximum(i - 1, 0), 0)),
                   pl.BlockSpec((N_META, D), lambda i: (0, 0)), pl.BlockSpec((1, D), lambda i: (0, 0))],
        out_shape=[jax.ShapeDtypeStruct((S, D), F32), jax.ShapeDtypeStruct((N_META, D), F32),
                   jax.ShapeDtypeStruct((1, D), F32)],
        compiler_params=_params(("arbitrary",), blk),
    )(dn, hs, gain, dres, after)


def _final_loss(hs, gain, tgt):
    T, D = hs.shape

    def body(h_ref, g_ref, t_ref, dhs_ref, dhb_ref, loss_ref, dg_ref):
        i = pl.program_id(0)
        h = h_ref[...]
        g = g_ref[...]
        r = lax.rsqrt(jnp.mean(h * h, axis=-1, keepdims=True) + EPS)
        xh = h * r
        e = jnp.where(i > 0, xh * g - t_ref[...], 0.0)
        tile_loss = jnp.sum(jnp.sum(e * e, axis=1, keepdims=True), axis=0, keepdims=True) * (0.5 / D)
        dout = e * (1.0 / D)
        dgain = jnp.sum(dout * xh, axis=0, keepdims=True)
        dxh = dout * g
        d = r * (dxh - xh * jnp.mean(dxh * xh, axis=-1, keepdims=True))
        dhs_ref[...] = d
        dhb_ref[...] = (FFN_RES_SCALE * d).astype(BF16)

        @pl.when(i == 0)
        def _():
            loss_ref[...] = jnp.zeros_like(loss_ref)
            dg_ref[...] = jnp.zeros_like(dg_ref)

        loss_ref[...] += jnp.broadcast_to(tile_loss, loss_ref.shape)
        dg_ref[...] += dgain

    blk = _nbytes((TT, D), F32) * 3 + _nbytes((TT, D), BF16)
    row = lambda i: (i, 0)
    return _pallas(
        body, name="final_loss", grid=(T // TT,),
        in_specs=[pl.BlockSpec((TT, D), row), pl.BlockSpec((1, D), lambda i: (0, 0)),
                  pl.BlockSpec((TT, D), lambda i: (jnp.maximum(i - 1, 0), 0))],
        out_specs=[pl.BlockSpec((TT, D), row), pl.BlockSpec((TT, D), row),
                   pl.BlockSpec((1, 128), lambda i: (0, 0)), pl.BlockSpec((1, D), lambda i: (0, 0))],
        out_shape=[jax.ShapeDtypeStruct((T, D), F32), jax.ShapeDtypeStruct((T, D), BF16),
                   jax.ShapeDtypeStruct((1, 128), F32), jax.ShapeDtypeStruct((1, D), F32)],
        compiler_params=_params(("arbitrary",), blk),
    )(hs, gain, tgt)


MXU_COLS = 256


def _tm(T):
    return _row_tile(T, 704, 16)


def _col_chunks(n):
    return [(c, min(MXU_COLS, n - c)) for c in range(0, n, MXU_COLS)]


TOKEN = pl.BlockSpec((8, 128), lambda *_: (0, 0))


def _ffn_up(n, wg, wu, shards, prev, after, name):
    T, D = n.shape
    Fs = wg.shape[2]
    tm = _tm(T)
    nprev = 0 if prev is None else 3

    def body(shards_ref, n_ref, wg_ref, wu_ref, after_ref, *refs):
        g_ref, u_ref, a_ref = refs[nprev:]
        nn = n_ref[...]
        for c0, cw in _col_chunks(Fs):
            if 2 * cw == MXU_COLS:
                both = jnp.concatenate([wg_ref[:, c0:c0 + cw], wu_ref[:, c0:c0 + cw]], axis=1)
                gu = jnp.dot(nn, both, preferred_element_type=F32)
                g, u = gu[:, :cw], gu[:, cw:]
            else:
                g = jnp.dot(nn, wg_ref[:, c0:c0 + cw], preferred_element_type=F32)
                u = jnp.dot(nn, wu_ref[:, c0:c0 + cw], preferred_element_type=F32)
            g_ref[:, c0:c0 + cw] = g.astype(BF16)
            u_ref[:, c0:c0 + cw] = u.astype(BF16)
            a_ref[:, c0:c0 + cw] = (jax.nn.silu(g) * u).astype(BF16)

    blk = _nbytes((tm, D), BF16) + 2 * _nbytes((D, Fs), BF16) + 3 * _nbytes((tm, Fs), BF16)
    out = pl.BlockSpec((tm, Fs), lambda j, i, p: (i, p[j]))
    shp = jax.ShapeDtypeStruct((T, N_CHIPS * Fs), BF16)
    return _pallas(
        body, name=name,
        grid_spec=pltpu.PrefetchScalarGridSpec(
            num_scalar_prefetch=1, grid=(shards.shape[0], T // tm),
            in_specs=[pl.BlockSpec((tm, D), lambda j, i, p: (i, 0)),
                      pl.BlockSpec((None, D, Fs), lambda j, i, p: (p[j], 0, 0)),
                      pl.BlockSpec((None, D, Fs), lambda j, i, p: (p[j], 0, 0)), TOKEN] + [ANY] * nprev,
            out_specs=[out, out, out]),
        out_shape=[shp, shp, shp], input_output_aliases={5 + q: q for q in range(nprev)},
        compiler_params=_params(("arbitrary", "arbitrary"), blk),
    )(shards, n, wg, wu, after, *(prev or ()))


def _ffn_down(a, wd, hs, shards, name, after=None, gain=None):
    T, F = a.shape
    _, Fs, D = wd.shape
    tm = _tm(T) if gain is None else _row_tile(T, 352, 16)
    tn = D // 2 if gain is None else D
    extra = [] if after is None else [after]
    gains = [] if gain is None else [gain]
    nk = shards.shape[0]
    nrest = len(extra) + len(gains)

    def body(shards_ref, a_ref, w_ref, h_ref, *refs):
        o_ref = refs[nrest]
        k = pl.program_id(2)
        part = FFN_RES_SCALE * jnp.dot(a_ref[...], w_ref[...], preferred_element_type=F32)

        @pl.when(k == 0)
        def _():
            o_ref[...] = h_ref[...] + part

        @pl.when(k > 0)
        def _():
            o_ref[...] += part

        if gains:
            @pl.when(k == nk - 1)
            def _():
                o = o_ref[...]
                r = lax.rsqrt(jnp.mean(o * o, axis=-1, keepdims=True) + EPS)
                refs[nrest + 1][...] = ((o * r) * refs[nrest - 1][...]).astype(BF16)

    blk = _nbytes((tm, Fs), BF16) + _nbytes((Fs, tn), BF16) + 3 * _nbytes((tm, tn), F32) + _nbytes((tm, tn), BF16)
    tile = pl.BlockSpec((tm, tn), lambda n, i, k, p: (i, n))
    res = _pallas(
        body, name=name,
        grid_spec=pltpu.PrefetchScalarGridSpec(
            num_scalar_prefetch=1, grid=(D // tn, T // tm, nk),
            in_specs=[pl.BlockSpec((tm, Fs), lambda n, i, k, p: (i, p[k])),
                      pl.BlockSpec((None, Fs, tn), lambda n, i, k, p: (p[k], 0, n)), tile]
                     + [TOKEN] * len(extra) + [pl.BlockSpec((1, D), lambda n, i, k, p: (0, 0))] * len(gains),
            out_specs=[tile] * (1 + len(gains))),
        out_shape=[jax.ShapeDtypeStruct((T, D), F32)] + [jax.ShapeDtypeStruct((T, D), BF16)] * len(gains),
        compiler_params=_params(("parallel", "parallel", "arbitrary"), blk),
    )(shards, a, wd, hs, *extra, *gains)
    return res[0] if gain is None else tuple(res)


def _ffn_down_whole(a, wd, hs, name):
    T, F = a.shape
    D = wd.shape[1]
    tm = _tm(T)
    tn = D // 4

    def body(a_ref, w_ref, h_ref, o_ref):
        o_ref[...] = h_ref[...] + FFN_RES_SCALE * jnp.dot(a_ref[...], w_ref[...], preferred_element_type=F32)

    blk = _nbytes((tm, F), BF16) + _nbytes((F, tn), BF16) + 3 * _nbytes((tm, tn), F32)
    return _pallas(
        body, name=name, grid=(T // tm, D // tn),
        in_specs=[pl.BlockSpec((tm, F), lambda i, n: (i, 0)), pl.BlockSpec((F, tn), lambda i, n: (0, n)),
                  pl.BlockSpec((tm, tn), lambda i, n: (i, n))],
        out_specs=pl.BlockSpec((tm, tn), lambda i, n: (i, n)),
        out_shape=jax.ShapeDtypeStruct((T, D), F32),
        compiler_params=_params(("parallel", "parallel"), blk),
    )(a, wd, hs)


def _mix_in(n, w, b):
    T, D = n.shape
    Ns = w.shape[2]
    tm = _tm(T)

    def body(n_ref, w_ref, b_ref, u_ref):
        u_ref[...] = jnp.dot(n_ref[...], w_ref[...], preferred_element_type=F32) + b_ref[...]

    blk = _nbytes((tm, D), BF16) + _nbytes((D, Ns), BF16) + 2 * _nbytes((tm, Ns), F32)
    return _pallas(
        body, name="mix_in", grid=(N_CHIPS, T // tm),
        in_specs=[pl.BlockSpec((tm, D), lambda j, i: (i, 0)), pl.BlockSpec((None, D, Ns), lambda j, i: (j, 0, 0)),
                  pl.BlockSpec((1, Ns), lambda j, i: (0, j))],
        out_specs=pl.BlockSpec((tm, Ns), lambda j, i: (i, j)),
        out_shape=jax.ShapeDtypeStruct((T, N_CHIPS * Ns), F32),
        compiler_params=_params(("parallel", "parallel"), blk),
    )(n, w, b)


def _mix_out(y, w, hs, gain, after):
    T, D = y.shape
    tm = _row_tile(T, 352, 16)

    def body(y_ref, w_ref, h_ref, g_ref, after_ref, o_ref, n_ref):
        o = h_ref[...] + jnp.dot(y_ref[...], w_ref[...], preferred_element_type=F32)
        o_ref[...] = o
        r = lax.rsqrt(jnp.mean(o * o, axis=-1, keepdims=True) + EPS)
        n_ref[...] = ((o * r) * g_ref[...]).astype(BF16)

    blk = 2 * _nbytes((tm, D), BF16) + _nbytes((D, D), BF16) + 3 * _nbytes((tm, D), F32)
    row = pl.BlockSpec((tm, D), lambda i: (i, 0))
    return _pallas(
        body, name="mix_out", grid=(T // tm,),
        in_specs=[row, pl.BlockSpec((D, D), lambda i: (0, 0)), row, pl.BlockSpec((1, D), lambda i: (0, 0)), TOKEN],
        out_specs=[row, row],
        out_shape=[jax.ShapeDtypeStruct((T, D), F32), jax.ShapeDtypeStruct((T, D), BF16)],
        compiler_params=_params(("parallel",), blk),
    )(y, w, hs, gain, after)


def _ffn_bwd_act(dfb, wd, g, u, after, name):
    T, D = dfb.shape
    F = wd.shape[0]
    tm = _row_tile(T, 1408, 16)
    tn = 2 * MXU_COLS

    tr = _row_tile(tm, 352, 16)

    def body(d_ref, w_ref, g_ref, u_ref, after_ref, dg_ref, du_ref):
        for r0 in range(0, tm, tr):
            dv = d_ref[r0:r0 + tr, :]
            for c0, cw in _col_chunks(tn):
                da = lax.dot_general(dv, w_ref[c0:c0 + cw, :], NT_DIMS, preferred_element_type=F32)
                gv = g_ref[r0:r0 + tr, c0:c0 + cw].astype(F32)
                uv = u_ref[r0:r0 + tr, c0:c0 + cw].astype(F32)
                s = _sigmoid(gv)
                du_ref[r0:r0 + tr, c0:c0 + cw] = (da * (gv * s)).astype(BF16)
                dg_ref[r0:r0 + tr, c0:c0 + cw] = (da * uv * _dsilu(gv, s)).astype(BF16)

    blk = _nbytes((tm, D), BF16) + _nbytes((tn, D), BF16) + 4 * _nbytes((tm, tn), BF16)
    io = pl.BlockSpec((tm, tn), lambda i, n: (i, n))
    shp = jax.ShapeDtypeStruct((T, F), BF16)
    return _pallas(
        body, name=name, grid=(T // tm, F // tn),
        in_specs=[pl.BlockSpec((tm, D), lambda i, n: (i, 0)), pl.BlockSpec((tn, D), lambda i, n: (n, 0)), io, io, TOKEN],
        out_specs=[io, io], out_shape=[shp, shp],
        compiler_params=_params(("parallel", "parallel"), blk),
    )(dfb, wd, g, u, after)


def _nt_panel(lhs_list, w_list, after, name):
    T = lhs_list[0].shape[0]
    nsh, Dout, Ks = w_list[0].shape
    npair = len(lhs_list)
    tm = _row_tile(T, 1408, 16)
    tn = Dout // 2

    def body(*refs):
        l_refs, w_refs, o_ref = refs[:npair], refs[npair:2 * npair], refs[2 * npair + 1]
        j = pl.program_id(2)
        k0 = Ks - Ks % MXU_COLS if npair == 2 and 2 * (Ks % MXU_COLS) == MXU_COLS else Ks
        acc = None
        for p in range(npair):
            part = lax.dot_general(l_refs[p][:, :k0], w_refs[p][:, :k0], NT_DIMS, preferred_element_type=F32)
            acc = part if acc is None else acc + part
        if k0 < Ks:
            lhs = jnp.concatenate([l_refs[p][:, k0:] for p in range(npair)], axis=1)
            rhs = jnp.concatenate([w_refs[p][:, k0:] for p in range(npair)], axis=1)
            acc = acc + lax.dot_general(lhs, rhs, NT_DIMS, preferred_element_type=F32)

        @pl.when(j == 0)
        def _():
            o_ref[...] = acc

        @pl.when(j > 0)
        def _():
            o_ref[...] += acc

    blk = npair * (_nbytes((tm, Ks), BF16) + _nbytes((tn, Ks), BF16)) + 2 * _nbytes((tm, tn), F32)
    return _pallas(
        body, name=name, grid=(Dout // tn, T // tm, nsh),
        in_specs=[pl.BlockSpec((tm, Ks), lambda n, i, j: (i, j))] * npair
                 + [pl.BlockSpec((None, tn, Ks), lambda n, i, j: (j, n, 0))] * npair + [TOKEN],
        out_specs=pl.BlockSpec((tm, tn), lambda n, i, j: (i, n)),
        out_shape=jax.ShapeDtypeStruct((T, Dout), F32),
        compiler_params=_params(("parallel", "parallel", "arbitrary"), blk),
    )(*lhs_list, *w_list, after)


def _tn_call(name, grid, lhs, lhs_spec, rhs_list, rhs_specs, out_shapes, out_specs, blk, after=None):
    nr = len(rhs_list)
    extra = [] if after is None else [after]

    def body(*refs):
        l_ref, r_refs, o_refs = refs[0], refs[1:1 + nr], refs[len(refs) - nr:]
        k = pl.program_id(len(grid) - 1)
        lv = l_ref[...]
        for q in range(nr):
            part = lax.dot_general(lv, r_refs[q][...], TN_DIMS, preferred_element_type=F32)
            part = part.reshape(o_refs[q].shape)

            @pl.when(k == 0)
            def _(o=o_refs[q], part=part):
                o[...] = part

            @pl.when(k > 0)
            def _(o=o_refs[q], part=part):
                o[...] += part

    return _pallas(
        body, name=name, grid=grid, in_specs=[lhs_spec] + rhs_specs + [TOKEN] * len(extra), out_specs=out_specs,
        out_shape=out_shapes, compiler_params=_params(("parallel",) * (len(grid) - 1) + ("arbitrary",), blk),
    )(lhs, *rhs_list, *extra)


def _tk(T):
    return T


def _wgrad_cols(n, rhs_list, name, after=None):
    T, D = n.shape
    Ns = rhs_list[0].shape[1] // N_CHIPS
    tk = _tk(T)
    nr = len(rhs_list)
    tm = D // 4
    blk = _nbytes((tk, tm), BF16) + nr * (_nbytes((tk, Ns), BF16) + 2 * _nbytes((tm, Ns), F32))
    return _tn_call(
        name, (N_CHIPS, D // tm, T // tk), n, pl.BlockSpec((tk, tm), lambda j, m, k: (k, m)),
        rhs_list, [pl.BlockSpec((tk, Ns), lambda j, m, k: (k, j))] * nr,
        [jax.ShapeDtypeStruct((N_CHIPS, 2, D // 2, Ns), F32)] * nr,
        [pl.BlockSpec((None, None, tm, Ns), lambda j, m, k: (j, m // 2, m % 2, 0))] * nr, blk, after)


def _wgrad_down(a, dfb, name):
    T, F = a.shape
    D = dfb.shape[1]
    Fs = F // N_CHIPS
    tk = _tk(T)
    tn = D // 4
    blk = _nbytes((tk, Fs), BF16) + _nbytes((tk, tn), BF16) + 2 * _nbytes((Fs, tn), F32)
    return _tn_call(
        name, (N_CHIPS, D // tn, T // tk), a, pl.BlockSpec((tk, Fs), lambda j, n, k: (k, j)),
        [dfb], [pl.BlockSpec((tk, tn), lambda j, n, k: (k, n))],
        [jax.ShapeDtypeStruct((N_CHIPS, 2, Fs // 2, D), F32)],
        [pl.BlockSpec((None, 2, Fs // 2, tn), lambda j, n, k: (j, 0, 0, n))], blk)[0]


def _wgrad_out(y, dmb):
    T, D = y.shape
    tk = _tk(T)
    tn = D // 2
    rows = D // (2 * N_CHIPS)
    blk = _nbytes((tk, D // 2), BF16) + _nbytes((tk, tn), BF16) + 2 * _nbytes((D // 2, tn), F32)
    return _tn_call(
        "wgrad_w_out", (2, D // tn, T // tk), y, pl.BlockSpec((tk, D // 2), lambda m, n, k: (k, m)),
        [dmb], [pl.BlockSpec((tk, tn), lambda m, n, k: (k, n))],
        [jax.ShapeDtypeStruct((N_CHIPS, 2, rows, D), F32)],
        [pl.BlockSpec((2, 2, rows, tn), lambda m, n, k: (m, 0, 0, n))], blk)[0]


def _row_masks(i, last):
    rows = i * TT + lax.broadcasted_iota(jnp.int32, (TT, 1), 0)
    prows = i * TT - HALO + lax.broadcasted_iota(jnp.int32, (HALO, 1), 0)
    return rows >= PAD, (prows >= PAD) & (i > 0), i < last


def _conv_inputs(u, up, mask_c, mask_p, zbuf, pbuf, C1):
    b, c, v, a, g = (u[:, k * C1:(k + 1) * C1] for k in range(5))
    cp, vp, ap, gp = (up[:, k * C1:(k + 1) * C1] for k in range(1, 5))
    sg = _sigmoid(g)
    pbuf[0:HALO, :] = jnp.where(mask_p, cp * vp, 0.0)
    pbuf[HALO:, :] = jnp.where(mask_c, c * v, 0.0)
    if zbuf is not None:
        zbuf[0:HALO, :] = jnp.where(mask_p, ap * _sigmoid(gp), 0.0)
        zbuf[HALO:, :] = jnp.where(mask_c, a * sg, 0.0)
    return b, c, v, a, sg


SUBLANES = 8
SHIFT_ROWS = TT + HALO - SUBLANES


def _shifted_scratch(C1):
    return pltpu.VMEM((SUBLANES - 1, SHIFT_ROWS, C1), F32)


def _fill_shifted(buf, sh):
    for r in range(1, SUBLANES):
        sh[r - 1] = buf[r:r + SHIFT_ROWS, :]


LANES = 128


def _window(buf, sh, lo, c0):
    if sh is None or lo % SUBLANES == 0:
        return buf[lo:lo + TT, c0:c0 + LANES]
    q, r = divmod(lo, SUBLANES)
    return sh[r - 1, q * SUBLANES:q * SUBLANES + TT, c0:c0 + LANES]


def _tap_sum(w_ref, buf, sh, starts):
    chunks = []
    for c0 in range(0, buf.shape[1], LANES):
        acc = None
        for k, lo in enumerate(starts):
            term = w_ref[k:k + 1, c0:c0 + LANES] * _window(buf, sh, lo, c0)
            acc = term if acc is None else acc + term
        chunks.append(acc)
    return jnp.concatenate(chunks, axis=1)


def _causal_conv(w_ref, buf, sh=None):
    K = w_ref.shape[0]
    return _tap_sum(w_ref, buf, sh, [HALO - (K - 1) + k for k in range(K)])


def _anticausal_conv(w_ref, buf, sh=None):
    K = w_ref.shape[0]
    return _tap_sum(w_ref, buf, sh, [K - 1 - k for k in range(K)])


def _conv_weight_sums(dw_ref, dy, buf, sh=None):
    K = dw_ref.shape[0]
    for c0 in range(0, buf.shape[1], LANES):
        dyc = dy[:, c0:c0 + LANES]
        for k in range(K):
            prod = dyc * _window(buf, sh, HALO - (K - 1) + k, c0)
            dw_ref[k:k + 1, c0:c0 + LANES] += jnp.sum(prod, axis=0, keepdims=True)


def _layernorm_stats(z1):
    mu = jnp.mean(z1, axis=-1, keepdims=True)
    zc = z1 - mu
    rs = lax.rsqrt(jnp.mean(zc * zc, axis=-1, keepdims=True) + EPS)
    return zc * rs, rs


def _mixer_specs(T, DIN, C1, ksc, kcf):
    cur = pl.BlockSpec((TT, DIN), lambda i: (i, 0))
    prev = pl.BlockSpec((HALO, DIN), lambda i: (jnp.maximum(i * (TT // HALO) - 1, 0), 0))
    full = lambda r: pl.BlockSpec((r, C1), lambda i: (0, 0))
    return cur, prev, [full(ksc), full(kcf), full(1), full(1), full(1)]


def _mix_conv_fwd(u, wsc, wcf, bcf, lg, lb):
    T, DIN = u.shape
    C1 = DIN // 5
    last = T // TT - 1

    def body(u_ref, up_ref, wsc_ref, wcf_ref, bcf_ref, lg_ref, lb_ref, y_ref, z1_ref, zbuf, pbuf, zsh):
        i = pl.program_id(0)
        mask_c, mask_p, _ = _row_masks(i, last)
        b, _, _, _, _ = _conv_inputs(u_ref[...], up_ref[...], mask_c, mask_p, zbuf, pbuf, C1)
        _fill_shifted(zbuf, zsh)
        cs = _causal_conv(wsc_ref, pbuf)
        z1 = _causal_conv(wcf_ref, zbuf, zsh) + bcf_ref[...]
        z1_ref[...] = z1
        zh, _ = _layernorm_stats(z1)
        ln = zh * lg_ref[...] + lb_ref[...]
        y_ref[:, 0:C1] = jnp.where(mask_c, b * cs, 0.0).astype(BF16)
        y_ref[:, C1:] = jnp.where(mask_c, jax.nn.silu(ln), 0.0).astype(BF16)

    cur, prev, small = _mixer_specs(T, DIN, C1, wsc.shape[0], wcf.shape[0])
    blk = _nbytes((TT + HALO, DIN), F32) + _nbytes((TT, 2 * C1), BF16) + 12 * _nbytes((TT + HALO, C1), F32)
    return _pallas(
        body, name="mix_conv_fwd", grid=(T // TT,),
        in_specs=[cur, prev] + small,
        out_specs=[pl.BlockSpec((TT, 2 * C1), lambda i: (i, 0)), pl.BlockSpec((TT, C1), lambda i: (i, 0))],
        out_shape=[jax.ShapeDtypeStruct((T, 2 * C1), BF16), jax.ShapeDtypeStruct((T, C1), F32)],
        scratch_shapes=[pltpu.VMEM((TT + HALO, C1), F32), pltpu.VMEM((TT + HALO, C1), F32), _shifted_scratch(C1)],
        compiler_params=_params(("arbitrary",), blk),
    )(u, u, wsc, wcf, bcf, lg, lb)


def _mix_conv_bwd1(u, z1, dy, wsc, lg, lb):
    T, DIN = u.shape
    C1 = DIN // 5
    last = T // TT - 1

    def body(u_ref, up_ref, z1_ref, dy_ref, wsc_ref, lg_ref, lb_ref,
             dz1_ref, dcs_ref, db_ref, dlg_ref, dlb_ref, dbcf_ref, pbuf):
        i = pl.program_id(0)
        mask_c, mask_p, _ = _row_masks(i, last)
        b, _, _, _, _ = _conv_inputs(u_ref[...], up_ref[...], mask_c, mask_p, None, pbuf, C1)
        cs = _causal_conv(wsc_ref, pbuf)
        zh, rs = _layernorm_stats(z1_ref[...])
        ln = zh * lg_ref[...] + lb_ref[...]
        dy = dy_ref[...]
        dysc = jnp.where(mask_c, dy[:, 0:C1], 0.0)
        dycf = jnp.where(mask_c, dy[:, C1:], 0.0)
        db_ref[...] = (dysc * cs).astype(BF16)
        dcs_ref[...] = dysc * b
        dl = dycf * _dsilu(ln, _sigmoid(ln))
        dzh = dl * lg_ref[...]
        dz1 = rs * (dzh - jnp.mean(dzh, axis=-1, keepdims=True) - zh * jnp.mean(dzh * zh, axis=-1, keepdims=True))
        dz1_ref[...] = dz1

        @pl.when(i == 0)
        def _():
            dlg_ref[...] = jnp.zeros_like(dlg_ref)
            dlb_ref[...] = jnp.zeros_like(dlb_ref)
            dbcf_ref[...] = jnp.zeros_like(dbcf_ref)

        dlg_ref[...] += jnp.sum(dl * zh, axis=0, keepdims=True)
        dlb_ref[...] += jnp.sum(dl, axis=0, keepdims=True)
        dbcf_ref[...] += jnp.sum(dz1, axis=0, keepdims=True)

    cur, prev, small = _mixer_specs(T, DIN, C1, wsc.shape[0], 1)
    tile = lambda: pl.BlockSpec((TT, C1), lambda i: (i, 0))
    vec = lambda: pl.BlockSpec((1, C1), lambda i: (0, 0))
    blk = _nbytes((TT + HALO, DIN), F32) + 5 * _nbytes((TT, C1), F32) + 12 * _nbytes((TT + HALO, C1), F32)
    return _pallas(
        body, name="mix_conv_bwd1", grid=(T // TT,),
        in_specs=[cur, prev, tile(), pl.BlockSpec((TT, 2 * C1), lambda i: (i, 0)), small[0], small[3], small[4]],
        out_specs=[tile(), tile(), tile(), vec(), vec(), vec()],
        out_shape=[jax.ShapeDtypeStruct((T, C1), F32), jax.ShapeDtypeStruct((T, C1), F32),
                   jax.ShapeDtypeStruct((T, C1), BF16)] + [jax.ShapeDtypeStruct((1, C1), F32)] * 3,
        scratch_shapes=[pltpu.VMEM((TT + HALO, C1), F32)],
        compiler_params=_params(("arbitrary",), blk),
    )(u, u, z1, dy, wsc, lg, lb)


def _mix_conv_bwd2(u, dz1, dcs, db, wsc, wcf):
    T, DIN = u.shape
    C1 = DIN // 5
    last = T // TT - 1
    ksc, kcf = wsc.shape[0], wcf.shape[0]

    def body(u_ref, up_ref, dz_ref, dzn_ref, dc_ref, dcn_ref, db_ref, wsc_ref, wcf_ref,
             du_ref, dbin_ref, dwsc_ref, dwcf_ref, zbuf, pbuf, dzbuf, dcbuf, zsh, dzsh):
        i = pl.program_id(0)
        mask_c, mask_p, has_next = _row_masks(i, last)
        _, c, v, a, sg = _conv_inputs(u_ref[...], up_ref[...], mask_c, mask_p, zbuf, pbuf, C1)
        dz1 = dz_ref[...]
        dcs = dc_ref[...]
        dzbuf[0:TT, :] = dz1
        dzbuf[TT:, :] = jnp.where(has_next, dzn_ref[...], 0.0)
        dcbuf[0:TT, :] = dcs
        dcbuf[TT:, :] = jnp.where(has_next, dcn_ref[...], 0.0)

        @pl.when(i == 0)
        def _():
            dbin_ref[...] = jnp.zeros_like(dbin_ref)
            dwsc_ref[...] = jnp.zeros_like(dwsc_ref)
            dwcf_ref[...] = jnp.zeros_like(dwcf_ref)

        _fill_shifted(zbuf, zsh)
        _fill_shifted(dzbuf, dzsh)
        _conv_weight_sums(dwcf_ref, dz1, zbuf, zsh)
        _conv_weight_sums(dwsc_ref, dcs, pbuf)
        dz0 = jnp.where(mask_c, _anticausal_conv(wcf_ref, dzbuf, dzsh), 0.0)
        dp = jnp.where(mask_c, _anticausal_conv(wsc_ref, dcbuf), 0.0)
        parts = (db_ref[...].astype(F32), dp * v, dp * c, dz0 * sg, dz0 * a * sg * (1.0 - sg))
        for k, part in enumerate(parts):
            du_ref[:, k * C1:(k + 1) * C1] = part.astype(BF16)
            dbin_ref[:, k * C1:(k + 1) * C1] += jnp.sum(part, axis=0, keepdims=True)

    cur, prev, small = _mixer_specs(T, DIN, C1, ksc, kcf)
    tile = lambda: pl.BlockSpec((TT, C1), lambda i: (i, 0))
    nxt = lambda: pl.BlockSpec((HALO, C1), lambda i: (jnp.minimum((i + 1) * (TT // HALO), T // HALO - 1), 0))
    blk = (_nbytes((TT + HALO, DIN), F32) + _nbytes((TT, DIN), BF16) + 5 * _nbytes((TT, C1), F32)
           + 16 * _nbytes((TT + HALO, C1), F32))
    buf = lambda: pltpu.VMEM((TT + HALO, C1), F32)
    return _pallas(
        body, name="mix_conv_bwd2", grid=(T // TT,),
        in_specs=[cur, prev, tile(), nxt(), tile(), nxt(), tile(), small[0], small[1]],
        out_specs=[pl.BlockSpec((TT, DIN), lambda i: (i, 0)), pl.BlockSpec((1, DIN), lambda i: (0, 0)),
                   pl.BlockSpec((ksc, C1), lambda i: (0, 0)), pl.BlockSpec((kcf, C1), lambda i: (0, 0))],
        out_shape=[jax.ShapeDtypeStruct((T, DIN), BF16), jax.ShapeDtypeStruct((1, DIN), F32),
                   jax.ShapeDtypeStruct((ksc, C1), F32), jax.ShapeDtypeStruct((kcf, C1), F32)],
        scratch_shapes=[buf(), buf(), buf(), buf(), _shifted_scratch(C1), _shifted_scratch(C1)],
        compiler_params=_params(("arbitrary",), blk),
    )(u, u, dz1, dz1, dcs, dcs, db, wsc, wcf)


def _place():
    x, y, c = lax.axis_index("x"), lax.axis_index("y"), lax.axis_index("c")
    chips = [(1 - x, y), (x, 1 - y), (1 - x, 1 - y)]
    return x, y, c, chips


ANY = pl.BlockSpec(memory_space=pl.ANY)


def _cast_own_block(place, w, name):
    R, C = w.shape
    tr = _row_tile(R // 2, 1024, 16)
    nblk = R // 2 // tr

    def body(place_ref, w_ref, o_ref):
        o_ref[...] = w_ref[...].astype(BF16)

    return _pallas(
        body, name=name,
        grid_spec=pltpu.PrefetchScalarGridSpec(
            num_scalar_prefetch=1, grid=(2, nblk),
            in_specs=[pl.BlockSpec((tr, C), lambda h, i, p: (h * nblk + i, 0))],
            out_specs=pl.BlockSpec((None, None, tr, C), lambda h, i, p: (p[0], h, i, 0))),
        out_shape=jax.ShapeDtypeStruct((N_CHIPS, 2, R // 2, C), BF16),
        compiler_params=_params(("parallel", "parallel"), _nbytes((tr, C), F32) + _nbytes((tr, C), BF16)),
    )(place, w)


HBM = pl.BlockSpec(memory_space=pltpu.HBM)
SEM = pl.BlockSpec(memory_space=pltpu.SEMAPHORE)
EFFECT = pltpu.SideEffectType.DATAFLOW_SIDE_EFFECTING


def _gather_copies(refs, send, recv, rels=(0, 1, 2)):
    x, y, c, chips = _place()
    s = 2 * x + y
    n = len(rels)
    return [pltpu.make_async_remote_copy(src_ref=ref.at[s, c], dst_ref=ref.at[s, c], send_sem=send.at[n * w + k],
                                         recv_sem=recv.at[n * w + k], device_id=(*chips[r], c), device_id_type=MESH)
            for w, ref in enumerate(refs) for k, r in enumerate(rels)]


def _scatter_copies(refs, send, recv):
    x, y, c, chips = _place()
    nw = len(refs) // 2
    return [pltpu.make_async_remote_copy(src_ref=refs[w].at[2 * tx + ty], dst_ref=refs[nw + w].at[r],
                                         send_sem=send.at[3 * w + r], recv_sem=recv.at[3 * w + r],
                                         device_id=(tx, ty, c), device_id_type=MESH)
            for w in range(nw) for r, (tx, ty) in enumerate(chips)]


def _pair_copies(refs, send, recv):
    x, y, c, _ = _place()
    nw = len(refs) // 2
    return [pltpu.make_async_remote_copy(src_ref=refs[w].at[j, 1 - c], dst_ref=refs[nw + w].at[j],
                                         send_sem=send.at[N_CHIPS * w + j], recv_sem=recv.at[N_CHIPS * w + j],
                                         device_id=(x, y, 1 - c), device_id_type=MESH)
            for w in range(nw) for j in range(N_CHIPS)]


def _forward_copies(refs, send, recv, rels=(0, 1, 2)):
    x, y, c, chips = _place()
    n = len(rels)
    copies = []
    for w, ref in enumerate(refs):
        for k, r in enumerate(rels):
            tx, ty = chips[r]
            blk = ref.at[2 * tx + ty, c]
            copies.append(pltpu.make_async_remote_copy(src_ref=blk, dst_ref=blk, send_sem=send.at[n * w + k],
                                                       recv_sem=recv.at[n * w + k], device_id=(x, y, 1 - c),
                                                       device_id_type=MESH))
    return copies


def _half_copies(refs, send, recv):
    x, y, c, _ = _place()
    return [pltpu.make_async_remote_copy(src_ref=ref.at[c], dst_ref=ref.at[c], send_sem=send.at[w], recv_sem=recv.at[w],
                                         device_id=(x, y, 1 - c), device_id_type=MESH)
            for w, ref in enumerate(refs)]


def _start_copies(bufs, after, ncopies, make_copies, name):
    n = len(bufs)

    def body(*refs):
        in_refs, send, recv, token = refs[:n], refs[n + 1], refs[n + 2], refs[2 * n + 3]
        for cp in make_copies(in_refs, send, recv):
            cp.start()
        token[...] = jnp.zeros_like(token)

    outs = _pallas(
        body, name=name, in_specs=[HBM] * n + [ANY],
        out_specs=[SEM, SEM] + [HBM] * n + [pl.BlockSpec(memory_space=pltpu.VMEM)],
        out_shape=[pltpu.SemaphoreType.DMA((ncopies,)), pltpu.SemaphoreType.DMA((ncopies,))]
                  + [pltpu.HBM(b.shape, b.dtype) for b in bufs] + [jax.ShapeDtypeStruct((8, 128), F32)],
        input_output_aliases={k: 2 + k for k in range(n)},
        compiler_params=pltpu.CompilerParams(has_side_effects=EFFECT),
    )(*[pltpu.with_memory_space_constraint(b, pltpu.HBM) for b in bufs], after)
    return outs[0], outs[1], list(outs[2:2 + n]), outs[2 + n]


def _wait_copies(send, recv, bufs, after, make_copies, name):
    n = len(bufs)

    def body(*refs):
        in_refs, send_ref, recv_ref = refs[:n], refs[n], refs[n + 1]
        for cp in make_copies(in_refs, send_ref, recv_ref):
            cp.wait_send()
            cp.wait_recv()

    outs = _pallas(
        body, name=name, in_specs=[HBM] * n + [SEM, SEM, ANY], out_specs=[HBM] * n,
        out_shape=[pltpu.HBM(b.shape, b.dtype) for b in bufs],
        input_output_aliases={k: k for k in range(n)},
        compiler_params=pltpu.CompilerParams(has_side_effects=EFFECT),
    )(*bufs, send, recv, after)
    return list(outs)


def _forward_halves(bufs, name, rels=(0, 1, 2)):
    nw = len(bufs)
    n = len(rels)

    def body(*refs):
        o_refs = refs[nw:2 * nw]
        send, recv = refs[2 * nw:]
        x, y, c, chips = _place()
        sib = (x, y, 1 - c)
        copies = []
        for w in range(nw):
            for k, r in enumerate(rels):
                tx, ty = chips[r]
                ref = o_refs[w].at[2 * tx + ty, c]
                cp = pltpu.make_async_remote_copy(src_ref=ref, dst_ref=ref, send_sem=send.at[n * w + k],
                                                  recv_sem=recv.at[n * w + k], device_id=sib, device_id_type=MESH)
                cp.start()
                copies.append(cp)
        for w in range(nw):
            for k, r in enumerate(rels):
                tx, ty = chips[r]
                ref = o_refs[w].at[2 * tx + ty, 1 - c]
                pltpu.make_async_remote_copy(src_ref=ref, dst_ref=ref, send_sem=send.at[n * w + k],
                                             recv_sem=recv.at[n * w + k], device_id=sib, device_id_type=MESH).wait_recv()
        for cp in copies:
            cp.wait_send()

    return _pallas(
        body, name=name, in_specs=[ANY] * nw, out_specs=[ANY] * nw,
        out_shape=[jax.ShapeDtypeStruct(b.shape, b.dtype) for b in bufs],
        input_output_aliases={w: w for w in range(nw)},
        scratch_shapes=[pltpu.SemaphoreType.DMA((n * nw,)), pltpu.SemaphoreType.DMA((n * nw,))],
    )(*bufs)


def _share_small(v, reduce, name, after):
    R, C = v.shape

    def body(v_ref, after_ref, o_ref, *scratch):
        if reduce:
            all_ref, send, recv, lsem = scratch
        else:
            all_ref = o_ref
            send, recv, lsem = scratch
        x, y, c, _ = _place()
        me = 4 * x + 2 * y + c
        loc = pltpu.make_async_copy(v_ref, all_ref.at[me], lsem)
        loc.start()
        copies = []
        for k in range(1, N_DEV):
            kx, ky, kc = (k >> 2) & 1, (k >> 1) & 1, k & 1
            peer = (x ^ kx, y ^ ky, c ^ kc)
            cp = pltpu.make_async_remote_copy(src_ref=v_ref, dst_ref=all_ref.at[me], send_sem=send.at[k - 1],
                                              recv_sem=recv.at[k - 1], device_id=peer, device_id_type=MESH)
            cp.start()
            copies.append(cp)
        for k in range(1, N_DEV):
            kx, ky, kc = (k >> 2) & 1, (k >> 1) & 1, k & 1
            src = 4 * (x ^ kx) + 2 * (y ^ ky) + (c ^ kc)
            pltpu.make_async_remote_copy(src_ref=v_ref, dst_ref=all_ref.at[src], send_sem=send.at[k - 1],
                                         recv_sem=recv.at[k - 1], device_id=(x, y, c), device_id_type=MESH).wait_recv()
        for cp in copies:
            cp.wait_send()
        loc.wait()
        if reduce:
            total = all_ref[0]
            for d in range(1, N_DEV):
                total = total + all_ref[d]
            o_ref[...] = total

    vm = pl.BlockSpec(memory_space=pltpu.VMEM)
    sems = [pltpu.SemaphoreType.DMA((N_DEV - 1,)), pltpu.SemaphoreType.DMA((N_DEV - 1,)), pltpu.SemaphoreType.DMA]
    if reduce:
        out_shape = jax.ShapeDtypeStruct((R, C), F32)
        scratch = [pltpu.VMEM((N_DEV, R, C), F32)] + sems
    else:
        out_shape = jax.ShapeDtypeStruct((N_DEV, R, C), F32)
        scratch = sems
    return _pallas(
        body, name=name, in_specs=[vm, ANY], out_specs=vm, out_shape=out_shape, scratch_shapes=scratch,
        compiler_params=pltpu.CompilerParams(vmem_limit_bytes=int(min(4 * N_DEV * R * C * 4 + 2 ** 24, 2 ** 25 + 2 ** 24))),
    )(v, after)


def _pair_sum(place, g, rb, name):
    _, _, Rh, C = g.shape
    tr = _row_tile(Rh, 1024, 16)

    def body(place_ref, g_ref, r_ref, q_ref):
        q_ref[...] = (g_ref[...] + r_ref[...]).astype(BF16)

    blk = 2 * _nbytes((tr, C), F32) + _nbytes((tr, C), BF16)
    return _pallas(
        body, name=name,
        grid_spec=pltpu.PrefetchScalarGridSpec(
            num_scalar_prefetch=1, grid=(N_CHIPS - 1, Rh // tr),
            in_specs=[pl.BlockSpec((None, None, tr, C), lambda j, i, p: (p[0] ^ (j + 1), p[1], i, 0)),
                      pl.BlockSpec((None, tr, C), lambda j, i, p: (p[0] ^ (j + 1), i, 0))],
            out_specs=pl.BlockSpec((None, tr, C), lambda j, i, p: (p[0] ^ (j + 1), i, 0))),
        out_shape=jax.ShapeDtypeStruct((N_CHIPS, Rh, C), BF16),
        compiler_params=_params(("parallel", "parallel"), blk),
    )(place, g, rb)


def _chip_sum(place, g, rb, rc, name):
    _, _, Rh, C = g.shape
    tr = _row_tile(Rh, 512, 16)

    def body(place_ref, g_ref, r_ref, rc_ref, o_ref):
        total = g_ref[...] + r_ref[...]
        for r in range(3):
            total = total + rc_ref[r].astype(F32)
        o_ref[...] = total

    blk = 3 * _nbytes((tr, C), F32) + 3 * _nbytes((tr, C), BF16)
    return _pallas(
        body, name=name,
        grid_spec=pltpu.PrefetchScalarGridSpec(
            num_scalar_prefetch=1, grid=(Rh // tr,),
            in_specs=[pl.BlockSpec((None, None, tr, C), lambda i, p: (p[0], p[1], i, 0)),
                      pl.BlockSpec((None, tr, C), lambda i, p: (p[0], i, 0)),
                      pl.BlockSpec((3, tr, C), lambda i, p: (0, i, 0))],
            out_specs=pl.BlockSpec((None, tr, C), lambda i, p: (p[1], i, 0))),
        out_shape=jax.ShapeDtypeStruct((2, Rh, C), F32),
        compiler_params=_params(("parallel",), blk),
    )(place, g, rb, rc)


def _adamw_math(w, g, m, v):
    m = ADAM_B1 * m + (1.0 - ADAM_B1) * g
    v = ADAM_B2 * v + (1.0 - ADAM_B2) * jnp.square(g)
    m_hat = m / (1.0 - ADAM_B1 ** ADAM_STEP)
    v_hat = v / (1.0 - ADAM_B2 ** ADAM_STEP)
    delta = -ADAM_LR * (m_hat / (jnp.sqrt(v_hat) + ADAM_EPS) + ADAM_WD * w)
    return delta, m, v


def _adamw(w, g, m, v, name):
    R, C = w.shape
    tr = _row_tile(R, 512)

    def body(w_ref, g_ref, m_ref, v_ref, go_ref, d_ref, nm_ref, nv_ref):
        gv = g_ref[...]
        d, nm, nv = _adamw_math(w_ref[...], gv, m_ref[...], v_ref[...])
        go_ref[...] = gv
        d_ref[...] = d
        nm_ref[...] = nm
        nv_ref[...] = nv

    spec = pl.BlockSpec((tr, C), lambda i: (i, 0))
    shp = jax.ShapeDtypeStruct((R, C), F32)
    return _pallas(
        body, name=name, grid=(R // tr,), in_specs=[spec] * 4, out_specs=[spec] * 4, out_shape=[shp] * 4,
        compiler_params=_params(("parallel",), 8 * _nbytes((tr, C), F32)),
    )(w, g, m, v)


def _adamw_small(ws, gs, ms, vs):
    n = len(ws)

    def body(*refs):
        for k in range(n):
            w_ref, g_ref, m_ref, v_ref = (refs[q * n + k] for q in range(4))
            d, nm, nv = _adamw_math(w_ref[...], g_ref[...], m_ref[...], v_ref[...])
            refs[4 * n + k][...] = d
            refs[5 * n + k][...] = nm
            refs[6 * n + k][...] = nv

    vm = pl.BlockSpec(memory_space=pltpu.VMEM)
    shapes = [jax.ShapeDtypeStruct(w.shape, F32) for w in ws]
    outs = _pallas(
        body, name="adamw_small", in_specs=[vm] * (4 * n), out_specs=[vm] * (3 * n), out_shape=shapes * 3,
    )(*ws, *gs, *ms, *vs)
    return outs[:n], outs[n:2 * n], outs[2 * n:]


def _pad_rows(a, rows):
    return jnp.pad(a, ((0, rows - a.shape[0]), (0, 0)))


def kernel(x, meta_tokens, ffn1_norm, ffn1_w_gate, ffn1_w_up, ffn1_w_down, mix_norm, w_in, b_in, conv_sc_w, conv_cf_w, conv_cf_b, ln_cf_g, ln_cf_b, w_out, ffn2_norm, ffn2_w_gate, ffn2_w_up, ffn2_w_down, final_norm, loss_target, m_meta_tokens, m_ffn1_norm, m_ffn1_w_gate, m_ffn1_w_up, m_ffn1_w_down, m_mix_norm, m_w_in, m_b_in, m_conv_sc_w, m_conv_cf_w, m_conv_cf_b, m_ln_cf_g, m_ln_cf_b, m_w_out, m_ffn2_norm, m_ffn2_w_gate, m_ffn2_w_up, m_ffn2_w_down, m_final_norm, v_meta_tokens, v_ffn1_norm, v_ffn1_w_gate, v_ffn1_w_up, v_ffn1_w_down, v_mix_norm, v_w_in, v_b_in, v_conv_sc_w, v_conv_cf_w, v_conv_cf_b, v_ln_cf_g, v_ln_cf_b, v_w_out, v_ffn2_norm, v_ffn2_w_gate, v_ffn2_w_up, v_ffn2_w_down, v_final_norm):
    xi, yi, ci = lax.axis_index("x"), lax.axis_index("y"), lax.axis_index("c")
    chip = 2 * xi + yi
    place = jnp.stack([chip, ci]).astype(jnp.int32)

    x2 = x[0]
    tgt = loss_target[0]
    S, D = x2.shape
    C1 = D // 2
    cs = conv_sc_w.shape[2]
    ksc, kcf = conv_sc_w.shape[1], conv_cf_w.shape[1]
    ms = meta_tokens.shape[1]

    big = {"ffn1_w_gate": ffn1_w_gate, "ffn1_w_up": ffn1_w_up, "ffn1_w_down": ffn1_w_down, "w_in": w_in, "w_out": w_out,
           "ffn2_w_gate": ffn2_w_gate, "ffn2_w_up": ffn2_w_up, "ffn2_w_down": ffn2_w_down}
    big_m = {"ffn1_w_gate": m_ffn1_w_gate, "ffn1_w_up": m_ffn1_w_up, "ffn1_w_down": m_ffn1_w_down, "w_in": m_w_in,
             "w_out": m_w_out, "ffn2_w_gate": m_ffn2_w_gate, "ffn2_w_up": m_ffn2_w_up, "ffn2_w_down": m_ffn2_w_down}
    big_v = {"ffn1_w_gate": v_ffn1_w_gate, "ffn1_w_up": v_ffn1_w_up, "ffn1_w_down": v_ffn1_w_down, "w_in": v_w_in,
             "w_out": v_w_out, "ffn2_w_gate": v_ffn2_w_gate, "ffn2_w_up": v_ffn2_w_up, "ffn2_w_down": v_ffn2_w_down}
    buf = {nm: _cast_own_block(place, w[0], "cast_" + nm) for nm, w in big.items()}
    whole_weight = lambda g: g.reshape(N_CHIPS, 2 * g.shape[2], g.shape[3])
    corner = lambda a: a.reshape(-1, a.shape[-1])[:8, :128]

    NEAR, FAR = (0, 1), (2,)
    groups = {"ffn1_near": (["ffn1_w_gate", "ffn1_w_up", "ffn1_w_down"], NEAR),
              "ffn1_far": (["ffn1_w_gate", "ffn1_w_up", "ffn1_w_down"], FAR),
              "mix": (["w_in", "w_out"], NEAR + FAR),
              "ffn2_up": (["ffn2_w_gate", "ffn2_w_up"], NEAR + FAR),
              "ffn2_down": (["ffn2_w_down"], NEAR + FAR)}
    started = {}

    def start(tag, after):
        nms, rels = groups[tag]
        copies = functools.partial(_gather_copies, rels=rels)
        send, recv, thru, token = _start_copies([buf[nm] for nm in nms], after, len(rels) * len(nms), copies,
                                                "gather_start_" + tag)
        for nm, b in zip(nms, thru):
            buf[nm] = b
        started[tag] = (send, recv, copies)
        return token

    def arrive(tag, after, then=None):
        nms, rels = groups[tag]
        send, recv, copies = started[tag]
        got = _wait_copies(send, recv, [buf[nm] for nm in nms], corner(after), copies, "gather_wait_" + tag)
        for nm, b in zip(nms, got):
            buf[nm] = b
        if then is not None:
            start(then, corner(got[0]))
        for nm, b in zip(nms, _forward_halves([buf[nm] for nm in nms], "gather_forward_" + tag, rels)):
            buf[nm] = b

    def passing(tag, after):
        nms, rels = groups[tag]
        send, recv, copies = started[tag]
        got = _wait_copies(send, recv, [buf[nm] for nm in nms], corner(after), copies, "gather_wait_" + tag)
        copies = functools.partial(_forward_copies, rels=rels)
        send, recv, thru, token = _start_copies(got, corner(got[0]), len(rels) * len(nms), copies,
                                                "gather_pass_" + tag)
        for nm, b in zip(nms, thru):
            buf[nm] = b
        started[tag] = (send, recv, copies)
        return token

    def passed(tag, after):
        nms, _ = groups[tag]
        send, recv, copies = started[tag]
        for nm, b in zip(nms, _wait_copies(send, recv, [buf[nm] for nm in nms], corner(after), copies,
                                           "gather_passed_" + tag)):
            buf[nm] = b

    tokens = lambda *arrays: jnp.concatenate([corner(a).astype(F32) for a in arrays], axis=0)
    assert ksc <= 8 and kcf <= 32 and cs <= ms
    pack = jnp.concatenate([
        meta_tokens,
        jnp.pad(conv_sc_w[0], ((0, 8 - ksc), (0, ms - cs))),
        jnp.pad(conv_cf_w[0], ((0, 32 - kcf), (0, ms - cs)))], axis=0)
    everyone = _share_small(pack, False, "share_params", pack)[0::2]
    meta_full = jnp.transpose(everyone[:, :N_META, :], (1, 0, 2)).reshape(N_META, D)
    wsc_full = jnp.transpose(everyone[:, N_META:N_META + ksc, :cs], (1, 0, 2)).reshape(ksc, C1)
    wcf_full = jnp.transpose(everyone[:, N_META + 8:N_META + 8 + kcf, :cs], (1, 0, 2)).reshape(kcf, C1)

    token = start("ffn1_near", corner(everyone))

    ffn1 = lambda: [whole_weight(buf[nm]) for nm in ["ffn1_w_gate", "ffn1_w_up", "ffn1_w_down"]]
    own = chip[None].astype(jnp.int32)
    near = jnp.stack([chip ^ 2, chip ^ 1]).astype(jnp.int32)
    far = (chip ^ 3)[None].astype(jnp.int32)
    all_chips = jnp.arange(N_CHIPS, dtype=jnp.int32)

    hs0, n1 = _embed_rms(x2, meta_full, ffn1_norm)
    wg1, wu1, wd1 = ffn1()
    gua = _ffn_up(n1, wg1, wu1, own, None, token, "ffn1_up_own")
    hs1 = _ffn_down(gua[2], wd1, hs0, own, "ffn1_down_own")
    later = [buf[nm] for nm in ["w_in", "w_out", "ffn2_w_gate", "ffn2_w_up", "ffn2_w_down"]]
    arrive("ffn1_near", tokens(hs1, *later), "ffn1_far")
    wg1, wu1, wd1 = ffn1()
    gua = _ffn_up(n1, wg1, wu1, near, gua, token, "ffn1_up_near")
    tok = start("mix", corner(gua[2]))
    tok = passing("ffn1_far", tok)
    hs1 = _ffn_down(gua[2], ffn1()[2], hs1, near, "ffn1_down_near", tok)
    passed("ffn1_far", hs1)
    wg1, wu1, wd1 = ffn1()
    g1, u1, a1 = _ffn_up(n1, wg1, wu1, far, gua, token, "ffn1_up_far")
    tok = passing("mix", a1)
    hs1, n2 = _ffn_down(a1, wd1, hs1, far, "ffn1_down_far", tok, mix_norm)
    F = N_CHIPS * wd1.shape[1]
    tok = start("ffn2_up", corner(hs1))
    passed("mix", tok)
    win, wout = whole_weight(buf["w_in"]), whole_weight(buf["w_out"])
    u = _mix_in(n2, win, b_in)
    y, z1 = _mix_conv_fwd(u, wsc_full, wcf_full, conv_cf_b, ln_cf_g, ln_cf_b)
    tok = start("ffn2_down", corner(y))
    tok = passing("ffn2_up", tok)
    hs2, n3 = _mix_out(y, wout.reshape(D, D), hs1, ffn2_norm, tok)
    passed("ffn2_up", hs2)
    wg2, wu2 = whole_weight(buf["ffn2_w_gate"]), whole_weight(buf["ffn2_w_up"])
    g2, u2, a2 = _ffn_up(n3, wg2, wu2, all_chips, None, token, "ffn2_up")
    passed("ffn2_down", passing("ffn2_down", a2))
    wd2 = whole_weight(buf["ffn2_w_down"])
    hs3 = _ffn_down_whole(a2, wd2.reshape(F, D), hs2, "ffn2_down")
    token_ffn2 = token

    def pair_start(group, after, tag):
        gs = [g for _, g in group]
        lands = [lax.empty((N_CHIPS,) + g.shape[2:], F32) for g in gs]
        send, recv, thru, token = _start_copies(gs + lands, after, N_CHIPS * len(gs), _pair_copies,
                                                "pair_start_" + tag)
        return (group, send, recv, thru, tag), token

    def scatter_start(state, after):
        group, send, recv, thru, tag = state
        thru = _wait_copies(send, recv, thru, corner(after), _pair_copies, "pair_wait_" + tag)
        gs, sib = thru[:len(group)], thru[len(group):]
        sums = [_pair_sum(place, g, rb, "pair_sum_" + nm) for (nm, _), g, rb in zip(group, gs, sib)]
        lands = [lax.empty((3,) + q.shape[1:], BF16) for q in sums]
        send, recv, thru, token = _start_copies(sums + lands, corner(sums[-1]), 3 * len(gs), _scatter_copies,
                                                "scatter_start_" + tag)
        return ([(nm, g) for (nm, _), g in zip(group, gs)], sib, send, recv, thru, tag), token

    def finish_sum(state, after):
        group, sib, send, recv, thru, tag = state
        lands = _wait_copies(send, recv, thru, corner(after), _scatter_copies, "scatter_wait_" + tag)[len(group):]
        mine = [_chip_sum(place, g, rb, rc, "chip_sum_" + nm) for (nm, g), rb, rc in zip(group, sib, lands)]
        send, recv, thru, token = _start_copies(mine, corner(mine[-1]), len(mine), _half_copies, "half_start_" + tag)
        return (group, send, recv, thru, tag), token

    def finish_adam(state, after):
        group, send, recv, thru, tag = state
        whole = _wait_copies(send, recv, thru, corner(after), _half_copies, "half_wait_" + tag)
        out = {}
        for (nm, _), g in zip(group, whole):
            w = big[nm]
            g_out, d, new_m, new_v = _adamw(w[0], g.reshape(w.shape[1:]), big_m[nm][0], big_v[nm][0], "adamw_" + nm)
            out[nm] = (g_out[None], d[None], new_m[None], new_v[None])
        return out

    dhs3, df2, loss_row, d_final = _final_loss(hs3, final_norm.reshape(1, D), tgt)

    dg2, du2 = _ffn_bwd_act(df2, wd2.reshape(F, D), g2, u2, token_ffn2, "ffn2_bwd_act")
    gw_d2 = _wgrad_down(a2, df2, "wgrad_ffn2_down")
    gw_g2 = _wgrad_cols(n3, [dg2], "wgrad_ffn2_gate")[0]
    gw_u2 = _wgrad_cols(n3, [du2], "wgrad_ffn2_up")[0]
    pair_ffn2, token = pair_start([("ffn2_w_gate", gw_g2), ("ffn2_w_up", gw_u2), ("ffn2_w_down", gw_d2)],
                                  corner(gw_u2), "ffn2")
    dn3 = _nt_panel([dg2, du2], [wg2, wu2], token, "ffn2_bwd_in")
    red_ffn2, token = scatter_start(pair_ffn2, dn3)
    dhs2, dm, d_ffn2 = _rms_bwd(dn3, hs2, ffn2_norm, dhs3, 1.0, "rms_bwd_ffn2")

    dy = _nt_panel([dm], [wout.reshape(1, D, D)], token, "mix_bwd_out")
    gw_out = _wgrad_out(y, dm)
    dz1, dcs, db, d_lg, d_lb, d_bcf = _mix_conv_bwd1(u, z1, dy, wsc_full, ln_cf_g, ln_cf_b)
    du, d_bin, d_wsc, d_wcf = _mix_conv_bwd2(u, dz1, dcs, db, wsc_full, wcf_full)
    gw_in = _wgrad_cols(n2, [du], "wgrad_w_in")[0]
    pair_mix, token = pair_start([("w_in", gw_in), ("w_out", gw_out)], corner(gw_in), "mix")
    dn2 = _nt_panel([du], [win], token, "mix_bwd_in")
    red_mix, token = scatter_start(pair_mix, dn2)
    dhs1, df1, d_mix = _rms_bwd(dn2, hs1, mix_norm, dhs2, FFN_RES_SCALE, "rms_bwd_mix")

    dg1, du1 = _ffn_bwd_act(df1, wd1.reshape(F, D), g1, u1, token, "ffn1_bwd_act")
    gw_d1 = _wgrad_down(a1, df1, "wgrad_ffn1_down")
    gw_g1 = _wgrad_cols(n1, [dg1], "wgrad_ffn1_gate")[0]
    pair_ffn1a, token = pair_start([("ffn1_w_down", gw_d1), ("ffn1_w_gate", gw_g1)], corner(gw_g1), "ffn1a")
    gw_u1 = _wgrad_cols(n1, [du1], "wgrad_ffn1_up", token)[0]
    red_ffn1a, token = scatter_start(pair_ffn1a, gw_u1)
    pair_ffn1b, token = pair_start([("ffn1_w_up", gw_u1)], token, "ffn1b")
    dn1 = _nt_panel([dg1, du1], [wg1, wu1], token, "ffn1_bwd_in")
    red_ffn1b, token = scatter_start(pair_ffn1b, dn1)
    grad_x, d_meta, d_ffn1 = _rms_bwd_first(dn1, hs0, ffn1_norm, dhs1, token)

    half_ffn2, tok = finish_sum(red_ffn2, grad_x)
    half_mix, tok = finish_sum(red_mix, tok)
    big_out = finish_adam(half_ffn2, tok)
    half_ffn1a, tok = finish_sum(red_ffn1a, big_out["ffn2_w_down"][1])
    big_out.update(finish_adam(half_mix, tok))
    half_ffn1b, tok = finish_sum(red_ffn1b, big_out["w_out"][1])
    big_out.update(finish_adam(half_ffn1a, tok))
    big_out.update(finish_adam(half_ffn1b, big_out["ffn1_w_gate"][1]))

    W = C1
    rows = lambda a: a.reshape(-1, W)
    parts = [rows(d_ffn1), rows(d_mix), rows(d_ffn2), rows(d_final), rows(d_bin), d_bcf, d_lg, d_lb,
             d_wsc, d_wcf, rows(d_meta), jnp.broadcast_to(loss_row[:, :1], (1, W))]
    sizes = [p.shape[0] for p in parts]
    total_rows = sum(sizes)
    packed = _pad_rows(jnp.concatenate(parts, axis=0), -(-total_rows // 8) * 8)
    summed = _share_small(packed, True, "sum_small", big_out["ffn1_w_up"][1])
    offs = [0]
    for n in sizes:
        offs.append(offs[-1] + n)
    piece = lambda k: summed[offs[k]:offs[k + 1]]
    loss = piece(11)[0, 0]
    g_ffn1, g_mix, g_ffn2 = (piece(k).reshape(1, D) for k in range(3))
    g_final = piece(3).reshape(1, D)
    g_bin = piece(4).reshape(1, -1)
    g_bcf, g_lg, g_lb = piece(5), piece(6), piece(7)
    g_wsc = lax.dynamic_slice_in_dim(piece(8), chip * cs, cs, axis=1)
    g_wcf = lax.dynamic_slice_in_dim(piece(9), chip * cs, cs, axis=1)
    g_meta = lax.dynamic_slice_in_dim(piece(10).reshape(N_META, D), chip * ms, ms, axis=1)

    small_names = ["meta_tokens", "ffn1_norm", "mix_norm", "b_in", "conv_sc_w", "conv_cf_w", "conv_cf_b", "ln_cf_g",
                   "ln_cf_b", "ffn2_norm", "final_norm"]
    small_w = [meta_tokens, ffn1_norm, mix_norm, b_in, conv_sc_w[0], conv_cf_w[0], conv_cf_b, ln_cf_g, ln_cf_b,
               ffn2_norm, final_norm.reshape(1, D)]
    small_g = [g_meta, g_ffn1, g_mix, g_bin, g_wsc, g_wcf, g_bcf, g_lg, g_lb, g_ffn2, g_final]
    small_m = [m_meta_tokens, m_ffn1_norm, m_mix_norm, m_b_in, m_conv_sc_w[0], m_conv_cf_w[0], m_conv_cf_b, m_ln_cf_g,
               m_ln_cf_b, m_ffn2_norm, m_final_norm.reshape(1, D)]
    small_v = [v_meta_tokens, v_ffn1_norm, v_mix_norm, v_b_in, v_conv_sc_w[0], v_conv_cf_w[0], v_conv_cf_b, v_ln_cf_g,
               v_ln_cf_b, v_ffn2_norm, v_final_norm.reshape(1, D)]
    s_d, s_m, s_v = _adamw_small(small_w, small_g, small_m, small_v)
    shapes = {"conv_sc_w": conv_sc_w.shape, "conv_cf_w": conv_cf_w.shape, "final_norm": final_norm.shape}
    small_out = {}
    for nm, g, d, m, v in zip(small_names, small_g, s_d, s_m, s_v):
        shp = shapes.get(nm, g.shape)
        small_out[nm] = tuple(t.reshape(shp) for t in (g, d, m, v))

    order = ["meta_tokens", "ffn1_norm", "ffn1_w_gate", "ffn1_w_up", "ffn1_w_down", "mix_norm", "w_in", "b_in",
             "conv_sc_w", "conv_cf_w", "conv_cf_b", "ln_cf_g", "ln_cf_b", "w_out", "ffn2_norm", "ffn2_w_gate",
             "ffn2_w_up", "ffn2_w_down", "final_norm"]
    res = {**big_out, **small_out}
    outs = [loss, grad_x[None]]
    for q in range(4):
        outs.extend(res[nm][q] for nm in order)
    return tuple(outs)
```
